```python
import math
import jax, jax.numpy as jnp
from jax import lax
import numpy as np

D_MODEL = 1024
BATCH = 8
SEQ = 8192
DEPTH = 1

N_HEADS = 8
HEAD_DIM = 128
ATTN_WIDTH = N_HEADS * HEAD_DIM
CONV_WIDTH = D_MODEL
CONV_K = 3
D_FF = 2816
PLE_DIM = 256
Q_BLOCK = 128
NORM_EPS = 1e-6
MIX_COLS = 3 * CONV_WIDTH + 3 * ATTN_WIDTH + 2 * D_MODEL

kernel_name = "hybrid_shortconv_stickbreaking_macaron_block"


def rms_norm(x, g):
    xf = x.astype(jnp.float32)
    y = xf * lax.rsqrt(jnp.mean(xf * xf, axis=-1, keepdims=True) + NORM_EPS)
    return (y * g.astype(jnp.float32)).astype(x.dtype)


def swiglu(x, w_in, w_out):
    gate, up = jnp.split(x @ w_in, 2, axis=-1)
    return (jax.nn.silu(gate) * up) @ w_out


def causal_depthwise_conv(x, w):
    return lax.conv_general_dilated(
        x, w[:, None, :].astype(x.dtype), window_strides=(1,),
        padding=[(CONV_K - 1, 0)], dimension_numbers=('NWC', 'WIO', 'NWC'),
        feature_group_count=x.shape[-1])


def stick_breaking_attention(q, k, v):
    b, h, s, d = q.shape
    nblk = s // Q_BLOCK
    scale = 1.0 / math.sqrt(d)
    k_pos = jnp.arange(s)
    vf = v.astype(jnp.float32)
    qb = q.reshape(b, h, nblk, Q_BLOCK, d).transpose(2, 0, 1, 3, 4)

    def block(args):
        q_blk, blk_idx = args
        q_pos = blk_idx * Q_BLOCK + jnp.arange(Q_BLOCK)
        z = jnp.einsum('bhqd,bhkd->bhqk', q_blk, k,
                       preferred_element_type=jnp.float32) * scale
        mask = k_pos[None, :] < q_pos[:, None]
        log_1m_beta = jnp.where(mask, jax.nn.log_sigmoid(-z), 0.0)
        tail = lax.cumsum(log_1m_beta, axis=3, reverse=True) - log_1m_beta
        a = jnp.where(mask, jnp.exp(jax.nn.log_sigmoid(z) + tail), 0.0)
        return jnp.einsum('bhqk,bhkd->bhqd', a, vf).astype(q.dtype)

    out = lax.map(block, (qb, jnp.arange(nblk)))
    return out.transpose(1, 2, 0, 3, 4).reshape(b, h, s, d)


def mix_split_points():
    widths = [CONV_WIDTH, CONV_WIDTH, CONV_WIDTH, ATTN_WIDTH, ATTN_WIDTH, ATTN_WIDTH, D_MODEL, D_MODEL]
    pts, acc = [], 0
    for w in widths[:-1]:
        acc += w
        pts.append(acc)
    return pts


def _fwd_setup_inputs(seed: int = 0) -> dict:
    key = jax.random.key(seed)
    ks = jax.random.split(key, 20)

    def w(k, shape, fan_in):
        return jax.random.normal(k, shape, jnp.float32) * (fan_in ** -0.5)

    def gain(k, shape):
        return 1.0 + 0.01 * jax.random.normal(k, shape, jnp.float32)

    return {
        "x": jax.random.normal(ks[0], (BATCH, SEQ, D_MODEL), jnp.float32),
        "p": jax.random.normal(ks[1], (DEPTH, BATCH, SEQ, PLE_DIM), jnp.float32),
        "ffn1_norm": gain(ks[2], (DEPTH, D_MODEL)),
        "ffn1_w_in": w(ks[3], (DEPTH, D_MODEL, 2 * D_FF), D_MODEL),
        "ffn1_w_out": w(ks[4], (DEPTH, D_FF, D_MODEL), D_FF),
        "mix_norm": gain(ks[5], (DEPTH, D_MODEL)),
        "w_mix_in": w(ks[6], (DEPTH, D_MODEL, MIX_COLS), D_MODEL),
        "conv_w": w(ks[7], (DEPTH, CONV_K, CONV_WIDTH), CONV_K),
        "w_conv_out": w(ks[8], (DEPTH, CONV_WIDTH, D_MODEL), CONV_WIDTH),
        "w_attn_out": w(ks[9], (DEPTH, ATTN_WIDTH, D_MODEL), ATTN_WIDTH),
        "w_mix_out": w(ks[10], (DEPTH, D_MODEL, D_MODEL), D_MODEL),
        "ffn2_norm": gain(ks[11], (DEPTH, D_MODEL)),
        "ffn2_w_in": w(ks[12], (DEPTH, D_MODEL, 2 * D_FF), D_MODEL),
        "ffn2_w_out": w(ks[13], (DEPTH, D_FF, D_MODEL), D_FF),
        "ple_norm": gain(ks[14], (DEPTH, D_MODEL)),
        "w_ple_gate": w(ks[15], (DEPTH, D_MODEL, D_MODEL), D_MODEL),
        "w_ple_proj": w(ks[16], (DEPTH, PLE_DIM, D_MODEL), PLE_DIM),
        "final_norm": gain(ks[17], (D_MODEL,)),
    }


def _fwd_reference(x, p, ffn1_norm, ffn1_w_in, ffn1_w_out, mix_norm, w_mix_in, conv_w,
              w_conv_out, w_attn_out, w_mix_out, ffn2_norm, ffn2_w_in, ffn2_w_out,
              ple_norm, w_ple_gate, w_ple_proj, final_norm):
    b, s, _ = x.shape
    splits = mix_split_points()
    h = x
    for i in range(DEPTH):
        h = h + 0.5 * swiglu(rms_norm(h, ffn1_norm[i]), ffn1_w_in[i], ffn1_w_out[i])

        u = rms_norm(h, mix_norm[i])
        c_b, c_c, c_x, q, k, v, g_conv, g_attn = jnp.split(u @ w_mix_in[i], splits, axis=-1)

        y_conv = (c_b * causal_depthwise_conv(c_c * c_x, conv_w[i])) @ w_conv_out[i]

        def heads(t):
            return t.reshape(b, s, N_HEADS, HEAD_DIM).transpose(0, 2, 1, 3)
        o = stick_breaking_attention(heads(q), heads(k), heads(v))
        y_attn = o.transpose(0, 2, 1, 3).reshape(b, s, ATTN_WIDTH) @ w_attn_out[i]

        merged = jax.nn.sigmoid(g_conv) * y_conv + jax.nn.sigmoid(g_attn) * y_attn
        h = h + merged @ w_mix_out[i]

        h = h + 0.5 * swiglu(rms_norm(h, ffn2_norm[i]), ffn2_w_in[i], ffn2_w_out[i])

        ple_gate = jax.nn.sigmoid(rms_norm(h, ple_norm[i]) @ w_ple_gate[i])
        h = h + ple_gate * (p[i] @ w_ple_proj[i])

    return rms_norm(h, final_norm)


import jax as _jax
import jax.numpy as _jnp

TWIN_FORMAT = 'train_step'
FWD_PARAMS = ['x', 'p', 'ffn1_norm', 'ffn1_w_in', 'ffn1_w_out', 'mix_norm', 'w_mix_in', 'conv_w', 'w_conv_out', 'w_attn_out', 'w_mix_out', 'ffn2_norm', 'ffn2_w_in', 'ffn2_w_out', 'ple_norm', 'w_ple_gate', 'w_ple_proj', 'final_norm']
TWIN_WEIGHTS = ['ffn1_norm', 'ffn1_w_in', 'ffn1_w_out', 'mix_norm', 'w_mix_in', 'conv_w', 'w_conv_out', 'w_attn_out', 'w_mix_out', 'ffn2_norm', 'ffn2_w_in', 'ffn2_w_out', 'ple_norm', 'w_ple_gate', 'w_ple_proj', 'final_norm']
TWIN_DIFF_INPUT = 'x'
TWIN_INPUTS = ['x', 'p', 'ffn1_norm', 'ffn1_w_in', 'ffn1_w_out', 'mix_norm', 'w_mix_in', 'conv_w', 'w_conv_out', 'w_attn_out', 'w_mix_out', 'ffn2_norm', 'ffn2_w_in', 'ffn2_w_out', 'ple_norm', 'w_ple_gate', 'w_ple_proj', 'final_norm', 'loss_target', 'm_ffn1_norm', 'm_ffn1_w_in', 'm_ffn1_w_out', 'm_mix_norm', 'm_w_mix_in', 'm_conv_w', 'm_w_conv_out', 'm_w_attn_out', 'm_w_mix_out', 'm_ffn2_norm', 'm_ffn2_w_in', 'm_ffn2_w_out', 'm_ple_norm', 'm_w_ple_gate', 'm_w_ple_proj', 'm_final_norm', 'v_ffn1_norm', 'v_ffn1_w_in', 'v_ffn1_w_out', 'v_mix_norm', 'v_w_mix_in', 'v_conv_w', 'v_w_conv_out', 'v_w_attn_out', 'v_w_mix_out', 'v_ffn2_norm', 'v_ffn2_w_in', 'v_ffn2_w_out', 'v_ple_norm', 'v_w_ple_gate', 'v_w_ple_proj', 'v_final_norm']
TWIN_OUTPUTS = ['loss', 'grad_x', 'grad_ffn1_norm', 'grad_ffn1_w_in', 'grad_ffn1_w_out', 'grad_mix_norm', 'grad_w_mix_in', 'grad_conv_w', 'grad_w_conv_out', 'grad_w_attn_out', 'grad_w_mix_out', 'grad_ffn2_norm', 'grad_ffn2_w_in', 'grad_ffn2_w_out', 'grad_ple_norm', 'grad_w_ple_gate', 'grad_w_ple_proj', 'grad_final_norm', 'delta_ffn1_norm', 'delta_ffn1_w_in', 'delta_ffn1_w_out', 'delta_mix_norm', 'delta_w_mix_in', 'delta_conv_w', 'delta_w_conv_out', 'delta_w_attn_out', 'delta_w_mix_out', 'delta_ffn2_norm', 'delta_ffn2_w_in', 'delta_ffn2_w_out', 'delta_ple_norm', 'delta_w_ple_gate', 'delta_w_ple_proj', 'delta_final_norm', 'new_m_ffn1_norm', 'new_m_ffn1_w_in', 'new_m_ffn1_w_out', 'new_m_mix_norm', 'new_m_w_mix_in', 'new_m_conv_w', 'new_m_w_conv_out', 'new_m_w_attn_out', 'new_m_w_mix_out', 'new_m_ffn2_norm', 'new_m_ffn2_w_in', 'new_m_ffn2_w_out', 'new_m_ple_norm', 'new_m_w_ple_gate', 'new_m_w_ple_proj', 'new_m_final_norm', 'new_v_ffn1_norm', 'new_v_ffn1_w_in', 'new_v_ffn1_w_out', 'new_v_mix_norm', 'new_v_w_mix_in', 'new_v_conv_w', 'new_v_w_conv_out', 'new_v_w_attn_out', 'new_v_w_mix_out', 'new_v_ffn2_norm', 'new_v_ffn2_w_in', 'new_v_ffn2_w_out', 'new_v_ple_norm', 'new_v_w_ple_gate', 'new_v_w_ple_proj', 'new_v_final_norm']
TWIN_LEAF_KINDS = {'loss': 'loss', 'grad_x': 'grad_x', 'grad_ffn1_norm': 'grad_w', 'grad_ffn1_w_in': 'grad_w', 'grad_ffn1_w_out': 'grad_w', 'grad_mix_norm': 'grad_w', 'grad_w_mix_in': 'grad_w', 'grad_conv_w': 'grad_w', 'grad_w_conv_out': 'grad_w', 'grad_w_attn_out': 'grad_w', 'grad_w_mix_out': 'grad_w', 'grad_ffn2_norm': 'grad_w', 'grad_ffn2_w_in': 'grad_w', 'grad_ffn2_w_out': 'grad_w', 'grad_ple_norm': 'grad_w', 'grad_w_ple_gate': 'grad_w', 'grad_w_ple_proj': 'grad_w', 'grad_final_norm': 'grad_w', 'delta_ffn1_norm': 'delta_w', 'delta_ffn1_w_in': 'delta_w', 'delta_ffn1_w_out': 'delta_w', 'delta_mix_norm': 'delta_w', 'delta_w_mix_in': 'delta_w', 'delta_conv_w': 'delta_w', 'delta_w_conv_out': 'delta_w', 'delta_w_attn_out': 'delta_w', 'delta_w_mix_out': 'delta_w', 'delta_ffn2_norm': 'delta_w', 'delta_ffn2_w_in': 'delta_w', 'delta_ffn2_w_out': 'delta_w', 'delta_ple_norm': 'delta_w', 'delta_w_ple_gate': 'delta_w', 'delta_w_ple_proj': 'delta_w', 'delta_final_norm': 'delta_w', 'new_m_ffn1_norm': 'new_m', 'new_m_ffn1_w_in': 'new_m', 'new_m_ffn1_w_out': 'new_m', 'new_m_mix_norm': 'new_m', 'new_m_w_mix_in': 'new_m', 'new_m_conv_w': 'new_m', 'new_m_w_conv_out': 'new_m', 'new_m_w_attn_out': 'new_m', 'new_m_w_mix_out': 'new_m', 'new_m_ffn2_norm': 'new_m', 'new_m_ffn2_w_in': 'new_m', 'new_m_ffn2_w_out': 'new_m', 'new_m_ple_norm': 'new_m', 'new_m_w_ple_gate': 'new_m', 'new_m_w_ple_proj': 'new_m', 'new_m_final_norm': 'new_m', 'new_v_ffn1_norm': 'new_v', 'new_v_ffn1_w_in': 'new_v', 'new_v_ffn1_w_out': 'new_v', 'new_v_mix_norm': 'new_v', 'new_v_w_mix_in': 'new_v', 'new_v_conv_w': 'new_v', 'new_v_w_conv_out': 'new_v', 'new_v_w_attn_out': 'new_v', 'new_v_w_mix_out': 'new_v', 'new_v_ffn2_norm': 'new_v', 'new_v_ffn2_w_in': 'new_v', 'new_v_ffn2_w_out': 'new_v', 'new_v_ple_norm': 'new_v', 'new_v_w_ple_gate': 'new_v', 'new_v_w_ple_proj': 'new_v', 'new_v_final_norm': 'new_v'}


def _forward(args):
    return _fwd_reference(*[args[k] for k in FWD_PARAMS])


def _output_shape():
    def fwd():
        inp = _fwd_setup_inputs(0)
        return _fwd_reference(*[inp[k] for k in FWD_PARAMS])
    out = _jax.eval_shape(fwd)
    return out.shape, out.dtype

N_MICROBATCH = 1
ADAM_LR = 0.001
ADAM_B1 = 0.9
ADAM_B2 = 0.999
ADAM_EPS = 1e-08
ADAM_WD = 0.01
ADAM_STEP = 10
PER_EXAMPLE_BATCH_AXIS = {'x': 0, 'p': 1, 'loss_target': 0}
SHARED_INPUTS = []
_WEIGHT_DTYPES = {'ffn1_norm': _jnp.float32, 'ffn1_w_in': _jnp.float32, 'ffn1_w_out': _jnp.float32, 'mix_norm': _jnp.float32, 'w_mix_in': _jnp.float32, 'conv_w': _jnp.float32, 'w_conv_out': _jnp.float32, 'w_attn_out': _jnp.float32, 'w_mix_out': _jnp.float32, 'ffn2_norm': _jnp.float32, 'ffn2_w_in': _jnp.float32, 'ffn2_w_out': _jnp.float32, 'ple_norm': _jnp.float32, 'w_ple_gate': _jnp.float32, 'w_ple_proj': _jnp.float32, 'final_norm': _jnp.float32}
MOMENT_SCALE = {'ffn1_norm': 1.248504e-01, 'ffn1_w_in': 5.099752e-02, 'ffn1_w_out': 8.327582e-02, 'mix_norm': 2.197013e-01, 'w_mix_in': 7.338005e-02, 'conv_w': 1.061013e-01, 'w_conv_out': 1.056747e-01, 'w_attn_out': 6.867152e-02, 'w_mix_out': 1.263085e-01, 'ffn2_norm': 8.080028e-02, 'ffn2_w_in': 3.381702e-02, 'ffn2_w_out': 5.524413e-02, 'ple_norm': 4.066788e-02, 'w_ple_gate': 3.873675e-02, 'w_ple_proj': 9.902903e-02, 'final_norm': 6.402261e+01}


def _to_microbatches(a, axis):
    t = _jnp.moveaxis(a, axis, 0)
    t = t.reshape((N_MICROBATCH, t.shape[0] // N_MICROBATCH) + t.shape[1:])
    return _jnp.moveaxis(t, 1, axis + 1)


def setup_inputs(seed: int = 0) -> dict:
    inp = _fwd_setup_inputs(seed)
    key = _jax.random.fold_in(_jax.random.key(seed), 7919)
    shape, _ = _output_shape()
    out = dict(inp)
    out["loss_target"] = _jax.random.normal(_jax.random.fold_in(key, 0), shape, _jnp.float32)
    for i, name in enumerate(TWIN_WEIGHTS):
        w = inp[name].astype(_jnp.float32)
        if MOMENT_SCALE is None:
            s = _jnp.sqrt(_jnp.mean(_jnp.square(w)) + 1e-30)
        else:
            s = MOMENT_SCALE[name]
        km, kv = _jax.random.split(_jax.random.fold_in(key, i + 1))
        out[name] = w
        out["m_" + name] = s * _jax.random.normal(km, w.shape, _jnp.float32)
        out["v_" + name] = (s * s) * _jax.random.uniform(kv, w.shape, _jnp.float32, 0.5, 1.5)
    if N_MICROBATCH > 1:
        for name, axis in PER_EXAMPLE_BATCH_AXIS.items():
            out[name] = _to_microbatches(out[name], axis)
    return {'x': out['x'], 'p': out['p'], 'ffn1_norm': out['ffn1_norm'], 'ffn1_w_in': out['ffn1_w_in'], 'ffn1_w_out': out['ffn1_w_out'], 'mix_norm': out['mix_norm'], 'w_mix_in': out['w_mix_in'], 'conv_w': out['conv_w'], 'w_conv_out': out['w_conv_out'], 'w_attn_out': out['w_attn_out'], 'w_mix_out': out['w_mix_out'], 'ffn2_norm': out['ffn2_norm'], 'ffn2_w_in': out['ffn2_w_in'], 'ffn2_w_out': out['ffn2_w_out'], 'ple_norm': out['ple_norm'], 'w_ple_gate': out['w_ple_gate'], 'w_ple_proj': out['w_ple_proj'], 'final_norm': out['final_norm'], 'loss_target': out['loss_target'], 'm_ffn1_norm': out['m_ffn1_norm'], 'm_ffn1_w_in': out['m_ffn1_w_in'], 'm_ffn1_w_out': out['m_ffn1_w_out'], 'm_mix_norm': out['m_mix_norm'], 'm_w_mix_in': out['m_w_mix_in'], 'm_conv_w': out['m_conv_w'], 'm_w_conv_out': out['m_w_conv_out'], 'm_w_attn_out': out['m_w_attn_out'], 'm_w_mix_out': out['m_w_mix_out'], 'm_ffn2_norm': out['m_ffn2_norm'], 'm_ffn2_w_in': out['m_ffn2_w_in'], 'm_ffn2_w_out': out['m_ffn2_w_out'], 'm_ple_norm': out['m_ple_norm'], 'm_w_ple_gate': out['m_w_ple_gate'], 'm_w_ple_proj': out['m_w_ple_proj'], 'm_final_norm': out['m_final_norm'], 'v_ffn1_norm': out['v_ffn1_norm'], 'v_ffn1_w_in': out['v_ffn1_w_in'], 'v_ffn1_w_out': out['v_ffn1_w_out'], 'v_mix_norm': out['v_mix_norm'], 'v_w_mix_in': out['v_w_mix_in'], 'v_conv_w': out['v_conv_w'], 'v_w_conv_out': out['v_w_conv_out'], 'v_w_attn_out': out['v_w_attn_out'], 'v_w_mix_out': out['v_w_mix_out'], 'v_ffn2_norm': out['v_ffn2_norm'], 'v_ffn2_w_in': out['v_ffn2_w_in'], 'v_ffn2_w_out': out['v_ffn2_w_out'], 'v_ple_norm': out['v_ple_norm'], 'v_w_ple_gate': out['v_w_ple_gate'], 'v_w_ple_proj': out['v_w_ple_proj'], 'v_final_norm': out['v_final_norm']}


def _loss(weights, diff, rest, loss_target):
    with _jax.named_scope("forward"):
        args = {**rest, TWIN_DIFF_INPUT: diff, **{k: w.astype(_WEIGHT_DTYPES[k]) for k, w in weights.items()}}
        y = _forward(args)
    with _jax.named_scope("loss_head"):
        err = _jnp.square(y.astype(_jnp.float32) - loss_target)
        return 0.5 * _jnp.sum(_jnp.mean(err, axis=-1)) if err.ndim else 0.5 * err


def _adamw(w, g, m, v):
    m = ADAM_B1 * m + (1.0 - ADAM_B1) * g
    v = ADAM_B2 * v + (1.0 - ADAM_B2) * _jnp.square(g)
    m_hat = m / (1.0 - ADAM_B1 ** ADAM_STEP)
    v_hat = v / (1.0 - ADAM_B2 ** ADAM_STEP)
    delta = -ADAM_LR * (m_hat / (_jnp.sqrt(v_hat) + ADAM_EPS) + ADAM_WD * w)
    return delta, m, v


def reference(x, p, ffn1_norm, ffn1_w_in, ffn1_w_out, mix_norm, w_mix_in, conv_w, w_conv_out, w_attn_out, w_mix_out, ffn2_norm, ffn2_w_in, ffn2_w_out, ple_norm, w_ple_gate, w_ple_proj, final_norm, loss_target, m_ffn1_norm, m_ffn1_w_in, m_ffn1_w_out, m_mix_norm, m_w_mix_in, m_conv_w, m_w_conv_out, m_w_attn_out, m_w_mix_out, m_ffn2_norm, m_ffn2_w_in, m_ffn2_w_out, m_ple_norm, m_w_ple_gate, m_w_ple_proj, m_final_norm, v_ffn1_norm, v_ffn1_w_in, v_ffn1_w_out, v_mix_norm, v_w_mix_in, v_conv_w, v_w_conv_out, v_w_attn_out, v_w_mix_out, v_ffn2_norm, v_ffn2_w_in, v_ffn2_w_out, v_ple_norm, v_w_ple_gate, v_w_ple_proj, v_final_norm):
    given = dict(x=x, p=p, ffn1_norm=ffn1_norm, ffn1_w_in=ffn1_w_in, ffn1_w_out=ffn1_w_out, mix_norm=mix_norm, w_mix_in=w_mix_in, conv_w=conv_w, w_conv_out=w_conv_out, w_attn_out=w_attn_out, w_mix_out=w_mix_out, ffn2_norm=ffn2_norm, ffn2_w_in=ffn2_w_in, ffn2_w_out=ffn2_w_out, ple_norm=ple_norm, w_ple_gate=w_ple_gate, w_ple_proj=w_ple_proj, final_norm=final_norm, loss_target=loss_target, m_ffn1_norm=m_ffn1_norm, m_ffn1_w_in=m_ffn1_w_in, m_ffn1_w_out=m_ffn1_w_out, m_mix_norm=m_mix_norm, m_w_mix_in=m_w_mix_in, m_conv_w=m_conv_w, m_w_conv_out=m_w_conv_out, m_w_attn_out=m_w_attn_out, m_w_mix_out=m_w_mix_out, m_ffn2_norm=m_ffn2_norm, m_ffn2_w_in=m_ffn2_w_in, m_ffn2_w_out=m_ffn2_w_out, m_ple_norm=m_ple_norm, m_w_ple_gate=m_w_ple_gate, m_w_ple_proj=m_w_ple_proj, m_final_norm=m_final_norm, v_ffn1_norm=v_ffn1_norm, v_ffn1_w_in=v_ffn1_w_in, v_ffn1_w_out=v_ffn1_w_out, v_mix_norm=v_mix_norm, v_w_mix_in=v_w_mix_in, v_conv_w=v_conv_w, v_w_conv_out=v_w_conv_out, v_w_attn_out=v_w_attn_out, v_w_mix_out=v_w_mix_out, v_ffn2_norm=v_ffn2_norm, v_ffn2_w_in=v_ffn2_w_in, v_ffn2_w_out=v_ffn2_w_out, v_ple_norm=v_ple_norm, v_w_ple_gate=v_w_ple_gate, v_w_ple_proj=v_w_ple_proj, v_final_norm=v_final_norm)
    weights = {n: given[n] for n in TWIN_WEIGHTS}
    shared = {n: given[n] for n in SHARED_INPUTS}
    per_example = {n: given[n] for n in ['x', 'p']}
    grad_fn = _jax.value_and_grad(_loss, argnums=(0, 1))

    def one_microbatch(ex, loss_target):
        ex = dict(ex)
        diff = ex.pop(TWIN_DIFF_INPUT)
        return grad_fn(weights, diff, {**shared, **ex}, loss_target)

    if N_MICROBATCH == 1:
        loss, (grad_w, grad_x) = one_microbatch(per_example, given["loss_target"])
    else:
        def body(carry, xs):
            loss_sum, grad_sum = carry
            l_k, (gw_k, gx_k) = one_microbatch(xs[0], xs[1])
            with _jax.named_scope("update"):
                return (loss_sum + l_k, _jax.tree.map(_jnp.add, grad_sum, gw_k)), gx_k

        init = (_jnp.zeros((), _jnp.float32), _jax.tree.map(_jnp.zeros_like, weights))
        (loss, grad_w), grad_x = _jax.lax.scan(body, init, (per_example, given["loss_target"]))
    with _jax.named_scope("update"):
        delta_w, new_m, new_v = {}, {}, {}
        for n in TWIN_WEIGHTS:
            delta_w[n], new_m[n], new_v[n] = _adamw(weights[n], grad_w[n], given["m_" + n], given["v_" + n])
    return (loss, grad_x, *[grad_w[n] for n in TWIN_WEIGHTS], *[delta_w[n] for n in TWIN_WEIGHTS],
            *[new_m[n] for n in TWIN_WEIGHTS], *[new_v[n] for n in TWIN_WEIGHTS])
```

```python
import functools
import math

import jax
import jax.numpy as jnp
from jax import lax
from jax.experimental import pallas as pl
from jax.experimental.pallas import tpu as pltpu

F32 = jnp.float32
BF16 = jnp.bfloat16
MESH = pl.DeviceIdType.MESH
ANY = pl.BlockSpec(memory_space=pl.ANY)

HEAD_DIM = 128
NORM_EPS = 1e-6
N_CHIPS = 4
N_DEV = 8
BF16_ROWS = 16
VMEM_LIMIT = 56 * 1024 * 1024
STICK_EXIT = 110.0

ADAM_LR = 0.001
ADAM_B1 = 0.9
ADAM_B2 = 0.999
ADAM_EPS = 1e-08
ADAM_WD = 0.01
ADAM_STEP = 10

NN = (((1,), (0,)), ((), ()))
NT = (((1,), (1,)), ((), ()))
TN = (((0,), (0,)), ((), ()))


def _params(sem=None, **kw):
    if sem is not None:
        kw["dimension_semantics"] = sem
    return pltpu.CompilerParams(vmem_limit_bytes=VMEM_LIMIT, **kw)


def _pcall(body, **kw):
    return pl.pallas_call(body, **kw)


def _tile(n, pref, mult=8):
    best = None
    for d in range(mult, min(n, pref) + 1, mult):
        if n % d == 0:
            best = d
    return best if best is not None else n


def _dot(a, b, dims):
    return lax.dot_general(a, b, dims, preferred_element_type=F32)


def _mm(name, a, b, out_sds, grid, a_spec, b_spec, o_spec, dims, acc_shape, res=None, alpha=1.0):
    nk = grid[2]

    def body(*refs):
        if res is not None:
            a_ref, b_ref, r_ref, o_ref = refs[:4]
        else:
            a_ref, b_ref, o_ref = refs[:3]
            r_ref = None

        def finish(r):
            if alpha != 1.0:
                r = r * alpha
            if r_ref is not None:
                r = r_ref[...] + r
            o_ref[...] = r.astype(o_ref.dtype)

        part = _dot(a_ref[...].astype(BF16), b_ref[...].astype(BF16), dims)
        if nk == 1:
            finish(part)
        else:
            acc_ref = refs[-1]
            kk = pl.program_id(2)

            @pl.when(kk == 0)
            def _():
                acc_ref[...] = part

            @pl.when(kk > 0)
            def _():
                acc_ref[...] += part

            @pl.when(kk == nk - 1)
            def _():
                finish(acc_ref[...])

    in_specs = [a_spec, b_spec]
    args = [a, b]
    if res is not None:
        in_specs.append(o_spec)
        args.append(res)
    scratch = [] if nk == 1 else [pltpu.VMEM(acc_shape, F32)]
    return _pcall(body, name=name, out_shape=out_sds, grid=grid, in_specs=in_specs, out_specs=o_spec,
                  scratch_shapes=scratch,
                  compiler_params=_params(("parallel", "parallel", "arbitrary")))(*args)


def mm_nn(name, a, w, out_dtype, tm, res=None, alpha=1.0):
    m, k = a.shape
    n = w.shape[1]
    return _mm(name, a, w, jax.ShapeDtypeStruct((m, n), out_dtype), (m // tm, 1, 1),
               pl.BlockSpec((tm, k), lambda i, j, r: (i, 0)),
               pl.BlockSpec((k, n), lambda i, j, r: (0, 0)),
               pl.BlockSpec((tm, n), lambda i, j, r: (i, 0)), NN, None, res=res, alpha=alpha)


def mm_nn_stacked(name, a, w4, out_dtype, tm, tn, j0=0, nj=None):
    m, k = a.shape
    cs = w4.shape[2]
    per = cs // tn
    nj = N_CHIPS * per - j0 if nj is None else nj
    return _mm(name, a, w4, jax.ShapeDtypeStruct((m, nj * tn), out_dtype), (m // tm, nj, 1),
               pl.BlockSpec((tm, k), lambda i, j, r: (i, 0)),
               pl.BlockSpec((None, k, tn), lambda i, j, r: ((j + j0) // per, 0, (j + j0) % per)),
               pl.BlockSpec((tm, tn), lambda i, j, r: (i, j)), NN, None)


def mm_nt(name, dy, w, out_dtype, tm, tko):
    m, n = dy.shape
    k = w.shape[0]
    return _mm(name, dy, w, jax.ShapeDtypeStruct((m, k), out_dtype), (m // tm, k // tko, 1),
               pl.BlockSpec((tm, n), lambda i, j, r: (i, 0)),
               pl.BlockSpec((tko, n), lambda i, j, r: (j, 0)),
               pl.BlockSpec((tm, tko), lambda i, j, r: (i, j)), NT, None)


def mm_nt_stacked(name, dy, w4, out_dtype, tm, tn):
    m = dy.shape[0]
    k, cs = w4.shape[1], w4.shape[2]
    per = cs // tn
    return _mm(name, dy, w4, jax.ShapeDtypeStruct((m, k), out_dtype), (m // tm, 1, N_CHIPS * per),
               pl.BlockSpec((tm, tn), lambda i, j, r: (i, r)),
               pl.BlockSpec((None, k, tn), lambda i, j, r: (r // per, 0, r % per)),
               pl.BlockSpec((tm, k), lambda i, j, r: (i, 0)), NT, (tm, k))


def mm_tn_rows(name, xa, dy, tt):
    t, k = xa.shape
    n = dy.shape[1]
    tkr = k // 2
    return _mm(name, xa, dy, jax.ShapeDtypeStruct((k, n), BF16), (k // tkr, 1, t // tt),
               pl.BlockSpec((tt, tkr), lambda i, j, r: (r, i)),
               pl.BlockSpec((tt, n), lambda i, j, r: (r, 0)),
               pl.BlockSpec((tkr, n), lambda i, j, r: (i, 0)), TN, (tkr, n))


def mm_tn_cols(name, xa, dy, tt):
    t, k = xa.shape
    cs = dy.shape[1] // N_CHIPS
    pr = k // 2
    return _mm(name, xa, dy, jax.ShapeDtypeStruct((N_CHIPS, 2, pr, cs), BF16), (2, N_CHIPS, t // tt),
               pl.BlockSpec((tt, pr), lambda i, j, r: (r, i)),
               pl.BlockSpec((tt, cs), lambda i, j, r: (r, j)),
               pl.BlockSpec((None, None, pr, cs), lambda i, j, r: (j, i, 0, 0)), TN, (pr, cs))


def _rows(tt, w, col=0):
    return pl.BlockSpec((tt, w), lambda i: (i, col))


def _whole(shape):
    return pl.BlockSpec(shape, lambda i: (0,) * len(shape))


def _rstd(h):
    return lax.rsqrt(jnp.mean(h * h, axis=-1, keepdims=True) + NORM_EPS)


def rms_fwd(name, h, g, tt):
    t, d = h.shape

    def body(h_ref, g_ref, o_ref):
        hv = h_ref[...]
        o_ref[...] = (hv * _rstd(hv) * g_ref[...]).astype(o_ref.dtype)

    return _pcall(body, name=name, out_shape=jax.ShapeDtypeStruct((t, d), BF16), grid=(t // tt,),
                  in_specs=[_rows(tt, d), _whole((1, d))], out_specs=_rows(tt, d),
                  compiler_params=_params(("parallel",)))(h, g)


def rms_bwd(name, h, g, dn, dres, alpha, tt):
    t, d = h.shape

    def body(h_ref, g_ref, dn_ref, dr_ref, dh_ref, dhb_ref, dg_ref):
        hv = h_ref[...]
        hn = hv * _rstd(hv)
        dnv = dn_ref[...]
        gy = dnv * g_ref[...]
        dh = dr_ref[...] + _rstd(hv) * (gy - hn * jnp.mean(gy * hn, axis=-1, keepdims=True))
        dh_ref[...] = dh
        dhb_ref[...] = (alpha * dh).astype(BF16)

        @pl.when(pl.program_id(0) == 0)
        def _():
            dg_ref[...] = jnp.zeros_like(dg_ref)

        dg_ref[...] += jnp.sum(dnv * hn, axis=0, keepdims=True)

    return _pcall(body, name=name,
                  out_shape=(jax.ShapeDtypeStruct((t, d), F32), jax.ShapeDtypeStruct((t, d), BF16),
                             jax.ShapeDtypeStruct((1, d), F32)),
                  grid=(t // tt,),
                  in_specs=[_rows(tt, d), _whole((1, d)), _rows(tt, d), _rows(tt, d)],
                  out_specs=(_rows(tt, d), _rows(tt, d), _whole((1, d))),
                  compiler_params=_params(("arbitrary",)))(h, g, dn, dres)


def swiglu_fwd(name, a, tt):
    t, f2 = a.shape
    f = f2 // 2

    def body(a_ref, o_ref):
        gate = a_ref[:, :f]
        up = a_ref[:, f:]
        o_ref[...] = (gate * jax.nn.sigmoid(gate) * up).astype(o_ref.dtype)

    return _pcall(body, name=name, out_shape=jax.ShapeDtypeStruct((t, f), BF16), grid=(t // tt,),
                  in_specs=[_rows(tt, f2)], out_specs=_rows(tt, f),
                  compiler_params=_params(("parallel",)))(a)


def swiglu_bwd(name, a, ds, tt):
    t, f2 = a.shape
    f = f2 // 2

    def body(a_ref, ds_ref, o_ref):
        gate = a_ref[:, :f]
        up = a_ref[:, f:]
        dsv = ds_ref[...]
        sg = jax.nn.sigmoid(gate)
        o_ref[:, :f] = (dsv * up * sg * (1.0 + gate * (1.0 - sg))).astype(o_ref.dtype)
        o_ref[:, f:] = (dsv * gate * sg).astype(o_ref.dtype)

    return _pcall(body, name=name, out_shape=jax.ShapeDtypeStruct((t, f2), BF16), grid=(t // tt,),
                  in_specs=[_rows(tt, f2), _rows(tt, f)], out_specs=_rows(tt, f2),
                  compiler_params=_params(("parallel",)))(a, ds)


def gate_fwd(name, gates, yc, ya, tt):
    t, d = yc.shape

    def body(g_ref, yc_ref, ya_ref, o_ref):
        o_ref[...] = (jax.nn.sigmoid(g_ref[:, :d]) * yc_ref[...]
                      + jax.nn.sigmoid(g_ref[:, d:]) * ya_ref[...]).astype(o_ref.dtype)

    return _pcall(body, name=name, out_shape=jax.ShapeDtypeStruct((t, d), BF16), grid=(t // tt,),
                  in_specs=[_rows(tt, 2 * d), _rows(tt, d), _rows(tt, d)], out_specs=_rows(tt, d),
                  compiler_params=_params(("parallel",)))(gates, yc, ya)


def gate_bwd(name, dm, gates, yc, ya, tt):
    t, d = yc.shape

    def body(dm_ref, g_ref, yc_ref, ya_ref, dyc_ref, dya_ref, dg_ref):
        dmv = dm_ref[...]
        sc = jax.nn.sigmoid(g_ref[:, :d])
        sa = jax.nn.sigmoid(g_ref[:, d:])
        dyc_ref[...] = (dmv * sc).astype(BF16)
        dya_ref[...] = (dmv * sa).astype(BF16)
        dg_ref[:, :d] = (dmv * yc_ref[...] * sc * (1.0 - sc)).astype(BF16)
        dg_ref[:, d:] = (dmv * ya_ref[...] * sa * (1.0 - sa)).astype(BF16)

    return _pcall(body, name=name,
                  out_shape=(jax.ShapeDtypeStruct((t, d), BF16), jax.ShapeDtypeStruct((t, d), BF16),
                             jax.ShapeDtypeStruct((t, 2 * d), BF16)),
                  grid=(t // tt,),
                  in_specs=[_rows(tt, d), _rows(tt, 2 * d), _rows(tt, d), _rows(tt, d)],
                  out_specs=(_rows(tt, d), _rows(tt, d), _rows(tt, 2 * d)),
                  compiler_params=_params(("parallel",)))(dm, gates, yc, ya)


def _shift_down(cur, prev8, s):
    tt = cur.shape[0]
    rolled = pltpu.roll(cur, s, 0)
    row8 = lax.broadcasted_iota(jnp.int32, prev8.shape, 0)
    first8 = jnp.where(row8 < s, pltpu.roll(prev8, s, 0), rolled[:8])
    return jnp.concatenate([first8, rolled[8:]], axis=0) if tt > 8 else first8


def _shift_up(cur, next8, s):
    tt = cur.shape[0]
    rolled = pltpu.roll(cur, tt - s, 0)
    row8 = lax.broadcasted_iota(jnp.int32, next8.shape, 0)
    last8 = jnp.where(row8 >= 8 - s, pltpu.roll(next8, 8 - s, 0), rolled[tt - 8:])
    return jnp.concatenate([rolled[:tt - 8], last8], axis=0) if tt > 8 else last8


def _prev8(tt, d, col):
    return pl.BlockSpec((8, d), lambda i: (jnp.maximum(i * (tt // 8) - 1, 0), col))


def _next8(tt, d, col, t):
    return pl.BlockSpec((8, d), lambda i: (jnp.minimum((i + 1) * (tt // 8), t // 8 - 1), col))


def conv_fwd(name, cbx, cw8, tt):
    t, d3 = cbx.shape
    d = d3 // 3

    def body(cb_ref, cc_ref, cx_ref, pc_ref, px_ref, w_ref, o_ref):
        has_prev = (pl.program_id(0) > 0).astype(F32)
        cc = cc_ref[...] * cx_ref[...]
        prev = pc_ref[...] * px_ref[...] * has_prev
        w = w_ref[...]
        conv = w[0:1] * _shift_down(cc, prev, 2) + w[1:2] * _shift_down(cc, prev, 1) + w[2:3] * cc
        o_ref[...] = (cb_ref[...] * conv).astype(o_ref.dtype)

    return _pcall(body, name=name, out_shape=jax.ShapeDtypeStruct((t, d), BF16), grid=(t // tt,),
                  in_specs=[_rows(tt, d, 0), _rows(tt, d, 1), _rows(tt, d, 2), _prev8(tt, d, 1), _prev8(tt, d, 2),
                            _whole((8, d))],
                  out_specs=_rows(tt, d), compiler_params=_params(("parallel",)))(cbx, cbx, cbx, cbx, cbx, cw8)


def conv_bwd(name, dyc, cbx, cw8, tt):
    t, d3 = cbx.shape
    d = d3 // 3
    n = t // tt

    def body(dy_ref, cb_ref, cc_ref, cx_ref, pc_ref, px_ref, ndy_ref, ncb_ref, w_ref, o_ref, dw_ref):
        i = pl.program_id(0)
        has_prev = (i > 0).astype(F32)
        has_next = (i < n - 1).astype(F32)
        cb = cb_ref[...]
        cc = cc_ref[...] * cx_ref[...]
        prev = pc_ref[...] * px_ref[...] * has_prev
        w = w_ref[...]
        cc1 = _shift_down(cc, prev, 1)
        cc2 = _shift_down(cc, prev, 2)
        conv = w[0:1] * cc2 + w[1:2] * cc1 + w[2:3] * cc
        dyv = dy_ref[...]
        dconv = dyv * cb
        dnext = ndy_ref[...] * ncb_ref[...] * has_next
        dcc = w[2:3] * dconv + w[1:2] * _shift_up(dconv, dnext, 1) + w[0:1] * _shift_up(dconv, dnext, 2)
        o_ref[:, :d] = (dyv * conv).astype(BF16)
        o_ref[:, d:2 * d] = (dcc * cx_ref[...]).astype(BF16)
        o_ref[:, 2 * d:] = (dcc * cc_ref[...]).astype(BF16)

        @pl.when(i == 0)
        def _():
            dw_ref[...] = jnp.zeros_like(dw_ref)

        dw_ref[0:1, :] += jnp.sum(dconv * cc2, axis=0, keepdims=True)
        dw_ref[1:2, :] += jnp.sum(dconv * cc1, axis=0, keepdims=True)
        dw_ref[2:3, :] += jnp.sum(dconv * cc, axis=0, keepdims=True)

    return _pcall(body, name=name,
                  out_shape=(jax.ShapeDtypeStruct((t, d3), BF16), jax.ShapeDtypeStruct((8, d), F32)),
                  grid=(n,),
                  in_specs=[_rows(tt, d), _rows(tt, d, 0), _rows(tt, d, 1), _rows(tt, d, 2),
                            _prev8(tt, d, 1), _prev8(tt, d, 2), _next8(tt, d, 0, t), _next8(tt, d, 0, t),
                            _whole((8, d))],
                  out_specs=(_rows(tt, d3), _whole((8, d))),
                  compiler_params=_params(("arbitrary",)))(dyc, cbx, cbx, cbx, cbx, cbx, dyc, cbx, cw8)


def tail(name, h3, zg, pp, tgt, gf, tt):
    t, d = h3.shape

    def body(h_ref, zg_ref, pp_ref, tg_ref, gf_ref, dh_ref, dpp_ref, dzg_ref, dgf_ref, loss_ref):
        pg = jax.nn.sigmoid(zg_ref[...])
        ppv = pp_ref[...]
        h4 = h_ref[...] + pg * ppv
        r4 = _rstd(h4)
        hn = h4 * r4
        gfv = gf_ref[...]
        err = hn * gfv - tg_ref[...]
        dy = err * (1.0 / d)
        gy = dy * gfv
        dh4 = r4 * (gy - hn * jnp.mean(gy * hn, axis=-1, keepdims=True))
        dh_ref[...] = dh4
        dpp_ref[...] = (dh4 * pg).astype(BF16)
        dzg_ref[...] = (dh4 * ppv * pg * (1.0 - pg)).astype(BF16)

        @pl.when(pl.program_id(0) == 0)
        def _():
            dgf_ref[...] = jnp.zeros_like(dgf_ref)
            loss_ref[...] = jnp.zeros_like(loss_ref)

        dgf_ref[...] += jnp.sum(dy * hn, axis=0, keepdims=True)
        tok = jnp.mean(err * err, axis=-1, keepdims=True)
        loss_ref[...] += 0.5 * jnp.sum(tok, axis=0, keepdims=True) * jnp.ones((1, loss_ref.shape[1]), F32)

    return _pcall(body, name=name,
                  out_shape=(jax.ShapeDtypeStruct((t, d), F32), jax.ShapeDtypeStruct((t, d), BF16),
                             jax.ShapeDtypeStruct((t, d), BF16), jax.ShapeDtypeStruct((1, d), F32),
                             jax.ShapeDtypeStruct((1, d), F32)),
                  grid=(t // tt,),
                  in_specs=[_rows(tt, d)] * 4 + [_whole((1, d))],
                  out_specs=(_rows(tt, d), _rows(tt, d), _rows(tt, d), _whole((1, d)), _whole((1, d))),
                  compiler_params=_params(("arbitrary",)))(h3, zg, pp, tgt, gf)


def _sb_block(q, kj, row, col, upper, c, diag):
    z = _dot(q, kj, NT) * (1.0 / math.sqrt(HEAD_DIM))
    lg = -(jnp.maximum(z, 0.0) + jnp.log(1.0 + jnp.exp(-jnp.abs(z))))
    if diag:
        lg = jnp.where(col < row, lg, 0.0)
    hi = lg.astype(BF16)
    lo = (lg - hi.astype(F32)).astype(BF16)
    cum = _dot(hi, upper, NN) + _dot(lo, upper, NN)
    log_a = z + cum + c
    if diag:
        log_a = jnp.where(col < row, log_a, -1e30)
    return z, jnp.exp(log_a), c + cum[:, 0:1]


def attn_fwd(name, qkv, tq):
    t, d3 = qkv.shape
    d = d3 // 3
    nh = d // HEAD_DIM
    nq = t // tq

    def body(q_ref, k_ref, v_ref, o_ref):
        i = pl.program_id(1)
        q = q_ref[...]
        row = lax.broadcasted_iota(jnp.int32, (tq, tq), 0)
        col = lax.broadcasted_iota(jnp.int32, (tq, tq), 1)
        upper = (row >= col).astype(BF16)

        def block(j, c, acc, diag):
            rows = pl.ds(pl.multiple_of(j * tq, tq), tq)
            _, a, c = _sb_block(q, k_ref[rows, :], row, col, upper, c, diag)
            return c, acc + _dot(a.astype(BF16), v_ref[rows, :], NN)

        c, acc = block(i, jnp.zeros((tq, 1), F32), jnp.zeros((tq, HEAD_DIM), F32), True)

        def cond(st):
            return jnp.logical_and(st[0] >= 0, jnp.max(st[1]) > -STICK_EXIT)

        def step(st):
            c2, acc2 = block(st[0], st[1], st[2], False)
            return st[0] - 1, c2, acc2

        _, _, acc = lax.while_loop(cond, step, (i - 1, c, acc))
        o_ref[...] = acc.astype(o_ref.dtype)

    return _pcall(body, name=name, out_shape=jax.ShapeDtypeStruct((t, d), BF16), grid=(nh, nq),
                  in_specs=[pl.BlockSpec((tq, HEAD_DIM), lambda h, i: (i, h)),
                            pl.BlockSpec((t, HEAD_DIM), lambda h, i: (0, nh + h)),
                            pl.BlockSpec((t, HEAD_DIM), lambda h, i: (0, 2 * nh + h))],
                  out_specs=pl.BlockSpec((tq, HEAD_DIM), lambda h, i: (i, h)),
                  compiler_params=_params(("parallel", "arbitrary")))(qkv, qkv, qkv)


def attn_bwd(name, qkv, do, tq):
    t, d3 = qkv.shape
    d = d3 // 3
    nh = d // HEAD_DIM
    nq = t // tq
    scale = 1.0 / math.sqrt(HEAD_DIM)

    def body(q_ref, k_ref, v_ref, do_ref, dq_ref, dk_ref, dv_ref, dk_acc, dv_acc, g_buf, z_buf):
        i = pl.program_id(1)

        @pl.when(i == 0)
        def _():
            dk_acc[...] = jnp.zeros_like(dk_acc)
            dv_acc[...] = jnp.zeros_like(dv_acc)

        q = q_ref[...]
        dov = do_ref[...]
        row = lax.broadcasted_iota(jnp.int32, (tq, tq), 0)
        col = lax.broadcasted_iota(jnp.int32, (tq, tq), 1)
        upper = (row >= col).astype(BF16)
        lower = (row <= col).astype(BF16)

        def first(j, c, diag):
            rows = pl.ds(pl.multiple_of(j * tq, tq), tq)
            z, a, c = _sb_block(q, k_ref[rows, :], row, col, upper, c, diag)
            g = _dot(dov, v_ref[rows, :], NT) * a
            g_buf[i - j] = g
            z_buf[i - j] = z
            dv_acc[rows, :] += _dot(a.astype(BF16), dov, TN)
            return c

        c = first(i, jnp.zeros((tq, 1), F32), True)

        def cond(st):
            return jnp.logical_and(st[0] >= 0, jnp.max(st[1]) > -STICK_EXIT)

        def step(st):
            return st[0] - 1, first(st[0], st[1], False)

        j_stop, _ = lax.while_loop(cond, step, (i - 1, c))

        def second(j, st):
            run, dq = st
            rows = pl.ds(pl.multiple_of(j * tq, tq), tq)
            g = g_buf[i - j]
            hi = g.astype(BF16)
            lo = (g - hi.astype(F32)).astype(BF16)
            p = run + _dot(hi, lower, NN) + _dot(lo, lower, NN)
            dz = g - jax.nn.sigmoid(z_buf[i - j]) * p
            dz = jnp.where(jnp.logical_or(col < row, j < i), dz, 0.0).astype(BF16)
            dk_acc[rows, :] += _dot(dz, q, TN)
            return run + jnp.sum(g, axis=1, keepdims=True), dq + _dot(dz, k_ref[rows, :], NN)

        _, dq = lax.fori_loop(j_stop + 1, i + 1, second,
                              (jnp.zeros((tq, 1), F32), jnp.zeros((tq, HEAD_DIM), F32)))
        dq_ref[...] = (dq * scale).astype(BF16)

        @pl.when(i == nq - 1)
        def _():
            dk_ref[...] = (dk_acc[...] * scale).astype(BF16)
            dv_ref[...] = dv_acc[...].astype(BF16)

    blk = pl.BlockSpec((tq, HEAD_DIM), lambda h, i: (i, h))
    col_h = pl.BlockSpec((t, HEAD_DIM), lambda h, i: (0, h))
    out = jax.ShapeDtypeStruct((t, d), BF16)
    return _pcall(body, name=name, out_shape=(out, out, out), grid=(nh, nq),
                  in_specs=[blk,
                            pl.BlockSpec((t, HEAD_DIM), lambda h, i: (0, nh + h)),
                            pl.BlockSpec((t, HEAD_DIM), lambda h, i: (0, 2 * nh + h)),
                            blk],
                  out_specs=(blk, col_h, col_h),
                  scratch_shapes=[pltpu.VMEM((t, HEAD_DIM), F32), pltpu.VMEM((t, HEAD_DIM), F32),
                                  pltpu.VMEM((nq, tq, tq), F32), pltpu.VMEM((nq, tq, tq), F32)],
                  compiler_params=_params(("parallel", "arbitrary")))(qkv, qkv, qkv, do)


def _place():
    x, y, c = lax.axis_index("x"), lax.axis_index("y"), lax.axis_index("c")
    chips = [(1 - x, y), (x, 1 - y), (1 - x, 1 - y)]
    return x, y, c, chips


def _remote(src, dst, send_sem, recv_sem, dev):
    return pltpu.make_async_remote_copy(src_ref=src, dst_ref=dst, send_sem=send_sem, recv_sem=recv_sem,
                                        device_id=dev, device_id_type=MESH)


def gather_weights(shards):
    n = len(shards)

    def body(*refs):
        src, out = refs[:n], refs[n:2 * n]
        local_sem, isend, irecv, dsend, drecv = refs[2 * n:]
        x, y, c, chips = _place()
        me = 2 * x + y
        sib = (x, y, 1 - c)
        local, first, passed = [], [], []
        for w in range(n):
            pr = src[w].shape[0] // 2
            half = pl.ds(pl.multiple_of(c * pr, BF16_ROWS), pr)
            mine = pltpu.make_async_copy(src[w], out[w].at[me], local_sem.at[w])
            mine.start()
            local.append(mine)
            for j, (cx, cy) in enumerate(chips):
                cp = _remote(src[w].at[half], out[w].at[me, half], isend.at[3 * w + j], irecv.at[3 * w + j],
                             (cx, cy, c))
                cp.start()
                first.append(cp)
        for w in range(n):
            pr = src[w].shape[0] // 2
            half = pl.ds(pl.multiple_of(c * pr, BF16_ROWS), pr)
            for j, (cx, cy) in enumerate(chips):
                landed = out[w].at[2 * cx + cy, half]
                _remote(landed, landed, isend.at[3 * w + j], irecv.at[3 * w + j], sib).wait_recv()
                fw = _remote(landed, landed, dsend.at[3 * w + j], drecv.at[3 * w + j], sib)
                fw.start()
                passed.append(fw)
        for w in range(n):
            pr = src[w].shape[0] // 2
            other = pl.ds(pl.multiple_of((1 - c) * pr, BF16_ROWS), pr)
            for j, (cx, cy) in enumerate(chips):
                landed = out[w].at[2 * cx + cy, other]
                _remote(landed, landed, dsend.at[3 * w + j], drecv.at[3 * w + j], sib).wait_recv()
        for cp in first + passed:
            cp.wait_send()
        for cp in local:
            cp.wait()

    return _pcall(body, name="gather_weights",
                  out_shape=[jax.ShapeDtypeStruct((N_CHIPS,) + s.shape, s.dtype) for s in shards],
                  in_specs=[ANY] * n, out_specs=[ANY] * n,
                  scratch_shapes=[pltpu.SemaphoreType.DMA((n,))] + [pltpu.SemaphoreType.DMA((3 * n,))] * 4,
                  compiler_params=_params())(*shards)


def exchange_siblings(pieces):
    n = len(pieces)

    def body(*refs):
        src, out = refs[:n], refs[n:2 * n]
        local_sem, send_sem, recv_sem = refs[2 * n:]
        x, y, c, _ = _place()
        sib = (x, y, 1 - c)
        started = []
        for w in range(n):
            for k in range(N_CHIPS):
                s = N_CHIPS * w + k
                mine = pltpu.make_async_copy(src[w].at[k, c], out[w].at[c, k], local_sem.at[s])
                mine.start()
                cp = _remote(src[w].at[k, 1 - c], out[w].at[c, k], send_sem.at[s], recv_sem.at[s], sib)
                cp.start()
                started.append((mine, cp))
        for w in range(n):
            for k in range(N_CHIPS):
                s = N_CHIPS * w + k
                landed = out[w].at[1 - c, k]
                _remote(landed, landed, send_sem.at[s], recv_sem.at[s], sib).wait_recv()
        for mine, cp in started:
            mine.wait()
            cp.wait_send()

    return _pcall(body, name="exchange_siblings",
                  out_shape=[jax.ShapeDtypeStruct((2, N_CHIPS) + s.shape[2:], s.dtype) for s in pieces],
                  in_specs=[ANY] * n, out_specs=[ANY] * n,
                  scratch_shapes=[pltpu.SemaphoreType.DMA((N_CHIPS * n,))] * 3,
                  compiler_params=_params())(*pieces)


def scatter_chips(parts):
    n = len(parts)

    def body(*refs):
        src, out = refs[:n], refs[n:2 * n]
        local_sem, send_sem, recv_sem = refs[2 * n:]
        x, y, c, chips = _place()
        me = 2 * x + y
        started = []
        for w in range(n):
            mine = pltpu.make_async_copy(src[w].at[me], out[w].at[me], local_sem.at[w])
            mine.start()
            started.append(mine)
            for j, (cx, cy) in enumerate(chips):
                cp = _remote(src[w].at[2 * cx + cy], out[w].at[me], send_sem.at[3 * w + j], recv_sem.at[3 * w + j],
                             (cx, cy, c))
                cp.start()
                started.append(cp)
        for w in range(n):
            for j, (cx, cy) in enumerate(chips):
                landed = out[w].at[2 * cx + cy]
                _remote(landed, landed, send_sem.at[3 * w + j], recv_sem.at[3 * w + j], (x, y, c)).wait_recv()
        for k, cp in enumerate(started):
            cp.wait() if k % 4 == 0 else cp.wait_send()

    return _pcall(body, name="scatter_chips",
                  out_shape=[jax.ShapeDtypeStruct(s.shape, s.dtype) for s in parts],
                  in_specs=[ANY] * n, out_specs=[ANY] * n,
                  scratch_shapes=[pltpu.SemaphoreType.DMA((n,))] + [pltpu.SemaphoreType.DMA((3 * n,))] * 2,
                  compiler_params=_params())(*parts)


def share_halves(halves):
    n = len(halves)

    def body(*refs):
        src, out = refs[:n], refs[n:2 * n]
        local_sem, send_sem, recv_sem = refs[2 * n:]
        x, y, c, _ = _place()
        sib = (x, y, 1 - c)
        started = []
        for w in range(n):
            mine = pltpu.make_async_copy(src[w], out[w].at[c], local_sem.at[w])
            mine.start()
            cp = _remote(src[w], out[w].at[c], send_sem.at[w], recv_sem.at[w], sib)
            cp.start()
            started.append((mine, cp))
        for w in range(n):
            landed = out[w].at[1 - c]
            _remote(landed, landed, send_sem.at[w], recv_sem.at[w], sib).wait_recv()
        for mine, cp in started:
            mine.wait()
            cp.wait_send()

    return _pcall(body, name="share_halves",
                  out_shape=[jax.ShapeDtypeStruct((2,) + s.shape, s.dtype) for s in halves],
                  in_specs=[ANY] * n, out_specs=[ANY] * n,
                  scratch_shapes=[pltpu.SemaphoreType.DMA((n,))] * 3,
                  compiler_params=_params())(*halves)


def gather_small(name, blk, reduce):
    r, cdim = blk.shape

    def body(in_ref, out_ref, *rest):
        if reduce:
            buf, send_sem, recv_sem = rest
        else:
            buf = out_ref
            send_sem, recv_sem = rest
        x, y, c, _ = _place()
        me = 4 * x + 2 * y + c
        buf[me] = in_ref[...]
        peers = []
        for dx in range(2):
            for dy in range(2):
                for dc in range(2):
                    if dx or dy or dc:
                        peers.append((dx, dy, dc))
        copies = []
        for s, (dx, dy, dc) in enumerate(peers):
            cp = _remote(in_ref, buf.at[me], send_sem.at[s], recv_sem.at[s],
                         ((1 - x if dx else x), (1 - y if dy else y), (1 - c if dc else c)))
            cp.start()
            copies.append(cp)
        for s, (dx, dy, dc) in enumerate(peers):
            px, py, pc_ = (1 - x if dx else x), (1 - y if dy else y), (1 - c if dc else c)
            landed = buf.at[4 * px + 2 * py + pc_]
            _remote(landed, landed, send_sem.at[s], recv_sem.at[s], (x, y, c)).wait_recv()
        for cp in copies:
            cp.wait_send()
        if reduce:
            tot = buf[0]
            for s in range(1, N_DEV):
                tot = tot + buf[s]
            out_ref[...] = tot

    vm = pl.BlockSpec(memory_space=pltpu.VMEM)
    out_shape = jax.ShapeDtypeStruct((r, cdim) if reduce else (N_DEV, r, cdim), F32)
    scratch = ([pltpu.VMEM((N_DEV, r, cdim), F32)] if reduce else []) + [pltpu.SemaphoreType.DMA((N_DEV - 1,))] * 2
    return _pcall(body, name=name, out_shape=out_shape, in_specs=[vm], out_specs=vm, scratch_shapes=scratch,
                  compiler_params=_params())(blk)


def add_slots(name, slots, out_dtype):
    s = slots.shape[0]
    pr, pc = slots.shape[-2:]
    nb = slots.size // (s * pr * pc)
    flat = slots.reshape(s, nb * pr, pc)
    tr = _tile(pr, max(8, (1 << 19) // pc), BF16_ROWS)

    def body(in_ref, o_ref):
        tot = in_ref[0].astype(F32)
        for k in range(1, s):
            tot = tot + in_ref[k].astype(F32)
        o_ref[...] = tot.astype(o_ref.dtype)

    out = _pcall(body, name=name, out_shape=jax.ShapeDtypeStruct((nb * pr, pc), out_dtype), grid=(nb * pr // tr,),
                 in_specs=[pl.BlockSpec((s, tr, pc), lambda i: (0, i, 0))],
                 out_specs=pl.BlockSpec((tr, pc), lambda i: (i, 0)),
                 compiler_params=_params(("parallel",)))(flat)
    return out.reshape(slots.shape[1:])


def adamw(name, w, g, m, v):
    rows, cols = w.shape
    tr = _tile(rows, max(8, (1 << 18) // cols))
    c1 = 1.0 / (1.0 - ADAM_B1 ** ADAM_STEP)
    c2 = 1.0 / (1.0 - ADAM_B2 ** ADAM_STEP)

    def body(w_ref, g_ref, m_ref, v_ref, d_ref, nm_ref, nv_ref):
        gv = g_ref[...]
        nm = ADAM_B1 * m_ref[...] + (1.0 - ADAM_B1) * gv
        nv = ADAM_B2 * v_ref[...] + (1.0 - ADAM_B2) * (gv * gv)
        nm_ref[...] = nm
        nv_ref[...] = nv
        d_ref[...] = -ADAM_LR * ((nm * c1) / (jnp.sqrt(nv * c2) + ADAM_EPS) + ADAM_WD * w_ref[...])

    spec = pl.BlockSpec((tr, cols), lambda i: (i, 0))
    sds = jax.ShapeDtypeStruct((rows, cols), F32)
    return _pcall(body, name=name, out_shape=(sds, sds, sds), grid=(rows // tr,),
                  in_specs=[spec] * 4, out_specs=(spec, spec, spec),
                  compiler_params=_params(("parallel",)))(w, g, m, v)


MATS = ["ffn1_w_in", "ffn1_w_out", "w_mix_in", "w_conv_out", "w_attn_out", "w_mix_out", "ffn2_w_in", "ffn2_w_out",
        "w_ple_gate", "w_ple_proj"]
COL_SHARDED = {"ffn1_w_in", "w_mix_in", "ffn2_w_in", "w_ple_proj"}
NORMS = ["ffn1_norm", "mix_norm", "ffn2_norm", "ple_norm", "final_norm"]
WEIGHTS = ["ffn1_norm", "ffn1_w_in", "ffn1_w_out", "mix_norm", "w_mix_in", "conv_w", "w_conv_out", "w_attn_out",
           "w_mix_out", "ffn2_norm", "ffn2_w_in", "ffn2_w_out", "ple_norm", "w_ple_gate", "w_ple_proj", "final_norm"]


def _pad_rows(a, rows):
    return jnp.concatenate([a, jnp.zeros((rows - a.shape[0],) + a.shape[1:], a.dtype)], axis=0)


def _step(x, p, tgt, w, m, v):
    t, d = x.shape
    tt = _tile(t, 256)
    tm = _tile(t, 512)
    tq = _tile(t, 256)

    full = dict(zip(MATS, gather_weights([w[k].astype(BF16) for k in MATS])))
    for k in MATS:
        if k not in COL_SHARDED:
            full[k] = full[k].reshape(-1, full[k].shape[2])
    cw_all = gather_small("gather_conv_w", _pad_rows(w["conv_w"], 8), False)
    cw8 = jnp.concatenate([cw_all[2 * k] for k in range(N_CHIPS)], axis=1)
    g1, gm, g2, gp, gf = (w[k].reshape(1, d) for k in NORMS)

    def ffn_fwd(tag, h, g, w_in, w_out):
        n = rms_fwd(tag + "_norm", h, g, tt)
        a = mm_nn_stacked(tag + "_in", n, w_in, F32, tm, w_in.shape[2])
        s = swiglu_fwd(tag + "_act", a, tt)
        return n, a, s, mm_nn(tag + "_out", s, w_out, F32, tm, res=h, alpha=0.5)

    def ffn_bwd(tag, h, g, w_in, w_out, n, a, s, df, dh):
        dw_out = mm_tn_rows(tag + "_dwout", s, df, tm)
        ds = mm_nt(tag + "_ds", df, w_out, F32, tm, w_out.shape[0] // 2)
        da = swiglu_bwd(tag + "_dact", a, ds, tt)
        dw_in = mm_tn_cols(tag + "_dwin", n, da, tm)
        dn = mm_nt_stacked(tag + "_dn", da, w_in, F32, tm, w_in.shape[2])
        return dn, dw_in, dw_out

    n1, a1, s1, h1 = ffn_fwd("ffn1", x, g1, full["ffn1_w_in"], full["ffn1_w_out"])
    u = rms_fwd("mix_norm", h1, gm, tt)
    wmix = full["w_mix_in"]
    cbx = mm_nn_stacked("mix_in_conv", u, wmix, F32, tm, d, 0, 3)
    qkv = mm_nn_stacked("mix_in_qkv", u, wmix, BF16, tm, d, 3, 3)
    gates = mm_nn_stacked("mix_in_gates", u, wmix, F32, tm, d, 6, 2)
    ycin = conv_fwd("conv", cbx, cw8, tt)
    y_conv = mm_nn("conv_out", ycin, full["w_conv_out"], F32, tm)
    o = attn_fwd("attn", qkv, tq)
    y_attn = mm_nn("attn_out", o, full["w_attn_out"], F32, tm)
    merged = gate_fwd("merge", gates, y_conv, y_attn, tt)
    h2 = mm_nn("mix_out", merged, full["w_mix_out"], F32, tm, res=h1, alpha=1.0)
    n2, a2, s2, h3 = ffn_fwd("ffn2", h2, g2, full["ffn2_w_in"], full["ffn2_w_out"])
    npl = rms_fwd("ple_norm", h3, gp, tt)
    zg = mm_nn("ple_gate", npl, full["w_ple_gate"], F32, tm)
    pp = mm_nn_stacked("ple_proj", p, full["w_ple_proj"], F32, tm, full["w_ple_proj"].shape[2])

    dh4, dpp, dzg, dgf, loss_row = tail("tail", h3, zg, pp, tgt, gf, tt)
    pieces = {}
    pieces["w_ple_proj"] = mm_tn_cols("ple_proj_dw", p, dpp, tm)
    pieces["w_ple_gate"] = mm_tn_rows("ple_gate_dw", npl, dzg, tm)
    dnp = mm_nt("ple_gate_dx", dzg, full["w_ple_gate"], F32, tm, d)
    dh3, df2, dgp = rms_bwd("ple_norm_bwd", h3, gp, dnp, dh4, 0.5, tt)
    dn2, pieces["ffn2_w_in"], pieces["ffn2_w_out"] = ffn_bwd(
        "ffn2", h2, g2, full["ffn2_w_in"], full["ffn2_w_out"], n2, a2, s2, df2, dh3)
    dh2, dh2b, dg2 = rms_bwd("ffn2_norm_bwd", h2, g2, dn2, dh3, 1.0, tt)
    pieces["w_mix_out"] = mm_tn_rows("mix_out_dw", merged, dh2b, tm)
    dmerged = mm_nt("mix_out_dx", dh2b, full["w_mix_out"], F32, tm, d)
    dyc, dya, dgates = gate_bwd("merge_bwd", dmerged, gates, y_conv, y_attn, tt)
    pieces["w_conv_out"] = mm_tn_rows("conv_out_dw", ycin, dyc, tm)
    dycin = mm_nt("conv_out_dx", dyc, full["w_conv_out"], F32, tm, d)
    dcbx, dcw8 = conv_bwd("conv_bwd", dycin, cbx, cw8, tt)
    pieces["w_attn_out"] = mm_tn_rows("attn_out_dw", o, dya, tm)
    do = mm_nt("attn_out_dx", dya, full["w_attn_out"], BF16, tm, d)
    dq, dk, dv = attn_bwd("attn_bwd", qkv, do, tq)
    dmix = jnp.concatenate([dcbx, dq, dk, dv, dgates], axis=1)
    pieces["w_mix_in"] = mm_tn_cols("mix_in_dw", u, dmix, tm)
    du = mm_nt_stacked("mix_in_dx", dmix, wmix, F32, tm, d)
    dh1, df1, dgm = rms_bwd("mix_norm_bwd", h1, gm, du, dh2, 0.5, tt)
    dn1, pieces["ffn1_w_in"], pieces["ffn1_w_out"] = ffn_bwd(
        "ffn1", x, g1, full["ffn1_w_in"], full["ffn1_w_out"], n1, a1, s1, df1, dh1)
    dx, _, dg1 = rms_bwd("ffn1_norm_bwd", x, g1, dn1, dh1, 1.0, tt)

    pcs = []
    for k in MATS:
        pc = pieces[k]
        if k not in COL_SHARDED:
            pc = pc.reshape(N_CHIPS, 2, pc.shape[0] // (2 * N_CHIPS), pc.shape[1])
        pcs.append(pc)
    both = exchange_siblings(pcs)
    chip_sums = [add_slots("sum_cores_" + k, b, BF16) for k, b in zip(MATS, both)]
    landed = scatter_chips(chip_sums)
    halves = [add_slots("sum_chips_" + k, q, F32) for k, q in zip(MATS, landed)]
    shared = share_halves(halves)
    grad, delta, new_m, new_v = {}, {}, {}, {}
    for k, sh in zip(MATS, shared):
        grad[k] = sh.reshape(w[k].shape)
        delta[k], new_m[k], new_v[k] = adamw("adamw_" + k, w[k], grad[k], m[k], v[k])

    small = jnp.concatenate([dg1, dgm, dg2, dgp, dgf, dcw8[:3], loss_row, jnp.zeros((7, d), F32)], axis=0)
    tot = gather_small("sum_small", small, True)
    loss = tot[8, 0]
    norm_w = jnp.concatenate([w[k].reshape(1, d) for k in NORMS] + [jnp.zeros((3, d), F32)], axis=0)
    norm_m = jnp.concatenate([m[k].reshape(1, d) for k in NORMS] + [jnp.zeros((3, d), F32)], axis=0)
    norm_v = jnp.concatenate([v[k].reshape(1, d) for k in NORMS] + [jnp.ones((3, d), F32)], axis=0)
    norm_g = jnp.concatenate([tot[0:5], jnp.zeros((3, d), F32)], axis=0)
    nd, nm, nv = adamw("adamw_norms", norm_w, norm_g, norm_m, norm_v)
    for r, k in enumerate(NORMS):
        grad[k] = norm_g[r].reshape(w[k].shape)
        delta[k], new_m[k], new_v[k] = (a[r].reshape(w[k].shape) for a in (nd, nm, nv))
    cs = d // N_CHIPS
    chip = 2 * lax.axis_index("x") + lax.axis_index("y")
    gcw = lax.dynamic_slice(tot[5:8], (0, chip * cs), (3, cs))
    cd, cm, cv = adamw("adamw_conv_w", _pad_rows(w["conv_w"], 8), _pad_rows(gcw, 8), _pad_rows(m["conv_w"], 8),
                       jnp.concatenate([v["conv_w"], jnp.ones((5, cs), F32)], axis=0))
    grad["conv_w"], delta["conv_w"], new_m["conv_w"], new_v["conv_w"] = gcw, cd[:3], cm[:3], cv[:3]
    return loss, dx, grad, delta, new_m, new_v


def kernel(x, p, ffn1_norm, ffn1_w_in, ffn1_w_out, mix_norm, w_mix_in, conv_w, w_conv_out, w_attn_out, w_mix_out, ffn2_norm, ffn2_w_in, ffn2_w_out, ple_norm, w_ple_gate, w_ple_proj, final_norm, loss_target, m_ffn1_norm, m_ffn1_w_in, m_ffn1_w_out, m_mix_norm, m_w_mix_in, m_conv_w, m_w_conv_out, m_w_attn_out, m_w_mix_out, m_ffn2_norm, m_ffn2_w_in, m_ffn2_w_out, m_ple_norm, m_w_ple_gate, m_w_ple_proj, m_final_norm, v_ffn1_norm, v_ffn1_w_in, v_ffn1_w_out, v_mix_norm, v_w_mix_in, v_conv_w, v_w_conv_out, v_w_attn_out, v_w_mix_out, v_ffn2_norm, v_ffn2_w_in, v_ffn2_w_out, v_ple_norm, v_w_ple_gate, v_w_ple_proj, v_final_norm):
    ws = (ffn1_norm, ffn1_w_in, ffn1_w_out, mix_norm, w_mix_in, conv_w, w_conv_out, w_attn_out, w_mix_out, ffn2_norm,
          ffn2_w_in, ffn2_w_out, ple_norm, w_ple_gate, w_ple_proj, final_norm)
    ms = (m_ffn1_norm, m_ffn1_w_in, m_ffn1_w_out, m_mix_norm, m_w_mix_in, m_conv_w, m_w_conv_out, m_w_attn_out,
          m_w_mix_out, m_ffn2_norm, m_ffn2_w_in, m_ffn2_w_out, m_ple_norm, m_w_ple_gate, m_w_ple_proj, m_final_norm)
    vs = (v_ffn1_norm, v_ffn1_w_in, v_ffn1_w_out, v_mix_norm, v_w_mix_in, v_conv_w, v_w_conv_out, v_w_attn_out,
          v_w_mix_out, v_ffn2_norm, v_ffn2_w_in, v_ffn2_w_out, v_ple_norm, v_w_ple_gate, v_w_ple_proj, v_final_norm)
    assert x.shape[0] == 1 and p.shape[:2] == (1, 1), "one sequence and one layer per device"

    def strip(a):
        return a[0] if a.ndim == 3 or (a.ndim == 2 and a.shape[0] == 1) else a

    w = {k: strip(a) for k, a in zip(WEIGHTS, ws)}
    m = {k: strip(a) for k, a in zip(WEIGHTS, ms)}
    v = {k: strip(a) for k, a in zip(WEIGHTS, vs)}
    loss, dx, grad, delta, new_m, new_v = _step(x[0], p[0, 0], loss_target[0], w, m, v)
    shapes = [a.shape for a in ws]
    outs = [loss, dx[None]]
    for res in (grad, delta, new_m, new_v):
        outs += [res[k].reshape(s) for k, s in zip(WEIGHTS, shapes)]
    return tuple(outs)
```

```python
import functools
import math

import jax
import jax.numpy as jnp
from jax import lax
from jax.experimental import pallas as pl
from jax.experimental.pallas import tpu as pltpu

F32 = jnp.float32
BF16 = jnp.bfloat16
MESH = pl.DeviceIdType.MESH
ANY = pl.BlockSpec(memory_space=pl.ANY)

HEAD_DIM = 128
NORM_EPS = 1e-6
N_CHIPS = 4
N_DEV = 8
BF16_ROWS = 16
VMEM_LIMIT = 56 * 1024 * 1024
STICK_EXIT = 110.0

ADAM_LR = 0.001
ADAM_B1 = 0.9
ADAM_B2 = 0.999
ADAM_EPS = 1e-08
ADAM_WD = 0.01
ADAM_STEP = 10

NN = (((1,), (0,)), ((), ()))
NT = (((1,), (1,)), ((), ()))
TN = (((0,), (0,)), ((), ()))


def _params(sem=None, **kw):
    if sem is not None:
        kw["dimension_semantics"] = sem
    return pltpu.CompilerParams(vmem_limit_bytes=VMEM_LIMIT, **kw)


def _pcall(body, **kw):
    return pl.pallas_call(body, **kw)


def _tile(n, pref, mult=8):
    best = None
    for d in range(mult, min(n, pref) + 1, mult):
        if n % d == 0:
            best = d
    return best if best is not None else n


def _dot(a, b, dims):
    return lax.dot_general(a, b, dims, preferred_element_type=F32)


def _mm(name, a, b, out_sds, grid, a_spec, b_spec, o_spec, dims, acc_shape, res=None, alpha=1.0):
    nk = grid[2]

    def body(*refs):
        if res is not None:
            a_ref, b_ref, r_ref, o_ref = refs[:4]
        else:
            a_ref, b_ref, o_ref = refs[:3]
            r_ref = None

        def finish(r):
            if alpha != 1.0:
                r = r * alpha
            if r_ref is not None:
                r = r_ref[...] + r
            o_ref[...] = r.astype(o_ref.dtype)

        part = _dot(a_ref[...].astype(BF16), b_ref[...].astype(BF16), dims)
        if nk == 1:
            finish(part)
        else:
            acc_ref = refs[-1]
            kk = pl.program_id(2)

            @pl.when(kk == 0)
            def _():
                acc_ref[...] = part

            @pl.when(kk > 0)
            def _():
                acc_ref[...] += part

            @pl.when(kk == nk - 1)
            def _():
                finish(acc_ref[...])

    in_specs = [a_spec, b_spec]
    args = [a, b]
    if res is not None:
        in_specs.append(o_spec)
        args.append(res)
    scratch = [] if nk == 1 else [pltpu.VMEM(acc_shape, F32)]
    return _pcall(body, name=name, out_shape=out_sds, grid=grid, in_specs=in_specs, out_specs=o_spec,
                  scratch_shapes=scratch,
                  compiler_params=_params(("parallel", "parallel", "arbitrary")))(*args)


def mm_nn(name, a, w, out_dtype, tm, res=None, alpha=1.0):
    m, k = a.shape
    n = w.shape[1]
    return _mm(name, a, w, jax.ShapeDtypeStruct((m, n), out_dtype), (m // tm, 1, 1),
               pl.BlockSpec((tm, k), lambda i, j, r: (i, 0)),
               pl.BlockSpec((k, n), lambda i, j, r: (0, 0)),
               pl.BlockSpec((tm, n), lambda i, j, r: (i, 0)), NN, None, res=res, alpha=alpha)


def mm_nn_stacked(name, a, w4, out_dtype, tm, tn, j0=0, nj=None):
    m, k = a.shape
    cs = w4.shape[2]
    per = cs // tn
    nj = N_CHIPS * per - j0 if nj is None else nj
    return _mm(name, a, w4, jax.ShapeDtypeStruct((m, nj * tn), out_dtype), (m // tm, nj, 1),
               pl.BlockSpec((tm, k), lambda i, j, r: (i, 0)),
               pl.BlockSpec((None, k, tn), lambda i, j, r: ((j + j0) // per, 0, (j + j0) % per)),
               pl.BlockSpec((tm, tn), lambda i, j, r: (i, j)), NN, None)


def mm_nt(name, dy, w, out_dtype, tm, tko):
    m, n = dy.shape
    k = w.shape[0]
    return _mm(name, dy, w, jax.ShapeDtypeStruct((m, k), out_dtype), (m // tm, k // tko, 1),
               pl.BlockSpec((tm, n), lambda i, j, r: (i, 0)),
               pl.BlockSpec((tko, n), lambda i, j, r: (j, 0)),
               pl.BlockSpec((tm, tko), lambda i, j, r: (i, j)), NT, None)


def mm_nt_stacked(name, dy, w4, out_dtype, tm, tn):
    m = dy.shape[0]
    k, cs = w4.shape[1], w4.shape[2]
    per = cs // tn
    return _mm(name, dy, w4, jax.ShapeDtypeStruct((m, k), out_dtype), (m // tm, 1, N_CHIPS * per),
               pl.BlockSpec((tm, tn), lambda i, j, r: (i, r)),
               pl.BlockSpec((None, k, tn), lambda i, j, r: (r // per, 0, r % per)),
               pl.BlockSpec((tm, k), lambda i, j, r: (i, 0)), NT, (tm, k))


def mm_tn_rows(name, xa, dy, tt):
    t, k = xa.shape
    n = dy.shape[1]
    tkr = k // 2
    return _mm(name, xa, dy, jax.ShapeDtypeStruct((k, n), BF16), (k // tkr, 1, t // tt),
               pl.BlockSpec((tt, tkr), lambda i, j, r: (r, i)),
               pl.BlockSpec((tt, n), lambda i, j, r: (r, 0)),
               pl.BlockSpec((tkr, n), lambda i, j, r: (i, 0)), TN, (tkr, n))


def mm_tn_whole(name, xa, dy, tt):
    t, k = xa.shape
    n = dy.shape[1]
    return _mm(name, xa, dy, jax.ShapeDtypeStruct((k, n), BF16), (1, 1, t // tt),
               pl.BlockSpec((tt, k), lambda i, j, r: (r, 0)),
               pl.BlockSpec((tt, n), lambda i, j, r: (r, 0)),
               pl.BlockSpec((k, n), lambda i, j, r: (0, 0)), TN, (k, n))


def mm_tn_cols(name, xa, dy, tt):
    t, k = xa.shape
    cs = dy.shape[1] // N_CHIPS
    pr = k // 2
    return _mm(name, xa, dy, jax.ShapeDtypeStruct((N_CHIPS, 2, pr, cs), BF16), (2, N_CHIPS, t // tt),
               pl.BlockSpec((tt, pr), lambda i, j, r: (r, i)),
               pl.BlockSpec((tt, cs), lambda i, j, r: (r, j)),
               pl.BlockSpec((None, None, pr, cs), lambda i, j, r: (j, i, 0, 0)), TN, (pr, cs))


def _rows(tt, w, col=0):
    return pl.BlockSpec((tt, w), lambda i: (i, col))


def _whole(shape):
    return pl.BlockSpec(shape, lambda i: (0,) * len(shape))


def _rstd(h):
    return lax.rsqrt(jnp.mean(h * h, axis=-1, keepdims=True) + NORM_EPS)


def rms_fwd(name, h, g, tt):
    t, d = h.shape

    def body(h_ref, g_ref, o_ref):
        hv = h_ref[...]
        o_ref[...] = (hv * _rstd(hv) * g_ref[...]).astype(o_ref.dtype)

    return _pcall(body, name=name, out_shape=jax.ShapeDtypeStruct((t, d), BF16), grid=(t // tt,),
                  in_specs=[_rows(tt, d), _whole((1, d))], out_specs=_rows(tt, d),
                  compiler_params=_params(("parallel",)))(h, g)


def rms_bwd(name, h, g, dn, dres, alpha, tt):
    t, d = h.shape

    def body(h_ref, g_ref, dn_ref, dr_ref, dh_ref, dhb_ref, dg_ref):
        hv = h_ref[...]
        hn = hv * _rstd(hv)
        dnv = dn_ref[...]
        gy = dnv * g_ref[...]
        dh = dr_ref[...] + _rstd(hv) * (gy - hn * jnp.mean(gy * hn, axis=-1, keepdims=True))
        dh_ref[...] = dh
        dhb_ref[...] = (alpha * dh).astype(BF16)

        @pl.when(pl.program_id(0) == 0)
        def _():
            dg_ref[...] = jnp.zeros_like(dg_ref)

        dg_ref[...] += jnp.sum(dnv * hn, axis=0, keepdims=True)

    return _pcall(body, name=name,
                  out_shape=(jax.ShapeDtypeStruct((t, d), F32), jax.ShapeDtypeStruct((t, d), BF16),
                             jax.ShapeDtypeStruct((1, d), F32)),
                  grid=(t // tt,),
                  in_specs=[_rows(tt, d), _whole((1, d)), _rows(tt, d), _rows(tt, d)],
                  out_specs=(_rows(tt, d), _rows(tt, d), _whole((1, d))),
                  compiler_params=_params(("arbitrary",)))(h, g, dn, dres)


def swiglu_fwd(name, a, tt):
    t, f2 = a.shape
    f = f2 // 2

    def body(a_ref, o_ref):
        gate = a_ref[:, :f]
        up = a_ref[:, f:]
        o_ref[...] = (gate * jax.nn.sigmoid(gate) * up).astype(o_ref.dtype)

    return _pcall(body, name=name, out_shape=jax.ShapeDtypeStruct((t, f), BF16), grid=(t // tt,),
                  in_specs=[_rows(tt, f2)], out_specs=_rows(tt, f),
                  compiler_params=_params(("parallel",)))(a)


def swiglu_bwd(name, a, ds, tt):
    t, f2 = a.shape
    f = f2 // 2

    def body(a_ref, ds_ref, o_ref):
        gate = a_ref[:, :f]
        up = a_ref[:, f:]
        dsv = ds_ref[...]
        sg = jax.nn.sigmoid(gate)
        o_ref[:, :f] = (dsv * up * sg * (1.0 + gate * (1.0 - sg))).astype(o_ref.dtype)
        o_ref[:, f:] = (dsv * gate * sg).astype(o_ref.dtype)

    return _pcall(body, name=name, out_shape=jax.ShapeDtypeStruct((t, f2), BF16), grid=(t // tt,),
                  in_specs=[_rows(tt, f2), _rows(tt, f)], out_specs=_rows(tt, f2),
                  compiler_params=_params(("parallel",)))(a, ds)


def gate_fwd(name, gates, yc, ya, tt):
    t, d = yc.shape

    def body(g_ref, yc_ref, ya_ref, o_ref):
        o_ref[...] = (jax.nn.sigmoid(g_ref[:, :d]) * yc_ref[...]
                      + jax.nn.sigmoid(g_ref[:, d:]) * ya_ref[...]).astype(o_ref.dtype)

    return _pcall(body, name=name, out_shape=jax.ShapeDtypeStruct((t, d), BF16), grid=(t // tt,),
                  in_specs=[_rows(tt, 2 * d), _rows(tt, d), _rows(tt, d)], out_specs=_rows(tt, d),
                  compiler_params=_params(("parallel",)))(gates, yc, ya)


def gate_bwd(name, dm, gates, yc, ya, tt):
    t, d = yc.shape

    def body(dm_ref, g_ref, yc_ref, ya_ref, dyc_ref, dya_ref, dg_ref):
        dmv = dm_ref[...]
        sc = jax.nn.sigmoid(g_ref[:, :d])
        sa = jax.nn.sigmoid(g_ref[:, d:])
        dyc_ref[...] = (dmv * sc).astype(BF16)
        dya_ref[...] = (dmv * sa).astype(BF16)
        dg_ref[:, :d] = (dmv * yc_ref[...] * sc * (1.0 - sc)).astype(BF16)
        dg_ref[:, d:] = (dmv * ya_ref[...] * sa * (1.0 - sa)).astype(BF16)

    return _pcall(body, name=name,
                  out_shape=(jax.ShapeDtypeStruct((t, d), BF16), jax.ShapeDtypeStruct((t, d), BF16),
                             jax.ShapeDtypeStruct((t, 2 * d), BF16)),
                  grid=(t // tt,),
                  in_specs=[_rows(tt, d), _rows(tt, 2 * d), _rows(tt, d), _rows(tt, d)],
                  out_specs=(_rows(tt, d), _rows(tt, d), _rows(tt, 2 * d)),
                  compiler_params=_params(("parallel",)))(dm, gates, yc, ya)


def _shift_down(cur, prev8, s):
    tt = cur.shape[0]
    rolled = pltpu.roll(cur, s, 0)
    row8 = lax.broadcasted_iota(jnp.int32, prev8.shape, 0)
    first8 = jnp.where(row8 < s, pltpu.roll(prev8, s, 0), rolled[:8])
    return jnp.concatenate([first8, rolled[8:]], axis=0) if tt > 8 else first8


def _shift_up(cur, next8, s):
    tt = cur.shape[0]
    rolled = pltpu.roll(cur, tt - s, 0)
    row8 = lax.broadcasted_iota(jnp.int32, next8.shape, 0)
    last8 = jnp.where(row8 >= 8 - s, pltpu.roll(next8, 8 - s, 0), rolled[tt - 8:])
    return jnp.concatenate([rolled[:tt - 8], last8], axis=0) if tt > 8 else last8


def _prev8(tt, d, col):
    return pl.BlockSpec((8, d), lambda i: (jnp.maximum(i * (tt // 8) - 1, 0), col))


def _next8(tt, d, col, t):
    return pl.BlockSpec((8, d), lambda i: (jnp.minimum((i + 1) * (tt // 8), t // 8 - 1), col))


def conv_fwd(name, cbx, cw8, tt):
    t, d3 = cbx.shape
    d = d3 // 3

    def body(cb_ref, cc_ref, cx_ref, pc_ref, px_ref, w_ref, o_ref):
        has_prev = (pl.program_id(0) > 0).astype(F32)
        cc = cc_ref[...] * cx_ref[...]
        prev = pc_ref[...] * px_ref[...] * has_prev
        w = w_ref[...]
        conv = w[0:1] * _shift_down(cc, prev, 2) + w[1:2] * _shift_down(cc, prev, 1) + w[2:3] * cc
        o_ref[...] = (cb_ref[...] * conv).astype(o_ref.dtype)

    return _pcall(body, name=name, out_shape=jax.ShapeDtypeStruct((t, d), BF16), grid=(t // tt,),
                  in_specs=[_rows(tt, d, 0), _rows(tt, d, 1), _rows(tt, d, 2), _prev8(tt, d, 1), _prev8(tt, d, 2),
                            _whole((8, d))],
                  out_specs=_rows(tt, d), compiler_params=_params(("parallel",)))(cbx, cbx, cbx, cbx, cbx, cw8)


def conv_bwd(name, dyc, cbx, cw8, tt):
    t, d3 = cbx.shape
    d = d3 // 3
    n = t // tt

    def body(dy_ref, cb_ref, cc_ref, cx_ref, pc_ref, px_ref, ndy_ref, ncb_ref, w_ref, o_ref, dw_ref):
        i = pl.program_id(0)
        has_prev = (i > 0).astype(F32)
        has_next = (i < n - 1).astype(F32)
        cb = cb_ref[...]
        cc = cc_ref[...] * cx_ref[...]
        prev = pc_ref[...] * px_ref[...] * has_prev
        w = w_ref[...]
        cc1 = _shift_down(cc, prev, 1)
        cc2 = _shift_down(cc, prev, 2)
        conv = w[0:1] * cc2 + w[1:2] * cc1 + w[2:3] * cc
        dyv = dy_ref[...]
        dconv = dyv * cb
        dnext = ndy_ref[...] * ncb_ref[...] * has_next
        dcc = w[2:3] * dconv + w[1:2] * _shift_up(dconv, dnext, 1) + w[0:1] * _shift_up(dconv, dnext, 2)
        o_ref[:, :d] = (dyv * conv).astype(BF16)
        o_ref[:, d:2 * d] = (dcc * cx_ref[...]).astype(BF16)
        o_ref[:, 2 * d:] = (dcc * cc_ref[...]).astype(BF16)

        @pl.when(i == 0)
        def _():
            dw_ref[...] = jnp.zeros_like(dw_ref)

        dw_ref[0:1, :] += jnp.sum(dconv * cc2, axis=0, keepdims=True)
        dw_ref[1:2, :] += jnp.sum(dconv * cc1, axis=0, keepdims=True)
        dw_ref[2:3, :] += jnp.sum(dconv * cc, axis=0, keepdims=True)

    return _pcall(body, name=name,
                  out_shape=(jax.ShapeDtypeStruct((t, d3), BF16), jax.ShapeDtypeStruct((8, d), F32)),
                  grid=(n,),
                  in_specs=[_rows(tt, d), _rows(tt, d, 0), _rows(tt, d, 1), _rows(tt, d, 2),
                            _prev8(tt, d, 1), _prev8(tt, d, 2), _next8(tt, d, 0, t), _next8(tt, d, 0, t),
                            _whole((8, d))],
                  out_specs=(_rows(tt, d3), _whole((8, d))),
                  compiler_params=_params(("arbitrary",)))(dyc, cbx, cbx, cbx, cbx, cbx, dyc, cbx, cw8)


def tail(name, h3, zg, pp, tgt, gf, tt):
    t, d = h3.shape

    def body(h_ref, zg_ref, pp_ref, tg_ref, gf_ref, dh_ref, dpp_ref, dzg_ref, dgf_ref, loss_ref):
        pg = jax.nn.sigmoid(zg_ref[...])
        ppv = pp_ref[...]
        h4 = h_ref[...] + pg * ppv
        r4 = _rstd(h4)
        hn = h4 * r4
        gfv = gf_ref[...]
        err = hn * gfv - tg_ref[...]
        dy = err * (1.0 / d)
        gy = dy * gfv
        dh4 = r4 * (gy - hn * jnp.mean(gy * hn, axis=-1, keepdims=True))
        dh_ref[...] = dh4
        dpp_ref[...] = (dh4 * pg).astype(BF16)
        dzg_ref[...] = (dh4 * ppv * pg * (1.0 - pg)).astype(BF16)

        @pl.when(pl.program_id(0) == 0)
        def _():
            dgf_ref[...] = jnp.zeros_like(dgf_ref)
            loss_ref[...] = jnp.zeros_like(loss_ref)

        dgf_ref[...] += jnp.sum(dy * hn, axis=0, keepdims=True)
        tok = jnp.mean(err * err, axis=-1, keepdims=True)
        loss_ref[...] += 0.5 * jnp.sum(tok, axis=0, keepdims=True) * jnp.ones((1, loss_ref.shape[1]), F32)

    return _pcall(body, name=name,
                  out_shape=(jax.ShapeDtypeStruct((t, d), F32), jax.ShapeDtypeStruct((t, d), BF16),
                             jax.ShapeDtypeStruct((t, d), BF16), jax.ShapeDtypeStruct((1, d), F32),
                             jax.ShapeDtypeStruct((1, d), F32)),
                  grid=(t // tt,),
                  in_specs=[_rows(tt, d)] * 4 + [_whole((1, d))],
                  out_specs=(_rows(tt, d), _rows(tt, d), _rows(tt, d), _whole((1, d)), _whole((1, d))),
                  compiler_params=_params(("arbitrary",)))(h3, zg, pp, tgt, gf)


def _sb_block(q, kj, row, col, upper, c, diag):
    z = _dot(q, kj, NT) * (1.0 / math.sqrt(HEAD_DIM))
    lg = -(jnp.maximum(z, 0.0) + jnp.log(1.0 + jnp.exp(-jnp.abs(z))))
    if diag:
        lg = jnp.where(col < row, lg, 0.0)
    hi = lg.astype(BF16)
    lo = (lg - hi.astype(F32)).astype(BF16)
    cum = _dot(hi, upper, NN) + _dot(lo, upper, NN)
    log_a = z + cum + c
    if diag:
        log_a = jnp.where(col < row, log_a, -1e30)
    return z, jnp.exp(log_a), c + cum[:, 0:1]


def attn_fwd(name, qkv, tq):
    t, d3 = qkv.shape
    d = d3 // 3
    nh = d // HEAD_DIM
    nq = t // tq

    def body(q_ref, k_ref, v_ref, o_ref):
        i = pl.program_id(1)
        q = q_ref[...]
        row = lax.broadcasted_iota(jnp.int32, (tq, tq), 0)
        col = lax.broadcasted_iota(jnp.int32, (tq, tq), 1)
        upper = (row >= col).astype(BF16)

        def block(j, c, acc, diag):
            rows = pl.ds(pl.multiple_of(j * tq, tq), tq)
            _, a, c = _sb_block(q, k_ref[rows, :], row, col, upper, c, diag)
            return c, acc + _dot(a.astype(BF16), v_ref[rows, :], NN)

        c, acc = block(i, jnp.zeros((tq, 1), F32), jnp.zeros((tq, HEAD_DIM), F32), True)

        def cond(st):
            return jnp.logical_and(st[0] >= 0, jnp.max(st[1]) > -STICK_EXIT)

        def step(st):
            c2, acc2 = block(st[0], st[1], st[2], False)
            return st[0] - 1, c2, acc2

        _, _, acc = lax.while_loop(cond, step, (i - 1, c, acc))
        o_ref[...] = acc.astype(o_ref.dtype)

    return _pcall(body, name=name, out_shape=jax.ShapeDtypeStruct((t, d), BF16), grid=(nh, nq),
                  in_specs=[pl.BlockSpec((tq, HEAD_DIM), lambda h, i: (i, h)),
                            pl.BlockSpec((t, HEAD_DIM), lambda h, i: (0, nh + h)),
                            pl.BlockSpec((t, HEAD_DIM), lambda h, i: (0, 2 * nh + h))],
                  out_specs=pl.BlockSpec((tq, HEAD_DIM), lambda h, i: (i, h)),
                  compiler_params=_params(("parallel", "arbitrary")))(qkv, qkv, qkv)


def attn_bwd(name, qkv, do, tq):
    t, d3 = qkv.shape
    d = d3 // 3
    nh = d // HEAD_DIM
    nq = t // tq
    scale = 1.0 / math.sqrt(HEAD_DIM)

    def body(q_ref, k_ref, v_ref, do_ref, dq_ref, dk_ref, dv_ref, dk_acc, dv_acc, g_buf, z_buf):
        i = pl.program_id(1)

        @pl.when(i == 0)
        def _():
            dk_acc[...] = jnp.zeros_like(dk_acc)
            dv_acc[...] = jnp.zeros_like(dv_acc)

        q = q_ref[...]
        dov = do_ref[...]
        row = lax.broadcasted_iota(jnp.int32, (tq, tq), 0)
        col = lax.broadcasted_iota(jnp.int32, (tq, tq), 1)
        upper = (row >= col).astype(BF16)
        lower = (row <= col).astype(BF16)

        def first(j, c, diag):
            rows = pl.ds(pl.multiple_of(j * tq, tq), tq)
            z, a, c = _sb_block(q, k_ref[rows, :], row, col, upper, c, diag)
            g = _dot(dov, v_ref[rows, :], NT) * a
            g_buf[i - j] = g
            z_buf[i - j] = z
            dv_acc[rows, :] += _dot(a.astype(BF16), dov, TN)
            return c

        c = first(i, jnp.zeros((tq, 1), F32), True)

        def cond(st):
            return jnp.logical_and(st[0] >= 0, jnp.max(st[1]) > -STICK_EXIT)

        def step(st):
            return st[0] - 1, first(st[0], st[1], False)

        j_stop, _ = lax.while_loop(cond, step, (i - 1, c))

        def second(j, st):
            run, dq = st
            rows = pl.ds(pl.multiple_of(j * tq, tq), tq)
            g = g_buf[i - j]
            hi = g.astype(BF16)
            lo = (g - hi.astype(F32)).astype(BF16)
            p = run + _dot(hi, lower, NN) + _dot(lo, lower, NN)
            dz = g - jax.nn.sigmoid(z_buf[i - j]) * p
            dz = jnp.where(jnp.logical_or(col < row, j < i), dz, 0.0).astype(BF16)
            dk_acc[rows, :] += _dot(dz, q, TN)
            return run + jnp.sum(g, axis=1, keepdims=True), dq + _dot(dz, k_ref[rows, :], NN)

        _, dq = lax.fori_loop(j_stop + 1, i + 1, second,
                              (jnp.zeros((tq, 1), F32), jnp.zeros((tq, HEAD_DIM), F32)))
        dq_ref[...] = (dq * scale).astype(BF16)

        @pl.when(i == nq - 1)
        def _():
            dk_ref[...] = (dk_acc[...] * scale).astype(BF16)
            dv_ref[...] = dv_acc[...].astype(BF16)

    blk = pl.BlockSpec((tq, HEAD_DIM), lambda h, i: (i, h))
    col_h = pl.BlockSpec((t, HEAD_DIM), lambda h, i: (0, h))
    out = jax.ShapeDtypeStruct((t, d), BF16)
    return _pcall(body, name=name, out_shape=(out, out, out), grid=(nh, nq),
                  in_specs=[blk,
                            pl.BlockSpec((t, HEAD_DIM), lambda h, i: (0, nh + h)),
                            pl.BlockSpec((t, HEAD_DIM), lambda h, i: (0, 2 * nh + h)),
                            blk],
                  out_specs=(blk, col_h, col_h),
                  scratch_shapes=[pltpu.VMEM((t, HEAD_DIM), F32), pltpu.VMEM((t, HEAD_DIM), F32),
                                  pltpu.VMEM((nq, tq, tq), F32), pltpu.VMEM((nq, tq, tq), F32)],
                  compiler_params=_params(("parallel", "arbitrary")))(qkv, qkv, qkv, do)


def _place():
    x, y, c = lax.axis_index("x"), lax.axis_index("y"), lax.axis_index("c")
    chips = [(1 - x, y), (x, 1 - y), (1 - x, 1 - y)]
    return x, y, c, chips


def _remote(src, dst, send_sem, recv_sem, dev):
    return pltpu.make_async_remote_copy(src_ref=src, dst_ref=dst, send_sem=send_sem, recv_sem=recv_sem,
                                        device_id=dev, device_id_type=MESH)


def place_shard(name, w, chip):
    r, cdim = w.shape
    tr = _tile(r, max(BF16_ROWS, (1 << 19) // cdim), BF16_ROWS)

    def body(chip_ref, w_ref, o_ref):
        o_ref[...] = w_ref[...].astype(BF16)

    spec = pltpu.PrefetchScalarGridSpec(
        num_scalar_prefetch=1, grid=(r // tr,),
        in_specs=[pl.BlockSpec((tr, cdim), lambda i, s: (i, 0))],
        out_specs=pl.BlockSpec((None, tr, cdim), lambda i, s: (s[0], i, 0)))
    return _pcall(body, name=name, out_shape=jax.ShapeDtypeStruct((N_CHIPS, r, cdim), BF16), grid_spec=spec,
                  compiler_params=_params(("parallel",)))(chip, w)


def gather_weights(bufs):
    n = len(bufs)

    def body(*refs):
        out = refs[n:2 * n]
        isend, irecv, dsend, drecv = refs[2 * n:]
        x, y, c, chips = _place()
        me = 2 * x + y
        sib = (x, y, 1 - c)
        first, passed = [], []
        for w in range(n):
            pr = out[w].shape[1] // 2
            half = pl.ds(pl.multiple_of(c * pr, BF16_ROWS), pr)
            for j, (cx, cy) in enumerate(chips):
                mine = out[w].at[me, half]
                cp = _remote(mine, mine, isend.at[3 * w + j], irecv.at[3 * w + j], (cx, cy, c))
                cp.start()
                first.append(cp)
        for w in range(n):
            pr = out[w].shape[1] // 2
            half = pl.ds(pl.multiple_of(c * pr, BF16_ROWS), pr)
            for j, (cx, cy) in enumerate(chips):
                landed = out[w].at[2 * cx + cy, half]
                _remote(landed, landed, isend.at[3 * w + j], irecv.at[3 * w + j], sib).wait_recv()
                fw = _remote(landed, landed, dsend.at[3 * w + j], drecv.at[3 * w + j], sib)
                fw.start()
                passed.append(fw)
        for w in range(n):
            pr = out[w].shape[1] // 2
            other = pl.ds(pl.multiple_of((1 - c) * pr, BF16_ROWS), pr)
            for j, (cx, cy) in enumerate(chips):
                landed = out[w].at[2 * cx + cy, other]
                _remote(landed, landed, dsend.at[3 * w + j], drecv.at[3 * w + j], sib).wait_recv()
        for cp in first + passed:
            cp.wait_send()

    return _pcall(body, name="gather_weights",
                  out_shape=[jax.ShapeDtypeStruct(s.shape, s.dtype) for s in bufs],
                  in_specs=[ANY] * n, out_specs=[ANY] * n, input_output_aliases={w: w for w in range(n)},
                  scratch_shapes=[pltpu.SemaphoreType.DMA((3 * n,))] * 4,
                  compiler_params=_params())(*bufs)


def exchange_siblings(pieces):
    n = len(pieces)

    def body(*refs):
        src, out = refs[:n], refs[n:2 * n]
        send_sem, recv_sem = refs[2 * n:]
        x, y, c, _ = _place()
        sib = (x, y, 1 - c)
        started = []
        for w in range(n):
            for k in range(N_CHIPS):
                s = N_CHIPS * w + k
                cp = _remote(src[w].at[k, 1 - c], out[w].at[k], send_sem.at[s], recv_sem.at[s], sib)
                cp.start()
                started.append(cp)
        for cp in started:
            cp.wait()

    return _pcall(body, name="exchange_siblings",
                  out_shape=[jax.ShapeDtypeStruct((N_CHIPS,) + s.shape[2:], s.dtype) for s in pieces],
                  in_specs=[ANY] * n, out_specs=[ANY] * n,
                  scratch_shapes=[pltpu.SemaphoreType.DMA((N_CHIPS * n,))] * 2,
                  compiler_params=_params())(*pieces)


def scatter_chips(parts):
    n = len(parts)

    def body(*refs):
        src, out = refs[:n], refs[n:2 * n]
        send_sem, recv_sem = refs[2 * n:]
        x, y, c, chips = _place()
        started = []
        for w in range(n):
            for j, (cx, cy) in enumerate(chips):
                cp = _remote(src[w].at[2 * cx + cy], out[w].at[j], send_sem.at[3 * w + j], recv_sem.at[3 * w + j],
                             (cx, cy, c))
                cp.start()
                started.append(cp)
        for cp in started:
            cp.wait()

    return _pcall(body, name="scatter_chips",
                  out_shape=[jax.ShapeDtypeStruct((3,) + s.shape[1:], s.dtype) for s in parts],
                  in_specs=[ANY] * n, out_specs=[ANY] * n,
                  scratch_shapes=[pltpu.SemaphoreType.DMA((3 * n,))] * 2,
                  compiler_params=_params())(*parts)


def share_halves(halves):
    n = len(halves)

    def body(*refs):
        buf = refs[n:2 * n]
        send_sem, recv_sem = refs[2 * n:]
        x, y, c, _ = _place()
        started = []
        for w in range(n):
            cp = _remote(buf[w].at[c], buf[w].at[c], send_sem.at[w], recv_sem.at[w], (x, y, 1 - c))
            cp.start()
            started.append(cp)
        for w in range(n):
            landed = buf[w].at[1 - c]
            _remote(landed, landed, send_sem.at[w], recv_sem.at[w], (x, y, 1 - c)).wait_recv()
        for cp in started:
            cp.wait_send()

    return _pcall(body, name="share_halves",
                  out_shape=[jax.ShapeDtypeStruct(s.shape, s.dtype) for s in halves],
                  in_specs=[ANY] * n, out_specs=[ANY] * n, input_output_aliases={w: w for w in range(n)},
                  scratch_shapes=[pltpu.SemaphoreType.DMA((n,))] * 2,
                  compiler_params=_params())(*halves)


def gather_small(name, blk, reduce):
    r, cdim = blk.shape

    def body(in_ref, out_ref, *rest):
        if reduce:
            buf, send_sem, recv_sem = rest
        else:
            buf = out_ref
            send_sem, recv_sem = rest
        x, y, c, _ = _place()
        me = 4 * x + 2 * y + c
        buf[me] = in_ref[...]
        peers = []
        for dx in range(2):
            for dy in range(2):
                for dc in range(2):
                    if dx or dy or dc:
                        peers.append((dx, dy, dc))
        copies = []
        for s, (dx, dy, dc) in enumerate(peers):
            cp = _remote(in_ref, buf.at[me], send_sem.at[s], recv_sem.at[s],
                         ((1 - x if dx else x), (1 - y if dy else y), (1 - c if dc else c)))
            cp.start()
            copies.append(cp)
        for s, (dx, dy, dc) in enumerate(peers):
            px, py, pc_ = (1 - x if dx else x), (1 - y if dy else y), (1 - c if dc else c)
            landed = buf.at[4 * px + 2 * py + pc_]
            _remote(landed, landed, send_sem.at[s], recv_sem.at[s], (x, y, c)).wait_recv()
        for cp in copies:
            cp.wait_send()
        if reduce:
            tot = buf[0]
            for s in range(1, N_DEV):
                tot = tot + buf[s]
            out_ref[...] = tot

    vm = pl.BlockSpec(memory_space=pltpu.VMEM)
    out_shape = jax.ShapeDtypeStruct((r, cdim) if reduce else (N_DEV, r, cdim), F32)
    scratch = ([pltpu.VMEM((N_DEV, r, cdim), F32)] if reduce else []) + [pltpu.SemaphoreType.DMA((N_DEV - 1,))] * 2
    return _pcall(body, name=name, out_shape=out_shape, in_specs=[vm], out_specs=vm, scratch_shapes=scratch,
                  compiler_params=_params())(blk)


def sum_cores(name, own, got, place):
    _, _, pr, pc = own.shape
    tr = _tile(pr, max(BF16_ROWS, (1 << 19) // pc), BF16_ROWS)

    def body(place_ref, own_ref, got_ref, o_ref):
        o_ref[...] = (own_ref[...].astype(F32) + got_ref[...].astype(F32)).astype(o_ref.dtype)

    spec = pltpu.PrefetchScalarGridSpec(
        num_scalar_prefetch=1, grid=(N_CHIPS, pr // tr),
        in_specs=[pl.BlockSpec((None, None, tr, pc), lambda k, i, s: (k, s[1], i, 0)),
                  pl.BlockSpec((None, tr, pc), lambda k, i, s: (k, i, 0))],
        out_specs=pl.BlockSpec((None, tr, pc), lambda k, i, s: (k, i, 0)))
    return _pcall(body, name=name, out_shape=jax.ShapeDtypeStruct((N_CHIPS, pr, pc), BF16), grid_spec=spec,
                  compiler_params=_params(("parallel", "parallel")))(place, own, got)


def sum_chips(name, part, got, place):
    _, pr, pc = part.shape
    tr = _tile(pr, max(BF16_ROWS, (1 << 18) // pc), BF16_ROWS)

    def body(place_ref, part_ref, got_ref, o_ref):
        tot = part_ref[...].astype(F32)
        for j in range(3):
            tot = tot + got_ref[j].astype(F32)
        o_ref[...] = tot

    spec = pltpu.PrefetchScalarGridSpec(
        num_scalar_prefetch=1, grid=(pr // tr,),
        in_specs=[pl.BlockSpec((None, tr, pc), lambda i, s: (s[0], i, 0)),
                  pl.BlockSpec((3, tr, pc), lambda i, s: (0, i, 0))],
        out_specs=pl.BlockSpec((None, tr, pc), lambda i, s: (s[1], i, 0)))
    return _pcall(body, name=name, out_shape=jax.ShapeDtypeStruct((2, pr, pc), F32), grid_spec=spec,
                  compiler_params=_params(("parallel",)))(place, part, got)


def adamw(name, w, g, m, v):
    rows, cols = w.shape
    tr = _tile(rows, max(8, (1 << 18) // cols))
    c1 = 1.0 / (1.0 - ADAM_B1 ** ADAM_STEP)
    c2 = 1.0 / (1.0 - ADAM_B2 ** ADAM_STEP)

    def body(w_ref, g_ref, m_ref, v_ref, d_ref, nm_ref, nv_ref):
        gv = g_ref[...]
        nm = ADAM_B1 * m_ref[...] + (1.0 - ADAM_B1) * gv
        nv = ADAM_B2 * v_ref[...] + (1.0 - ADAM_B2) * (gv * gv)
        nm_ref[...] = nm
        nv_ref[...] = nv
        d_ref[...] = -ADAM_LR * ((nm * c1) / (jnp.sqrt(nv * c2) + ADAM_EPS) + ADAM_WD * w_ref[...])

    spec = pl.BlockSpec((tr, cols), lambda i: (i, 0))
    sds = jax.ShapeDtypeStruct((rows, cols), F32)
    return _pcall(body, name=name, out_shape=(sds, sds, sds), grid=(rows // tr,),
                  in_specs=[spec] * 4, out_specs=(spec, spec, spec),
                  compiler_params=_params(("parallel",)))(w, g, m, v)


MATS = ["ffn1_w_in", "ffn1_w_out", "w_mix_in", "w_conv_out", "w_attn_out", "w_mix_out", "ffn2_w_in", "ffn2_w_out",
        "w_ple_gate", "w_ple_proj"]
COL_SHARDED = {"ffn1_w_in", "w_mix_in", "ffn2_w_in", "w_ple_proj"}
NORMS = ["ffn1_norm", "mix_norm", "ffn2_norm", "ple_norm", "final_norm"]
WEIGHTS = ["ffn1_norm", "ffn1_w_in", "ffn1_w_out", "mix_norm", "w_mix_in", "conv_w", "w_conv_out", "w_attn_out",
           "w_mix_out", "ffn2_norm", "ffn2_w_in", "ffn2_w_out", "ple_norm", "w_ple_gate", "w_ple_proj", "final_norm"]


def _pad_rows(a, rows):
    return jnp.concatenate([a, jnp.zeros((rows - a.shape[0],) + a.shape[1:], a.dtype)], axis=0)


def _step(x, p, tgt, w, m, v):
    t, d = x.shape
    tt = _tile(t, 256)
    tm = _tile(t, 512)
    tq = _tile(t, 256)

    chip = 2 * lax.axis_index("x") + lax.axis_index("y")
    place = jnp.stack([chip, lax.axis_index("c")]).astype(jnp.int32)

    full = dict(zip(MATS, gather_weights([place_shard("place_" + k, w[k], place) for k in MATS])))
    for k in MATS:
        if k not in COL_SHARDED:
            full[k] = full[k].reshape(-1, full[k].shape[2])
    wpp = full["w_ple_proj"]
    wpp = jnp.transpose(wpp, (1, 0, 2)).reshape(wpp.shape[1], -1)
    cw_all = gather_small("gather_conv_w", _pad_rows(w["conv_w"], 8), False)
    cw8 = jnp.concatenate([cw_all[2 * k] for k in range(N_CHIPS)], axis=1)
    g1, gm, g2, gp, gf = (w[k].reshape(1, d) for k in NORMS)

    def ffn_fwd(tag, h, g, w_in, w_out):
        n = rms_fwd(tag + "_norm", h, g, tt)
        a = mm_nn_stacked(tag + "_in", n, w_in, F32, tm, w_in.shape[2])
        s = swiglu_fwd(tag + "_act", a, tt)
        return n, a, s, mm_nn(tag + "_out", s, w_out, F32, tm, res=h, alpha=0.5)

    def ffn_bwd(tag, h, g, w_in, w_out, n, a, s, df, dh):
        dw_out = mm_tn_rows(tag + "_dwout", s, df, tm)
        ds = mm_nt(tag + "_ds", df, w_out, F32, tm, w_out.shape[0] // 2)
        da = swiglu_bwd(tag + "_dact", a, ds, tt)
        dw_in = mm_tn_cols(tag + "_dwin", n, da, tm)
        dn = mm_nt_stacked(tag + "_dn", da, w_in, F32, tm, w_in.shape[2])
        return dn, dw_in, dw_out

    n1, a1, s1, h1 = ffn_fwd("ffn1", x, g1, full["ffn1_w_in"], full["ffn1_w_out"])
    u = rms_fwd("mix_norm", h1, gm, tt)
    wmix = full["w_mix_in"]
    cbx = mm_nn_stacked("mix_in_conv", u, wmix, F32, tm, d, 0, 3)
    qkv = mm_nn_stacked("mix_in_qkv", u, wmix, BF16, tm, d, 3, 3)
    gates = mm_nn_stacked("mix_in_gates", u, wmix, F32, tm, d, 6, 2)
    ycin = conv_fwd("conv", cbx, cw8, tt)
    y_conv = mm_nn("conv_out", ycin, full["w_conv_out"], F32, tm)
    o = attn_fwd("attn", qkv, tq)
    y_attn = mm_nn("attn_out", o, full["w_attn_out"], F32, tm)
    merged = gate_fwd("merge", gates, y_conv, y_attn, tt)
    h2 = mm_nn("mix_out", merged, full["w_mix_out"], F32, tm, res=h1, alpha=1.0)
    n2, a2, s2, h3 = ffn_fwd("ffn2", h2, g2, full["ffn2_w_in"], full["ffn2_w_out"])
    npl = rms_fwd("ple_norm", h3, gp, tt)
    zg = mm_nn("ple_gate", npl, full["w_ple_gate"], F32, tm)
    pp = mm_nn("ple_proj", p, wpp, F32, tm)

    dh4, dpp, dzg, dgf, loss_row = tail("tail", h3, zg, pp, tgt, gf, tt)
    pieces = {}
    dwpp = mm_tn_whole("ple_proj_dw", p, dpp, tm)
    pieces["w_ple_proj"] = jnp.transpose(dwpp.reshape(2, p.shape[1] // 2, N_CHIPS, d // N_CHIPS), (2, 0, 1, 3))
    pieces["w_ple_gate"] = mm_tn_rows("ple_gate_dw", npl, dzg, tm)
    dnp = mm_nt("ple_gate_dx", dzg, full["w_ple_gate"], F32, tm, d)
    dh3, df2, dgp = rms_bwd("ple_norm_bwd", h3, gp, dnp, dh4, 0.5, tt)
    dn2, pieces["ffn2_w_in"], pieces["ffn2_w_out"] = ffn_bwd(
        "ffn2", h2, g2, full["ffn2_w_in"], full["ffn2_w_out"], n2, a2, s2, df2, dh3)
    dh2, dh2b, dg2 = rms_bwd("ffn2_norm_bwd", h2, g2, dn2, dh3, 1.0, tt)
    pieces["w_mix_out"] = mm_tn_rows("mix_out_dw", merged, dh2b, tm)
    dmerged = mm_nt("mix_out_dx", dh2b, full["w_mix_out"], F32, tm, d)
    dyc, dya, dgates = gate_bwd("merge_bwd", dmerged, gates, y_conv, y_attn, tt)
    pieces["w_conv_out"] = mm_tn_rows("conv_out_dw", ycin, dyc, tm)
    dycin = mm_nt("conv_out_dx", dyc, full["w_conv_out"], F32, tm, d)
    dcbx, dcw8 = conv_bwd("conv_bwd", dycin, cbx, cw8, tt)
    pieces["w_attn_out"] = mm_tn_rows("attn_out_dw", o, dya, tm)
    do = mm_nt("attn_out_dx", dya, full["w_attn_out"], BF16, tm, d)
    dq, dk, dv = attn_bwd("attn_bwd", qkv, do, tq)
    dmix = jnp.concatenate([dcbx, dq, dk, dv, dgates], axis=1)
    pieces["w_mix_in"] = mm_tn_cols("mix_in_dw", u, dmix, tm)
    du = mm_nt_stacked("mix_in_dx", dmix, wmix, F32, tm, d)
    dh1, df1, dgm = rms_bwd("mix_norm_bwd", h1, gm, du, dh2, 0.5, tt)
    dn1, pieces["ffn1_w_in"], pieces["ffn1_w_out"] = ffn_bwd(
        "ffn1", x, g1, full["ffn1_w_in"], full["ffn1_w_out"], n1, a1, s1, df1, dh1)
    dx, _, dg1 = rms_bwd("ffn1_norm_bwd", x, g1, dn1, dh1, 1.0, tt)

    pcs = []
    for k in MATS:
        pc = pieces[k]
        if k not in COL_SHARDED:
            pc = pc.reshape(N_CHIPS, 2, pc.shape[0] // (2 * N_CHIPS), pc.shape[1])
        pcs.append(pc)
    got = exchange_siblings(pcs)
    chip_sums = [sum_cores("sum_cores_" + k, a, b, place) for k, a, b in zip(MATS, pcs, got)]
    landed = scatter_chips(chip_sums)
    halves = [sum_chips("sum_chips_" + k, a, b, place) for k, a, b in zip(MATS, chip_sums, landed)]
    shared = share_halves(halves)
    grad, delta, new_m, new_v = {}, {}, {}, {}
    for k, sh in zip(MATS, shared):
        grad[k] = sh.reshape(w[k].shape)
        delta[k], new_m[k], new_v[k] = adamw("adamw_" + k, w[k], grad[k], m[k], v[k])

    small = jnp.concatenate([dg1, dgm, dg2, dgp, dgf, dcw8[:3], loss_row, jnp.zeros((7, d), F32)], axis=0)
    tot = gather_small("sum_small", small, True)
    loss = tot[8, 0]
    norm_w = jnp.concatenate([w[k].reshape(1, d) for k in NORMS] + [jnp.zeros((3, d), F32)], axis=0)
    norm_m = jnp.concatenate([m[k].reshape(1, d) for k in NORMS] + [jnp.zeros((3, d), F32)], axis=0)
    norm_v = jnp.concatenate([v[k].reshape(1, d) for k in NORMS] + [jnp.ones((3, d), F32)], axis=0)
    norm_g = jnp.concatenate([tot[0:5], jnp.zeros((3, d), F32)], axis=0)
    nd, nm, nv = adamw("adamw_norms", norm_w, norm_g, norm_m, norm_v)
    for r, k in enumerate(NORMS):
        grad[k] = norm_g[r].reshape(w[k].shape)
        delta[k], new_m[k], new_v[k] = (a[r].reshape(w[k].shape) for a in (nd, nm, nv))
    cs = d // N_CHIPS
    gcw = lax.dynamic_slice(tot[5:8], (0, chip * cs), (3, cs))
    cd, cm, cv = adamw("adamw_conv_w", _pad_rows(w["conv_w"], 8), _pad_rows(gcw, 8), _pad_rows(m["conv_w"], 8),
                       jnp.concatenate([v["conv_w"], jnp.ones((5, cs), F32)], axis=0))
    grad["conv_w"], delta["conv_w"], new_m["conv_w"], new_v["conv_w"] = gcw, cd[:3], cm[:3], cv[:3]
    return loss, dx, grad, delta, new_m, new_v


def kernel(x, p, ffn1_norm, ffn1_w_in, ffn1_w_out, mix_norm, w_mix_in, conv_w, w_conv_out, w_attn_out, w_mix_out, ffn2_norm, ffn2_w_in, ffn2_w_out, ple_norm, w_ple_gate, w_ple_proj, final_norm, loss_target, m_ffn1_norm, m_ffn1_w_in, m_ffn1_w_out, m_mix_norm, m_w_mix_in, m_conv_w, m_w_conv_out, m_w_attn_out, m_w_mix_out, m_ffn2_norm, m_ffn2_w_in, m_ffn2_w_out, m_ple_norm, m_w_ple_gate, m_w_ple_proj, m_final_norm, v_ffn1_norm, v_ffn1_w_in, v_ffn1_w_out, v_mix_norm, v_w_mix_in, v_conv_w, v_w_conv_out, v_w_attn_out, v_w_mix_out, v_ffn2_norm, v_ffn2_w_in, v_ffn2_w_out, v_ple_norm, v_w_ple_gate, v_w_ple_proj, v_final_norm):
    ws = (ffn1_norm, ffn1_w_in, ffn1_w_out, mix_norm, w_mix_in, conv_w, w_conv_out, w_attn_out, w_mix_out, ffn2_norm,
          ffn2_w_in, ffn2_w_out, ple_norm, w_ple_gate, w_ple_proj, final_norm)
    ms = (m_ffn1_norm, m_ffn1_w_in, m_ffn1_w_out, m_mix_norm, m_w_mix_in, m_conv_w, m_w_conv_out, m_w_attn_out,
          m_w_mix_out, m_ffn2_norm, m_ffn2_w_in, m_ffn2_w_out, m_ple_norm, m_w_ple_gate, m_w_ple_proj, m_final_norm)
    vs = (v_ffn1_norm, v_ffn1_w_in, v_ffn1_w_out, v_mix_norm, v_w_mix_in, v_conv_w, v_w_conv_out, v_w_attn_out,
          v_w_mix_out, v_ffn2_norm, v_ffn2_w_in, v_ffn2_w_out, v_ple_norm, v_w_ple_gate, v_w_ple_proj, v_final_norm)
    assert x.shape[0] == 1 and p.shape[:2] == (1, 1), "one sequence and one layer per device"

    def strip(a):
        return a[0] if a.ndim == 3 or (a.ndim == 2 and a.shape[0] == 1) else a

    w = {k: strip(a) for k, a in zip(WEIGHTS, ws)}
    m = {k: strip(a) for k, a in zip(WEIGHTS, ms)}
    v = {k: strip(a) for k, a in zip(WEIGHTS, vs)}
    loss, dx, grad, delta, new_m, new_v = _step(x[0], p[0, 0], loss_target[0], w, m, v)
    shapes = [a.shape for a in ws]
    outs = [loss, dx[None]]
    for res in (grad, delta, new_m, new_v):
        outs += [res[k].reshape(s) for k, s in zip(WEIGHTS, shapes)]
    return tuple(outs)
```

```python
import functools
import math

import jax
import jax.numpy as jnp
from jax import lax
from jax.experimental import pallas as pl
from jax.experimental.pallas import tpu as pltpu

F32 = jnp.float32
BF16 = jnp.bfloat16
MESH = pl.DeviceIdType.MESH
ANY = pl.BlockSpec(memory_space=pl.ANY)

HEAD_DIM = 128
NORM_EPS = 1e-6
N_CHIPS = 4
N_DEV = 8
BF16_ROWS = 16
VMEM_LIMIT = 56 * 1024 * 1024
STICK_EXIT = 110.0

ADAM_LR = 0.001
ADAM_B1 = 0.9
ADAM_B2 = 0.999
ADAM_EPS = 1e-08
ADAM_WD = 0.01
ADAM_STEP = 10

NN = (((1,), (0,)), ((), ()))
NT = (((1,), (1,)), ((), ()))
TN = (((0,), (0,)), ((), ()))


def _params(sem=None, **kw):
    if sem is not None:
        kw["dimension_semantics"] = sem
    return pltpu.CompilerParams(vmem_limit_bytes=VMEM_LIMIT, **kw)


def _pcall(body, **kw):
    return pl.pallas_call(body, **kw)


def _tile(n, pref, mult=8):
    best = None
    for d in range(mult, min(n, pref) + 1, mult):
        if n % d == 0:
            best = d
    return best if best is not None else n


def _dot(a, b, dims):
    return lax.dot_general(a, b, dims, preferred_element_type=F32)


def _mm(name, a, b, out_sds, grid, a_spec, b_spec, o_spec, dims, acc_shape, res=None, alpha=1.0):
    nk = grid[2]

    def body(*refs):
        if res is not None:
            a_ref, b_ref, r_ref, o_ref = refs[:4]
        else:
            a_ref, b_ref, o_ref = refs[:3]
            r_ref = None

        def finish(r):
            if alpha != 1.0:
                r = r * alpha
            if r_ref is not None:
                r = r_ref[...] + r
            o_ref[...] = r.astype(o_ref.dtype)

        part = _dot(a_ref[...].astype(BF16), b_ref[...].astype(BF16), dims)
        if nk == 1:
            finish(part)
        else:
            acc_ref = refs[-1]
            kk = pl.program_id(2)

            @pl.when(kk == 0)
            def _():
                acc_ref[...] = part

            @pl.when(kk > 0)
            def _():
                acc_ref[...] += part

            @pl.when(kk == nk - 1)
            def _():
                finish(acc_ref[...])

    in_specs = [a_spec, b_spec]
    args = [a, b]
    if res is not None:
        in_specs.append(o_spec)
        args.append(res)
    scratch = [] if nk == 1 else [pltpu.VMEM(acc_shape, F32)]
    return _pcall(body, name=name, out_shape=out_sds, grid=grid, in_specs=in_specs, out_specs=o_spec,
                  scratch_shapes=scratch,
                  compiler_params=_params(("parallel", "parallel", "arbitrary")))(*args)


def mm_nn(name, a, w, out_dtype, tm, res=None, alpha=1.0):
    m, k = a.shape
    n = w.shape[1]
    return _mm(name, a, w, jax.ShapeDtypeStruct((m, n), out_dtype), (m // tm, 1, 1),
               pl.BlockSpec((tm, k), lambda i, j, r: (i, 0)),
               pl.BlockSpec((k, n), lambda i, j, r: (0, 0)),
               pl.BlockSpec((tm, n), lambda i, j, r: (i, 0)), NN, None, res=res, alpha=alpha)


def mm_nn_stacked(name, a, w4, out_dtype, tm, tn, j0=0, nj=None):
    m, k = a.shape
    cs = w4.shape[2]
    per = cs // tn
    nj = N_CHIPS * per - j0 if nj is None else nj
    return _mm(name, a, w4, jax.ShapeDtypeStruct((m, nj * tn), out_dtype), (m // tm, nj, 1),
               pl.BlockSpec((tm, k), lambda i, j, r: (i, 0)),
               pl.BlockSpec((None, k, tn), lambda i, j, r: ((j + j0) // per, 0, (j + j0) % per)),
               pl.BlockSpec((tm, tn), lambda i, j, r: (i, j)), NN, None)


def mm_nt(name, dy, w, out_dtype, tm, tko):
    m, n = dy.shape
    k = w.shape[0]
    return _mm(name, dy, w, jax.ShapeDtypeStruct((m, k), out_dtype), (m // tm, k // tko, 1),
               pl.BlockSpec((tm, n), lambda i, j, r: (i, 0)),
               pl.BlockSpec((tko, n), lambda i, j, r: (j, 0)),
               pl.BlockSpec((tm, tko), lambda i, j, r: (i, j)), NT, None)


def mm_nt_stacked(name, dy, w4, out_dtype, tm, tn):
    m = dy.shape[0]
    k, cs = w4.shape[1], w4.shape[2]
    per = cs // tn
    return _mm(name, dy, w4, jax.ShapeDtypeStruct((m, k), out_dtype), (m // tm, 1, N_CHIPS * per),
               pl.BlockSpec((tm, tn), lambda i, j, r: (i, r)),
               pl.BlockSpec((None, k, tn), lambda i, j, r: (r // per, 0, r % per)),
               pl.BlockSpec((tm, k), lambda i, j, r: (i, 0)), NT, (tm, k))


def mm_tn_rows(name, xa, dy, tt):
    t, k = xa.shape
    n = dy.shape[1]
    tkr = k // 2
    return _mm(name, xa, dy, jax.ShapeDtypeStruct((k, n), BF16), (k // tkr, 1, t // tt),
               pl.BlockSpec((tt, tkr), lambda i, j, r: (r, i)),
               pl.BlockSpec((tt, n), lambda i, j, r: (r, 0)),
               pl.BlockSpec((tkr, n), lambda i, j, r: (i, 0)), TN, (tkr, n))


def mm_tn_whole(name, xa, dy, tt):
    t, k = xa.shape
    n = dy.shape[1]
    return _mm(name, xa, dy, jax.ShapeDtypeStruct((k, n), BF16), (1, 1, t // tt),
               pl.BlockSpec((tt, k), lambda i, j, r: (r, 0)),
               pl.BlockSpec((tt, n), lambda i, j, r: (r, 0)),
               pl.BlockSpec((k, n), lambda i, j, r: (0, 0)), TN, (k, n))


def mm_tn_cols(name, xa, dy, tt):
    t, k = xa.shape
    cs = dy.shape[1] // N_CHIPS
    pr = k // 2
    return _mm(name, xa, dy, jax.ShapeDtypeStruct((N_CHIPS, 2, pr, cs), BF16), (2, N_CHIPS, t // tt),
               pl.BlockSpec((tt, pr), lambda i, j, r: (r, i)),
               pl.BlockSpec((tt, cs), lambda i, j, r: (r, j)),
               pl.BlockSpec((None, None, pr, cs), lambda i, j, r: (j, i, 0, 0)), TN, (pr, cs))


def _rows(tt, w, col=0):
    return pl.BlockSpec((tt, w), lambda i: (i, col))


def _whole(shape):
    return pl.BlockSpec(shape, lambda i: (0,) * len(shape))


def _rstd(h):
    return lax.rsqrt(jnp.mean(h * h, axis=-1, keepdims=True) + NORM_EPS)


def rms_fwd(name, h, g, tt):
    t, d = h.shape

    def body(h_ref, g_ref, o_ref):
        hv = h_ref[...]
        o_ref[...] = (hv * _rstd(hv) * g_ref[...]).astype(o_ref.dtype)

    return _pcall(body, name=name, out_shape=jax.ShapeDtypeStruct((t, d), BF16), grid=(t // tt,),
                  in_specs=[_rows(tt, d), _whole((1, d))], out_specs=_rows(tt, d),
                  compiler_params=_params(("parallel",)))(h, g)


def rms_bwd(name, h, g, dn, dres, alpha, tt):
    t, d = h.shape

    def body(h_ref, g_ref, dn_ref, dr_ref, dh_ref, dhb_ref, dg_ref):
        hv = h_ref[...]
        hn = hv * _rstd(hv)
        dnv = dn_ref[...]
        gy = dnv * g_ref[...]
        dh = dr_ref[...] + _rstd(hv) * (gy - hn * jnp.mean(gy * hn, axis=-1, keepdims=True))
        dh_ref[...] = dh
        dhb_ref[...] = (alpha * dh).astype(BF16)

        @pl.when(pl.program_id(0) == 0)
        def _():
            dg_ref[...] = jnp.zeros_like(dg_ref)

        dg_ref[...] += jnp.sum(dnv * hn, axis=0, keepdims=True)

    return _pcall(body, name=name,
                  out_shape=(jax.ShapeDtypeStruct((t, d), F32), jax.ShapeDtypeStruct((t, d), BF16),
                             jax.ShapeDtypeStruct((1, d), F32)),
                  grid=(t // tt,),
                  in_specs=[_rows(tt, d), _whole((1, d)), _rows(tt, d), _rows(tt, d)],
                  out_specs=(_rows(tt, d), _rows(tt, d), _whole((1, d))),
                  compiler_params=_params(("arbitrary",)))(h, g, dn, dres)


def swiglu_fwd(name, a, tt):
    t, f2 = a.shape
    f = f2 // 2

    def body(a_ref, o_ref):
        gate = a_ref[:, :f]
        up = a_ref[:, f:]
        o_ref[...] = (gate * jax.nn.sigmoid(gate) * up).astype(o_ref.dtype)

    return _pcall(body, name=name, out_shape=jax.ShapeDtypeStruct((t, f), BF16), grid=(t // tt,),
                  in_specs=[_rows(tt, f2)], out_specs=_rows(tt, f),
                  compiler_params=_params(("parallel",)))(a)


def swiglu_bwd(name, a, ds, tt):
    t, f2 = a.shape
    f = f2 // 2

    def body(a_ref, ds_ref, o_ref):
        gate = a_ref[:, :f]
        up = a_ref[:, f:]
        dsv = ds_ref[...]
        sg = jax.nn.sigmoid(gate)
        o_ref[:, :f] = (dsv * up * sg * (1.0 + gate * (1.0 - sg))).astype(o_ref.dtype)
        o_ref[:, f:] = (dsv * gate * sg).astype(o_ref.dtype)

    return _pcall(body, name=name, out_shape=jax.ShapeDtypeStruct((t, f2), BF16), grid=(t // tt,),
                  in_specs=[_rows(tt, f2), _rows(tt, f)], out_specs=_rows(tt, f2),
                  compiler_params=_params(("parallel",)))(a, ds)


def gate_fwd(name, gates, yc, ya, tt):
    t, d = yc.shape

    def body(g_ref, yc_ref, ya_ref, o_ref):
        o_ref[...] = (jax.nn.sigmoid(g_ref[:, :d]) * yc_ref[...]
                      + jax.nn.sigmoid(g_ref[:, d:]) * ya_ref[...]).astype(o_ref.dtype)

    return _pcall(body, name=name, out_shape=jax.ShapeDtypeStruct((t, d), BF16), grid=(t // tt,),
                  in_specs=[_rows(tt, 2 * d), _rows(tt, d), _rows(tt, d)], out_specs=_rows(tt, d),
                  compiler_params=_params(("parallel",)))(gates, yc, ya)


def gate_bwd(name, dm, gates, yc, ya, tt):
    t, d = yc.shape

    def body(dm_ref, g_ref, yc_ref, ya_ref, dyc_ref, dya_ref, dg_ref):
        dmv = dm_ref[...]
        sc = jax.nn.sigmoid(g_ref[:, :d])
        sa = jax.nn.sigmoid(g_ref[:, d:])
        dyc_ref[...] = (dmv * sc).astype(BF16)
        dya_ref[...] = (dmv * sa).astype(BF16)
        dg_ref[:, :d] = (dmv * yc_ref[...] * sc * (1.0 - sc)).astype(BF16)
        dg_ref[:, d:] = (dmv * ya_ref[...] * sa * (1.0 - sa)).astype(BF16)

    return _pcall(body, name=name,
                  out_shape=(jax.ShapeDtypeStruct((t, d), BF16), jax.ShapeDtypeStruct((t, d), BF16),
                             jax.ShapeDtypeStruct((t, 2 * d), BF16)),
                  grid=(t // tt,),
                  in_specs=[_rows(tt, d), _rows(tt, 2 * d), _rows(tt, d), _rows(tt, d)],
                  out_specs=(_rows(tt, d), _rows(tt, d), _rows(tt, 2 * d)),
                  compiler_params=_params(("parallel",)))(dm, gates, yc, ya)


def _shift_down(cur, prev8, s):
    tt = cur.shape[0]
    rolled = pltpu.roll(cur, s, 0)
    row8 = lax.broadcasted_iota(jnp.int32, prev8.shape, 0)
    first8 = jnp.where(row8 < s, pltpu.roll(prev8, s, 0), rolled[:8])
    return jnp.concatenate([first8, rolled[8:]], axis=0) if tt > 8 else first8


def _shift_up(cur, next8, s):
    tt = cur.shape[0]
    rolled = pltpu.roll(cur, tt - s, 0)
    row8 = lax.broadcasted_iota(jnp.int32, next8.shape, 0)
    last8 = jnp.where(row8 >= 8 - s, pltpu.roll(next8, 8 - s, 0), rolled[tt - 8:])
    return jnp.concatenate([rolled[:tt - 8], last8], axis=0) if tt > 8 else last8


def _prev8(tt, d, col):
    return pl.BlockSpec((8, d), lambda i: (jnp.maximum(i * (tt // 8) - 1, 0), col))


def _next8(tt, d, col, t):
    return pl.BlockSpec((8, d), lambda i: (jnp.minimum((i + 1) * (tt // 8), t // 8 - 1), col))


def conv_fwd(name, cbx, cw8, tt):
    t, d3 = cbx.shape
    d = d3 // 3

    def body(cb_ref, cc_ref, cx_ref, pc_ref, px_ref, w_ref, o_ref):
        has_prev = (pl.program_id(0) > 0).astype(F32)
        cc = cc_ref[...] * cx_ref[...]
        prev = pc_ref[...] * px_ref[...] * has_prev
        w = w_ref[...]
        conv = w[0:1] * _shift_down(cc, prev, 2) + w[1:2] * _shift_down(cc, prev, 1) + w[2:3] * cc
        o_ref[...] = (cb_ref[...] * conv).astype(o_ref.dtype)

    return _pcall(body, name=name, out_shape=jax.ShapeDtypeStruct((t, d), BF16), grid=(t // tt,),
                  in_specs=[_rows(tt, d, 0), _rows(tt, d, 1), _rows(tt, d, 2), _prev8(tt, d, 1), _prev8(tt, d, 2),
                            _whole((8, d))],
                  out_specs=_rows(tt, d), compiler_params=_params(("parallel",)))(cbx, cbx, cbx, cbx, cbx, cw8)


def conv_bwd(name, dyc, cbx, cw8, tt):
    t, d3 = cbx.shape
    d = d3 // 3
    n = t // tt

    def body(dy_ref, cb_ref, cc_ref, cx_ref, pc_ref, px_ref, ndy_ref, ncb_ref, w_ref, o_ref, dw_ref):
        i = pl.program_id(0)
        has_prev = (i > 0).astype(F32)
        has_next = (i < n - 1).astype(F32)
        cb = cb_ref[...]
        cc = cc_ref[...] * cx_ref[...]
        prev = pc_ref[...] * px_ref[...] * has_prev
        w = w_ref[...]
        cc1 = _shift_down(cc, prev, 1)
        cc2 = _shift_down(cc, prev, 2)
        conv = w[0:1] * cc2 + w[1:2] * cc1 + w[2:3] * cc
        dyv = dy_ref[...]
        dconv = dyv * cb
        dnext = ndy_ref[...] * ncb_ref[...] * has_next
        dcc = w[2:3] * dconv + w[1:2] * _shift_up(dconv, dnext, 1) + w[0:1] * _shift_up(dconv, dnext, 2)
        o_ref[:, :d] = (dyv * conv).astype(BF16)
        o_ref[:, d:2 * d] = (dcc * cx_ref[...]).astype(BF16)
        o_ref[:, 2 * d:] = (dcc * cc_ref[...]).astype(BF16)

        @pl.when(i == 0)
        def _():
            dw_ref[...] = jnp.zeros_like(dw_ref)

        dw_ref[0:1, :] += jnp.sum(dconv * cc2, axis=0, keepdims=True)
        dw_ref[1:2, :] += jnp.sum(dconv * cc1, axis=0, keepdims=True)
        dw_ref[2:3, :] += jnp.sum(dconv * cc, axis=0, keepdims=True)

    return _pcall(body, name=name,
                  out_shape=(jax.ShapeDtypeStruct((t, d3), BF16), jax.ShapeDtypeStruct((8, d), F32)),
                  grid=(n,),
                  in_specs=[_rows(tt, d), _rows(tt, d, 0), _rows(tt, d, 1), _rows(tt, d, 2),
                            _prev8(tt, d, 1), _prev8(tt, d, 2), _next8(tt, d, 0, t), _next8(tt, d, 0, t),
                            _whole((8, d))],
                  out_specs=(_rows(tt, d3), _whole((8, d))),
                  compiler_params=_params(("arbitrary",)))(dyc, cbx, cbx, cbx, cbx, cbx, dyc, cbx, cw8)


def tail(name, h3, zg, pp, tgt, gf, tt):
    t, d = h3.shape

    def body(h_ref, zg_ref, pp_ref, tg_ref, gf_ref, dh_ref, dpp_ref, dzg_ref, dgf_ref, loss_ref):
        pg = jax.nn.sigmoid(zg_ref[...])
        ppv = pp_ref[...]
        h4 = h_ref[...] + pg * ppv
        r4 = _rstd(h4)
        hn = h4 * r4
        gfv = gf_ref[...]
        err = hn * gfv - tg_ref[...]
        dy = err * (1.0 / d)
        gy = dy * gfv
        dh4 = r4 * (gy - hn * jnp.mean(gy * hn, axis=-1, keepdims=True))
        dh_ref[...] = dh4
        dpp_ref[...] = (dh4 * pg).astype(BF16)
        dzg_ref[...] = (dh4 * ppv * pg * (1.0 - pg)).astype(BF16)

        @pl.when(pl.program_id(0) == 0)
        def _():
            dgf_ref[...] = jnp.zeros_like(dgf_ref)
            loss_ref[...] = jnp.zeros_like(loss_ref)

        dgf_ref[...] += jnp.sum(dy * hn, axis=0, keepdims=True)
        tok = jnp.mean(err * err, axis=-1, keepdims=True)
        loss_ref[...] += 0.5 * jnp.sum(tok, axis=0, keepdims=True) * jnp.ones((1, loss_ref.shape[1]), F32)

    return _pcall(body, name=name,
                  out_shape=(jax.ShapeDtypeStruct((t, d), F32), jax.ShapeDtypeStruct((t, d), BF16),
                             jax.ShapeDtypeStruct((t, d), BF16), jax.ShapeDtypeStruct((1, d), F32),
                             jax.ShapeDtypeStruct((1, d), F32)),
                  grid=(t // tt,),
                  in_specs=[_rows(tt, d)] * 4 + [_whole((1, d))],
                  out_specs=(_rows(tt, d), _rows(tt, d), _rows(tt, d), _whole((1, d)), _whole((1, d))),
                  compiler_params=_params(("arbitrary",)))(h3, zg, pp, tgt, gf)


SCALE = 1.0 / math.sqrt(HEAD_DIM)


def _log_stick(z):
    return -(jnp.maximum(z, 0.0) + jnp.log(1.0 + jnp.exp(-jnp.abs(z))))


def _tri_sum(x, tri):
    hi = x.astype(BF16)
    lo = (x - hi.astype(F32)).astype(BF16)
    return _dot(hi, tri, NN) + _dot(lo, tri, NN)


def _sb_pair(q, k_d, k_p, below, upper, has_prev):
    z_d = _dot(q, k_d, NT) * SCALE
    z_p = _dot(q, k_p, NT) * SCALE
    cum_d = _tri_sum(jnp.where(below, _log_stick(z_d), 0.0), upper)
    cum_p = _tri_sum(jnp.where(has_prev, _log_stick(z_p), 0.0), upper)
    c_d = cum_d[:, 0:1]
    a_d = jnp.exp(jnp.where(below, z_d + cum_d, -1e30))
    a_p = jnp.exp(jnp.where(has_prev, z_p + cum_p + c_d, -1e30))
    return z_d, z_p, a_d, a_p, c_d + cum_p[:, 0:1]


def _sb_far(q, kj, upper, c):
    z = _dot(q, kj, NT) * SCALE
    cum = _tri_sum(_log_stick(z), upper)
    return z, jnp.exp(z + cum + c), c + cum[:, 0:1]


def _block_rows(j, tq):
    return pl.ds(pl.multiple_of(j * tq, tq), tq)


def attn_fwd(name, qkv, tq):
    t, d3 = qkv.shape
    d = d3 // 3
    nh = d // HEAD_DIM
    nq = t // tq

    def body(q_ref, k_ref, v_ref, o_ref):
        i = pl.program_id(1)
        q = q_ref[...]
        row = lax.broadcasted_iota(jnp.int32, (tq, tq), 0)
        col = lax.broadcasted_iota(jnp.int32, (tq, tq), 1)
        upper = (row >= col).astype(BF16)
        rows_d = _block_rows(i, tq)
        rows_p = _block_rows(jnp.maximum(i - 1, 0), tq)
        _, _, a_d, a_p, c = _sb_pair(q, k_ref[rows_d, :], k_ref[rows_p, :], col < row, upper, i > 0)
        acc = _dot(a_d.astype(BF16), v_ref[rows_d, :], NN) + _dot(a_p.astype(BF16), v_ref[rows_p, :], NN)

        def cond(st):
            return jnp.logical_and(st[0] >= 0, jnp.max(st[1]) > -STICK_EXIT)

        def step(st):
            rows = _block_rows(st[0], tq)
            _, a, c2 = _sb_far(q, k_ref[rows, :], upper, st[1])
            return st[0] - 1, c2, st[2] + _dot(a.astype(BF16), v_ref[rows, :], NN)

        _, _, acc = lax.while_loop(cond, step, (i - 2, c, acc))
        o_ref[...] = acc.astype(o_ref.dtype)

    return _pcall(body, name=name, out_shape=jax.ShapeDtypeStruct((t, d), BF16), grid=(nh, nq),
                  in_specs=[pl.BlockSpec((tq, HEAD_DIM), lambda h, i: (i, h)),
                            pl.BlockSpec((t, HEAD_DIM), lambda h, i: (0, nh + h)),
                            pl.BlockSpec((t, HEAD_DIM), lambda h, i: (0, 2 * nh + h))],
                  out_specs=pl.BlockSpec((tq, HEAD_DIM), lambda h, i: (i, h)),
                  compiler_params=_params(("parallel", "arbitrary")))(qkv, qkv, qkv)


def attn_bwd(name, qkv, do, tq):
    t, d3 = qkv.shape
    d = d3 // 3
    nh = d // HEAD_DIM
    nq = t // tq

    def body(q_ref, k_ref, v_ref, do_ref, dq_ref, dk_ref, dv_ref, dk_acc, dv_acc, g_buf, z_buf):
        i = pl.program_id(1)

        @pl.when(i == 0)
        def _():
            dk_acc[...] = jnp.zeros_like(dk_acc)
            dv_acc[...] = jnp.zeros_like(dv_acc)

        q = q_ref[...]
        dov = do_ref[...]
        row = lax.broadcasted_iota(jnp.int32, (tq, tq), 0)
        col = lax.broadcasted_iota(jnp.int32, (tq, tq), 1)
        below = col < row
        has_prev = i > 0
        upper = (row >= col).astype(BF16)
        lower = (row <= col).astype(BF16)
        rows_d = _block_rows(i, tq)
        rows_p = _block_rows(jnp.maximum(i - 1, 0), tq)

        z_d, z_p, a_d, a_p, c = _sb_pair(q, k_ref[rows_d, :], k_ref[rows_p, :], below, upper, has_prev)
        g_d = _dot(dov, v_ref[rows_d, :], NT) * a_d
        g_p = _dot(dov, v_ref[rows_p, :], NT) * a_p
        dv_acc[rows_d, :] += _dot(a_d.astype(BF16), dov, TN)
        dv_acc[rows_p, :] += _dot(a_p.astype(BF16), dov, TN)

        def cond(st):
            return jnp.logical_and(st[0] >= 0, jnp.max(st[1]) > -STICK_EXIT)

        def step(st):
            j = st[0]
            rows = _block_rows(j, tq)
            z, a, c2 = _sb_far(q, k_ref[rows, :], upper, st[1])
            g_buf[i - j] = _dot(dov, v_ref[rows, :], NT) * a
            z_buf[i - j] = z
            dv_acc[rows, :] += _dot(a.astype(BF16), dov, TN)
            return j - 1, c2

        j_stop, _ = lax.while_loop(cond, step, (i - 2, c))

        def far(j, st):
            run, dq = st
            rows = _block_rows(j, tq)
            g = g_buf[i - j]
            dz = (g - jax.nn.sigmoid(z_buf[i - j]) * (run + _tri_sum(g, lower))).astype(BF16)
            dk_acc[rows, :] += _dot(dz, q, TN)
            return run + jnp.sum(g, axis=1, keepdims=True), dq + _dot(dz, k_ref[rows, :], NN)

        run, dq = lax.fori_loop(j_stop + 1, i - 1, far,
                                (jnp.zeros((tq, 1), F32), jnp.zeros((tq, HEAD_DIM), F32)))
        p_p = run + _tri_sum(g_p, lower)
        p_d = run + jnp.sum(g_p, axis=1, keepdims=True) + _tri_sum(g_d, lower)
        dz_p = jnp.where(has_prev, g_p - jax.nn.sigmoid(z_p) * p_p, 0.0).astype(BF16)
        dz_d = jnp.where(below, g_d - jax.nn.sigmoid(z_d) * p_d, 0.0).astype(BF16)
        dk_acc[rows_p, :] += _dot(dz_p, q, TN)
        dk_acc[rows_d, :] += _dot(dz_d, q, TN)
        dq = dq + _dot(dz_p, k_ref[rows_p, :], NN) + _dot(dz_d, k_ref[rows_d, :], NN)
        dq_ref[...] = (dq * SCALE).astype(BF16)

        @pl.when(i == nq - 1)
        def _():
            dk_ref[...] = (dk_acc[...] * SCALE).astype(BF16)
            dv_ref[...] = dv_acc[...].astype(BF16)

    blk = pl.BlockSpec((tq, HEAD_DIM), lambda h, i: (i, h))
    col_h = pl.BlockSpec((t, HEAD_DIM), lambda h, i: (0, h))
    out = jax.ShapeDtypeStruct((t, d), BF16)
    return _pcall(body, name=name, out_shape=(out, out, out), grid=(nh, nq),
                  in_specs=[blk,
                            pl.BlockSpec((t, HEAD_DIM), lambda h, i: (0, nh + h)),
                            pl.BlockSpec((t, HEAD_DIM), lambda h, i: (0, 2 * nh + h)),
                            blk],
                  out_specs=(blk, col_h, col_h),
                  scratch_shapes=[pltpu.VMEM((t, HEAD_DIM), F32), pltpu.VMEM((t, HEAD_DIM), F32),
                                  pltpu.VMEM((nq, tq, tq), F32), pltpu.VMEM((nq, tq, tq), F32)],
                  compiler_params=_params(("parallel", "arbitrary")))(qkv, qkv, qkv, do)


def _place():
    x, y, c = lax.axis_index("x"), lax.axis_index("y"), lax.axis_index("c")
    chips = [(1 - x, y), (x, 1 - y), (1 - x, 1 - y)]
    return x, y, c, chips


def _remote(src, dst, send_sem, recv_sem, dev):
    return pltpu.make_async_remote_copy(src_ref=src, dst_ref=dst, send_sem=send_sem, recv_sem=recv_sem,
                                        device_id=dev, device_id_type=MESH)


def place_shard(name, w, chip):
    r, cdim = w.shape
    tr = _tile(r, max(BF16_ROWS, (1 << 19) // cdim), BF16_ROWS)

    def body(chip_ref, w_ref, o_ref):
        o_ref[...] = w_ref[...].astype(BF16)

    spec = pltpu.PrefetchScalarGridSpec(
        num_scalar_prefetch=1, grid=(r // tr,),
        in_specs=[pl.BlockSpec((tr, cdim), lambda i, s: (i, 0))],
        out_specs=pl.BlockSpec((None, tr, cdim), lambda i, s: (s[0], i, 0)))
    return _pcall(body, name=name, out_shape=jax.ShapeDtypeStruct((N_CHIPS, r, cdim), BF16), grid_spec=spec,
                  compiler_params=_params(("parallel",)))(chip, w)


def gather_weights(bufs):
    n = len(bufs)

    def body(*refs):
        out = refs[n:2 * n]
        isend, irecv, dsend, drecv = refs[2 * n:]
        x, y, c, chips = _place()
        me = 2 * x + y
        sib = (x, y, 1 - c)
        first, passed = [], []
        for w in range(n):
            pr = out[w].shape[1] // 2
            half = pl.ds(pl.multiple_of(c * pr, BF16_ROWS), pr)
            for j, (cx, cy) in enumerate(chips):
                mine = out[w].at[me, half]
                cp = _remote(mine, mine, isend.at[3 * w + j], irecv.at[3 * w + j], (cx, cy, c))
                cp.start()
                first.append(cp)
        for w in range(n):
            pr = out[w].shape[1] // 2
            half = pl.ds(pl.multiple_of(c * pr, BF16_ROWS), pr)
            for j, (cx, cy) in enumerate(chips):
                landed = out[w].at[2 * cx + cy, half]
                _remote(landed, landed, isend.at[3 * w + j], irecv.at[3 * w + j], sib).wait_recv()
                fw = _remote(landed, landed, dsend.at[3 * w + j], drecv.at[3 * w + j], sib)
                fw.start()
                passed.append(fw)
        for w in range(n):
            pr = out[w].shape[1] // 2
            other = pl.ds(pl.multiple_of((1 - c) * pr, BF16_ROWS), pr)
            for j, (cx, cy) in enumerate(chips):
                landed = out[w].at[2 * cx + cy, other]
                _remote(landed, landed, dsend.at[3 * w + j], drecv.at[3 * w + j], sib).wait_recv()
        for cp in first + passed:
            cp.wait_send()

    return _pcall(body, name="gather_weights",
                  out_shape=[jax.ShapeDtypeStruct(s.shape, s.dtype) for s in bufs],
                  in_specs=[ANY] * n, out_specs=[ANY] * n, input_output_aliases={w: w for w in range(n)},
                  scratch_shapes=[pltpu.SemaphoreType.DMA((3 * n,))] * 4,
                  compiler_params=_params())(*bufs)


def exchange_siblings(pieces):
    n = len(pieces)

    def body(*refs):
        src, out = refs[:n], refs[n:2 * n]
        send_sem, recv_sem = refs[2 * n:]
        x, y, c, _ = _place()
        sib = (x, y, 1 - c)
        started = []
        for w in range(n):
            for k in range(N_CHIPS):
                s = N_CHIPS * w + k
                cp = _remote(src[w].at[k, 1 - c], out[w].at[k], send_sem.at[s], recv_sem.at[s], sib)
                cp.start()
                started.append(cp)
        for cp in started:
            cp.wait()

    return _pcall(body, name="exchange_siblings",
                  out_shape=[jax.ShapeDtypeStruct((N_CHIPS,) + s.shape[2:], s.dtype) for s in pieces],
                  in_specs=[ANY] * n, out_specs=[ANY] * n,
                  scratch_shapes=[pltpu.SemaphoreType.DMA((N_CHIPS * n,))] * 2,
                  compiler_params=_params())(*pieces)


def scatter_chips(parts):
    n = len(parts)

    def body(*refs):
        src, out = refs[:n], refs[n:2 * n]
        send_sem, recv_sem = refs[2 * n:]
        x, y, c, chips = _place()
        started = []
        for w in range(n):
            for j, (cx, cy) in enumerate(chips):
                cp = _remote(src[w].at[2 * cx + cy], out[w].at[j], send_sem.at[3 * w + j], recv_sem.at[3 * w + j],
                             (cx, cy, c))
                cp.start()
                started.append(cp)
        for cp in started:
            cp.wait()

    return _pcall(body, name="scatter_chips",
                  out_shape=[jax.ShapeDtypeStruct((3,) + s.shape[1:], s.dtype) for s in parts],
                  in_specs=[ANY] * n, out_specs=[ANY] * n,
                  scratch_shapes=[pltpu.SemaphoreType.DMA((3 * n,))] * 2,
                  compiler_params=_params())(*parts)


def share_halves(halves):
    n = len(halves)

    def body(*refs):
        buf = refs[n:2 * n]
        send_sem, recv_sem = refs[2 * n:]
        x, y, c, _ = _place()
        started = []
        for w in range(n):
            cp = _remote(buf[w].at[c], buf[w].at[c], send_sem.at[w], recv_sem.at[w], (x, y, 1 - c))
            cp.start()
            started.append(cp)
        for w in range(n):
            landed = buf[w].at[1 - c]
            _remote(landed, landed, send_sem.at[w], recv_sem.at[w], (x, y, 1 - c)).wait_recv()
        for cp in started:
            cp.wait_send()

    return _pcall(body, name="share_halves",
                  out_shape=[jax.ShapeDtypeStruct(s.shape, s.dtype) for s in halves],
                  in_specs=[ANY] * n, out_specs=[ANY] * n, input_output_aliases={w: w for w in range(n)},
                  scratch_shapes=[pltpu.SemaphoreType.DMA((n,))] * 2,
                  compiler_params=_params())(*halves)


def gather_small(name, blk, reduce):
    r, cdim = blk.shape

    def body(in_ref, out_ref, *rest):
        if reduce:
            buf, send_sem, recv_sem = rest
        else:
            buf = out_ref
            send_sem, recv_sem = rest
        x, y, c, _ = _place()
        me = 4 * x + 2 * y + c
        buf[me] = in_ref[...]
        peers = []
        for dx in range(2):
            for dy in range(2):
                for dc in range(2):
                    if dx or dy or dc:
                        peers.append((dx, dy, dc))
        copies = []
        for s, (dx, dy, dc) in enumerate(peers):
            cp = _remote(in_ref, buf.at[me], send_sem.at[s], recv_sem.at[s],
                         ((1 - x if dx else x), (1 - y if dy else y), (1 - c if dc else c)))
            cp.start()
            copies.append(cp)
        for s, (dx, dy, dc) in enumerate(peers):
            px, py, pc_ = (1 - x if dx else x), (1 - y if dy else y), (1 - c if dc else c)
            landed = buf.at[4 * px + 2 * py + pc_]
            _remote(landed, landed, send_sem.at[s], recv_sem.at[s], (x, y, c)).wait_recv()
        for cp in copies:
            cp.wait_send()
        if reduce:
            tot = buf[0]
            for s in range(1, N_DEV):
                tot = tot + buf[s]
            out_ref[...] = tot

    vm = pl.BlockSpec(memory_space=pltpu.VMEM)
    out_shape = jax.ShapeDtypeStruct((r, cdim) if reduce else (N_DEV, r, cdim), F32)
    scratch = ([pltpu.VMEM((N_DEV, r, cdim), F32)] if reduce else []) + [pltpu.SemaphoreType.DMA((N_DEV - 1,))] * 2
    return _pcall(body, name=name, out_shape=out_shape, in_specs=[vm], out_specs=vm, scratch_shapes=scratch,
                  compiler_params=_params())(blk)


def sum_cores(name, own, got, place):
    _, _, pr, pc = own.shape
    tr = _tile(pr, max(BF16_ROWS, (1 << 19) // pc), BF16_ROWS)

    def body(place_ref, own_ref, got_ref, o_ref):
        o_ref[...] = (own_ref[...].astype(F32) + got_ref[...].astype(F32)).astype(o_ref.dtype)

    spec = pltpu.PrefetchScalarGridSpec(
        num_scalar_prefetch=1, grid=(N_CHIPS, pr // tr),
        in_specs=[pl.BlockSpec((None, None, tr, pc), lambda k, i, s: (k, s[1], i, 0)),
                  pl.BlockSpec((None, tr, pc), lambda k, i, s: (k, i, 0))],
        out_specs=pl.BlockSpec((None, tr, pc), lambda k, i, s: (k, i, 0)))
    return _pcall(body, name=name, out_shape=jax.ShapeDtypeStruct((N_CHIPS, pr, pc), BF16), grid_spec=spec,
                  compiler_params=_params(("parallel", "parallel")))(place, own, got)


def sum_chips(name, part, got, place):
    _, pr, pc = part.shape
    tr = _tile(pr, max(BF16_ROWS, (1 << 18) // pc), BF16_ROWS)

    def body(place_ref, part_ref, got_ref, o_ref):
        tot = part_ref[...].astype(F32)
        for j in range(3):
            tot = tot + got_ref[j].astype(F32)
        o_ref[...] = tot

    spec = pltpu.PrefetchScalarGridSpec(
        num_scalar_prefetch=1, grid=(pr // tr,),
        in_specs=[pl.BlockSpec((None, tr, pc), lambda i, s: (s[0], i, 0)),
                  pl.BlockSpec((3, tr, pc), lambda i, s: (0, i, 0))],
        out_specs=pl.BlockSpec((None, tr, pc), lambda i, s: (s[1], i, 0)))
    return _pcall(body, name=name, out_shape=jax.ShapeDtypeStruct((2, pr, pc), F32), grid_spec=spec,
                  compiler_params=_params(("parallel",)))(place, part, got)


def adamw(name, w, g, m, v):
    rows, cols = w.shape
    tr = _tile(rows, max(8, (1 << 18) // cols))
    c1 = 1.0 / (1.0 - ADAM_B1 ** ADAM_STEP)
    c2 = 1.0 / (1.0 - ADAM_B2 ** ADAM_STEP)

    def body(w_ref, g_ref, m_ref, v_ref, d_ref, nm_ref, nv_ref):
        gv = g_ref[...]
        nm = ADAM_B1 * m_ref[...] + (1.0 - ADAM_B1) * gv
        nv = ADAM_B2 * v_ref[...] + (1.0 - ADAM_B2) * (gv * gv)
        nm_ref[...] = nm
        nv_ref[...] = nv
        d_ref[...] = -ADAM_LR * ((nm * c1) / (jnp.sqrt(nv * c2) + ADAM_EPS) + ADAM_WD * w_ref[...])

    spec = pl.BlockSpec((tr, cols), lambda i: (i, 0))
    sds = jax.ShapeDtypeStruct((rows, cols), F32)
    return _pcall(body, name=name, out_shape=(sds, sds, sds), grid=(rows // tr,),
                  in_specs=[spec] * 4, out_specs=(spec, spec, spec),
                  compiler_params=_params(("parallel",)))(w, g, m, v)


MATS = ["ffn1_w_in", "ffn1_w_out", "w_mix_in", "w_conv_out", "w_attn_out", "w_mix_out", "ffn2_w_in", "ffn2_w_out",
        "w_ple_gate", "w_ple_proj"]
COL_SHARDED = {"ffn1_w_in", "w_mix_in", "ffn2_w_in", "w_ple_proj"}
NORMS = ["ffn1_norm", "mix_norm", "ffn2_norm", "ple_norm", "final_norm"]
WEIGHTS = ["ffn1_norm", "ffn1_w_in", "ffn1_w_out", "mix_norm", "w_mix_in", "conv_w", "w_conv_out", "w_attn_out",
           "w_mix_out", "ffn2_norm", "ffn2_w_in", "ffn2_w_out", "ple_norm", "w_ple_gate", "w_ple_proj", "final_norm"]


def _pad_rows(a, rows):
    return jnp.concatenate([a, jnp.zeros((rows - a.shape[0],) + a.shape[1:], a.dtype)], axis=0)


def _step(x, p, tgt, w, m, v):
    t, d = x.shape
    tt = _tile(t, 256)
    tm = _tile(t, 512)
    tq = _tile(t, 256)

    chip = 2 * lax.axis_index("x") + lax.axis_index("y")
    place = jnp.stack([chip, lax.axis_index("c")]).astype(jnp.int32)

    full = dict(zip(MATS, gather_weights([place_shard("place_" + k, w[k], place) for k in MATS])))
    for k in MATS:
        if k not in COL_SHARDED:
            full[k] = full[k].reshape(-1, full[k].shape[2])
    wpp = full["w_ple_proj"]
    wpp = jnp.transpose(wpp, (1, 0, 2)).reshape(wpp.shape[1], -1)
    cw_all = gather_small("gather_conv_w", _pad_rows(w["conv_w"], 8), False)
    cw8 = jnp.concatenate([cw_all[2 * k] for k in range(N_CHIPS)], axis=1)
    g1, gm, g2, gp, gf = (w[k].reshape(1, d) for k in NORMS)

    def ffn_fwd(tag, h, g, w_in, w_out):
        n = rms_fwd(tag + "_norm", h, g, tt)
        a = mm_nn_stacked(tag + "_in", n, w_in, F32, tm, w_in.shape[2])
        s = swiglu_fwd(tag + "_act", a, tt)
        return n, a, s, mm_nn(tag + "_out", s, w_out, F32, tm, res=h, alpha=0.5)

    def ffn_bwd(tag, h, g, w_in, w_out, n, a, s, df, dh):
        dw_out = mm_tn_rows(tag + "_dwout", s, df, tm)
        ds = mm_nt(tag + "_ds", df, w_out, F32, tm, w_out.shape[0] // 2)
        da = swiglu_bwd(tag + "_dact", a, ds, tt)
        dw_in = mm_tn_cols(tag + "_dwin", n, da, tm)
        dn = mm_nt_stacked(tag + "_dn", da, w_in, F32, tm, w_in.shape[2])
        return dn, dw_in, dw_out

    n1, a1, s1, h1 = ffn_fwd("ffn1", x, g1, full["ffn1_w_in"], full["ffn1_w_out"])
    u = rms_fwd("mix_norm", h1, gm, tt)
    wmix = full["w_mix_in"]
    cbx = mm_nn_stacked("mix_in_conv", u, wmix, F32, tm, d, 0, 3)
    qkv = mm_nn_stacked("mix_in_qkv", u, wmix, BF16, tm, d, 3, 3)
    gates = mm_nn_stacked("mix_in_gates", u, wmix, F32, tm, d, 6, 2)
    ycin = conv_fwd("conv", cbx, cw8, tt)
    y_conv = mm_nn("conv_out", ycin, full["w_conv_out"], F32, tm)
    o = attn_fwd("attn", qkv, tq)
    y_attn = mm_nn("attn_out", o, full["w_attn_out"], F32, tm)
    merged = gate_fwd("merge", gates, y_conv, y_attn, tt)
    h2 = mm_nn("mix_out", merged, full["w_mix_out"], F32, tm, res=h1, alpha=1.0)
    n2, a2, s2, h3 = ffn_fwd("ffn2", h2, g2, full["ffn2_w_in"], full["ffn2_w_out"])
    npl = rms_fwd("ple_norm", h3, gp, tt)
    zg = mm_nn("ple_gate", npl, full["w_ple_gate"], F32, tm)
    pp = mm_nn("ple_proj", p, wpp, F32, tm)

    dh4, dpp, dzg, dgf, loss_row = tail("tail", h3, zg, pp, tgt, gf, tt)
    pieces = {}
    dwpp = mm_tn_whole("ple_proj_dw", p, dpp, tm)
    pieces["w_ple_proj"] = jnp.transpose(dwpp.reshape(2, p.shape[1] // 2, N_CHIPS, d // N_CHIPS), (2, 0, 1, 3))
    pieces["w_ple_gate"] = mm_tn_rows("ple_gate_dw", npl, dzg, tm)
    dnp = mm_nt("ple_gate_dx", dzg, full["w_ple_gate"], F32, tm, d)
    dh3, df2, dgp = rms_bwd("ple_norm_bwd", h3, gp, dnp, dh4, 0.5, tt)
    dn2, pieces["ffn2_w_in"], pieces["ffn2_w_out"] = ffn_bwd(
        "ffn2", h2, g2, full["ffn2_w_in"], full["ffn2_w_out"], n2, a2, s2, df2, dh3)
    dh2, dh2b, dg2 = rms_bwd("ffn2_norm_bwd", h2, g2, dn2, dh3, 1.0, tt)
    pieces["w_mix_out"] = mm_tn_rows("mix_out_dw", merged, dh2b, tm)
    dmerged = mm_nt("mix_out_dx", dh2b, full["w_mix_out"], F32, tm, d)
    dyc, dya, dgates = gate_bwd("merge_bwd", dmerged, gates, y_conv, y_attn, tt)
    pieces["w_conv_out"] = mm_tn_rows("conv_out_dw", ycin, dyc, tm)
    dycin = mm_nt("conv_out_dx", dyc, full["w_conv_out"], F32, tm, d)
    dcbx, dcw8 = conv_bwd("conv_bwd", dycin, cbx, cw8, tt)
    pieces["w_attn_out"] = mm_tn_rows("attn_out_dw", o, dya, tm)
    do = mm_nt("attn_out_dx", dya, full["w_attn_out"], BF16, tm, d)
    dq, dk, dv = attn_bwd("attn_bwd", qkv, do, tq)
    dmix = jnp.concatenate([dcbx, dq, dk, dv, dgates], axis=1)
    pieces["w_mix_in"] = mm_tn_cols("mix_in_dw", u, dmix, tm)
    du = mm_nt_stacked("mix_in_dx", dmix, wmix, F32, tm, d)
    dh1, df1, dgm = rms_bwd("mix_norm_bwd", h1, gm, du, dh2, 0.5, tt)
    dn1, pieces["ffn1_w_in"], pieces["ffn1_w_out"] = ffn_bwd(
        "ffn1", x, g1, full["ffn1_w_in"], full["ffn1_w_out"], n1, a1, s1, df1, dh1)
    dx, _, dg1 = rms_bwd("ffn1_norm_bwd", x, g1, dn1, dh1, 1.0, tt)

    pcs = []
    for k in MATS:
        pc = pieces[k]
        if k not in COL_SHARDED:
            pc = pc.reshape(N_CHIPS, 2, pc.shape[0] // (2 * N_CHIPS), pc.shape[1])
        pcs.append(pc)
    got = exchange_siblings(pcs)
    chip_sums = [sum_cores("sum_cores_" + k, a, b, place) for k, a, b in zip(MATS, pcs, got)]
    landed = scatter_chips(chip_sums)
    halves = [sum_chips("sum_chips_" + k, a, b, place) for k, a, b in zip(MATS, chip_sums, landed)]
    shared = share_halves(halves)
    grad, delta, new_m, new_v = {}, {}, {}, {}
    for k, sh in zip(MATS, shared):
        grad[k] = sh.reshape(w[k].shape)
        delta[k], new_m[k], new_v[k] = adamw("adamw_" + k, w[k], grad[k], m[k], v[k])

    small = jnp.concatenate([dg1, dgm, dg2, dgp, dgf, dcw8[:3], loss_row, jnp.zeros((7, d), F32)], axis=0)
    tot = gather_small("sum_small", small, True)
    loss = tot[8, 0]
    norm_w = jnp.concatenate([w[k].reshape(1, d) for k in NORMS] + [jnp.zeros((3, d), F32)], axis=0)
    norm_m = jnp.concatenate([m[k].reshape(1, d) for k in NORMS] + [jnp.zeros((3, d), F32)], axis=0)
    norm_v = jnp.concatenate([v[k].reshape(1, d) for k in NORMS] + [jnp.ones((3, d), F32)], axis=0)
    norm_g = jnp.concatenate([tot[0:5], jnp.zeros((3, d), F32)], axis=0)
    nd, nm, nv = adamw("adamw_norms", norm_w, norm_g, norm_m, norm_v)
    for r, k in enumerate(NORMS):
        grad[k] = norm_g[r].reshape(w[k].shape)
        delta[k], new_m[k], new_v[k] = (a[r].reshape(w[k].shape) for a in (nd, nm, nv))
    cs = d // N_CHIPS
    gcw = lax.dynamic_slice(tot[5:8], (0, chip * cs), (3, cs))
    cd, cm, cv = adamw("adamw_conv_w", _pad_rows(w["conv_w"], 8), _pad_rows(gcw, 8), _pad_rows(m["conv_w"], 8),
                       jnp.concatenate([v["conv_w"], jnp.ones((5, cs), F32)], axis=0))
    grad["conv_w"], delta["conv_w"], new_m["conv_w"], new_v["conv_w"] = gcw, cd[:3], cm[:3], cv[:3]
    return loss, dx, grad, delta, new_m, new_v


def kernel(x, p, ffn1_norm, ffn1_w_in, ffn1_w_out, mix_norm, w_mix_in, conv_w, w_conv_out, w_attn_out, w_mix_out, ffn2_norm, ffn2_w_in, ffn2_w_out, ple_norm, w_ple_gate, w_ple_proj, final_norm, loss_target, m_ffn1_norm, m_ffn1_w_in, m_ffn1_w_out, m_mix_norm, m_w_mix_in, m_conv_w, m_w_conv_out, m_w_attn_out, m_w_mix_out, m_ffn2_norm, m_ffn2_w_in, m_ffn2_w_out, m_ple_norm, m_w_ple_gate, m_w_ple_proj, m_final_norm, v_ffn1_norm, v_ffn1_w_in, v_ffn1_w_out, v_mix_norm, v_w_mix_in, v_conv_w, v_w_conv_out, v_w_attn_out, v_w_mix_out, v_ffn2_norm, v_ffn2_w_in, v_ffn2_w_out, v_ple_norm, v_w_ple_gate, v_w_ple_proj, v_final_norm):
    ws = (ffn1_norm, ffn1_w_in, ffn1_w_out, mix_norm, w_mix_in, conv_w, w_conv_out, w_attn_out, w_mix_out, ffn2_norm,
          ffn2_w_in, ffn2_w_out, ple_norm, w_ple_gate, w_ple_proj, final_norm)
    ms = (m_ffn1_norm, m_ffn1_w_in, m_ffn1_w_out, m_mix_norm, m_w_mix_in, m_conv_w, m_w_conv_out, m_w_attn_out,
          m_w_mix_out, m_ffn2_norm, m_ffn2_w_in, m_ffn2_w_out, m_ple_norm, m_w_ple_gate, m_w_ple_proj, m_final_norm)
    vs = (v_ffn1_norm, v_ffn1_w_in, v_ffn1_w_out, v_mix_norm, v_w_mix_in, v_conv_w, v_w_conv_out, v_w_attn_out,
          v_w_mix_out, v_ffn2_norm, v_ffn2_w_in, v_ffn2_w_out, v_ple_norm, v_w_ple_gate, v_w_ple_proj, v_final_norm)
    assert x.shape[0] == 1 and p.shape[:2] == (1, 1), "one sequence and one layer per device"

    def strip(a):
        return a[0] if a.ndim == 3 or (a.ndim == 2 and a.shape[0] == 1) else a

    w = {k: strip(a) for k, a in zip(WEIGHTS, ws)}
    m = {k: strip(a) for k, a in zip(WEIGHTS, ms)}
    v = {k: strip(a) for k, a in zip(WEIGHTS, vs)}
    loss, dx, grad, delta, new_m, new_v = _step(x[0], p[0, 0], loss_target[0], w, m, v)
    shapes = [a.shape for a in ws]
    outs = [loss, dx[None]]
    for res in (grad, delta, new_m, new_v):
        outs += [res[k].reshape(s) for k, s in zip(WEIGHTS, shapes)]
    return tuple(outs)
```

```python
import functools
import math

import jax
import jax.numpy as jnp
from jax import lax
from jax.experimental import pallas as pl
from jax.experimental.pallas import tpu as pltpu

F32 = jnp.float32
BF16 = jnp.bfloat16
MESH = pl.DeviceIdType.MESH
ANY = pl.BlockSpec(memory_space=pl.ANY)

HEAD_DIM = 128
NORM_EPS = 1e-6
N_CHIPS = 4
N_DEV = 8
BF16_ROWS = 16
VMEM_LIMIT = 56 * 1024 * 1024
STICK_EXIT = 110.0

ADAM_LR = 0.001
ADAM_B1 = 0.9
ADAM_B2 = 0.999
ADAM_EPS = 1e-08
ADAM_WD = 0.01
ADAM_STEP = 10

NN = (((1,), (0,)), ((), ()))
NT = (((1,), (1,)), ((), ()))
TN = (((0,), (0,)), ((), ()))


def _params(sem=None, **kw):
    if sem is not None:
        kw["dimension_semantics"] = sem
    return pltpu.CompilerParams(vmem_limit_bytes=VMEM_LIMIT, **kw)


def _pcall(body, **kw):
    return pl.pallas_call(body, **kw)


def _tile(n, pref, mult=8):
    best = None
    for d in range(mult, min(n, pref) + 1, mult):
        if n % d == 0:
            best = d
    return best if best is not None else n


def _dot(a, b, dims):
    return lax.dot_general(a, b, dims, preferred_element_type=F32)


def _mm(name, a, b, out_sds, grid, a_spec, b_spec, o_spec, dims, acc_shape, res=None, alpha=1.0, comm=None):
    nk = grid[2]
    n_in = 2 + (res is not None)
    n_cin = len(comm.ins) if comm else 0
    n_cout = len(comm.outs) if comm else 0
    steps = grid[0] * grid[1] * grid[2]

    def body(*refs):
        a_ref, b_ref = refs[:2]
        r_ref = refs[2] if res is not None else None
        c_ins = refs[n_in:n_in + n_cin]
        o_ref = refs[n_in + n_cin]
        c_outs = refs[n_in + n_cin + 1:n_in + n_cin + 1 + n_cout]
        scratch = refs[n_in + n_cin + 1 + n_cout:]
        sems = scratch[0 if nk == 1 else 1:]
        step = (pl.program_id(0) * grid[1] + pl.program_id(1)) * grid[2] + pl.program_id(2)

        if comm:
            @pl.when(step == 0)
            def _():
                comm.first(c_ins, c_outs, sems)

        def finish(r):
            if alpha != 1.0:
                r = r * alpha
            if r_ref is not None:
                r = r_ref[...] + r
            o_ref[...] = r.astype(o_ref.dtype)

        part = _dot(a_ref[...].astype(BF16), b_ref[...].astype(BF16), dims)
        if nk == 1:
            finish(part)
        else:
            acc_ref = scratch[0]
            kk = pl.program_id(2)

            @pl.when(kk == 0)
            def _():
                acc_ref[...] = part

            @pl.when(kk > 0)
            def _():
                acc_ref[...] += part

            @pl.when(kk == nk - 1)
            def _():
                finish(acc_ref[...])

        if comm:
            @pl.when(step == (3 * steps) // 4)
            def _():
                comm.mid(c_ins, c_outs, sems)

            @pl.when(step == steps - 1)
            def _():
                comm.last(c_ins, c_outs, sems)

    in_specs = [a_spec, b_spec]
    args = [a, b]
    if res is not None:
        in_specs.append(o_spec)
        args.append(res)
    scratch = [] if nk == 1 else [pltpu.VMEM(acc_shape, F32)]
    if not comm:
        return _pcall(body, name=name, out_shape=out_sds, grid=grid, in_specs=in_specs, out_specs=o_spec,
                      scratch_shapes=scratch,
                      compiler_params=_params(("parallel", "parallel", "arbitrary")))(*args)
    outs = _pcall(body, name=name, out_shape=[out_sds] + comm.outs, grid=grid,
                  in_specs=in_specs + [ANY] * n_cin, out_specs=[o_spec] + [ANY] * n_cout,
                  input_output_aliases={n_in + k: 1 + v for k, v in comm.aliases.items()},
                  scratch_shapes=scratch + comm.sems,
                  compiler_params=_params(("arbitrary", "arbitrary", "arbitrary")))(*args, *comm.ins)
    return outs[0], list(outs[1:])


def mm_nn(name, a, w, out_dtype, tm, res=None, alpha=1.0):
    m, k = a.shape
    n = w.shape[1]
    return _mm(name, a, w, jax.ShapeDtypeStruct((m, n), out_dtype), (m // tm, 1, 1),
               pl.BlockSpec((tm, k), lambda i, j, r: (i, 0)),
               pl.BlockSpec((k, n), lambda i, j, r: (0, 0)),
               pl.BlockSpec((tm, n), lambda i, j, r: (i, 0)), NN, None, res=res, alpha=alpha)


def mm_nn_stacked(name, a, w4, out_dtype, tm, tn, j0=0, nj=None, comm=None):
    m, k = a.shape
    cs = w4.shape[2]
    per = cs // tn
    nj = N_CHIPS * per - j0 if nj is None else nj
    return _mm(name, a, w4, jax.ShapeDtypeStruct((m, nj * tn), out_dtype), (m // tm, nj, 1),
               pl.BlockSpec((tm, k), lambda i, j, r: (i, 0)),
               pl.BlockSpec((None, k, tn), lambda i, j, r: ((j + j0) // per, 0, (j + j0) % per)),
               pl.BlockSpec((tm, tn), lambda i, j, r: (i, j)), NN, None, comm=comm)


def mm_nt(name, dy, w, out_dtype, tm, tko):
    m, n = dy.shape
    k = w.shape[0]
    return _mm(name, dy, w, jax.ShapeDtypeStruct((m, k), out_dtype), (m // tm, k // tko, 1),
               pl.BlockSpec((tm, n), lambda i, j, r: (i, 0)),
               pl.BlockSpec((tko, n), lambda i, j, r: (j, 0)),
               pl.BlockSpec((tm, tko), lambda i, j, r: (i, j)), NT, None)


def mm_nt_stacked(name, dy, w4, out_dtype, tm, tn, comm=None):
    m = dy.shape[0]
    k, cs = w4.shape[1], w4.shape[2]
    per = cs // tn
    return _mm(name, dy, w4, jax.ShapeDtypeStruct((m, k), out_dtype), (m // tm, 1, N_CHIPS * per),
               pl.BlockSpec((tm, tn), lambda i, j, r: (i, r)),
               pl.BlockSpec((None, k, tn), lambda i, j, r: (r // per, 0, r % per)),
               pl.BlockSpec((tm, k), lambda i, j, r: (i, 0)), NT, (tm, k), comm=comm)


def mm_tn_rows(name, xa, dy, tt):
    t, k = xa.shape
    n = dy.shape[1]
    tkr = k // 2
    return _mm(name, xa, dy, jax.ShapeDtypeStruct((k, n), BF16), (k // tkr, 1, t // tt),
               pl.BlockSpec((tt, tkr), lambda i, j, r: (r, i)),
               pl.BlockSpec((tt, n), lambda i, j, r: (r, 0)),
               pl.BlockSpec((tkr, n), lambda i, j, r: (i, 0)), TN, (tkr, n))


def mm_tn_whole(name, xa, dy, tt):
    t, k = xa.shape
    n = dy.shape[1]
    return _mm(name, xa, dy, jax.ShapeDtypeStruct((k, n), BF16), (1, 1, t // tt),
               pl.BlockSpec((tt, k), lambda i, j, r: (r, 0)),
               pl.BlockSpec((tt, n), lambda i, j, r: (r, 0)),
               pl.BlockSpec((k, n), lambda i, j, r: (0, 0)), TN, (k, n))


def mm_tn_cols(name, xa, dy, tt, comm=None):
    t, k = xa.shape
    cs = dy.shape[1] // N_CHIPS
    pr = k // 2
    return _mm(name, xa, dy, jax.ShapeDtypeStruct((N_CHIPS, 2, pr, cs), BF16), (2, N_CHIPS, t // tt),
               pl.BlockSpec((tt, pr), lambda i, j, r: (r, i)),
               pl.BlockSpec((tt, cs), lambda i, j, r: (r, j)),
               pl.BlockSpec((None, None, pr, cs), lambda i, j, r: (j, i, 0, 0)), TN, (pr, cs), comm=comm)


def _rows(tt, w, col=0):
    return pl.BlockSpec((tt, w), lambda i: (i, col))


def _whole(shape):
    return pl.BlockSpec(shape, lambda i: (0,) * len(shape))


def _rstd(h):
    return lax.rsqrt(jnp.mean(h * h, axis=-1, keepdims=True) + NORM_EPS)


def rms_fwd(name, h, g, tt):
    t, d = h.shape

    def body(h_ref, g_ref, o_ref):
        hv = h_ref[...]
        o_ref[...] = (hv * _rstd(hv) * g_ref[...]).astype(o_ref.dtype)

    return _pcall(body, name=name, out_shape=jax.ShapeDtypeStruct((t, d), BF16), grid=(t // tt,),
                  in_specs=[_rows(tt, d), _whole((1, d))], out_specs=_rows(tt, d),
                  compiler_params=_params(("parallel",)))(h, g)


def rms_bwd(name, h, g, dn, dres, alpha, tt):
    t, d = h.shape

    def body(h_ref, g_ref, dn_ref, dr_ref, dh_ref, dhb_ref, dg_ref):
        hv = h_ref[...]
        hn = hv * _rstd(hv)
        dnv = dn_ref[...]
        gy = dnv * g_ref[...]
        dh = dr_ref[...] + _rstd(hv) * (gy - hn * jnp.mean(gy * hn, axis=-1, keepdims=True))
        dh_ref[...] = dh
        dhb_ref[...] = (alpha * dh).astype(BF16)

        @pl.when(pl.program_id(0) == 0)
        def _():
            dg_ref[...] = jnp.zeros_like(dg_ref)

        dg_ref[...] += jnp.sum(dnv * hn, axis=0, keepdims=True)

    return _pcall(body, name=name,
                  out_shape=(jax.ShapeDtypeStruct((t, d), F32), jax.ShapeDtypeStruct((t, d), BF16),
                             jax.ShapeDtypeStruct((1, d), F32)),
                  grid=(t // tt,),
                  in_specs=[_rows(tt, d), _whole((1, d)), _rows(tt, d), _rows(tt, d)],
                  out_specs=(_rows(tt, d), _rows(tt, d), _whole((1, d))),
                  compiler_params=_params(("arbitrary",)))(h, g, dn, dres)


def swiglu_fwd(name, a, tt):
    t, f2 = a.shape
    f = f2 // 2

    def body(a_ref, o_ref):
        gate = a_ref[:, :f]
        up = a_ref[:, f:]
        o_ref[...] = (gate * jax.nn.sigmoid(gate) * up).astype(o_ref.dtype)

    return _pcall(body, name=name, out_shape=jax.ShapeDtypeStruct((t, f), BF16), grid=(t // tt,),
                  in_specs=[_rows(tt, f2)], out_specs=_rows(tt, f),
                  compiler_params=_params(("parallel",)))(a)


def swiglu_bwd(name, a, ds, tt):
    t, f2 = a.shape
    f = f2 // 2

    def body(a_ref, ds_ref, o_ref):
        gate = a_ref[:, :f]
        up = a_ref[:, f:]
        dsv = ds_ref[...]
        sg = jax.nn.sigmoid(gate)
        o_ref[:, :f] = (dsv * up * sg * (1.0 + gate * (1.0 - sg))).astype(o_ref.dtype)
        o_ref[:, f:] = (dsv * gate * sg).astype(o_ref.dtype)

    return _pcall(body, name=name, out_shape=jax.ShapeDtypeStruct((t, f2), BF16), grid=(t // tt,),
                  in_specs=[_rows(tt, f2), _rows(tt, f)], out_specs=_rows(tt, f2),
                  compiler_params=_params(("parallel",)))(a, ds)


def gate_fwd(name, gates, yc, ya, tt):
    t, d = yc.shape

    def body(g_ref, yc_ref, ya_ref, o_ref):
        o_ref[...] = (jax.nn.sigmoid(g_ref[:, :d]) * yc_ref[...]
                      + jax.nn.sigmoid(g_ref[:, d:]) * ya_ref[...]).astype(o_ref.dtype)

    return _pcall(body, name=name, out_shape=jax.ShapeDtypeStruct((t, d), BF16), grid=(t // tt,),
                  in_specs=[_rows(tt, 2 * d), _rows(tt, d), _rows(tt, d)], out_specs=_rows(tt, d),
                  compiler_params=_params(("parallel",)))(gates, yc, ya)


def gate_bwd(name, dm, gates, yc, ya, tt):
    t, d = yc.shape

    def body(dm_ref, g_ref, yc_ref, ya_ref, dyc_ref, dya_ref, dg_ref):
        dmv = dm_ref[...]
        sc = jax.nn.sigmoid(g_ref[:, :d])
        sa = jax.nn.sigmoid(g_ref[:, d:])
        dyc_ref[...] = (dmv * sc).astype(BF16)
        dya_ref[...] = (dmv * sa).astype(BF16)
        dg_ref[:, :d] = (dmv * yc_ref[...] * sc * (1.0 - sc)).astype(BF16)
        dg_ref[:, d:] = (dmv * ya_ref[...] * sa * (1.0 - sa)).astype(BF16)

    return _pcall(body, name=name,
                  out_shape=(jax.ShapeDtypeStruct((t, d), BF16), jax.ShapeDtypeStruct((t, d), BF16),
                             jax.ShapeDtypeStruct((t, 2 * d), BF16)),
                  grid=(t // tt,),
                  in_specs=[_rows(tt, d), _rows(tt, 2 * d), _rows(tt, d), _rows(tt, d)],
                  out_specs=(_rows(tt, d), _rows(tt, d), _rows(tt, 2 * d)),
                  compiler_params=_params(("parallel",)))(dm, gates, yc, ya)


def _shift_down(cur, prev8, s):
    tt = cur.shape[0]
    rolled = pltpu.roll(cur, s, 0)
    row8 = lax.broadcasted_iota(jnp.int32, prev8.shape, 0)
    first8 = jnp.where(row8 < s, pltpu.roll(prev8, s, 0), rolled[:8])
    return jnp.concatenate([first8, rolled[8:]], axis=0) if tt > 8 else first8


def _shift_up(cur, next8, s):
    tt = cur.shape[0]
    rolled = pltpu.roll(cur, tt - s, 0)
    row8 = lax.broadcasted_iota(jnp.int32, next8.shape, 0)
    last8 = jnp.where(row8 >= 8 - s, pltpu.roll(next8, 8 - s, 0), rolled[tt - 8:])
    return jnp.concatenate([rolled[:tt - 8], last8], axis=0) if tt > 8 else last8


def _prev8(tt, d, col):
    return pl.BlockSpec((8, d), lambda i: (jnp.maximum(i * (tt // 8) - 1, 0), col))


def _next8(tt, d, col, t):
    return pl.BlockSpec((8, d), lambda i: (jnp.minimum((i + 1) * (tt // 8), t // 8 - 1), col))


def conv_fwd(name, cbx, cw8, tt):
    t, d3 = cbx.shape
    d = d3 // 3

    def body(cb_ref, cc_ref, cx_ref, pc_ref, px_ref, w_ref, o_ref):
        has_prev = (pl.program_id(0) > 0).astype(F32)
        cc = cc_ref[...] * cx_ref[...]
        prev = pc_ref[...] * px_ref[...] * has_prev
        w = w_ref[...]
        conv = w[0:1] * _shift_down(cc, prev, 2) + w[1:2] * _shift_down(cc, prev, 1) + w[2:3] * cc
        o_ref[...] = (cb_ref[...] * conv).astype(o_ref.dtype)

    return _pcall(body, name=name, out_shape=jax.ShapeDtypeStruct((t, d), BF16), grid=(t // tt,),
                  in_specs=[_rows(tt, d, 0), _rows(tt, d, 1), _rows(tt, d, 2), _prev8(tt, d, 1), _prev8(tt, d, 2),
                            _whole((8, d))],
                  out_specs=_rows(tt, d), compiler_params=_params(("parallel",)))(cbx, cbx, cbx, cbx, cbx, cw8)


def conv_bwd(name, dyc, cbx, cw8, tt):
    t, d3 = cbx.shape
    d = d3 // 3
    n = t // tt

    def body(dy_ref, cb_ref, cc_ref, cx_ref, pc_ref, px_ref, ndy_ref, ncb_ref, w_ref, o_ref, dw_ref):
        i = pl.program_id(0)
        has_prev = (i > 0).astype(F32)
        has_next = (i < n - 1).astype(F32)
        cb = cb_ref[...]
        cc = cc_ref[...] * cx_ref[...]
        prev = pc_ref[...] * px_ref[...] * has_prev
        w = w_ref[...]
        cc1 = _shift_down(cc, prev, 1)
        cc2 = _shift_down(cc, prev, 2)
        conv = w[0:1] * cc2 + w[1:2] * cc1 + w[2:3] * cc
        dyv = dy_ref[...]
        dconv = dyv * cb
        dnext = ndy_ref[...] * ncb_ref[...] * has_next
        dcc = w[2:3] * dconv + w[1:2] * _shift_up(dconv, dnext, 1) + w[0:1] * _shift_up(dconv, dnext, 2)
        o_ref[:, :d] = (dyv * conv).astype(BF16)
        o_ref[:, d:2 * d] = (dcc * cx_ref[...]).astype(BF16)
        o_ref[:, 2 * d:] = (dcc * cc_ref[...]).astype(BF16)

        @pl.when(i == 0)
        def _():
            dw_ref[...] = jnp.zeros_like(dw_ref)

        dw_ref[0:1, :] += jnp.sum(dconv * cc2, axis=0, keepdims=True)
        dw_ref[1:2, :] += jnp.sum(dconv * cc1, axis=0, keepdims=True)
        dw_ref[2:3, :] += jnp.sum(dconv * cc, axis=0, keepdims=True)

    return _pcall(body, name=name,
                  out_shape=(jax.ShapeDtypeStruct((t, d3), BF16), jax.ShapeDtypeStruct((8, d), F32)),
                  grid=(n,),
                  in_specs=[_rows(tt, d), _rows(tt, d, 0), _rows(tt, d, 1), _rows(tt, d, 2),
                            _prev8(tt, d, 1), _prev8(tt, d, 2), _next8(tt, d, 0, t), _next8(tt, d, 0, t),
                            _whole((8, d))],
                  out_specs=(_rows(tt, d3), _whole((8, d))),
                  compiler_params=_params(("arbitrary",)))(dyc, cbx, cbx, cbx, cbx, cbx, dyc, cbx, cw8)


def tail(name, h3, zg, pp, tgt, gf, tt):
    t, d = h3.shape

    def body(h_ref, zg_ref, pp_ref, tg_ref, gf_ref, dh_ref, dpp_ref, dzg_ref, dgf_ref, loss_ref):
        pg = jax.nn.sigmoid(zg_ref[...])
        ppv = pp_ref[...]
        h4 = h_ref[...] + pg * ppv
        r4 = _rstd(h4)
        hn = h4 * r4
        gfv = gf_ref[...]
        err = hn * gfv - tg_ref[...]
        dy = err * (1.0 / d)
        gy = dy * gfv
        dh4 = r4 * (gy - hn * jnp.mean(gy * hn, axis=-1, keepdims=True))
        dh_ref[...] = dh4
        dpp_ref[...] = (dh4 * pg).astype(BF16)
        dzg_ref[...] = (dh4 * ppv * pg * (1.0 - pg)).astype(BF16)

        @pl.when(pl.program_id(0) == 0)
        def _():
            dgf_ref[...] = jnp.zeros_like(dgf_ref)
            loss_ref[...] = jnp.zeros_like(loss_ref)

        dgf_ref[...] += jnp.sum(dy * hn, axis=0, keepdims=True)
        tok = jnp.mean(err * err, axis=-1, keepdims=True)
        loss_ref[...] += 0.5 * jnp.sum(tok, axis=0, keepdims=True) * jnp.ones((1, loss_ref.shape[1]), F32)

    return _pcall(body, name=name,
                  out_shape=(jax.ShapeDtypeStruct((t, d), F32), jax.ShapeDtypeStruct((t, d), BF16),
                             jax.ShapeDtypeStruct((t, d), BF16), jax.ShapeDtypeStruct((1, d), F32),
                             jax.ShapeDtypeStruct((1, d), F32)),
                  grid=(t // tt,),
                  in_specs=[_rows(tt, d)] * 4 + [_whole((1, d))],
                  out_specs=(_rows(tt, d), _rows(tt, d), _rows(tt, d), _whole((1, d)), _whole((1, d))),
                  compiler_params=_params(("arbitrary",)))(h3, zg, pp, tgt, gf)


SCALE = 1.0 / math.sqrt(HEAD_DIM)


def _log_stick(z):
    return -(jnp.maximum(z, 0.0) + jnp.log(1.0 + jnp.exp(-jnp.abs(z))))


def _tri_sum(x, tri):
    hi = x.astype(BF16)
    lo = (x - hi.astype(F32)).astype(BF16)
    return _dot(hi, tri, NN) + _dot(lo, tri, NN)


def _sb_pair(q, k_d, k_p, below, upper, has_prev):
    z_d = _dot(q, k_d, NT) * SCALE
    z_p = _dot(q, k_p, NT) * SCALE
    cum_d = _tri_sum(jnp.where(below, _log_stick(z_d), 0.0), upper)
    cum_p = _tri_sum(jnp.where(has_prev, _log_stick(z_p), 0.0), upper)
    c_d = cum_d[:, 0:1]
    a_d = jnp.exp(jnp.where(below, z_d + cum_d, -1e30))
    a_p = jnp.exp(jnp.where(has_prev, z_p + cum_p + c_d, -1e30))
    return z_d, z_p, a_d, a_p, c_d + cum_p[:, 0:1]


def _sb_far(q, kj, upper, c):
    z = _dot(q, kj, NT) * SCALE
    cum = _tri_sum(_log_stick(z), upper)
    return z, jnp.exp(z + cum + c), c + cum[:, 0:1]


def _block_rows(j, tq):
    return pl.ds(pl.multiple_of(j * tq, tq), tq)


def attn_fwd(name, qkv, tq):
    t, d3 = qkv.shape
    d = d3 // 3
    nh = d // HEAD_DIM
    nq = t // tq

    def body(q_ref, k_ref, v_ref, o_ref):
        i = pl.program_id(1)
        q = q_ref[...]
        row = lax.broadcasted_iota(jnp.int32, (tq, tq), 0)
        col = lax.broadcasted_iota(jnp.int32, (tq, tq), 1)
        upper = (row >= col).astype(BF16)
        rows_d = _block_rows(i, tq)
        rows_p = _block_rows(jnp.maximum(i - 1, 0), tq)
        _, _, a_d, a_p, c = _sb_pair(q, k_ref[rows_d, :], k_ref[rows_p, :], col < row, upper, i > 0)
        acc = _dot(a_d.astype(BF16), v_ref[rows_d, :], NN) + _dot(a_p.astype(BF16), v_ref[rows_p, :], NN)

        def cond(st):
            return jnp.logical_and(st[0] >= 0, jnp.max(st[1]) > -STICK_EXIT)

        def step(st):
            rows = _block_rows(st[0], tq)
            _, a, c2 = _sb_far(q, k_ref[rows, :], upper, st[1])
            return st[0] - 1, c2, st[2] + _dot(a.astype(BF16), v_ref[rows, :], NN)

        _, _, acc = lax.while_loop(cond, step, (i - 2, c, acc))
        o_ref[...] = acc.astype(o_ref.dtype)

    return _pcall(body, name=name, out_shape=jax.ShapeDtypeStruct((t, d), BF16), grid=(nh, nq),
                  in_specs=[pl.BlockSpec((tq, HEAD_DIM), lambda h, i: (i, h)),
                            pl.BlockSpec((t, HEAD_DIM), lambda h, i: (0, nh + h)),
                            pl.BlockSpec((t, HEAD_DIM), lambda h, i: (0, 2 * nh + h))],
                  out_specs=pl.BlockSpec((tq, HEAD_DIM), lambda h, i: (i, h)),
                  compiler_params=_params(("parallel", "arbitrary")))(qkv, qkv, qkv)


def attn_bwd(name, qkv, do, tq):
    t, d3 = qkv.shape
    d = d3 // 3
    nh = d // HEAD_DIM
    nq = t // tq

    def body(q_ref, k_ref, v_ref, do_ref, dq_ref, dk_ref, dv_ref, dk_acc, dv_acc, g_buf, z_buf):
        i = pl.program_id(1)

        @pl.when(i == 0)
        def _():
            dk_acc[...] = jnp.zeros_like(dk_acc)
            dv_acc[...] = jnp.zeros_like(dv_acc)

        q = q_ref[...]
        dov = do_ref[...]
        row = lax.broadcasted_iota(jnp.int32, (tq, tq), 0)
        col = lax.broadcasted_iota(jnp.int32, (tq, tq), 1)
        below = col < row
        has_prev = i > 0
        upper = (row >= col).astype(BF16)
        lower = (row <= col).astype(BF16)
        rows_d = _block_rows(i, tq)
        rows_p = _block_rows(jnp.maximum(i - 1, 0), tq)

        z_d, z_p, a_d, a_p, c = _sb_pair(q, k_ref[rows_d, :], k_ref[rows_p, :], below, upper, has_prev)
        g_d = _dot(dov, v_ref[rows_d, :], NT) * a_d
        g_p = _dot(dov, v_ref[rows_p, :], NT) * a_p
        dv_acc[rows_d, :] += _dot(a_d.astype(BF16), dov, TN)
        dv_acc[rows_p, :] += _dot(a_p.astype(BF16), dov, TN)

        def cond(st):
            return jnp.logical_and(st[0] >= 0, jnp.max(st[1]) > -STICK_EXIT)

        def step(st):
            j = st[0]
            rows = _block_rows(j, tq)
            z, a, c2 = _sb_far(q, k_ref[rows, :], upper, st[1])
            g_buf[i - j] = _dot(dov, v_ref[rows, :], NT) * a
            z_buf[i - j] = z
            dv_acc[rows, :] += _dot(a.astype(BF16), dov, TN)
            return j - 1, c2

        j_stop, _ = lax.while_loop(cond, step, (i - 2, c))

        def far(j, st):
            run, dq = st
            rows = _block_rows(j, tq)
            g = g_buf[i - j]
            dz = (g - jax.nn.sigmoid(z_buf[i - j]) * (run + _tri_sum(g, lower))).astype(BF16)
            dk_acc[rows, :] += _dot(dz, q, TN)
            return run + jnp.sum(g, axis=1, keepdims=True), dq + _dot(dz, k_ref[rows, :], NN)

        run, dq = lax.fori_loop(j_stop + 1, i - 1, far,
                                (jnp.zeros((tq, 1), F32), jnp.zeros((tq, HEAD_DIM), F32)))
        p_p = run + _tri_sum(g_p, lower)
        p_d = run + jnp.sum(g_p, axis=1, keepdims=True) + _tri_sum(g_d, lower)
        dz_p = jnp.where(has_prev, g_p - jax.nn.sigmoid(z_p) * p_p, 0.0).astype(BF16)
        dz_d = jnp.where(below, g_d - jax.nn.sigmoid(z_d) * p_d, 0.0).astype(BF16)
        dk_acc[rows_p, :] += _dot(dz_p, q, TN)
        dk_acc[rows_d, :] += _dot(dz_d, q, TN)
        dq = dq + _dot(dz_p, k_ref[rows_p, :], NN) + _dot(dz_d, k_ref[rows_d, :], NN)
        dq_ref[...] = (dq * SCALE).astype(BF16)

        @pl.when(i == nq - 1)
        def _():
            dk_ref[...] = (dk_acc[...] * SCALE).astype(BF16)
            dv_ref[...] = dv_acc[...].astype(BF16)

    blk = pl.BlockSpec((tq, HEAD_DIM), lambda h, i: (i, h))
    col_h = pl.BlockSpec((t, HEAD_DIM), lambda h, i: (0, h))
    out = jax.ShapeDtypeStruct((t, d), BF16)
    return _pcall(body, name=name, out_shape=(out, out, out), grid=(nh, nq),
                  in_specs=[blk,
                            pl.BlockSpec((t, HEAD_DIM), lambda h, i: (0, nh + h)),
                            pl.BlockSpec((t, HEAD_DIM), lambda h, i: (0, 2 * nh + h)),
                            blk],
                  out_specs=(blk, col_h, col_h),
                  scratch_shapes=[pltpu.VMEM((t, HEAD_DIM), F32), pltpu.VMEM((t, HEAD_DIM), F32),
                                  pltpu.VMEM((nq, tq, tq), F32), pltpu.VMEM((nq, tq, tq), F32)],
                  compiler_params=_params(("parallel", "arbitrary")))(qkv, qkv, qkv, do)


def _place():
    x, y, c = lax.axis_index("x"), lax.axis_index("y"), lax.axis_index("c")
    chips = [(1 - x, y), (x, 1 - y), (1 - x, 1 - y)]
    return x, y, c, chips


def _remote(src, dst, send_sem, recv_sem, dev):
    return pltpu.make_async_remote_copy(src_ref=src, dst_ref=dst, send_sem=send_sem, recv_sem=recv_sem,
                                        device_id=dev, device_id_type=MESH)


def place_shard(name, w, chip):
    r, cdim = w.shape
    tr = _tile(r, max(BF16_ROWS, (1 << 19) // cdim), BF16_ROWS)

    def body(chip_ref, w_ref, o_ref):
        o_ref[...] = w_ref[...].astype(BF16)

    spec = pltpu.PrefetchScalarGridSpec(
        num_scalar_prefetch=1, grid=(r // tr,),
        in_specs=[pl.BlockSpec((tr, cdim), lambda i, s: (i, 0))],
        out_specs=pl.BlockSpec((None, tr, cdim), lambda i, s: (s[0], i, 0)))
    return _pcall(body, name=name, out_shape=jax.ShapeDtypeStruct((N_CHIPS, r, cdim), BF16), grid_spec=spec,
                  compiler_params=_params(("parallel",)))(chip, w)


class Comm:
    def __init__(self, ins, outs, aliases, sems, first, mid, last):
        self.ins, self.outs, self.aliases, self.sems = list(ins), list(outs), dict(aliases), list(sems)
        self.first, self.mid, self.last = first, mid, last


def run_comm(name, comm):
    ni, no = len(comm.ins), len(comm.outs)

    def body(*refs):
        ins, outs, sems = refs[:ni], refs[ni:ni + no], refs[ni + no:]
        comm.first(ins, outs, sems)
        comm.mid(ins, outs, sems)
        comm.last(ins, outs, sems)

    return _pcall(body, name=name, out_shape=comm.outs, in_specs=[ANY] * ni, out_specs=[ANY] * no,
                  input_output_aliases=comm.aliases, scratch_shapes=comm.sems, compiler_params=_params())(*comm.ins)


def gather_comm(bufs):
    n = len(bufs)

    def half(out, w, which):
        pr = out[w].shape[1] // 2
        return pl.ds(pl.multiple_of(which * pr, BF16_ROWS), pr)

    def first(ins, out, sems):
        isend, irecv, _, _ = sems
        x, y, c, chips = _place()
        for w in range(n):
            mine = out[w].at[2 * x + y, half(out, w, c)]
            for j, (cx, cy) in enumerate(chips):
                _remote(mine, mine, isend.at[3 * w + j], irecv.at[3 * w + j], (cx, cy, c)).start()

    def mid(ins, out, sems):
        isend, irecv, dsend, drecv = sems
        x, y, c, chips = _place()
        sib = (x, y, 1 - c)
        for w in range(n):
            for j, (cx, cy) in enumerate(chips):
                landed = out[w].at[2 * cx + cy, half(out, w, c)]
                _remote(landed, landed, isend.at[3 * w + j], irecv.at[3 * w + j], sib).wait_recv()
                _remote(landed, landed, dsend.at[3 * w + j], drecv.at[3 * w + j], sib).start()

    def last(ins, out, sems):
        isend, irecv, dsend, drecv = sems
        x, y, c, chips = _place()
        sib = (x, y, 1 - c)
        for w in range(n):
            for j, (cx, cy) in enumerate(chips):
                landed = out[w].at[2 * cx + cy, half(out, w, 1 - c)]
                _remote(landed, landed, dsend.at[3 * w + j], drecv.at[3 * w + j], sib).wait_recv()
        for w in range(n):
            sent = out[w].at[0, half(out, w, c)]
            for j in range(3):
                _remote(sent, sent, isend.at[3 * w + j], irecv.at[3 * w + j], sib).wait_send()
                _remote(sent, sent, dsend.at[3 * w + j], drecv.at[3 * w + j], sib).wait_send()

    return Comm(bufs, [jax.ShapeDtypeStruct(s.shape, s.dtype) for s in bufs], {w: w for w in range(n)},
                [pltpu.SemaphoreType.DMA((3 * n,))] * 4, first, mid, last)


def _nothing(ins, outs, sems):
    return None


def exchange_comm(pieces):
    n = len(pieces)

    def copies(src, out, sems):
        x, y, c, _ = _place()
        return [_remote(src[w].at[k, 1 - c], out[w].at[k], sems[0].at[N_CHIPS * w + k], sems[1].at[N_CHIPS * w + k],
                        (x, y, 1 - c)) for w in range(n) for k in range(N_CHIPS)]

    def first(src, out, sems):
        for cp in copies(src, out, sems):
            cp.start()

    def last(src, out, sems):
        for cp in copies(src, out, sems):
            cp.wait()

    return Comm(pieces, [jax.ShapeDtypeStruct((N_CHIPS,) + s.shape[2:], s.dtype) for s in pieces], {},
                [pltpu.SemaphoreType.DMA((N_CHIPS * n,))] * 2, first, _nothing, last)


def scatter_comm(parts):
    n = len(parts)

    def copies(src, out, sems):
        x, y, c, chips = _place()
        return [_remote(src[w].at[2 * cx + cy], out[w].at[j], sems[0].at[3 * w + j], sems[1].at[3 * w + j], (cx, cy, c))
                for w in range(n) for j, (cx, cy) in enumerate(chips)]

    def first(src, out, sems):
        for cp in copies(src, out, sems):
            cp.start()

    def last(src, out, sems):
        for cp in copies(src, out, sems):
            cp.wait()

    return Comm(parts, [jax.ShapeDtypeStruct((3,) + s.shape[1:], s.dtype) for s in parts], {},
                [pltpu.SemaphoreType.DMA((3 * n,))] * 2, first, _nothing, last)


def share_comm(halves):
    n = len(halves)

    def first(ins, buf, sems):
        x, y, c, _ = _place()
        for w in range(n):
            _remote(buf[w].at[c], buf[w].at[c], sems[0].at[w], sems[1].at[w], (x, y, 1 - c)).start()

    def last(ins, buf, sems):
        x, y, c, _ = _place()
        for w in range(n):
            landed = buf[w].at[1 - c]
            _remote(landed, landed, sems[0].at[w], sems[1].at[w], (x, y, 1 - c)).wait_recv()
        for w in range(n):
            _remote(buf[w].at[c], buf[w].at[c], sems[0].at[w], sems[1].at[w], (x, y, 1 - c)).wait_send()

    return Comm(halves, [jax.ShapeDtypeStruct(s.shape, s.dtype) for s in halves], {w: w for w in range(n)},
                [pltpu.SemaphoreType.DMA((n,))] * 2, first, _nothing, last)


def gather_small(name, blk, reduce):
    r, cdim = blk.shape

    def body(in_ref, out_ref, *rest):
        if reduce:
            buf, send_sem, recv_sem = rest
        else:
            buf = out_ref
            send_sem, recv_sem = rest
        x, y, c, _ = _place()
        me = 4 * x + 2 * y + c
        buf[me] = in_ref[...]
        peers = []
        for dx in range(2):
            for dy in range(2):
                for dc in range(2):
                    if dx or dy or dc:
                        peers.append((dx, dy, dc))
        copies = []
        for s, (dx, dy, dc) in enumerate(peers):
            cp = _remote(in_ref, buf.at[me], send_sem.at[s], recv_sem.at[s],
                         ((1 - x if dx else x), (1 - y if dy else y), (1 - c if dc else c)))
            cp.start()
            copies.append(cp)
        for s, (dx, dy, dc) in enumerate(peers):
            px, py, pc_ = (1 - x if dx else x), (1 - y if dy else y), (1 - c if dc else c)
            landed = buf.at[4 * px + 2 * py + pc_]
            _remote(landed, landed, send_sem.at[s], recv_sem.at[s], (x, y, c)).wait_recv()
        for cp in copies:
            cp.wait_send()
        if reduce:
            tot = buf[0]
            for s in range(1, N_DEV):
                tot = tot + buf[s]
            out_ref[...] = tot

    vm = pl.BlockSpec(memory_space=pltpu.VMEM)
    out_shape = jax.ShapeDtypeStruct((r, cdim) if reduce else (N_DEV, r, cdim), F32)
    scratch = ([pltpu.VMEM((N_DEV, r, cdim), F32)] if reduce else []) + [pltpu.SemaphoreType.DMA((N_DEV - 1,))] * 2
    return _pcall(body, name=name, out_shape=out_shape, in_specs=[vm], out_specs=vm, scratch_shapes=scratch,
                  compiler_params=_params())(blk)


def sum_cores(name, own, got, place):
    _, _, pr, pc = own.shape
    tr = _tile(pr, max(BF16_ROWS, (1 << 19) // pc), BF16_ROWS)

    def body(place_ref, own_ref, got_ref, o_ref):
        o_ref[...] = (own_ref[...].astype(F32) + got_ref[...].astype(F32)).astype(o_ref.dtype)

    spec = pltpu.PrefetchScalarGridSpec(
        num_scalar_prefetch=1, grid=(N_CHIPS, pr // tr),
        in_specs=[pl.BlockSpec((None, None, tr, pc), lambda k, i, s: (k, s[1], i, 0)),
                  pl.BlockSpec((None, tr, pc), lambda k, i, s: (k, i, 0))],
        out_specs=pl.BlockSpec((None, tr, pc), lambda k, i, s: (k, i, 0)))
    return _pcall(body, name=name, out_shape=jax.ShapeDtypeStruct((N_CHIPS, pr, pc), BF16), grid_spec=spec,
                  compiler_params=_params(("parallel", "parallel")))(place, own, got)


def sum_chips(name, part, got, place):
    _, pr, pc = part.shape
    tr = _tile(pr, max(BF16_ROWS, (1 << 18) // pc), BF16_ROWS)

    def body(place_ref, part_ref, got_ref, o_ref):
        tot = part_ref[...].astype(F32)
        for j in range(3):
            tot = tot + got_ref[j].astype(F32)
        o_ref[...] = tot

    spec = pltpu.PrefetchScalarGridSpec(
        num_scalar_prefetch=1, grid=(pr // tr,),
        in_specs=[pl.BlockSpec((None, tr, pc), lambda i, s: (s[0], i, 0)),
                  pl.BlockSpec((3, tr, pc), lambda i, s: (0, i, 0))],
        out_specs=pl.BlockSpec((None, tr, pc), lambda i, s: (s[1], i, 0)))
    return _pcall(body, name=name, out_shape=jax.ShapeDtypeStruct((2, pr, pc), F32), grid_spec=spec,
                  compiler_params=_params(("parallel",)))(place, part, got)


def adamw(name, w, g, m, v):
    rows, cols = w.shape
    tr = _tile(rows, max(8, (1 << 18) // cols))
    c1 = 1.0 / (1.0 - ADAM_B1 ** ADAM_STEP)
    c2 = 1.0 / (1.0 - ADAM_B2 ** ADAM_STEP)

    def body(w_ref, g_ref, m_ref, v_ref, d_ref, nm_ref, nv_ref):
        gv = g_ref[...]
        nm = ADAM_B1 * m_ref[...] + (1.0 - ADAM_B1) * gv
        nv = ADAM_B2 * v_ref[...] + (1.0 - ADAM_B2) * (gv * gv)
        nm_ref[...] = nm
        nv_ref[...] = nv
        d_ref[...] = -ADAM_LR * ((nm * c1) / (jnp.sqrt(nv * c2) + ADAM_EPS) + ADAM_WD * w_ref[...])

    spec = pl.BlockSpec((tr, cols), lambda i: (i, 0))
    sds = jax.ShapeDtypeStruct((rows, cols), F32)
    return _pcall(body, name=name, out_shape=(sds, sds, sds), grid=(rows // tr,),
                  in_specs=[spec] * 4, out_specs=(spec, spec, spec),
                  compiler_params=_params(("parallel",)))(w, g, m, v)


MATS = ["ffn1_w_in", "ffn1_w_out", "w_mix_in", "w_conv_out", "w_attn_out", "w_mix_out", "ffn2_w_in", "ffn2_w_out",
        "w_ple_gate", "w_ple_proj"]
COL_SHARDED = {"ffn1_w_in", "w_mix_in", "ffn2_w_in", "w_ple_proj"}
NORMS = ["ffn1_norm", "mix_norm", "ffn2_norm", "ple_norm", "final_norm"]
WEIGHTS = ["ffn1_norm", "ffn1_w_in", "ffn1_w_out", "mix_norm", "w_mix_in", "conv_w", "w_conv_out", "w_attn_out",
           "w_mix_out", "ffn2_norm", "ffn2_w_in", "ffn2_w_out", "ple_norm", "w_ple_gate", "w_ple_proj", "final_norm"]


def _pad_rows(a, rows):
    return jnp.concatenate([a, jnp.zeros((rows - a.shape[0],) + a.shape[1:], a.dtype)], axis=0)


def _step(x, p, tgt, w, m, v):
    t, d = x.shape
    tt = _tile(t, 256)
    tm = _tile(t, 512)
    tq = _tile(t, 256)

    chip = 2 * lax.axis_index("x") + lax.axis_index("y")
    place = jnp.stack([chip, lax.axis_index("c")]).astype(jnp.int32)

    placed = {k: place_shard("place_" + k, w[k], place) for k in MATS}
    full = {}

    def keep(names, bufs):
        for k, buf in zip(names, bufs):
            full[k] = buf if k in COL_SHARDED else buf.reshape(-1, buf.shape[2])

    def gather_of(names):
        return gather_comm([placed[k] for k in names])

    keep(["ffn1_w_in"], run_comm("gather_first", gather_of(["ffn1_w_in"])))
    cw_all = gather_small("gather_conv_w", _pad_rows(w["conv_w"], 8), False)
    cw8 = jnp.concatenate([cw_all[2 * k] for k in range(N_CHIPS)], axis=1)
    g1, gm, g2, gp, gf = (w[k].reshape(1, d) for k in NORMS)

    def ffn_fwd(tag, h, g, w_in, w_out_name, riders):
        n = rms_fwd(tag + "_norm", h, g, tt)
        if riders:
            a, bufs = mm_nn_stacked(tag + "_in", n, w_in, F32, tm, w_in.shape[2], comm=gather_of(riders))
            keep(riders, bufs)
        else:
            a = mm_nn_stacked(tag + "_in", n, w_in, F32, tm, w_in.shape[2])
        s = swiglu_fwd(tag + "_act", a, tt)
        return n, a, s, mm_nn(tag + "_out", s, full[w_out_name], F32, tm, res=h, alpha=0.5)

    n1, a1, s1, h1 = ffn_fwd("ffn1", x, g1, full["ffn1_w_in"], "ffn1_w_out", ["ffn1_w_out", "w_mix_in"])
    u = rms_fwd("mix_norm", h1, gm, tt)
    wmix = full["w_mix_in"]
    riders = [["w_conv_out", "w_attn_out", "w_mix_out"], ["ffn2_w_in"], ["ffn2_w_out", "w_ple_gate", "w_ple_proj"]]
    cbx, bufs = mm_nn_stacked("mix_in_conv", u, wmix, F32, tm, d, 0, 3, comm=gather_of(riders[0]))
    keep(riders[0], bufs)
    qkv, bufs = mm_nn_stacked("mix_in_qkv", u, wmix, BF16, tm, d, 3, 3, comm=gather_of(riders[1]))
    keep(riders[1], bufs)
    gates, bufs = mm_nn_stacked("mix_in_gates", u, wmix, F32, tm, d, 6, 2, comm=gather_of(riders[2]))
    keep(riders[2], bufs)
    wpp = full["w_ple_proj"]
    wpp = jnp.transpose(wpp, (1, 0, 2)).reshape(wpp.shape[1], -1)
    ycin = conv_fwd("conv", cbx, cw8, tt)
    y_conv = mm_nn("conv_out", ycin, full["w_conv_out"], F32, tm)
    o = attn_fwd("attn", qkv, tq)
    y_attn = mm_nn("attn_out", o, full["w_attn_out"], F32, tm)
    merged = gate_fwd("merge", gates, y_conv, y_attn, tt)
    h2 = mm_nn("mix_out", merged, full["w_mix_out"], F32, tm, res=h1, alpha=1.0)
    n2, a2, s2, h3 = ffn_fwd("ffn2", h2, g2, full["ffn2_w_in"], "ffn2_w_out", [])
    npl = rms_fwd("ple_norm", h3, gp, tt)
    zg = mm_nn("ple_gate", npl, full["w_ple_gate"], F32, tm)
    pp = mm_nn("ple_proj", p, wpp, F32, tm)

    pieces, chip_sums, halves = {}, {}, {}

    def as_pieces(k):
        pc = pieces[k]
        return pc if k in COL_SHARDED else pc.reshape(N_CHIPS, 2, pc.shape[0] // (2 * N_CHIPS), pc.shape[1])

    def sum_siblings(tag, names):
        pcs = [as_pieces(k) for k in names]
        got = run_comm("exchange_" + tag, exchange_comm(pcs))
        for k, a, b in zip(names, pcs, got):
            chip_sums[k] = sum_cores("sum_cores_" + k, a, b, place)

    def scatter_of(names):
        return scatter_comm([chip_sums[k] for k in names])

    def sum_landed(names, landed):
        for k, b in zip(names, landed):
            halves[k] = sum_chips("sum_chips_" + k, chip_sums[k], b, place)

    dh4, dpp, dzg, dgf, loss_row = tail("tail", h3, zg, pp, tgt, gf, tt)
    dwpp = mm_tn_whole("ple_proj_dw", p, dpp, tm)
    pieces["w_ple_proj"] = jnp.transpose(dwpp.reshape(2, p.shape[1] // 2, N_CHIPS, d // N_CHIPS), (2, 0, 1, 3))
    pieces["w_ple_gate"] = mm_tn_rows("ple_gate_dw", npl, dzg, tm)
    dnp = mm_nt("ple_gate_dx", dzg, full["w_ple_gate"], F32, tm, d)
    dh3, df2, dgp = rms_bwd("ple_norm_bwd", h3, gp, dnp, dh4, 0.5, tt)
    w_in, w_out = full["ffn2_w_in"], full["ffn2_w_out"]
    pieces["ffn2_w_out"] = mm_tn_rows("ffn2_dwout", s2, df2, tm)
    ds2 = mm_nt("ffn2_ds", df2, w_out, F32, tm, w_out.shape[0] // 2)
    da2 = swiglu_bwd("ffn2_dact", a2, ds2, tt)
    pieces["ffn2_w_in"] = mm_tn_cols("ffn2_dwin", n2, da2, tm)
    dn2 = mm_nt_stacked("ffn2_dn", da2, w_in, F32, tm, w_in.shape[2])
    dh2, dh2b, dg2 = rms_bwd("ffn2_norm_bwd", h2, g2, dn2, dh3, 1.0, tt)
    pieces["w_mix_out"] = mm_tn_rows("mix_out_dw", merged, dh2b, tm)
    dmerged = mm_nt("mix_out_dx", dh2b, full["w_mix_out"], F32, tm, d)
    dyc, dya, dgates = gate_bwd("merge_bwd", dmerged, gates, y_conv, y_attn, tt)
    pieces["w_conv_out"] = mm_tn_rows("conv_out_dw", ycin, dyc, tm)
    dycin = mm_nt("conv_out_dx", dyc, full["w_conv_out"], F32, tm, d)
    dcbx, dcw8 = conv_bwd("conv_bwd", dycin, cbx, cw8, tt)
    pieces["w_attn_out"] = mm_tn_rows("attn_out_dw", o, dya, tm)
    do = mm_nt("attn_out_dx", dya, full["w_attn_out"], BF16, tm, d)
    dq, dk, dv = attn_bwd("attn_bwd", qkv, do, tq)
    dmix = jnp.concatenate([dcbx, dq, dk, dv, dgates], axis=1)
    pieces["w_mix_in"] = mm_tn_cols("mix_in_dw", u, dmix, tm)
    early = ["ffn2_w_in", "ffn2_w_out", "w_ple_gate", "w_ple_proj", "w_mix_out", "w_conv_out", "w_attn_out"]
    sum_siblings("early", early + ["w_mix_in"])
    du, landed = mm_nt_stacked("mix_in_dx", dmix, wmix, F32, tm, d, comm=scatter_of(early))
    sum_landed(early, landed)
    dh1, df1, dgm = rms_bwd("mix_norm_bwd", h1, gm, du, dh2, 0.5, tt)
    w_in, w_out = full["ffn1_w_in"], full["ffn1_w_out"]
    pieces["ffn1_w_out"] = mm_tn_rows("ffn1_dwout", s1, df1, tm)
    ds1 = mm_nt("ffn1_ds", df1, w_out, F32, tm, w_out.shape[0] // 2)
    da1 = swiglu_bwd("ffn1_dact", a1, ds1, tt)
    pieces["ffn1_w_in"], landed = mm_tn_cols("ffn1_dwin", n1, da1, tm, comm=scatter_of(["w_mix_in"]))
    sum_landed(["w_mix_in"], landed)
    late = ["ffn1_w_in", "ffn1_w_out"]
    sum_siblings("late", late)
    dn1, landed = mm_nt_stacked("ffn1_dn", da1, w_in, F32, tm, w_in.shape[2], comm=scatter_of(late))
    sum_landed(late, landed)
    dx, _, dg1 = rms_bwd("ffn1_norm_bwd", x, g1, dn1, dh1, 1.0, tt)

    shared = run_comm("share_halves", share_comm([halves[k] for k in MATS]))
    grad, delta, new_m, new_v = {}, {}, {}, {}
    for k, sh in zip(MATS, shared):
        grad[k] = sh.reshape(w[k].shape)
        delta[k], new_m[k], new_v[k] = adamw("adamw_" + k, w[k], grad[k], m[k], v[k])

    small = jnp.concatenate([dg1, dgm, dg2, dgp, dgf, dcw8[:3], loss_row, jnp.zeros((7, d), F32)], axis=0)
    tot = gather_small("sum_small", small, True)
    loss = tot[8, 0]
    norm_w = jnp.concatenate([w[k].reshape(1, d) for k in NORMS] + [jnp.zeros((3, d), F32)], axis=0)
    norm_m = jnp.concatenate([m[k].reshape(1, d) for k in NORMS] + [jnp.zeros((3, d), F32)], axis=0)
    norm_v = jnp.concatenate([v[k].reshape(1, d) for k in NORMS] + [jnp.ones((3, d), F32)], axis=0)
    norm_g = jnp.concatenate([tot[0:5], jnp.zeros((3, d), F32)], axis=0)
    nd, nm, nv = adamw("adamw_norms", norm_w, norm_g, norm_m, norm_v)
    for r, k in enumerate(NORMS):
        grad[k] = norm_g[r].reshape(w[k].shape)
        delta[k], new_m[k], new_v[k] = (a[r].reshape(w[k].shape) for a in (nd, nm, nv))
    cs = d // N_CHIPS
    gcw = lax.dynamic_slice(tot[5:8], (0, chip * cs), (3, cs))
    cd, cm, cv = adamw("adamw_conv_w", _pad_rows(w["conv_w"], 8), _pad_rows(gcw, 8), _pad_rows(m["conv_w"], 8),
                       jnp.concatenate([v["conv_w"], jnp.ones((5, cs), F32)], axis=0))
    grad["conv_w"], delta["conv_w"], new_m["conv_w"], new_v["conv_w"] = gcw, cd[:3], cm[:3], cv[:3]
    return loss, dx, grad, delta, new_m, new_v


def kernel(x, p, ffn1_norm, ffn1_w_in, ffn1_w_out, mix_norm, w_mix_in, conv_w, w_conv_out, w_attn_out, w_mix_out, ffn2_norm, ffn2_w_in, ffn2_w_out, ple_norm, w_ple_gate, w_ple_proj, final_norm, loss_target, m_ffn1_norm, m_ffn1_w_in, m_ffn1_w_out, m_mix_norm, m_w_mix_in, m_conv_w, m_w_conv_out, m_w_attn_out, m_w_mix_out, m_ffn2_norm, m_ffn2_w_in, m_ffn2_w_out, m_ple_norm, m_w_ple_gate, m_w_ple_proj, m_final_norm, v_ffn1_norm, v_ffn1_w_in, v_ffn1_w_out, v_mix_norm, v_w_mix_in, v_conv_w, v_w_conv_out, v_w_attn_out, v_w_mix_out, v_ffn2_norm, v_ffn2_w_in, v_ffn2_w_out, v_ple_norm, v_w_ple_gate, v_w_ple_proj, v_final_norm):
    ws = (ffn1_norm, ffn1_w_in, ffn1_w_out, mix_norm, w_mix_in, conv_w, w_conv_out, w_attn_out, w_mix_out, ffn2_norm,
          ffn2_w_in, ffn2_w_out, ple_norm, w_ple_gate, w_ple_proj, final_norm)
    ms = (m_ffn1_norm, m_ffn1_w_in, m_ffn1_w_out, m_mix_norm, m_w_mix_in, m_conv_w, m_w_conv_out, m_w_attn_out,
          m_w_mix_out, m_ffn2_norm, m_ffn2_w_in, m_ffn2_w_out, m_ple_norm, m_w_ple_gate, m_w_ple_proj, m_final_norm)
    vs = (v_ffn1_norm, v_ffn1_w_in, v_ffn1_w_out, v_mix_norm, v_w_mix_in, v_conv_w, v_w_conv_out, v_w_attn_out,
          v_w_mix_out, v_ffn2_norm, v_ffn2_w_in, v_ffn2_w_out, v_ple_norm, v_w_ple_gate, v_w_ple_proj, v_final_norm)
    assert x.shape[0] == 1 and p.shape[:2] == (1, 1), "one sequence and one layer per device"

    def strip(a):
        return a[0] if a.ndim == 3 or (a.ndim == 2 and a.shape[0] == 1) else a

    w = {k: strip(a) for k, a in zip(WEIGHTS, ws)}
    m = {k: strip(a) for k, a in zip(WEIGHTS, ms)}
    v = {k: strip(a) for k, a in zip(WEIGHTS, vs)}
    loss, dx, grad, delta, new_m, new_v = _step(x[0], p[0, 0], loss_target[0], w, m, v)
    shapes = [a.shape for a in ws]
    outs = [loss, dx[None]]
    for res in (grad, delta, new_m, new_v):
        outs += [res[k].reshape(s) for k, s in zip(WEIGHTS, shapes)]
    return tuple(outs)
```

```python
import functools
import math

import jax
import jax.numpy as jnp
from jax import lax
from jax.experimental import pallas as pl
from jax.experimental.pallas import tpu as pltpu

F32 = jnp.float32
BF16 = jnp.bfloat16
MESH = pl.DeviceIdType.MESH
ANY = pl.BlockSpec(memory_space=pl.ANY)

HEAD_DIM = 128
NORM_EPS = 1e-6
N_CHIPS = 4
N_DEV = 8
BF16_ROWS = 16
VMEM_LIMIT = 56 * 1024 * 1024
STICK_EXIT = 110.0

ADAM_LR = 0.001
ADAM_B1 = 0.9
ADAM_B2 = 0.999
ADAM_EPS = 1e-08
ADAM_WD = 0.01
ADAM_STEP = 10

NN = (((1,), (0,)), ((), ()))
NT = (((1,), (1,)), ((), ()))
TN = (((0,), (0,)), ((), ()))


def _params(sem=None, **kw):
    if sem is not None:
        kw["dimension_semantics"] = sem
    return pltpu.CompilerParams(vmem_limit_bytes=VMEM_LIMIT, **kw)


def _pcall(body, **kw):
    return pl.pallas_call(body, **kw)


def _tile(n, pref, mult=8):
    best = None
    for d in range(mult, min(n, pref) + 1, mult):
        if n % d == 0:
            best = d
    return best if best is not None else n


def _dot(a, b, dims):
    return lax.dot_general(a, b, dims, preferred_element_type=F32)


def _call(name, body, grid, in_specs, out_specs, out_shape, args, scratch=(), sem=None, comm=None):
    n_in, n_out, n_sc = len(in_specs), len(out_specs), len(scratch)
    if comm is None:
        def plain(*refs):
            body(refs[:n_in], refs[n_in:n_in + n_out], refs[n_in + n_out:])

        return _pcall(plain, name=name, out_shape=list(out_shape), grid=grid, in_specs=list(in_specs),
                      out_specs=list(out_specs), scratch_shapes=list(scratch), compiler_params=_params(sem))(*args)
    n_cin, n_cout = len(comm.ins), len(comm.outs)
    steps = math.prod(grid)

    def hosted(*refs):
        ins, c_ins = refs[:n_in], refs[n_in:n_in + n_cin]
        outs = refs[n_in + n_cin:n_in + n_cin + n_out]
        c_outs = refs[n_in + n_cin + n_out:n_in + n_cin + n_out + n_cout]
        rest = refs[n_in + n_cin + n_out + n_cout:]
        sems = rest[n_sc:]
        step = pl.program_id(0)
        for ax in range(1, len(grid)):
            step = step * grid[ax] + pl.program_id(ax)

        @pl.when(step == 0)
        def _():
            comm.first(c_ins, c_outs, sems)

        body(ins, outs, rest[:n_sc])

        @pl.when(step == (3 * steps) // 4)
        def _():
            comm.mid(c_ins, c_outs, sems)

        @pl.when(step == steps - 1)
        def _():
            comm.last(c_ins, c_outs, sems)

    res = _pcall(hosted, name=name, out_shape=list(out_shape) + comm.outs, grid=grid,
                 in_specs=list(in_specs) + [ANY] * n_cin, out_specs=list(out_specs) + [ANY] * n_cout,
                 input_output_aliases={n_in + k: n_out + v for k, v in comm.aliases.items()},
                 scratch_shapes=list(scratch) + comm.sems,
                 compiler_params=_params(("arbitrary",) * len(grid)))(*args, *comm.ins)
    return list(res[:n_out]), list(res[n_out:])


def _mm(name, a, b, out_sds, grid, a_spec, b_spec, o_spec, dims, acc_shape, res=None, alpha=1.0, comm=None):
    nk = grid[2]

    def body(ins, outs, scratch):
        a_ref, b_ref = ins[:2]
        r_ref = ins[2] if res is not None else None
        o_ref = outs[0]

        def finish(r):
            if alpha != 1.0:
                r = r * alpha
            if r_ref is not None:
                r = r_ref[...] + r
            o_ref[...] = r.astype(o_ref.dtype)

        part = _dot(a_ref[...].astype(BF16), b_ref[...].astype(BF16), dims)
        if nk == 1:
            finish(part)
        else:
            acc_ref = scratch[0]
            kk = pl.program_id(2)

            @pl.when(kk == 0)
            def _():
                acc_ref[...] = part

            @pl.when(kk > 0)
            def _():
                acc_ref[...] += part

            @pl.when(kk == nk - 1)
            def _():
                finish(acc_ref[...])

    in_specs = [a_spec, b_spec]
    args = [a, b]
    if res is not None:
        in_specs.append(o_spec)
        args.append(res)
    scratch = [] if nk == 1 else [pltpu.VMEM(acc_shape, F32)]
    got = _call(name, body, grid, in_specs, [o_spec], [out_sds], args, scratch,
                ("parallel", "parallel", "arbitrary"), comm)
    return got[0] if comm is None else (got[0][0], got[1])


def ffn_in_act(name, n, w4, tm, comm=None):
    t, d = n.shape
    cs = w4.shape[2]

    def body(ins, outs, scratch):
        n_ref, wg_ref, wu_ref = ins
        a_ref, s_ref = outs
        nv = n_ref[...]
        gate = _dot(nv, wg_ref[...], NN)
        up = _dot(nv, wu_ref[...], NN)
        a_ref[0] = gate.astype(BF16)
        a_ref[1] = up.astype(BF16)
        s_ref[...] = (gate * jax.nn.sigmoid(gate) * up).astype(BF16)

    got = _call(name, body, (t // tm, 2),
                [pl.BlockSpec((tm, d), lambda i, j: (i, 0)),
                 pl.BlockSpec((None, d, cs), lambda i, j: (j, 0, 0)),
                 pl.BlockSpec((None, d, cs), lambda i, j: (2 + j, 0, 0))],
                [pl.BlockSpec((2, tm, cs), lambda i, j: (0, i, j)), pl.BlockSpec((tm, cs), lambda i, j: (i, j))],
                [jax.ShapeDtypeStruct((2, t, 2 * cs), BF16), jax.ShapeDtypeStruct((t, 2 * cs), BF16)],
                [n, w4, w4], (), ("parallel", "parallel"), comm)
    return got if comm is None else (got[0], got[1])


def ffn_ds_dact(name, df, w_out, a3, tm):
    t, d = df.shape
    f = w_out.shape[0]
    cs = f // 2

    def body(ins, outs, scratch):
        df_ref, w_ref, a_ref = ins
        ds = _dot(df_ref[...], w_ref[...], NT)
        gate = a_ref[0].astype(F32)
        up = a_ref[1].astype(F32)
        sg = jax.nn.sigmoid(gate)
        outs[0][0] = (ds * up * sg * (1.0 + gate * (1.0 - sg))).astype(BF16)
        outs[0][1] = (ds * gate * sg).astype(BF16)

    blk = pl.BlockSpec((2, tm, cs), lambda i, j: (0, i, j))
    return _call(name, body, (t // tm, 2),
                 [pl.BlockSpec((tm, d), lambda i, j: (i, 0)), pl.BlockSpec((cs, d), lambda i, j: (j, 0)), blk],
                 [blk], [jax.ShapeDtypeStruct((2, t, f), BF16)], [df, w_out, a3], (), ("parallel", "parallel"))[0]


def mm_nn(name, a, w, out_dtype, tm, res=None, alpha=1.0):
    m, k = a.shape
    n = w.shape[1]
    return _mm(name, a, w, jax.ShapeDtypeStruct((m, n), out_dtype), (m // tm, 1, 1),
               pl.BlockSpec((tm, k), lambda i, j, r: (i, 0)),
               pl.BlockSpec((k, n), lambda i, j, r: (0, 0)),
               pl.BlockSpec((tm, n), lambda i, j, r: (i, 0)), NN, None, res=res, alpha=alpha)


def mm_nn_stacked(name, a, w4, out_dtype, tm, tn, j0=0, nj=None, comm=None):
    m, k = a.shape
    cs = w4.shape[2]
    per = cs // tn
    nj = N_CHIPS * per - j0 if nj is None else nj
    return _mm(name, a, w4, jax.ShapeDtypeStruct((m, nj * tn), out_dtype), (m // tm, nj, 1),
               pl.BlockSpec((tm, k), lambda i, j, r: (i, 0)),
               pl.BlockSpec((None, k, tn), lambda i, j, r: ((j + j0) // per, 0, (j + j0) % per)),
               pl.BlockSpec((tm, tn), lambda i, j, r: (i, j)), NN, None, comm=comm)


def mm_nt(name, dy, w, out_dtype, tm, tko):
    m, n = dy.shape
    k = w.shape[0]
    return _mm(name, dy, w, jax.ShapeDtypeStruct((m, k), out_dtype), (m // tm, k // tko, 1),
               pl.BlockSpec((tm, n), lambda i, j, r: (i, 0)),
               pl.BlockSpec((tko, n), lambda i, j, r: (j, 0)),
               pl.BlockSpec((tm, tko), lambda i, j, r: (i, j)), NT, None)


def mm_nt_stacked(name, dy, w4, out_dtype, tm, tn, comm=None):
    m = dy.shape[-2]
    k, cs = w4.shape[1], w4.shape[2]
    per = cs // tn
    if dy.ndim == 3:
        dy_spec = pl.BlockSpec((None, tm, cs), lambda i, j, r: (r // 2, i, r % 2))
    else:
        dy_spec = pl.BlockSpec((tm, tn), lambda i, j, r: (i, r))
    return _mm(name, dy, w4, jax.ShapeDtypeStruct((m, k), out_dtype), (m // tm, 1, N_CHIPS * per), dy_spec,
               pl.BlockSpec((None, k, tn), lambda i, j, r: (r // per, 0, r % per)),
               pl.BlockSpec((tm, k), lambda i, j, r: (i, 0)), NT, (tm, k), comm=comm)


def mm_tn_rows(name, xa, dy, tt):
    t, k = xa.shape
    n = dy.shape[1]
    tkr = k // 2
    return _mm(name, xa, dy, jax.ShapeDtypeStruct((k, n), BF16), (k // tkr, 1, t // tt),
               pl.BlockSpec((tt, tkr), lambda i, j, r: (r, i)),
               pl.BlockSpec((tt, n), lambda i, j, r: (r, 0)),
               pl.BlockSpec((tkr, n), lambda i, j, r: (i, 0)), TN, (tkr, n))


def mm_tn_whole(name, xa, dy, tt):
    t, k = xa.shape
    n = dy.shape[1]
    return _mm(name, xa, dy, jax.ShapeDtypeStruct((k, n), BF16), (1, 1, t // tt),
               pl.BlockSpec((tt, k), lambda i, j, r: (r, 0)),
               pl.BlockSpec((tt, n), lambda i, j, r: (r, 0)),
               pl.BlockSpec((k, n), lambda i, j, r: (0, 0)), TN, (k, n))


def mm_tn_cols(name, xa, dy, tt, comm=None):
    t, k = xa.shape
    pr = k // 2
    if dy.ndim == 3:
        cs = dy.shape[2] // 2
        dy_spec = pl.BlockSpec((None, tt, cs), lambda i, j, r: (j // 2, r, j % 2))
    else:
        cs = dy.shape[1] // N_CHIPS
        dy_spec = pl.BlockSpec((tt, cs), lambda i, j, r: (r, j))
    return _mm(name, xa, dy, jax.ShapeDtypeStruct((N_CHIPS, 2, pr, cs), BF16), (2, N_CHIPS, t // tt),
               pl.BlockSpec((tt, pr), lambda i, j, r: (r, i)), dy_spec,
               pl.BlockSpec((None, None, pr, cs), lambda i, j, r: (j, i, 0, 0)), TN, (pr, cs), comm=comm)


def _rows(tt, w, col=0):
    return pl.BlockSpec((tt, w), lambda i: (i, col))


def _whole(shape):
    return pl.BlockSpec(shape, lambda i: (0,) * len(shape))


def _rstd(h):
    return lax.rsqrt(jnp.mean(h * h, axis=-1, keepdims=True) + NORM_EPS)


def rms_fwd(name, h, g, tt):
    t, d = h.shape

    def body(h_ref, g_ref, o_ref):
        hv = h_ref[...]
        o_ref[...] = (hv * _rstd(hv) * g_ref[...]).astype(o_ref.dtype)

    return _pcall(body, name=name, out_shape=jax.ShapeDtypeStruct((t, d), BF16), grid=(t // tt,),
                  in_specs=[_rows(tt, d), _whole((1, d))], out_specs=_rows(tt, d),
                  compiler_params=_params(("parallel",)))(h, g)


def rms_bwd(name, h, g, dn, dres, alpha, tt):
    t, d = h.shape

    def body(h_ref, g_ref, dn_ref, dr_ref, dh_ref, dhb_ref, dg_ref):
        hv = h_ref[...]
        hn = hv * _rstd(hv)
        dnv = dn_ref[...]
        gy = dnv * g_ref[...]
        dh = dr_ref[...] + _rstd(hv) * (gy - hn * jnp.mean(gy * hn, axis=-1, keepdims=True))
        dh_ref[...] = dh
        dhb_ref[...] = (alpha * dh).astype(BF16)

        @pl.when(pl.program_id(0) == 0)
        def _():
            dg_ref[...] = jnp.zeros_like(dg_ref)

        dg_ref[...] += jnp.sum(dnv * hn, axis=0, keepdims=True)

    return _pcall(body, name=name,
                  out_shape=(jax.ShapeDtypeStruct((t, d), F32), jax.ShapeDtypeStruct((t, d), BF16),
                             jax.ShapeDtypeStruct((1, d), F32)),
                  grid=(t // tt,),
                  in_specs=[_rows(tt, d), _whole((1, d)), _rows(tt, d), _rows(tt, d)],
                  out_specs=(_rows(tt, d), _rows(tt, d), _whole((1, d))),
                  compiler_params=_params(("arbitrary",)))(h, g, dn, dres)


def gate_fwd(name, gates, yc, ya, tt):
    t, d = yc.shape

    def body(g_ref, yc_ref, ya_ref, o_ref):
        o_ref[...] = (jax.nn.sigmoid(g_ref[:, :d]) * yc_ref[...]
                      + jax.nn.sigmoid(g_ref[:, d:]) * ya_ref[...]).astype(o_ref.dtype)

    return _pcall(body, name=name, out_shape=jax.ShapeDtypeStruct((t, d), BF16), grid=(t // tt,),
                  in_specs=[_rows(tt, 2 * d), _rows(tt, d), _rows(tt, d)], out_specs=_rows(tt, d),
                  compiler_params=_params(("parallel",)))(gates, yc, ya)


def gate_bwd(name, dm, gates, yc, ya, tt):
    t, d = yc.shape

    def body(dm_ref, g_ref, yc_ref, ya_ref, dyc_ref, dya_ref, dg_ref):
        dmv = dm_ref[...]
        sc = jax.nn.sigmoid(g_ref[:, :d])
        sa = jax.nn.sigmoid(g_ref[:, d:])
        dyc_ref[...] = (dmv * sc).astype(BF16)
        dya_ref[...] = (dmv * sa).astype(BF16)
        dg_ref[:, :d] = (dmv * yc_ref[...] * sc * (1.0 - sc)).astype(BF16)
        dg_ref[:, d:] = (dmv * ya_ref[...] * sa * (1.0 - sa)).astype(BF16)

    return _pcall(body, name=name,
                  out_shape=(jax.ShapeDtypeStruct((t, d), BF16), jax.ShapeDtypeStruct((t, d), BF16),
                             jax.ShapeDtypeStruct((t, 2 * d), BF16)),
                  grid=(t // tt,),
                  in_specs=[_rows(tt, d), _rows(tt, 2 * d), _rows(tt, d), _rows(tt, d)],
                  out_specs=(_rows(tt, d), _rows(tt, d), _rows(tt, 2 * d)),
                  compiler_params=_params(("parallel",)))(dm, gates, yc, ya)


def _shift_down(cur, prev8, s):
    tt = cur.shape[0]
    rolled = pltpu.roll(cur, s, 0)
    row8 = lax.broadcasted_iota(jnp.int32, prev8.shape, 0)
    first8 = jnp.where(row8 < s, pltpu.roll(prev8, s, 0), rolled[:8])
    return jnp.concatenate([first8, rolled[8:]], axis=0) if tt > 8 else first8


def _shift_up(cur, next8, s):
    tt = cur.shape[0]
    rolled = pltpu.roll(cur, tt - s, 0)
    row8 = lax.broadcasted_iota(jnp.int32, next8.shape, 0)
    last8 = jnp.where(row8 >= 8 - s, pltpu.roll(next8, 8 - s, 0), rolled[tt - 8:])
    return jnp.concatenate([rolled[:tt - 8], last8], axis=0) if tt > 8 else last8


def _prev8(tt, d, col):
    return pl.BlockSpec((8, d), lambda i: (jnp.maximum(i * (tt // 8) - 1, 0), col))


def _next8(tt, d, col, t):
    return pl.BlockSpec((8, d), lambda i: (jnp.minimum((i + 1) * (tt // 8), t // 8 - 1), col))


def conv_fwd(name, cbx, cw8, tt):
    t, d3 = cbx.shape
    d = d3 // 3

    def body(cb_ref, cc_ref, cx_ref, pc_ref, px_ref, w_ref, o_ref):
        has_prev = (pl.program_id(0) > 0).astype(F32)
        cc = cc_ref[...] * cx_ref[...]
        prev = pc_ref[...] * px_ref[...] * has_prev
        w = w_ref[...]
        conv = w[0:1] * _shift_down(cc, prev, 2) + w[1:2] * _shift_down(cc, prev, 1) + w[2:3] * cc
        o_ref[...] = (cb_ref[...] * conv).astype(o_ref.dtype)

    return _pcall(body, name=name, out_shape=jax.ShapeDtypeStruct((t, d), BF16), grid=(t // tt,),
                  in_specs=[_rows(tt, d, 0), _rows(tt, d, 1), _rows(tt, d, 2), _prev8(tt, d, 1), _prev8(tt, d, 2),
                            _whole((8, d))],
                  out_specs=_rows(tt, d), compiler_params=_params(("parallel",)))(cbx, cbx, cbx, cbx, cbx, cw8)


def conv_bwd(name, dyc, cbx, cw8, tt):
    t, d3 = cbx.shape
    d = d3 // 3
    n = t // tt

    def body(dy_ref, cb_ref, cc_ref, cx_ref, pc_ref, px_ref, ndy_ref, ncb_ref, w_ref, o_ref, dw_ref):
        i = pl.program_id(0)
        has_prev = (i > 0).astype(F32)
        has_next = (i < n - 1).astype(F32)
        cb = cb_ref[...]
        cc = cc_ref[...] * cx_ref[...]
        prev = pc_ref[...] * px_ref[...] * has_prev
        w = w_ref[...]
        cc1 = _shift_down(cc, prev, 1)
        cc2 = _shift_down(cc, prev, 2)
        conv = w[0:1] * cc2 + w[1:2] * cc1 + w[2:3] * cc
        dyv = dy_ref[...]
        dconv = dyv * cb
        dnext = ndy_ref[...] * ncb_ref[...] * has_next
        dcc = w[2:3] * dconv + w[1:2] * _shift_up(dconv, dnext, 1) + w[0:1] * _shift_up(dconv, dnext, 2)
        o_ref[:, :d] = (dyv * conv).astype(BF16)
        o_ref[:, d:2 * d] = (dcc * cx_ref[...]).astype(BF16)
        o_ref[:, 2 * d:] = (dcc * cc_ref[...]).astype(BF16)

        @pl.when(i == 0)
        def _():
            dw_ref[...] = jnp.zeros_like(dw_ref)

        dw_ref[0:1, :] += jnp.sum(dconv * cc2, axis=0, keepdims=True)
        dw_ref[1:2, :] += jnp.sum(dconv * cc1, axis=0, keepdims=True)
        dw_ref[2:3, :] += jnp.sum(dconv * cc, axis=0, keepdims=True)

    return _pcall(body, name=name,
                  out_shape=(jax.ShapeDtypeStruct((t, d3), BF16), jax.ShapeDtypeStruct((8, d), F32)),
                  grid=(n,),
                  in_specs=[_rows(tt, d), _rows(tt, d, 0), _rows(tt, d, 1), _rows(tt, d, 2),
                            _prev8(tt, d, 1), _prev8(tt, d, 2), _next8(tt, d, 0, t), _next8(tt, d, 0, t),
                            _whole((8, d))],
                  out_specs=(_rows(tt, d3), _whole((8, d))),
                  compiler_params=_params(("arbitrary",)))(dyc, cbx, cbx, cbx, cbx, cbx, dyc, cbx, cw8)


def tail(name, h3, zg, pp, tgt, gf, tt):
    t, d = h3.shape

    def body(h_ref, zg_ref, pp_ref, tg_ref, gf_ref, dh_ref, dpp_ref, dzg_ref, dgf_ref, loss_ref):
        pg = jax.nn.sigmoid(zg_ref[...])
        ppv = pp_ref[...]
        h4 = h_ref[...] + pg * ppv
        r4 = _rstd(h4)
        hn = h4 * r4
        gfv = gf_ref[...]
        err = hn * gfv - tg_ref[...]
        dy = err * (1.0 / d)
        gy = dy * gfv
        dh4 = r4 * (gy - hn * jnp.mean(gy * hn, axis=-1, keepdims=True))
        dh_ref[...] = dh4
        dpp_ref[...] = (dh4 * pg).astype(BF16)
        dzg_ref[...] = (dh4 * ppv * pg * (1.0 - pg)).astype(BF16)

        @pl.when(pl.program_id(0) == 0)
        def _():
            dgf_ref[...] = jnp.zeros_like(dgf_ref)
            loss_ref[...] = jnp.zeros_like(loss_ref)

        dgf_ref[...] += jnp.sum(dy * hn, axis=0, keepdims=True)
        tok = jnp.mean(err * err, axis=-1, keepdims=True)
        loss_ref[...] += 0.5 * jnp.sum(tok, axis=0, keepdims=True) * jnp.ones((1, loss_ref.shape[1]), F32)

    return _pcall(body, name=name,
                  out_shape=(jax.ShapeDtypeStruct((t, d), F32), jax.ShapeDtypeStruct((t, d), BF16),
                             jax.ShapeDtypeStruct((t, d), BF16), jax.ShapeDtypeStruct((1, d), F32),
                             jax.ShapeDtypeStruct((1, d), F32)),
                  grid=(t // tt,),
                  in_specs=[_rows(tt, d)] * 4 + [_whole((1, d))],
                  out_specs=(_rows(tt, d), _rows(tt, d), _rows(tt, d), _whole((1, d)), _whole((1, d))),
                  compiler_params=_params(("arbitrary",)))(h3, zg, pp, tgt, gf)


SCALE = 1.0 / math.sqrt(HEAD_DIM)


def _log_stick(z):
    return -(jnp.maximum(z, 0.0) + jnp.log(1.0 + jnp.exp(-jnp.abs(z))))


def _tri_sum(x, tri):
    hi = x.astype(BF16)
    lo = (x - hi.astype(F32)).astype(BF16)
    return _dot(hi, tri, NN) + _dot(lo, tri, NN)


def _sb_pair(q, k_d, k_p, below, upper, has_prev):
    z_d = _dot(q, k_d, NT) * SCALE
    z_p = _dot(q, k_p, NT) * SCALE
    cum_d = _tri_sum(jnp.where(below, _log_stick(z_d), 0.0), upper)
    cum_p = _tri_sum(jnp.where(has_prev, _log_stick(z_p), 0.0), upper)
    c_d = cum_d[:, 0:1]
    a_d = jnp.exp(jnp.where(below, z_d + cum_d, -1e30))
    a_p = jnp.exp(jnp.where(has_prev, z_p + cum_p + c_d, -1e30))
    return z_d, z_p, a_d, a_p, c_d + cum_p[:, 0:1]


def _sb_far(q, kj, upper, c):
    z = _dot(q, kj, NT) * SCALE
    cum = _tri_sum(_log_stick(z), upper)
    return z, jnp.exp(z + cum + c), c + cum[:, 0:1]


def _block_rows(j, tq):
    return pl.ds(pl.multiple_of(j * tq, tq), tq)


def attn_fwd(name, qkv, tq):
    t, d3 = qkv.shape
    d = d3 // 3
    nh = d // HEAD_DIM
    nq = t // tq

    def body(q_ref, k_ref, v_ref, o_ref):
        i = pl.program_id(1)
        q = q_ref[...]
        row = lax.broadcasted_iota(jnp.int32, (tq, tq), 0)
        col = lax.broadcasted_iota(jnp.int32, (tq, tq), 1)
        upper = (row >= col).astype(BF16)
        rows_d = _block_rows(i, tq)
        rows_p = _block_rows(jnp.maximum(i - 1, 0), tq)
        _, _, a_d, a_p, c = _sb_pair(q, k_ref[rows_d, :], k_ref[rows_p, :], col < row, upper, i > 0)
        acc = _dot(a_d.astype(BF16), v_ref[rows_d, :], NN) + _dot(a_p.astype(BF16), v_ref[rows_p, :], NN)

        def cond(st):
            return jnp.logical_and(st[0] >= 0, jnp.max(st[1]) > -STICK_EXIT)

        def step(st):
            rows = _block_rows(st[0], tq)
            _, a, c2 = _sb_far(q, k_ref[rows, :], upper, st[1])
            return st[0] - 1, c2, st[2] + _dot(a.astype(BF16), v_ref[rows, :], NN)

        _, _, acc = lax.while_loop(cond, step, (i - 2, c, acc))
        o_ref[...] = acc.astype(o_ref.dtype)

    return _pcall(body, name=name, out_shape=jax.ShapeDtypeStruct((t, d), BF16), grid=(nh, nq),
                  in_specs=[pl.BlockSpec((tq, HEAD_DIM), lambda h, i: (i, h)),
                            pl.BlockSpec((t, HEAD_DIM), lambda h, i: (0, nh + h)),
                            pl.BlockSpec((t, HEAD_DIM), lambda h, i: (0, 2 * nh + h))],
                  out_specs=pl.BlockSpec((tq, HEAD_DIM), lambda h, i: (i, h)),
                  compiler_params=_params(("parallel", "arbitrary")))(qkv, qkv, qkv)


def attn_bwd(name, qkv, do, tq):
    t, d3 = qkv.shape
    d = d3 // 3
    nh = d // HEAD_DIM
    nq = t // tq

    def body(q_ref, k_ref, v_ref, do_ref, dq_ref, dk_ref, dv_ref, dk_acc, dv_acc, g_buf, z_buf):
        i = pl.program_id(1)

        @pl.when(i == 0)
        def _():
            dk_acc[...] = jnp.zeros_like(dk_acc)
            dv_acc[...] = jnp.zeros_like(dv_acc)

        q = q_ref[...]
        dov = do_ref[...]
        row = lax.broadcasted_iota(jnp.int32, (tq, tq), 0)
        col = lax.broadcasted_iota(jnp.int32, (tq, tq), 1)
        below = col < row
        has_prev = i > 0
        upper = (row >= col).astype(BF16)
        lower = (row <= col).astype(BF16)
        rows_d = _block_rows(i, tq)
        rows_p = _block_rows(jnp.maximum(i - 1, 0), tq)

        z_d, z_p, a_d, a_p, c = _sb_pair(q, k_ref[rows_d, :], k_ref[rows_p, :], below, upper, has_prev)
        g_d = _dot(dov, v_ref[rows_d, :], NT) * a_d
        g_p = _dot(dov, v_ref[rows_p, :], NT) * a_p
        dv_acc[rows_d, :] += _dot(a_d.astype(BF16), dov, TN)
        dv_acc[rows_p, :] += _dot(a_p.astype(BF16), dov, TN)

        def cond(st):
            return jnp.logical_and(st[0] >= 0, jnp.max(st[1]) > -STICK_EXIT)

        def step(st):
            j = st[0]
            rows = _block_rows(j, tq)
            z, a, c2 = _sb_far(q, k_ref[rows, :], upper, st[1])
            g_buf[i - j] = _dot(dov, v_ref[rows, :], NT) * a
            z_buf[i - j] = z
            dv_acc[rows, :] += _dot(a.astype(BF16), dov, TN)
            return j - 1, c2

        j_stop, _ = lax.while_loop(cond, step, (i - 2, c))

        def far(j, st):
            run, dq = st
            rows = _block_rows(j, tq)
            g = g_buf[i - j]
            dz = (g - jax.nn.sigmoid(z_buf[i - j]) * (run + _tri_sum(g, lower))).astype(BF16)
            dk_acc[rows, :] += _dot(dz, q, TN)
            return run + jnp.sum(g, axis=1, keepdims=True), dq + _dot(dz, k_ref[rows, :], NN)

        run, dq = lax.fori_loop(j_stop + 1, i - 1, far,
                                (jnp.zeros((tq, 1), F32), jnp.zeros((tq, HEAD_DIM), F32)))
        p_p = run + _tri_sum(g_p, lower)
        p_d = run + jnp.sum(g_p, axis=1, keepdims=True) + _tri_sum(g_d, lower)
        dz_p = jnp.where(has_prev, g_p - jax.nn.sigmoid(z_p) * p_p, 0.0).astype(BF16)
        dz_d = jnp.where(below, g_d - jax.nn.sigmoid(z_d) * p_d, 0.0).astype(BF16)
        dk_acc[rows_p, :] += _dot(dz_p, q, TN)
        dk_acc[rows_d, :] += _dot(dz_d, q, TN)
        dq = dq + _dot(dz_p, k_ref[rows_p, :], NN) + _dot(dz_d, k_ref[rows_d, :], NN)
        dq_ref[...] = (dq * SCALE).astype(BF16)

        @pl.when(i == nq - 1)
        def _():
            dk_ref[...] = (dk_acc[...] * SCALE).astype(BF16)
            dv_ref[...] = dv_acc[...].astype(BF16)

    blk = pl.BlockSpec((tq, HEAD_DIM), lambda h, i: (i, h))
    col_h = pl.BlockSpec((t, HEAD_DIM), lambda h, i: (0, h))
    out = jax.ShapeDtypeStruct((t, d), BF16)
    return _pcall(body, name=name, out_shape=(out, out, out), grid=(nh, nq),
                  in_specs=[blk,
                            pl.BlockSpec((t, HEAD_DIM), lambda h, i: (0, nh + h)),
                            pl.BlockSpec((t, HEAD_DIM), lambda h, i: (0, 2 * nh + h)),
                            blk],
                  out_specs=(blk, col_h, col_h),
                  scratch_shapes=[pltpu.VMEM((t, HEAD_DIM), F32), pltpu.VMEM((t, HEAD_DIM), F32),
                                  pltpu.VMEM((nq, tq, tq), F32), pltpu.VMEM((nq, tq, tq), F32)],
                  compiler_params=_params(("parallel", "arbitrary")))(qkv, qkv, qkv, do)


def _place():
    x, y, c = lax.axis_index("x"), lax.axis_index("y"), lax.axis_index("c")
    chips = [(1 - x, y), (x, 1 - y), (1 - x, 1 - y)]
    return x, y, c, chips


def _remote(src, dst, send_sem, recv_sem, dev):
    return pltpu.make_async_remote_copy(src_ref=src, dst_ref=dst, send_sem=send_sem, recv_sem=recv_sem,
                                        device_id=dev, device_id_type=MESH)


def place_shard(name, w, chip):
    r, cdim = w.shape
    tr = _tile(r, max(BF16_ROWS, (1 << 19) // cdim), BF16_ROWS)

    def body(chip_ref, w_ref, o_ref):
        o_ref[...] = w_ref[...].astype(BF16)

    spec = pltpu.PrefetchScalarGridSpec(
        num_scalar_prefetch=1, grid=(r // tr,),
        in_specs=[pl.BlockSpec((tr, cdim), lambda i, s: (i, 0))],
        out_specs=pl.BlockSpec((None, tr, cdim), lambda i, s: (s[0], i, 0)))
    return _pcall(body, name=name, out_shape=jax.ShapeDtypeStruct((N_CHIPS, r, cdim), BF16), grid_spec=spec,
                  compiler_params=_params(("parallel",)))(chip, w)


class Comm:
    def __init__(self, ins, outs, aliases, sems, first, mid, last):
        self.ins, self.outs, self.aliases, self.sems = list(ins), list(outs), dict(aliases), list(sems)
        self.first, self.mid, self.last = first, mid, last


def run_comm(name, comm):
    ni, no = len(comm.ins), len(comm.outs)

    def body(*refs):
        ins, outs, sems = refs[:ni], refs[ni:ni + no], refs[ni + no:]
        comm.first(ins, outs, sems)
        comm.mid(ins, outs, sems)
        comm.last(ins, outs, sems)

    return _pcall(body, name=name, out_shape=comm.outs, in_specs=[ANY] * ni, out_specs=[ANY] * no,
                  input_output_aliases=comm.aliases, scratch_shapes=comm.sems, compiler_params=_params())(*comm.ins)


def gather_comm(bufs):
    n = len(bufs)

    def half(out, w, which):
        pr = out[w].shape[1] // 2
        return pl.ds(pl.multiple_of(which * pr, BF16_ROWS), pr)

    def first(ins, out, sems):
        isend, irecv, _, _ = sems
        x, y, c, chips = _place()
        for w in range(n):
            mine = out[w].at[2 * x + y, half(out, w, c)]
            for j, (cx, cy) in enumerate(chips):
                _remote(mine, mine, isend.at[3 * w + j], irecv.at[3 * w + j], (cx, cy, c)).start()

    def mid(ins, out, sems):
        isend, irecv, dsend, drecv = sems
        x, y, c, chips = _place()
        sib = (x, y, 1 - c)
        for w in range(n):
            for j, (cx, cy) in enumerate(chips):
                landed = out[w].at[2 * cx + cy, half(out, w, c)]
                _remote(landed, landed, isend.at[3 * w + j], irecv.at[3 * w + j], sib).wait_recv()
                _remote(landed, landed, dsend.at[3 * w + j], drecv.at[3 * w + j], sib).start()

    def last(ins, out, sems):
        isend, irecv, dsend, drecv = sems
        x, y, c, chips = _place()
        sib = (x, y, 1 - c)
        for w in range(n):
            for j, (cx, cy) in enumerate(chips):
                landed = out[w].at[2 * cx + cy, half(out, w, 1 - c)]
                _remote(landed, landed, dsend.at[3 * w + j], drecv.at[3 * w + j], sib).wait_recv()
        for w in range(n):
            sent = out[w].at[0, half(out, w, c)]
            for j in range(3):
                _remote(sent, sent, isend.at[3 * w + j], irecv.at[3 * w + j], sib).wait_send()
                _remote(sent, sent, dsend.at[3 * w + j], drecv.at[3 * w + j], sib).wait_send()

    return Comm(bufs, [jax.ShapeDtypeStruct(s.shape, s.dtype) for s in bufs], {w: w for w in range(n)},
                [pltpu.SemaphoreType.DMA((3 * n,))] * 4, first, mid, last)


def _nothing(ins, outs, sems):
    return None


def exchange_comm(pieces):
    n = len(pieces)

    def copies(src, out, sems):
        x, y, c, _ = _place()
        return [_remote(src[w].at[k, 1 - c], out[w].at[k], sems[0].at[N_CHIPS * w + k], sems[1].at[N_CHIPS * w + k],
                        (x, y, 1 - c)) for w in range(n) for k in range(N_CHIPS)]

    def first(src, out, sems):
        for cp in copies(src, out, sems):
            cp.start()

    def last(src, out, sems):
        for cp in copies(src, out, sems):
            cp.wait()

    return Comm(pieces, [jax.ShapeDtypeStruct((N_CHIPS,) + s.shape[2:], s.dtype) for s in pieces], {},
                [pltpu.SemaphoreType.DMA((N_CHIPS * n,))] * 2, first, _nothing, last)


def scatter_comm(parts):
    n = len(parts)

    def copies(src, out, sems):
        x, y, c, chips = _place()
        return [_remote(src[w].at[2 * cx + cy], out[w].at[j], sems[0].at[3 * w + j], sems[1].at[3 * w + j], (cx, cy, c))
                for w in range(n) for j, (cx, cy) in enumerate(chips)]

    def first(src, out, sems):
        for cp in copies(src, out, sems):
            cp.start()

    def last(src, out, sems):
        for cp in copies(src, out, sems):
            cp.wait()

    return Comm(parts, [jax.ShapeDtypeStruct((3,) + s.shape[1:], s.dtype) for s in parts], {},
                [pltpu.SemaphoreType.DMA((3 * n,))] * 2, first, _nothing, last)


def share_comm(halves):
    n = len(halves)

    def first(ins, buf, sems):
        x, y, c, _ = _place()
        for w in range(n):
            _remote(buf[w].at[c], buf[w].at[c], sems[0].at[w], sems[1].at[w], (x, y, 1 - c)).start()

    def last(ins, buf, sems):
        x, y, c, _ = _place()
        for w in range(n):
            landed = buf[w].at[1 - c]
            _remote(landed, landed, sems[0].at[w], sems[1].at[w], (x, y, 1 - c)).wait_recv()
        for w in range(n):
            _remote(buf[w].at[c], buf[w].at[c], sems[0].at[w], sems[1].at[w], (x, y, 1 - c)).wait_send()

    return Comm(halves, [jax.ShapeDtypeStruct(s.shape, s.dtype) for s in halves], {w: w for w in range(n)},
                [pltpu.SemaphoreType.DMA((n,))] * 2, first, _nothing, last)


def gather_small(name, blk, reduce):
    r, cdim = blk.shape

    def body(in_ref, out_ref, *rest):
        if reduce:
            buf, send_sem, recv_sem = rest
        else:
            buf = out_ref
            send_sem, recv_sem = rest
        x, y, c, _ = _place()
        me = 4 * x + 2 * y + c
        buf[me] = in_ref[...]
        peers = []
        for dx in range(2):
            for dy in range(2):
                for dc in range(2):
                    if dx or dy or dc:
                        peers.append((dx, dy, dc))
        copies = []
        for s, (dx, dy, dc) in enumerate(peers):
            cp = _remote(in_ref, buf.at[me], send_sem.at[s], recv_sem.at[s],
                         ((1 - x if dx else x), (1 - y if dy else y), (1 - c if dc else c)))
            cp.start()
            copies.append(cp)
        for s, (dx, dy, dc) in enumerate(peers):
            px, py, pc_ = (1 - x if dx else x), (1 - y if dy else y), (1 - c if dc else c)
            landed = buf.at[4 * px + 2 * py + pc_]
            _remote(landed, landed, send_sem.at[s], recv_sem.at[s], (x, y, c)).wait_recv()
        for cp in copies:
            cp.wait_send()
        if reduce:
            tot = buf[0]
            for s in range(1, N_DEV):
                tot = tot + buf[s]
            out_ref[...] = tot

    vm = pl.BlockSpec(memory_space=pltpu.VMEM)
    out_shape = jax.ShapeDtypeStruct((r, cdim) if reduce else (N_DEV, r, cdim), F32)
    scratch = ([pltpu.VMEM((N_DEV, r, cdim), F32)] if reduce else []) + [pltpu.SemaphoreType.DMA((N_DEV - 1,))] * 2
    return _pcall(body, name=name, out_shape=out_shape, in_specs=[vm], out_specs=vm, scratch_shapes=scratch,
                  compiler_params=_params())(blk)


def sum_cores(name, own, got, place):
    _, _, pr, pc = own.shape
    tr = _tile(pr, max(BF16_ROWS, (1 << 19) // pc), BF16_ROWS)

    def body(place_ref, own_ref, got_ref, o_ref):
        o_ref[...] = (own_ref[...].astype(F32) + got_ref[...].astype(F32)).astype(o_ref.dtype)

    spec = pltpu.PrefetchScalarGridSpec(
        num_scalar_prefetch=1, grid=(N_CHIPS, pr // tr),
        in_specs=[pl.BlockSpec((None, None, tr, pc), lambda k, i, s: (k, s[1], i, 0)),
                  pl.BlockSpec((None, tr, pc), lambda k, i, s: (k, i, 0))],
        out_specs=pl.BlockSpec((None, tr, pc), lambda k, i, s: (k, i, 0)))
    return _pcall(body, name=name, out_shape=jax.ShapeDtypeStruct((N_CHIPS, pr, pc), BF16), grid_spec=spec,
                  compiler_params=_params(("parallel", "parallel")))(place, own, got)


def sum_chips(name, part, got, place):
    _, pr, pc = part.shape
    tr = _tile(pr, max(BF16_ROWS, (1 << 18) // pc), BF16_ROWS)

    def body(place_ref, part_ref, got_ref, o_ref):
        tot = part_ref[...].astype(F32)
        for j in range(3):
            tot = tot + got_ref[j].astype(F32)
        o_ref[...] = tot

    spec = pltpu.PrefetchScalarGridSpec(
        num_scalar_prefetch=1, grid=(pr // tr,),
        in_specs=[pl.BlockSpec((None, tr, pc), lambda i, s: (s[0], i, 0)),
                  pl.BlockSpec((3, tr, pc), lambda i, s: (0, i, 0))],
        out_specs=pl.BlockSpec((None, tr, pc), lambda i, s: (s[1], i, 0)))
    return _pcall(body, name=name, out_shape=jax.ShapeDtypeStruct((2, pr, pc), F32), grid_spec=spec,
                  compiler_params=_params(("parallel",)))(place, part, got)


def adamw(name, w, g, m, v):
    rows, cols = w.shape
    tr = _tile(rows, max(8, (1 << 18) // cols))
    c1 = 1.0 / (1.0 - ADAM_B1 ** ADAM_STEP)
    c2 = 1.0 / (1.0 - ADAM_B2 ** ADAM_STEP)

    def body(w_ref, g_ref, m_ref, v_ref, d_ref, nm_ref, nv_ref):
        gv = g_ref[...]
        nm = ADAM_B1 * m_ref[...] + (1.0 - ADAM_B1) * gv
        nv = ADAM_B2 * v_ref[...] + (1.0 - ADAM_B2) * (gv * gv)
        nm_ref[...] = nm
        nv_ref[...] = nv
        d_ref[...] = -ADAM_LR * ((nm * c1) / (jnp.sqrt(nv * c2) + ADAM_EPS) + ADAM_WD * w_ref[...])

    spec = pl.BlockSpec((tr, cols), lambda i: (i, 0))
    sds = jax.ShapeDtypeStruct((rows, cols), F32)
    return _pcall(body, name=name, out_shape=(sds, sds, sds), grid=(rows // tr,),
                  in_specs=[spec] * 4, out_specs=(spec, spec, spec),
                  compiler_params=_params(("parallel",)))(w, g, m, v)


MATS = ["ffn1_w_in", "ffn1_w_out", "w_mix_in", "w_conv_out", "w_attn_out", "w_mix_out", "ffn2_w_in", "ffn2_w_out",
        "w_ple_gate", "w_ple_proj"]
COL_SHARDED = {"ffn1_w_in", "w_mix_in", "ffn2_w_in", "w_ple_proj"}
NORMS = ["ffn1_norm", "mix_norm", "ffn2_norm", "ple_norm", "final_norm"]
WEIGHTS = ["ffn1_norm", "ffn1_w_in", "ffn1_w_out", "mix_norm", "w_mix_in", "conv_w", "w_conv_out", "w_attn_out",
           "w_mix_out", "ffn2_norm", "ffn2_w_in", "ffn2_w_out", "ple_norm", "w_ple_gate", "w_ple_proj", "final_norm"]


def _pad_rows(a, rows):
    return jnp.concatenate([a, jnp.zeros((rows - a.shape[0],) + a.shape[1:], a.dtype)], axis=0)


def _step(x, p, tgt, w, m, v):
    t, d = x.shape
    tt = _tile(t, 256)
    tm = _tile(t, 512)
    tm2 = _tile(t, 1024)
    tq = _tile(t, 256)

    chip = 2 * lax.axis_index("x") + lax.axis_index("y")
    place = jnp.stack([chip, lax.axis_index("c")]).astype(jnp.int32)

    placed = {k: place_shard("place_" + k, w[k], place) for k in MATS}
    full = {}

    def keep(names, bufs):
        for k, buf in zip(names, bufs):
            full[k] = buf if k in COL_SHARDED else buf.reshape(-1, buf.shape[2])

    def gather_of(names):
        return gather_comm([placed[k] for k in names])

    keep(["ffn1_w_in"], run_comm("gather_first", gather_of(["ffn1_w_in"])))
    cw_all = gather_small("gather_conv_w", _pad_rows(w["conv_w"], 8), False)
    cw8 = jnp.concatenate([cw_all[2 * k] for k in range(N_CHIPS)], axis=1)
    g1, gm, g2, gp, gf = (w[k].reshape(1, d) for k in NORMS)

    def ffn_fwd(tag, h, g, w_in, w_out_name, riders):
        n = rms_fwd(tag + "_norm", h, g, tt)
        if riders:
            (a, s), bufs = ffn_in_act(tag + "_in", n, w_in, tm, comm=gather_of(riders))
            keep(riders, bufs)
        else:
            a, s = ffn_in_act(tag + "_in", n, w_in, tm)
        return n, a, s, mm_nn(tag + "_out", s, full[w_out_name], F32, tm, res=h, alpha=0.5)

    n1, a1, s1, h1 = ffn_fwd("ffn1", x, g1, full["ffn1_w_in"], "ffn1_w_out", ["ffn1_w_out", "w_mix_in"])
    u = rms_fwd("mix_norm", h1, gm, tt)
    wmix = full["w_mix_in"]
    riders = [["w_conv_out", "w_attn_out", "w_mix_out"], ["ffn2_w_in"], ["ffn2_w_out", "w_ple_gate", "w_ple_proj"]]
    cbx, bufs = mm_nn_stacked("mix_in_conv", u, wmix, F32, tm2, d, 0, 3, comm=gather_of(riders[0]))
    keep(riders[0], bufs)
    qkv, bufs = mm_nn_stacked("mix_in_qkv", u, wmix, BF16, tm2, d, 3, 3, comm=gather_of(riders[1]))
    keep(riders[1], bufs)
    gates, bufs = mm_nn_stacked("mix_in_gates", u, wmix, F32, tm2, d, 6, 2, comm=gather_of(riders[2]))
    keep(riders[2], bufs)
    wpp = full["w_ple_proj"]
    wpp = jnp.transpose(wpp, (1, 0, 2)).reshape(wpp.shape[1], -1)
    ycin = conv_fwd("conv", cbx, cw8, tt)
    y_conv = mm_nn("conv_out", ycin, full["w_conv_out"], F32, tm)
    o = attn_fwd("attn", qkv, tq)
    y_attn = mm_nn("attn_out", o, full["w_attn_out"], F32, tm)
    merged = gate_fwd("merge", gates, y_conv, y_attn, tt)
    h2 = mm_nn("mix_out", merged, full["w_mix_out"], F32, tm, res=h1, alpha=1.0)
    n2, a2, s2, h3 = ffn_fwd("ffn2", h2, g2, full["ffn2_w_in"], "ffn2_w_out", [])
    npl = rms_fwd("ple_norm", h3, gp, tt)
    zg = mm_nn("ple_gate", npl, full["w_ple_gate"], F32, tm)
    pp = mm_nn("ple_proj", p, wpp, F32, tm)

    pieces, chip_sums, halves = {}, {}, {}

    def as_pieces(k):
        pc = pieces[k]
        return pc if k in COL_SHARDED else pc.reshape(N_CHIPS, 2, pc.shape[0] // (2 * N_CHIPS), pc.shape[1])

    def sum_siblings(tag, names):
        pcs = [as_pieces(k) for k in names]
        got = run_comm("exchange_" + tag, exchange_comm(pcs))
        for k, a, b in zip(names, pcs, got):
            chip_sums[k] = sum_cores("sum_cores_" + k, a, b, place)

    def scatter_of(names):
        return scatter_comm([chip_sums[k] for k in names])

    def sum_landed(names, landed):
        for k, b in zip(names, landed):
            halves[k] = sum_chips("sum_chips_" + k, chip_sums[k], b, place)

    dh4, dpp, dzg, dgf, loss_row = tail("tail", h3, zg, pp, tgt, gf, tt)
    dwpp = mm_tn_whole("ple_proj_dw", p, dpp, tm)
    pieces["w_ple_proj"] = jnp.transpose(dwpp.reshape(2, p.shape[1] // 2, N_CHIPS, d // N_CHIPS), (2, 0, 1, 3))
    pieces["w_ple_gate"] = mm_tn_rows("ple_gate_dw", npl, dzg, tm)
    dnp = mm_nt("ple_gate_dx", dzg, full["w_ple_gate"], F32, tm, d)
    dh3, df2, dgp = rms_bwd("ple_norm_bwd", h3, gp, dnp, dh4, 0.5, tt)
    w_in, w_out = full["ffn2_w_in"], full["ffn2_w_out"]
    pieces["ffn2_w_out"] = mm_tn_rows("ffn2_dwout", s2, df2, tm)
    da2 = ffn_ds_dact("ffn2_ds", df2, w_out, a2, tm)
    pieces["ffn2_w_in"] = mm_tn_cols("ffn2_dwin", n2, da2, tm)
    dn2 = mm_nt_stacked("ffn2_dn", da2, w_in, F32, tm2, w_in.shape[2])
    dh2, dh2b, dg2 = rms_bwd("ffn2_norm_bwd", h2, g2, dn2, dh3, 1.0, tt)
    pieces["w_mix_out"] = mm_tn_rows("mix_out_dw", merged, dh2b, tm)
    dmerged = mm_nt("mix_out_dx", dh2b, full["w_mix_out"], F32, tm, d)
    dyc, dya, dgates = gate_bwd("merge_bwd", dmerged, gates, y_conv, y_attn, tt)
    pieces["w_conv_out"] = mm_tn_rows("conv_out_dw", ycin, dyc, tm)
    dycin = mm_nt("conv_out_dx", dyc, full["w_conv_out"], F32, tm, d)
    dcbx, dcw8 = conv_bwd("conv_bwd", dycin, cbx, cw8, tt)
    pieces["w_attn_out"] = mm_tn_rows("attn_out_dw", o, dya, tm)
    do = mm_nt("attn_out_dx", dya, full["w_attn_out"], BF16, tm, d)
    dq, dk, dv = attn_bwd("attn_bwd", qkv, do, tq)
    dmix = jnp.concatenate([dcbx, dq, dk, dv, dgates], axis=1)
    pieces["w_mix_in"] = mm_tn_cols("mix_in_dw", u, dmix, tm)
    early = ["ffn2_w_in", "ffn2_w_out", "w_ple_gate", "w_ple_proj", "w_mix_out", "w_conv_out", "w_attn_out"]
    sum_siblings("early", early + ["w_mix_in"])
    du, landed = mm_nt_stacked("mix_in_dx", dmix, wmix, F32, tm2, d, comm=scatter_of(early))
    sum_landed(early, landed)
    dh1, df1, dgm = rms_bwd("mix_norm_bwd", h1, gm, du, dh2, 0.5, tt)
    w_in, w_out = full["ffn1_w_in"], full["ffn1_w_out"]
    pieces["ffn1_w_out"] = mm_tn_rows("ffn1_dwout", s1, df1, tm)
    da1 = ffn_ds_dact("ffn1_ds", df1, w_out, a1, tm)
    pieces["ffn1_w_in"], landed = mm_tn_cols("ffn1_dwin", n1, da1, tm, comm=scatter_of(["w_mix_in"]))
    sum_landed(["w_mix_in"], landed)
    late = ["ffn1_w_in", "ffn1_w_out"]
    sum_siblings("late", late)
    dn1, landed = mm_nt_stacked("ffn1_dn", da1, w_in, F32, tm2, w_in.shape[2], comm=scatter_of(late))
    sum_landed(late, landed)
    dx, _, dg1 = rms_bwd("ffn1_norm_bwd", x, g1, dn1, dh1, 1.0, tt)

    shared = run_comm("share_halves", share_comm([halves[k] for k in MATS]))
    grad, delta, new_m, new_v = {}, {}, {}, {}
    for k, sh in zip(MATS, shared):
        grad[k] = sh.reshape(w[k].shape)
        delta[k], new_m[k], new_v[k] = adamw("adamw_" + k, w[k], grad[k], m[k], v[k])

    small = jnp.concatenate([dg1, dgm, dg2, dgp, dgf, dcw8[:3], loss_row, jnp.zeros((7, d), F32)], axis=0)
    tot = gather_small("sum_small", small, True)
    loss = tot[8, 0]
    norm_w = jnp.concatenate([w[k].reshape(1, d) for k in NORMS] + [jnp.zeros((3, d), F32)], axis=0)
    norm_m = jnp.concatenate([m[k].reshape(1, d) for k in NORMS] + [jnp.zeros((3, d), F32)], axis=0)
    norm_v = jnp.concatenate([v[k].reshape(1, d) for k in NORMS] + [jnp.ones((3, d), F32)], axis=0)
    norm_g = jnp.concatenate([tot[0:5], jnp.zeros((3, d), F32)], axis=0)
    nd, nm, nv = adamw("adamw_norms", norm_w, norm_g, norm_m, norm_v)
    for r, k in enumerate(NORMS):
        grad[k] = norm_g[r].reshape(w[k].shape)
        delta[k], new_m[k], new_v[k] = (a[r].reshape(w[k].shape) for a in (nd, nm, nv))
    cs = d // N_CHIPS
    gcw = lax.dynamic_slice(tot[5:8], (0, chip * cs), (3, cs))
    cd, cm, cv = adamw("adamw_conv_w", _pad_rows(w["conv_w"], 8), _pad_rows(gcw, 8), _pad_rows(m["conv_w"], 8),
                       jnp.concatenate([v["conv_w"], jnp.ones((5, cs), F32)], axis=0))
    grad["conv_w"], delta["conv_w"], new_m["conv_w"], new_v["conv_w"] = gcw, cd[:3], cm[:3], cv[:3]
    return loss, dx, grad, delta, new_m, new_v


def kernel(x, p, ffn1_norm, ffn1_w_in, ffn1_w_out, mix_norm, w_mix_in, conv_w, w_conv_out, w_attn_out, w_mix_out, ffn2_norm, ffn2_w_in, ffn2_w_out, ple_norm, w_ple_gate, w_ple_proj, final_norm, loss_target, m_ffn1_norm, m_ffn1_w_in, m_ffn1_w_out, m_mix_norm, m_w_mix_in, m_conv_w, m_w_conv_out, m_w_attn_out, m_w_mix_out, m_ffn2_norm, m_ffn2_w_in, m_ffn2_w_out, m_ple_norm, m_w_ple_gate, m_w_ple_proj, m_final_norm, v_ffn1_norm, v_ffn1_w_in, v_ffn1_w_out, v_mix_norm, v_w_mix_in, v_conv_w, v_w_conv_out, v_w_attn_out, v_w_mix_out, v_ffn2_norm, v_ffn2_w_in, v_ffn2_w_out, v_ple_norm, v_w_ple_gate, v_w_ple_proj, v_final_norm):
    ws = (ffn1_norm, ffn1_w_in, ffn1_w_out, mix_norm, w_mix_in, conv_w, w_conv_out, w_attn_out, w_mix_out, ffn2_norm,
          ffn2_w_in, ffn2_w_out, ple_norm, w_ple_gate, w_ple_proj, final_norm)
    ms = (m_ffn1_norm, m_ffn1_w_in, m_ffn1_w_out, m_mix_norm, m_w_mix_in, m_conv_w, m_w_conv_out, m_w_attn_out,
          m_w_mix_out, m_ffn2_norm, m_ffn2_w_in, m_ffn2_w_out, m_ple_norm, m_w_ple_gate, m_w_ple_proj, m_final_norm)
    vs = (v_ffn1_norm, v_ffn1_w_in, v_ffn1_w_out, v_mix_norm, v_w_mix_in, v_conv_w, v_w_conv_out, v_w_attn_out,
          v_w_mix_out, v_ffn2_norm, v_ffn2_w_in, v_ffn2_w_out, v_ple_norm, v_w_ple_gate, v_w_ple_proj, v_final_norm)
    assert x.shape[0] == 1 and p.shape[:2] == (1, 1), "one sequence and one layer per device"

    def strip(a):
        return a[0] if a.ndim == 3 or (a.ndim == 2 and a.shape[0] == 1) else a

    w = {k: strip(a) for k, a in zip(WEIGHTS, ws)}
    m = {k: strip(a) for k, a in zip(WEIGHTS, ms)}
    v = {k: strip(a) for k, a in zip(WEIGHTS, vs)}
    loss, dx, grad, delta, new_m, new_v = _step(x[0], p[0, 0], loss_target[0], w, m, v)
    shapes = [a.shape for a in ws]
    outs = [loss, dx[None]]
    for res in (grad, delta, new_m, new_v):
        outs += [res[k].reshape(s) for k, s in zip(WEIGHTS, shapes)]
    return tuple(outs)
```

```python
import functools
import math

import jax
import jax.numpy as jnp
from jax import lax
from jax.experimental import pallas as pl
from jax.experimental.pallas import tpu as pltpu

F32 = jnp.float32
BF16 = jnp.bfloat16
MESH = pl.DeviceIdType.MESH
ANY = pl.BlockSpec(memory_space=pl.ANY)

HEAD_DIM = 128
NORM_EPS = 1e-6
N_CHIPS = 4
N_DEV = 8
BF16_ROWS = 16
VMEM_LIMIT = 56 * 1024 * 1024
ACC_BYTES = 8 * 1024 * 1024
STICK_EXIT = 110.0

ADAM_LR = 0.001
ADAM_B1 = 0.9
ADAM_B2 = 0.999
ADAM_EPS = 1e-08
ADAM_WD = 0.01
ADAM_STEP = 10

NN = (((1,), (0,)), ((), ()))
NT = (((1,), (1,)), ((), ()))
TN = (((0,), (0,)), ((), ()))


def _params(sem=None, **kw):
    if sem is not None:
        kw["dimension_semantics"] = sem
    return pltpu.CompilerParams(vmem_limit_bytes=VMEM_LIMIT, **kw)


def _pcall(body, **kw):
    return pl.pallas_call(body, **kw)


def _tile(n, pref, mult=8):
    best = None
    for d in range(mult, min(n, pref) + 1, mult):
        if n % d == 0:
            best = d
    return best if best is not None else n


def _dot(a, b, dims):
    return lax.dot_general(a, b, dims, preferred_element_type=F32)


def _call(name, body, grid, in_specs, out_specs, out_shape, args, scratch=(), sem=None, comm=None):
    n_in, n_out, n_sc = len(in_specs), len(out_specs), len(scratch)
    if comm is None:
        def plain(*refs):
            body(refs[:n_in], refs[n_in:n_in + n_out], refs[n_in + n_out:])

        return _pcall(plain, name=name, out_shape=list(out_shape), grid=grid, in_specs=list(in_specs),
                      out_specs=list(out_specs), scratch_shapes=list(scratch), compiler_params=_params(sem))(*args)
    n_cin, n_cout = len(comm.ins), len(comm.outs)
    steps = math.prod(grid)

    def hosted(*refs):
        ins, c_ins = refs[:n_in], refs[n_in:n_in + n_cin]
        outs = refs[n_in + n_cin:n_in + n_cin + n_out]
        c_outs = refs[n_in + n_cin + n_out:n_in + n_cin + n_out + n_cout]
        rest = refs[n_in + n_cin + n_out + n_cout:]
        sems = rest[n_sc:]
        step = pl.program_id(0)
        for ax in range(1, len(grid)):
            step = step * grid[ax] + pl.program_id(ax)

        @pl.when(step == 0)
        def _():
            comm.first(c_ins, c_outs, sems)

        body(ins, outs, rest[:n_sc])

        @pl.when(step == (3 * steps) // 4)
        def _():
            comm.mid(c_ins, c_outs, sems)

        @pl.when(step == steps - 1)
        def _():
            comm.last(c_ins, c_outs, sems)

    res = _pcall(hosted, name=name, out_shape=list(out_shape) + comm.outs, grid=grid,
                 in_specs=list(in_specs) + [ANY] * n_cin, out_specs=list(out_specs) + [ANY] * n_cout,
                 input_output_aliases={n_in + k: n_out + v for k, v in comm.aliases.items()},
                 scratch_shapes=list(scratch) + comm.sems,
                 compiler_params=_params(("arbitrary",) * len(grid)))(*args, *comm.ins)
    return list(res[:n_out]), list(res[n_out:])


def _mm(name, a, b, out_sds, grid, a_spec, b_spec, o_spec, dims, acc_shape, res=None, alpha=1.0, comm=None):
    nk = grid[2]

    def body(ins, outs, scratch):
        a_ref, b_ref = ins[:2]
        r_ref = ins[2] if res is not None else None
        o_ref = outs[0]

        def finish(r):
            if alpha != 1.0:
                r = r * alpha
            if r_ref is not None:
                r = r_ref[...] + r
            if len(o_ref.shape) == 3:
                half = o_ref.shape[1]
                o_ref[0] = r[:half].astype(o_ref.dtype)
                o_ref[1] = r[half:].astype(o_ref.dtype)
            else:
                o_ref[...] = r.astype(o_ref.dtype)

        if nk == 1:
            finish(_dot(a_ref[...].astype(BF16), b_ref[...].astype(BF16), dims))
        else:
            acc_ref = scratch[0]
            kk = pl.program_id(2)

            @pl.when(kk == 0)
            def _():
                acc_ref[...] = jnp.zeros_like(acc_ref)

            acc_ref[...] += _dot(a_ref[...].astype(BF16), b_ref[...].astype(BF16), dims)

            @pl.when(kk == nk - 1)
            def _():
                finish(acc_ref[...])

    in_specs = [a_spec, b_spec]
    args = [a, b]
    if res is not None:
        in_specs.append(o_spec)
        args.append(res)
    scratch = [] if nk == 1 else [pltpu.VMEM(acc_shape, F32)]
    got = _call(name, body, grid, in_specs, [o_spec], [out_sds], args, scratch,
                ("parallel", "parallel", "arbitrary"), comm)
    return got[0] if comm is None else (got[0][0], got[1])


def ffn_in_act(name, n, w4, tm, comm=None):
    t, d = n.shape
    cs = w4.shape[2]

    def body(ins, outs, scratch):
        n_ref, wg_ref, wu_ref = ins
        a_ref, s_ref = outs
        nv = n_ref[...]
        gate = _dot(nv, wg_ref[...], NN)
        up = _dot(nv, wu_ref[...], NN)
        a_ref[0] = gate.astype(BF16)
        a_ref[1] = up.astype(BF16)
        s_ref[...] = (gate * jax.nn.sigmoid(gate) * up).astype(BF16)

    got = _call(name, body, (t // tm, 2),
                [pl.BlockSpec((tm, d), lambda i, j: (i, 0)),
                 pl.BlockSpec((None, d, cs), lambda i, j: (j, 0, 0)),
                 pl.BlockSpec((None, d, cs), lambda i, j: (2 + j, 0, 0))],
                [pl.BlockSpec((2, tm, cs), lambda i, j: (0, i, j)), pl.BlockSpec((tm, cs), lambda i, j: (i, j))],
                [jax.ShapeDtypeStruct((2, t, 2 * cs), BF16), jax.ShapeDtypeStruct((t, 2 * cs), BF16)],
                [n, w4, w4], (), ("parallel", "parallel"), comm)
    return got if comm is None else (got[0], got[1])


def ffn_ds_dact(name, df, w_out, a3, tm):
    t, d = df.shape
    f = w_out.shape[0]
    cs = f // 2

    def body(ins, outs, scratch):
        df_ref, w_ref, a_ref = ins
        ds = _dot(df_ref[...], w_ref[...], NT)
        gate = a_ref[0].astype(F32)
        up = a_ref[1].astype(F32)
        sg = jax.nn.sigmoid(gate)
        outs[0][0] = (ds * up * sg * (1.0 + gate * (1.0 - sg))).astype(BF16)
        outs[0][1] = (ds * gate * sg).astype(BF16)

    blk = pl.BlockSpec((2, tm, cs), lambda i, j: (0, i, j))
    return _call(name, body, (t // tm, 2),
                 [pl.BlockSpec((tm, d), lambda i, j: (i, 0)), pl.BlockSpec((cs, d), lambda i, j: (j, 0)), blk],
                 [blk], [jax.ShapeDtypeStruct((2, t, f), BF16)], [df, w_out, a3], (), ("parallel", "parallel"))[0]


def mm_nn(name, a, w, out_dtype, tm, res=None, alpha=1.0):
    m, k = a.shape
    n = w.shape[1]
    return _mm(name, a, w, jax.ShapeDtypeStruct((m, n), out_dtype), (m // tm, 1, 1),
               pl.BlockSpec((tm, k), lambda i, j, r: (i, 0)),
               pl.BlockSpec((k, n), lambda i, j, r: (0, 0)),
               pl.BlockSpec((tm, n), lambda i, j, r: (i, 0)), NN, None, res=res, alpha=alpha)


def mm_nn_stacked(name, a, w4, out_dtype, tm, tn, j0=0, nj=None, comm=None):
    m, k = a.shape
    cs = w4.shape[2]
    per = cs // tn
    nj = N_CHIPS * per - j0 if nj is None else nj
    return _mm(name, a, w4, jax.ShapeDtypeStruct((m, nj * tn), out_dtype), (m // tm, nj, 1),
               pl.BlockSpec((tm, k), lambda i, j, r: (i, 0)),
               pl.BlockSpec((None, k, tn), lambda i, j, r: ((j + j0) // per, 0, (j + j0) % per)),
               pl.BlockSpec((tm, tn), lambda i, j, r: (i, j)), NN, None, comm=comm)


def mm_nt(name, dy, w, out_dtype, tm, tko):
    m, n = dy.shape
    k = w.shape[0]
    return _mm(name, dy, w, jax.ShapeDtypeStruct((m, k), out_dtype), (m // tm, k // tko, 1),
               pl.BlockSpec((tm, n), lambda i, j, r: (i, 0)),
               pl.BlockSpec((tko, n), lambda i, j, r: (j, 0)),
               pl.BlockSpec((tm, tko), lambda i, j, r: (i, j)), NT, None)


def mm_nt_stacked(name, dy, w4, out_dtype, tm, tn, comm=None):
    m = dy.shape[-2]
    k, cs = w4.shape[1], w4.shape[2]
    per = cs // tn
    if dy.ndim == 3:
        dy_spec = pl.BlockSpec((None, tm, cs), lambda i, j, r: (r // 2, i, r % 2))
    else:
        dy_spec = pl.BlockSpec((tm, tn), lambda i, j, r: (i, r))
    return _mm(name, dy, w4, jax.ShapeDtypeStruct((m, k), out_dtype), (m // tm, 1, N_CHIPS * per), dy_spec,
               pl.BlockSpec((None, k, tn), lambda i, j, r: (r // per, 0, r % per)),
               pl.BlockSpec((tm, k), lambda i, j, r: (i, 0)), NT, (tm, k), comm=comm)


def mm_tn_rows(name, xa, dy, tt):
    t, k = xa.shape
    n = dy.shape[1]
    tkr = k if k * n * 4 <= ACC_BYTES else k // 2
    return _mm(name, xa, dy, jax.ShapeDtypeStruct((k, n), BF16), (k // tkr, 1, t // tt),
               pl.BlockSpec((tt, tkr), lambda i, j, r: (r, i)),
               pl.BlockSpec((tt, n), lambda i, j, r: (r, 0)),
               pl.BlockSpec((tkr, n), lambda i, j, r: (i, 0)), TN, (tkr, n))


def mm_tn_whole(name, xa, dy, tt):
    t, k = xa.shape
    n = dy.shape[1]
    return _mm(name, xa, dy, jax.ShapeDtypeStruct((k, n), BF16), (1, 1, t // tt),
               pl.BlockSpec((tt, k), lambda i, j, r: (r, 0)),
               pl.BlockSpec((tt, n), lambda i, j, r: (r, 0)),
               pl.BlockSpec((k, n), lambda i, j, r: (0, 0)), TN, (k, n))


def mm_tn_cols(name, xa, dy, tt, comm=None):
    t, k = xa.shape
    pr = k // 2
    if dy.ndim == 3:
        cs = dy.shape[2] // 2
        dy_spec = pl.BlockSpec((None, tt, cs), lambda i, j, r: (j // 2, r, j % 2))
    else:
        cs = dy.shape[1] // N_CHIPS
        dy_spec = pl.BlockSpec((tt, cs), lambda i, j, r: (r, j))
    return _mm(name, xa, dy, jax.ShapeDtypeStruct((N_CHIPS, 2, pr, cs), BF16), (1, N_CHIPS, t // tt),
               pl.BlockSpec((tt, k), lambda i, j, r: (r, 0)), dy_spec,
               pl.BlockSpec((None, 2, pr, cs), lambda i, j, r: (j, 0, 0, 0)), TN, (k, cs), comm=comm)


def _rows(tt, w, col=0):
    return pl.BlockSpec((tt, w), lambda i: (i, col))


def _whole(shape):
    return pl.BlockSpec(shape, lambda i: (0,) * len(shape))


def _rstd(h):
    return lax.rsqrt(jnp.mean(h * h, axis=-1, keepdims=True) + NORM_EPS)


def rms_fwd(name, h, g, tt):
    t, d = h.shape

    def body(h_ref, g_ref, o_ref):
        hv = h_ref[...]
        o_ref[...] = (hv * _rstd(hv) * g_ref[...]).astype(o_ref.dtype)

    return _pcall(body, name=name, out_shape=jax.ShapeDtypeStruct((t, d), BF16), grid=(t // tt,),
                  in_specs=[_rows(tt, d), _whole((1, d))], out_specs=_rows(tt, d),
                  compiler_params=_params(("parallel",)))(h, g)


def rms_bwd(name, h, g, dn, dres, alpha, tt):
    t, d = h.shape

    def body(h_ref, g_ref, dn_ref, dr_ref, dh_ref, dhb_ref, dg_ref):
        hv = h_ref[...]
        hn = hv * _rstd(hv)
        dnv = dn_ref[...]
        gy = dnv * g_ref[...]
        dh = dr_ref[...] + _rstd(hv) * (gy - hn * jnp.mean(gy * hn, axis=-1, keepdims=True))
        dh_ref[...] = dh
        dhb_ref[...] = (alpha * dh).astype(BF16)

        @pl.when(pl.program_id(0) == 0)
        def _():
            dg_ref[...] = jnp.zeros_like(dg_ref)

        dg_ref[...] += jnp.sum(dnv * hn, axis=0, keepdims=True)

    return _pcall(body, name=name,
                  out_shape=(jax.ShapeDtypeStruct((t, d), F32), jax.ShapeDtypeStruct((t, d), BF16),
                             jax.ShapeDtypeStruct((1, d), F32)),
                  grid=(t // tt,),
                  in_specs=[_rows(tt, d), _whole((1, d)), _rows(tt, d), _rows(tt, d)],
                  out_specs=(_rows(tt, d), _rows(tt, d), _whole((1, d))),
                  compiler_params=_params(("arbitrary",)))(h, g, dn, dres)


def gate_fwd(name, gates, yc, ya, tt):
    t, d = yc.shape

    def body(g_ref, yc_ref, ya_ref, o_ref):
        o_ref[...] = (jax.nn.sigmoid(g_ref[:, :d]) * yc_ref[...]
                      + jax.nn.sigmoid(g_ref[:, d:]) * ya_ref[...]).astype(o_ref.dtype)

    return _pcall(body, name=name, out_shape=jax.ShapeDtypeStruct((t, d), BF16), grid=(t // tt,),
                  in_specs=[_rows(tt, 2 * d), _rows(tt, d), _rows(tt, d)], out_specs=_rows(tt, d),
                  compiler_params=_params(("parallel",)))(gates, yc, ya)


def gate_bwd(name, dm, gates, yc, ya, tt):
    t, d = yc.shape

    def body(dm_ref, g_ref, yc_ref, ya_ref, dyc_ref, dya_ref, dg_ref):
        dmv = dm_ref[...]
        sc = jax.nn.sigmoid(g_ref[:, :d])
        sa = jax.nn.sigmoid(g_ref[:, d:])
        dyc_ref[...] = (dmv * sc).astype(BF16)
        dya_ref[...] = (dmv * sa).astype(BF16)
        dg_ref[:, :d] = (dmv * yc_ref[...] * sc * (1.0 - sc)).astype(BF16)
        dg_ref[:, d:] = (dmv * ya_ref[...] * sa * (1.0 - sa)).astype(BF16)

    return _pcall(body, name=name,
                  out_shape=(jax.ShapeDtypeStruct((t, d), BF16), jax.ShapeDtypeStruct((t, d), BF16),
                             jax.ShapeDtypeStruct((t, 2 * d), BF16)),
                  grid=(t // tt,),
                  in_specs=[_rows(tt, d), _rows(tt, 2 * d), _rows(tt, d), _rows(tt, d)],
                  out_specs=(_rows(tt, d), _rows(tt, d), _rows(tt, 2 * d)),
                  compiler_params=_params(("parallel",)))(dm, gates, yc, ya)


def _shift_down(cur, prev8, s):
    tt = cur.shape[0]
    rolled = pltpu.roll(cur, s, 0)
    row8 = lax.broadcasted_iota(jnp.int32, prev8.shape, 0)
    first8 = jnp.where(row8 < s, pltpu.roll(prev8, s, 0), rolled[:8])
    return jnp.concatenate([first8, rolled[8:]], axis=0) if tt > 8 else first8


def _shift_up(cur, next8, s):
    tt = cur.shape[0]
    rolled = pltpu.roll(cur, tt - s, 0)
    row8 = lax.broadcasted_iota(jnp.int32, next8.shape, 0)
    last8 = jnp.where(row8 >= 8 - s, pltpu.roll(next8, 8 - s, 0), rolled[tt - 8:])
    return jnp.concatenate([rolled[:tt - 8], last8], axis=0) if tt > 8 else last8


def _prev8(tt, d, col):
    return pl.BlockSpec((8, d), lambda i: (jnp.maximum(i * (tt // 8) - 1, 0), col))


def _next8(tt, d, col, t):
    return pl.BlockSpec((8, d), lambda i: (jnp.minimum((i + 1) * (tt // 8), t // 8 - 1), col))


def conv_fwd(name, cbx, cw8, tt):
    t, d3 = cbx.shape
    d = d3 // 3

    def body(cb_ref, cc_ref, cx_ref, pc_ref, px_ref, w_ref, o_ref):
        has_prev = (pl.program_id(0) > 0).astype(F32)
        cc = cc_ref[...] * cx_ref[...]
        prev = pc_ref[...] * px_ref[...] * has_prev
        w = w_ref[...]
        conv = w[0:1] * _shift_down(cc, prev, 2) + w[1:2] * _shift_down(cc, prev, 1) + w[2:3] * cc
        o_ref[...] = (cb_ref[...] * conv).astype(o_ref.dtype)

    return _pcall(body, name=name, out_shape=jax.ShapeDtypeStruct((t, d), BF16), grid=(t // tt,),
                  in_specs=[_rows(tt, d, 0), _rows(tt, d, 1), _rows(tt, d, 2), _prev8(tt, d, 1), _prev8(tt, d, 2),
                            _whole((8, d))],
                  out_specs=_rows(tt, d), compiler_params=_params(("parallel",)))(cbx, cbx, cbx, cbx, cbx, cw8)


def conv_bwd(name, dyc, cbx, cw8, tt):
    t, d3 = cbx.shape
    d = d3 // 3
    n = t // tt

    def body(dy_ref, cb_ref, cc_ref, cx_ref, pc_ref, px_ref, ndy_ref, ncb_ref, w_ref, o_ref, dw_ref):
        i = pl.program_id(0)
        has_prev = (i > 0).astype(F32)
        has_next = (i < n - 1).astype(F32)
        cb = cb_ref[...]
        cc = cc_ref[...] * cx_ref[...]
        prev = pc_ref[...] * px_ref[...] * has_prev
        w = w_ref[...]
        cc1 = _shift_down(cc, prev, 1)
        cc2 = _shift_down(cc, prev, 2)
        conv = w[0:1] * cc2 + w[1:2] * cc1 + w[2:3] * cc
        dyv = dy_ref[...]
        dconv = dyv * cb
        dnext = ndy_ref[...] * ncb_ref[...] * has_next
        dcc = w[2:3] * dconv + w[1:2] * _shift_up(dconv, dnext, 1) + w[0:1] * _shift_up(dconv, dnext, 2)
        o_ref[:, :d] = (dyv * conv).astype(BF16)
        o_ref[:, d:2 * d] = (dcc * cx_ref[...]).astype(BF16)
        o_ref[:, 2 * d:] = (dcc * cc_ref[...]).astype(BF16)

        @pl.when(i == 0)
        def _():
            dw_ref[...] = jnp.zeros_like(dw_ref)

        dw_ref[0:1, :] += jnp.sum(dconv * cc2, axis=0, keepdims=True)
        dw_ref[1:2, :] += jnp.sum(dconv * cc1, axis=0, keepdims=True)
        dw_ref[2:3, :] += jnp.sum(dconv * cc, axis=0, keepdims=True)

    return _pcall(body, name=name,
                  out_shape=(jax.ShapeDtypeStruct((t, d3), BF16), jax.ShapeDtypeStruct((8, d), F32)),
                  grid=(n,),
                  in_specs=[_rows(tt, d), _rows(tt, d, 0), _rows(tt, d, 1), _rows(tt, d, 2),
                            _prev8(tt, d, 1), _prev8(tt, d, 2), _next8(tt, d, 0, t), _next8(tt, d, 0, t),
                            _whole((8, d))],
                  out_specs=(_rows(tt, d3), _whole((8, d))),
                  compiler_params=_params(("arbitrary",)))(dyc, cbx, cbx, cbx, cbx, cbx, dyc, cbx, cw8)


def tail(name, h3, zg, pp, tgt, gf, tt):
    t, d = h3.shape

    def body(h_ref, zg_ref, pp_ref, tg_ref, gf_ref, dh_ref, dpp_ref, dzg_ref, dgf_ref, loss_ref):
        pg = jax.nn.sigmoid(zg_ref[...])
        ppv = pp_ref[...]
        h4 = h_ref[...] + pg * ppv
        r4 = _rstd(h4)
        hn = h4 * r4
        gfv = gf_ref[...]
        err = hn * gfv - tg_ref[...]
        dy = err * (1.0 / d)
        gy = dy * gfv
        dh4 = r4 * (gy - hn * jnp.mean(gy * hn, axis=-1, keepdims=True))
        dh_ref[...] = dh4
        dpp_ref[...] = (dh4 * pg).astype(BF16)
        dzg_ref[...] = (dh4 * ppv * pg * (1.0 - pg)).astype(BF16)

        @pl.when(pl.program_id(0) == 0)
        def _():
            dgf_ref[...] = jnp.zeros_like(dgf_ref)
            loss_ref[...] = jnp.zeros_like(loss_ref)

        dgf_ref[...] += jnp.sum(dy * hn, axis=0, keepdims=True)
        tok = jnp.mean(err * err, axis=-1, keepdims=True)
        loss_ref[...] += 0.5 * jnp.sum(tok, axis=0, keepdims=True) * jnp.ones((1, loss_ref.shape[1]), F32)

    return _pcall(body, name=name,
                  out_shape=(jax.ShapeDtypeStruct((t, d), F32), jax.ShapeDtypeStruct((t, d), BF16),
                             jax.ShapeDtypeStruct((t, d), BF16), jax.ShapeDtypeStruct((1, d), F32),
                             jax.ShapeDtypeStruct((1, d), F32)),
                  grid=(t // tt,),
                  in_specs=[_rows(tt, d)] * 4 + [_whole((1, d))],
                  out_specs=(_rows(tt, d), _rows(tt, d), _rows(tt, d), _whole((1, d)), _whole((1, d))),
                  compiler_params=_params(("arbitrary",)))(h3, zg, pp, tgt, gf)


SCALE = 1.0 / math.sqrt(HEAD_DIM)


def _log_stick(z):
    return -(jnp.maximum(z, 0.0) + jnp.log(1.0 + jnp.exp(-jnp.abs(z))))


def _tri_sum(x, tri):
    hi = x.astype(BF16)
    lo = (x - hi.astype(F32)).astype(BF16)
    return _dot(hi, tri, NN) + _dot(lo, tri, NN)


def _sb_pair(q, k_d, k_p, below, upper, has_prev):
    z_d = _dot(q, k_d, NT) * SCALE
    z_p = _dot(q, k_p, NT) * SCALE
    cum_d = _tri_sum(jnp.where(below, _log_stick(z_d), 0.0), upper)
    cum_p = _tri_sum(jnp.where(has_prev, _log_stick(z_p), 0.0), upper)
    c_d = cum_d[:, 0:1]
    a_d = jnp.exp(jnp.where(below, z_d + cum_d, -1e30))
    a_p = jnp.exp(jnp.where(has_prev, z_p + cum_p + c_d, -1e30))
    return z_d, z_p, a_d, a_p, c_d + cum_p[:, 0:1]


def _sb_far(q, kj, upper, c):
    z = _dot(q, kj, NT) * SCALE
    cum = _tri_sum(_log_stick(z), upper)
    return z, jnp.exp(z + cum + c), c + cum[:, 0:1]


def _block_rows(j, tq):
    return pl.ds(pl.multiple_of(j * tq, tq), tq)


def attn_fwd(name, qkv, tq):
    t, d3 = qkv.shape
    d = d3 // 3
    nh = d // HEAD_DIM
    nq = t // tq

    def body(q_ref, k_ref, v_ref, o_ref):
        i = pl.program_id(1)
        q = q_ref[...]
        row = lax.broadcasted_iota(jnp.int32, (tq, tq), 0)
        col = lax.broadcasted_iota(jnp.int32, (tq, tq), 1)
        upper = (row >= col).astype(BF16)
        rows_d = _block_rows(i, tq)
        rows_p = _block_rows(jnp.maximum(i - 1, 0), tq)
        _, _, a_d, a_p, c = _sb_pair(q, k_ref[rows_d, :], k_ref[rows_p, :], col < row, upper, i > 0)
        acc = _dot(a_d.astype(BF16), v_ref[rows_d, :], NN) + _dot(a_p.astype(BF16), v_ref[rows_p, :], NN)

        def cond(st):
            return jnp.logical_and(st[0] >= 0, jnp.max(st[1]) > -STICK_EXIT)

        def step(st):
            rows = _block_rows(st[0], tq)
            _, a, c2 = _sb_far(q, k_ref[rows, :], upper, st[1])
            return st[0] - 1, c2, st[2] + _dot(a.astype(BF16), v_ref[rows, :], NN)

        _, _, acc = lax.while_loop(cond, step, (i - 2, c, acc))
        o_ref[...] = acc.astype(o_ref.dtype)

    return _pcall(body, name=name, out_shape=jax.ShapeDtypeStruct((t, d), BF16), grid=(nh, nq),
                  in_specs=[pl.BlockSpec((tq, HEAD_DIM), lambda h, i: (i, h)),
                            pl.BlockSpec((t, HEAD_DIM), lambda h, i: (0, nh + h)),
                            pl.BlockSpec((t, HEAD_DIM), lambda h, i: (0, 2 * nh + h))],
                  out_specs=pl.BlockSpec((tq, HEAD_DIM), lambda h, i: (i, h)),
                  compiler_params=_params(("parallel", "arbitrary")))(qkv, qkv, qkv)


def attn_bwd(name, qkv, do, tq):
    t, d3 = qkv.shape
    d = d3 // 3
    nh = d // HEAD_DIM
    nq = t // tq

    def body(q_ref, k_ref, v_ref, do_ref, dq_ref, dk_ref, dv_ref, dk_acc, dv_acc, g_buf, z_buf):
        i = pl.program_id(1)

        @pl.when(i == 0)
        def _():
            dk_acc[...] = jnp.zeros_like(dk_acc)
            dv_acc[...] = jnp.zeros_like(dv_acc)

        q = q_ref[...]
        dov = do_ref[...]
        row = lax.broadcasted_iota(jnp.int32, (tq, tq), 0)
        col = lax.broadcasted_iota(jnp.int32, (tq, tq), 1)
        below = col < row
        has_prev = i > 0
        upper = (row >= col).astype(BF16)
        lower = (row <= col).astype(BF16)
        rows_d = _block_rows(i, tq)
        rows_p = _block_rows(jnp.maximum(i - 1, 0), tq)

        z_d, z_p, a_d, a_p, c = _sb_pair(q, k_ref[rows_d, :], k_ref[rows_p, :], below, upper, has_prev)
        g_d = _dot(dov, v_ref[rows_d, :], NT) * a_d
        g_p = _dot(dov, v_ref[rows_p, :], NT) * a_p
        dv_acc[rows_d, :] += _dot(a_d.astype(BF16), dov, TN)
        dv_acc[rows_p, :] += _dot(a_p.astype(BF16), dov, TN)

        def cond(st):
            return jnp.logical_and(st[0] >= 0, jnp.max(st[1]) > -STICK_EXIT)

        def step(st):
            j = st[0]
            rows = _block_rows(j, tq)
            z, a, c2 = _sb_far(q, k_ref[rows, :], upper, st[1])
            g_buf[i - j] = _dot(dov, v_ref[rows, :], NT) * a
            z_buf[i - j] = z
            dv_acc[rows, :] += _dot(a.astype(BF16), dov, TN)
            return j - 1, c2

        j_stop, _ = lax.while_loop(cond, step, (i - 2, c))

        def far(j, st):
            run, dq = st
            rows = _block_rows(j, tq)
            g = g_buf[i - j]
            dz = (g - jax.nn.sigmoid(z_buf[i - j]) * (run + _tri_sum(g, lower))).astype(BF16)
            dk_acc[rows, :] += _dot(dz, q, TN)
            return run + jnp.sum(g, axis=1, keepdims=True), dq + _dot(dz, k_ref[rows, :], NN)

        run, dq = lax.fori_loop(j_stop + 1, i - 1, far,
                                (jnp.zeros((tq, 1), F32), jnp.zeros((tq, HEAD_DIM), F32)))
        p_p = run + _tri_sum(g_p, lower)
        p_d = run + jnp.sum(g_p, axis=1, keepdims=True) + _tri_sum(g_d, lower)
        dz_p = jnp.where(has_prev, g_p - jax.nn.sigmoid(z_p) * p_p, 0.0).astype(BF16)
        dz_d = jnp.where(below, g_d - jax.nn.sigmoid(z_d) * p_d, 0.0).astype(BF16)
        dk_acc[rows_p, :] += _dot(dz_p, q, TN)
        dk_acc[rows_d, :] += _dot(dz_d, q, TN)
        dq = dq + _dot(dz_p, k_ref[rows_p, :], NN) + _dot(dz_d, k_ref[rows_d, :], NN)
        dq_ref[...] = (dq * SCALE).astype(BF16)

        @pl.when(i == nq - 1)
        def _():
            dk_ref[...] = (dk_acc[...] * SCALE).astype(BF16)
            dv_ref[...] = dv_acc[...].astype(BF16)

    blk = pl.BlockSpec((tq, HEAD_DIM), lambda h, i: (i, h))
    col_h = pl.BlockSpec((t, HEAD_DIM), lambda h, i: (0, h))
    out = jax.ShapeDtypeStruct((t, d), BF16)
    return _pcall(body, name=name, out_shape=(out, out, out), grid=(nh, nq),
                  in_specs=[blk,
                            pl.BlockSpec((t, HEAD_DIM), lambda h, i: (0, nh + h)),
                            pl.BlockSpec((t, HEAD_DIM), lambda h, i: (0, 2 * nh + h)),
                            blk],
                  out_specs=(blk, col_h, col_h),
                  scratch_shapes=[pltpu.VMEM((t, HEAD_DIM), F32), pltpu.VMEM((t, HEAD_DIM), F32),
                                  pltpu.VMEM((nq, tq, tq), F32), pltpu.VMEM((nq, tq, tq), F32)],
                  compiler_params=_params(("parallel", "arbitrary")))(qkv, qkv, qkv, do)


def _place():
    x, y, c = lax.axis_index("x"), lax.axis_index("y"), lax.axis_index("c")
    chips = [(1 - x, y), (x, 1 - y), (1 - x, 1 - y)]
    return x, y, c, chips


def _remote(src, dst, send_sem, recv_sem, dev):
    return pltpu.make_async_remote_copy(src_ref=src, dst_ref=dst, send_sem=send_sem, recv_sem=recv_sem,
                                        device_id=dev, device_id_type=MESH)


def place_shard(name, w, chip):
    r, cdim = w.shape
    tr = _tile(r, max(BF16_ROWS, (1 << 19) // cdim), BF16_ROWS)

    def body(chip_ref, w_ref, o_ref):
        o_ref[...] = w_ref[...].astype(BF16)

    spec = pltpu.PrefetchScalarGridSpec(
        num_scalar_prefetch=1, grid=(r // tr,),
        in_specs=[pl.BlockSpec((tr, cdim), lambda i, s: (i, 0))],
        out_specs=pl.BlockSpec((None, tr, cdim), lambda i, s: (s[0], i, 0)))
    return _pcall(body, name=name, out_shape=jax.ShapeDtypeStruct((N_CHIPS, r, cdim), BF16), grid_spec=spec,
                  compiler_params=_params(("parallel",)))(chip, w)


class Comm:
    def __init__(self, ins, outs, aliases, sems, first, mid, last):
        self.ins, self.outs, self.aliases, self.sems = list(ins), list(outs), dict(aliases), list(sems)
        self.first, self.mid, self.last = first, mid, last


def run_comm(name, comm):
    ni, no = len(comm.ins), len(comm.outs)

    def body(*refs):
        ins, outs, sems = refs[:ni], refs[ni:ni + no], refs[ni + no:]
        comm.first(ins, outs, sems)
        comm.mid(ins, outs, sems)
        comm.last(ins, outs, sems)

    return _pcall(body, name=name, out_shape=comm.outs, in_specs=[ANY] * ni, out_specs=[ANY] * no,
                  input_output_aliases=comm.aliases, scratch_shapes=comm.sems, compiler_params=_params())(*comm.ins)


def gather_comm(bufs):
    n = len(bufs)

    def half(out, w, which):
        pr = out[w].shape[1] // 2
        return pl.ds(pl.multiple_of(which * pr, BF16_ROWS), pr)

    def first(ins, out, sems):
        isend, irecv, _, _ = sems
        x, y, c, chips = _place()
        for w in range(n):
            mine = out[w].at[2 * x + y, half(out, w, c)]
            for j, (cx, cy) in enumerate(chips):
                _remote(mine, mine, isend.at[3 * w + j], irecv.at[3 * w + j], (cx, cy, c)).start()

    def mid(ins, out, sems):
        isend, irecv, dsend, drecv = sems
        x, y, c, chips = _place()
        sib = (x, y, 1 - c)
        for w in range(n):
            for j, (cx, cy) in enumerate(chips):
                landed = out[w].at[2 * cx + cy, half(out, w, c)]
                _remote(landed, landed, isend.at[3 * w + j], irecv.at[3 * w + j], sib).wait_recv()
                _remote(landed, landed, dsend.at[3 * w + j], drecv.at[3 * w + j], sib).start()

    def last(ins, out, sems):
        isend, irecv, dsend, drecv = sems
        x, y, c, chips = _place()
        sib = (x, y, 1 - c)
        for w in range(n):
            for j, (cx, cy) in enumerate(chips):
                landed = out[w].at[2 * cx + cy, half(out, w, 1 - c)]
                _remote(landed, landed, dsend.at[3 * w + j], drecv.at[3 * w + j], sib).wait_recv()
        for w in range(n):
            sent = out[w].at[0, half(out, w, c)]
            for j in range(3):
                _remote(sent, sent, isend.at[3 * w + j], irecv.at[3 * w + j], sib).wait_send()
                _remote(sent, sent, dsend.at[3 * w + j], drecv.at[3 * w + j], sib).wait_send()

    return Comm(bufs, [jax.ShapeDtypeStruct(s.shape, s.dtype) for s in bufs], {w: w for w in range(n)},
                [pltpu.SemaphoreType.DMA((3 * n,))] * 4, first, mid, last)


def _nothing(ins, outs, sems):
    return None


def exchange_comm(pieces):
    n = len(pieces)

    def copies(src, out, sems):
        x, y, c, _ = _place()
        return [_remote(src[w].at[k, 1 - c], out[w].at[k], sems[0].at[N_CHIPS * w + k], sems[1].at[N_CHIPS * w + k],
                        (x, y, 1 - c)) for w in range(n) for k in range(N_CHIPS)]

    def first(src, out, sems):
        for cp in copies(src, out, sems):
            cp.start()

    def last(src, out, sems):
        for cp in copies(src, out, sems):
            cp.wait()

    return Comm(pieces, [jax.ShapeDtypeStruct((N_CHIPS,) + s.shape[2:], s.dtype) for s in pieces], {},
                [pltpu.SemaphoreType.DMA((N_CHIPS * n,))] * 2, first, _nothing, last)


def scatter_comm(parts):
    n = len(parts)

    def copies(src, out, sems):
        x, y, c, chips = _place()
        return [_remote(src[w].at[2 * cx + cy], out[w].at[j], sems[0].at[3 * w + j], sems[1].at[3 * w + j], (cx, cy, c))
                for w in range(n) for j, (cx, cy) in enumerate(chips)]

    def first(src, out, sems):
        for cp in copies(src, out, sems):
            cp.start()

    def last(src, out, sems):
        for cp in copies(src, out, sems):
            cp.wait()

    return Comm(parts, [jax.ShapeDtypeStruct((3,) + s.shape[1:], s.dtype) for s in parts], {},
                [pltpu.SemaphoreType.DMA((3 * n,))] * 2, first, _nothing, last)


def share_comm(halves):
    n = len(halves)

    def first(ins, buf, sems):
        x, y, c, _ = _place()
        for w in range(n):
            _remote(buf[w].at[c], buf[w].at[c], sems[0].at[w], sems[1].at[w], (x, y, 1 - c)).start()

    def last(ins, buf, sems):
        x, y, c, _ = _place()
        for w in range(n):
            landed = buf[w].at[1 - c]
            _remote(landed, landed, sems[0].at[w], sems[1].at[w], (x, y, 1 - c)).wait_recv()
        for w in range(n):
            _remote(buf[w].at[c], buf[w].at[c], sems[0].at[w], sems[1].at[w], (x, y, 1 - c)).wait_send()

    return Comm(halves, [jax.ShapeDtypeStruct(s.shape, s.dtype) for s in halves], {w: w for w in range(n)},
                [pltpu.SemaphoreType.DMA((n,))] * 2, first, _nothing, last)


def gather_small(name, blk, reduce):
    r, cdim = blk.shape

    def body(in_ref, out_ref, *rest):
        if reduce:
            buf, send_sem, recv_sem = rest
        else:
            buf = out_ref
            send_sem, recv_sem = rest
        x, y, c, _ = _place()
        me = 4 * x + 2 * y + c
        buf[me] = in_ref[...]
        peers = []
        for dx in range(2):
            for dy in range(2):
                for dc in range(2):
                    if dx or dy or dc:
                        peers.append((dx, dy, dc))
        copies = []
        for s, (dx, dy, dc) in enumerate(peers):
            cp = _remote(in_ref, buf.at[me], send_sem.at[s], recv_sem.at[s],
                         ((1 - x if dx else x), (1 - y if dy else y), (1 - c if dc else c)))
            cp.start()
            copies.append(cp)
        for s, (dx, dy, dc) in enumerate(peers):
            px, py, pc_ = (1 - x if dx else x), (1 - y if dy else y), (1 - c if dc else c)
            landed = buf.at[4 * px + 2 * py + pc_]
            _remote(landed, landed, send_sem.at[s], recv_sem.at[s], (x, y, c)).wait_recv()
        for cp in copies:
            cp.wait_send()
        if reduce:
            tot = buf[0]
            for s in range(1, N_DEV):
                tot = tot + buf[s]
            out_ref[...] = tot

    vm = pl.BlockSpec(memory_space=pltpu.VMEM)
    out_shape = jax.ShapeDtypeStruct((r, cdim) if reduce else (N_DEV, r, cdim), F32)
    scratch = ([pltpu.VMEM((N_DEV, r, cdim), F32)] if reduce else []) + [pltpu.SemaphoreType.DMA((N_DEV - 1,))] * 2
    return _pcall(body, name=name, out_shape=out_shape, in_specs=[vm], out_specs=vm, scratch_shapes=scratch,
                  compiler_params=_params())(blk)


def sum_cores(name, own, got, place):
    _, _, pr, pc = own.shape
    tr = _tile(pr, max(BF16_ROWS, (1 << 19) // pc), BF16_ROWS)

    def body(place_ref, own_ref, got_ref, o_ref):
        o_ref[...] = (own_ref[...].astype(F32) + got_ref[...].astype(F32)).astype(o_ref.dtype)

    spec = pltpu.PrefetchScalarGridSpec(
        num_scalar_prefetch=1, grid=(N_CHIPS, pr // tr),
        in_specs=[pl.BlockSpec((None, None, tr, pc), lambda k, i, s: (k, s[1], i, 0)),
                  pl.BlockSpec((None, tr, pc), lambda k, i, s: (k, i, 0))],
        out_specs=pl.BlockSpec((None, tr, pc), lambda k, i, s: (k, i, 0)))
    return _pcall(body, name=name, out_shape=jax.ShapeDtypeStruct((N_CHIPS, pr, pc), BF16), grid_spec=spec,
                  compiler_params=_params(("parallel", "parallel")))(place, own, got)


def sum_chips(name, part, got, place):
    _, pr, pc = part.shape
    tr = _tile(pr, max(BF16_ROWS, (1 << 18) // pc), BF16_ROWS)

    def body(place_ref, part_ref, got_ref, o_ref):
        tot = part_ref[...].astype(F32)
        for j in range(3):
            tot = tot + got_ref[j].astype(F32)
        o_ref[...] = tot

    spec = pltpu.PrefetchScalarGridSpec(
        num_scalar_prefetch=1, grid=(pr // tr,),
        in_specs=[pl.BlockSpec((None, tr, pc), lambda i, s: (s[0], i, 0)),
                  pl.BlockSpec((3, tr, pc), lambda i, s: (0, i, 0))],
        out_specs=pl.BlockSpec((None, tr, pc), lambda i, s: (s[1], i, 0)))
    return _pcall(body, name=name, out_shape=jax.ShapeDtypeStruct((2, pr, pc), F32), grid_spec=spec,
                  compiler_params=_params(("parallel",)))(place, part, got)


def adamw(name, w, g, m, v):
    rows, cols = w.shape
    tr = _tile(rows, max(8, (1 << 18) // cols))
    c1 = 1.0 / (1.0 - ADAM_B1 ** ADAM_STEP)
    c2 = 1.0 / (1.0 - ADAM_B2 ** ADAM_STEP)

    def body(w_ref, g_ref, m_ref, v_ref, d_ref, nm_ref, nv_ref):
        gv = g_ref[...]
        nm = ADAM_B1 * m_ref[...] + (1.0 - ADAM_B1) * gv
        nv = ADAM_B2 * v_ref[...] + (1.0 - ADAM_B2) * (gv * gv)
        nm_ref[...] = nm
        nv_ref[...] = nv
        d_ref[...] = -ADAM_LR * ((nm * c1) / (jnp.sqrt(nv * c2) + ADAM_EPS) + ADAM_WD * w_ref[...])

    spec = pl.BlockSpec((tr, cols), lambda i: (i, 0))
    sds = jax.ShapeDtypeStruct((rows, cols), F32)
    return _pcall(body, name=name, out_shape=(sds, sds, sds), grid=(rows // tr,),
                  in_specs=[spec] * 4, out_specs=(spec, spec, spec),
                  compiler_params=_params(("parallel",)))(w, g, m, v)


MATS = ["ffn1_w_in", "ffn1_w_out", "w_mix_in", "w_conv_out", "w_attn_out", "w_mix_out", "ffn2_w_in", "ffn2_w_out",
        "w_ple_gate", "w_ple_proj"]
COL_SHARDED = {"ffn1_w_in", "w_mix_in", "ffn2_w_in", "w_ple_proj"}
NORMS = ["ffn1_norm", "mix_norm", "ffn2_norm", "ple_norm", "final_norm"]
WEIGHTS = ["ffn1_norm", "ffn1_w_in", "ffn1_w_out", "mix_norm", "w_mix_in", "conv_w", "w_conv_out", "w_attn_out",
           "w_mix_out", "ffn2_norm", "ffn2_w_in", "ffn2_w_out", "ple_norm", "w_ple_gate", "w_ple_proj", "final_norm"]


def _pad_rows(a, rows):
    return jnp.concatenate([a, jnp.zeros((rows - a.shape[0],) + a.shape[1:], a.dtype)], axis=0)


def _step(x, p, tgt, w, m, v):
    t, d = x.shape
    tt = _tile(t, 256)
    tm = _tile(t, 512)
    tm2 = _tile(t, 1024)
    tq = _tile(t, 256)

    chip = 2 * lax.axis_index("x") + lax.axis_index("y")
    place = jnp.stack([chip, lax.axis_index("c")]).astype(jnp.int32)

    placed = {k: place_shard("place_" + k, w[k], place) for k in MATS}
    full = {}

    def keep(names, bufs):
        for k, buf in zip(names, bufs):
            full[k] = buf if k in COL_SHARDED else buf.reshape(-1, buf.shape[2])

    def gather_of(names):
        return gather_comm([placed[k] for k in names])

    keep(["ffn1_w_in"], run_comm("gather_first", gather_of(["ffn1_w_in"])))
    cw_all = gather_small("gather_conv_w", _pad_rows(w["conv_w"], 8), False)
    cw8 = jnp.concatenate([cw_all[2 * k] for k in range(N_CHIPS)], axis=1)
    g1, gm, g2, gp, gf = (w[k].reshape(1, d) for k in NORMS)

    def ffn_fwd(tag, h, g, w_in, w_out_name, riders):
        n = rms_fwd(tag + "_norm", h, g, tt)
        if riders:
            (a, s), bufs = ffn_in_act(tag + "_in", n, w_in, tm, comm=gather_of(riders))
            keep(riders, bufs)
        else:
            a, s = ffn_in_act(tag + "_in", n, w_in, tm)
        return n, a, s, mm_nn(tag + "_out", s, full[w_out_name], F32, tm, res=h, alpha=0.5)

    n1, a1, s1, h1 = ffn_fwd("ffn1", x, g1, full["ffn1_w_in"], "ffn1_w_out", ["ffn1_w_out", "w_mix_in"])
    u = rms_fwd("mix_norm", h1, gm, tt)
    wmix = full["w_mix_in"]
    riders = [["w_conv_out", "w_attn_out", "w_mix_out"], ["ffn2_w_in"], ["ffn2_w_out", "w_ple_gate", "w_ple_proj"]]
    cbx, bufs = mm_nn_stacked("mix_in_conv", u, wmix, F32, tm2, d, 0, 3, comm=gather_of(riders[0]))
    keep(riders[0], bufs)
    qkv, bufs = mm_nn_stacked("mix_in_qkv", u, wmix, BF16, tm2, d, 3, 3, comm=gather_of(riders[1]))
    keep(riders[1], bufs)
    gates, bufs = mm_nn_stacked("mix_in_gates", u, wmix, F32, tm2, d, 6, 2, comm=gather_of(riders[2]))
    keep(riders[2], bufs)
    wpp = full["w_ple_proj"]
    wpp = jnp.transpose(wpp, (1, 0, 2)).reshape(wpp.shape[1], -1)
    ycin = conv_fwd("conv", cbx, cw8, tt)
    y_conv = mm_nn("conv_out", ycin, full["w_conv_out"], F32, tm)
    o = attn_fwd("attn", qkv, tq)
    y_attn = mm_nn("attn_out", o, full["w_attn_out"], F32, tm)
    merged = gate_fwd("merge", gates, y_conv, y_attn, tt)
    h2 = mm_nn("mix_out", merged, full["w_mix_out"], F32, tm, res=h1, alpha=1.0)
    n2, a2, s2, h3 = ffn_fwd("ffn2", h2, g2, full["ffn2_w_in"], "ffn2_w_out", [])
    npl = rms_fwd("ple_norm", h3, gp, tt)
    zg = mm_nn("ple_gate", npl, full["w_ple_gate"], F32, tm)
    pp = mm_nn("ple_proj", p, wpp, F32, tm)

    pieces, chip_sums, halves = {}, {}, {}

    def as_pieces(k):
        pc = pieces[k]
        return pc if k in COL_SHARDED else pc.reshape(N_CHIPS, 2, pc.shape[0] // (2 * N_CHIPS), pc.shape[1])

    def sum_siblings(tag, names):
        pcs = [as_pieces(k) for k in names]
        got = run_comm("exchange_" + tag, exchange_comm(pcs))
        for k, a, b in zip(names, pcs, got):
            chip_sums[k] = sum_cores("sum_cores_" + k, a, b, place)

    def scatter_of(names):
        return scatter_comm([chip_sums[k] for k in names])

    def sum_landed(names, landed):
        for k, b in zip(names, landed):
            halves[k] = sum_chips("sum_chips_" + k, chip_sums[k], b, place)

    dh4, dpp, dzg, dgf, loss_row = tail("tail", h3, zg, pp, tgt, gf, tt)
    dwpp = mm_tn_whole("ple_proj_dw", p, dpp, tm2)
    pieces["w_ple_proj"] = jnp.transpose(dwpp.reshape(2, p.shape[1] // 2, N_CHIPS, d // N_CHIPS), (2, 0, 1, 3))
    pieces["w_ple_gate"] = mm_tn_rows("ple_gate_dw", npl, dzg, tm2)
    dnp = mm_nt("ple_gate_dx", dzg, full["w_ple_gate"], F32, tm, d)
    dh3, df2, dgp = rms_bwd("ple_norm_bwd", h3, gp, dnp, dh4, 0.5, tt)
    w_in, w_out = full["ffn2_w_in"], full["ffn2_w_out"]
    pieces["ffn2_w_out"] = mm_tn_rows("ffn2_dwout", s2, df2, tm2)
    da2 = ffn_ds_dact("ffn2_ds", df2, w_out, a2, tm)
    pieces["ffn2_w_in"] = mm_tn_cols("ffn2_dwin", n2, da2, tm2)
    dn2 = mm_nt_stacked("ffn2_dn", da2, w_in, F32, tm2, w_in.shape[2])
    dh2, dh2b, dg2 = rms_bwd("ffn2_norm_bwd", h2, g2, dn2, dh3, 1.0, tt)
    pieces["w_mix_out"] = mm_tn_rows("mix_out_dw", merged, dh2b, tm2)
    dmerged = mm_nt("mix_out_dx", dh2b, full["w_mix_out"], F32, tm, d)
    dyc, dya, dgates = gate_bwd("merge_bwd", dmerged, gates, y_conv, y_attn, tt)
    pieces["w_conv_out"] = mm_tn_rows("conv_out_dw", ycin, dyc, tm2)
    dycin = mm_nt("conv_out_dx", dyc, full["w_conv_out"], F32, tm, d)
    dcbx, dcw8 = conv_bwd("conv_bwd", dycin, cbx, cw8, tt)
    pieces["w_attn_out"] = mm_tn_rows("attn_out_dw", o, dya, tm2)
    do = mm_nt("attn_out_dx", dya, full["w_attn_out"], BF16, tm, d)
    dq, dk, dv = attn_bwd("attn_bwd", qkv, do, tq)
    dmix = jnp.concatenate([dcbx, dq, dk, dv, dgates], axis=1)
    pieces["w_mix_in"] = mm_tn_cols("mix_in_dw", u, dmix, tm2)
    early = ["ffn2_w_in", "ffn2_w_out", "w_ple_gate", "w_ple_proj", "w_mix_out", "w_conv_out", "w_attn_out"]
    sum_siblings("early", early + ["w_mix_in"])
    du, landed = mm_nt_stacked("mix_in_dx", dmix, wmix, F32, tm2, d, comm=scatter_of(early))
    sum_landed(early, landed)
    dh1, df1, dgm = rms_bwd("mix_norm_bwd", h1, gm, du, dh2, 0.5, tt)
    w_in, w_out = full["ffn1_w_in"], full["ffn1_w_out"]
    pieces["ffn1_w_out"] = mm_tn_rows("ffn1_dwout", s1, df1, tm2)
    da1 = ffn_ds_dact("ffn1_ds", df1, w_out, a1, tm)
    pieces["ffn1_w_in"], landed = mm_tn_cols("ffn1_dwin", n1, da1, tm2, comm=scatter_of(["w_mix_in"]))
    sum_landed(["w_mix_in"], landed)
    late = ["ffn1_w_in", "ffn1_w_out"]
    sum_siblings("late", late)
    dn1, landed = mm_nt_stacked("ffn1_dn", da1, w_in, F32, tm2, w_in.shape[2], comm=scatter_of(late))
    sum_landed(late, landed)
    dx, _, dg1 = rms_bwd("ffn1_norm_bwd", x, g1, dn1, dh1, 1.0, tt)

    shared = run_comm("share_halves", share_comm([halves[k] for k in MATS]))
    grad, delta, new_m, new_v = {}, {}, {}, {}
    for k, sh in zip(MATS, shared):
        grad[k] = sh.reshape(w[k].shape)
        delta[k], new_m[k], new_v[k] = adamw("adamw_" + k, w[k], grad[k], m[k], v[k])

    small = jnp.concatenate([dg1, dgm, dg2, dgp, dgf, dcw8[:3], loss_row, jnp.zeros((7, d), F32)], axis=0)
    tot = gather_small("sum_small", small, True)
    loss = tot[8, 0]
    norm_w = jnp.concatenate([w[k].reshape(1, d) for k in NORMS] + [jnp.zeros((3, d), F32)], axis=0)
    norm_m = jnp.concatenate([m[k].reshape(1, d) for k in NORMS] + [jnp.zeros((3, d), F32)], axis=0)
    norm_v = jnp.concatenate([v[k].reshape(1, d) for k in NORMS] + [jnp.ones((3, d), F32)], axis=0)
    norm_g = jnp.concatenate([tot[0:5], jnp.zeros((3, d), F32)], axis=0)
    nd, nm, nv = adamw("adamw_norms", norm_w, norm_g, norm_m, norm_v)
    for r, k in enumerate(NORMS):
        grad[k] = norm_g[r].reshape(w[k].shape)
        delta[k], new_m[k], new_v[k] = (a[r].reshape(w[k].shape) for a in (nd, nm, nv))
    cs = d // N_CHIPS
    gcw = lax.dynamic_slice(tot[5:8], (0, chip * cs), (3, cs))
    cd, cm, cv = adamw("adamw_conv_w", _pad_rows(w["conv_w"], 8), _pad_rows(gcw, 8), _pad_rows(m["conv_w"], 8),
                       jnp.concatenate([v["conv_w"], jnp.ones((5, cs), F32)], axis=0))
    grad["conv_w"], delta["conv_w"], new_m["conv_w"], new_v["conv_w"] = gcw, cd[:3], cm[:3], cv[:3]
    return loss, dx, grad, delta, new_m, new_v


def kernel(x, p, ffn1_norm, ffn1_w_in, ffn1_w_out, mix_norm, w_mix_in, conv_w, w_conv_out, w_attn_out, w_mix_out, ffn2_norm, ffn2_w_in, ffn2_w_out, ple_norm, w_ple_gate, w_ple_proj, final_norm, loss_target, m_ffn1_norm, m_ffn1_w_in, m_ffn1_w_out, m_mix_norm, m_w_mix_in, m_conv_w, m_w_conv_out, m_w_attn_out, m_w_mix_out, m_ffn2_norm, m_ffn2_w_in, m_ffn2_w_out, m_ple_norm, m_w_ple_gate, m_w_ple_proj, m_final_norm, v_ffn1_norm, v_ffn1_w_in, v_ffn1_w_out, v_mix_norm, v_w_mix_in, v_conv_w, v_w_conv_out, v_w_attn_out, v_w_mix_out, v_ffn2_norm, v_ffn2_w_in, v_ffn2_w_out, v_ple_norm, v_w_ple_gate, v_w_ple_proj, v_final_norm):
    ws = (ffn1_norm, ffn1_w_in, ffn1_w_out, mix_norm, w_mix_in, conv_w, w_conv_out, w_attn_out, w_mix_out, ffn2_norm,
          ffn2_w_in, ffn2_w_out, ple_norm, w_ple_gate, w_ple_proj, final_norm)
    ms = (m_ffn1_norm, m_ffn1_w_in, m_ffn1_w_out, m_mix_norm, m_w_mix_in, m_conv_w, m_w_conv_out, m_w_attn_out,
          m_w_mix_out, m_ffn2_norm, m_ffn2_w_in, m_ffn2_w_out, m_ple_norm, m_w_ple_gate, m_w_ple_proj, m_final_norm)
    vs = (v_ffn1_norm, v_ffn1_w_in, v_ffn1_w_out, v_mix_norm, v_w_mix_in, v_conv_w, v_w_conv_out, v_w_attn_out,
          v_w_mix_out, v_ffn2_norm, v_ffn2_w_in, v_ffn2_w_out, v_ple_norm, v_w_ple_gate, v_w_ple_proj, v_final_norm)
    assert x.shape[0] == 1 and p.shape[:2] == (1, 1), "one sequence and one layer per device"

    def strip(a):
        return a[0] if a.ndim == 3 or (a.ndim == 2 and a.shape[0] == 1) else a

    w = {k: strip(a) for k, a in zip(WEIGHTS, ws)}
    m = {k: strip(a) for k, a in zip(WEIGHTS, ms)}
    v = {k: strip(a) for k, a in zip(WEIGHTS, vs)}
    loss, dx, grad, delta, new_m, new_v = _step(x[0], p[0, 0], loss_target[0], w, m, v)
    shapes = [a.shape for a in ws]
    outs = [loss, dx[None]]
    for res in (grad, delta, new_m, new_v):
        outs += [res[k].reshape(s) for k, s in zip(WEIGHTS, shapes)]
    return tuple(outs)
```

```python
import functools
import math

import jax
import jax.numpy as jnp
from jax import lax
from jax.experimental import pallas as pl
from jax.experimental.pallas import tpu as pltpu

F32 = jnp.float32
BF16 = jnp.bfloat16
MESH = pl.DeviceIdType.MESH
ANY = pl.BlockSpec(memory_space=pl.ANY)

HEAD_DIM = 128
NORM_EPS = 1e-6
N_CHIPS = 4
N_DEV = 8
BF16_ROWS = 16
VMEM_LIMIT = 56 * 1024 * 1024
ACC_BYTES = 8 * 1024 * 1024
STICK_EXIT = 110.0

ADAM_LR = 0.001
ADAM_B1 = 0.9
ADAM_B2 = 0.999
ADAM_EPS = 1e-08
ADAM_WD = 0.01
ADAM_STEP = 10

NN = (((1,), (0,)), ((), ()))
NT = (((1,), (1,)), ((), ()))
TN = (((0,), (0,)), ((), ()))


def _params(sem=None, **kw):
    if sem is not None:
        kw["dimension_semantics"] = sem
    return pltpu.CompilerParams(vmem_limit_bytes=VMEM_LIMIT, **kw)


def _pcall(body, **kw):
    return pl.pallas_call(body, **kw)


def _tile(n, pref, mult=8):
    best = None
    for d in range(mult, min(n, pref) + 1, mult):
        if n % d == 0:
            best = d
    return best if best is not None else n


def _dot(a, b, dims):
    return lax.dot_general(a, b, dims, preferred_element_type=F32)


def _call(name, body, grid, in_specs, out_specs, out_shape, args, scratch=(), sem=None, comm=None):
    n_in, n_out, n_sc = len(in_specs), len(out_specs), len(scratch)
    if comm is None:
        def plain(*refs):
            body(refs[:n_in], refs[n_in:n_in + n_out], refs[n_in + n_out:])

        return _pcall(plain, name=name, out_shape=list(out_shape), grid=grid, in_specs=list(in_specs),
                      out_specs=list(out_specs), scratch_shapes=list(scratch), compiler_params=_params(sem))(*args)
    n_cin, n_cout = len(comm.ins), len(comm.outs)
    steps = math.prod(grid)

    def hosted(*refs):
        ins, c_ins = refs[:n_in], refs[n_in:n_in + n_cin]
        outs = refs[n_in + n_cin:n_in + n_cin + n_out]
        c_outs = refs[n_in + n_cin + n_out:n_in + n_cin + n_out + n_cout]
        rest = refs[n_in + n_cin + n_out + n_cout:]
        sems = rest[n_sc:]
        step = pl.program_id(0)
        for ax in range(1, len(grid)):
            step = step * grid[ax] + pl.program_id(ax)

        @pl.when(step == 0)
        def _():
            comm.first(c_ins, c_outs, sems)

        body(ins, outs, rest[:n_sc])

        @pl.when(step == (3 * steps) // 4)
        def _():
            comm.mid(c_ins, c_outs, sems)

        @pl.when(step == steps - 1)
        def _():
            comm.last(c_ins, c_outs, sems)

    res = _pcall(hosted, name=name, out_shape=list(out_shape) + comm.outs, grid=grid,
                 in_specs=list(in_specs) + [ANY] * n_cin, out_specs=list(out_specs) + [ANY] * n_cout,
                 input_output_aliases={n_in + k: n_out + v for k, v in comm.aliases.items()},
                 scratch_shapes=list(scratch) + comm.sems,
                 compiler_params=_params(("arbitrary",) * len(grid)))(*args, *comm.ins)
    return list(res[:n_out]), list(res[n_out:])


def _mm(name, a, b, out_sds, grid, a_spec, b_spec, o_spec, dims, acc_shape, res=None, alpha=1.0, comm=None):
    nk = grid[2]

    def body(ins, outs, scratch):
        a_ref, b_ref = ins[:2]
        r_ref = ins[2] if res is not None else None
        o_ref = outs[0]

        def finish(r):
            if alpha != 1.0:
                r = r * alpha
            if r_ref is not None:
                r = r_ref[...] + r
            if len(o_ref.shape) == 3:
                half = o_ref.shape[1]
                o_ref[0] = r[:half].astype(o_ref.dtype)
                o_ref[1] = r[half:].astype(o_ref.dtype)
            else:
                o_ref[...] = r.astype(o_ref.dtype)

        if nk == 1:
            finish(_dot(a_ref[...].astype(BF16), b_ref[...].astype(BF16), dims))
        else:
            acc_ref = scratch[0]
            kk = pl.program_id(2)

            @pl.when(kk == 0)
            def _():
                acc_ref[...] = jnp.zeros_like(acc_ref)

            acc_ref[...] += _dot(a_ref[...].astype(BF16), b_ref[...].astype(BF16), dims)

            @pl.when(kk == nk - 1)
            def _():
                finish(acc_ref[...])

    in_specs = [a_spec, b_spec]
    args = [a, b]
    if res is not None:
        in_specs.append(o_spec)
        args.append(res)
    scratch = [] if nk == 1 else [pltpu.VMEM(acc_shape, F32)]
    got = _call(name, body, grid, in_specs, [o_spec], [out_sds], args, scratch,
                ("parallel", "parallel", "arbitrary"), comm)
    return got[0] if comm is None else (got[0][0], got[1])


def ffn_in_act(name, n, w4, tm, comm=None):
    t, d = n.shape
    cs = w4.shape[2]

    def body(ins, outs, scratch):
        n_ref, wg_ref, wu_ref = ins
        a_ref, s_ref = outs
        nv = n_ref[...]
        gate = _dot(nv, wg_ref[...], NN)
        up = _dot(nv, wu_ref[...], NN)
        a_ref[0] = gate.astype(BF16)
        a_ref[1] = up.astype(BF16)
        s_ref[...] = (gate * jax.nn.sigmoid(gate) * up).astype(BF16)

    got = _call(name, body, (t // tm, 2),
                [pl.BlockSpec((tm, d), lambda i, j: (i, 0)),
                 pl.BlockSpec((None, d, cs), lambda i, j: (j, 0, 0)),
                 pl.BlockSpec((None, d, cs), lambda i, j: (2 + j, 0, 0))],
                [pl.BlockSpec((2, tm, cs), lambda i, j: (0, i, j)), pl.BlockSpec((tm, cs), lambda i, j: (i, j))],
                [jax.ShapeDtypeStruct((2, t, 2 * cs), BF16), jax.ShapeDtypeStruct((t, 2 * cs), BF16)],
                [n, w4, w4], (), ("parallel", "parallel"), comm)
    return got if comm is None else (got[0], got[1])


def ffn_ds_dact(name, df, w_out, a3, tm):
    t, d = df.shape
    f = w_out.shape[0]
    cs = f // 2

    def body(ins, outs, scratch):
        df_ref, w_ref, a_ref = ins
        ds = _dot(df_ref[...], w_ref[...], NT)
        gate = a_ref[0].astype(F32)
        up = a_ref[1].astype(F32)
        sg = jax.nn.sigmoid(gate)
        outs[0][0] = (ds * up * sg * (1.0 + gate * (1.0 - sg))).astype(BF16)
        outs[0][1] = (ds * gate * sg).astype(BF16)

    blk = pl.BlockSpec((2, tm, cs), lambda i, j: (0, i, j))
    return _call(name, body, (t // tm, 2),
                 [pl.BlockSpec((tm, d), lambda i, j: (i, 0)), pl.BlockSpec((cs, d), lambda i, j: (j, 0)), blk],
                 [blk], [jax.ShapeDtypeStruct((2, t, f), BF16)], [df, w_out, a3], (), ("parallel", "parallel"))[0]


def _part_ranges(parts, d):
    out, lo = [], 0
    for p in parts:
        out.append((lo, p.shape[1] // d))
        lo += p.shape[1] // d
    return out, lo


def mm_nt_parts(name, parts, w4, tm, comm=None):
    m = parts[0].shape[0]
    d, cs = w4.shape[1], w4.shape[2]
    per = cs // d
    ranges, nblk = _part_ranges(parts, d)

    def body(ins, outs, scratch):
        w_ref, acc = ins[-1], scratch[0]
        r = pl.program_id(1)

        @pl.when(r == 0)
        def _():
            acc[...] = jnp.zeros_like(acc)

        for (lo, n), a_ref in zip(ranges, ins[:-1]):
            @pl.when(jnp.logical_and(r >= lo, r < lo + n))
            def _(a_ref=a_ref):
                acc[...] += _dot(a_ref[...], w_ref[...], NT)

        @pl.when(r == nblk - 1)
        def _():
            outs[0][...] = acc[...]

    specs = [pl.BlockSpec((tm, d), lambda i, r, lo=lo, n=n: (i, jnp.clip(r - lo, 0, n - 1))) for lo, n in ranges]
    specs.append(pl.BlockSpec((None, d, d), lambda i, r: (r // per, 0, r % per)))
    got = _call(name, body, (m // tm, nblk), specs, [pl.BlockSpec((tm, d), lambda i, r: (i, 0))],
                [jax.ShapeDtypeStruct((m, d), F32)], list(parts) + [w4], [pltpu.VMEM((tm, d), F32)],
                ("parallel", "arbitrary"), comm)
    return got[0] if comm is None else (got[0][0], got[1])


def mm_tn_parts(name, xa, parts, tt, comm=None):
    t, k = xa.shape
    d = k
    pr = k // 2
    ranges, nblk = _part_ranges(parts, d)
    per = nblk // N_CHIPS

    def body(ins, outs, scratch):
        x_ref, acc = ins[0], scratch[0]
        jb, r = pl.program_id(0), pl.program_id(1)

        @pl.when(r == 0)
        def _():
            acc[...] = jnp.zeros_like(acc)

        for (lo, n), p_ref in zip(ranges, ins[1:]):
            @pl.when(jnp.logical_and(jb >= lo, jb < lo + n))
            def _(p_ref=p_ref):
                acc[...] += _dot(x_ref[...], p_ref[...], TN)

        @pl.when(r == t // tt - 1)
        def _():
            outs[0][0] = acc[:pr].astype(BF16)
            outs[0][1] = acc[pr:].astype(BF16)

    def part_spec(lo, n):
        return pl.BlockSpec((tt, d), lambda jb, r: (jnp.where(jnp.logical_and(jb >= lo, jb < lo + n), r, 0),
                                                    jnp.clip(jb - lo, 0, n - 1)))

    specs = [pl.BlockSpec((tt, k), lambda jb, r: (r, 0))] + [part_spec(lo, n) for lo, n in ranges]
    got = _call(name, body, (nblk, t // tt), specs,
                [pl.BlockSpec((None, 2, pr, d), lambda jb, r: (jb // per, 0, 0, jb % per))],
                [jax.ShapeDtypeStruct((N_CHIPS, 2, pr, per * d), BF16)], [xa] + list(parts),
                [pltpu.VMEM((k, d), F32)], ("parallel", "arbitrary"), comm)
    return got[0] if comm is None else (got[0][0], got[1])


def mm_nn(name, a, w, out_dtype, tm, res=None, alpha=1.0):
    m, k = a.shape
    n = w.shape[1]
    return _mm(name, a, w, jax.ShapeDtypeStruct((m, n), out_dtype), (m // tm, 1, 1),
               pl.BlockSpec((tm, k), lambda i, j, r: (i, 0)),
               pl.BlockSpec((k, n), lambda i, j, r: (0, 0)),
               pl.BlockSpec((tm, n), lambda i, j, r: (i, 0)), NN, None, res=res, alpha=alpha)


def mm_nn_stacked(name, a, w4, out_dtype, tm, tn, j0=0, nj=None, comm=None):
    m, k = a.shape
    cs = w4.shape[2]
    per = cs // tn
    nj = N_CHIPS * per - j0 if nj is None else nj
    return _mm(name, a, w4, jax.ShapeDtypeStruct((m, nj * tn), out_dtype), (m // tm, nj, 1),
               pl.BlockSpec((tm, k), lambda i, j, r: (i, 0)),
               pl.BlockSpec((None, k, tn), lambda i, j, r: ((j + j0) // per, 0, (j + j0) % per)),
               pl.BlockSpec((tm, tn), lambda i, j, r: (i, j)), NN, None, comm=comm)


def mm_nt(name, dy, w, out_dtype, tm, tko):
    m, n = dy.shape
    k = w.shape[0]
    return _mm(name, dy, w, jax.ShapeDtypeStruct((m, k), out_dtype), (m // tm, k // tko, 1),
               pl.BlockSpec((tm, n), lambda i, j, r: (i, 0)),
               pl.BlockSpec((tko, n), lambda i, j, r: (j, 0)),
               pl.BlockSpec((tm, tko), lambda i, j, r: (i, j)), NT, None)


def mm_nt_stacked(name, dy, w4, out_dtype, tm, tn, comm=None):
    m = dy.shape[-2]
    k, cs = w4.shape[1], w4.shape[2]
    per = cs // tn
    if dy.ndim == 3:
        dy_spec = pl.BlockSpec((None, tm, cs), lambda i, j, r: (r // 2, i, r % 2))
    else:
        dy_spec = pl.BlockSpec((tm, tn), lambda i, j, r: (i, r))
    return _mm(name, dy, w4, jax.ShapeDtypeStruct((m, k), out_dtype), (m // tm, 1, N_CHIPS * per), dy_spec,
               pl.BlockSpec((None, k, tn), lambda i, j, r: (r // per, 0, r % per)),
               pl.BlockSpec((tm, k), lambda i, j, r: (i, 0)), NT, (tm, k), comm=comm)


def mm_tn_rows(name, xa, dy, tt):
    t, k = xa.shape
    n = dy.shape[1]
    tkr = k if k * n * 4 <= ACC_BYTES else k // 2
    return _mm(name, xa, dy, jax.ShapeDtypeStruct((k, n), BF16), (k // tkr, 1, t // tt),
               pl.BlockSpec((tt, tkr), lambda i, j, r: (r, i)),
               pl.BlockSpec((tt, n), lambda i, j, r: (r, 0)),
               pl.BlockSpec((tkr, n), lambda i, j, r: (i, 0)), TN, (tkr, n))


def mm_tn_whole(name, xa, dy, tt):
    t, k = xa.shape
    n = dy.shape[1]
    return _mm(name, xa, dy, jax.ShapeDtypeStruct((k, n), BF16), (1, 1, t // tt),
               pl.BlockSpec((tt, k), lambda i, j, r: (r, 0)),
               pl.BlockSpec((tt, n), lambda i, j, r: (r, 0)),
               pl.BlockSpec((k, n), lambda i, j, r: (0, 0)), TN, (k, n))


def mm_tn_cols(name, xa, dy, tt, comm=None):
    t, k = xa.shape
    pr = k // 2
    if dy.ndim == 3:
        cs = dy.shape[2] // 2
        dy_spec = pl.BlockSpec((None, tt, cs), lambda i, j, r: (j // 2, r, j % 2))
    else:
        cs = dy.shape[1] // N_CHIPS
        dy_spec = pl.BlockSpec((tt, cs), lambda i, j, r: (r, j))
    return _mm(name, xa, dy, jax.ShapeDtypeStruct((N_CHIPS, 2, pr, cs), BF16), (1, N_CHIPS, t // tt),
               pl.BlockSpec((tt, k), lambda i, j, r: (r, 0)), dy_spec,
               pl.BlockSpec((None, 2, pr, cs), lambda i, j, r: (j, 0, 0, 0)), TN, (k, cs), comm=comm)


def _rows(tt, w, col=0):
    return pl.BlockSpec((tt, w), lambda i: (i, col))


def _whole(shape):
    return pl.BlockSpec(shape, lambda i: (0,) * len(shape))


def _rstd(h):
    return lax.rsqrt(jnp.mean(h * h, axis=-1, keepdims=True) + NORM_EPS)


def rms_fwd(name, h, g, tt, comm=None):
    t, d = h.shape

    def body(ins, outs, scratch):
        hv = ins[0][...]
        outs[0][...] = (hv * _rstd(hv) * ins[1][...]).astype(BF16)

    got = _call(name, body, (t // tt,), [_rows(tt, d), _whole((1, d))], [_rows(tt, d)],
                [jax.ShapeDtypeStruct((t, d), BF16)], [h, g], (), ("parallel",), comm)
    return got[0] if comm is None else (got[0][0], got[1])


def rms_bwd(name, h, g, dn, dres, alpha, tt):
    t, d = h.shape

    def body(h_ref, g_ref, dn_ref, dr_ref, dh_ref, dhb_ref, dg_ref):
        hv = h_ref[...]
        hn = hv * _rstd(hv)
        dnv = dn_ref[...]
        gy = dnv * g_ref[...]
        dh = dr_ref[...] + _rstd(hv) * (gy - hn * jnp.mean(gy * hn, axis=-1, keepdims=True))
        dh_ref[...] = dh
        dhb_ref[...] = (alpha * dh).astype(BF16)

        @pl.when(pl.program_id(0) == 0)
        def _():
            dg_ref[...] = jnp.zeros_like(dg_ref)

        dg_ref[...] += jnp.sum(dnv * hn, axis=0, keepdims=True)

    return _pcall(body, name=name,
                  out_shape=(jax.ShapeDtypeStruct((t, d), F32), jax.ShapeDtypeStruct((t, d), BF16),
                             jax.ShapeDtypeStruct((1, d), F32)),
                  grid=(t // tt,),
                  in_specs=[_rows(tt, d), _whole((1, d)), _rows(tt, d), _rows(tt, d)],
                  out_specs=(_rows(tt, d), _rows(tt, d), _whole((1, d))),
                  compiler_params=_params(("arbitrary",)))(h, g, dn, dres)


def gate_fwd(name, gates, yc, ya, tt):
    t, d = yc.shape

    def body(g_ref, yc_ref, ya_ref, o_ref):
        o_ref[...] = (jax.nn.sigmoid(g_ref[:, :d]) * yc_ref[...]
                      + jax.nn.sigmoid(g_ref[:, d:]) * ya_ref[...]).astype(o_ref.dtype)

    return _pcall(body, name=name, out_shape=jax.ShapeDtypeStruct((t, d), BF16), grid=(t // tt,),
                  in_specs=[_rows(tt, 2 * d), _rows(tt, d), _rows(tt, d)], out_specs=_rows(tt, d),
                  compiler_params=_params(("parallel",)))(gates, yc, ya)


def gate_bwd(name, dm, gates, yc, ya, tt):
    t, d = yc.shape

    def body(dm_ref, g_ref, yc_ref, ya_ref, dyc_ref, dya_ref, dg_ref):
        dmv = dm_ref[...]
        sc = jax.nn.sigmoid(g_ref[:, :d])
        sa = jax.nn.sigmoid(g_ref[:, d:])
        dyc_ref[...] = (dmv * sc).astype(BF16)
        dya_ref[...] = (dmv * sa).astype(BF16)
        dg_ref[:, :d] = (dmv * yc_ref[...] * sc * (1.0 - sc)).astype(BF16)
        dg_ref[:, d:] = (dmv * ya_ref[...] * sa * (1.0 - sa)).astype(BF16)

    return _pcall(body, name=name,
                  out_shape=(jax.ShapeDtypeStruct((t, d), BF16), jax.ShapeDtypeStruct((t, d), BF16),
                             jax.ShapeDtypeStruct((t, 2 * d), BF16)),
                  grid=(t // tt,),
                  in_specs=[_rows(tt, d), _rows(tt, 2 * d), _rows(tt, d), _rows(tt, d)],
                  out_specs=(_rows(tt, d), _rows(tt, d), _rows(tt, 2 * d)),
                  compiler_params=_params(("parallel",)))(dm, gates, yc, ya)


def _shift_down(cur, prev8, s):
    tt = cur.shape[0]
    rolled = pltpu.roll(cur, s, 0)
    row8 = lax.broadcasted_iota(jnp.int32, prev8.shape, 0)
    first8 = jnp.where(row8 < s, pltpu.roll(prev8, s, 0), rolled[:8])
    return jnp.concatenate([first8, rolled[8:]], axis=0) if tt > 8 else first8


def _shift_up(cur, next8, s):
    tt = cur.shape[0]
    rolled = pltpu.roll(cur, tt - s, 0)
    row8 = lax.broadcasted_iota(jnp.int32, next8.shape, 0)
    last8 = jnp.where(row8 >= 8 - s, pltpu.roll(next8, 8 - s, 0), rolled[tt - 8:])
    return jnp.concatenate([rolled[:tt - 8], last8], axis=0) if tt > 8 else last8


def _prev8(tt, d, col):
    return pl.BlockSpec((8, d), lambda i: (jnp.maximum(i * (tt // 8) - 1, 0), col))


def _next8(tt, d, col, t):
    return pl.BlockSpec((8, d), lambda i: (jnp.minimum((i + 1) * (tt // 8), t // 8 - 1), col))


def conv_fwd(name, cbx, cw8, tt):
    t, d3 = cbx.shape
    d = d3 // 3

    def body(cb_ref, cc_ref, cx_ref, pc_ref, px_ref, w_ref, o_ref):
        has_prev = (pl.program_id(0) > 0).astype(F32)
        cc = cc_ref[...] * cx_ref[...]
        prev = pc_ref[...] * px_ref[...] * has_prev
        w = w_ref[...]
        conv = w[0:1] * _shift_down(cc, prev, 2) + w[1:2] * _shift_down(cc, prev, 1) + w[2:3] * cc
        o_ref[...] = (cb_ref[...] * conv).astype(o_ref.dtype)

    return _pcall(body, name=name, out_shape=jax.ShapeDtypeStruct((t, d), BF16), grid=(t // tt,),
                  in_specs=[_rows(tt, d, 0), _rows(tt, d, 1), _rows(tt, d, 2), _prev8(tt, d, 1), _prev8(tt, d, 2),
                            _whole((8, d))],
                  out_specs=_rows(tt, d), compiler_params=_params(("parallel",)))(cbx, cbx, cbx, cbx, cbx, cw8)


def conv_bwd(name, dyc, cbx, cw8, tt):
    t, d3 = cbx.shape
    d = d3 // 3
    n = t // tt

    def body(dy_ref, cb_ref, cc_ref, cx_ref, pc_ref, px_ref, ndy_ref, ncb_ref, w_ref, o_ref, dw_ref):
        i = pl.program_id(0)
        has_prev = (i > 0).astype(F32)
        has_next = (i < n - 1).astype(F32)
        cb = cb_ref[...]
        cc = cc_ref[...] * cx_ref[...]
        prev = pc_ref[...] * px_ref[...] * has_prev
        w = w_ref[...]
        cc1 = _shift_down(cc, prev, 1)
        cc2 = _shift_down(cc, prev, 2)
        conv = w[0:1] * cc2 + w[1:2] * cc1 + w[2:3] * cc
        dyv = dy_ref[...]
        dconv = dyv * cb
        dnext = ndy_ref[...] * ncb_ref[...] * has_next
        dcc = w[2:3] * dconv + w[1:2] * _shift_up(dconv, dnext, 1) + w[0:1] * _shift_up(dconv, dnext, 2)
        o_ref[:, :d] = (dyv * conv).astype(BF16)
        o_ref[:, d:2 * d] = (dcc * cx_ref[...]).astype(BF16)
        o_ref[:, 2 * d:] = (dcc * cc_ref[...]).astype(BF16)

        @pl.when(i == 0)
        def _():
            dw_ref[...] = jnp.zeros_like(dw_ref)

        dw_ref[0:1, :] += jnp.sum(dconv * cc2, axis=0, keepdims=True)
        dw_ref[1:2, :] += jnp.sum(dconv * cc1, axis=0, keepdims=True)
        dw_ref[2:3, :] += jnp.sum(dconv * cc, axis=0, keepdims=True)

    return _pcall(body, name=name,
                  out_shape=(jax.ShapeDtypeStruct((t, d3), BF16), jax.ShapeDtypeStruct((8, d), F32)),
                  grid=(n,),
                  in_specs=[_rows(tt, d), _rows(tt, d, 0), _rows(tt, d, 1), _rows(tt, d, 2),
                            _prev8(tt, d, 1), _prev8(tt, d, 2), _next8(tt, d, 0, t), _next8(tt, d, 0, t),
                            _whole((8, d))],
                  out_specs=(_rows(tt, d3), _whole((8, d))),
                  compiler_params=_params(("arbitrary",)))(dyc, cbx, cbx, cbx, cbx, cbx, dyc, cbx, cw8)


def tail(name, h3, zg, pp, tgt, gf, tt):
    t, d = h3.shape

    def body(h_ref, zg_ref, pp_ref, tg_ref, gf_ref, dh_ref, dpp_ref, dzg_ref, dgf_ref, loss_ref):
        pg = jax.nn.sigmoid(zg_ref[...])
        ppv = pp_ref[...]
        h4 = h_ref[...] + pg * ppv
        r4 = _rstd(h4)
        hn = h4 * r4
        gfv = gf_ref[...]
        err = hn * gfv - tg_ref[...]
        dy = err * (1.0 / d)
        gy = dy * gfv
        dh4 = r4 * (gy - hn * jnp.mean(gy * hn, axis=-1, keepdims=True))
        dh_ref[...] = dh4
        dpp_ref[...] = (dh4 * pg).astype(BF16)
        dzg_ref[...] = (dh4 * ppv * pg * (1.0 - pg)).astype(BF16)

        @pl.when(pl.program_id(0) == 0)
        def _():
            dgf_ref[...] = jnp.zeros_like(dgf_ref)
            loss_ref[...] = jnp.zeros_like(loss_ref)

        dgf_ref[...] += jnp.sum(dy * hn, axis=0, keepdims=True)
        tok = jnp.mean(err * err, axis=-1, keepdims=True)
        loss_ref[...] += 0.5 * jnp.sum(tok, axis=0, keepdims=True) * jnp.ones((1, loss_ref.shape[1]), F32)

    return _pcall(body, name=name,
                  out_shape=(jax.ShapeDtypeStruct((t, d), F32), jax.ShapeDtypeStruct((t, d), BF16),
                             jax.ShapeDtypeStruct((t, d), BF16), jax.ShapeDtypeStruct((1, d), F32),
                             jax.ShapeDtypeStruct((1, d), F32)),
                  grid=(t // tt,),
                  in_specs=[_rows(tt, d)] * 4 + [_whole((1, d))],
                  out_specs=(_rows(tt, d), _rows(tt, d), _rows(tt, d), _whole((1, d)), _whole((1, d))),
                  compiler_params=_params(("arbitrary",)))(h3, zg, pp, tgt, gf)


SCALE = 1.0 / math.sqrt(HEAD_DIM)


def _log_stick(z):
    return -(jnp.maximum(z, 0.0) + jnp.log(1.0 + jnp.exp(-jnp.abs(z))))


def _tri_sum(x, tri):
    hi = x.astype(BF16)
    lo = (x - hi.astype(F32)).astype(BF16)
    return _dot(hi, tri, NN) + _dot(lo, tri, NN)


KEY_BLOCK = 128
NEAR = 3


def _sb_near(qs, jds, k_ref, below, upper):
    pairs = [(s, b) for s in range(len(qs)) for b in range(NEAR)]
    rows = {(s, b): _block_rows(jnp.maximum(jds[s] - b, 0), KEY_BLOCK) for s, b in pairs}
    z = {(s, b): _dot(qs[s], k_ref[rows[s, b], :], NT) * SCALE for s, b in pairs}
    lg = {(s, b): jnp.where(below, _log_stick(z[s, b]), 0.0) if b == 0 else _log_stick(z[s, b]) for s, b in pairs}
    cum = {(s, b): _tri_sum(lg[s, b], upper) for s, b in pairs}
    out, carries = [], []
    for s in range(len(qs)):
        c = cum[s, 0][:, 0:1]
        blocks = [(rows[s, 0], z[s, 0], jnp.exp(jnp.where(below, z[s, 0] + cum[s, 0], -1e30)))]
        for b in range(1, NEAR):
            live = jds[s] >= b
            blocks.append((rows[s, b], z[s, b], jnp.exp(z[s, b] + cum[s, b] + (c + jnp.where(live, 0.0, -1e30)))))
            c = c + jnp.where(live, cum[s, b][:, 0:1], 0.0)
        out.append(blocks)
        carries.append(c)
    return out, carries


def _sb_far(q, kj, upper, c):
    z = _dot(q, kj, NT) * SCALE
    cum = _tri_sum(_log_stick(z), upper)
    return z, jnp.exp(z + cum + c), c + cum[:, 0:1]


def _block_rows(j, size):
    return pl.ds(pl.multiple_of(j * size, size), size)


def _sweep_on(st):
    return jnp.logical_and(st[0] >= 0, jnp.max(st[1]) > -STICK_EXIT)


def attn_fwd(name, qkv, tq):
    t, d3 = qkv.shape
    d = d3 // 3
    nh = d // HEAD_DIM
    nq = t // tq
    tb = KEY_BLOCK
    nsub = tq // tb

    def body(q_ref, k_ref, v_ref, o_ref):
        i = pl.program_id(1)
        row = lax.broadcasted_iota(jnp.int32, (tb, tb), 0)
        col = lax.broadcasted_iota(jnp.int32, (tb, tb), 1)
        upper = (row >= col).astype(BF16)
        qs = [q_ref[s * tb:(s + 1) * tb, :] for s in range(nsub)]
        jds = [i * nsub + s for s in range(nsub)]
        near, carries = _sb_near(qs, jds, k_ref, col < row, upper)
        state = []
        for s in range(nsub):
            acc = jnp.zeros((tb, HEAD_DIM), F32)
            for rows, _, a in near[s]:
                acc = acc + _dot(a.astype(BF16), v_ref[rows, :], NN)
            state.append((qs[s], jds[s], carries[s], acc))
        for s, (q, jd, c, acc) in enumerate(state):

            def step(st, q=q):
                rows = _block_rows(st[0], tb)
                _, a, c2 = _sb_far(q, k_ref[rows, :], upper, st[1])
                return st[0] - 1, c2, st[2] + _dot(a.astype(BF16), v_ref[rows, :], NN)

            _, _, acc = lax.while_loop(_sweep_on, step, (jd - NEAR, c, acc))
            o_ref[s * tb:(s + 1) * tb, :] = acc.astype(o_ref.dtype)

    return _pcall(body, name=name, out_shape=jax.ShapeDtypeStruct((t, d), BF16), grid=(nh, nq),
                  in_specs=[pl.BlockSpec((tq, HEAD_DIM), lambda h, i: (i, h)),
                            pl.BlockSpec((t, HEAD_DIM), lambda h, i: (0, nh + h)),
                            pl.BlockSpec((t, HEAD_DIM), lambda h, i: (0, 2 * nh + h))],
                  out_specs=pl.BlockSpec((tq, HEAD_DIM), lambda h, i: (i, h)),
                  compiler_params=_params(("parallel", "arbitrary")))(qkv, qkv, qkv)


def attn_bwd(name, qkv, do, tq):
    t, d3 = qkv.shape
    d = d3 // 3
    nh = d // HEAD_DIM
    nq = t // tq
    tb = KEY_BLOCK
    nsub = tq // tb

    def body(q_ref, k_ref, v_ref, do_ref, dq_ref, dk_ref, dv_ref, dk_acc, dv_acc, g_buf, z_buf):
        i = pl.program_id(1)

        @pl.when(i == 0)
        def _():
            dk_acc[...] = jnp.zeros_like(dk_acc)
            dv_acc[...] = jnp.zeros_like(dv_acc)

        row = lax.broadcasted_iota(jnp.int32, (tb, tb), 0)
        col = lax.broadcasted_iota(jnp.int32, (tb, tb), 1)
        below = col < row
        upper = (row >= col).astype(BF16)
        lower = (row <= col).astype(BF16)

        qs = [q_ref[s * tb:(s + 1) * tb, :] for s in range(nsub)]
        dos = [do_ref[s * tb:(s + 1) * tb, :] for s in range(nsub)]
        jds = [i * nsub + s for s in range(nsub)]
        near, carries = _sb_near(qs, jds, k_ref, below, upper)
        da = [[_dot(dos[s], v_ref[rows, :], NT) for rows, _, _ in near[s]] for s in range(nsub)]
        state = []
        for s in range(nsub):
            kept = [(rows, z, da[s][b] * a) for b, (rows, z, a) in enumerate(near[s])]
            for rows, _, a in near[s]:
                dv_acc[rows, :] += _dot(a.astype(BF16), dos[s], TN)
            state.append((qs[s], dos[s], jds[s], carries[s], kept))

        carried = []
        for s, (q, dov, jd, c, kept) in enumerate(state):
            def step(st, s=s, q=q, dov=dov, jd=jd):
                j = st[0]
                rows = _block_rows(j, tb)
                z, a, c2 = _sb_far(q, k_ref[rows, :], upper, st[1])
                g_buf[s, jd - j] = _dot(dov, v_ref[rows, :], NT) * a
                z_buf[s, jd - j] = z
                dv_acc[rows, :] += _dot(a.astype(BF16), dov, TN)
                return j - 1, c2

            j_stop, _ = lax.while_loop(_sweep_on, step, (jd - NEAR, c))

            def far(j, st, s=s, q=q, jd=jd):
                run, dq = st
                rows = _block_rows(j, tb)
                g = g_buf[s, jd - j]
                dz = (g - jax.nn.sigmoid(z_buf[s, jd - j]) * (run + _tri_sum(g, lower))).astype(BF16)
                dk_acc[rows, :] += _dot(dz, q, TN)
                return run + jnp.sum(g, axis=1, keepdims=True), dq + _dot(dz, k_ref[rows, :], NN)

            carried.append(lax.fori_loop(j_stop + 1, jd - NEAR + 1, far,
                                         (jnp.zeros((tb, 1), F32), jnp.zeros((tb, HEAD_DIM), F32))))

        tri = [[_tri_sum(g, lower) for _, _, g in st[4]] for st in state]
        sig = [[jax.nn.sigmoid(z) for _, z, _ in st[4]] for st in state]
        for s, (q, dov, jd, c, kept) in enumerate(state):
            run, dq = carried[s]
            for b in reversed(range(NEAR)):
                rows, z, g = kept[b]
                dz = g - sig[s][b] * (run + tri[s][b])
                if b == 0:
                    dz = jnp.where(below, dz, 0.0)
                dz = dz.astype(BF16)
                dk_acc[rows, :] += _dot(dz, q, TN)
                dq = dq + _dot(dz, k_ref[rows, :], NN)
                if b:
                    run = run + jnp.sum(g, axis=1, keepdims=True)
            dq_ref[s * tb:(s + 1) * tb, :] = (dq * SCALE).astype(BF16)

        @pl.when(i == nq - 1)
        def _():
            dk_ref[...] = (dk_acc[...] * SCALE).astype(BF16)
            dv_ref[...] = dv_acc[...].astype(BF16)

    blk = pl.BlockSpec((tq, HEAD_DIM), lambda h, i: (i, h))
    col_h = pl.BlockSpec((t, HEAD_DIM), lambda h, i: (0, h))
    out = jax.ShapeDtypeStruct((t, d), BF16)
    return _pcall(body, name=name, out_shape=(out, out, out), grid=(nh, nq),
                  in_specs=[blk,
                            pl.BlockSpec((t, HEAD_DIM), lambda h, i: (0, nh + h)),
                            pl.BlockSpec((t, HEAD_DIM), lambda h, i: (0, 2 * nh + h)),
                            blk],
                  out_specs=(blk, col_h, col_h),
                  scratch_shapes=[pltpu.VMEM((t, HEAD_DIM), F32), pltpu.VMEM((t, HEAD_DIM), F32),
                                  pltpu.VMEM((nsub, t // tb, tb, tb), F32), pltpu.VMEM((nsub, t // tb, tb, tb), F32)],
                  compiler_params=_params(("parallel", "arbitrary")))(qkv, qkv, qkv, do)


def _place():
    x, y, c = lax.axis_index("x"), lax.axis_index("y"), lax.axis_index("c")
    chips = [(1 - x, y), (x, 1 - y), (1 - x, 1 - y)]
    return x, y, c, chips


def _remote(src, dst, send_sem, recv_sem, dev):
    return pltpu.make_async_remote_copy(src_ref=src, dst_ref=dst, send_sem=send_sem, recv_sem=recv_sem,
                                        device_id=dev, device_id_type=MESH)


def place_shard(name, w, chip):
    r, cdim = w.shape
    tr = _tile(r, max(BF16_ROWS, (1 << 19) // cdim), BF16_ROWS)

    def body(chip_ref, w_ref, o_ref):
        o_ref[...] = w_ref[...].astype(BF16)

    spec = pltpu.PrefetchScalarGridSpec(
        num_scalar_prefetch=1, grid=(r // tr,),
        in_specs=[pl.BlockSpec((tr, cdim), lambda i, s: (i, 0))],
        out_specs=pl.BlockSpec((None, tr, cdim), lambda i, s: (s[0], i, 0)))
    return _pcall(body, name=name, out_shape=jax.ShapeDtypeStruct((N_CHIPS, r, cdim), BF16), grid_spec=spec,
                  compiler_params=_params(("parallel",)))(chip, w)


class Comm:
    def __init__(self, ins, outs, aliases, sems, first, mid, last):
        self.ins, self.outs, self.aliases, self.sems = list(ins), list(outs), dict(aliases), list(sems)
        self.first, self.mid, self.last = first, mid, last


def run_comm(name, comm):
    ni, no = len(comm.ins), len(comm.outs)

    def body(*refs):
        ins, outs, sems = refs[:ni], refs[ni:ni + no], refs[ni + no:]
        comm.first(ins, outs, sems)
        comm.mid(ins, outs, sems)
        comm.last(ins, outs, sems)

    return _pcall(body, name=name, out_shape=comm.outs, in_specs=[ANY] * ni, out_specs=[ANY] * no,
                  input_output_aliases=comm.aliases, scratch_shapes=comm.sems, compiler_params=_params())(*comm.ins)


def gather_comm(bufs):
    n = len(bufs)

    def half(out, w, which):
        pr = out[w].shape[1] // 2
        return pl.ds(pl.multiple_of(which * pr, BF16_ROWS), pr)

    def first(ins, out, sems):
        isend, irecv, _, _ = sems
        x, y, c, chips = _place()
        for w in range(n):
            mine = out[w].at[2 * x + y, half(out, w, c)]
            for j, (cx, cy) in enumerate(chips):
                _remote(mine, mine, isend.at[3 * w + j], irecv.at[3 * w + j], (cx, cy, c)).start()

    def mid(ins, out, sems):
        isend, irecv, dsend, drecv = sems
        x, y, c, chips = _place()
        sib = (x, y, 1 - c)
        for w in range(n):
            for j, (cx, cy) in enumerate(chips):
                landed = out[w].at[2 * cx + cy, half(out, w, c)]
                _remote(landed, landed, isend.at[3 * w + j], irecv.at[3 * w + j], sib).wait_recv()
                _remote(landed, landed, dsend.at[3 * w + j], drecv.at[3 * w + j], sib).start()

    def last(ins, out, sems):
        isend, irecv, dsend, drecv = sems
        x, y, c, chips = _place()
        sib = (x, y, 1 - c)
        for w in range(n):
            for j, (cx, cy) in enumerate(chips):
                landed = out[w].at[2 * cx + cy, half(out, w, 1 - c)]
                _remote(landed, landed, dsend.at[3 * w + j], drecv.at[3 * w + j], sib).wait_recv()
        for w in range(n):
            sent = out[w].at[0, half(out, w, c)]
            for j in range(3):
                _remote(sent, sent, isend.at[3 * w + j], irecv.at[3 * w + j], sib).wait_send()
                _remote(sent, sent, dsend.at[3 * w + j], drecv.at[3 * w + j], sib).wait_send()

    return Comm(bufs, [jax.ShapeDtypeStruct(s.shape, s.dtype) for s in bufs], {w: w for w in range(n)},
                [pltpu.SemaphoreType.DMA((3 * n,))] * 4, first, mid, last)


def _nothing(ins, outs, sems):
    return None


def exchange_comm(pieces):
    n = len(pieces)

    def copies(src, out, sems):
        x, y, c, _ = _place()
        return [_remote(src[w].at[k, 1 - c], out[w].at[k], sems[0].at[N_CHIPS * w + k], sems[1].at[N_CHIPS * w + k],
                        (x, y, 1 - c)) for w in range(n) for k in range(N_CHIPS)]

    def first(src, out, sems):
        for cp in copies(src, out, sems):
            cp.start()

    def last(src, out, sems):
        for cp in copies(src, out, sems):
            cp.wait()

    return Comm(pieces, [jax.ShapeDtypeStruct((N_CHIPS,) + s.shape[2:], s.dtype) for s in pieces], {},
                [pltpu.SemaphoreType.DMA((N_CHIPS * n,))] * 2, first, _nothing, last)


def scatter_comm(parts):
    n = len(parts)

    def copies(src, out, sems):
        x, y, c, chips = _place()
        return [_remote(src[w].at[2 * cx + cy], out[w].at[j], sems[0].at[3 * w + j], sems[1].at[3 * w + j], (cx, cy, c))
                for w in range(n) for j, (cx, cy) in enumerate(chips)]

    def first(src, out, sems):
        for cp in copies(src, out, sems):
            cp.start()

    def last(src, out, sems):
        for cp in copies(src, out, sems):
            cp.wait()

    return Comm(parts, [jax.ShapeDtypeStruct((3,) + s.shape[1:], s.dtype) for s in parts], {},
                [pltpu.SemaphoreType.DMA((3 * n,))] * 2, first, _nothing, last)


def share_comm(halves):
    n = len(halves)

    def first(ins, buf, sems):
        x, y, c, _ = _place()
        for w in range(n):
            _remote(buf[w].at[c], buf[w].at[c], sems[0].at[w], sems[1].at[w], (x, y, 1 - c)).start()

    def last(ins, buf, sems):
        x, y, c, _ = _place()
        for w in range(n):
            landed = buf[w].at[1 - c]
            _remote(landed, landed, sems[0].at[w], sems[1].at[w], (x, y, 1 - c)).wait_recv()
        for w in range(n):
            _remote(buf[w].at[c], buf[w].at[c], sems[0].at[w], sems[1].at[w], (x, y, 1 - c)).wait_send()

    return Comm(halves, [jax.ShapeDtypeStruct(s.shape, s.dtype) for s in halves], {w: w for w in range(n)},
                [pltpu.SemaphoreType.DMA((n,))] * 2, first, _nothing, last)


def gather_small(name, blk, reduce):
    r, cdim = blk.shape

    def body(in_ref, out_ref, *rest):
        if reduce:
            buf, send_sem, recv_sem = rest
        else:
            buf = out_ref
            send_sem, recv_sem = rest
        x, y, c, _ = _place()
        me = 4 * x + 2 * y + c
        buf[me] = in_ref[...]
        peers = []
        for dx in range(2):
            for dy in range(2):
                for dc in range(2):
                    if dx or dy or dc:
                        peers.append((dx, dy, dc))
        copies = []
        for s, (dx, dy, dc) in enumerate(peers):
            cp = _remote(in_ref, buf.at[me], send_sem.at[s], recv_sem.at[s],
                         ((1 - x if dx else x), (1 - y if dy else y), (1 - c if dc else c)))
            cp.start()
            copies.append(cp)
        for s, (dx, dy, dc) in enumerate(peers):
            px, py, pc_ = (1 - x if dx else x), (1 - y if dy else y), (1 - c if dc else c)
            landed = buf.at[4 * px + 2 * py + pc_]
            _remote(landed, landed, send_sem.at[s], recv_sem.at[s], (x, y, c)).wait_recv()
        for cp in copies:
            cp.wait_send()
        if reduce:
            tot = buf[0]
            for s in range(1, N_DEV):
                tot = tot + buf[s]
            out_ref[...] = tot

    vm = pl.BlockSpec(memory_space=pltpu.VMEM)
    out_shape = jax.ShapeDtypeStruct((r, cdim) if reduce else (N_DEV, r, cdim), F32)
    scratch = ([pltpu.VMEM((N_DEV, r, cdim), F32)] if reduce else []) + [pltpu.SemaphoreType.DMA((N_DEV - 1,))] * 2
    return _pcall(body, name=name, out_shape=out_shape, in_specs=[vm], out_specs=vm, scratch_shapes=scratch,
                  compiler_params=_params())(blk)


def sum_cores(name, own, got, place):
    _, _, pr, pc = own.shape
    tr = _tile(pr, max(BF16_ROWS, (1 << 19) // pc), BF16_ROWS)

    def body(place_ref, own_ref, got_ref, o_ref):
        o_ref[...] = (own_ref[...].astype(F32) + got_ref[...].astype(F32)).astype(o_ref.dtype)

    spec = pltpu.PrefetchScalarGridSpec(
        num_scalar_prefetch=1, grid=(N_CHIPS, pr // tr),
        in_specs=[pl.BlockSpec((None, None, tr, pc), lambda k, i, s: (k, s[1], i, 0)),
                  pl.BlockSpec((None, tr, pc), lambda k, i, s: (k, i, 0))],
        out_specs=pl.BlockSpec((None, tr, pc), lambda k, i, s: (k, i, 0)))
    return _pcall(body, name=name, out_shape=jax.ShapeDtypeStruct((N_CHIPS, pr, pc), BF16), grid_spec=spec,
                  compiler_params=_params(("parallel", "parallel")))(place, own, got)


def sum_chips(name, part, got, place):
    _, pr, pc = part.shape
    tr = _tile(pr, max(BF16_ROWS, (1 << 18) // pc), BF16_ROWS)

    def body(place_ref, part_ref, got_ref, o_ref):
        tot = part_ref[...].astype(F32)
        for j in range(3):
            tot = tot + got_ref[j].astype(F32)
        o_ref[...] = tot

    spec = pltpu.PrefetchScalarGridSpec(
        num_scalar_prefetch=1, grid=(pr // tr,),
        in_specs=[pl.BlockSpec((None, tr, pc), lambda i, s: (s[0], i, 0)),
                  pl.BlockSpec((3, tr, pc), lambda i, s: (0, i, 0))],
        out_specs=pl.BlockSpec((None, tr, pc), lambda i, s: (s[1], i, 0)))
    return _pcall(body, name=name, out_shape=jax.ShapeDtypeStruct((2, pr, pc), F32), grid_spec=spec,
                  compiler_params=_params(("parallel",)))(place, part, got)


def adamw(name, w, g, m, v):
    rows, cols = w.shape
    tr = _tile(rows, max(8, (1 << 18) // cols))
    c1 = 1.0 / (1.0 - ADAM_B1 ** ADAM_STEP)
    c2 = 1.0 / (1.0 - ADAM_B2 ** ADAM_STEP)

    def body(w_ref, g_ref, m_ref, v_ref, d_ref, nm_ref, nv_ref):
        gv = g_ref[...]
        nm = ADAM_B1 * m_ref[...] + (1.0 - ADAM_B1) * gv
        nv = ADAM_B2 * v_ref[...] + (1.0 - ADAM_B2) * (gv * gv)
        nm_ref[...] = nm
        nv_ref[...] = nv
        d_ref[...] = -ADAM_LR * ((nm * c1) / (jnp.sqrt(nv * c2) + ADAM_EPS) + ADAM_WD * w_ref[...])

    spec = pl.BlockSpec((tr, cols), lambda i: (i, 0))
    sds = jax.ShapeDtypeStruct((rows, cols), F32)
    return _pcall(body, name=name, out_shape=(sds, sds, sds), grid=(rows // tr,),
                  in_specs=[spec] * 4, out_specs=(spec, spec, spec),
                  compiler_params=_params(("parallel",)))(w, g, m, v)


MATS = ["ffn1_w_in", "ffn1_w_out", "w_mix_in", "w_conv_out", "w_attn_out", "w_mix_out", "ffn2_w_in", "ffn2_w_out",
        "w_ple_gate", "w_ple_proj"]
COL_SHARDED = {"ffn1_w_in", "w_mix_in", "ffn2_w_in", "w_ple_proj"}
NORMS = ["ffn1_norm", "mix_norm", "ffn2_norm", "ple_norm", "final_norm"]
WEIGHTS = ["ffn1_norm", "ffn1_w_in", "ffn1_w_out", "mix_norm", "w_mix_in", "conv_w", "w_conv_out", "w_attn_out",
           "w_mix_out", "ffn2_norm", "ffn2_w_in", "ffn2_w_out", "ple_norm", "w_ple_gate", "w_ple_proj", "final_norm"]


def _pad_rows(a, rows):
    return jnp.concatenate([a, jnp.zeros((rows - a.shape[0],) + a.shape[1:], a.dtype)], axis=0)


def _step(x, p, tgt, w, m, v):
    t, d = x.shape
    tt = _tile(t, 256)
    tm = _tile(t, 512)
    tm2 = _tile(t, 1024)
    tq = _tile(t, 256)

    chip = 2 * lax.axis_index("x") + lax.axis_index("y")
    place = jnp.stack([chip, lax.axis_index("c")]).astype(jnp.int32)

    placed = {k: place_shard("place_" + k, w[k], place) for k in MATS}
    full = {}

    def keep(names, bufs):
        for k, buf in zip(names, bufs):
            full[k] = buf if k in COL_SHARDED else buf.reshape(-1, buf.shape[2])

    def gather_of(names):
        return gather_comm([placed[k] for k in names])

    cw_all = gather_small("gather_conv_w", _pad_rows(w["conv_w"], 8), False)
    cw8 = jnp.concatenate([cw_all[2 * k] for k in range(N_CHIPS)], axis=1)
    g1, gm, g2, gp, gf = (w[k].reshape(1, d) for k in NORMS)

    def ffn_fwd(tag, h, g, first, w_in_name, w_out_name, riders):
        if first:
            n, bufs = rms_fwd(tag + "_norm", h, g, tt, comm=gather_of(first))
            keep(first, bufs)
        else:
            n = rms_fwd(tag + "_norm", h, g, tt)
        w_in = full[w_in_name]
        if riders:
            (a, s), bufs = ffn_in_act(tag + "_in", n, w_in, tm, comm=gather_of(riders))
            keep(riders, bufs)
        else:
            a, s = ffn_in_act(tag + "_in", n, w_in, tm)
        return n, a, s, mm_nn(tag + "_out", s, full[w_out_name], F32, tm, res=h, alpha=0.5)

    n1, a1, s1, h1 = ffn_fwd("ffn1", x, g1, ["ffn1_w_in"], "ffn1_w_in", "ffn1_w_out", ["ffn1_w_out", "w_mix_in"])
    u = rms_fwd("mix_norm", h1, gm, tt)
    wmix = full["w_mix_in"]
    riders = [["w_conv_out", "w_attn_out", "w_mix_out"], ["ffn2_w_in"], ["ffn2_w_out", "w_ple_gate", "w_ple_proj"]]
    cbx, bufs = mm_nn_stacked("mix_in_conv", u, wmix, F32, tm2, d, 0, 3, comm=gather_of(riders[0]))
    keep(riders[0], bufs)
    qkv, bufs = mm_nn_stacked("mix_in_qkv", u, wmix, BF16, tm2, d, 3, 3, comm=gather_of(riders[1]))
    keep(riders[1], bufs)
    gates, bufs = mm_nn_stacked("mix_in_gates", u, wmix, F32, tm2, d, 6, 2, comm=gather_of(riders[2]))
    keep(riders[2], bufs)
    wpp = full["w_ple_proj"]
    wpp = jnp.transpose(wpp, (1, 0, 2)).reshape(wpp.shape[1], -1)
    ycin = conv_fwd("conv", cbx, cw8, tt)
    y_conv = mm_nn("conv_out", ycin, full["w_conv_out"], F32, tm)
    o = attn_fwd("attn", qkv, tq)
    y_attn = mm_nn("attn_out", o, full["w_attn_out"], F32, tm)
    merged = gate_fwd("merge", gates, y_conv, y_attn, tt)
    h2 = mm_nn("mix_out", merged, full["w_mix_out"], F32, tm, res=h1, alpha=1.0)
    n2, a2, s2, h3 = ffn_fwd("ffn2", h2, g2, [], "ffn2_w_in", "ffn2_w_out", [])
    npl = rms_fwd("ple_norm", h3, gp, tt)
    zg = mm_nn("ple_gate", npl, full["w_ple_gate"], F32, tm)
    pp = mm_nn("ple_proj", p, wpp, F32, tm)

    pieces, chip_sums, halves = {}, {}, {}

    def as_pieces(k):
        pc = pieces[k]
        return pc if k in COL_SHARDED else pc.reshape(N_CHIPS, 2, pc.shape[0] // (2 * N_CHIPS), pc.shape[1])

    def sum_siblings(tag, names):
        pcs = [as_pieces(k) for k in names]
        got = run_comm("exchange_" + tag, exchange_comm(pcs))
        for k, a, b in zip(names, pcs, got):
            chip_sums[k] = sum_cores("sum_cores_" + k, a, b, place)

    def scatter_of(names):
        return scatter_comm([chip_sums[k] for k in names])

    def sum_landed(names, landed):
        for k, b in zip(names, landed):
            halves[k] = sum_chips("sum_chips_" + k, chip_sums[k], b, place)

    dh4, dpp, dzg, dgf, loss_row = tail("tail", h3, zg, pp, tgt, gf, tt)
    dwpp = mm_tn_whole("ple_proj_dw", p, dpp, tm2)
    pieces["w_ple_proj"] = jnp.transpose(dwpp.reshape(2, p.shape[1] // 2, N_CHIPS, d // N_CHIPS), (2, 0, 1, 3))
    pieces["w_ple_gate"] = mm_tn_rows("ple_gate_dw", npl, dzg, tm2)
    dnp = mm_nt("ple_gate_dx", dzg, full["w_ple_gate"], F32, tm, d)
    dh3, df2, dgp = rms_bwd("ple_norm_bwd", h3, gp, dnp, dh4, 0.5, tt)
    w_in, w_out = full["ffn2_w_in"], full["ffn2_w_out"]
    pieces["ffn2_w_out"] = mm_tn_rows("ffn2_dwout", s2, df2, tm2)
    da2 = ffn_ds_dact("ffn2_ds", df2, w_out, a2, tm)
    pieces["ffn2_w_in"] = mm_tn_cols("ffn2_dwin", n2, da2, tm2)
    dn2 = mm_nt_stacked("ffn2_dn", da2, w_in, F32, tm2, w_in.shape[2])
    dh2, dh2b, dg2 = rms_bwd("ffn2_norm_bwd", h2, g2, dn2, dh3, 1.0, tt)
    pieces["w_mix_out"] = mm_tn_rows("mix_out_dw", merged, dh2b, tm2)
    dmerged = mm_nt("mix_out_dx", dh2b, full["w_mix_out"], F32, tm, d)
    dyc, dya, dgates = gate_bwd("merge_bwd", dmerged, gates, y_conv, y_attn, tt)
    pieces["w_conv_out"] = mm_tn_rows("conv_out_dw", ycin, dyc, tm2)
    dycin = mm_nt("conv_out_dx", dyc, full["w_conv_out"], F32, tm, d)
    dcbx, dcw8 = conv_bwd("conv_bwd", dycin, cbx, cw8, tt)
    pieces["w_attn_out"] = mm_tn_rows("attn_out_dw", o, dya, tm2)
    do = mm_nt("attn_out_dx", dya, full["w_attn_out"], BF16, tm, d)
    dq, dk, dv = attn_bwd("attn_bwd", qkv, do, tq)
    dmix = [dcbx, dq, dk, dv, dgates]
    early = ["ffn2_w_in", "ffn2_w_out", "w_ple_gate", "w_ple_proj", "w_mix_out", "w_conv_out", "w_attn_out"]
    swap = exchange_comm([as_pieces(k) for k in early])
    pieces["w_mix_in"], got = mm_tn_parts("mix_in_dw", u, dmix, tm2, comm=swap)
    for k, a, b in zip(early, swap.ins, got):
        chip_sums[k] = sum_cores("sum_cores_" + k, a, b, place)
    du, landed = mm_nt_parts("mix_in_dx", dmix, wmix, tm2, comm=scatter_of(early))
    sum_landed(early, landed)
    sum_siblings("mix", ["w_mix_in"])
    dh1, df1, dgm = rms_bwd("mix_norm_bwd", h1, gm, du, dh2, 0.5, tt)
    w_in, w_out = full["ffn1_w_in"], full["ffn1_w_out"]
    pieces["ffn1_w_out"] = mm_tn_rows("ffn1_dwout", s1, df1, tm2)
    da1 = ffn_ds_dact("ffn1_ds", df1, w_out, a1, tm)
    pieces["ffn1_w_in"], landed = mm_tn_cols("ffn1_dwin", n1, da1, tm2, comm=scatter_of(["w_mix_in"]))
    sum_landed(["w_mix_in"], landed)
    late = ["ffn1_w_in", "ffn1_w_out"]
    sum_siblings("late", late)
    dn1, landed = mm_nt_stacked("ffn1_dn", da1, w_in, F32, tm2, w_in.shape[2], comm=scatter_of(late))
    sum_landed(late, landed)
    dx, _, dg1 = rms_bwd("ffn1_norm_bwd", x, g1, dn1, dh1, 1.0, tt)

    shared = run_comm("share_halves", share_comm([halves[k] for k in MATS]))
    grad, delta, new_m, new_v = {}, {}, {}, {}
    for k, sh in zip(MATS, shared):
        grad[k] = sh.reshape(w[k].shape)
        delta[k], new_m[k], new_v[k] = adamw("adamw_" + k, w[k], grad[k], m[k], v[k])

    small = jnp.concatenate([dg1, dgm, dg2, dgp, dgf, dcw8[:3], loss_row, jnp.zeros((7, d), F32)], axis=0)
    tot = gather_small("sum_small", small, True)
    loss = tot[8, 0]
    norm_w = jnp.concatenate([w[k].reshape(1, d) for k in NORMS] + [jnp.zeros((3, d), F32)], axis=0)
    norm_m = jnp.concatenate([m[k].reshape(1, d) for k in NORMS] + [jnp.zeros((3, d), F32)], axis=0)
    norm_v = jnp.concatenate([v[k].reshape(1, d) for k in NORMS] + [jnp.ones((3, d), F32)], axis=0)
    norm_g = jnp.concatenate([tot[0:5], jnp.zeros((3, d), F32)], axis=0)
    nd, nm, nv = adamw("adamw_norms", norm_w, norm_g, norm_m, norm_v)
    for r, k in enumerate(NORMS):
        grad[k] = norm_g[r].reshape(w[k].shape)
        delta[k], new_m[k], new_v[k] = (a[r].reshape(w[k].shape) for a in (nd, nm, nv))
    cs = d // N_CHIPS
    gcw = lax.dynamic_slice(tot[5:8], (0, chip * cs), (3, cs))
    cd, cm, cv = adamw("adamw_conv_w", _pad_rows(w["conv_w"], 8), _pad_rows(gcw, 8), _pad_rows(m["conv_w"], 8),
                       jnp.concatenate([v["conv_w"], jnp.ones((5, cs), F32)], axis=0))
    grad["conv_w"], delta["conv_w"], new_m["conv_w"], new_v["conv_w"] = gcw, cd[:3], cm[:3], cv[:3]
    return loss, dx, grad, delta, new_m, new_v


def kernel(x, p, ffn1_norm, ffn1_w_in, ffn1_w_out, mix_norm, w_mix_in, conv_w, w_conv_out, w_attn_out, w_mix_out, ffn2_norm, ffn2_w_in, ffn2_w_out, ple_norm, w_ple_gate, w_ple_proj, final_norm, loss_target, m_ffn1_norm, m_ffn1_w_in, m_ffn1_w_out, m_mix_norm, m_w_mix_in, m_conv_w, m_w_conv_out, m_w_attn_out, m_w_mix_out, m_ffn2_norm, m_ffn2_w_in, m_ffn2_w_out, m_ple_norm, m_w_ple_gate, m_w_ple_proj, m_final_norm, v_ffn1_norm, v_ffn1_w_in, v_ffn1_w_out, v_mix_norm, v_w_mix_in, v_conv_w, v_w_conv_out, v_w_attn_out, v_w_mix_out, v_ffn2_norm, v_ffn2_w_in, v_ffn2_w_out, v_ple_norm, v_w_ple_gate, v_w_ple_proj, v_final_norm):
    ws = (ffn1_norm, ffn1_w_in, ffn1_w_out, mix_norm, w_mix_in, conv_w, w_conv_out, w_attn_out, w_mix_out, ffn2_norm,
          ffn2_w_in, ffn2_w_out, ple_norm, w_ple_gate, w_ple_proj, final_norm)
    ms = (m_ffn1_norm, m_ffn1_w_in, m_ffn1_w_out, m_mix_norm, m_w_mix_in, m_conv_w, m_w_conv_out, m_w_attn_out,
          m_w_mix_out, m_ffn2_norm, m_ffn2_w_in, m_ffn2_w_out, m_ple_norm, m_w_ple_gate, m_w_ple_proj, m_final_norm)
    vs = (v_ffn1_norm, v_ffn1_w_in, v_ffn1_w_out, v_mix_norm, v_w_mix_in, v_conv_w, v_w_conv_out, v_w_attn_out,
          v_w_mix_out, v_ffn2_norm, v_ffn2_w_in, v_ffn2_w_out, v_ple_norm, v_w_ple_gate, v_w_ple_proj, v_final_norm)
    assert x.shape[0] == 1 and p.shape[:2] == (1, 1), "one sequence and one layer per device"

    def strip(a):
        return a[0] if a.ndim == 3 or (a.ndim == 2 and a.shape[0] == 1) else a

    w = {k: strip(a) for k, a in zip(WEIGHTS, ws)}
    m = {k: strip(a) for k, a in zip(WEIGHTS, ms)}
    v = {k: strip(a) for k, a in zip(WEIGHTS, vs)}
    loss, dx, grad, delta, new_m, new_v = _step(x[0], p[0, 0], loss_target[0], w, m, v)
    shapes = [a.shape for a in ws]
    outs = [loss, dx[None]]
    for res in (grad, delta, new_m, new_v):
        outs += [res[k].reshape(s) for k, s in zip(WEIGHTS, shapes)]
    return tuple(outs)
```

```python
import functools
import math

import jax
import jax.numpy as jnp
from jax import lax
from jax.experimental import pallas as pl
from jax.experimental.pallas import tpu as pltpu

F32 = jnp.float32
BF16 = jnp.bfloat16
MESH = pl.DeviceIdType.MESH
ANY = pl.BlockSpec(memory_space=pl.ANY)

HEAD_DIM = 128
NORM_EPS = 1e-6
N_CHIPS = 4
N_DEV = 8
BF16_ROWS = 16
VMEM_LIMIT = 56 * 1024 * 1024
ACC_BYTES = 8 * 1024 * 1024
STICK_EXIT = 110.0

ADAM_LR = 0.001
ADAM_B1 = 0.9
ADAM_B2 = 0.999
ADAM_EPS = 1e-08
ADAM_WD = 0.01
ADAM_STEP = 10

NN = (((1,), (0,)), ((), ()))
NT = (((1,), (1,)), ((), ()))
TN = (((0,), (0,)), ((), ()))


def _params(sem=None, **kw):
    if sem is not None:
        kw["dimension_semantics"] = sem
    return pltpu.CompilerParams(vmem_limit_bytes=VMEM_LIMIT, **kw)


def _pcall(body, **kw):
    return pl.pallas_call(body, **kw)


def _tile(n, pref, mult=8):
    best = None
    for d in range(mult, min(n, pref) + 1, mult):
        if n % d == 0:
            best = d
    return best if best is not None else n


def _dot(a, b, dims):
    return lax.dot_general(a, b, dims, preferred_element_type=F32)


def _call(name, body, grid, in_specs, out_specs, out_shape, args, scratch=(), sem=None, comm=None):
    n_in, n_out, n_sc = len(in_specs), len(out_specs), len(scratch)
    if comm is None:
        def plain(*refs):
            body(refs[:n_in], refs[n_in:n_in + n_out], refs[n_in + n_out:])

        return _pcall(plain, name=name, out_shape=list(out_shape), grid=grid, in_specs=list(in_specs),
                      out_specs=list(out_specs), scratch_shapes=list(scratch), compiler_params=_params(sem))(*args)
    n_cin, n_cout = len(comm.ins), len(comm.outs)
    steps = math.prod(grid)

    def hosted(*refs):
        ins, c_ins = refs[:n_in], refs[n_in:n_in + n_cin]
        outs = refs[n_in + n_cin:n_in + n_cin + n_out]
        c_outs = refs[n_in + n_cin + n_out:n_in + n_cin + n_out + n_cout]
        rest = refs[n_in + n_cin + n_out + n_cout:]
        sems = rest[n_sc:]
        step = pl.program_id(0)
        for ax in range(1, len(grid)):
            step = step * grid[ax] + pl.program_id(ax)

        @pl.when(step == 0)
        def _():
            comm.first(c_ins, c_outs, sems)

        body(ins, outs, rest[:n_sc])

        @pl.when(step == (3 * steps) // 4)
        def _():
            comm.mid(c_ins, c_outs, sems)

        @pl.when(step == steps - 1)
        def _():
            comm.last(c_ins, c_outs, sems)

    res = _pcall(hosted, name=name, out_shape=list(out_shape) + comm.outs, grid=grid,
                 in_specs=list(in_specs) + [ANY] * n_cin, out_specs=list(out_specs) + [ANY] * n_cout,
                 input_output_aliases={n_in + k: n_out + v for k, v in comm.aliases.items()},
                 scratch_shapes=list(scratch) + comm.sems,
                 compiler_params=_params(("arbitrary",) * len(grid)))(*args, *comm.ins)
    return list(res[:n_out]), list(res[n_out:])


NORM_CHUNK = 256


def _norm_bwd_tile(read_dn, rows, first, h_ref, g_ref, dr_ref, dh_ref, dhb_ref, dg_ref, alpha):
    @pl.when(first)
    def _():
        dg_ref[...] = jnp.zeros_like(dg_ref)

    gv = g_ref[...]
    tot = jnp.zeros_like(gv)
    for c0 in range(0, rows, NORM_CHUNK):
        sl = slice(c0, min(rows, c0 + NORM_CHUNK))
        hv = h_ref[sl, :]
        rs = _rstd(hv)
        hn = hv * rs
        dnv = read_dn(sl)
        gy = dnv * gv
        dh = dr_ref[sl, :] + rs * (gy - hn * jnp.mean(gy * hn, axis=-1, keepdims=True))
        dh_ref[sl, :] = dh
        dhb_ref[sl, :] = (alpha * dh).astype(BF16)
        tot = tot + jnp.sum(dnv * hn, axis=0, keepdims=True)
    dg_ref[...] += tot


def _mm(name, a, b, out_sds, grid, a_spec, b_spec, o_spec, dims, acc_shape, res=None, alpha=1.0, comm=None,
        norm=None):
    nk = grid[2]

    def body(ins, outs, scratch):
        a_ref, b_ref = ins[:2]
        r_ref = ins[2] if res is not None else None
        o_ref = outs[0]

        def finish(read):
            if norm is not None:
                first = jnp.logical_and(pl.program_id(0) == 0, pl.program_id(1) == 0)
                _norm_bwd_tile(read, o_ref.shape[0], first, *ins[2:5], *outs, alpha)
                return
            r = read(slice(None))
            if alpha != 1.0:
                r = r * alpha
            if r_ref is not None:
                r = r_ref[...] + r
            if len(o_ref.shape) == 3:
                half = o_ref.shape[1]
                o_ref[0] = r[:half].astype(o_ref.dtype)
                o_ref[1] = r[half:].astype(o_ref.dtype)
            else:
                o_ref[...] = r.astype(o_ref.dtype)

        if nk == 1:
            part = _dot(a_ref[...].astype(BF16), b_ref[...].astype(BF16), dims)
            finish(lambda sl: part[sl])
        else:
            acc_ref = scratch[0]
            kk = pl.program_id(2)

            @pl.when(kk == 0)
            def _():
                acc_ref[...] = jnp.zeros_like(acc_ref)

            acc_ref[...] += _dot(a_ref[...].astype(BF16), b_ref[...].astype(BF16), dims)

            @pl.when(kk == nk - 1)
            def _():
                finish(lambda sl: acc_ref[sl, :])

    in_specs = [a_spec, b_spec]
    args = [a, b]
    out_specs, out_shape = [o_spec], [out_sds]
    sem = ("parallel", "parallel", "arbitrary")
    if res is not None:
        in_specs.append(o_spec)
        args.append(res)
    if norm is not None:
        width = out_sds.shape[1]
        whole = pl.BlockSpec((1, width), lambda i, j, r: (0, 0))
        in_specs += [o_spec, whole, o_spec]
        args += list(norm)
        out_specs = [o_spec, o_spec, whole]
        out_shape = [jax.ShapeDtypeStruct(out_sds.shape, F32), jax.ShapeDtypeStruct(out_sds.shape, BF16),
                     jax.ShapeDtypeStruct((1, width), F32)]
        sem = ("arbitrary", "arbitrary", "arbitrary")
    scratch = [] if nk == 1 else [pltpu.VMEM(acc_shape, F32)]
    got = _call(name, body, grid, in_specs, out_specs, out_shape, args, scratch, sem, comm)
    if norm is not None:
        return got if comm is None else (got[0], got[1])
    return got[0] if comm is None else (got[0][0], got[1])


def ffn_in_act(name, n, w4, tm, comm=None):
    t, d = n.shape
    cs = w4.shape[2]

    def body(ins, outs, scratch):
        n_ref, wg_ref, wu_ref = ins
        a_ref, s_ref = outs
        nv = n_ref[...]
        gate = _dot(nv, wg_ref[...], NN)
        up = _dot(nv, wu_ref[...], NN)
        a_ref[0] = gate.astype(BF16)
        a_ref[1] = up.astype(BF16)
        s_ref[...] = (gate * jax.nn.sigmoid(gate) * up).astype(BF16)

    got = _call(name, body, (t // tm, 2),
                [pl.BlockSpec((tm, d), lambda i, j: (i, 0)),
                 pl.BlockSpec((None, d, cs), lambda i, j: (j, 0, 0)),
                 pl.BlockSpec((None, d, cs), lambda i, j: (2 + j, 0, 0))],
                [pl.BlockSpec((2, tm, cs), lambda i, j: (0, i, j)), pl.BlockSpec((tm, cs), lambda i, j: (i, j))],
                [jax.ShapeDtypeStruct((2, t, 2 * cs), BF16), jax.ShapeDtypeStruct((t, 2 * cs), BF16)],
                [n, w4, w4], (), ("parallel", "parallel"), comm)
    return got if comm is None else (got[0], got[1])


def ffn_ds_dact(name, df, w_out, a3, tm):
    t, d = df.shape
    f = w_out.shape[0]
    cs = f // 2

    def body(ins, outs, scratch):
        df_ref, w_ref, a_ref = ins
        ds = _dot(df_ref[...], w_ref[...], NT)
        gate = a_ref[0].astype(F32)
        up = a_ref[1].astype(F32)
        sg = jax.nn.sigmoid(gate)
        outs[0][0] = (ds * up * sg * (1.0 + gate * (1.0 - sg))).astype(BF16)
        outs[0][1] = (ds * gate * sg).astype(BF16)

    blk = pl.BlockSpec((2, tm, cs), lambda i, j: (0, i, j))
    return _call(name, body, (t // tm, 2),
                 [pl.BlockSpec((tm, d), lambda i, j: (i, 0)), pl.BlockSpec((cs, d), lambda i, j: (j, 0)), blk],
                 [blk], [jax.ShapeDtypeStruct((2, t, f), BF16)], [df, w_out, a3], (), ("parallel", "parallel"))[0]


def _part_ranges(parts, d):
    out, lo = [], 0
    for p in parts:
        out.append((lo, p.shape[1] // d))
        lo += p.shape[1] // d
    return out, lo


def mm_nt_parts(name, parts, w4, tm, norm, alpha, comm=None):
    m = parts[0].shape[0]
    d, cs = w4.shape[1], w4.shape[2]
    per = cs // d
    ranges, nblk = _part_ranges(parts, d)
    np_ = len(parts)

    def body(ins, outs, scratch):
        w_ref, acc = ins[np_], scratch[0]
        r = pl.program_id(1)

        @pl.when(r == 0)
        def _():
            acc[...] = jnp.zeros_like(acc)

        for (lo, n), a_ref in zip(ranges, ins[:np_]):
            @pl.when(jnp.logical_and(r >= lo, r < lo + n))
            def _(a_ref=a_ref):
                acc[...] += _dot(a_ref[...], w_ref[...], NT)

        @pl.when(r == nblk - 1)
        def _():
            _norm_bwd_tile(lambda sl: acc[sl, :], tm, pl.program_id(0) == 0, *ins[np_ + 1:], *outs, alpha)

    rows = pl.BlockSpec((tm, d), lambda i, r: (i, 0))
    whole = pl.BlockSpec((1, d), lambda i, r: (0, 0))
    specs = [pl.BlockSpec((tm, d), lambda i, r, lo=lo, n=n: (i, jnp.clip(r - lo, 0, n - 1))) for lo, n in ranges]
    specs += [pl.BlockSpec((None, d, d), lambda i, r: (r // per, 0, r % per)), rows, whole, rows]
    got = _call(name, body, (m // tm, nblk), specs, [rows, rows, whole],
                [jax.ShapeDtypeStruct((m, d), F32), jax.ShapeDtypeStruct((m, d), BF16),
                 jax.ShapeDtypeStruct((1, d), F32)],
                list(parts) + [w4] + list(norm), [pltpu.VMEM((tm, d), F32)], ("arbitrary", "arbitrary"), comm)
    return got if comm is None else (got[0], got[1])


def mm_tn_parts(name, xa, parts, tt, comm=None):
    t, k = xa.shape
    d = k
    pr = k // 2
    ranges, nblk = _part_ranges(parts, d)
    per = nblk // N_CHIPS

    def body(ins, outs, scratch):
        x_ref, acc = ins[0], scratch[0]
        jb, r = pl.program_id(0), pl.program_id(1)

        @pl.when(r == 0)
        def _():
            acc[...] = jnp.zeros_like(acc)

        for (lo, n), p_ref in zip(ranges, ins[1:]):
            @pl.when(jnp.logical_and(jb >= lo, jb < lo + n))
            def _(p_ref=p_ref):
                acc[...] += _dot(x_ref[...], p_ref[...], TN)

        @pl.when(r == t // tt - 1)
        def _():
            outs[0][0] = acc[:pr].astype(BF16)
            outs[0][1] = acc[pr:].astype(BF16)

    def part_spec(lo, n):
        return pl.BlockSpec((tt, d), lambda jb, r: (jnp.where(jnp.logical_and(jb >= lo, jb < lo + n), r, 0),
                                                    jnp.clip(jb - lo, 0, n - 1)))

    specs = [pl.BlockSpec((tt, k), lambda jb, r: (r, 0))] + [part_spec(lo, n) for lo, n in ranges]
    got = _call(name, body, (nblk, t // tt), specs,
                [pl.BlockSpec((None, 2, pr, d), lambda jb, r: (jb // per, 0, 0, jb % per))],
                [jax.ShapeDtypeStruct((N_CHIPS, 2, pr, per * d), BF16)], [xa] + list(parts),
                [pltpu.VMEM((k, d), F32)], ("parallel", "arbitrary"), comm)
    return got[0] if comm is None else (got[0][0], got[1])


def mm_nn(name, a, w, out_dtype, tm, res=None, alpha=1.0):
    m, k = a.shape
    n = w.shape[1]
    return _mm(name, a, w, jax.ShapeDtypeStruct((m, n), out_dtype), (m // tm, 1, 1),
               pl.BlockSpec((tm, k), lambda i, j, r: (i, 0)),
               pl.BlockSpec((k, n), lambda i, j, r: (0, 0)),
               pl.BlockSpec((tm, n), lambda i, j, r: (i, 0)), NN, None, res=res, alpha=alpha)


def mm_nn_stacked(name, a, w4, out_dtype, tm, tn, j0=0, nj=None, comm=None):
    m, k = a.shape
    cs = w4.shape[2]
    per = cs // tn
    nj = N_CHIPS * per - j0 if nj is None else nj
    return _mm(name, a, w4, jax.ShapeDtypeStruct((m, nj * tn), out_dtype), (m // tm, nj, 1),
               pl.BlockSpec((tm, k), lambda i, j, r: (i, 0)),
               pl.BlockSpec((None, k, tn), lambda i, j, r: ((j + j0) // per, 0, (j + j0) % per)),
               pl.BlockSpec((tm, tn), lambda i, j, r: (i, j)), NN, None, comm=comm)


def mm_nt(name, dy, w, out_dtype, tm, tko, norm=None, alpha=1.0):
    m, n = dy.shape
    k = w.shape[0]
    return _mm(name, dy, w, jax.ShapeDtypeStruct((m, k), out_dtype), (m // tm, k // tko, 1),
               pl.BlockSpec((tm, n), lambda i, j, r: (i, 0)),
               pl.BlockSpec((tko, n), lambda i, j, r: (j, 0)),
               pl.BlockSpec((tm, tko), lambda i, j, r: (i, j)), NT, None, norm=norm, alpha=alpha)


def mm_nt_stacked(name, dy, w4, out_dtype, tm, tn, comm=None, norm=None, alpha=1.0):
    m = dy.shape[-2]
    k, cs = w4.shape[1], w4.shape[2]
    per = cs // tn
    if dy.ndim == 3:
        dy_spec = pl.BlockSpec((None, tm, cs), lambda i, j, r: (r // 2, i, r % 2))
    else:
        dy_spec = pl.BlockSpec((tm, tn), lambda i, j, r: (i, r))
    return _mm(name, dy, w4, jax.ShapeDtypeStruct((m, k), out_dtype), (m // tm, 1, N_CHIPS * per), dy_spec,
               pl.BlockSpec((None, k, tn), lambda i, j, r: (r // per, 0, r % per)),
               pl.BlockSpec((tm, k), lambda i, j, r: (i, 0)), NT, (tm, k), comm=comm, norm=norm, alpha=alpha)


def mm_tn_rows(name, xa, dy, tt):
    t, k = xa.shape
    n = dy.shape[1]
    tkr = k if k * n * 4 <= ACC_BYTES else k // 2
    return _mm(name, xa, dy, jax.ShapeDtypeStruct((k, n), BF16), (k // tkr, 1, t // tt),
               pl.BlockSpec((tt, tkr), lambda i, j, r: (r, i)),
               pl.BlockSpec((tt, n), lambda i, j, r: (r, 0)),
               pl.BlockSpec((tkr, n), lambda i, j, r: (i, 0)), TN, (tkr, n))


def mm_tn_whole(name, xa, dy, tt):
    t, k = xa.shape
    n = dy.shape[1]
    return _mm(name, xa, dy, jax.ShapeDtypeStruct((k, n), BF16), (1, 1, t // tt),
               pl.BlockSpec((tt, k), lambda i, j, r: (r, 0)),
               pl.BlockSpec((tt, n), lambda i, j, r: (r, 0)),
               pl.BlockSpec((k, n), lambda i, j, r: (0, 0)), TN, (k, n))


def mm_tn_cols(name, xa, dy, tt, comm=None):
    t, k = xa.shape
    pr = k // 2
    if dy.ndim == 3:
        cs = dy.shape[2] // 2
        dy_spec = pl.BlockSpec((None, tt, cs), lambda i, j, r: (j // 2, r, j % 2))
    else:
        cs = dy.shape[1] // N_CHIPS
        dy_spec = pl.BlockSpec((tt, cs), lambda i, j, r: (r, j))
    return _mm(name, xa, dy, jax.ShapeDtypeStruct((N_CHIPS, 2, pr, cs), BF16), (1, N_CHIPS, t // tt),
               pl.BlockSpec((tt, k), lambda i, j, r: (r, 0)), dy_spec,
               pl.BlockSpec((None, 2, pr, cs), lambda i, j, r: (j, 0, 0, 0)), TN, (k, cs), comm=comm)


def _rows(tt, w, col=0):
    return pl.BlockSpec((tt, w), lambda i: (i, col))


def _whole(shape):
    return pl.BlockSpec(shape, lambda i: (0,) * len(shape))


def _rstd(h):
    return lax.rsqrt(jnp.mean(h * h, axis=-1, keepdims=True) + NORM_EPS)


def rms_fwd(name, h, g, tt, comm=None):
    t, d = h.shape

    def body(ins, outs, scratch):
        hv = ins[0][...]
        outs[0][...] = (hv * _rstd(hv) * ins[1][...]).astype(BF16)

    got = _call(name, body, (t // tt,), [_rows(tt, d), _whole((1, d))], [_rows(tt, d)],
                [jax.ShapeDtypeStruct((t, d), BF16)], [h, g], (), ("parallel",), comm)
    return got[0] if comm is None else (got[0][0], got[1])


def gate_fwd(name, gates, yc, ya, tt):
    t, d = yc.shape

    def body(g_ref, yc_ref, ya_ref, o_ref):
        o_ref[...] = (jax.nn.sigmoid(g_ref[:, :d].astype(F32)) * yc_ref[...].astype(F32)
                      + jax.nn.sigmoid(g_ref[:, d:].astype(F32)) * ya_ref[...].astype(F32)).astype(o_ref.dtype)

    return _pcall(body, name=name, out_shape=jax.ShapeDtypeStruct((t, d), BF16), grid=(t // tt,),
                  in_specs=[_rows(tt, 2 * d), _rows(tt, d), _rows(tt, d)], out_specs=_rows(tt, d),
                  compiler_params=_params(("parallel",)))(gates, yc, ya)


def gate_bwd(name, dm, gates, yc, ya, tt):
    t, d = yc.shape

    def body(dm_ref, g_ref, yc_ref, ya_ref, dyc_ref, dya_ref, dg_ref):
        dmv = dm_ref[...]
        sc = jax.nn.sigmoid(g_ref[:, :d].astype(F32))
        sa = jax.nn.sigmoid(g_ref[:, d:].astype(F32))
        dyc_ref[...] = (dmv * sc).astype(BF16)
        dya_ref[...] = (dmv * sa).astype(BF16)
        dg_ref[:, :d] = (dmv * yc_ref[...].astype(F32) * sc * (1.0 - sc)).astype(BF16)
        dg_ref[:, d:] = (dmv * ya_ref[...].astype(F32) * sa * (1.0 - sa)).astype(BF16)

    return _pcall(body, name=name,
                  out_shape=(jax.ShapeDtypeStruct((t, d), BF16), jax.ShapeDtypeStruct((t, d), BF16),
                             jax.ShapeDtypeStruct((t, 2 * d), BF16)),
                  grid=(t // tt,),
                  in_specs=[_rows(tt, d), _rows(tt, 2 * d), _rows(tt, d), _rows(tt, d)],
                  out_specs=(_rows(tt, d), _rows(tt, d), _rows(tt, 2 * d)),
                  compiler_params=_params(("parallel",)))(dm, gates, yc, ya)


def _shift_down(cur, prev8, s):
    tt = cur.shape[0]
    rolled = pltpu.roll(cur, s, 0)
    row8 = lax.broadcasted_iota(jnp.int32, prev8.shape, 0)
    first8 = jnp.where(row8 < s, pltpu.roll(prev8, s, 0), rolled[:8])
    return jnp.concatenate([first8, rolled[8:]], axis=0) if tt > 8 else first8


def _shift_up(cur, next8, s):
    tt = cur.shape[0]
    rolled = pltpu.roll(cur, tt - s, 0)
    row8 = lax.broadcasted_iota(jnp.int32, next8.shape, 0)
    last8 = jnp.where(row8 >= 8 - s, pltpu.roll(next8, 8 - s, 0), rolled[tt - 8:])
    return jnp.concatenate([rolled[:tt - 8], last8], axis=0) if tt > 8 else last8


def _prev8(tt, d, col):
    return pl.BlockSpec((8, d), lambda i: (jnp.maximum(i * (tt // 8) - 1, 0), col))


def _next8(tt, d, col, t):
    return pl.BlockSpec((8, d), lambda i: (jnp.minimum((i + 1) * (tt // 8), t // 8 - 1), col))


def conv_fwd(name, cbx, cw8, tt):
    t, d3 = cbx.shape
    d = d3 // 3

    def body(cb_ref, cc_ref, cx_ref, pc_ref, px_ref, w_ref, o_ref):
        has_prev = (pl.program_id(0) > 0).astype(F32)
        cc = cc_ref[...] * cx_ref[...]
        prev = pc_ref[...] * px_ref[...] * has_prev
        w = w_ref[...]
        conv = w[0:1] * _shift_down(cc, prev, 2) + w[1:2] * _shift_down(cc, prev, 1) + w[2:3] * cc
        o_ref[...] = (cb_ref[...] * conv).astype(o_ref.dtype)

    return _pcall(body, name=name, out_shape=jax.ShapeDtypeStruct((t, d), BF16), grid=(t // tt,),
                  in_specs=[_rows(tt, d, 0), _rows(tt, d, 1), _rows(tt, d, 2), _prev8(tt, d, 1), _prev8(tt, d, 2),
                            _whole((8, d))],
                  out_specs=_rows(tt, d), compiler_params=_params(("parallel",)))(cbx, cbx, cbx, cbx, cbx, cw8)


def conv_bwd(name, dyc, cbx, cw8, tt):
    t, d3 = cbx.shape
    d = d3 // 3
    n = t // tt

    def body(dy_ref, cb_ref, cc_ref, cx_ref, pc_ref, px_ref, ndy_ref, ncb_ref, w_ref, o_ref, dw_ref):
        i = pl.program_id(0)
        has_prev = (i > 0).astype(F32)
        has_next = (i < n - 1).astype(F32)
        cb = cb_ref[...]
        cc = cc_ref[...] * cx_ref[...]
        prev = pc_ref[...] * px_ref[...] * has_prev
        w = w_ref[...]
        cc1 = _shift_down(cc, prev, 1)
        cc2 = _shift_down(cc, prev, 2)
        conv = w[0:1] * cc2 + w[1:2] * cc1 + w[2:3] * cc
        dyv = dy_ref[...]
        dconv = dyv * cb
        dnext = ndy_ref[...] * ncb_ref[...] * has_next
        dcc = w[2:3] * dconv + w[1:2] * _shift_up(dconv, dnext, 1) + w[0:1] * _shift_up(dconv, dnext, 2)
        o_ref[:, :d] = (dyv * conv).astype(BF16)
        o_ref[:, d:2 * d] = (dcc * cx_ref[...]).astype(BF16)
        o_ref[:, 2 * d:] = (dcc * cc_ref[...]).astype(BF16)

        @pl.when(i == 0)
        def _():
            dw_ref[...] = jnp.zeros_like(dw_ref)

        dw_ref[0:1, :] += jnp.sum(dconv * cc2, axis=0, keepdims=True)
        dw_ref[1:2, :] += jnp.sum(dconv * cc1, axis=0, keepdims=True)
        dw_ref[2:3, :] += jnp.sum(dconv * cc, axis=0, keepdims=True)

    return _pcall(body, name=name,
                  out_shape=(jax.ShapeDtypeStruct((t, d3), BF16), jax.ShapeDtypeStruct((8, d), F32)),
                  grid=(n,),
                  in_specs=[_rows(tt, d), _rows(tt, d, 0), _rows(tt, d, 1), _rows(tt, d, 2),
                            _prev8(tt, d, 1), _prev8(tt, d, 2), _next8(tt, d, 0, t), _next8(tt, d, 0, t),
                            _whole((8, d))],
                  out_specs=(_rows(tt, d3), _whole((8, d))),
                  compiler_params=_params(("arbitrary",)))(dyc, cbx, cbx, cbx, cbx, cbx, dyc, cbx, cw8)


def tail(name, h3, zg, pp, tgt, gf, tt):
    t, d = h3.shape

    def body(h_ref, zg_ref, pp_ref, tg_ref, gf_ref, dh_ref, dpp_ref, dzg_ref, dgf_ref, loss_ref):
        pg = jax.nn.sigmoid(zg_ref[...])
        ppv = pp_ref[...]
        h4 = h_ref[...] + pg * ppv
        r4 = _rstd(h4)
        hn = h4 * r4
        gfv = gf_ref[...]
        err = hn * gfv - tg_ref[...]
        dy = err * (1.0 / d)
        gy = dy * gfv
        dh4 = r4 * (gy - hn * jnp.mean(gy * hn, axis=-1, keepdims=True))
        dh_ref[...] = dh4
        dpp_ref[...] = (dh4 * pg).astype(BF16)
        dzg_ref[...] = (dh4 * ppv * pg * (1.0 - pg)).astype(BF16)

        @pl.when(pl.program_id(0) == 0)
        def _():
            dgf_ref[...] = jnp.zeros_like(dgf_ref)
            loss_ref[...] = jnp.zeros_like(loss_ref)

        dgf_ref[...] += jnp.sum(dy * hn, axis=0, keepdims=True)
        tok = jnp.mean(err * err, axis=-1, keepdims=True)
        loss_ref[...] += 0.5 * jnp.sum(tok, axis=0, keepdims=True) * jnp.ones((1, loss_ref.shape[1]), F32)

    return _pcall(body, name=name,
                  out_shape=(jax.ShapeDtypeStruct((t, d), F32), jax.ShapeDtypeStruct((t, d), BF16),
                             jax.ShapeDtypeStruct((t, d), BF16), jax.ShapeDtypeStruct((1, d), F32),
                             jax.ShapeDtypeStruct((1, d), F32)),
                  grid=(t // tt,),
                  in_specs=[_rows(tt, d)] * 4 + [_whole((1, d))],
                  out_specs=(_rows(tt, d), _rows(tt, d), _rows(tt, d), _whole((1, d)), _whole((1, d))),
                  compiler_params=_params(("arbitrary",)))(h3, zg, pp, tgt, gf)


SCALE = 1.0 / math.sqrt(HEAD_DIM)


def _log_stick(z):
    return -(jnp.maximum(z, 0.0) + jnp.log(1.0 + jnp.exp(-jnp.abs(z))))


def _tri_sum(x, tri):
    hi = x.astype(BF16)
    lo = (x - hi.astype(F32)).astype(BF16)
    return _dot(hi, tri, NN) + _dot(lo, tri, NN)


KEY_BLOCK = 128
NEAR = 3


def _sb_near(qs, jds, k_ref, below, upper):
    pairs = [(s, b) for s in range(len(qs)) for b in range(NEAR)]
    rows = {(s, b): _block_rows(jnp.maximum(jds[s] - b, 0), KEY_BLOCK) for s, b in pairs}
    z = {(s, b): _dot(qs[s], k_ref[rows[s, b], :], NT) * SCALE for s, b in pairs}
    lg = {(s, b): jnp.where(below, _log_stick(z[s, b]), 0.0) if b == 0 else _log_stick(z[s, b]) for s, b in pairs}
    cum = {(s, b): _tri_sum(lg[s, b], upper) for s, b in pairs}
    out, carries = [], []
    for s in range(len(qs)):
        c = cum[s, 0][:, 0:1]
        blocks = [(rows[s, 0], z[s, 0], jnp.exp(jnp.where(below, z[s, 0] + cum[s, 0], -1e30)))]
        for b in range(1, NEAR):
            live = jds[s] >= b
            blocks.append((rows[s, b], z[s, b], jnp.exp(z[s, b] + cum[s, b] + (c + jnp.where(live, 0.0, -1e30)))))
            c = c + jnp.where(live, cum[s, b][:, 0:1], 0.0)
        out.append(blocks)
        carries.append(c)
    return out, carries


def _sb_far(q, kj, upper, c):
    z = _dot(q, kj, NT) * SCALE
    cum = _tri_sum(_log_stick(z), upper)
    return z, jnp.exp(z + cum + c), c + cum[:, 0:1]


def _block_rows(j, size):
    return pl.ds(pl.multiple_of(j * size, size), size)


def _sweep_on(st):
    return jnp.logical_and(st[0] >= 0, jnp.max(st[1]) > -STICK_EXIT)


def attn_fwd(name, qkv, tq):
    t, d3 = qkv.shape
    d = d3 // 3
    nh = d // HEAD_DIM
    nq = t // tq
    tb = KEY_BLOCK
    nsub = tq // tb

    def body(q_ref, k_ref, v_ref, o_ref):
        i = pl.program_id(1)
        row = lax.broadcasted_iota(jnp.int32, (tb, tb), 0)
        col = lax.broadcasted_iota(jnp.int32, (tb, tb), 1)
        upper = (row >= col).astype(BF16)
        qs = [q_ref[s * tb:(s + 1) * tb, :] for s in range(nsub)]
        jds = [i * nsub + s for s in range(nsub)]
        near, carries = _sb_near(qs, jds, k_ref, col < row, upper)
        state = []
        for s in range(nsub):
            acc = jnp.zeros((tb, HEAD_DIM), F32)
            for rows, _, a in near[s]:
                acc = acc + _dot(a.astype(BF16), v_ref[rows, :], NN)
            state.append((qs[s], jds[s], carries[s], acc))
        for s, (q, jd, c, acc) in enumerate(state):

            def step(st, q=q):
                rows = _block_rows(st[0], tb)
                _, a, c2 = _sb_far(q, k_ref[rows, :], upper, st[1])
                return st[0] - 1, c2, st[2] + _dot(a.astype(BF16), v_ref[rows, :], NN)

            _, _, acc = lax.while_loop(_sweep_on, step, (jd - NEAR, c, acc))
            o_ref[s * tb:(s + 1) * tb, :] = acc.astype(o_ref.dtype)

    return _pcall(body, name=name, out_shape=jax.ShapeDtypeStruct((t, d), BF16), grid=(nh, nq),
                  in_specs=[pl.BlockSpec((tq, HEAD_DIM), lambda h, i: (i, h)),
                            pl.BlockSpec((t, HEAD_DIM), lambda h, i: (0, nh + h)),
                            pl.BlockSpec((t, HEAD_DIM), lambda h, i: (0, 2 * nh + h))],
                  out_specs=pl.BlockSpec((tq, HEAD_DIM), lambda h, i: (i, h)),
                  compiler_params=_params(("parallel", "arbitrary")))(qkv, qkv, qkv)


def attn_bwd(name, qkv, do, tq):
    t, d3 = qkv.shape
    d = d3 // 3
    nh = d // HEAD_DIM
    nq = t // tq
    tb = KEY_BLOCK
    nsub = tq // tb

    def body(q_ref, k_ref, v_ref, do_ref, dq_ref, dk_ref, dv_ref, dk_acc, dv_acc, g_buf, z_buf):
        i = pl.program_id(1)

        @pl.when(i == 0)
        def _():
            dk_acc[...] = jnp.zeros_like(dk_acc)
            dv_acc[...] = jnp.zeros_like(dv_acc)

        row = lax.broadcasted_iota(jnp.int32, (tb, tb), 0)
        col = lax.broadcasted_iota(jnp.int32, (tb, tb), 1)
        below = col < row
        upper = (row >= col).astype(BF16)
        lower = (row <= col).astype(BF16)

        qs = [q_ref[s * tb:(s + 1) * tb, :] for s in range(nsub)]
        dos = [do_ref[s * tb:(s + 1) * tb, :] for s in range(nsub)]
        jds = [i * nsub + s for s in range(nsub)]
        near, carries = _sb_near(qs, jds, k_ref, below, upper)
        da = [[_dot(dos[s], v_ref[rows, :], NT) for rows, _, _ in near[s]] for s in range(nsub)]
        state = []
        for s in range(nsub):
            kept = [(rows, z, da[s][b] * a) for b, (rows, z, a) in enumerate(near[s])]
            for rows, _, a in near[s]:
                dv_acc[rows, :] += _dot(a.astype(BF16), dos[s], TN)
            state.append((qs[s], dos[s], jds[s], carries[s], kept))

        carried = []
        for s, (q, dov, jd, c, kept) in enumerate(state):
            def step(st, s=s, q=q, dov=dov, jd=jd):
                j = st[0]
                rows = _block_rows(j, tb)
                z, a, c2 = _sb_far(q, k_ref[rows, :], upper, st[1])
                g_buf[s, jd - j] = _dot(dov, v_ref[rows, :], NT) * a
                z_buf[s, jd - j] = z
                dv_acc[rows, :] += _dot(a.astype(BF16), dov, TN)
                return j - 1, c2

            j_stop, _ = lax.while_loop(_sweep_on, step, (jd - NEAR, c))

            def far(j, st, s=s, q=q, jd=jd):
                run, dq = st
                rows = _block_rows(j, tb)
                g = g_buf[s, jd - j]
                dz = (g - jax.nn.sigmoid(z_buf[s, jd - j]) * (run + _tri_sum(g, lower))).astype(BF16)
                dk_acc[rows, :] += _dot(dz, q, TN)
                return run + jnp.sum(g, axis=1, keepdims=True), dq + _dot(dz, k_ref[rows, :], NN)

            carried.append(lax.fori_loop(j_stop + 1, jd - NEAR + 1, far,
                                         (jnp.zeros((tb, 1), F32), jnp.zeros((tb, HEAD_DIM), F32))))

        tri = [[_tri_sum(g, lower) for _, _, g in st[4]] for st in state]
        sig = [[jax.nn.sigmoid(z) for _, z, _ in st[4]] for st in state]
        for s, (q, dov, jd, c, kept) in enumerate(state):
            run, dq = carried[s]
            for b in reversed(range(NEAR)):
                rows, z, g = kept[b]
                dz = g - sig[s][b] * (run + tri[s][b])
                if b == 0:
                    dz = jnp.where(below, dz, 0.0)
                dz = dz.astype(BF16)
                dk_acc[rows, :] += _dot(dz, q, TN)
                dq = dq + _dot(dz, k_ref[rows, :], NN)
                if b:
                    run = run + jnp.sum(g, axis=1, keepdims=True)
            dq_ref[s * tb:(s + 1) * tb, :] = (dq * SCALE).astype(BF16)

        @pl.when(i == nq - 1)
        def _():
            dk_ref[...] = (dk_acc[...] * SCALE).astype(BF16)
            dv_ref[...] = dv_acc[...].astype(BF16)

    blk = pl.BlockSpec((tq, HEAD_DIM), lambda h, i: (i, h))
    col_h = pl.BlockSpec((t, HEAD_DIM), lambda h, i: (0, h))
    out = jax.ShapeDtypeStruct((t, d), BF16)
    return _pcall(body, name=name, out_shape=(out, out, out), grid=(nh, nq),
                  in_specs=[blk,
                            pl.BlockSpec((t, HEAD_DIM), lambda h, i: (0, nh + h)),
                            pl.BlockSpec((t, HEAD_DIM), lambda h, i: (0, 2 * nh + h)),
                            blk],
                  out_specs=(blk, col_h, col_h),
                  scratch_shapes=[pltpu.VMEM((t, HEAD_DIM), F32), pltpu.VMEM((t, HEAD_DIM), F32),
                                  pltpu.VMEM((nsub, t // tb, tb, tb), F32), pltpu.VMEM((nsub, t // tb, tb, tb), F32)],
                  compiler_params=_params(("parallel", "arbitrary")))(qkv, qkv, qkv, do)


def _place():
    x, y, c = lax.axis_index("x"), lax.axis_index("y"), lax.axis_index("c")
    chips = [(1 - x, y), (x, 1 - y), (1 - x, 1 - y)]
    return x, y, c, chips


def _remote(src, dst, send_sem, recv_sem, dev):
    return pltpu.make_async_remote_copy(src_ref=src, dst_ref=dst, send_sem=send_sem, recv_sem=recv_sem,
                                        device_id=dev, device_id_type=MESH)


def place_shard(name, w, chip):
    r, cdim = w.shape
    tr = _tile(r, max(BF16_ROWS, (1 << 19) // cdim), BF16_ROWS)

    def body(chip_ref, w_ref, o_ref):
        o_ref[...] = w_ref[...].astype(BF16)

    spec = pltpu.PrefetchScalarGridSpec(
        num_scalar_prefetch=1, grid=(r // tr,),
        in_specs=[pl.BlockSpec((tr, cdim), lambda i, s: (i, 0))],
        out_specs=pl.BlockSpec((None, tr, cdim), lambda i, s: (s[0], i, 0)))
    return _pcall(body, name=name, out_shape=jax.ShapeDtypeStruct((N_CHIPS, r, cdim), BF16), grid_spec=spec,
                  compiler_params=_params(("parallel",)))(chip, w)


class Comm:
    def __init__(self, ins, outs, aliases, sems, first, mid, last):
        self.ins, self.outs, self.aliases, self.sems = list(ins), list(outs), dict(aliases), list(sems)
        self.first, self.mid, self.last = first, mid, last


def run_comm(name, comm):
    ni, no = len(comm.ins), len(comm.outs)

    def body(*refs):
        ins, outs, sems = refs[:ni], refs[ni:ni + no], refs[ni + no:]
        comm.first(ins, outs, sems)
        comm.mid(ins, outs, sems)
        comm.last(ins, outs, sems)

    return _pcall(body, name=name, out_shape=comm.outs, in_specs=[ANY] * ni, out_specs=[ANY] * no,
                  input_output_aliases=comm.aliases, scratch_shapes=comm.sems, compiler_params=_params())(*comm.ins)


def gather_comm(bufs):
    n = len(bufs)

    def half(out, w, which):
        pr = out[w].shape[1] // 2
        return pl.ds(pl.multiple_of(which * pr, BF16_ROWS), pr)

    def first(ins, out, sems):
        isend, irecv, _, _ = sems
        x, y, c, chips = _place()
        for w in range(n):
            mine = out[w].at[2 * x + y, half(out, w, c)]
            for j, (cx, cy) in enumerate(chips):
                _remote(mine, mine, isend.at[3 * w + j], irecv.at[3 * w + j], (cx, cy, c)).start()

    def mid(ins, out, sems):
        isend, irecv, dsend, drecv = sems
        x, y, c, chips = _place()
        sib = (x, y, 1 - c)
        for w in range(n):
            for j, (cx, cy) in enumerate(chips):
                landed = out[w].at[2 * cx + cy, half(out, w, c)]
                _remote(landed, landed, isend.at[3 * w + j], irecv.at[3 * w + j], sib).wait_recv()
                _remote(landed, landed, dsend.at[3 * w + j], drecv.at[3 * w + j], sib).start()

    def last(ins, out, sems):
        isend, irecv, dsend, drecv = sems
        x, y, c, chips = _place()
        sib = (x, y, 1 - c)
        for w in range(n):
            for j, (cx, cy) in enumerate(chips):
                landed = out[w].at[2 * cx + cy, half(out, w, 1 - c)]
                _remote(landed, landed, dsend.at[3 * w + j], drecv.at[3 * w + j], sib).wait_recv()
        for w in range(n):
            sent = out[w].at[0, half(out, w, c)]
            for j in range(3):
                _remote(sent, sent, isend.at[3 * w + j], irecv.at[3 * w + j], sib).wait_send()
                _remote(sent, sent, dsend.at[3 * w + j], drecv.at[3 * w + j], sib).wait_send()

    return Comm(bufs, [jax.ShapeDtypeStruct(s.shape, s.dtype) for s in bufs], {w: w for w in range(n)},
                [pltpu.SemaphoreType.DMA((3 * n,))] * 4, first, mid, last)


def _nothing(ins, outs, sems):
    return None


def exchange_comm(pieces):
    n = len(pieces)

    def copies(src, out, sems):
        x, y, c, _ = _place()
        return [_remote(src[w].at[k, 1 - c], out[w].at[k], sems[0].at[N_CHIPS * w + k], sems[1].at[N_CHIPS * w + k],
                        (x, y, 1 - c)) for w in range(n) for k in range(N_CHIPS)]

    def first(src, out, sems):
        for cp in copies(src, out, sems):
            cp.start()

    def last(src, out, sems):
        for cp in copies(src, out, sems):
            cp.wait()

    return Comm(pieces, [jax.ShapeDtypeStruct((N_CHIPS,) + s.shape[2:], s.dtype) for s in pieces], {},
                [pltpu.SemaphoreType.DMA((N_CHIPS * n,))] * 2, first, _nothing, last)


def scatter_comm(parts):
    n = len(parts)

    def copies(src, out, sems):
        x, y, c, chips = _place()
        return [_remote(src[w].at[2 * cx + cy], out[w].at[j], sems[0].at[3 * w + j], sems[1].at[3 * w + j], (cx, cy, c))
                for w in range(n) for j, (cx, cy) in enumerate(chips)]

    def first(src, out, sems):
        for cp in copies(src, out, sems):
            cp.start()

    def last(src, out, sems):
        for cp in copies(src, out, sems):
            cp.wait()

    return Comm(parts, [jax.ShapeDtypeStruct((3,) + s.shape[1:], s.dtype) for s in parts], {},
                [pltpu.SemaphoreType.DMA((3 * n,))] * 2, first, _nothing, last)


def share_comm(halves):
    n = len(halves)

    def first(ins, buf, sems):
        x, y, c, _ = _place()
        for w in range(n):
            _remote(buf[w].at[c], buf[w].at[c], sems[0].at[w], sems[1].at[w], (x, y, 1 - c)).start()

    def last(ins, buf, sems):
        x, y, c, _ = _place()
        for w in range(n):
            landed = buf[w].at[1 - c]
            _remote(landed, landed, sems[0].at[w], sems[1].at[w], (x, y, 1 - c)).wait_recv()
        for w in range(n):
            _remote(buf[w].at[c], buf[w].at[c], sems[0].at[w], sems[1].at[w], (x, y, 1 - c)).wait_send()

    return Comm(halves, [jax.ShapeDtypeStruct(s.shape, s.dtype) for s in halves], {w: w for w in range(n)},
                [pltpu.SemaphoreType.DMA((n,))] * 2, first, _nothing, last)


def gather_small(name, blk, reduce):
    r, cdim = blk.shape

    def body(in_ref, out_ref, *rest):
        if reduce:
            buf, send_sem, recv_sem = rest
        else:
            buf = out_ref
            send_sem, recv_sem = rest
        x, y, c, _ = _place()
        me = 4 * x + 2 * y + c
        buf[me] = in_ref[...]
        peers = []
        for dx in range(2):
            for dy in range(2):
                for dc in range(2):
                    if dx or dy or dc:
                        peers.append((dx, dy, dc))
        copies = []
        for s, (dx, dy, dc) in enumerate(peers):
            cp = _remote(in_ref, buf.at[me], send_sem.at[s], recv_sem.at[s],
                         ((1 - x if dx else x), (1 - y if dy else y), (1 - c if dc else c)))
            cp.start()
            copies.append(cp)
        for s, (dx, dy, dc) in enumerate(peers):
            px, py, pc_ = (1 - x if dx else x), (1 - y if dy else y), (1 - c if dc else c)
            landed = buf.at[4 * px + 2 * py + pc_]
            _remote(landed, landed, send_sem.at[s], recv_sem.at[s], (x, y, c)).wait_recv()
        for cp in copies:
            cp.wait_send()
        if reduce:
            tot = buf[0]
            for s in range(1, N_DEV):
                tot = tot + buf[s]
            out_ref[...] = tot

    vm = pl.BlockSpec(memory_space=pltpu.VMEM)
    out_shape = jax.ShapeDtypeStruct((r, cdim) if reduce else (N_DEV, r, cdim), F32)
    scratch = ([pltpu.VMEM((N_DEV, r, cdim), F32)] if reduce else []) + [pltpu.SemaphoreType.DMA((N_DEV - 1,))] * 2
    return _pcall(body, name=name, out_shape=out_shape, in_specs=[vm], out_specs=vm, scratch_shapes=scratch,
                  compiler_params=_params())(blk)


def sum_cores(name, own, got, place):
    _, _, pr, pc = own.shape
    tr = _tile(pr, max(BF16_ROWS, (1 << 19) // pc), BF16_ROWS)

    def body(place_ref, own_ref, got_ref, o_ref):
        o_ref[...] = (own_ref[...].astype(F32) + got_ref[...].astype(F32)).astype(o_ref.dtype)

    spec = pltpu.PrefetchScalarGridSpec(
        num_scalar_prefetch=1, grid=(N_CHIPS, pr // tr),
        in_specs=[pl.BlockSpec((None, None, tr, pc), lambda k, i, s: (k, s[1], i, 0)),
                  pl.BlockSpec((None, tr, pc), lambda k, i, s: (k, i, 0))],
        out_specs=pl.BlockSpec((None, tr, pc), lambda k, i, s: (k, i, 0)))
    return _pcall(body, name=name, out_shape=jax.ShapeDtypeStruct((N_CHIPS, pr, pc), BF16), grid_spec=spec,
                  compiler_params=_params(("parallel", "parallel")))(place, own, got)


def sum_chips(name, part, got, place):
    _, pr, pc = part.shape
    tr = _tile(pr, max(BF16_ROWS, (1 << 18) // pc), BF16_ROWS)

    def body(place_ref, part_ref, got_ref, o_ref):
        tot = part_ref[...].astype(F32)
        for j in range(3):
            tot = tot + got_ref[j].astype(F32)
        o_ref[...] = tot

    spec = pltpu.PrefetchScalarGridSpec(
        num_scalar_prefetch=1, grid=(pr // tr,),
        in_specs=[pl.BlockSpec((None, tr, pc), lambda i, s: (s[0], i, 0)),
                  pl.BlockSpec((3, tr, pc), lambda i, s: (0, i, 0))],
        out_specs=pl.BlockSpec((None, tr, pc), lambda i, s: (s[1], i, 0)))
    return _pcall(body, name=name, out_shape=jax.ShapeDtypeStruct((2, pr, pc), F32), grid_spec=spec,
                  compiler_params=_params(("parallel",)))(place, part, got)


def adamw(name, w, g, m, v):
    rows, cols = w.shape
    tr = _tile(rows, max(8, (1 << 18) // cols))
    c1 = 1.0 / (1.0 - ADAM_B1 ** ADAM_STEP)
    c2 = 1.0 / (1.0 - ADAM_B2 ** ADAM_STEP)

    def body(w_ref, g_ref, m_ref, v_ref, d_ref, nm_ref, nv_ref):
        gv = g_ref[...]
        nm = ADAM_B1 * m_ref[...] + (1.0 - ADAM_B1) * gv
        nv = ADAM_B2 * v_ref[...] + (1.0 - ADAM_B2) * (gv * gv)
        nm_ref[...] = nm
        nv_ref[...] = nv
        d_ref[...] = -ADAM_LR * ((nm * c1) / (jnp.sqrt(nv * c2) + ADAM_EPS) + ADAM_WD * w_ref[...])

    spec = pl.BlockSpec((tr, cols), lambda i: (i, 0))
    sds = jax.ShapeDtypeStruct((rows, cols), F32)
    return _pcall(body, name=name, out_shape=(sds, sds, sds), grid=(rows // tr,),
                  in_specs=[spec] * 4, out_specs=(spec, spec, spec),
                  compiler_params=_params(("parallel",)))(w, g, m, v)


MATS = ["ffn1_w_in", "ffn1_w_out", "w_mix_in", "w_conv_out", "w_attn_out", "w_mix_out", "ffn2_w_in", "ffn2_w_out",
        "w_ple_gate", "w_ple_proj"]
COL_SHARDED = {"ffn1_w_in", "w_mix_in", "ffn2_w_in", "w_ple_proj"}
NORMS = ["ffn1_norm", "mix_norm", "ffn2_norm", "ple_norm", "final_norm"]
WEIGHTS = ["ffn1_norm", "ffn1_w_in", "ffn1_w_out", "mix_norm", "w_mix_in", "conv_w", "w_conv_out", "w_attn_out",
           "w_mix_out", "ffn2_norm", "ffn2_w_in", "ffn2_w_out", "ple_norm", "w_ple_gate", "w_ple_proj", "final_norm"]


def _pad_rows(a, rows):
    return jnp.concatenate([a, jnp.zeros((rows - a.shape[0],) + a.shape[1:], a.dtype)], axis=0)


def _step(x, p, tgt, w, m, v):
    t, d = x.shape
    tt = _tile(t, 256)
    tm = _tile(t, 512)
    tm2 = _tile(t, 1024)
    tq = _tile(t, 256)

    chip = 2 * lax.axis_index("x") + lax.axis_index("y")
    place = jnp.stack([chip, lax.axis_index("c")]).astype(jnp.int32)

    placed = {k: place_shard("place_" + k, w[k], place) for k in MATS}
    full = {}

    def keep(names, bufs):
        for k, buf in zip(names, bufs):
            full[k] = buf if k in COL_SHARDED else buf.reshape(-1, buf.shape[2])

    def gather_of(names):
        return gather_comm([placed[k] for k in names])

    cw_all = gather_small("gather_conv_w", _pad_rows(w["conv_w"], 8), False)
    cw8 = jnp.concatenate([cw_all[2 * k] for k in range(N_CHIPS)], axis=1)
    g1, gm, g2, gp, gf = (w[k].reshape(1, d) for k in NORMS)

    def ffn_fwd(tag, h, g, first, w_in_name, w_out_name, riders):
        if first:
            n, bufs = rms_fwd(tag + "_norm", h, g, tt, comm=gather_of(first))
            keep(first, bufs)
        else:
            n = rms_fwd(tag + "_norm", h, g, tt)
        w_in = full[w_in_name]
        if riders:
            (a, s), bufs = ffn_in_act(tag + "_in", n, w_in, tm, comm=gather_of(riders))
            keep(riders, bufs)
        else:
            a, s = ffn_in_act(tag + "_in", n, w_in, tm)
        return n, a, s, mm_nn(tag + "_out", s, full[w_out_name], F32, tm, res=h, alpha=0.5)

    n1, a1, s1, h1 = ffn_fwd("ffn1", x, g1, ["ffn1_w_in"], "ffn1_w_in", "ffn1_w_out", ["ffn1_w_out", "w_mix_in"])
    u = rms_fwd("mix_norm", h1, gm, tt)
    wmix = full["w_mix_in"]
    riders = [["w_conv_out", "w_attn_out", "w_mix_out"], ["ffn2_w_in"], ["ffn2_w_out", "w_ple_gate", "w_ple_proj"]]
    cbx, bufs = mm_nn_stacked("mix_in_conv", u, wmix, F32, tm2, d, 0, 3, comm=gather_of(riders[0]))
    keep(riders[0], bufs)
    qkv, bufs = mm_nn_stacked("mix_in_qkv", u, wmix, BF16, tm2, d, 3, 3, comm=gather_of(riders[1]))
    keep(riders[1], bufs)
    gates, bufs = mm_nn_stacked("mix_in_gates", u, wmix, BF16, tm2, d, 6, 2, comm=gather_of(riders[2]))
    keep(riders[2], bufs)
    wpp = full["w_ple_proj"]
    wpp = jnp.transpose(wpp, (1, 0, 2)).reshape(wpp.shape[1], -1)
    ycin = conv_fwd("conv", cbx, cw8, tt)
    y_conv = mm_nn("conv_out", ycin, full["w_conv_out"], BF16, tm)
    o = attn_fwd("attn", qkv, tq)
    y_attn = mm_nn("attn_out", o, full["w_attn_out"], BF16, tm)
    merged = gate_fwd("merge", gates, y_conv, y_attn, tt)
    h2 = mm_nn("mix_out", merged, full["w_mix_out"], F32, tm, res=h1, alpha=1.0)
    n2, a2, s2, h3 = ffn_fwd("ffn2", h2, g2, [], "ffn2_w_in", "ffn2_w_out", [])
    npl = rms_fwd("ple_norm", h3, gp, tt)
    zg = mm_nn("ple_gate", npl, full["w_ple_gate"], F32, tm)
    pp = mm_nn("ple_proj", p, wpp, F32, tm)

    pieces, chip_sums, halves = {}, {}, {}

    def as_pieces(k):
        pc = pieces[k]
        return pc if k in COL_SHARDED else pc.reshape(N_CHIPS, 2, pc.shape[0] // (2 * N_CHIPS), pc.shape[1])

    def sum_siblings(tag, names):
        pcs = [as_pieces(k) for k in names]
        got = run_comm("exchange_" + tag, exchange_comm(pcs))
        for k, a, b in zip(names, pcs, got):
            chip_sums[k] = sum_cores("sum_cores_" + k, a, b, place)

    def scatter_of(names):
        return scatter_comm([chip_sums[k] for k in names])

    def sum_landed(names, landed):
        for k, b in zip(names, landed):
            halves[k] = sum_chips("sum_chips_" + k, chip_sums[k], b, place)

    dh4, dpp, dzg, dgf, loss_row = tail("tail", h3, zg, pp, tgt, gf, tt)
    dwpp = mm_tn_whole("ple_proj_dw", p, dpp, tm2)
    pieces["w_ple_proj"] = jnp.transpose(dwpp.reshape(2, p.shape[1] // 2, N_CHIPS, d // N_CHIPS), (2, 0, 1, 3))
    pieces["w_ple_gate"] = mm_tn_rows("ple_gate_dw", npl, dzg, tm2)
    dh3, df2, dgp = mm_nt("ple_gate_dx", dzg, full["w_ple_gate"], F32, tm, d, norm=(h3, gp, dh4), alpha=0.5)
    w_in, w_out = full["ffn2_w_in"], full["ffn2_w_out"]
    pieces["ffn2_w_out"] = mm_tn_rows("ffn2_dwout", s2, df2, tm2)
    da2 = ffn_ds_dact("ffn2_ds", df2, w_out, a2, tm)
    pieces["ffn2_w_in"] = mm_tn_cols("ffn2_dwin", n2, da2, tm2)
    dh2, dh2b, dg2 = mm_nt_stacked("ffn2_dn", da2, w_in, F32, tm, w_in.shape[2], norm=(h2, g2, dh3))
    pieces["w_mix_out"] = mm_tn_rows("mix_out_dw", merged, dh2b, tm2)
    dmerged = mm_nt("mix_out_dx", dh2b, full["w_mix_out"], F32, tm, d)
    dyc, dya, dgates = gate_bwd("merge_bwd", dmerged, gates, y_conv, y_attn, tt)
    pieces["w_conv_out"] = mm_tn_rows("conv_out_dw", ycin, dyc, tm2)
    dycin = mm_nt("conv_out_dx", dyc, full["w_conv_out"], F32, tm, d)
    dcbx, dcw8 = conv_bwd("conv_bwd", dycin, cbx, cw8, tt)
    pieces["w_attn_out"] = mm_tn_rows("attn_out_dw", o, dya, tm2)
    do = mm_nt("attn_out_dx", dya, full["w_attn_out"], BF16, tm, d)
    dq, dk, dv = attn_bwd("attn_bwd", qkv, do, tq)
    dmix = [dcbx, dq, dk, dv, dgates]
    early = ["ffn2_w_in", "ffn2_w_out", "w_ple_gate", "w_ple_proj", "w_mix_out", "w_conv_out", "w_attn_out"]
    swap = exchange_comm([as_pieces(k) for k in early])
    pieces["w_mix_in"], got = mm_tn_parts("mix_in_dw", u, dmix, tm2, comm=swap)
    for k, a, b in zip(early, swap.ins, got):
        chip_sums[k] = sum_cores("sum_cores_" + k, a, b, place)
    (dh1, df1, dgm), landed = mm_nt_parts("mix_in_dx", dmix, wmix, tm, (h1, gm, dh2), 0.5, comm=scatter_of(early))
    sum_landed(early, landed)
    sum_siblings("mix", ["w_mix_in"])
    w_in, w_out = full["ffn1_w_in"], full["ffn1_w_out"]
    pieces["ffn1_w_out"] = mm_tn_rows("ffn1_dwout", s1, df1, tm2)
    da1 = ffn_ds_dact("ffn1_ds", df1, w_out, a1, tm)
    pieces["ffn1_w_in"], landed = mm_tn_cols("ffn1_dwin", n1, da1, tm2, comm=scatter_of(["w_mix_in"]))
    sum_landed(["w_mix_in"], landed)
    late = ["ffn1_w_in", "ffn1_w_out"]
    sum_siblings("late", late)
    (dx, _, dg1), landed = mm_nt_stacked("ffn1_dn", da1, w_in, F32, tm, w_in.shape[2], comm=scatter_of(late),
                                         norm=(x, g1, dh1))
    sum_landed(late, landed)

    shared = run_comm("share_halves", share_comm([halves[k] for k in MATS]))
    grad, delta, new_m, new_v = {}, {}, {}, {}
    for k, sh in zip(MATS, shared):
        grad[k] = sh.reshape(w[k].shape)
        delta[k], new_m[k], new_v[k] = adamw("adamw_" + k, w[k], grad[k], m[k], v[k])

    small = jnp.concatenate([dg1, dgm, dg2, dgp, dgf, dcw8[:3], loss_row, jnp.zeros((7, d), F32)], axis=0)
    tot = gather_small("sum_small", small, True)
    loss = tot[8, 0]
    norm_w = jnp.concatenate([w[k].reshape(1, d) for k in NORMS] + [jnp.zeros((3, d), F32)], axis=0)
    norm_m = jnp.concatenate([m[k].reshape(1, d) for k in NORMS] + [jnp.zeros((3, d), F32)], axis=0)
    norm_v = jnp.concatenate([v[k].reshape(1, d) for k in NORMS] + [jnp.ones((3, d), F32)], axis=0)
    norm_g = jnp.concatenate([tot[0:5], jnp.zeros((3, d), F32)], axis=0)
    nd, nm, nv = adamw("adamw_norms", norm_w, norm_g, norm_m, norm_v)
    for r, k in enumerate(NORMS):
        grad[k] = norm_g[r].reshape(w[k].shape)
        delta[k], new_m[k], new_v[k] = (a[r].reshape(w[k].shape) for a in (nd, nm, nv))
    cs = d // N_CHIPS
    gcw = lax.dynamic_slice(tot[5:8], (0, chip * cs), (3, cs))
    cd, cm, cv = adamw("adamw_conv_w", _pad_rows(w["conv_w"], 8), _pad_rows(gcw, 8), _pad_rows(m["conv_w"], 8),
                       jnp.concatenate([v["conv_w"], jnp.ones((5, cs), F32)], axis=0))
    grad["conv_w"], delta["conv_w"], new_m["conv_w"], new_v["conv_w"] = gcw, cd[:3], cm[:3], cv[:3]
    return loss, dx, grad, delta, new_m, new_v


def kernel(x, p, ffn1_norm, ffn1_w_in, ffn1_w_out, mix_norm, w_mix_in, conv_w, w_conv_out, w_attn_out, w_mix_out, ffn2_norm, ffn2_w_in, ffn2_w_out, ple_norm, w_ple_gate, w_ple_proj, final_norm, loss_target, m_ffn1_norm, m_ffn1_w_in, m_ffn1_w_out, m_mix_norm, m_w_mix_in, m_conv_w, m_w_conv_out, m_w_attn_out, m_w_mix_out, m_ffn2_norm, m_ffn2_w_in, m_ffn2_w_out, m_ple_norm, m_w_ple_gate, m_w_ple_proj, m_final_norm, v_ffn1_norm, v_ffn1_w_in, v_ffn1_w_out, v_mix_norm, v_w_mix_in, v_conv_w, v_w_conv_out, v_w_attn_out, v_w_mix_out, v_ffn2_norm, v_ffn2_w_in, v_ffn2_w_out, v_ple_norm, v_w_ple_gate, v_w_ple_proj, v_final_norm):
    ws = (ffn1_norm, ffn1_w_in, ffn1_w_out, mix_norm, w_mix_in, conv_w, w_conv_out, w_attn_out, w_mix_out, ffn2_norm,
          ffn2_w_in, ffn2_w_out, ple_norm, w_ple_gate, w_ple_proj, final_norm)
    ms = (m_ffn1_norm, m_ffn1_w_in, m_ffn1_w_out, m_mix_norm, m_w_mix_in, m_conv_w, m_w_conv_out, m_w_attn_out,
          m_w_mix_out, m_ffn2_norm, m_ffn2_w_in, m_ffn2_w_out, m_ple_norm, m_w_ple_gate, m_w_ple_proj, m_final_norm)
    vs = (v_ffn1_norm, v_ffn1_w_in, v_ffn1_w_out, v_mix_norm, v_w_mix_in, v_conv_w, v_w_conv_out, v_w_attn_out,
          v_w_mix_out, v_ffn2_norm, v_ffn2_w_in, v_ffn2_w_out, v_ple_norm, v_w_ple_gate, v_w_ple_proj, v_final_norm)
    assert x.shape[0] == 1 and p.shape[:2] == (1, 1), "one sequence and one layer per device"

    def strip(a):
        return a[0] if a.ndim == 3 or (a.ndim == 2 and a.shape[0] == 1) else a

    w = {k: strip(a) for k, a in zip(WEIGHTS, ws)}
    m = {k: strip(a) for k, a in zip(WEIGHTS, ms)}
    v = {k: strip(a) for k, a in zip(WEIGHTS, vs)}
    loss, dx, grad, delta, new_m, new_v = _step(x[0], p[0, 0], loss_target[0], w, m, v)
    shapes = [a.shape for a in ws]
    outs = [loss, dx[None]]
    for res in (grad, delta, new_m, new_v):
        outs += [res[k].reshape(s) for k, s in zip(WEIGHTS, shapes)]
    return tuple(outs)
```

```python
import functools
import math

import jax
import jax.numpy as jnp
from jax import lax
from jax.experimental import pallas as pl
from jax.experimental.pallas import tpu as pltpu

F32 = jnp.float32
BF16 = jnp.bfloat16
MESH = pl.DeviceIdType.MESH
ANY = pl.BlockSpec(memory_space=pl.ANY)

HEAD_DIM = 128
NORM_EPS = 1e-6
N_CHIPS = 4
N_DEV = 8
BF16_ROWS = 16
VMEM_LIMIT = 56 * 1024 * 1024
ACC_BYTES = 8 * 1024 * 1024
STICK_EXIT = 110.0

ADAM_LR = 0.001
ADAM_B1 = 0.9
ADAM_B2 = 0.999
ADAM_EPS = 1e-08
ADAM_WD = 0.01
ADAM_STEP = 10

NN = (((1,), (0,)), ((), ()))
NT = (((1,), (1,)), ((), ()))
TN = (((0,), (0,)), ((), ()))


def _params(sem=None, **kw):
    if sem is not None:
        kw["dimension_semantics"] = sem
    return pltpu.CompilerParams(vmem_limit_bytes=VMEM_LIMIT, **kw)


def _pcall(body, **kw):
    return pl.pallas_call(body, **kw)


def _tile(n, pref, mult=8):
    best = None
    for d in range(mult, min(n, pref) + 1, mult):
        if n % d == 0:
            best = d
    return best if best is not None else n


def _dot(a, b, dims):
    return lax.dot_general(a, b, dims, preferred_element_type=F32)


def _call(name, body, grid, in_specs, out_specs, out_shape, args, scratch=(), sem=None, comm=None):
    n_in, n_out, n_sc = len(in_specs), len(out_specs), len(scratch)
    if comm is None:
        def plain(*refs):
            body(refs[:n_in], refs[n_in:n_in + n_out], refs[n_in + n_out:])

        return _pcall(plain, name=name, out_shape=list(out_shape), grid=grid, in_specs=list(in_specs),
                      out_specs=list(out_specs), scratch_shapes=list(scratch), compiler_params=_params(sem))(*args)
    n_cin, n_cout = len(comm.ins), len(comm.outs)
    steps = math.prod(grid)

    def hosted(*refs):
        ins, c_ins = refs[:n_in], refs[n_in:n_in + n_cin]
        outs = refs[n_in + n_cin:n_in + n_cin + n_out]
        c_outs = refs[n_in + n_cin + n_out:n_in + n_cin + n_out + n_cout]
        rest = refs[n_in + n_cin + n_out + n_cout:]
        sems = rest[n_sc:]
        step = pl.program_id(0)
        for ax in range(1, len(grid)):
            step = step * grid[ax] + pl.program_id(ax)

        @pl.when(step == 0)
        def _():
            comm.first(c_ins, c_outs, sems)

        body(ins, outs, rest[:n_sc])

        @pl.when(step == (3 * steps) // 4)
        def _():
            comm.mid(c_ins, c_outs, sems)

        @pl.when(step == steps - 1)
        def _():
            comm.last(c_ins, c_outs, sems)

    res = _pcall(hosted, name=name, out_shape=list(out_shape) + comm.outs, grid=grid,
                 in_specs=list(in_specs) + [ANY] * n_cin, out_specs=list(out_specs) + [ANY] * n_cout,
                 input_output_aliases={n_in + k: n_out + v for k, v in comm.aliases.items()},
                 scratch_shapes=list(scratch) + comm.sems,
                 compiler_params=_params(("arbitrary",) * len(grid)))(*args, *comm.ins)
    return list(res[:n_out]), list(res[n_out:])


NORM_CHUNK = 256


def _norm_bwd_tile(read_dn, rows, first, h_ref, g_ref, dr_ref, dh_ref, dhb_ref, dg_ref, alpha):
    @pl.when(first)
    def _():
        dg_ref[...] = jnp.zeros_like(dg_ref)

    gv = g_ref[...]
    tot = jnp.zeros_like(gv)
    for c0 in range(0, rows, NORM_CHUNK):
        sl = slice(c0, min(rows, c0 + NORM_CHUNK))
        hv = h_ref[sl, :]
        rs = _rstd(hv)
        hn = hv * rs
        dnv = read_dn(sl)
        gy = dnv * gv
        dh = dr_ref[sl, :] + rs * (gy - hn * jnp.mean(gy * hn, axis=-1, keepdims=True))
        dh_ref[sl, :] = dh
        dhb_ref[sl, :] = (alpha * dh).astype(BF16)
        tot = tot + jnp.sum(dnv * hn, axis=0, keepdims=True)
    dg_ref[...] += tot


def _mm(name, a, b, out_sds, grid, a_spec, b_spec, o_spec, dims, acc_shape, res=None, alpha=1.0, comm=None,
        norm=None):
    nk = grid[2]

    def body(ins, outs, scratch):
        a_ref, b_ref = ins[:2]
        r_ref = ins[2] if res is not None else None
        o_ref = outs[0]

        def finish(read):
            if norm is not None:
                first = jnp.logical_and(pl.program_id(0) == 0, pl.program_id(1) == 0)
                _norm_bwd_tile(read, o_ref.shape[0], first, *ins[2:5], *outs, alpha)
                return
            r = read(slice(None))
            if alpha != 1.0:
                r = r * alpha
            if r_ref is not None:
                r = r_ref[...] + r
            if len(o_ref.shape) == 3:
                half = o_ref.shape[1]
                o_ref[0] = r[:half].astype(o_ref.dtype)
                o_ref[1] = r[half:].astype(o_ref.dtype)
            else:
                o_ref[...] = r.astype(o_ref.dtype)

        if nk == 1:
            part = _dot(a_ref[...].astype(BF16), b_ref[...].astype(BF16), dims)
            finish(lambda sl: part[sl])
        else:
            acc_ref = scratch[0]
            kk = pl.program_id(2)

            @pl.when(kk == 0)
            def _():
                acc_ref[...] = jnp.zeros_like(acc_ref)

            acc_ref[...] += _dot(a_ref[...].astype(BF16), b_ref[...].astype(BF16), dims)

            @pl.when(kk == nk - 1)
            def _():
                finish(lambda sl: acc_ref[sl, :])

    in_specs = [a_spec, b_spec]
    args = [a, b]
    out_specs, out_shape = [o_spec], [out_sds]
    sem = ("parallel", "parallel", "arbitrary")
    if res is not None:
        in_specs.append(o_spec)
        args.append(res)
    if norm is not None:
        width = out_sds.shape[1]
        whole = pl.BlockSpec((1, width), lambda i, j, r: (0, 0))
        in_specs += [o_spec, whole, o_spec]
        args += list(norm)
        out_specs = [o_spec, o_spec, whole]
        out_shape = [jax.ShapeDtypeStruct(out_sds.shape, F32), jax.ShapeDtypeStruct(out_sds.shape, BF16),
                     jax.ShapeDtypeStruct((1, width), F32)]
        sem = ("arbitrary", "arbitrary", "arbitrary")
    scratch = [] if nk == 1 else [pltpu.VMEM(acc_shape, F32)]
    got = _call(name, body, grid, in_specs, out_specs, out_shape, args, scratch, sem, comm)
    if norm is not None:
        return got if comm is None else (got[0], got[1])
    return got[0] if comm is None else (got[0][0], got[1])


def ffn_in_act(name, n, w4, tm, comm=None):
    t, d = n.shape
    cs = w4.shape[2]

    def body(ins, outs, scratch):
        n_ref, wg_ref, wu_ref = ins
        a_ref, s_ref = outs
        nv = n_ref[...]
        gate = _dot(nv, wg_ref[...], NN)
        up = _dot(nv, wu_ref[...], NN)
        a_ref[0] = gate.astype(BF16)
        a_ref[1] = up.astype(BF16)
        s_ref[...] = (gate * jax.nn.sigmoid(gate) * up).astype(BF16)

    got = _call(name, body, (t // tm, 2),
                [pl.BlockSpec((tm, d), lambda i, j: (i, 0)),
                 pl.BlockSpec((None, d, cs), lambda i, j: (j, 0, 0)),
                 pl.BlockSpec((None, d, cs), lambda i, j: (2 + j, 0, 0))],
                [pl.BlockSpec((2, tm, cs), lambda i, j: (0, i, j)), pl.BlockSpec((tm, cs), lambda i, j: (i, j))],
                [jax.ShapeDtypeStruct((2, t, 2 * cs), BF16), jax.ShapeDtypeStruct((t, 2 * cs), BF16)],
                [n, w4, w4], (), ("parallel", "parallel"), comm)
    return got if comm is None else (got[0], got[1])


def ffn_ds_dact(name, df, w_out, a3, tm):
    t, d = df.shape
    f = w_out.shape[0]
    cs = f // 2

    def body(ins, outs, scratch):
        df_ref, w_ref, a_ref = ins
        ds = _dot(df_ref[...], w_ref[...], NT)
        gate = a_ref[0].astype(F32)
        up = a_ref[1].astype(F32)
        sg = jax.nn.sigmoid(gate)
        outs[0][0] = (ds * up * sg * (1.0 + gate * (1.0 - sg))).astype(BF16)
        outs[0][1] = (ds * gate * sg).astype(BF16)

    blk = pl.BlockSpec((2, tm, cs), lambda i, j: (0, i, j))
    return _call(name, body, (t // tm, 2),
                 [pl.BlockSpec((tm, d), lambda i, j: (i, 0)), pl.BlockSpec((cs, d), lambda i, j: (j, 0)), blk],
                 [blk], [jax.ShapeDtypeStruct((2, t, f), BF16)], [df, w_out, a3], (), ("parallel", "parallel"))[0]


def _part_ranges(parts, d):
    out, lo = [], 0
    for p in parts:
        out.append((lo, p.shape[1] // d))
        lo += p.shape[1] // d
    return out, lo


def mm_nt_parts(name, parts, w4, tm, norm, alpha, comm=None):
    m = parts[0].shape[0]
    d, cs = w4.shape[1], w4.shape[2]
    per = cs // d
    ranges, nblk = _part_ranges(parts, d)
    np_ = len(parts)

    def body(ins, outs, scratch):
        w_ref, acc = ins[np_], scratch[0]
        r = pl.program_id(1)

        @pl.when(r == 0)
        def _():
            acc[...] = jnp.zeros_like(acc)

        for (lo, n), a_ref in zip(ranges, ins[:np_]):
            @pl.when(jnp.logical_and(r >= lo, r < lo + n))
            def _(a_ref=a_ref):
                acc[...] += _dot(a_ref[...], w_ref[...], NT)

        @pl.when(r == nblk - 1)
        def _():
            _norm_bwd_tile(lambda sl: acc[sl, :], tm, pl.program_id(0) == 0, *ins[np_ + 1:], *outs, alpha)

    rows = pl.BlockSpec((tm, d), lambda i, r: (i, 0))
    whole = pl.BlockSpec((1, d), lambda i, r: (0, 0))
    specs = [pl.BlockSpec((tm, d), lambda i, r, lo=lo, n=n: (i, jnp.clip(r - lo, 0, n - 1))) for lo, n in ranges]
    specs += [pl.BlockSpec((None, d, d), lambda i, r: (r // per, 0, r % per)), rows, whole, rows]
    got = _call(name, body, (m // tm, nblk), specs, [rows, rows, whole],
                [jax.ShapeDtypeStruct((m, d), F32), jax.ShapeDtypeStruct((m, d), BF16),
                 jax.ShapeDtypeStruct((1, d), F32)],
                list(parts) + [w4] + list(norm), [pltpu.VMEM((tm, d), F32)], ("arbitrary", "arbitrary"), comm)
    return got if comm is None else (got[0], got[1])


def mm_tn_parts(name, xa, parts, tt, comm=None):
    t, k = xa.shape
    d = k
    pr = k // 2
    ranges, nblk = _part_ranges(parts, d)
    per = nblk // N_CHIPS

    def body(ins, outs, scratch):
        x_ref, acc = ins[0], scratch[0]
        jb, r = pl.program_id(0), pl.program_id(1)

        @pl.when(r == 0)
        def _():
            acc[...] = jnp.zeros_like(acc)

        for (lo, n), p_ref in zip(ranges, ins[1:]):
            @pl.when(jnp.logical_and(jb >= lo, jb < lo + n))
            def _(p_ref=p_ref):
                acc[...] += _dot(x_ref[...], p_ref[...], TN)

        @pl.when(r == t // tt - 1)
        def _():
            outs[0][0] = acc[:pr].astype(BF16)
            outs[0][1] = acc[pr:].astype(BF16)

    def part_spec(lo, n):
        return pl.BlockSpec((tt, d), lambda jb, r: (jnp.where(jnp.logical_and(jb >= lo, jb < lo + n), r, 0),
                                                    jnp.clip(jb - lo, 0, n - 1)))

    specs = [pl.BlockSpec((tt, k), lambda jb, r: (r, 0))] + [part_spec(lo, n) for lo, n in ranges]
    got = _call(name, body, (nblk, t // tt), specs,
                [pl.BlockSpec((None, 2, pr, d), lambda jb, r: (jb // per, 0, 0, jb % per))],
                [jax.ShapeDtypeStruct((N_CHIPS, 2, pr, per * d), BF16)], [xa] + list(parts),
                [pltpu.VMEM((k, d), F32)], ("parallel", "arbitrary"), comm)
    return got[0] if comm is None else (got[0][0], got[1])


def mm_nn(name, a, w, out_dtype, tm, res=None, alpha=1.0):
    m, k = a.shape
    n = w.shape[1]
    return _mm(name, a, w, jax.ShapeDtypeStruct((m, n), out_dtype), (m // tm, 1, 1),
               pl.BlockSpec((tm, k), lambda i, j, r: (i, 0)),
               pl.BlockSpec((k, n), lambda i, j, r: (0, 0)),
               pl.BlockSpec((tm, n), lambda i, j, r: (i, 0)), NN, None, res=res, alpha=alpha)


def mm_nn_stacked(name, a, w4, out_dtype, tm, tn, j0=0, nj=None, comm=None):
    m, k = a.shape
    cs = w4.shape[2]
    per = cs // tn
    nj = N_CHIPS * per - j0 if nj is None else nj
    return _mm(name, a, w4, jax.ShapeDtypeStruct((m, nj * tn), out_dtype), (m // tm, nj, 1),
               pl.BlockSpec((tm, k), lambda i, j, r: (i, 0)),
               pl.BlockSpec((None, k, tn), lambda i, j, r: ((j + j0) // per, 0, (j + j0) % per)),
               pl.BlockSpec((tm, tn), lambda i, j, r: (i, j)), NN, None, comm=comm)


def mm_nt(name, dy, w, out_dtype, tm, tko, norm=None, alpha=1.0):
    m, n = dy.shape
    k = w.shape[0]
    return _mm(name, dy, w, jax.ShapeDtypeStruct((m, k), out_dtype), (m // tm, k // tko, 1),
               pl.BlockSpec((tm, n), lambda i, j, r: (i, 0)),
               pl.BlockSpec((tko, n), lambda i, j, r: (j, 0)),
               pl.BlockSpec((tm, tko), lambda i, j, r: (i, j)), NT, None, norm=norm, alpha=alpha)


def mm_nt_stacked(name, dy, w4, out_dtype, tm, tn, comm=None, norm=None, alpha=1.0):
    m = dy.shape[-2]
    k, cs = w4.shape[1], w4.shape[2]
    per = cs // tn
    if dy.ndim == 3:
        dy_spec = pl.BlockSpec((None, tm, cs), lambda i, j, r: (r // 2, i, r % 2))
    else:
        dy_spec = pl.BlockSpec((tm, tn), lambda i, j, r: (i, r))
    return _mm(name, dy, w4, jax.ShapeDtypeStruct((m, k), out_dtype), (m // tm, 1, N_CHIPS * per), dy_spec,
               pl.BlockSpec((None, k, tn), lambda i, j, r: (r // per, 0, r % per)),
               pl.BlockSpec((tm, k), lambda i, j, r: (i, 0)), NT, (tm, k), comm=comm, norm=norm, alpha=alpha)


def mm_tn_rows(name, xa, dy, tt):
    t, k = xa.shape
    n = dy.shape[1]
    tkr = k if k * n * 4 <= ACC_BYTES else k // 2
    return _mm(name, xa, dy, jax.ShapeDtypeStruct((k, n), BF16), (k // tkr, 1, t // tt),
               pl.BlockSpec((tt, tkr), lambda i, j, r: (r, i)),
               pl.BlockSpec((tt, n), lambda i, j, r: (r, 0)),
               pl.BlockSpec((tkr, n), lambda i, j, r: (i, 0)), TN, (tkr, n))


def mm_tn_whole(name, xa, dy, tt):
    t, k = xa.shape
    n = dy.shape[1]
    return _mm(name, xa, dy, jax.ShapeDtypeStruct((k, n), BF16), (1, 1, t // tt),
               pl.BlockSpec((tt, k), lambda i, j, r: (r, 0)),
               pl.BlockSpec((tt, n), lambda i, j, r: (r, 0)),
               pl.BlockSpec((k, n), lambda i, j, r: (0, 0)), TN, (k, n))


def mm_tn_cols(name, xa, dy, tt, comm=None):
    t, k = xa.shape
    pr = k // 2
    if dy.ndim == 3:
        cs = dy.shape[2] // 2
        dy_spec = pl.BlockSpec((None, tt, cs), lambda i, j, r: (j // 2, r, j % 2))
    else:
        cs = dy.shape[1] // N_CHIPS
        dy_spec = pl.BlockSpec((tt, cs), lambda i, j, r: (r, j))
    return _mm(name, xa, dy, jax.ShapeDtypeStruct((N_CHIPS, 2, pr, cs), BF16), (1, N_CHIPS, t // tt),
               pl.BlockSpec((tt, k), lambda i, j, r: (r, 0)), dy_spec,
               pl.BlockSpec((None, 2, pr, cs), lambda i, j, r: (j, 0, 0, 0)), TN, (k, cs), comm=comm)


def _rows(tt, w, col=0):
    return pl.BlockSpec((tt, w), lambda i: (i, col))


def _whole(shape):
    return pl.BlockSpec(shape, lambda i: (0,) * len(shape))


def _rstd(h):
    return lax.rsqrt(jnp.mean(h * h, axis=-1, keepdims=True) + NORM_EPS)


def rms_fwd(name, h, g, tt, comm=None):
    t, d = h.shape

    def body(ins, outs, scratch):
        hv = ins[0][...]
        outs[0][...] = (hv * _rstd(hv) * ins[1][...]).astype(BF16)

    got = _call(name, body, (t // tt,), [_rows(tt, d), _whole((1, d))], [_rows(tt, d)],
                [jax.ShapeDtypeStruct((t, d), BF16)], [h, g], (), ("parallel",), comm)
    return got[0] if comm is None else (got[0][0], got[1])


def gate_fwd(name, gates, yc, ya, tt):
    t, d = yc.shape

    def body(g_ref, yc_ref, ya_ref, o_ref):
        o_ref[...] = (jax.nn.sigmoid(g_ref[:, :d].astype(F32)) * yc_ref[...].astype(F32)
                      + jax.nn.sigmoid(g_ref[:, d:].astype(F32)) * ya_ref[...].astype(F32)).astype(o_ref.dtype)

    return _pcall(body, name=name, out_shape=jax.ShapeDtypeStruct((t, d), BF16), grid=(t // tt,),
                  in_specs=[_rows(tt, 2 * d), _rows(tt, d), _rows(tt, d)], out_specs=_rows(tt, d),
                  compiler_params=_params(("parallel",)))(gates, yc, ya)


def gate_bwd(name, dm, gates, yc, ya, tt):
    t, d = yc.shape

    def body(dm_ref, g_ref, yc_ref, ya_ref, dyc_ref, dya_ref, dg_ref):
        dmv = dm_ref[...]
        sc = jax.nn.sigmoid(g_ref[:, :d].astype(F32))
        sa = jax.nn.sigmoid(g_ref[:, d:].astype(F32))
        dyc_ref[...] = (dmv * sc).astype(BF16)
        dya_ref[...] = (dmv * sa).astype(BF16)
        dg_ref[:, :d] = (dmv * yc_ref[...].astype(F32) * sc * (1.0 - sc)).astype(BF16)
        dg_ref[:, d:] = (dmv * ya_ref[...].astype(F32) * sa * (1.0 - sa)).astype(BF16)

    return _pcall(body, name=name,
                  out_shape=(jax.ShapeDtypeStruct((t, d), BF16), jax.ShapeDtypeStruct((t, d), BF16),
                             jax.ShapeDtypeStruct((t, 2 * d), BF16)),
                  grid=(t // tt,),
                  in_specs=[_rows(tt, d), _rows(tt, 2 * d), _rows(tt, d), _rows(tt, d)],
                  out_specs=(_rows(tt, d), _rows(tt, d), _rows(tt, 2 * d)),
                  compiler_params=_params(("parallel",)))(dm, gates, yc, ya)


def _shift_down(cur, prev8, s):
    tt = cur.shape[0]
    rolled = pltpu.roll(cur, s, 0)
    row8 = lax.broadcasted_iota(jnp.int32, prev8.shape, 0)
    first8 = jnp.where(row8 < s, pltpu.roll(prev8, s, 0), rolled[:8])
    return jnp.concatenate([first8, rolled[8:]], axis=0) if tt > 8 else first8


def _shift_up(cur, next8, s):
    tt = cur.shape[0]
    rolled = pltpu.roll(cur, tt - s, 0)
    row8 = lax.broadcasted_iota(jnp.int32, next8.shape, 0)
    last8 = jnp.where(row8 >= 8 - s, pltpu.roll(next8, 8 - s, 0), rolled[tt - 8:])
    return jnp.concatenate([rolled[:tt - 8], last8], axis=0) if tt > 8 else last8


def _prev8(tt, d, col):
    return pl.BlockSpec((8, d), lambda i: (jnp.maximum(i * (tt // 8) - 1, 0), col))


def _next8(tt, d, col, t):
    return pl.BlockSpec((8, d), lambda i: (jnp.minimum((i + 1) * (tt // 8), t // 8 - 1), col))


def conv_fwd(name, cbx, cw8, tt):
    t, d3 = cbx.shape
    d = d3 // 3

    def body(cb_ref, cc_ref, cx_ref, pc_ref, px_ref, w_ref, o_ref):
        has_prev = (pl.program_id(0) > 0).astype(F32)
        cc = cc_ref[...] * cx_ref[...]
        prev = pc_ref[...] * px_ref[...] * has_prev
        w = w_ref[...]
        conv = w[0:1] * _shift_down(cc, prev, 2) + w[1:2] * _shift_down(cc, prev, 1) + w[2:3] * cc
        o_ref[...] = (cb_ref[...] * conv).astype(o_ref.dtype)

    return _pcall(body, name=name, out_shape=jax.ShapeDtypeStruct((t, d), BF16), grid=(t // tt,),
                  in_specs=[_rows(tt, d, 0), _rows(tt, d, 1), _rows(tt, d, 2), _prev8(tt, d, 1), _prev8(tt, d, 2),
                            _whole((8, d))],
                  out_specs=_rows(tt, d), compiler_params=_params(("parallel",)))(cbx, cbx, cbx, cbx, cbx, cw8)


def conv_bwd(name, dyc, cbx, cw8, tt):
    t, d3 = cbx.shape
    d = d3 // 3
    n = t // tt

    def body(dy_ref, cb_ref, cc_ref, cx_ref, pc_ref, px_ref, ndy_ref, ncb_ref, w_ref, o_ref, dw_ref):
        i = pl.program_id(0)
        has_prev = (i > 0).astype(F32)
        has_next = (i < n - 1).astype(F32)
        cb = cb_ref[...]
        cc = cc_ref[...] * cx_ref[...]
        prev = pc_ref[...] * px_ref[...] * has_prev
        w = w_ref[...]
        cc1 = _shift_down(cc, prev, 1)
        cc2 = _shift_down(cc, prev, 2)
        conv = w[0:1] * cc2 + w[1:2] * cc1 + w[2:3] * cc
        dyv = dy_ref[...]
        dconv = dyv * cb
        dnext = ndy_ref[...] * ncb_ref[...] * has_next
        dcc = w[2:3] * dconv + w[1:2] * _shift_up(dconv, dnext, 1) + w[0:1] * _shift_up(dconv, dnext, 2)
        o_ref[:, :d] = (dyv * conv).astype(BF16)
        o_ref[:, d:2 * d] = (dcc * cx_ref[...]).astype(BF16)
        o_ref[:, 2 * d:] = (dcc * cc_ref[...]).astype(BF16)

        @pl.when(i == 0)
        def _():
            dw_ref[...] = jnp.zeros_like(dw_ref)

        dw_ref[0:1, :] += jnp.sum(dconv * cc2, axis=0, keepdims=True)
        dw_ref[1:2, :] += jnp.sum(dconv * cc1, axis=0, keepdims=True)
        dw_ref[2:3, :] += jnp.sum(dconv * cc, axis=0, keepdims=True)

    return _pcall(body, name=name,
                  out_shape=(jax.ShapeDtypeStruct((t, d3), BF16), jax.ShapeDtypeStruct((8, d), F32)),
                  grid=(n,),
                  in_specs=[_rows(tt, d), _rows(tt, d, 0), _rows(tt, d, 1), _rows(tt, d, 2),
                            _prev8(tt, d, 1), _prev8(tt, d, 2), _next8(tt, d, 0, t), _next8(tt, d, 0, t),
                            _whole((8, d))],
                  out_specs=(_rows(tt, d3), _whole((8, d))),
                  compiler_params=_params(("arbitrary",)))(dyc, cbx, cbx, cbx, cbx, cbx, dyc, cbx, cw8)


def tail(name, h3, zg, pp, tgt, gf, tt):
    t, d = h3.shape

    def body(h_ref, zg_ref, pp_ref, tg_ref, gf_ref, dh_ref, dpp_ref, dzg_ref, dgf_ref, loss_ref):
        pg = jax.nn.sigmoid(zg_ref[...])
        ppv = pp_ref[...]
        h4 = h_ref[...] + pg * ppv
        r4 = _rstd(h4)
        hn = h4 * r4
        gfv = gf_ref[...]
        err = hn * gfv - tg_ref[...]
        dy = err * (1.0 / d)
        gy = dy * gfv
        dh4 = r4 * (gy - hn * jnp.mean(gy * hn, axis=-1, keepdims=True))
        dh_ref[...] = dh4
        dpp_ref[...] = (dh4 * pg).astype(BF16)
        dzg_ref[...] = (dh4 * ppv * pg * (1.0 - pg)).astype(BF16)

        @pl.when(pl.program_id(0) == 0)
        def _():
            dgf_ref[...] = jnp.zeros_like(dgf_ref)
            loss_ref[...] = jnp.zeros_like(loss_ref)

        dgf_ref[...] += jnp.sum(dy * hn, axis=0, keepdims=True)
        tok = jnp.mean(err * err, axis=-1, keepdims=True)
        loss_ref[...] += 0.5 * jnp.sum(tok, axis=0, keepdims=True) * jnp.ones((1, loss_ref.shape[1]), F32)

    return _pcall(body, name=name,
                  out_shape=(jax.ShapeDtypeStruct((t, d), F32), jax.ShapeDtypeStruct((t, d), BF16),
                             jax.ShapeDtypeStruct((t, d), BF16), jax.ShapeDtypeStruct((1, d), F32),
                             jax.ShapeDtypeStruct((1, d), F32)),
                  grid=(t // tt,),
                  in_specs=[_rows(tt, d)] * 4 + [_whole((1, d))],
                  out_specs=(_rows(tt, d), _rows(tt, d), _rows(tt, d), _whole((1, d)), _whole((1, d))),
                  compiler_params=_params(("arbitrary",)))(h3, zg, pp, tgt, gf)


SCALE = 1.0 / math.sqrt(HEAD_DIM)


def _log_stick(z):
    return -(jnp.maximum(z, 0.0) + jnp.log(1.0 + jnp.exp(-jnp.abs(z))))


def _tri_sum(x, tri):
    hi = x.astype(BF16)
    lo = (x - hi.astype(F32)).astype(BF16)
    return _dot(hi, tri, NN) + _dot(lo, tri, NN)


KEY_BLOCK = 128
NEAR = 3


def _sb_near(qs, jds, k_ref, below, upper):
    pairs = [(s, b) for s in range(len(qs)) for b in range(NEAR)]
    rows = {(s, b): _block_rows(jnp.maximum(jds[s] - b, 0), KEY_BLOCK) for s, b in pairs}
    z = {(s, b): _dot(qs[s], k_ref[rows[s, b], :], NT) * SCALE for s, b in pairs}
    lg = {(s, b): jnp.where(below, _log_stick(z[s, b]), 0.0) if b == 0 else _log_stick(z[s, b]) for s, b in pairs}
    cum = {(s, b): _tri_sum(lg[s, b], upper) for s, b in pairs}
    out, carries = [], []
    for s in range(len(qs)):
        c = cum[s, 0][:, 0:1]
        blocks = [(rows[s, 0], z[s, 0], jnp.exp(jnp.where(below, z[s, 0] + cum[s, 0], -1e30)))]
        for b in range(1, NEAR):
            live = jds[s] >= b
            blocks.append((rows[s, b], z[s, b], jnp.exp(z[s, b] + cum[s, b] + (c + jnp.where(live, 0.0, -1e30)))))
            c = c + jnp.where(live, cum[s, b][:, 0:1], 0.0)
        out.append(blocks)
        carries.append(c)
    return out, carries


def _sb_far(q, kj, upper, c):
    z = _dot(q, kj, NT) * SCALE
    cum = _tri_sum(_log_stick(z), upper)
    return z, jnp.exp(z + cum + c), c + cum[:, 0:1]


def _block_rows(j, size):
    return pl.ds(pl.multiple_of(j * size, size), size)


def _sweep_on(st):
    return jnp.logical_and(st[0] >= 0, jnp.max(st[1]) > -STICK_EXIT)


def attn_fwd(name, qkv, tq):
    t, d3 = qkv.shape
    d = d3 // 3
    nh = d // HEAD_DIM
    nq = t // tq
    tb = KEY_BLOCK
    nsub = tq // tb

    def body(q_ref, k_ref, v_ref, o_ref):
        i = pl.program_id(1)
        row = lax.broadcasted_iota(jnp.int32, (tb, tb), 0)
        col = lax.broadcasted_iota(jnp.int32, (tb, tb), 1)
        upper = (row >= col).astype(BF16)
        qs = [q_ref[s * tb:(s + 1) * tb, :] for s in range(nsub)]
        jds = [i * nsub + s for s in range(nsub)]
        near, carries = _sb_near(qs, jds, k_ref, col < row, upper)
        state = []
        for s in range(nsub):
            acc = jnp.zeros((tb, HEAD_DIM), F32)
            for rows, _, a in near[s]:
                acc = acc + _dot(a.astype(BF16), v_ref[rows, :], NN)
            state.append((qs[s], jds[s], carries[s], acc))
        for s, (q, jd, c, acc) in enumerate(state):

            def step(st, q=q):
                rows = _block_rows(st[0], tb)
                _, a, c2 = _sb_far(q, k_ref[rows, :], upper, st[1])
                return st[0] - 1, c2, st[2] + _dot(a.astype(BF16), v_ref[rows, :], NN)

            _, _, acc = lax.while_loop(_sweep_on, step, (jd - NEAR, c, acc))
            o_ref[s * tb:(s + 1) * tb, :] = acc.astype(o_ref.dtype)

    return _pcall(body, name=name, out_shape=jax.ShapeDtypeStruct((t, d), BF16), grid=(nh, nq),
                  in_specs=[pl.BlockSpec((tq, HEAD_DIM), lambda h, i: (i, h)),
                            pl.BlockSpec((t, HEAD_DIM), lambda h, i: (0, nh + h)),
                            pl.BlockSpec((t, HEAD_DIM), lambda h, i: (0, 2 * nh + h))],
                  out_specs=pl.BlockSpec((tq, HEAD_DIM), lambda h, i: (i, h)),
                  compiler_params=_params(("parallel", "arbitrary")))(qkv, qkv, qkv)


def attn_bwd(name, qkv, do, tq):
    t, d3 = qkv.shape
    d = d3 // 3
    nh = d // HEAD_DIM
    nq = t // tq
    tb = KEY_BLOCK
    nsub = tq // tb

    def body(q_ref, k_ref, v_ref, do_ref, dq_ref, dk_ref, dv_ref, dk_acc, dv_acc, g_buf, z_buf):
        i = pl.program_id(1)

        @pl.when(i == 0)
        def _():
            dk_acc[...] = jnp.zeros_like(dk_acc)
            dv_acc[...] = jnp.zeros_like(dv_acc)

        row = lax.broadcasted_iota(jnp.int32, (tb, tb), 0)
        col = lax.broadcasted_iota(jnp.int32, (tb, tb), 1)
        below = col < row
        upper = (row >= col).astype(BF16)
        lower = (row <= col).astype(BF16)

        qs = [q_ref[s * tb:(s + 1) * tb, :] for s in range(nsub)]
        dos = [do_ref[s * tb:(s + 1) * tb, :] for s in range(nsub)]
        jds = [i * nsub + s for s in range(nsub)]
        near, carries = _sb_near(qs, jds, k_ref, below, upper)
        da = [[_dot(dos[s], v_ref[rows, :], NT) for rows, _, _ in near[s]] for s in range(nsub)]
        state = []
        for s in range(nsub):
            kept = [(rows, z, da[s][b] * a) for b, (rows, z, a) in enumerate(near[s])]
            for rows, _, a in near[s]:
                dv_acc[rows, :] += _dot(a.astype(BF16), dos[s], TN)
            state.append((qs[s], dos[s], jds[s], carries[s], kept))

        carried = []
        for s, (q, dov, jd, c, kept) in enumerate(state):
            def step(st, s=s, q=q, dov=dov, jd=jd):
                j = st[0]
                rows = _block_rows(j, tb)
                z, a, c2 = _sb_far(q, k_ref[rows, :], upper, st[1])
                g_buf[jd - j] = _dot(dov, v_ref[rows, :], NT) * a
                z_buf[jd - j] = z
                dv_acc[rows, :] += _dot(a.astype(BF16), dov, TN)
                return j - 1, c2

            j_stop, _ = lax.while_loop(_sweep_on, step, (jd - NEAR, c))

            def far(j, st, s=s, q=q, jd=jd):
                run, dq = st
                rows = _block_rows(j, tb)
                g = g_buf[jd - j]
                dz = (g - jax.nn.sigmoid(z_buf[jd - j]) * (run + _tri_sum(g, lower))).astype(BF16)
                dk_acc[rows, :] += _dot(dz, q, TN)
                return run + jnp.sum(g, axis=1, keepdims=True), dq + _dot(dz, k_ref[rows, :], NN)

            carried.append(lax.fori_loop(j_stop + 1, jd - NEAR + 1, far,
                                         (jnp.zeros((tb, 1), F32), jnp.zeros((tb, HEAD_DIM), F32))))

        tri = [[_tri_sum(g, lower) for _, _, g in st[4]] for st in state]
        sig = [[jax.nn.sigmoid(z) for _, z, _ in st[4]] for st in state]
        for s, (q, dov, jd, c, kept) in enumerate(state):
            run, dq = carried[s]
            for b in reversed(range(NEAR)):
                rows, z, g = kept[b]
                dz = g - sig[s][b] * (run + tri[s][b])
                if b == 0:
                    dz = jnp.where(below, dz, 0.0)
                dz = dz.astype(BF16)
                dk_acc[rows, :] += _dot(dz, q, TN)
                dq = dq + _dot(dz, k_ref[rows, :], NN)
                if b:
                    run = run + jnp.sum(g, axis=1, keepdims=True)
            dq_ref[s * tb:(s + 1) * tb, :] = (dq * SCALE).astype(BF16)

        @pl.when(i == nq - 1)
        def _():
            dk_ref[...] = (dk_acc[...] * SCALE).astype(BF16)
            dv_ref[...] = dv_acc[...].astype(BF16)

    blk = pl.BlockSpec((tq, HEAD_DIM), lambda h, i: (i, h))
    col_h = pl.BlockSpec((t, HEAD_DIM), lambda h, i: (0, h))
    out = jax.ShapeDtypeStruct((t, d), BF16)
    return _pcall(body, name=name, out_shape=(out, out, out), grid=(nh, nq),
                  in_specs=[blk,
                            pl.BlockSpec((t, HEAD_DIM), lambda h, i: (0, nh + h)),
                            pl.BlockSpec((t, HEAD_DIM), lambda h, i: (0, 2 * nh + h)),
                            blk],
                  out_specs=(blk, col_h, col_h),
                  scratch_shapes=[pltpu.VMEM((t, HEAD_DIM), F32), pltpu.VMEM((t, HEAD_DIM), F32),
                                  pltpu.VMEM((t // tb, tb, tb), F32), pltpu.VMEM((t // tb, tb, tb), F32)],
                  compiler_params=_params(("parallel", "arbitrary")))(qkv, qkv, qkv, do)


def _place():
    x, y, c = lax.axis_index("x"), lax.axis_index("y"), lax.axis_index("c")
    chips = [(1 - x, y), (x, 1 - y), (1 - x, 1 - y)]
    return x, y, c, chips


def _remote(src, dst, send_sem, recv_sem, dev):
    return pltpu.make_async_remote_copy(src_ref=src, dst_ref=dst, send_sem=send_sem, recv_sem=recv_sem,
                                        device_id=dev, device_id_type=MESH)


def place_shard(name, w, chip):
    r, cdim = w.shape
    tr = _tile(r, max(BF16_ROWS, (1 << 19) // cdim), BF16_ROWS)

    def body(chip_ref, w_ref, o_ref):
        o_ref[...] = w_ref[...].astype(BF16)

    spec = pltpu.PrefetchScalarGridSpec(
        num_scalar_prefetch=1, grid=(r // tr,),
        in_specs=[pl.BlockSpec((tr, cdim), lambda i, s: (i, 0))],
        out_specs=pl.BlockSpec((None, tr, cdim), lambda i, s: (s[0], i, 0)))
    return _pcall(body, name=name, out_shape=jax.ShapeDtypeStruct((N_CHIPS, r, cdim), BF16), grid_spec=spec,
                  compiler_params=_params(("parallel",)))(chip, w)


class Comm:
    def __init__(self, ins, outs, aliases, sems, first, mid, last):
        self.ins, self.outs, self.aliases, self.sems = list(ins), list(outs), dict(aliases), list(sems)
        self.first, self.mid, self.last = first, mid, last


def run_comm(name, comm):
    ni, no = len(comm.ins), len(comm.outs)

    def body(*refs):
        ins, outs, sems = refs[:ni], refs[ni:ni + no], refs[ni + no:]
        comm.first(ins, outs, sems)
        comm.mid(ins, outs, sems)
        comm.last(ins, outs, sems)

    return _pcall(body, name=name, out_shape=comm.outs, in_specs=[ANY] * ni, out_specs=[ANY] * no,
                  input_output_aliases=comm.aliases, scratch_shapes=comm.sems, compiler_params=_params())(*comm.ins)


def gather_comm(bufs):
    n = len(bufs)

    def half(out, w, which):
        pr = out[w].shape[1] // 2
        return pl.ds(pl.multiple_of(which * pr, BF16_ROWS), pr)

    def first(ins, out, sems):
        isend, irecv, _, _ = sems
        x, y, c, chips = _place()
        for w in range(n):
            mine = out[w].at[2 * x + y, half(out, w, c)]
            for j, (cx, cy) in enumerate(chips):
                _remote(mine, mine, isend.at[3 * w + j], irecv.at[3 * w + j], (cx, cy, c)).start()

    def mid(ins, out, sems):
        isend, irecv, dsend, drecv = sems
        x, y, c, chips = _place()
        sib = (x, y, 1 - c)
        for w in range(n):
            for j, (cx, cy) in enumerate(chips):
                landed = out[w].at[2 * cx + cy, half(out, w, c)]
                _remote(landed, landed, isend.at[3 * w + j], irecv.at[3 * w + j], sib).wait_recv()
                _remote(landed, landed, dsend.at[3 * w + j], drecv.at[3 * w + j], sib).start()

    def last(ins, out, sems):
        isend, irecv, dsend, drecv = sems
        x, y, c, chips = _place()
        sib = (x, y, 1 - c)
        for w in range(n):
            for j, (cx, cy) in enumerate(chips):
                landed = out[w].at[2 * cx + cy, half(out, w, 1 - c)]
                _remote(landed, landed, dsend.at[3 * w + j], drecv.at[3 * w + j], sib).wait_recv()
        for w in range(n):
            sent = out[w].at[0, half(out, w, c)]
            for j in range(3):
                _remote(sent, sent, isend.at[3 * w + j], irecv.at[3 * w + j], sib).wait_send()
                _remote(sent, sent, dsend.at[3 * w + j], drecv.at[3 * w + j], sib).wait_send()

    return Comm(bufs, [jax.ShapeDtypeStruct(s.shape, s.dtype) for s in bufs], {w: w for w in range(n)},
                [pltpu.SemaphoreType.DMA((3 * n,))] * 4, first, mid, last)


def _nothing(ins, outs, sems):
    return None


def exchange_comm(pieces):
    n = len(pieces)

    def copies(src, out, sems):
        x, y, c, _ = _place()
        return [_remote(src[w].at[k, 1 - c], out[w].at[k], sems[0].at[N_CHIPS * w + k], sems[1].at[N_CHIPS * w + k],
                        (x, y, 1 - c)) for w in range(n) for k in range(N_CHIPS)]

    def first(src, out, sems):
        for cp in copies(src, out, sems):
            cp.start()

    def last(src, out, sems):
        for cp in copies(src, out, sems):
            cp.wait()

    return Comm(pieces, [jax.ShapeDtypeStruct((N_CHIPS,) + s.shape[2:], s.dtype) for s in pieces], {},
                [pltpu.SemaphoreType.DMA((N_CHIPS * n,))] * 2, first, _nothing, last)


def scatter_comm(parts):
    n = len(parts)

    def copies(src, out, sems):
        x, y, c, chips = _place()
        return [_remote(src[w].at[2 * cx + cy], out[w].at[j], sems[0].at[3 * w + j], sems[1].at[3 * w + j], (cx, cy, c))
                for w in range(n) for j, (cx, cy) in enumerate(chips)]

    def first(src, out, sems):
        for cp in copies(src, out, sems):
            cp.start()

    def last(src, out, sems):
        for cp in copies(src, out, sems):
            cp.wait()

    return Comm(parts, [jax.ShapeDtypeStruct((3,) + s.shape[1:], s.dtype) for s in parts], {},
                [pltpu.SemaphoreType.DMA((3 * n,))] * 2, first, _nothing, last)


def share_comm(halves):
    n = len(halves)

    def first(ins, buf, sems):
        x, y, c, _ = _place()
        for w in range(n):
            _remote(buf[w].at[c], buf[w].at[c], sems[0].at[w], sems[1].at[w], (x, y, 1 - c)).start()

    def last(ins, buf, sems):
        x, y, c, _ = _place()
        for w in range(n):
            landed = buf[w].at[1 - c]
            _remote(landed, landed, sems[0].at[w], sems[1].at[w], (x, y, 1 - c)).wait_recv()
        for w in range(n):
            _remote(buf[w].at[c], buf[w].at[c], sems[0].at[w], sems[1].at[w], (x, y, 1 - c)).wait_send()

    return Comm(halves, [jax.ShapeDtypeStruct(s.shape, s.dtype) for s in halves], {w: w for w in range(n)},
                [pltpu.SemaphoreType.DMA((n,))] * 2, first, _nothing, last)


def gather_small(name, blk, reduce):
    r, cdim = blk.shape

    def body(in_ref, out_ref, *rest):
        if reduce:
            buf, send_sem, recv_sem = rest
        else:
            buf = out_ref
            send_sem, recv_sem = rest
        x, y, c, _ = _place()
        me = 4 * x + 2 * y + c
        buf[me] = in_ref[...]
        peers = []
        for dx in range(2):
            for dy in range(2):
                for dc in range(2):
                    if dx or dy or dc:
                        peers.append((dx, dy, dc))
        copies = []
        for s, (dx, dy, dc) in enumerate(peers):
            cp = _remote(in_ref, buf.at[me], send_sem.at[s], recv_sem.at[s],
                         ((1 - x if dx else x), (1 - y if dy else y), (1 - c if dc else c)))
            cp.start()
            copies.append(cp)
        for s, (dx, dy, dc) in enumerate(peers):
            px, py, pc_ = (1 - x if dx else x), (1 - y if dy else y), (1 - c if dc else c)
            landed = buf.at[4 * px + 2 * py + pc_]
            _remote(landed, landed, send_sem.at[s], recv_sem.at[s], (x, y, c)).wait_recv()
        for cp in copies:
            cp.wait_send()
        if reduce:
            tot = buf[0]
            for s in range(1, N_DEV):
                tot = tot + buf[s]
            out_ref[...] = tot

    vm = pl.BlockSpec(memory_space=pltpu.VMEM)
    out_shape = jax.ShapeDtypeStruct((r, cdim) if reduce else (N_DEV, r, cdim), F32)
    scratch = ([pltpu.VMEM((N_DEV, r, cdim), F32)] if reduce else []) + [pltpu.SemaphoreType.DMA((N_DEV - 1,))] * 2
    return _pcall(body, name=name, out_shape=out_shape, in_specs=[vm], out_specs=vm, scratch_shapes=scratch,
                  compiler_params=_params())(blk)


def sum_cores(name, own, got, place):
    _, _, pr, pc = own.shape
    tr = _tile(pr, max(BF16_ROWS, (1 << 19) // pc), BF16_ROWS)

    def body(place_ref, own_ref, got_ref, o_ref):
        o_ref[...] = (own_ref[...].astype(F32) + got_ref[...].astype(F32)).astype(o_ref.dtype)

    spec = pltpu.PrefetchScalarGridSpec(
        num_scalar_prefetch=1, grid=(N_CHIPS, pr // tr),
        in_specs=[pl.BlockSpec((None, None, tr, pc), lambda k, i, s: (k, s[1], i, 0)),
                  pl.BlockSpec((None, tr, pc), lambda k, i, s: (k, i, 0))],
        out_specs=pl.BlockSpec((None, tr, pc), lambda k, i, s: (k, i, 0)))
    return _pcall(body, name=name, out_shape=jax.ShapeDtypeStruct((N_CHIPS, pr, pc), BF16), grid_spec=spec,
                  compiler_params=_params(("parallel", "parallel")))(place, own, got)


def sum_chips(name, part, got, place):
    _, pr, pc = part.shape
    tr = _tile(pr, max(BF16_ROWS, (1 << 18) // pc), BF16_ROWS)

    def body(place_ref, part_ref, got_ref, o_ref):
        tot = part_ref[...].astype(F32)
        for j in range(3):
            tot = tot + got_ref[j].astype(F32)
        o_ref[...] = tot

    spec = pltpu.PrefetchScalarGridSpec(
        num_scalar_prefetch=1, grid=(pr // tr,),
        in_specs=[pl.BlockSpec((None, tr, pc), lambda i, s: (s[0], i, 0)),
                  pl.BlockSpec((3, tr, pc), lambda i, s: (0, i, 0))],
        out_specs=pl.BlockSpec((None, tr, pc), lambda i, s: (s[1], i, 0)))
    return _pcall(body, name=name, out_shape=jax.ShapeDtypeStruct((2, pr, pc), F32), grid_spec=spec,
                  compiler_params=_params(("parallel",)))(place, part, got)


def adamw(name, w, g, m, v):
    rows, cols = w.shape
    tr = _tile(rows, max(8, (1 << 18) // cols))
    c1 = 1.0 / (1.0 - ADAM_B1 ** ADAM_STEP)
    c2 = 1.0 / (1.0 - ADAM_B2 ** ADAM_STEP)

    def body(w_ref, g_ref, m_ref, v_ref, d_ref, nm_ref, nv_ref):
        gv = g_ref[...]
        nm = ADAM_B1 * m_ref[...] + (1.0 - ADAM_B1) * gv
        nv = ADAM_B2 * v_ref[...] + (1.0 - ADAM_B2) * (gv * gv)
        nm_ref[...] = nm
        nv_ref[...] = nv
        d_ref[...] = -ADAM_LR * ((nm * c1) / (jnp.sqrt(nv * c2) + ADAM_EPS) + ADAM_WD * w_ref[...])

    spec = pl.BlockSpec((tr, cols), lambda i: (i, 0))
    sds = jax.ShapeDtypeStruct((rows, cols), F32)
    return _pcall(body, name=name, out_shape=(sds, sds, sds), grid=(rows // tr,),
                  in_specs=[spec] * 4, out_specs=(spec, spec, spec),
                  compiler_params=_params(("parallel",)))(w, g, m, v)


MATS = ["ffn1_w_in", "ffn1_w_out", "w_mix_in", "w_conv_out", "w_attn_out", "w_mix_out", "ffn2_w_in", "ffn2_w_out",
        "w_ple_gate", "w_ple_proj"]
COL_SHARDED = {"ffn1_w_in", "w_mix_in", "ffn2_w_in", "w_ple_proj"}
NORMS = ["ffn1_norm", "mix_norm", "ffn2_norm", "ple_norm", "final_norm"]
WEIGHTS = ["ffn1_norm", "ffn1_w_in", "ffn1_w_out", "mix_norm", "w_mix_in", "conv_w", "w_conv_out", "w_attn_out",
           "w_mix_out", "ffn2_norm", "ffn2_w_in", "ffn2_w_out", "ple_norm", "w_ple_gate", "w_ple_proj", "final_norm"]


def _pad_rows(a, rows):
    return jnp.concatenate([a, jnp.zeros((rows - a.shape[0],) + a.shape[1:], a.dtype)], axis=0)


def _step(x, p, tgt, w, m, v):
    t, d = x.shape
    tt = _tile(t, 256)
    tm = _tile(t, 512)
    tm2 = _tile(t, 1024)
    tq = _tile(t, 1024)

    chip = 2 * lax.axis_index("x") + lax.axis_index("y")
    place = jnp.stack([chip, lax.axis_index("c")]).astype(jnp.int32)

    placed = {k: place_shard("place_" + k, w[k], place) for k in MATS}
    full = {}

    def keep(names, bufs):
        for k, buf in zip(names, bufs):
            full[k] = buf if k in COL_SHARDED else buf.reshape(-1, buf.shape[2])

    def gather_of(names):
        return gather_comm([placed[k] for k in names])

    cw_all = gather_small("gather_conv_w", _pad_rows(w["conv_w"], 8), False)
    cw8 = jnp.concatenate([cw_all[2 * k] for k in range(N_CHIPS)], axis=1)
    g1, gm, g2, gp, gf = (w[k].reshape(1, d) for k in NORMS)

    def ffn_fwd(tag, h, g, first, w_in_name, w_out_name, riders):
        if first:
            n, bufs = rms_fwd(tag + "_norm", h, g, tt, comm=gather_of(first))
            keep(first, bufs)
        else:
            n = rms_fwd(tag + "_norm", h, g, tt)
        w_in = full[w_in_name]
        if riders:
            (a, s), bufs = ffn_in_act(tag + "_in", n, w_in, tm, comm=gather_of(riders))
            keep(riders, bufs)
        else:
            a, s = ffn_in_act(tag + "_in", n, w_in, tm)
        return n, a, s, mm_nn(tag + "_out", s, full[w_out_name], F32, tm, res=h, alpha=0.5)

    n1, a1, s1, h1 = ffn_fwd("ffn1", x, g1, ["ffn1_w_in"], "ffn1_w_in", "ffn1_w_out", ["ffn1_w_out", "w_mix_in"])
    u = rms_fwd("mix_norm", h1, gm, tt)
    wmix = full["w_mix_in"]
    riders = [["w_conv_out", "w_attn_out", "w_mix_out"], ["ffn2_w_in"], ["ffn2_w_out", "w_ple_gate", "w_ple_proj"]]
    cbx, bufs = mm_nn_stacked("mix_in_conv", u, wmix, F32, tm2, d, 0, 3, comm=gather_of(riders[0]))
    keep(riders[0], bufs)
    qkv, bufs = mm_nn_stacked("mix_in_qkv", u, wmix, BF16, tm2, d, 3, 3, comm=gather_of(riders[1]))
    keep(riders[1], bufs)
    gates, bufs = mm_nn_stacked("mix_in_gates", u, wmix, BF16, tm2, d, 6, 2, comm=gather_of(riders[2]))
    keep(riders[2], bufs)
    wpp = full["w_ple_proj"]
    wpp = jnp.transpose(wpp, (1, 0, 2)).reshape(wpp.shape[1], -1)
    ycin = conv_fwd("conv", cbx, cw8, tt)
    y_conv = mm_nn("conv_out", ycin, full["w_conv_out"], BF16, tm)
    o = attn_fwd("attn", qkv, tq)
    y_attn = mm_nn("attn_out", o, full["w_attn_out"], BF16, tm)
    merged = gate_fwd("merge", gates, y_conv, y_attn, tt)
    h2 = mm_nn("mix_out", merged, full["w_mix_out"], F32, tm, res=h1, alpha=1.0)
    n2, a2, s2, h3 = ffn_fwd("ffn2", h2, g2, [], "ffn2_w_in", "ffn2_w_out", [])
    npl = rms_fwd("ple_norm", h3, gp, tt)
    zg = mm_nn("ple_gate", npl, full["w_ple_gate"], F32, tm)
    pp = mm_nn("ple_proj", p, wpp, F32, tm)

    pieces, chip_sums, halves = {}, {}, {}

    def as_pieces(k):
        pc = pieces[k]
        return pc if k in COL_SHARDED else pc.reshape(N_CHIPS, 2, pc.shape[0] // (2 * N_CHIPS), pc.shape[1])

    def sum_siblings(tag, names):
        pcs = [as_pieces(k) for k in names]
        got = run_comm("exchange_" + tag, exchange_comm(pcs))
        for k, a, b in zip(names, pcs, got):
            chip_sums[k] = sum_cores("sum_cores_" + k, a, b, place)

    def scatter_of(names):
        return scatter_comm([chip_sums[k] for k in names])

    def sum_landed(names, landed):
        for k, b in zip(names, landed):
            halves[k] = sum_chips("sum_chips_" + k, chip_sums[k], b, place)

    dh4, dpp, dzg, dgf, loss_row = tail("tail", h3, zg, pp, tgt, gf, tt)
    dwpp = mm_tn_whole("ple_proj_dw", p, dpp, tm2)
    pieces["w_ple_proj"] = jnp.transpose(dwpp.reshape(2, p.shape[1] // 2, N_CHIPS, d // N_CHIPS), (2, 0, 1, 3))
    pieces["w_ple_gate"] = mm_tn_rows("ple_gate_dw", npl, dzg, tm2)
    dh3, df2, dgp = mm_nt("ple_gate_dx", dzg, full["w_ple_gate"], F32, tm, d, norm=(h3, gp, dh4), alpha=0.5)
    w_in, w_out = full["ffn2_w_in"], full["ffn2_w_out"]
    pieces["ffn2_w_out"] = mm_tn_rows("ffn2_dwout", s2, df2, tm2)
    da2 = ffn_ds_dact("ffn2_ds", df2, w_out, a2, tm)
    pieces["ffn2_w_in"] = mm_tn_cols("ffn2_dwin", n2, da2, tm2)
    dh2, dh2b, dg2 = mm_nt_stacked("ffn2_dn", da2, w_in, F32, tm, w_in.shape[2], norm=(h2, g2, dh3))
    pieces["w_mix_out"] = mm_tn_rows("mix_out_dw", merged, dh2b, tm2)
    dmerged = mm_nt("mix_out_dx", dh2b, full["w_mix_out"], F32, tm, d)
    dyc, dya, dgates = gate_bwd("merge_bwd", dmerged, gates, y_conv, y_attn, tt)
    pieces["w_conv_out"] = mm_tn_rows("conv_out_dw", ycin, dyc, tm2)
    dycin = mm_nt("conv_out_dx", dyc, full["w_conv_out"], F32, tm, d)
    dcbx, dcw8 = conv_bwd("conv_bwd", dycin, cbx, cw8, tt)
    pieces["w_attn_out"] = mm_tn_rows("attn_out_dw", o, dya, tm2)
    do = mm_nt("attn_out_dx", dya, full["w_attn_out"], BF16, tm, d)
    dq, dk, dv = attn_bwd("attn_bwd", qkv, do, tq)
    dmix = [dcbx, dq, dk, dv, dgates]
    early = ["ffn2_w_in", "ffn2_w_out", "w_ple_gate", "w_ple_proj", "w_mix_out", "w_conv_out", "w_attn_out"]
    swap = exchange_comm([as_pieces(k) for k in early])
    pieces["w_mix_in"], got = mm_tn_parts("mix_in_dw", u, dmix, tm2, comm=swap)
    for k, a, b in zip(early, swap.ins, got):
        chip_sums[k] = sum_cores("sum_cores_" + k, a, b, place)
    (dh1, df1, dgm), landed = mm_nt_parts("mix_in_dx", dmix, wmix, tm, (h1, gm, dh2), 0.5, comm=scatter_of(early))
    sum_landed(early, landed)
    sum_siblings("mix", ["w_mix_in"])
    w_in, w_out = full["ffn1_w_in"], full["ffn1_w_out"]
    pieces["ffn1_w_out"] = mm_tn_rows("ffn1_dwout", s1, df1, tm2)
    da1 = ffn_ds_dact("ffn1_ds", df1, w_out, a1, tm)
    pieces["ffn1_w_in"], landed = mm_tn_cols("ffn1_dwin", n1, da1, tm2, comm=scatter_of(["w_mix_in"]))
    sum_landed(["w_mix_in"], landed)
    late = ["ffn1_w_in", "ffn1_w_out"]
    sum_siblings("late", late)
    (dx, _, dg1), landed = mm_nt_stacked("ffn1_dn", da1, w_in, F32, tm, w_in.shape[2], comm=scatter_of(late),
                                         norm=(x, g1, dh1))
    sum_landed(late, landed)

    shared = run_comm("share_halves", share_comm([halves[k] for k in MATS]))
    grad, delta, new_m, new_v = {}, {}, {}, {}
    for k, sh in zip(MATS, shared):
        grad[k] = sh.reshape(w[k].shape)
        delta[k], new_m[k], new_v[k] = adamw("adamw_" + k, w[k], grad[k], m[k], v[k])

    small = jnp.concatenate([dg1, dgm, dg2, dgp, dgf, dcw8[:3], loss_row, jnp.zeros((7, d), F32)], axis=0)
    tot = gather_small("sum_small", small, True)
    loss = tot[8, 0]
    norm_w = jnp.concatenate([w[k].reshape(1, d) for k in NORMS] + [jnp.zeros((3, d), F32)], axis=0)
    norm_m = jnp.concatenate([m[k].reshape(1, d) for k in NORMS] + [jnp.zeros((3, d), F32)], axis=0)
    norm_v = jnp.concatenate([v[k].reshape(1, d) for k in NORMS] + [jnp.ones((3, d), F32)], axis=0)
    norm_g = jnp.concatenate([tot[0:5], jnp.zeros((3, d), F32)], axis=0)
    nd, nm, nv = adamw("adamw_norms", norm_w, norm_g, norm_m, norm_v)
    for r, k in enumerate(NORMS):
        grad[k] = norm_g[r].reshape(w[k].shape)
        delta[k], new_m[k], new_v[k] = (a[r].reshape(w[k].shape) for a in (nd, nm, nv))
    cs = d // N_CHIPS
    gcw = lax.dynamic_slice(tot[5:8], (0, chip * cs), (3, cs))
    cd, cm, cv = adamw("adamw_conv_w", _pad_rows(w["conv_w"], 8), _pad_rows(gcw, 8), _pad_rows(m["conv_w"], 8),
                       jnp.concatenate([v["conv_w"], jnp.ones((5, cs), F32)], axis=0))
    grad["conv_w"], delta["conv_w"], new_m["conv_w"], new_v["conv_w"] = gcw, cd[:3], cm[:3], cv[:3]
    return loss, dx, grad, delta, new_m, new_v


def kernel(x, p, ffn1_norm, ffn1_w_in, ffn1_w_out, mix_norm, w_mix_in, conv_w, w_conv_out, w_attn_out, w_mix_out, ffn2_norm, ffn2_w_in, ffn2_w_out, ple_norm, w_ple_gate, w_ple_proj, final_norm, loss_target, m_ffn1_norm, m_ffn1_w_in, m_ffn1_w_out, m_mix_norm, m_w_mix_in, m_conv_w, m_w_conv_out, m_w_attn_out, m_w_mix_out, m_ffn2_norm, m_ffn2_w_in, m_ffn2_w_out, m_ple_norm, m_w_ple_gate, m_w_ple_proj, m_final_norm, v_ffn1_norm, v_ffn1_w_in, v_ffn1_w_out, v_mix_norm, v_w_mix_in, v_conv_w, v_w_conv_out, v_w_attn_out, v_w_mix_out, v_ffn2_norm, v_ffn2_w_in, v_ffn2_w_out, v_ple_norm, v_w_ple_gate, v_w_ple_proj, v_final_norm):
    ws = (ffn1_norm, ffn1_w_in, ffn1_w_out, mix_norm, w_mix_in, conv_w, w_conv_out, w_attn_out, w_mix_out, ffn2_norm,
          ffn2_w_in, ffn2_w_out, ple_norm, w_ple_gate, w_ple_proj, final_norm)
    ms = (m_ffn1_norm, m_ffn1_w_in, m_ffn1_w_out, m_mix_norm, m_w_mix_in, m_conv_w, m_w_conv_out, m_w_attn_out,
          m_w_mix_out, m_ffn2_norm, m_ffn2_w_in, m_ffn2_w_out, m_ple_norm, m_w_ple_gate, m_w_ple_proj, m_final_norm)
    vs = (v_ffn1_norm, v_ffn1_w_in, v_ffn1_w_out, v_mix_norm, v_w_mix_in, v_conv_w, v_w_conv_out, v_w_attn_out,
          v_w_mix_out, v_ffn2_norm, v_ffn2_w_in, v_ffn2_w_out, v_ple_norm, v_w_ple_gate, v_w_ple_proj, v_final_norm)
    assert x.shape[0] == 1 and p.shape[:2] == (1, 1), "one sequence and one layer per device"

    def strip(a):
        return a[0] if a.ndim == 3 or (a.ndim == 2 and a.shape[0] == 1) else a

    w = {k: strip(a) for k, a in zip(WEIGHTS, ws)}
    m = {k: strip(a) for k, a in zip(WEIGHTS, ms)}
    v = {k: strip(a) for k, a in zip(WEIGHTS, vs)}
    loss, dx, grad, delta, new_m, new_v = _step(x[0], p[0, 0], loss_target[0], w, m, v)
    shapes = [a.shape for a in ws]
    outs = [loss, dx[None]]
    for res in (grad, delta, new_m, new_v):
        outs += [res[k].reshape(s) for k, s in zip(WEIGHTS, shapes)]
    return tuple(outs)
```

```python
import functools
import math

import jax
import jax.numpy as jnp
from jax import lax
from jax.experimental import pallas as pl
from jax.experimental.pallas import tpu as pltpu

F32 = jnp.float32
BF16 = jnp.bfloat16
MESH = pl.DeviceIdType.MESH
ANY = pl.BlockSpec(memory_space=pl.ANY)

HEAD_DIM = 128
NORM_EPS = 1e-6
N_CHIPS = 4
N_DEV = 8
BF16_ROWS = 16
VMEM_LIMIT = 56 * 1024 * 1024
ACC_BYTES = 8 * 1024 * 1024
STICK_EXIT = 110.0

ADAM_LR = 0.001
ADAM_B1 = 0.9
ADAM_B2 = 0.999
ADAM_EPS = 1e-08
ADAM_WD = 0.01
ADAM_STEP = 10

NN = (((1,), (0,)), ((), ()))
NT = (((1,), (1,)), ((), ()))
TN = (((0,), (0,)), ((), ()))


def _params(sem=None, **kw):
    if sem is not None:
        kw["dimension_semantics"] = sem
    return pltpu.CompilerParams(vmem_limit_bytes=VMEM_LIMIT, **kw)


def _pcall(body, **kw):
    return pl.pallas_call(body, **kw)


def _tile(n, pref, mult=8):
    best = None
    for d in range(mult, min(n, pref) + 1, mult):
        if n % d == 0:
            best = d
    return best if best is not None else n


def _dot(a, b, dims):
    return lax.dot_general(a, b, dims, preferred_element_type=F32)


def _call(name, body, grid, in_specs, out_specs, out_shape, args, scratch=(), sem=None, comm=None):
    n_in, n_out, n_sc = len(in_specs), len(out_specs), len(scratch)
    if comm is None:
        def plain(*refs):
            body(refs[:n_in], refs[n_in:n_in + n_out], refs[n_in + n_out:])

        return _pcall(plain, name=name, out_shape=list(out_shape), grid=grid, in_specs=list(in_specs),
                      out_specs=list(out_specs), scratch_shapes=list(scratch), compiler_params=_params(sem))(*args)
    n_cin, n_cout = len(comm.ins), len(comm.outs)
    steps = math.prod(grid)

    def hosted(*refs):
        ins, c_ins = refs[:n_in], refs[n_in:n_in + n_cin]
        outs = refs[n_in + n_cin:n_in + n_cin + n_out]
        c_outs = refs[n_in + n_cin + n_out:n_in + n_cin + n_out + n_cout]
        rest = refs[n_in + n_cin + n_out + n_cout:]
        sems = rest[n_sc:]
        step = pl.program_id(0)
        for ax in range(1, len(grid)):
            step = step * grid[ax] + pl.program_id(ax)

        @pl.when(step == 0)
        def _():
            comm.first(c_ins, c_outs, sems)

        body(ins, outs, rest[:n_sc])

        @pl.when(step == (3 * steps) // 4)
        def _():
            comm.mid(c_ins, c_outs, sems)

        @pl.when(step == steps - 1)
        def _():
            comm.last(c_ins, c_outs, sems)

    res = _pcall(hosted, name=name, out_shape=list(out_shape) + comm.outs, grid=grid,
                 in_specs=list(in_specs) + [ANY] * n_cin, out_specs=list(out_specs) + [ANY] * n_cout,
                 input_output_aliases={n_in + k: n_out + v for k, v in comm.aliases.items()},
                 scratch_shapes=list(scratch) + comm.sems,
                 compiler_params=_params(("arbitrary",) * len(grid)))(*args, *comm.ins)
    return list(res[:n_out]), list(res[n_out:])


NORM_CHUNK = 256


def _norm_bwd_tile(read_dn, rows, first, h_ref, g_ref, dr_ref, dh_ref, dhb_ref, dg_ref, alpha):
    @pl.when(first)
    def _():
        dg_ref[...] = jnp.zeros_like(dg_ref)

    gv = g_ref[...]
    tot = jnp.zeros_like(gv)
    for c0 in range(0, rows, NORM_CHUNK):
        sl = slice(c0, min(rows, c0 + NORM_CHUNK))
        hv = h_ref[sl, :]
        rs = _rstd(hv)
        hn = hv * rs
        dnv = read_dn(sl)
        gy = dnv * gv
        dh = dr_ref[sl, :] + rs * (gy - hn * jnp.mean(gy * hn, axis=-1, keepdims=True))
        dh_ref[sl, :] = dh
        dhb_ref[sl, :] = (alpha * dh).astype(BF16)
        tot = tot + jnp.sum(dnv * hn, axis=0, keepdims=True)
    dg_ref[...] += tot


def _mm(name, a, b, out_sds, grid, a_spec, b_spec, o_spec, dims, acc_shape, res=None, alpha=1.0, comm=None,
        norm=None):
    nk = grid[2]

    def body(ins, outs, scratch):
        a_ref, b_ref = ins[:2]
        r_ref = ins[2] if res is not None else None
        o_ref = outs[0]

        def finish(read):
            if norm is not None:
                first = jnp.logical_and(pl.program_id(0) == 0, pl.program_id(1) == 0)
                _norm_bwd_tile(read, o_ref.shape[0], first, *ins[2:5], *outs, alpha)
                return
            r = read(slice(None))
            if alpha != 1.0:
                r = r * alpha
            if r_ref is not None:
                r = r_ref[...] + r
            if len(o_ref.shape) == 3:
                half = o_ref.shape[1]
                o_ref[0] = r[:half].astype(o_ref.dtype)
                o_ref[1] = r[half:].astype(o_ref.dtype)
            else:
                o_ref[...] = r.astype(o_ref.dtype)

        if nk == 1:
            part = _dot(a_ref[...].astype(BF16), b_ref[...].astype(BF16), dims)
            finish(lambda sl: part[sl])
        else:
            acc_ref = scratch[0]
            kk = pl.program_id(2)

            @pl.when(kk == 0)
            def _():
                acc_ref[...] = jnp.zeros_like(acc_ref)

            acc_ref[...] += _dot(a_ref[...].astype(BF16), b_ref[...].astype(BF16), dims)

            @pl.when(kk == nk - 1)
            def _():
                finish(lambda sl: acc_ref[sl, :])

    in_specs = [a_spec, b_spec]
    args = [a, b]
    out_specs, out_shape = [o_spec], [out_sds]
    sem = ("parallel", "parallel", "arbitrary")
    if res is not None:
        in_specs.append(o_spec)
        args.append(res)
    if norm is not None:
        width = out_sds.shape[1]
        whole = pl.BlockSpec((1, width), lambda i, j, r: (0, 0))
        in_specs += [o_spec, whole, o_spec]
        args += list(norm)
        out_specs = [o_spec, o_spec, whole]
        out_shape = [jax.ShapeDtypeStruct(out_sds.shape, F32), jax.ShapeDtypeStruct(out_sds.shape, BF16),
                     jax.ShapeDtypeStruct((1, width), F32)]
        sem = ("arbitrary", "arbitrary", "arbitrary")
    scratch = [] if nk == 1 else [pltpu.VMEM(acc_shape, F32)]
    got = _call(name, body, grid, in_specs, out_specs, out_shape, args, scratch, sem, comm)
    if norm is not None:
        return got if comm is None else (got[0], got[1])
    return got[0] if comm is None else (got[0][0], got[1])


def ffn_in_act(name, n, w4, tm, comm=None):
    t, d = n.shape
    cs = w4.shape[2]

    def body(ins, outs, scratch):
        n_ref, wg_ref, wu_ref = ins
        a_ref, s_ref = outs
        nv = n_ref[...]
        gate = _dot(nv, wg_ref[...], NN)
        up = _dot(nv, wu_ref[...], NN)
        a_ref[0] = gate.astype(BF16)
        a_ref[1] = up.astype(BF16)
        s_ref[...] = (gate * jax.nn.sigmoid(gate) * up).astype(BF16)

    got = _call(name, body, (t // tm, 2),
                [pl.BlockSpec((tm, d), lambda i, j: (i, 0)),
                 pl.BlockSpec((None, d, cs), lambda i, j: (j, 0, 0)),
                 pl.BlockSpec((None, d, cs), lambda i, j: (2 + j, 0, 0))],
                [pl.BlockSpec((2, tm, cs), lambda i, j: (0, i, j)), pl.BlockSpec((tm, cs), lambda i, j: (i, j))],
                [jax.ShapeDtypeStruct((2, t, 2 * cs), BF16), jax.ShapeDtypeStruct((t, 2 * cs), BF16)],
                [n, w4, w4], (), ("parallel", "parallel"), comm)
    return got if comm is None else (got[0], got[1])


def ffn_ds_dact(name, df, w_out, a3, tm):
    t, d = df.shape
    f = w_out.shape[0]
    cs = f // 2

    def body(ins, outs, scratch):
        df_ref, w_ref, a_ref = ins
        ds = _dot(df_ref[...], w_ref[...], NT)
        gate = a_ref[0].astype(F32)
        up = a_ref[1].astype(F32)
        sg = jax.nn.sigmoid(gate)
        outs[0][0] = (ds * up * sg * (1.0 + gate * (1.0 - sg))).astype(BF16)
        outs[0][1] = (ds * gate * sg).astype(BF16)

    blk = pl.BlockSpec((2, tm, cs), lambda i, j: (0, i, j))
    return _call(name, body, (t // tm, 2),
                 [pl.BlockSpec((tm, d), lambda i, j: (i, 0)), pl.BlockSpec((cs, d), lambda i, j: (j, 0)), blk],
                 [blk], [jax.ShapeDtypeStruct((2, t, f), BF16)], [df, w_out, a3], (), ("parallel", "parallel"))[0]


def _part_ranges(parts, d):
    out, lo = [], 0
    for p in parts:
        out.append((lo, p.shape[1] // d))
        lo += p.shape[1] // d
    return out, lo


def mm_nt_parts(name, parts, w4, tm, norm, alpha, comm=None):
    m = parts[0].shape[0]
    d, cs = w4.shape[1], w4.shape[2]
    per = cs // d
    ranges, nblk = _part_ranges(parts, d)
    np_ = len(parts)

    def body(ins, outs, scratch):
        w_ref, acc = ins[np_], scratch[0]
        r = pl.program_id(1)

        @pl.when(r == 0)
        def _():
            acc[...] = jnp.zeros_like(acc)

        for (lo, n), a_ref in zip(ranges, ins[:np_]):
            @pl.when(jnp.logical_and(r >= lo, r < lo + n))
            def _(a_ref=a_ref):
                acc[...] += _dot(a_ref[...], w_ref[...], NT)

        @pl.when(r == nblk - 1)
        def _():
            _norm_bwd_tile(lambda sl: acc[sl, :], tm, pl.program_id(0) == 0, *ins[np_ + 1:], *outs, alpha)

    rows = pl.BlockSpec((tm, d), lambda i, r: (i, 0))
    whole = pl.BlockSpec((1, d), lambda i, r: (0, 0))
    specs = [pl.BlockSpec((tm, d), lambda i, r, lo=lo, n=n: (i, jnp.clip(r - lo, 0, n - 1))) for lo, n in ranges]
    specs += [pl.BlockSpec((None, d, d), lambda i, r: (r // per, 0, r % per)), rows, whole, rows]
    got = _call(name, body, (m // tm, nblk), specs, [rows, rows, whole],
                [jax.ShapeDtypeStruct((m, d), F32), jax.ShapeDtypeStruct((m, d), BF16),
                 jax.ShapeDtypeStruct((1, d), F32)],
                list(parts) + [w4] + list(norm), [pltpu.VMEM((tm, d), F32)], ("arbitrary", "arbitrary"), comm)
    return got if comm is None else (got[0], got[1])


def mm_tn_parts(name, xa, parts, tt, comm=None):
    t, k = xa.shape
    d = k
    pr = k // 2
    ranges, nblk = _part_ranges(parts, d)
    per = nblk // N_CHIPS

    def body(ins, outs, scratch):
        x_ref, acc = ins[0], scratch[0]
        jb, r = pl.program_id(0), pl.program_id(1)

        @pl.when(r == 0)
        def _():
            acc[...] = jnp.zeros_like(acc)

        for (lo, n), p_ref in zip(ranges, ins[1:]):
            @pl.when(jnp.logical_and(jb >= lo, jb < lo + n))
            def _(p_ref=p_ref):
                acc[...] += _dot(x_ref[...], p_ref[...], TN)

        @pl.when(r == t // tt - 1)
        def _():
            outs[0][0] = acc[:pr].astype(BF16)
            outs[0][1] = acc[pr:].astype(BF16)

    def part_spec(lo, n):
        return pl.BlockSpec((tt, d), lambda jb, r: (jnp.where(jnp.logical_and(jb >= lo, jb < lo + n), r, 0),
                                                    jnp.clip(jb - lo, 0, n - 1)))

    specs = [pl.BlockSpec((tt, k), lambda jb, r: (r, 0))] + [part_spec(lo, n) for lo, n in ranges]
    got = _call(name, body, (nblk, t // tt), specs,
                [pl.BlockSpec((None, 2, pr, d), lambda jb, r: (jb // per, 0, 0, jb % per))],
                [jax.ShapeDtypeStruct((N_CHIPS, 2, pr, per * d), BF16)], [xa] + list(parts),
                [pltpu.VMEM((k, d), F32)], ("parallel", "arbitrary"), comm)
    return got[0] if comm is None else (got[0][0], got[1])


def mm_nn(name, a, w, out_dtype, tm, res=None, alpha=1.0):
    m, k = a.shape
    n = w.shape[1]
    return _mm(name, a, w, jax.ShapeDtypeStruct((m, n), out_dtype), (m // tm, 1, 1),
               pl.BlockSpec((tm, k), lambda i, j, r: (i, 0)),
               pl.BlockSpec((k, n), lambda i, j, r: (0, 0)),
               pl.BlockSpec((tm, n), lambda i, j, r: (i, 0)), NN, None, res=res, alpha=alpha)


def mm_nn_stacked(name, a, w4, out_dtype, tm, tn, j0=0, nj=None, comm=None):
    m, k = a.shape
    cs = w4.shape[2]
    per = cs // tn
    nj = N_CHIPS * per - j0 if nj is None else nj
    return _mm(name, a, w4, jax.ShapeDtypeStruct((m, nj * tn), out_dtype), (m // tm, nj, 1),
               pl.BlockSpec((tm, k), lambda i, j, r: (i, 0)),
               pl.BlockSpec((None, k, tn), lambda i, j, r: ((j + j0) // per, 0, (j + j0) % per)),
               pl.BlockSpec((tm, tn), lambda i, j, r: (i, j)), NN, None, comm=comm)


def mm_nt(name, dy, w, out_dtype, tm, tko, norm=None, alpha=1.0):
    m, n = dy.shape
    k = w.shape[0]
    return _mm(name, dy, w, jax.ShapeDtypeStruct((m, k), out_dtype), (m // tm, k // tko, 1),
               pl.BlockSpec((tm, n), lambda i, j, r: (i, 0)),
               pl.BlockSpec((tko, n), lambda i, j, r: (j, 0)),
               pl.BlockSpec((tm, tko), lambda i, j, r: (i, j)), NT, None, norm=norm, alpha=alpha)


def mm_nt_stacked(name, dy, w4, out_dtype, tm, tn, comm=None, norm=None, alpha=1.0):
    m = dy.shape[-2]
    k, cs = w4.shape[1], w4.shape[2]
    per = cs // tn
    if dy.ndim == 3:
        dy_spec = pl.BlockSpec((None, tm, cs), lambda i, j, r: (r // 2, i, r % 2))
    else:
        dy_spec = pl.BlockSpec((tm, tn), lambda i, j, r: (i, r))
    return _mm(name, dy, w4, jax.ShapeDtypeStruct((m, k), out_dtype), (m // tm, 1, N_CHIPS * per), dy_spec,
               pl.BlockSpec((None, k, tn), lambda i, j, r: (r // per, 0, r % per)),
               pl.BlockSpec((tm, k), lambda i, j, r: (i, 0)), NT, (tm, k), comm=comm, norm=norm, alpha=alpha)


def mm_tn_rows(name, xa, dy, tt):
    t, k = xa.shape
    n = dy.shape[1]
    tkr = k if k * n * 4 <= ACC_BYTES else k // 2
    return _mm(name, xa, dy, jax.ShapeDtypeStruct((k, n), BF16), (k // tkr, 1, t // tt),
               pl.BlockSpec((tt, tkr), lambda i, j, r: (r, i)),
               pl.BlockSpec((tt, n), lambda i, j, r: (r, 0)),
               pl.BlockSpec((tkr, n), lambda i, j, r: (i, 0)), TN, (tkr, n))


def mm_tn_whole(name, xa, dy, tt):
    t, k = xa.shape
    n = dy.shape[1]
    return _mm(name, xa, dy, jax.ShapeDtypeStruct((k, n), BF16), (1, 1, t // tt),
               pl.BlockSpec((tt, k), lambda i, j, r: (r, 0)),
               pl.BlockSpec((tt, n), lambda i, j, r: (r, 0)),
               pl.BlockSpec((k, n), lambda i, j, r: (0, 0)), TN, (k, n))


def mm_tn_cols(name, xa, dy, tt, comm=None):
    t, k = xa.shape
    pr = k // 2
    if dy.ndim == 3:
        cs = dy.shape[2] // 2
        dy_spec = pl.BlockSpec((None, tt, cs), lambda i, j, r: (j // 2, r, j % 2))
    else:
        cs = dy.shape[1] // N_CHIPS
        dy_spec = pl.BlockSpec((tt, cs), lambda i, j, r: (r, j))
    return _mm(name, xa, dy, jax.ShapeDtypeStruct((N_CHIPS, 2, pr, cs), BF16), (1, N_CHIPS, t // tt),
               pl.BlockSpec((tt, k), lambda i, j, r: (r, 0)), dy_spec,
               pl.BlockSpec((None, 2, pr, cs), lambda i, j, r: (j, 0, 0, 0)), TN, (k, cs), comm=comm)


def _rows(tt, w, col=0):
    return pl.BlockSpec((tt, w), lambda i: (i, col))


def _whole(shape):
    return pl.BlockSpec(shape, lambda i: (0,) * len(shape))


def _rstd(h):
    return lax.rsqrt(jnp.mean(h * h, axis=-1, keepdims=True) + NORM_EPS)


def rms_fwd(name, h, g, tt, comm=None):
    t, d = h.shape

    def body(ins, outs, scratch):
        hv = ins[0][...]
        outs[0][...] = (hv * _rstd(hv) * ins[1][...]).astype(BF16)

    got = _call(name, body, (t // tt,), [_rows(tt, d), _whole((1, d))], [_rows(tt, d)],
                [jax.ShapeDtypeStruct((t, d), BF16)], [h, g], (), ("parallel",), comm)
    return got[0] if comm is None else (got[0][0], got[1])


def mix_out_fwd(name, gates, yc, ya, h, w, tt):
    t, d = yc.shape

    def body(g_ref, yc_ref, ya_ref, h_ref, w_ref, m_ref, o_ref):
        merged = (jax.nn.sigmoid(g_ref[:, :d].astype(F32)) * yc_ref[...].astype(F32)
                  + jax.nn.sigmoid(g_ref[:, d:].astype(F32)) * ya_ref[...].astype(F32)).astype(BF16)
        m_ref[...] = merged
        o_ref[...] = h_ref[...] + _dot(merged, w_ref[...], NN)

    return _pcall(body, name=name,
                  out_shape=(jax.ShapeDtypeStruct((t, d), BF16), jax.ShapeDtypeStruct((t, d), F32)),
                  grid=(t // tt,),
                  in_specs=[_rows(tt, 2 * d), _rows(tt, d), _rows(tt, d), _rows(tt, d), _whole((d, d))],
                  out_specs=(_rows(tt, d), _rows(tt, d)),
                  compiler_params=_params(("parallel",)))(gates, yc, ya, h, w)


def mix_out_bwd(name, dh, w, gates, yc, ya, tt):
    t, d = yc.shape

    def body(dh_ref, w_ref, g_ref, yc_ref, ya_ref, dyc_ref, dya_ref, dg_ref):
        dmv = _dot(dh_ref[...], w_ref[...], NT)
        sc = jax.nn.sigmoid(g_ref[:, :d].astype(F32))
        sa = jax.nn.sigmoid(g_ref[:, d:].astype(F32))
        dyc_ref[...] = (dmv * sc).astype(BF16)
        dya_ref[...] = (dmv * sa).astype(BF16)
        dg_ref[:, :d] = (dmv * yc_ref[...].astype(F32) * sc * (1.0 - sc)).astype(BF16)
        dg_ref[:, d:] = (dmv * ya_ref[...].astype(F32) * sa * (1.0 - sa)).astype(BF16)

    return _pcall(body, name=name,
                  out_shape=(jax.ShapeDtypeStruct((t, d), BF16), jax.ShapeDtypeStruct((t, d), BF16),
                             jax.ShapeDtypeStruct((t, 2 * d), BF16)),
                  grid=(t // tt,),
                  in_specs=[_rows(tt, d), _whole((d, d)), _rows(tt, 2 * d), _rows(tt, d), _rows(tt, d)],
                  out_specs=(_rows(tt, d), _rows(tt, d), _rows(tt, 2 * d)),
                  compiler_params=_params(("parallel",)))(dh, w, gates, yc, ya)


def _shift_down(cur, prev8, s):
    tt = cur.shape[0]
    rolled = pltpu.roll(cur, s, 0)
    row8 = lax.broadcasted_iota(jnp.int32, prev8.shape, 0)
    first8 = jnp.where(row8 < s, pltpu.roll(prev8, s, 0), rolled[:8])
    return jnp.concatenate([first8, rolled[8:]], axis=0) if tt > 8 else first8


def _shift_up(cur, next8, s):
    tt = cur.shape[0]
    rolled = pltpu.roll(cur, tt - s, 0)
    row8 = lax.broadcasted_iota(jnp.int32, next8.shape, 0)
    last8 = jnp.where(row8 >= 8 - s, pltpu.roll(next8, 8 - s, 0), rolled[tt - 8:])
    return jnp.concatenate([rolled[:tt - 8], last8], axis=0) if tt > 8 else last8


def _prev8(tt, d, col):
    return pl.BlockSpec((8, d), lambda i: (jnp.maximum(i * (tt // 8) - 1, 0), col))


def _next8(tt, d, col, t):
    return pl.BlockSpec((8, d), lambda i: (jnp.minimum((i + 1) * (tt // 8), t // 8 - 1), col))


def conv_fwd(name, cbx, cw8, tt):
    t, d3 = cbx.shape
    d = d3 // 3

    def body(cb_ref, cc_ref, cx_ref, pc_ref, px_ref, w_ref, o_ref):
        has_prev = (pl.program_id(0) > 0).astype(F32)
        cc = cc_ref[...] * cx_ref[...]
        prev = pc_ref[...] * px_ref[...] * has_prev
        w = w_ref[...]
        conv = w[0:1] * _shift_down(cc, prev, 2) + w[1:2] * _shift_down(cc, prev, 1) + w[2:3] * cc
        o_ref[...] = (cb_ref[...] * conv).astype(o_ref.dtype)

    return _pcall(body, name=name, out_shape=jax.ShapeDtypeStruct((t, d), BF16), grid=(t // tt,),
                  in_specs=[_rows(tt, d, 0), _rows(tt, d, 1), _rows(tt, d, 2), _prev8(tt, d, 1), _prev8(tt, d, 2),
                            _whole((8, d))],
                  out_specs=_rows(tt, d), compiler_params=_params(("parallel",)))(cbx, cbx, cbx, cbx, cbx, cw8)


def conv_bwd(name, dyc, cbx, cw8, tt):
    t, d3 = cbx.shape
    d = d3 // 3
    n = t // tt

    def body(dy_ref, cb_ref, cc_ref, cx_ref, pc_ref, px_ref, ndy_ref, ncb_ref, w_ref, o_ref, dw_ref):
        i = pl.program_id(0)
        has_prev = (i > 0).astype(F32)
        has_next = (i < n - 1).astype(F32)
        cb = cb_ref[...]
        cc = cc_ref[...] * cx_ref[...]
        prev = pc_ref[...] * px_ref[...] * has_prev
        w = w_ref[...]
        cc1 = _shift_down(cc, prev, 1)
        cc2 = _shift_down(cc, prev, 2)
        conv = w[0:1] * cc2 + w[1:2] * cc1 + w[2:3] * cc
        dyv = dy_ref[...]
        dconv = dyv * cb
        dnext = ndy_ref[...] * ncb_ref[...] * has_next
        dcc = w[2:3] * dconv + w[1:2] * _shift_up(dconv, dnext, 1) + w[0:1] * _shift_up(dconv, dnext, 2)
        o_ref[:, :d] = (dyv * conv).astype(BF16)
        o_ref[:, d:2 * d] = (dcc * cx_ref[...]).astype(BF16)
        o_ref[:, 2 * d:] = (dcc * cc_ref[...]).astype(BF16)

        @pl.when(i == 0)
        def _():
            dw_ref[...] = jnp.zeros_like(dw_ref)

        dw_ref[0:1, :] += jnp.sum(dconv * cc2, axis=0, keepdims=True)
        dw_ref[1:2, :] += jnp.sum(dconv * cc1, axis=0, keepdims=True)
        dw_ref[2:3, :] += jnp.sum(dconv * cc, axis=0, keepdims=True)

    return _pcall(body, name=name,
                  out_shape=(jax.ShapeDtypeStruct((t, d3), BF16), jax.ShapeDtypeStruct((8, d), F32)),
                  grid=(n,),
                  in_specs=[_rows(tt, d), _rows(tt, d, 0), _rows(tt, d, 1), _rows(tt, d, 2),
                            _prev8(tt, d, 1), _prev8(tt, d, 2), _next8(tt, d, 0, t), _next8(tt, d, 0, t),
                            _whole((8, d))],
                  out_specs=(_rows(tt, d3), _whole((8, d))),
                  compiler_params=_params(("arbitrary",)))(dyc, cbx, cbx, cbx, cbx, cbx, dyc, cbx, cw8)


def tail(name, h3, p, tgt, gp, gf, w_gate, w_proj, tt):
    t, d = h3.shape
    pd = p.shape[1]

    def body(h_ref, p_ref, tg_ref, gp_ref, gf_ref, wg_ref, wp_ref, np_ref, dh_ref, dpp_ref, dzg_ref, dgf_ref,
             loss_ref):
        hv = h_ref[...]
        npl = (hv * _rstd(hv) * gp_ref[...]).astype(BF16)
        np_ref[...] = npl
        pg = jax.nn.sigmoid(_dot(npl, wg_ref[...], NN))
        ppv = _dot(p_ref[...].astype(BF16), wp_ref[...], NN)
        h4 = hv + pg * ppv
        r4 = _rstd(h4)
        hn = h4 * r4
        gfv = gf_ref[...]
        err = hn * gfv - tg_ref[...]
        dy = err * (1.0 / d)
        gy = dy * gfv
        dh4 = r4 * (gy - hn * jnp.mean(gy * hn, axis=-1, keepdims=True))
        dh_ref[...] = dh4
        dpp_ref[...] = (dh4 * pg).astype(BF16)
        dzg_ref[...] = (dh4 * ppv * pg * (1.0 - pg)).astype(BF16)

        @pl.when(pl.program_id(0) == 0)
        def _():
            dgf_ref[...] = jnp.zeros_like(dgf_ref)
            loss_ref[...] = jnp.zeros_like(loss_ref)

        dgf_ref[...] += jnp.sum(dy * hn, axis=0, keepdims=True)
        tok = jnp.mean(err * err, axis=-1, keepdims=True)
        loss_ref[...] += 0.5 * jnp.sum(tok, axis=0, keepdims=True) * jnp.ones((1, loss_ref.shape[1]), F32)

    return _pcall(body, name=name,
                  out_shape=(jax.ShapeDtypeStruct((t, d), BF16), jax.ShapeDtypeStruct((t, d), F32),
                             jax.ShapeDtypeStruct((t, d), BF16), jax.ShapeDtypeStruct((t, d), BF16),
                             jax.ShapeDtypeStruct((1, d), F32), jax.ShapeDtypeStruct((1, d), F32)),
                  grid=(t // tt,),
                  in_specs=[_rows(tt, d), _rows(tt, pd), _rows(tt, d), _whole((1, d)), _whole((1, d)),
                            _whole((d, d)), _whole((pd, d))],
                  out_specs=(_rows(tt, d), _rows(tt, d), _rows(tt, d), _rows(tt, d), _whole((1, d)),
                             _whole((1, d))),
                  compiler_params=_params(("arbitrary",)))(h3, p, tgt, gp, gf, w_gate, w_proj)


SCALE = 1.0 / math.sqrt(HEAD_DIM)


def _log_stick(z):
    return -(jnp.maximum(z, 0.0) + jnp.log(1.0 + jnp.exp(-jnp.abs(z))))


def _tri_sum(x, tri):
    hi = x.astype(BF16)
    lo = (x - hi.astype(F32)).astype(BF16)
    return _dot(hi, tri, NN) + _dot(lo, tri, NN)


KEY_BLOCK = 128
NEAR = 3


def _sb_near(qs, jds, k_ref, below, upper):
    pairs = [(s, b) for s in range(len(qs)) for b in range(NEAR)]
    rows = {(s, b): _block_rows(jnp.maximum(jds[s] - b, 0), KEY_BLOCK) for s, b in pairs}
    z = {(s, b): _dot(qs[s], k_ref[rows[s, b], :], NT) * SCALE for s, b in pairs}
    lg = {(s, b): jnp.where(below, _log_stick(z[s, b]), 0.0) if b == 0 else _log_stick(z[s, b]) for s, b in pairs}
    cum = {(s, b): _tri_sum(lg[s, b], upper) for s, b in pairs}
    out, carries = [], []
    for s in range(len(qs)):
        c = cum[s, 0][:, 0:1]
        blocks = [(rows[s, 0], z[s, 0], jnp.exp(jnp.where(below, z[s, 0] + cum[s, 0], -1e30)))]
        for b in range(1, NEAR):
            live = jds[s] >= b
            blocks.append((rows[s, b], z[s, b], jnp.exp(z[s, b] + cum[s, b] + (c + jnp.where(live, 0.0, -1e30)))))
            c = c + jnp.where(live, cum[s, b][:, 0:1], 0.0)
        out.append(blocks)
        carries.append(c)
    return out, carries


def _sb_far(q, kj, upper, c):
    z = _dot(q, kj, NT) * SCALE
    cum = _tri_sum(_log_stick(z), upper)
    return z, jnp.exp(z + cum + c), c + cum[:, 0:1]


def _block_rows(j, size):
    return pl.ds(pl.multiple_of(j * size, size), size)


def _sweep_on(st):
    return jnp.logical_and(st[0] >= 0, jnp.max(st[1]) > -STICK_EXIT)


def attn_fwd(name, qkv, tq):
    t, d3 = qkv.shape
    d = d3 // 3
    nh = d // HEAD_DIM
    nq = t // tq
    tb = KEY_BLOCK
    nsub = tq // tb

    def body(q_ref, k_ref, v_ref, o_ref):
        i = pl.program_id(1)
        row = lax.broadcasted_iota(jnp.int32, (tb, tb), 0)
        col = lax.broadcasted_iota(jnp.int32, (tb, tb), 1)
        upper = (row >= col).astype(BF16)
        qs = [q_ref[s * tb:(s + 1) * tb, :] for s in range(nsub)]
        jds = [i * nsub + s for s in range(nsub)]
        near, carries = _sb_near(qs, jds, k_ref, col < row, upper)
        state = []
        for s in range(nsub):
            acc = jnp.zeros((tb, HEAD_DIM), F32)
            for rows, _, a in near[s]:
                acc = acc + _dot(a.astype(BF16), v_ref[rows, :], NN)
            state.append((qs[s], jds[s], carries[s], acc))
        for s, (q, jd, c, acc) in enumerate(state):

            def step(st, q=q):
                rows = _block_rows(st[0], tb)
                _, a, c2 = _sb_far(q, k_ref[rows, :], upper, st[1])
                return st[0] - 1, c2, st[2] + _dot(a.astype(BF16), v_ref[rows, :], NN)

            _, _, acc = lax.while_loop(_sweep_on, step, (jd - NEAR, c, acc))
            o_ref[s * tb:(s + 1) * tb, :] = acc.astype(o_ref.dtype)

    return _pcall(body, name=name, out_shape=jax.ShapeDtypeStruct((t, d), BF16), grid=(nh, nq),
                  in_specs=[pl.BlockSpec((tq, HEAD_DIM), lambda h, i: (i, h)),
                            pl.BlockSpec((t, HEAD_DIM), lambda h, i: (0, nh + h)),
                            pl.BlockSpec((t, HEAD_DIM), lambda h, i: (0, 2 * nh + h))],
                  out_specs=pl.BlockSpec((tq, HEAD_DIM), lambda h, i: (i, h)),
                  compiler_params=_params(("parallel", "arbitrary")))(qkv, qkv, qkv)


def attn_bwd(name, qkv, do, tq):
    t, d3 = qkv.shape
    d = d3 // 3
    nh = d // HEAD_DIM
    nq = t // tq
    tb = KEY_BLOCK
    nsub = tq // tb

    def body(q_ref, k_ref, v_ref, do_ref, dq_ref, dk_ref, dv_ref, dk_acc, dv_acc, g_buf, z_buf):
        i = pl.program_id(1)

        @pl.when(i == 0)
        def _():
            dk_acc[...] = jnp.zeros_like(dk_acc)
            dv_acc[...] = jnp.zeros_like(dv_acc)

        row = lax.broadcasted_iota(jnp.int32, (tb, tb), 0)
        col = lax.broadcasted_iota(jnp.int32, (tb, tb), 1)
        below = col < row
        upper = (row >= col).astype(BF16)
        lower = (row <= col).astype(BF16)

        qs = [q_ref[s * tb:(s + 1) * tb, :] for s in range(nsub)]
        dos = [do_ref[s * tb:(s + 1) * tb, :] for s in range(nsub)]
        jds = [i * nsub + s for s in range(nsub)]
        near, carries = _sb_near(qs, jds, k_ref, below, upper)
        da = [[_dot(dos[s], v_ref[rows, :], NT) for rows, _, _ in near[s]] for s in range(nsub)]
        state = []
        for s in range(nsub):
            kept = [(rows, z, da[s][b] * a) for b, (rows, z, a) in enumerate(near[s])]
            for rows, _, a in near[s]:
                dv_acc[rows, :] += _dot(a.astype(BF16), dos[s], TN)
            state.append((qs[s], dos[s], jds[s], carries[s], kept))

        carried = []
        for s, (q, dov, jd, c, kept) in enumerate(state):
            def step(st, s=s, q=q, dov=dov, jd=jd):
                j = st[0]
                rows = _block_rows(j, tb)
                z, a, c2 = _sb_far(q, k_ref[rows, :], upper, st[1])
                g_buf[jd - j] = _dot(dov, v_ref[rows, :], NT) * a
                z_buf[jd - j] = z
                dv_acc[rows, :] += _dot(a.astype(BF16), dov, TN)
                return j - 1, c2

            j_stop, _ = lax.while_loop(_sweep_on, step, (jd - NEAR, c))

            def far(j, st, s=s, q=q, jd=jd):
                run, dq = st
                rows = _block_rows(j, tb)
                g = g_buf[jd - j]
                dz = (g - jax.nn.sigmoid(z_buf[jd - j]) * (run + _tri_sum(g, lower))).astype(BF16)
                dk_acc[rows, :] += _dot(dz, q, TN)
                return run + jnp.sum(g, axis=1, keepdims=True), dq + _dot(dz, k_ref[rows, :], NN)

            carried.append(lax.fori_loop(j_stop + 1, jd - NEAR + 1, far,
                                         (jnp.zeros((tb, 1), F32), jnp.zeros((tb, HEAD_DIM), F32))))

        tri = [[_tri_sum(g, lower) for _, _, g in st[4]] for st in state]
        sig = [[jax.nn.sigmoid(z) for _, z, _ in st[4]] for st in state]
        for s, (q, dov, jd, c, kept) in enumerate(state):
            run, dq = carried[s]
            for b in reversed(range(NEAR)):
                rows, z, g = kept[b]
                dz = g - sig[s][b] * (run + tri[s][b])
                if b == 0:
                    dz = jnp.where(below, dz, 0.0)
                dz = dz.astype(BF16)
                dk_acc[rows, :] += _dot(dz, q, TN)
                dq = dq + _dot(dz, k_ref[rows, :], NN)
                if b:
                    run = run + jnp.sum(g, axis=1, keepdims=True)
            dq_ref[s * tb:(s + 1) * tb, :] = (dq * SCALE).astype(BF16)

        @pl.when(i == nq - 1)
        def _():
            dk_ref[...] = (dk_acc[...] * SCALE).astype(BF16)
            dv_ref[...] = dv_acc[...].astype(BF16)

    blk = pl.BlockSpec((tq, HEAD_DIM), lambda h, i: (i, h))
    col_h = pl.BlockSpec((t, HEAD_DIM), lambda h, i: (0, h))
    out = jax.ShapeDtypeStruct((t, d), BF16)
    return _pcall(body, name=name, out_shape=(out, out, out), grid=(nh, nq),
                  in_specs=[blk,
                            pl.BlockSpec((t, HEAD_DIM), lambda h, i: (0, nh + h)),
                            pl.BlockSpec((t, HEAD_DIM), lambda h, i: (0, 2 * nh + h)),
                            blk],
                  out_specs=(blk, col_h, col_h),
                  scratch_shapes=[pltpu.VMEM((t, HEAD_DIM), F32), pltpu.VMEM((t, HEAD_DIM), F32),
                                  pltpu.VMEM((t // tb, tb, tb), F32), pltpu.VMEM((t // tb, tb, tb), F32)],
                  compiler_params=_params(("parallel", "arbitrary")))(qkv, qkv, qkv, do)


def _place():
    x, y, c = lax.axis_index("x"), lax.axis_index("y"), lax.axis_index("c")
    chips = [(1 - x, y), (x, 1 - y), (1 - x, 1 - y)]
    return x, y, c, chips


def _remote(src, dst, send_sem, recv_sem, dev):
    return pltpu.make_async_remote_copy(src_ref=src, dst_ref=dst, send_sem=send_sem, recv_sem=recv_sem,
                                        device_id=dev, device_id_type=MESH)


def place_shard(name, w, chip):
    r, cdim = w.shape
    tr = _tile(r, max(BF16_ROWS, (1 << 19) // cdim), BF16_ROWS)

    def body(chip_ref, w_ref, o_ref):
        o_ref[...] = w_ref[...].astype(BF16)

    spec = pltpu.PrefetchScalarGridSpec(
        num_scalar_prefetch=1, grid=(r // tr,),
        in_specs=[pl.BlockSpec((tr, cdim), lambda i, s: (i, 0))],
        out_specs=pl.BlockSpec((None, tr, cdim), lambda i, s: (s[0], i, 0)))
    return _pcall(body, name=name, out_shape=jax.ShapeDtypeStruct((N_CHIPS, r, cdim), BF16), grid_spec=spec,
                  compiler_params=_params(("parallel",)))(chip, w)


class Comm:
    def __init__(self, ins, outs, aliases, sems, first, mid, last):
        self.ins, self.outs, self.aliases, self.sems = list(ins), list(outs), dict(aliases), list(sems)
        self.first, self.mid, self.last = first, mid, last


def run_comm(name, comm):
    ni, no = len(comm.ins), len(comm.outs)

    def body(*refs):
        ins, outs, sems = refs[:ni], refs[ni:ni + no], refs[ni + no:]
        comm.first(ins, outs, sems)
        comm.mid(ins, outs, sems)
        comm.last(ins, outs, sems)

    return _pcall(body, name=name, out_shape=comm.outs, in_specs=[ANY] * ni, out_specs=[ANY] * no,
                  input_output_aliases=comm.aliases, scratch_shapes=comm.sems, compiler_params=_params())(*comm.ins)


def gather_comm(bufs):
    n = len(bufs)

    def half(out, w, which):
        pr = out[w].shape[1] // 2
        return pl.ds(pl.multiple_of(which * pr, BF16_ROWS), pr)

    def first(ins, out, sems):
        isend, irecv, _, _ = sems
        x, y, c, chips = _place()
        for w in range(n):
            mine = out[w].at[2 * x + y, half(out, w, c)]
            for j, (cx, cy) in enumerate(chips):
                _remote(mine, mine, isend.at[3 * w + j], irecv.at[3 * w + j], (cx, cy, c)).start()

    def mid(ins, out, sems):
        isend, irecv, dsend, drecv = sems
        x, y, c, chips = _place()
        sib = (x, y, 1 - c)
        for w in range(n):
            for j, (cx, cy) in enumerate(chips):
                landed = out[w].at[2 * cx + cy, half(out, w, c)]
                _remote(landed, landed, isend.at[3 * w + j], irecv.at[3 * w + j], sib).wait_recv()
                _remote(landed, landed, dsend.at[3 * w + j], drecv.at[3 * w + j], sib).start()

    def last(ins, out, sems):
        isend, irecv, dsend, drecv = sems
        x, y, c, chips = _place()
        sib = (x, y, 1 - c)
        for w in range(n):
            for j, (cx, cy) in enumerate(chips):
                landed = out[w].at[2 * cx + cy, half(out, w, 1 - c)]
                _remote(landed, landed, dsend.at[3 * w + j], drecv.at[3 * w + j], sib).wait_recv()
        for w in range(n):
            sent = out[w].at[0, half(out, w, c)]
            for j in range(3):
                _remote(sent, sent, isend.at[3 * w + j], irecv.at[3 * w + j], sib).wait_send()
                _remote(sent, sent, dsend.at[3 * w + j], drecv.at[3 * w + j], sib).wait_send()

    return Comm(bufs, [jax.ShapeDtypeStruct(s.shape, s.dtype) for s in bufs], {w: w for w in range(n)},
                [pltpu.SemaphoreType.DMA((3 * n,))] * 4, first, mid, last)


def _nothing(ins, outs, sems):
    return None


def exchange_comm(pieces):
    n = len(pieces)

    def copies(src, out, sems):
        x, y, c, _ = _place()
        return [_remote(src[w].at[k, 1 - c], out[w].at[k], sems[0].at[N_CHIPS * w + k], sems[1].at[N_CHIPS * w + k],
                        (x, y, 1 - c)) for w in range(n) for k in range(N_CHIPS)]

    def first(src, out, sems):
        for cp in copies(src, out, sems):
            cp.start()

    def last(src, out, sems):
        for cp in copies(src, out, sems):
            cp.wait()

    return Comm(pieces, [jax.ShapeDtypeStruct((N_CHIPS,) + s.shape[2:], s.dtype) for s in pieces], {},
                [pltpu.SemaphoreType.DMA((N_CHIPS * n,))] * 2, first, _nothing, last)


def scatter_comm(parts):
    n = len(parts)

    def copies(src, out, sems):
        x, y, c, chips = _place()
        return [_remote(src[w].at[2 * cx + cy], out[w].at[j], sems[0].at[3 * w + j], sems[1].at[3 * w + j], (cx, cy, c))
                for w in range(n) for j, (cx, cy) in enumerate(chips)]

    def first(src, out, sems):
        for cp in copies(src, out, sems):
            cp.start()

    def last(src, out, sems):
        for cp in copies(src, out, sems):
            cp.wait()

    return Comm(parts, [jax.ShapeDtypeStruct((3,) + s.shape[1:], s.dtype) for s in parts], {},
                [pltpu.SemaphoreType.DMA((3 * n,))] * 2, first, _nothing, last)


def share_comm(halves):
    n = len(halves)

    def first(ins, buf, sems):
        x, y, c, _ = _place()
        for w in range(n):
            _remote(buf[w].at[c], buf[w].at[c], sems[0].at[w], sems[1].at[w], (x, y, 1 - c)).start()

    def last(ins, buf, sems):
        x, y, c, _ = _place()
        for w in range(n):
            landed = buf[w].at[1 - c]
            _remote(landed, landed, sems[0].at[w], sems[1].at[w], (x, y, 1 - c)).wait_recv()
        for w in range(n):
            _remote(buf[w].at[c], buf[w].at[c], sems[0].at[w], sems[1].at[w], (x, y, 1 - c)).wait_send()

    return Comm(halves, [jax.ShapeDtypeStruct(s.shape, s.dtype) for s in halves], {w: w for w in range(n)},
                [pltpu.SemaphoreType.DMA((n,))] * 2, first, _nothing, last)


def gather_small(name, blk, reduce):
    r, cdim = blk.shape

    def body(in_ref, out_ref, *rest):
        if reduce:
            buf, send_sem, recv_sem = rest
        else:
            buf = out_ref
            send_sem, recv_sem = rest
        x, y, c, _ = _place()
        me = 4 * x + 2 * y + c
        buf[me] = in_ref[...]
        peers = []
        for dx in range(2):
            for dy in range(2):
                for dc in range(2):
                    if dx or dy or dc:
                        peers.append((dx, dy, dc))
        copies = []
        for s, (dx, dy, dc) in enumerate(peers):
            cp = _remote(in_ref, buf.at[me], send_sem.at[s], recv_sem.at[s],
                         ((1 - x if dx else x), (1 - y if dy else y), (1 - c if dc else c)))
            cp.start()
            copies.append(cp)
        for s, (dx, dy, dc) in enumerate(peers):
            px, py, pc_ = (1 - x if dx else x), (1 - y if dy else y), (1 - c if dc else c)
            landed = buf.at[4 * px + 2 * py + pc_]
            _remote(landed, landed, send_sem.at[s], recv_sem.at[s], (x, y, c)).wait_recv()
        for cp in copies:
            cp.wait_send()
        if reduce:
            tot = buf[0]
            for s in range(1, N_DEV):
                tot = tot + buf[s]
            out_ref[...] = tot

    vm = pl.BlockSpec(memory_space=pltpu.VMEM)
    out_shape = jax.ShapeDtypeStruct((r, cdim) if reduce else (N_DEV, r, cdim), F32)
    scratch = ([pltpu.VMEM((N_DEV, r, cdim), F32)] if reduce else []) + [pltpu.SemaphoreType.DMA((N_DEV - 1,))] * 2
    return _pcall(body, name=name, out_shape=out_shape, in_specs=[vm], out_specs=vm, scratch_shapes=scratch,
                  compiler_params=_params())(blk)


def sum_cores(name, own, got, place):
    _, _, pr, pc = own.shape
    tr = _tile(pr, max(BF16_ROWS, (1 << 19) // pc), BF16_ROWS)

    def body(place_ref, own_ref, got_ref, o_ref):
        o_ref[...] = (own_ref[...].astype(F32) + got_ref[...].astype(F32)).astype(o_ref.dtype)

    spec = pltpu.PrefetchScalarGridSpec(
        num_scalar_prefetch=1, grid=(N_CHIPS, pr // tr),
        in_specs=[pl.BlockSpec((None, None, tr, pc), lambda k, i, s: (k, s[1], i, 0)),
                  pl.BlockSpec((None, tr, pc), lambda k, i, s: (k, i, 0))],
        out_specs=pl.BlockSpec((None, tr, pc), lambda k, i, s: (k, i, 0)))
    return _pcall(body, name=name, out_shape=jax.ShapeDtypeStruct((N_CHIPS, pr, pc), BF16), grid_spec=spec,
                  compiler_params=_params(("parallel", "parallel")))(place, own, got)


def sum_chips(name, part, got, place):
    _, pr, pc = part.shape
    tr = _tile(pr, max(BF16_ROWS, (1 << 18) // pc), BF16_ROWS)

    def body(place_ref, part_ref, got_ref, o_ref):
        tot = part_ref[...].astype(F32)
        for j in range(3):
            tot = tot + got_ref[j].astype(F32)
        o_ref[...] = tot

    spec = pltpu.PrefetchScalarGridSpec(
        num_scalar_prefetch=1, grid=(pr // tr,),
        in_specs=[pl.BlockSpec((None, tr, pc), lambda i, s: (s[0], i, 0)),
                  pl.BlockSpec((3, tr, pc), lambda i, s: (0, i, 0))],
        out_specs=pl.BlockSpec((None, tr, pc), lambda i, s: (s[1], i, 0)))
    return _pcall(body, name=name, out_shape=jax.ShapeDtypeStruct((2, pr, pc), F32), grid_spec=spec,
                  compiler_params=_params(("parallel",)))(place, part, got)


def adamw(name, w, g, m, v):
    rows, cols = w.shape
    tr = _tile(rows, max(8, (1 << 18) // cols))
    c1 = 1.0 / (1.0 - ADAM_B1 ** ADAM_STEP)
    c2 = 1.0 / (1.0 - ADAM_B2 ** ADAM_STEP)

    def body(w_ref, g_ref, m_ref, v_ref, d_ref, nm_ref, nv_ref):
        gv = g_ref[...]
        nm = ADAM_B1 * m_ref[...] + (1.0 - ADAM_B1) * gv
        nv = ADAM_B2 * v_ref[...] + (1.0 - ADAM_B2) * (gv * gv)
        nm_ref[...] = nm
        nv_ref[...] = nv
        d_ref[...] = -ADAM_LR * ((nm * c1) / (jnp.sqrt(nv * c2) + ADAM_EPS) + ADAM_WD * w_ref[...])

    spec = pl.BlockSpec((tr, cols), lambda i: (i, 0))
    sds = jax.ShapeDtypeStruct((rows, cols), F32)
    return _pcall(body, name=name, out_shape=(sds, sds, sds), grid=(rows // tr,),
                  in_specs=[spec] * 4, out_specs=(spec, spec, spec),
                  compiler_params=_params(("parallel",)))(w, g, m, v)


MATS = ["ffn1_w_in", "ffn1_w_out", "w_mix_in", "w_conv_out", "w_attn_out", "w_mix_out", "ffn2_w_in", "ffn2_w_out",
        "w_ple_gate", "w_ple_proj"]
COL_SHARDED = {"ffn1_w_in", "w_mix_in", "ffn2_w_in", "w_ple_proj"}
NORMS = ["ffn1_norm", "mix_norm", "ffn2_norm", "ple_norm", "final_norm"]
WEIGHTS = ["ffn1_norm", "ffn1_w_in", "ffn1_w_out", "mix_norm", "w_mix_in", "conv_w", "w_conv_out", "w_attn_out",
           "w_mix_out", "ffn2_norm", "ffn2_w_in", "ffn2_w_out", "ple_norm", "w_ple_gate", "w_ple_proj", "final_norm"]


def _pad_rows(a, rows):
    return jnp.concatenate([a, jnp.zeros((rows - a.shape[0],) + a.shape[1:], a.dtype)], axis=0)


def _step(x, p, tgt, w, m, v):
    t, d = x.shape
    tt = _tile(t, 256)
    tm = _tile(t, 512)
    tm2 = _tile(t, 1024)
    tq = _tile(t, 1024)

    chip = 2 * lax.axis_index("x") + lax.axis_index("y")
    place = jnp.stack([chip, lax.axis_index("c")]).astype(jnp.int32)

    placed = {k: place_shard("place_" + k, w[k], place) for k in MATS}
    full = {}

    def keep(names, bufs):
        for k, buf in zip(names, bufs):
            full[k] = buf if k in COL_SHARDED else buf.reshape(-1, buf.shape[2])

    def gather_of(names):
        return gather_comm([placed[k] for k in names])

    cw_all = gather_small("gather_conv_w", _pad_rows(w["conv_w"], 8), False)
    cw8 = jnp.concatenate([cw_all[2 * k] for k in range(N_CHIPS)], axis=1)
    g1, gm, g2, gp, gf = (w[k].reshape(1, d) for k in NORMS)

    def ffn_fwd(tag, h, g, first, w_in_name, w_out_name, riders):
        if first:
            n, bufs = rms_fwd(tag + "_norm", h, g, tt, comm=gather_of(first))
            keep(first, bufs)
        else:
            n = rms_fwd(tag + "_norm", h, g, tt)
        w_in = full[w_in_name]
        if riders:
            (a, s), bufs = ffn_in_act(tag + "_in", n, w_in, tm, comm=gather_of(riders))
            keep(riders, bufs)
        else:
            a, s = ffn_in_act(tag + "_in", n, w_in, tm)
        return n, a, s, mm_nn(tag + "_out", s, full[w_out_name], F32, tm, res=h, alpha=0.5)

    n1, a1, s1, h1 = ffn_fwd("ffn1", x, g1, ["ffn1_w_in"], "ffn1_w_in", "ffn1_w_out", ["ffn1_w_out", "w_mix_in"])
    u = rms_fwd("mix_norm", h1, gm, tt)
    wmix = full["w_mix_in"]
    riders = [["w_conv_out", "w_attn_out", "w_mix_out"], ["ffn2_w_in"], ["ffn2_w_out", "w_ple_gate", "w_ple_proj"]]
    cbx, bufs = mm_nn_stacked("mix_in_conv", u, wmix, F32, tm2, d, 0, 3, comm=gather_of(riders[0]))
    keep(riders[0], bufs)
    qkv, bufs = mm_nn_stacked("mix_in_qkv", u, wmix, BF16, tm2, d, 3, 3, comm=gather_of(riders[1]))
    keep(riders[1], bufs)
    gates, bufs = mm_nn_stacked("mix_in_gates", u, wmix, BF16, tm2, d, 6, 2, comm=gather_of(riders[2]))
    keep(riders[2], bufs)
    wpp = full["w_ple_proj"]
    wpp = jnp.transpose(wpp, (1, 0, 2)).reshape(wpp.shape[1], -1)
    ycin = conv_fwd("conv", cbx, cw8, tt)
    y_conv = mm_nn("conv_out", ycin, full["w_conv_out"], BF16, tm)
    o = attn_fwd("attn", qkv, tq)
    y_attn = mm_nn("attn_out", o, full["w_attn_out"], BF16, tm)
    merged, h2 = mix_out_fwd("mix_out", gates, y_conv, y_attn, h1, full["w_mix_out"], tm)
    n2, a2, s2, h3 = ffn_fwd("ffn2", h2, g2, [], "ffn2_w_in", "ffn2_w_out", [])

    pieces, chip_sums, halves = {}, {}, {}

    def as_pieces(k):
        pc = pieces[k]
        return pc if k in COL_SHARDED else pc.reshape(N_CHIPS, 2, pc.shape[0] // (2 * N_CHIPS), pc.shape[1])

    def sum_siblings(tag, names):
        pcs = [as_pieces(k) for k in names]
        got = run_comm("exchange_" + tag, exchange_comm(pcs))
        for k, a, b in zip(names, pcs, got):
            chip_sums[k] = sum_cores("sum_cores_" + k, a, b, place)

    def scatter_of(names):
        return scatter_comm([chip_sums[k] for k in names])

    def sum_landed(names, landed):
        for k, b in zip(names, landed):
            halves[k] = sum_chips("sum_chips_" + k, chip_sums[k], b, place)

    npl, dh4, dpp, dzg, dgf, loss_row = tail("tail", h3, p, tgt, gp, gf, full["w_ple_gate"], wpp, tt)
    dwpp = mm_tn_whole("ple_proj_dw", p, dpp, tm2)
    pieces["w_ple_proj"] = jnp.transpose(dwpp.reshape(2, p.shape[1] // 2, N_CHIPS, d // N_CHIPS), (2, 0, 1, 3))
    pieces["w_ple_gate"] = mm_tn_rows("ple_gate_dw", npl, dzg, tm2)
    dh3, df2, dgp = mm_nt("ple_gate_dx", dzg, full["w_ple_gate"], F32, tm, d, norm=(h3, gp, dh4), alpha=0.5)
    w_in, w_out = full["ffn2_w_in"], full["ffn2_w_out"]
    pieces["ffn2_w_out"] = mm_tn_rows("ffn2_dwout", s2, df2, tm2)
    da2 = ffn_ds_dact("ffn2_ds", df2, w_out, a2, tm)
    pieces["ffn2_w_in"] = mm_tn_cols("ffn2_dwin", n2, da2, tm2)
    dh2, dh2b, dg2 = mm_nt_stacked("ffn2_dn", da2, w_in, F32, tm2, w_in.shape[2], norm=(h2, g2, dh3))
    pieces["w_mix_out"] = mm_tn_rows("mix_out_dw", merged, dh2b, tm2)
    dyc, dya, dgates = mix_out_bwd("mix_out_dx", dh2b, full["w_mix_out"], gates, y_conv, y_attn, tm)
    pieces["w_conv_out"] = mm_tn_rows("conv_out_dw", ycin, dyc, tm2)
    dycin = mm_nt("conv_out_dx", dyc, full["w_conv_out"], F32, tm, d)
    dcbx, dcw8 = conv_bwd("conv_bwd", dycin, cbx, cw8, tt)
    pieces["w_attn_out"] = mm_tn_rows("attn_out_dw", o, dya, tm2)
    do = mm_nt("attn_out_dx", dya, full["w_attn_out"], BF16, tm, d)
    dq, dk, dv = attn_bwd("attn_bwd", qkv, do, tq)
    dmix = [dcbx, dq, dk, dv, dgates]
    early = ["ffn2_w_in", "ffn2_w_out", "w_ple_gate", "w_ple_proj", "w_mix_out", "w_conv_out", "w_attn_out"]
    swap = exchange_comm([as_pieces(k) for k in early])
    pieces["w_mix_in"], got = mm_tn_parts("mix_in_dw", u, dmix, tm2, comm=swap)
    for k, a, b in zip(early, swap.ins, got):
        chip_sums[k] = sum_cores("sum_cores_" + k, a, b, place)
    (dh1, df1, dgm), landed = mm_nt_parts("mix_in_dx", dmix, wmix, tm, (h1, gm, dh2), 0.5, comm=scatter_of(early))
    sum_landed(early, landed)
    sum_siblings("mix", ["w_mix_in"])
    w_in, w_out = full["ffn1_w_in"], full["ffn1_w_out"]
    pieces["ffn1_w_out"] = mm_tn_rows("ffn1_dwout", s1, df1, tm2)
    da1 = ffn_ds_dact("ffn1_ds", df1, w_out, a1, tm)
    pieces["ffn1_w_in"], landed = mm_tn_cols("ffn1_dwin", n1, da1, tm2, comm=scatter_of(["w_mix_in"]))
    sum_landed(["w_mix_in"], landed)
    late = ["ffn1_w_in", "ffn1_w_out"]
    sum_siblings("late", late)
    (dx, _, dg1), landed = mm_nt_stacked("ffn1_dn", da1, w_in, F32, tm2, w_in.shape[2], comm=scatter_of(late),
                                         norm=(x, g1, dh1))
    sum_landed(late, landed)

    shared = run_comm("share_halves", share_comm([halves[k] for k in MATS]))
    grad, delta, new_m, new_v = {}, {}, {}, {}
    for k, sh in zip(MATS, shared):
        grad[k] = sh.reshape(w[k].shape)
        delta[k], new_m[k], new_v[k] = adamw("adamw_" + k, w[k], grad[k], m[k], v[k])

    small = jnp.concatenate([dg1, dgm, dg2, dgp, dgf, dcw8[:3], loss_row, jnp.zeros((7, d), F32)], axis=0)
    tot = gather_small("sum_small", small, True)
    loss = tot[8, 0]
    norm_w = jnp.concatenate([w[k].reshape(1, d) for k in NORMS] + [jnp.zeros((3, d), F32)], axis=0)
    norm_m = jnp.concatenate([m[k].reshape(1, d) for k in NORMS] + [jnp.zeros((3, d), F32)], axis=0)
    norm_v = jnp.concatenate([v[k].reshape(1, d) for k in NORMS] + [jnp.ones((3, d), F32)], axis=0)
    norm_g = jnp.concatenate([tot[0:5], jnp.zeros((3, d), F32)], axis=0)
    nd, nm, nv = adamw("adamw_norms", norm_w, norm_g, norm_m, norm_v)
    for r, k in enumerate(NORMS):
        grad[k] = norm_g[r].reshape(w[k].shape)
        delta[k], new_m[k], new_v[k] = (a[r].reshape(w[k].shape) for a in (nd, nm, nv))
    cs = d // N_CHIPS
    gcw = lax.dynamic_slice(tot[5:8], (0, chip * cs), (3, cs))
    cd, cm, cv = adamw("adamw_conv_w", _pad_rows(w["conv_w"], 8), _pad_rows(gcw, 8), _pad_rows(m["conv_w"], 8),
                       jnp.concatenate([v["conv_w"], jnp.ones((5, cs), F32)], axis=0))
    grad["conv_w"], delta["conv_w"], new_m["conv_w"], new_v["conv_w"] = gcw, cd[:3], cm[:3], cv[:3]
    return loss, dx, grad, delta, new_m, new_v


def kernel(x, p, ffn1_norm, ffn1_w_in, ffn1_w_out, mix_norm, w_mix_in, conv_w, w_conv_out, w_attn_out, w_mix_out, ffn2_norm, ffn2_w_in, ffn2_w_out, ple_norm, w_ple_gate, w_ple_proj, final_norm, loss_target, m_ffn1_norm, m_ffn1_w_in, m_ffn1_w_out, m_mix_norm, m_w_mix_in, m_conv_w, m_w_conv_out, m_w_attn_out, m_w_mix_out, m_ffn2_norm, m_ffn2_w_in, m_ffn2_w_out, m_ple_norm, m_w_ple_gate, m_w_ple_proj, m_final_norm, v_ffn1_norm, v_ffn1_w_in, v_ffn1_w_out, v_mix_norm, v_w_mix_in, v_conv_w, v_w_conv_out, v_w_attn_out, v_w_mix_out, v_ffn2_norm, v_ffn2_w_in, v_ffn2_w_out, v_ple_norm, v_w_ple_gate, v_w_ple_proj, v_final_norm):
    ws = (ffn1_norm, ffn1_w_in, ffn1_w_out, mix_norm, w_mix_in, conv_w, w_conv_out, w_attn_out, w_mix_out, ffn2_norm,
          ffn2_w_in, ffn2_w_out, ple_norm, w_ple_gate, w_ple_proj, final_norm)
    ms = (m_ffn1_norm, m_ffn1_w_in, m_ffn1_w_out, m_mix_norm, m_w_mix_in, m_conv_w, m_w_conv_out, m_w_attn_out,
          m_w_mix_out, m_ffn2_norm, m_ffn2_w_in, m_ffn2_w_out, m_ple_norm, m_w_ple_gate, m_w_ple_proj, m_final_norm)
    vs = (v_ffn1_norm, v_ffn1_w_in, v_ffn1_w_out, v_mix_norm, v_w_mix_in, v_conv_w, v_w_conv_out, v_w_attn_out,
          v_w_mix_out, v_ffn2_norm, v_ffn2_w_in, v_ffn2_w_out, v_ple_norm, v_w_ple_gate, v_w_ple_proj, v_final_norm)
    assert x.shape[0] == 1 and p.shape[:2] == (1, 1), "one sequence and one layer per device"

    def strip(a):
        return a[0] if a.ndim == 3 or (a.ndim == 2 and a.shape[0] == 1) else a

    w = {k: strip(a) for k, a in zip(WEIGHTS, ws)}
    m = {k: strip(a) for k, a in zip(WEIGHTS, ms)}
    v = {k: strip(a) for k, a in zip(WEIGHTS, vs)}
    loss, dx, grad, delta, new_m, new_v = _step(x[0], p[0, 0], loss_target[0], w, m, v)
    shapes = [a.shape for a in ws]
    outs = [loss, dx[None]]
    for res in (grad, delta, new_m, new_v):
        outs += [res[k].reshape(s) for k, s in zip(WEIGHTS, shapes)]
    return tuple(outs)
```

```python
import functools
import math

import jax
import jax.numpy as jnp
from jax import lax
from jax.experimental import pallas as pl
from jax.experimental.pallas import tpu as pltpu

F32 = jnp.float32
BF16 = jnp.bfloat16
MESH = pl.DeviceIdType.MESH
ANY = pl.BlockSpec(memory_space=pl.ANY)

HEAD_DIM = 128
NORM_EPS = 1e-6
N_CHIPS = 4
N_DEV = 8
BF16_ROWS = 16
VMEM_LIMIT = 56 * 1024 * 1024
ACC_BYTES = 8 * 1024 * 1024
STICK_EXIT = 110.0

ADAM_LR = 0.001
ADAM_B1 = 0.9
ADAM_B2 = 0.999
ADAM_EPS = 1e-08
ADAM_WD = 0.01
ADAM_STEP = 10

NN = (((1,), (0,)), ((), ()))
NT = (((1,), (1,)), ((), ()))
TN = (((0,), (0,)), ((), ()))


def _params(sem=None, **kw):
    if sem is not None:
        kw["dimension_semantics"] = sem
    return pltpu.CompilerParams(vmem_limit_bytes=VMEM_LIMIT, **kw)


def _pcall(body, **kw):
    return pl.pallas_call(body, **kw)


def _tile(n, pref, mult=8):
    best = None
    for d in range(mult, min(n, pref) + 1, mult):
        if n % d == 0:
            best = d
    return best if best is not None else n


def _dot(a, b, dims):
    return lax.dot_general(a, b, dims, preferred_element_type=F32)


def _call(name, body, grid, in_specs, out_specs, out_shape, args, scratch=(), sem=None, comm=None):
    n_in, n_out, n_sc = len(in_specs), len(out_specs), len(scratch)
    if comm is None:
        def plain(*refs):
            body(refs[:n_in], refs[n_in:n_in + n_out], refs[n_in + n_out:])

        return _pcall(plain, name=name, out_shape=list(out_shape), grid=grid, in_specs=list(in_specs),
                      out_specs=list(out_specs), scratch_shapes=list(scratch), compiler_params=_params(sem))(*args)
    n_cin, n_cout = len(comm.ins), len(comm.outs)
    steps = math.prod(grid)

    def hosted(*refs):
        ins, c_ins = refs[:n_in], refs[n_in:n_in + n_cin]
        outs = refs[n_in + n_cin:n_in + n_cin + n_out]
        c_outs = refs[n_in + n_cin + n_out:n_in + n_cin + n_out + n_cout]
        rest = refs[n_in + n_cin + n_out + n_cout:]
        sems = rest[n_sc:]
        step = pl.program_id(0)
        for ax in range(1, len(grid)):
            step = step * grid[ax] + pl.program_id(ax)

        @pl.when(step == 0)
        def _():
            comm.first(c_ins, c_outs, sems)

        body(ins, outs, rest[:n_sc])

        @pl.when(step == (3 * steps) // 4)
        def _():
            comm.mid(c_ins, c_outs, sems)

        @pl.when(step == steps - 1)
        def _():
            comm.last(c_ins, c_outs, sems)

    res = _pcall(hosted, name=name, out_shape=list(out_shape) + comm.outs, grid=grid,
                 in_specs=list(in_specs) + [ANY] * n_cin, out_specs=list(out_specs) + [ANY] * n_cout,
                 input_output_aliases={n_in + k: n_out + v for k, v in comm.aliases.items()},
                 scratch_shapes=list(scratch) + comm.sems,
                 compiler_params=_params(("arbitrary",) * len(grid)))(*args, *comm.ins)
    return list(res[:n_out]), list(res[n_out:])


NORM_CHUNK = 256


def _norm_bwd_tile(read_dn, rows, first, h_ref, g_ref, dr_ref, dh_ref, dhb_ref, dg_ref, alpha):
    @pl.when(first)
    def _():
        dg_ref[...] = jnp.zeros_like(dg_ref)

    gv = g_ref[...]
    tot = jnp.zeros_like(gv)
    for c0 in range(0, rows, NORM_CHUNK):
        sl = slice(c0, min(rows, c0 + NORM_CHUNK))
        hv = h_ref[sl, :]
        rs = _rstd(hv)
        hn = hv * rs
        dnv = read_dn(sl)
        gy = dnv * gv
        dh = dr_ref[sl, :] + rs * (gy - hn * jnp.mean(gy * hn, axis=-1, keepdims=True))
        dh_ref[sl, :] = dh
        dhb_ref[sl, :] = (alpha * dh).astype(BF16)
        tot = tot + jnp.sum(dnv * hn, axis=0, keepdims=True)
    dg_ref[...] += tot


def _mm(name, a, b, out_sds, grid, a_spec, b_spec, o_spec, dims, acc_shape, res=None, alpha=1.0, comm=None,
        norm=None):
    nk = grid[2]

    def body(ins, outs, scratch):
        a_ref, b_ref = ins[:2]
        r_ref = ins[2] if res is not None else None
        o_ref = outs[0]

        def finish(read):
            if norm is not None:
                first = jnp.logical_and(pl.program_id(0) == 0, pl.program_id(1) == 0)
                _norm_bwd_tile(read, o_ref.shape[0], first, *ins[2:5], *outs, alpha)
                return
            r = read(slice(None))
            if alpha != 1.0:
                r = r * alpha
            if r_ref is not None:
                r = r_ref[...] + r
            if len(o_ref.shape) == 3:
                half = o_ref.shape[1]
                o_ref[0] = r[:half].astype(o_ref.dtype)
                o_ref[1] = r[half:].astype(o_ref.dtype)
            else:
                o_ref[...] = r.astype(o_ref.dtype)

        if nk == 1:
            part = _dot(a_ref[...].astype(BF16), b_ref[...].astype(BF16), dims)
            finish(lambda sl: part[sl])
        else:
            acc_ref = scratch[0]
            kk = pl.program_id(2)

            @pl.when(kk == 0)
            def _():
                acc_ref[...] = jnp.zeros_like(acc_ref)

            acc_ref[...] += _dot(a_ref[...].astype(BF16), b_ref[...].astype(BF16), dims)

            @pl.when(kk == nk - 1)
            def _():
                finish(lambda sl: acc_ref[sl, :])

    in_specs = [a_spec, b_spec]
    args = [a, b]
    out_specs, out_shape = [o_spec], [out_sds]
    sem = ("parallel", "parallel", "arbitrary")
    if res is not None:
        in_specs.append(o_spec)
        args.append(res)
    if norm is not None:
        width = out_sds.shape[1]
        whole = pl.BlockSpec((1, width), lambda i, j, r: (0, 0))
        in_specs += [o_spec, whole, o_spec]
        args += list(norm)
        out_specs = [o_spec, o_spec, whole]
        out_shape = [jax.ShapeDtypeStruct(out_sds.shape, F32), jax.ShapeDtypeStruct(out_sds.shape, BF16),
                     jax.ShapeDtypeStruct((1, width), F32)]
        sem = ("arbitrary", "arbitrary", "arbitrary")
    scratch = [] if nk == 1 else [pltpu.VMEM(acc_shape, F32)]
    got = _call(name, body, grid, in_specs, out_specs, out_shape, args, scratch, sem, comm)
    if norm is not None:
        return got if comm is None else (got[0], got[1])
    return got[0] if comm is None else (got[0][0], got[1])


def ffn_in_act(name, n, w4, tm, comm=None, gain=None):
    t, d = n.shape
    cs = w4.shape[2]

    def body(ins, outs, scratch):
        wg_ref, wu_ref = ins[-2:]
        a_ref, s_ref = outs[:2]
        if gain is None:
            nv = ins[0][...]
        else:
            @pl.when(pl.program_id(1) == 0)
            def _():
                hv = ins[0][...]
                scratch[0][...] = (hv * _rstd(hv) * ins[1][...]).astype(BF16)
                outs[2][...] = scratch[0][...]

            nv = scratch[0][...]
        gate = _dot(nv, wg_ref[...], NN)
        up = _dot(nv, wu_ref[...], NN)
        a_ref[0] = gate.astype(BF16)
        a_ref[1] = up.astype(BF16)
        s_ref[...] = (gate * jax.nn.sigmoid(gate) * up).astype(BF16)

    rows = pl.BlockSpec((tm, d), lambda i, j: (i, 0))
    in_specs = [rows] + ([] if gain is None else [pl.BlockSpec((1, d), lambda i, j: (0, 0))])
    in_specs += [pl.BlockSpec((None, d, cs), lambda i, j: (j, 0, 0)),
                 pl.BlockSpec((None, d, cs), lambda i, j: (2 + j, 0, 0))]
    out_specs = [pl.BlockSpec((2, tm, cs), lambda i, j: (0, i, j)), pl.BlockSpec((tm, cs), lambda i, j: (i, j))]
    out_shape = [jax.ShapeDtypeStruct((2, t, 2 * cs), BF16), jax.ShapeDtypeStruct((t, 2 * cs), BF16)]
    if gain is not None:
        out_specs.append(rows)
        out_shape.append(jax.ShapeDtypeStruct((t, d), BF16))
    got = _call(name, body, (t // tm, 2), in_specs, out_specs, out_shape,
                [n] + ([] if gain is None else [gain]) + [w4, w4],
                [] if gain is None else [pltpu.VMEM((tm, d), BF16)], ("parallel", "arbitrary"), comm)
    return got if comm is None else (got[0], got[1])


def ffn_ds_dact(name, df, w_out, a3, tm):
    t, d = df.shape
    f = w_out.shape[0]
    cs = f // 2

    def body(ins, outs, scratch):
        df_ref, w_ref, a_ref = ins
        ds = _dot(df_ref[...], w_ref[...], NT)
        gate = a_ref[0].astype(F32)
        up = a_ref[1].astype(F32)
        sg = jax.nn.sigmoid(gate)
        outs[0][0] = (ds * up * sg * (1.0 + gate * (1.0 - sg))).astype(BF16)
        outs[0][1] = (ds * gate * sg).astype(BF16)

    blk = pl.BlockSpec((2, tm, cs), lambda i, j: (0, i, j))
    return _call(name, body, (t // tm, 2),
                 [pl.BlockSpec((tm, d), lambda i, j: (i, 0)), pl.BlockSpec((cs, d), lambda i, j: (j, 0)), blk],
                 [blk], [jax.ShapeDtypeStruct((2, t, f), BF16)], [df, w_out, a3], (), ("parallel", "parallel"))[0]


def _part_ranges(parts, d):
    out, lo = [], 0
    for p in parts:
        out.append((lo, p.shape[1] // d))
        lo += p.shape[1] // d
    return out, lo


def mm_nt_parts(name, parts, w4, tm, norm, alpha, comm=None):
    m = parts[0].shape[0]
    d, cs = w4.shape[1], w4.shape[2]
    per = cs // d
    ranges, nblk = _part_ranges(parts, d)
    np_ = len(parts)

    def body(ins, outs, scratch):
        w_ref, acc = ins[np_], scratch[0]
        r = pl.program_id(1)

        @pl.when(r == 0)
        def _():
            acc[...] = jnp.zeros_like(acc)

        for (lo, n), a_ref in zip(ranges, ins[:np_]):
            @pl.when(jnp.logical_and(r >= lo, r < lo + n))
            def _(a_ref=a_ref):
                acc[...] += _dot(a_ref[...], w_ref[...], NT)

        @pl.when(r == nblk - 1)
        def _():
            _norm_bwd_tile(lambda sl: acc[sl, :], tm, pl.program_id(0) == 0, *ins[np_ + 1:], *outs, alpha)

    rows = pl.BlockSpec((tm, d), lambda i, r: (i, 0))
    whole = pl.BlockSpec((1, d), lambda i, r: (0, 0))
    specs = [pl.BlockSpec((tm, d), lambda i, r, lo=lo, n=n: (i, jnp.clip(r - lo, 0, n - 1))) for lo, n in ranges]
    specs += [pl.BlockSpec((None, d, d), lambda i, r: (r // per, 0, r % per)), rows, whole, rows]
    got = _call(name, body, (m // tm, nblk), specs, [rows, rows, whole],
                [jax.ShapeDtypeStruct((m, d), F32), jax.ShapeDtypeStruct((m, d), BF16),
                 jax.ShapeDtypeStruct((1, d), F32)],
                list(parts) + [w4] + list(norm), [pltpu.VMEM((tm, d), F32)], ("arbitrary", "arbitrary"), comm)
    return got if comm is None else (got[0], got[1])


def mm_tn_parts(name, xa, parts, tt, comm=None):
    t, k = xa.shape
    d = k
    pr = k // 2
    ranges, nblk = _part_ranges(parts, d)
    per = nblk // N_CHIPS

    def body(ins, outs, scratch):
        x_ref, acc = ins[0], scratch[0]
        jb, r = pl.program_id(0), pl.program_id(1)

        @pl.when(r == 0)
        def _():
            acc[...] = jnp.zeros_like(acc)

        for (lo, n), p_ref in zip(ranges, ins[1:]):
            @pl.when(jnp.logical_and(jb >= lo, jb < lo + n))
            def _(p_ref=p_ref):
                acc[...] += _dot(x_ref[...], p_ref[...], TN)

        @pl.when(r == t // tt - 1)
        def _():
            outs[0][0] = acc[:pr].astype(BF16)
            outs[0][1] = acc[pr:].astype(BF16)

    def part_spec(lo, n):
        return pl.BlockSpec((tt, d), lambda jb, r: (jnp.where(jnp.logical_and(jb >= lo, jb < lo + n), r, 0),
                                                    jnp.clip(jb - lo, 0, n - 1)))

    specs = [pl.BlockSpec((tt, k), lambda jb, r: (r, 0))] + [part_spec(lo, n) for lo, n in ranges]
    got = _call(name, body, (nblk, t // tt), specs,
                [pl.BlockSpec((None, 2, pr, d), lambda jb, r: (jb // per, 0, 0, jb % per))],
                [jax.ShapeDtypeStruct((N_CHIPS, 2, pr, per * d), BF16)], [xa] + list(parts),
                [pltpu.VMEM((k, d), F32)], ("parallel", "arbitrary"), comm)
    return got[0] if comm is None else (got[0][0], got[1])


def mm_nn(name, a, w, out_dtype, tm, res=None, alpha=1.0):
    m, k = a.shape
    n = w.shape[1]
    return _mm(name, a, w, jax.ShapeDtypeStruct((m, n), out_dtype), (m // tm, 1, 1),
               pl.BlockSpec((tm, k), lambda i, j, r: (i, 0)),
               pl.BlockSpec((k, n), lambda i, j, r: (0, 0)),
               pl.BlockSpec((tm, n), lambda i, j, r: (i, 0)), NN, None, res=res, alpha=alpha)


def mm_nn_stacked(name, a, w4, out_dtype, tm, tn, j0=0, nj=None, comm=None):
    m, k = a.shape
    cs = w4.shape[2]
    per = cs // tn
    nj = N_CHIPS * per - j0 if nj is None else nj
    return _mm(name, a, w4, jax.ShapeDtypeStruct((m, nj * tn), out_dtype), (m // tm, nj, 1),
               pl.BlockSpec((tm, k), lambda i, j, r: (i, 0)),
               pl.BlockSpec((None, k, tn), lambda i, j, r: ((j + j0) // per, 0, (j + j0) % per)),
               pl.BlockSpec((tm, tn), lambda i, j, r: (i, j)), NN, None, comm=comm)


def mm_nt(name, dy, w, out_dtype, tm, tko, norm=None, alpha=1.0):
    m, n = dy.shape
    k = w.shape[0]
    return _mm(name, dy, w, jax.ShapeDtypeStruct((m, k), out_dtype), (m // tm, k // tko, 1),
               pl.BlockSpec((tm, n), lambda i, j, r: (i, 0)),
               pl.BlockSpec((tko, n), lambda i, j, r: (j, 0)),
               pl.BlockSpec((tm, tko), lambda i, j, r: (i, j)), NT, None, norm=norm, alpha=alpha)


def mm_nt_stacked(name, dy, w4, out_dtype, tm, tn, comm=None, norm=None, alpha=1.0):
    m = dy.shape[-2]
    k, cs = w4.shape[1], w4.shape[2]
    per = cs // tn
    if dy.ndim == 3:
        dy_spec = pl.BlockSpec((None, tm, cs), lambda i, j, r: (r // 2, i, r % 2))
    else:
        dy_spec = pl.BlockSpec((tm, tn), lambda i, j, r: (i, r))
    return _mm(name, dy, w4, jax.ShapeDtypeStruct((m, k), out_dtype), (m // tm, 1, N_CHIPS * per), dy_spec,
               pl.BlockSpec((None, k, tn), lambda i, j, r: (r // per, 0, r % per)),
               pl.BlockSpec((tm, k), lambda i, j, r: (i, 0)), NT, (tm, k), comm=comm, norm=norm, alpha=alpha)


def mm_tn_rows(name, xa, dy, tt):
    t, k = xa.shape
    n = dy.shape[1]
    tkr = k if k * n * 4 <= ACC_BYTES else k // 2
    return _mm(name, xa, dy, jax.ShapeDtypeStruct((k, n), BF16), (k // tkr, 1, t // tt),
               pl.BlockSpec((tt, tkr), lambda i, j, r: (r, i)),
               pl.BlockSpec((tt, n), lambda i, j, r: (r, 0)),
               pl.BlockSpec((tkr, n), lambda i, j, r: (i, 0)), TN, (tkr, n))


def mm_tn_whole(name, xa, dy, tt):
    t, k = xa.shape
    n = dy.shape[1]
    return _mm(name, xa, dy, jax.ShapeDtypeStruct((k, n), BF16), (1, 1, t // tt),
               pl.BlockSpec((tt, k), lambda i, j, r: (r, 0)),
               pl.BlockSpec((tt, n), lambda i, j, r: (r, 0)),
               pl.BlockSpec((k, n), lambda i, j, r: (0, 0)), TN, (k, n))


def mm_tn_cols(name, xa, dy, tt, comm=None):
    t, k = xa.shape
    pr = k // 2
    if dy.ndim == 3:
        cs = dy.shape[2] // 2
        dy_spec = pl.BlockSpec((None, tt, cs), lambda i, j, r: (j // 2, r, j % 2))
    else:
        cs = dy.shape[1] // N_CHIPS
        dy_spec = pl.BlockSpec((tt, cs), lambda i, j, r: (r, j))
    return _mm(name, xa, dy, jax.ShapeDtypeStruct((N_CHIPS, 2, pr, cs), BF16), (1, N_CHIPS, t // tt),
               pl.BlockSpec((tt, k), lambda i, j, r: (r, 0)), dy_spec,
               pl.BlockSpec((None, 2, pr, cs), lambda i, j, r: (j, 0, 0, 0)), TN, (k, cs), comm=comm)


def _rows(tt, w, col=0):
    return pl.BlockSpec((tt, w), lambda i: (i, col))


def _whole(shape):
    return pl.BlockSpec(shape, lambda i: (0,) * len(shape))


def _rstd(h):
    return lax.rsqrt(jnp.mean(h * h, axis=-1, keepdims=True) + NORM_EPS)


def rms_fwd(name, h, g, tt, comm=None):
    t, d = h.shape

    def body(ins, outs, scratch):
        hv = ins[0][...]
        outs[0][...] = (hv * _rstd(hv) * ins[1][...]).astype(BF16)

    got = _call(name, body, (t // tt,), [_rows(tt, d), _whole((1, d))], [_rows(tt, d)],
                [jax.ShapeDtypeStruct((t, d), BF16)], [h, g], (), ("parallel",), comm)
    return got[0] if comm is None else (got[0][0], got[1])


def mix_out_fwd(name, gates, yc, ya, h, w, tt):
    t, d = yc.shape

    def body(g_ref, yc_ref, ya_ref, h_ref, w_ref, m_ref, o_ref):
        merged = (jax.nn.sigmoid(g_ref[:, :d].astype(F32)) * yc_ref[...].astype(F32)
                  + jax.nn.sigmoid(g_ref[:, d:].astype(F32)) * ya_ref[...].astype(F32)).astype(BF16)
        m_ref[...] = merged
        o_ref[...] = h_ref[...] + _dot(merged, w_ref[...], NN)

    return _pcall(body, name=name,
                  out_shape=(jax.ShapeDtypeStruct((t, d), BF16), jax.ShapeDtypeStruct((t, d), F32)),
                  grid=(t // tt,),
                  in_specs=[_rows(tt, 2 * d), _rows(tt, d), _rows(tt, d), _rows(tt, d), _whole((d, d))],
                  out_specs=(_rows(tt, d), _rows(tt, d)),
                  compiler_params=_params(("parallel",)))(gates, yc, ya, h, w)


def mix_out_bwd(name, dh, w, gates, yc, ya, tt):
    t, d = yc.shape

    def body(dh_ref, w_ref, g_ref, yc_ref, ya_ref, dyc_ref, dya_ref, dg_ref):
        dmv = _dot(dh_ref[...], w_ref[...], NT)
        sc = jax.nn.sigmoid(g_ref[:, :d].astype(F32))
        sa = jax.nn.sigmoid(g_ref[:, d:].astype(F32))
        dyc_ref[...] = (dmv * sc).astype(BF16)
        dya_ref[...] = (dmv * sa).astype(BF16)
        dg_ref[:, :d] = (dmv * yc_ref[...].astype(F32) * sc * (1.0 - sc)).astype(BF16)
        dg_ref[:, d:] = (dmv * ya_ref[...].astype(F32) * sa * (1.0 - sa)).astype(BF16)

    return _pcall(body, name=name,
                  out_shape=(jax.ShapeDtypeStruct((t, d), BF16), jax.ShapeDtypeStruct((t, d), BF16),
                             jax.ShapeDtypeStruct((t, 2 * d), BF16)),
                  grid=(t // tt,),
                  in_specs=[_rows(tt, d), _whole((d, d)), _rows(tt, 2 * d), _rows(tt, d), _rows(tt, d)],
                  out_specs=(_rows(tt, d), _rows(tt, d), _rows(tt, 2 * d)),
                  compiler_params=_params(("parallel",)))(dh, w, gates, yc, ya)


def _shift_down(cur, prev8, s):
    tt = cur.shape[0]
    rolled = pltpu.roll(cur, s, 0)
    row8 = lax.broadcasted_iota(jnp.int32, prev8.shape, 0)
    first8 = jnp.where(row8 < s, pltpu.roll(prev8, s, 0), rolled[:8])
    return jnp.concatenate([first8, rolled[8:]], axis=0) if tt > 8 else first8


def _shift_up(cur, next8, s):
    tt = cur.shape[0]
    rolled = pltpu.roll(cur, tt - s, 0)
    row8 = lax.broadcasted_iota(jnp.int32, next8.shape, 0)
    last8 = jnp.where(row8 >= 8 - s, pltpu.roll(next8, 8 - s, 0), rolled[tt - 8:])
    return jnp.concatenate([rolled[:tt - 8], last8], axis=0) if tt > 8 else last8


def _prev8(tt, d, col):
    return pl.BlockSpec((8, d), lambda i: (jnp.maximum(i * (tt // 8) - 1, 0), col))


def _next8(tt, d, col, t):
    return pl.BlockSpec((8, d), lambda i: (jnp.minimum((i + 1) * (tt // 8), t // 8 - 1), col))


def conv_out_fwd(name, cbx, cw8, w_out, tt):
    t, d3 = cbx.shape
    d = d3 // 3

    def body(cb_ref, cc_ref, cx_ref, pc_ref, px_ref, w_ref, wo_ref, o_ref, y_ref):
        has_prev = (pl.program_id(0) > 0).astype(F32)
        cc = cc_ref[...] * cx_ref[...]
        prev = pc_ref[...] * px_ref[...] * has_prev
        w = w_ref[...]
        conv = w[0:1] * _shift_down(cc, prev, 2) + w[1:2] * _shift_down(cc, prev, 1) + w[2:3] * cc
        ycin = (cb_ref[...] * conv).astype(BF16)
        o_ref[...] = ycin
        y_ref[...] = _dot(ycin, wo_ref[...], NN).astype(BF16)

    out = jax.ShapeDtypeStruct((t, d), BF16)
    return _pcall(body, name=name, out_shape=(out, out), grid=(t // tt,),
                  in_specs=[_rows(tt, d, 0), _rows(tt, d, 1), _rows(tt, d, 2), _prev8(tt, d, 1), _prev8(tt, d, 2),
                            _whole((8, d)), _whole((d, d))],
                  out_specs=(_rows(tt, d), _rows(tt, d)),
                  compiler_params=_params(("parallel",)))(cbx, cbx, cbx, cbx, cbx, cw8, w_out)


def conv_out_bwd(name, dyc, w_out, cbx, cw8, tt):
    t, d3 = cbx.shape
    d = d3 // 3
    n = t // tt

    def body(dy_ref, ndy_ref, wo_ref, cb_ref, cc_ref, cx_ref, pc_ref, px_ref, ncb_ref, w_ref, o_ref, dw_ref):
        i = pl.program_id(0)
        has_prev = (i > 0).astype(F32)
        has_next = (i < n - 1).astype(F32)
        cb = cb_ref[...]
        cc = cc_ref[...] * cx_ref[...]
        prev = pc_ref[...] * px_ref[...] * has_prev
        w = w_ref[...]
        cc1 = _shift_down(cc, prev, 1)
        cc2 = _shift_down(cc, prev, 2)
        conv = w[0:1] * cc2 + w[1:2] * cc1 + w[2:3] * cc
        dyv = _dot(dy_ref[...], wo_ref[...], NT)
        dconv = dyv * cb
        dnext = _dot(ndy_ref[...], wo_ref[...], NT)[:8] * ncb_ref[...] * has_next
        dcc = w[2:3] * dconv + w[1:2] * _shift_up(dconv, dnext, 1) + w[0:1] * _shift_up(dconv, dnext, 2)
        o_ref[:, :d] = (dyv * conv).astype(BF16)
        o_ref[:, d:2 * d] = (dcc * cx_ref[...]).astype(BF16)
        o_ref[:, 2 * d:] = (dcc * cc_ref[...]).astype(BF16)

        @pl.when(i == 0)
        def _():
            dw_ref[...] = jnp.zeros_like(dw_ref)

        dw_ref[0:1, :] += jnp.sum(dconv * cc2, axis=0, keepdims=True)
        dw_ref[1:2, :] += jnp.sum(dconv * cc1, axis=0, keepdims=True)
        dw_ref[2:3, :] += jnp.sum(dconv * cc, axis=0, keepdims=True)

    return _pcall(body, name=name,
                  out_shape=(jax.ShapeDtypeStruct((t, d3), BF16), jax.ShapeDtypeStruct((8, d), F32)),
                  grid=(n,),
                  in_specs=[_rows(tt, d),
                            pl.BlockSpec((BF16_ROWS, d), lambda i: (jnp.minimum((i + 1) * (tt // BF16_ROWS),
                                                                                t // BF16_ROWS - 1), 0)),
                            _whole((d, d)), _rows(tt, d, 0), _rows(tt, d, 1), _rows(tt, d, 2),
                            _prev8(tt, d, 1), _prev8(tt, d, 2), _next8(tt, d, 0, t), _whole((8, d))],
                  out_specs=(_rows(tt, d3), _whole((8, d))),
                  compiler_params=_params(("arbitrary",)))(dyc, dyc, w_out, cbx, cbx, cbx, cbx, cbx, cbx, cw8)


def tail(name, h3, p, tgt, gp, gf, w_gate, w_proj, tt):
    t, d = h3.shape
    pd = p.shape[1]

    def body(h_ref, p_ref, tg_ref, gp_ref, gf_ref, wg_ref, wp_ref, np_ref, dh_ref, dpp_ref, dzg_ref, dgf_ref,
             loss_ref):
        hv = h_ref[...]
        npl = (hv * _rstd(hv) * gp_ref[...]).astype(BF16)
        np_ref[...] = npl
        pg = jax.nn.sigmoid(_dot(npl, wg_ref[...], NN))
        ppv = _dot(p_ref[...].astype(BF16), wp_ref[...], NN)
        h4 = hv + pg * ppv
        r4 = _rstd(h4)
        hn = h4 * r4
        gfv = gf_ref[...]
        err = hn * gfv - tg_ref[...]
        dy = err * (1.0 / d)
        gy = dy * gfv
        dh4 = r4 * (gy - hn * jnp.mean(gy * hn, axis=-1, keepdims=True))
        dh_ref[...] = dh4
        dpp_ref[...] = (dh4 * pg).astype(BF16)
        dzg_ref[...] = (dh4 * ppv * pg * (1.0 - pg)).astype(BF16)

        @pl.when(pl.program_id(0) == 0)
        def _():
            dgf_ref[...] = jnp.zeros_like(dgf_ref)
            loss_ref[...] = jnp.zeros_like(loss_ref)

        dgf_ref[...] += jnp.sum(dy * hn, axis=0, keepdims=True)
        tok = jnp.mean(err * err, axis=-1, keepdims=True)
        loss_ref[...] += 0.5 * jnp.sum(tok, axis=0, keepdims=True) * jnp.ones((1, loss_ref.shape[1]), F32)

    return _pcall(body, name=name,
                  out_shape=(jax.ShapeDtypeStruct((t, d), BF16), jax.ShapeDtypeStruct((t, d), F32),
                             jax.ShapeDtypeStruct((t, d), BF16), jax.ShapeDtypeStruct((t, d), BF16),
                             jax.ShapeDtypeStruct((1, d), F32), jax.ShapeDtypeStruct((1, d), F32)),
                  grid=(t // tt,),
                  in_specs=[_rows(tt, d), _rows(tt, pd), _rows(tt, d), _whole((1, d)), _whole((1, d)),
                            _whole((d, d)), _whole((pd, d))],
                  out_specs=(_rows(tt, d), _rows(tt, d), _rows(tt, d), _rows(tt, d), _whole((1, d)),
                             _whole((1, d))),
                  compiler_params=_params(("arbitrary",)))(h3, p, tgt, gp, gf, w_gate, w_proj)


SCALE = 1.0 / math.sqrt(HEAD_DIM)


def _log_stick(z):
    return -(jnp.maximum(z, 0.0) + jnp.log(1.0 + jnp.exp(-jnp.abs(z))))


def _tri_sum(x, tri):
    hi = x.astype(BF16)
    lo = (x - hi.astype(F32)).astype(BF16)
    return _dot(hi, tri, NN) + _dot(lo, tri, NN)


KEY_BLOCK = 128
NEAR = 3


def _sb_near(qs, jds, k_ref, below, upper):
    pairs = [(s, b) for s in range(len(qs)) for b in range(NEAR)]
    rows = {(s, b): _block_rows(jnp.maximum(jds[s] - b, 0), KEY_BLOCK) for s, b in pairs}
    z = {(s, b): _dot(qs[s], k_ref[rows[s, b], :], NT) * SCALE for s, b in pairs}
    lg = {(s, b): jnp.where(below, _log_stick(z[s, b]), 0.0) if b == 0 else _log_stick(z[s, b]) for s, b in pairs}
    cum = {(s, b): _tri_sum(lg[s, b], upper) for s, b in pairs}
    out, carries = [], []
    for s in range(len(qs)):
        c = cum[s, 0][:, 0:1]
        blocks = [(rows[s, 0], z[s, 0], jnp.exp(jnp.where(below, z[s, 0] + cum[s, 0], -1e30)))]
        for b in range(1, NEAR):
            live = jds[s] >= b
            blocks.append((rows[s, b], z[s, b], jnp.exp(z[s, b] + cum[s, b] + (c + jnp.where(live, 0.0, -1e30)))))
            c = c + jnp.where(live, cum[s, b][:, 0:1], 0.0)
        out.append(blocks)
        carries.append(c)
    return out, carries


def _sb_far(q, kj, upper, c):
    z = _dot(q, kj, NT) * SCALE
    cum = _tri_sum(_log_stick(z), upper)
    return z, jnp.exp(z + cum + c), c + cum[:, 0:1]


def _block_rows(j, size):
    return pl.ds(pl.multiple_of(j * size, size), size)


def _sweep_on(st):
    return jnp.logical_and(st[0] >= 0, jnp.max(st[1]) > -STICK_EXIT)


def attn_fwd(name, qkv, tq):
    t, d3 = qkv.shape
    d = d3 // 3
    nh = d // HEAD_DIM
    nq = t // tq
    tb = KEY_BLOCK
    nsub = tq // tb

    def body(q_ref, k_ref, v_ref, o_ref):
        i = pl.program_id(1)
        row = lax.broadcasted_iota(jnp.int32, (tb, tb), 0)
        col = lax.broadcasted_iota(jnp.int32, (tb, tb), 1)
        upper = (row >= col).astype(BF16)
        qs = [q_ref[s * tb:(s + 1) * tb, :] for s in range(nsub)]
        jds = [i * nsub + s for s in range(nsub)]
        near, carries = _sb_near(qs, jds, k_ref, col < row, upper)
        state = []
        for s in range(nsub):
            acc = jnp.zeros((tb, HEAD_DIM), F32)
            for rows, _, a in near[s]:
                acc = acc + _dot(a.astype(BF16), v_ref[rows, :], NN)
            state.append((qs[s], jds[s], carries[s], acc))
        for s, (q, jd, c, acc) in enumerate(state):

            def step(st, q=q):
                rows = _block_rows(st[0], tb)
                _, a, c2 = _sb_far(q, k_ref[rows, :], upper, st[1])
                return st[0] - 1, c2, st[2] + _dot(a.astype(BF16), v_ref[rows, :], NN)

            _, _, acc = lax.while_loop(_sweep_on, step, (jd - NEAR, c, acc))
            o_ref[s * tb:(s + 1) * tb, :] = acc.astype(o_ref.dtype)

    return _pcall(body, name=name, out_shape=jax.ShapeDtypeStruct((t, d), BF16), grid=(nh, nq),
                  in_specs=[pl.BlockSpec((tq, HEAD_DIM), lambda h, i: (i, h)),
                            pl.BlockSpec((t, HEAD_DIM), lambda h, i: (0, nh + h)),
                            pl.BlockSpec((t, HEAD_DIM), lambda h, i: (0, 2 * nh + h))],
                  out_specs=pl.BlockSpec((tq, HEAD_DIM), lambda h, i: (i, h)),
                  compiler_params=_params(("parallel", "arbitrary")))(qkv, qkv, qkv)


def attn_bwd(name, qkv, do, tq):
    t, d3 = qkv.shape
    d = d3 // 3
    nh = d // HEAD_DIM
    nq = t // tq
    tb = KEY_BLOCK
    nsub = tq // tb

    def body(q_ref, k_ref, v_ref, do_ref, dq_ref, dk_ref, dv_ref, dk_acc, dv_acc, g_buf, z_buf):
        i = pl.program_id(1)

        @pl.when(i == 0)
        def _():
            dk_acc[...] = jnp.zeros_like(dk_acc)
            dv_acc[...] = jnp.zeros_like(dv_acc)

        row = lax.broadcasted_iota(jnp.int32, (tb, tb), 0)
        col = lax.broadcasted_iota(jnp.int32, (tb, tb), 1)
        below = col < row
        upper = (row >= col).astype(BF16)
        lower = (row <= col).astype(BF16)

        qs = [q_ref[s * tb:(s + 1) * tb, :] for s in range(nsub)]
        dos = [do_ref[s * tb:(s + 1) * tb, :] for s in range(nsub)]
        jds = [i * nsub + s for s in range(nsub)]
        near, carries = _sb_near(qs, jds, k_ref, below, upper)
        da = [[_dot(dos[s], v_ref[rows, :], NT) for rows, _, _ in near[s]] for s in range(nsub)]
        state = []
        for s in range(nsub):
            kept = [(rows, z, da[s][b] * a) for b, (rows, z, a) in enumerate(near[s])]
            for rows, _, a in near[s]:
                dv_acc[rows, :] += _dot(a.astype(BF16), dos[s], TN)
            state.append((qs[s], dos[s], jds[s], carries[s], kept))

        carried = []
        for s, (q, dov, jd, c, kept) in enumerate(state):
            def step(st, s=s, q=q, dov=dov, jd=jd):
                j = st[0]
                rows = _block_rows(j, tb)
                z, a, c2 = _sb_far(q, k_ref[rows, :], upper, st[1])
                g_buf[jd - j] = _dot(dov, v_ref[rows, :], NT) * a
                z_buf[jd - j] = z
                dv_acc[rows, :] += _dot(a.astype(BF16), dov, TN)
                return j - 1, c2

            j_stop, _ = lax.while_loop(_sweep_on, step, (jd - NEAR, c))

            def far(j, st, s=s, q=q, jd=jd):
                run, dq = st
                rows = _block_rows(j, tb)
                g = g_buf[jd - j]
                dz = (g - jax.nn.sigmoid(z_buf[jd - j]) * (run + _tri_sum(g, lower))).astype(BF16)
                dk_acc[rows, :] += _dot(dz, q, TN)
                return run + jnp.sum(g, axis=1, keepdims=True), dq + _dot(dz, k_ref[rows, :], NN)

            carried.append(lax.fori_loop(j_stop + 1, jd - NEAR + 1, far,
                                         (jnp.zeros((tb, 1), F32), jnp.zeros((tb, HEAD_DIM), F32))))

        tri = [[_tri_sum(g, lower) for _, _, g in st[4]] for st in state]
        sig = [[jax.nn.sigmoid(z) for _, z, _ in st[4]] for st in state]
        for s, (q, dov, jd, c, kept) in enumerate(state):
            run, dq = carried[s]
            for b in reversed(range(NEAR)):
                rows, z, g = kept[b]
                dz = g - sig[s][b] * (run + tri[s][b])
                if b == 0:
                    dz = jnp.where(below, dz, 0.0)
                dz = dz.astype(BF16)
                dk_acc[rows, :] += _dot(dz, q, TN)
                dq = dq + _dot(dz, k_ref[rows, :], NN)
                if b:
                    run = run + jnp.sum(g, axis=1, keepdims=True)
            dq_ref[s * tb:(s + 1) * tb, :] = (dq * SCALE).astype(BF16)

        @pl.when(i == nq - 1)
        def _():
            dk_ref[...] = (dk_acc[...] * SCALE).astype(BF16)
            dv_ref[...] = dv_acc[...].astype(BF16)

    blk = pl.BlockSpec((tq, HEAD_DIM), lambda h, i: (i, h))
    col_h = pl.BlockSpec((t, HEAD_DIM), lambda h, i: (0, h))
    out = jax.ShapeDtypeStruct((t, d), BF16)
    return _pcall(body, name=name, out_shape=(out, out, out), grid=(nh, nq),
                  in_specs=[blk,
                            pl.BlockSpec((t, HEAD_DIM), lambda h, i: (0, nh + h)),
                            pl.BlockSpec((t, HEAD_DIM), lambda h, i: (0, 2 * nh + h)),
                            blk],
                  out_specs=(blk, col_h, col_h),
                  scratch_shapes=[pltpu.VMEM((t, HEAD_DIM), F32), pltpu.VMEM((t, HEAD_DIM), F32),
                                  pltpu.VMEM((t // tb, tb, tb), F32), pltpu.VMEM((t // tb, tb, tb), F32)],
                  compiler_params=_params(("parallel", "arbitrary")))(qkv, qkv, qkv, do)


def _place():
    x, y, c = lax.axis_index("x"), lax.axis_index("y"), lax.axis_index("c")
    chips = [(1 - x, y), (x, 1 - y), (1 - x, 1 - y)]
    return x, y, c, chips


def _remote(src, dst, send_sem, recv_sem, dev):
    return pltpu.make_async_remote_copy(src_ref=src, dst_ref=dst, send_sem=send_sem, recv_sem=recv_sem,
                                        device_id=dev, device_id_type=MESH)


def place_shard(name, w, chip):
    r, cdim = w.shape
    tr = _tile(r, max(BF16_ROWS, (1 << 19) // cdim), BF16_ROWS)

    def body(chip_ref, w_ref, o_ref):
        o_ref[...] = w_ref[...].astype(BF16)

    spec = pltpu.PrefetchScalarGridSpec(
        num_scalar_prefetch=1, grid=(r // tr,),
        in_specs=[pl.BlockSpec((tr, cdim), lambda i, s: (i, 0))],
        out_specs=pl.BlockSpec((None, tr, cdim), lambda i, s: (s[0], i, 0)))
    return _pcall(body, name=name, out_shape=jax.ShapeDtypeStruct((N_CHIPS, r, cdim), BF16), grid_spec=spec,
                  compiler_params=_params(("parallel",)))(chip, w)


class Comm:
    def __init__(self, ins, outs, aliases, sems, first, mid, last):
        self.ins, self.outs, self.aliases, self.sems = list(ins), list(outs), dict(aliases), list(sems)
        self.first, self.mid, self.last = first, mid, last


def run_comm(name, comm):
    ni, no = len(comm.ins), len(comm.outs)

    def body(*refs):
        ins, outs, sems = refs[:ni], refs[ni:ni + no], refs[ni + no:]
        comm.first(ins, outs, sems)
        comm.mid(ins, outs, sems)
        comm.last(ins, outs, sems)

    return _pcall(body, name=name, out_shape=comm.outs, in_specs=[ANY] * ni, out_specs=[ANY] * no,
                  input_output_aliases=comm.aliases, scratch_shapes=comm.sems, compiler_params=_params())(*comm.ins)


def gather_comm(bufs):
    n = len(bufs)

    def half(out, w, which):
        pr = out[w].shape[1] // 2
        return pl.ds(pl.multiple_of(which * pr, BF16_ROWS), pr)

    def first(ins, out, sems):
        isend, irecv, _, _ = sems
        x, y, c, chips = _place()
        for w in range(n):
            mine = out[w].at[2 * x + y, half(out, w, c)]
            for j, (cx, cy) in enumerate(chips):
                _remote(mine, mine, isend.at[3 * w + j], irecv.at[3 * w + j], (cx, cy, c)).start()

    def mid(ins, out, sems):
        isend, irecv, dsend, drecv = sems
        x, y, c, chips = _place()
        sib = (x, y, 1 - c)
        for w in range(n):
            for j, (cx, cy) in enumerate(chips):
                landed = out[w].at[2 * cx + cy, half(out, w, c)]
                _remote(landed, landed, isend.at[3 * w + j], irecv.at[3 * w + j], sib).wait_recv()
                _remote(landed, landed, dsend.at[3 * w + j], drecv.at[3 * w + j], sib).start()

    def last(ins, out, sems):
        isend, irecv, dsend, drecv = sems
        x, y, c, chips = _place()
        sib = (x, y, 1 - c)
        for w in range(n):
            for j, (cx, cy) in enumerate(chips):
                landed = out[w].at[2 * cx + cy, half(out, w, 1 - c)]
                _remote(landed, landed, dsend.at[3 * w + j], drecv.at[3 * w + j], sib).wait_recv()
        for w in range(n):
            sent = out[w].at[0, half(out, w, c)]
            for j in range(3):
                _remote(sent, sent, isend.at[3 * w + j], irecv.at[3 * w + j], sib).wait_send()
                _remote(sent, sent, dsend.at[3 * w + j], drecv.at[3 * w + j], sib).wait_send()

    return Comm(bufs, [jax.ShapeDtypeStruct(s.shape, s.dtype) for s in bufs], {w: w for w in range(n)},
                [pltpu.SemaphoreType.DMA((3 * n,))] * 4, first, mid, last)


def _nothing(ins, outs, sems):
    return None


def exchange_comm(pieces):
    n = len(pieces)

    def copies(src, out, sems):
        x, y, c, _ = _place()
        return [_remote(src[w].at[k, 1 - c], out[w].at[k], sems[0].at[N_CHIPS * w + k], sems[1].at[N_CHIPS * w + k],
                        (x, y, 1 - c)) for w in range(n) for k in range(N_CHIPS)]

    def first(src, out, sems):
        for cp in copies(src, out, sems):
            cp.start()

    def last(src, out, sems):
        for cp in copies(src, out, sems):
            cp.wait()

    return Comm(pieces, [jax.ShapeDtypeStruct((N_CHIPS,) + s.shape[2:], s.dtype) for s in pieces], {},
                [pltpu.SemaphoreType.DMA((N_CHIPS * n,))] * 2, first, _nothing, last)


def scatter_comm(parts):
    n = len(parts)

    def copies(src, out, sems):
        x, y, c, chips = _place()
        return [_remote(src[w].at[2 * cx + cy], out[w].at[j], sems[0].at[3 * w + j], sems[1].at[3 * w + j], (cx, cy, c))
                for w in range(n) for j, (cx, cy) in enumerate(chips)]

    def first(src, out, sems):
        for cp in copies(src, out, sems):
            cp.start()

    def last(src, out, sems):
        for cp in copies(src, out, sems):
            cp.wait()

    return Comm(parts, [jax.ShapeDtypeStruct((3,) + s.shape[1:], s.dtype) for s in parts], {},
                [pltpu.SemaphoreType.DMA((3 * n,))] * 2, first, _nothing, last)


def share_comm(halves):
    n = len(halves)

    def first(ins, buf, sems):
        x, y, c, _ = _place()
        for w in range(n):
            _remote(buf[w].at[c], buf[w].at[c], sems[0].at[w], sems[1].at[w], (x, y, 1 - c)).start()

    def last(ins, buf, sems):
        x, y, c, _ = _place()
        for w in range(n):
            landed = buf[w].at[1 - c]
            _remote(landed, landed, sems[0].at[w], sems[1].at[w], (x, y, 1 - c)).wait_recv()
        for w in range(n):
            _remote(buf[w].at[c], buf[w].at[c], sems[0].at[w], sems[1].at[w], (x, y, 1 - c)).wait_send()

    return Comm(halves, [jax.ShapeDtypeStruct(s.shape, s.dtype) for s in halves], {w: w for w in range(n)},
                [pltpu.SemaphoreType.DMA((n,))] * 2, first, _nothing, last)


def gather_small(name, blk, reduce):
    r, cdim = blk.shape

    def body(in_ref, out_ref, *rest):
        if reduce:
            buf, send_sem, recv_sem = rest
        else:
            buf = out_ref
            send_sem, recv_sem = rest
        x, y, c, _ = _place()
        me = 4 * x + 2 * y + c
        buf[me] = in_ref[...]
        peers = []
        for dx in range(2):
            for dy in range(2):
                for dc in range(2):
                    if dx or dy or dc:
                        peers.append((dx, dy, dc))
        copies = []
        for s, (dx, dy, dc) in enumerate(peers):
            cp = _remote(in_ref, buf.at[me], send_sem.at[s], recv_sem.at[s],
                         ((1 - x if dx else x), (1 - y if dy else y), (1 - c if dc else c)))
            cp.start()
            copies.append(cp)
        for s, (dx, dy, dc) in enumerate(peers):
            px, py, pc_ = (1 - x if dx else x), (1 - y if dy else y), (1 - c if dc else c)
            landed = buf.at[4 * px + 2 * py + pc_]
            _remote(landed, landed, send_sem.at[s], recv_sem.at[s], (x, y, c)).wait_recv()
        for cp in copies:
            cp.wait_send()
        if reduce:
            tot = buf[0]
            for s in range(1, N_DEV):
                tot = tot + buf[s]
            out_ref[...] = tot

    vm = pl.BlockSpec(memory_space=pltpu.VMEM)
    out_shape = jax.ShapeDtypeStruct((r, cdim) if reduce else (N_DEV, r, cdim), F32)
    scratch = ([pltpu.VMEM((N_DEV, r, cdim), F32)] if reduce else []) + [pltpu.SemaphoreType.DMA((N_DEV - 1,))] * 2
    return _pcall(body, name=name, out_shape=out_shape, in_specs=[vm], out_specs=vm, scratch_shapes=scratch,
                  compiler_params=_params())(blk)


def sum_cores(name, own, got, place):
    _, _, pr, pc = own.shape
    tr = _tile(pr, max(BF16_ROWS, (1 << 19) // pc), BF16_ROWS)

    def body(place_ref, own_ref, got_ref, o_ref):
        o_ref[...] = (own_ref[...].astype(F32) + got_ref[...].astype(F32)).astype(o_ref.dtype)

    spec = pltpu.PrefetchScalarGridSpec(
        num_scalar_prefetch=1, grid=(N_CHIPS, pr // tr),
        in_specs=[pl.BlockSpec((None, None, tr, pc), lambda k, i, s: (k, s[1], i, 0)),
                  pl.BlockSpec((None, tr, pc), lambda k, i, s: (k, i, 0))],
        out_specs=pl.BlockSpec((None, tr, pc), lambda k, i, s: (k, i, 0)))
    return _pcall(body, name=name, out_shape=jax.ShapeDtypeStruct((N_CHIPS, pr, pc), BF16), grid_spec=spec,
                  compiler_params=_params(("parallel", "parallel")))(place, own, got)


def sum_chips(name, part, got, place):
    _, pr, pc = part.shape
    tr = _tile(pr, max(BF16_ROWS, (1 << 18) // pc), BF16_ROWS)

    def body(place_ref, part_ref, got_ref, o_ref):
        tot = part_ref[...].astype(F32)
        for j in range(3):
            tot = tot + got_ref[j].astype(F32)
        o_ref[...] = tot

    spec = pltpu.PrefetchScalarGridSpec(
        num_scalar_prefetch=1, grid=(pr // tr,),
        in_specs=[pl.BlockSpec((None, tr, pc), lambda i, s: (s[0], i, 0)),
                  pl.BlockSpec((3, tr, pc), lambda i, s: (0, i, 0))],
        out_specs=pl.BlockSpec((None, tr, pc), lambda i, s: (s[1], i, 0)))
    return _pcall(body, name=name, out_shape=jax.ShapeDtypeStruct((2, pr, pc), F32), grid_spec=spec,
                  compiler_params=_params(("parallel",)))(place, part, got)


def adamw(name, w, g, m, v):
    rows, cols = w.shape
    tr = _tile(rows, max(8, (1 << 18) // cols))
    c1 = 1.0 / (1.0 - ADAM_B1 ** ADAM_STEP)
    c2 = 1.0 / (1.0 - ADAM_B2 ** ADAM_STEP)

    def body(w_ref, g_ref, m_ref, v_ref, d_ref, nm_ref, nv_ref):
        gv = g_ref[...]
        nm = ADAM_B1 * m_ref[...] + (1.0 - ADAM_B1) * gv
        nv = ADAM_B2 * v_ref[...] + (1.0 - ADAM_B2) * (gv * gv)
        nm_ref[...] = nm
        nv_ref[...] = nv
        d_ref[...] = -ADAM_LR * ((nm * c1) / (jnp.sqrt(nv * c2) + ADAM_EPS) + ADAM_WD * w_ref[...])

    spec = pl.BlockSpec((tr, cols), lambda i: (i, 0))
    sds = jax.ShapeDtypeStruct((rows, cols), F32)
    return _pcall(body, name=name, out_shape=(sds, sds, sds), grid=(rows // tr,),
                  in_specs=[spec] * 4, out_specs=(spec, spec, spec),
                  compiler_params=_params(("parallel",)))(w, g, m, v)


MATS = ["ffn1_w_in", "ffn1_w_out", "w_mix_in", "w_conv_out", "w_attn_out", "w_mix_out", "ffn2_w_in", "ffn2_w_out",
        "w_ple_gate", "w_ple_proj"]
COL_SHARDED = {"ffn1_w_in", "w_mix_in", "ffn2_w_in", "w_ple_proj"}
NORMS = ["ffn1_norm", "mix_norm", "ffn2_norm", "ple_norm", "final_norm"]
WEIGHTS = ["ffn1_norm", "ffn1_w_in", "ffn1_w_out", "mix_norm", "w_mix_in", "conv_w", "w_conv_out", "w_attn_out",
           "w_mix_out", "ffn2_norm", "ffn2_w_in", "ffn2_w_out", "ple_norm", "w_ple_gate", "w_ple_proj", "final_norm"]


def _pad_rows(a, rows):
    return jnp.concatenate([a, jnp.zeros((rows - a.shape[0],) + a.shape[1:], a.dtype)], axis=0)


def _step(x, p, tgt, w, m, v):
    t, d = x.shape
    tt = _tile(t, 256)
    tm = _tile(t, 512)
    tm2 = _tile(t, 1024)
    tq = _tile(t, 1024)

    chip = 2 * lax.axis_index("x") + lax.axis_index("y")
    place = jnp.stack([chip, lax.axis_index("c")]).astype(jnp.int32)

    placed = {k: place_shard("place_" + k, w[k], place) for k in MATS}
    full = {}

    def keep(names, bufs):
        for k, buf in zip(names, bufs):
            full[k] = buf if k in COL_SHARDED else buf.reshape(-1, buf.shape[2])

    def gather_of(names):
        return gather_comm([placed[k] for k in names])

    cw_all = gather_small("gather_conv_w", _pad_rows(w["conv_w"], 8), False)
    cw8 = jnp.concatenate([cw_all[2 * k] for k in range(N_CHIPS)], axis=1)
    g1, gm, g2, gp, gf = (w[k].reshape(1, d) for k in NORMS)

    def ffn_fwd(tag, h, g, first, w_in_name, w_out_name, riders):
        if first:
            n, bufs = rms_fwd(tag + "_norm", h, g, tt, comm=gather_of(first))
            keep(first, bufs)
            (a, s), bufs = ffn_in_act(tag + "_in", n, full[w_in_name], tm, comm=gather_of(riders))
            keep(riders, bufs)
        else:
            a, s, n = ffn_in_act(tag + "_in", h, full[w_in_name], tm, gain=g)
        return n, a, s, mm_nn(tag + "_out", s, full[w_out_name], F32, tm, res=h, alpha=0.5)

    n1, a1, s1, h1 = ffn_fwd("ffn1", x, g1, ["ffn1_w_in"], "ffn1_w_in", "ffn1_w_out", ["ffn1_w_out", "w_mix_in"])
    u = rms_fwd("mix_norm", h1, gm, tt)
    wmix = full["w_mix_in"]
    riders = [["w_conv_out", "w_attn_out", "w_mix_out"], ["ffn2_w_in"], ["ffn2_w_out", "w_ple_gate", "w_ple_proj"]]
    cbx, bufs = mm_nn_stacked("mix_in_conv", u, wmix, F32, tm2, d, 0, 3, comm=gather_of(riders[0]))
    keep(riders[0], bufs)
    qkv, bufs = mm_nn_stacked("mix_in_qkv", u, wmix, BF16, tm2, d, 3, 3, comm=gather_of(riders[1]))
    keep(riders[1], bufs)
    gates, bufs = mm_nn_stacked("mix_in_gates", u, wmix, BF16, tm2, d, 6, 2, comm=gather_of(riders[2]))
    keep(riders[2], bufs)
    wpp = full["w_ple_proj"]
    wpp = jnp.transpose(wpp, (1, 0, 2)).reshape(wpp.shape[1], -1)
    ycin, y_conv = conv_out_fwd("conv_out", cbx, cw8, full["w_conv_out"], tt)
    o = attn_fwd("attn", qkv, tq)
    y_attn = mm_nn("attn_out", o, full["w_attn_out"], BF16, tm)
    merged, h2 = mix_out_fwd("mix_out", gates, y_conv, y_attn, h1, full["w_mix_out"], tm)
    n2, a2, s2, h3 = ffn_fwd("ffn2", h2, g2, [], "ffn2_w_in", "ffn2_w_out", [])

    pieces, chip_sums, halves = {}, {}, {}

    def as_pieces(k):
        pc = pieces[k]
        return pc if k in COL_SHARDED else pc.reshape(N_CHIPS, 2, pc.shape[0] // (2 * N_CHIPS), pc.shape[1])

    def sum_siblings(tag, names):
        pcs = [as_pieces(k) for k in names]
        got = run_comm("exchange_" + tag, exchange_comm(pcs))
        for k, a, b in zip(names, pcs, got):
            chip_sums[k] = sum_cores("sum_cores_" + k, a, b, place)

    def scatter_of(names):
        return scatter_comm([chip_sums[k] for k in names])

    def sum_landed(names, landed):
        for k, b in zip(names, landed):
            halves[k] = sum_chips("sum_chips_" + k, chip_sums[k], b, place)

    npl, dh4, dpp, dzg, dgf, loss_row = tail("tail", h3, p, tgt, gp, gf, full["w_ple_gate"], wpp, tt)
    dwpp = mm_tn_whole("ple_proj_dw", p, dpp, tm2)
    pieces["w_ple_proj"] = jnp.transpose(dwpp.reshape(2, p.shape[1] // 2, N_CHIPS, d // N_CHIPS), (2, 0, 1, 3))
    pieces["w_ple_gate"] = mm_tn_rows("ple_gate_dw", npl, dzg, tm2)
    dh3, df2, dgp = mm_nt("ple_gate_dx", dzg, full["w_ple_gate"], F32, tm, d, norm=(h3, gp, dh4), alpha=0.5)
    w_in, w_out = full["ffn2_w_in"], full["ffn2_w_out"]
    pieces["ffn2_w_out"] = mm_tn_rows("ffn2_dwout", s2, df2, tm2)
    da2 = ffn_ds_dact("ffn2_ds", df2, w_out, a2, tm)
    pieces["ffn2_w_in"] = mm_tn_cols("ffn2_dwin", n2, da2, tm2)
    dh2, dh2b, dg2 = mm_nt_stacked("ffn2_dn", da2, w_in, F32, tm2, w_in.shape[2], norm=(h2, g2, dh3))
    pieces["w_mix_out"] = mm_tn_rows("mix_out_dw", merged, dh2b, tm2)
    dyc, dya, dgates = mix_out_bwd("mix_out_dx", dh2b, full["w_mix_out"], gates, y_conv, y_attn, tm)
    pieces["w_conv_out"] = mm_tn_rows("conv_out_dw", ycin, dyc, tm2)
    dcbx, dcw8 = conv_out_bwd("conv_out_dx", dyc, full["w_conv_out"], cbx, cw8, tt)
    pieces["w_attn_out"] = mm_tn_rows("attn_out_dw", o, dya, tm2)
    do = mm_nt("attn_out_dx", dya, full["w_attn_out"], BF16, tm, d)
    dq, dk, dv = attn_bwd("attn_bwd", qkv, do, tq)
    dmix = [dcbx, dq, dk, dv, dgates]
    early = ["ffn2_w_in", "ffn2_w_out", "w_ple_gate", "w_ple_proj", "w_mix_out", "w_conv_out", "w_attn_out"]
    swap = exchange_comm([as_pieces(k) for k in early])
    pieces["w_mix_in"], got = mm_tn_parts("mix_in_dw", u, dmix, tm2, comm=swap)
    for k, a, b in zip(early, swap.ins, got):
        chip_sums[k] = sum_cores("sum_cores_" + k, a, b, place)
    (dh1, df1, dgm), landed = mm_nt_parts("mix_in_dx", dmix, wmix, tm, (h1, gm, dh2), 0.5, comm=scatter_of(early))
    sum_landed(early, landed)
    sum_siblings("mix", ["w_mix_in"])
    w_in, w_out = full["ffn1_w_in"], full["ffn1_w_out"]
    pieces["ffn1_w_out"] = mm_tn_rows("ffn1_dwout", s1, df1, tm2)
    da1 = ffn_ds_dact("ffn1_ds", df1, w_out, a1, tm)
    pieces["ffn1_w_in"], landed = mm_tn_cols("ffn1_dwin", n1, da1, tm2, comm=scatter_of(["w_mix_in"]))
    sum_landed(["w_mix_in"], landed)
    late = ["ffn1_w_in", "ffn1_w_out"]
    sum_siblings("late", late)
    (dx, _, dg1), landed = mm_nt_stacked("ffn1_dn", da1, w_in, F32, tm2, w_in.shape[2], comm=scatter_of(late),
                                         norm=(x, g1, dh1))
    sum_landed(late, landed)

    shared = run_comm("share_halves", share_comm([halves[k] for k in MATS]))
    grad, delta, new_m, new_v = {}, {}, {}, {}
    for k, sh in zip(MATS, shared):
        grad[k] = sh.reshape(w[k].shape)
        delta[k], new_m[k], new_v[k] = adamw("adamw_" + k, w[k], grad[k], m[k], v[k])

    small = jnp.concatenate([dg1, dgm, dg2, dgp, dgf, dcw8[:3], loss_row, jnp.zeros((7, d), F32)], axis=0)
    tot = gather_small("sum_small", small, True)
    loss = tot[8, 0]
    norm_w = jnp.concatenate([w[k].reshape(1, d) for k in NORMS] + [jnp.zeros((3, d), F32)], axis=0)
    norm_m = jnp.concatenate([m[k].reshape(1, d) for k in NORMS] + [jnp.zeros((3, d), F32)], axis=0)
    norm_v = jnp.concatenate([v[k].reshape(1, d) for k in NORMS] + [jnp.ones((3, d), F32)], axis=0)
    norm_g = jnp.concatenate([tot[0:5], jnp.zeros((3, d), F32)], axis=0)
    nd, nm, nv = adamw("adamw_norms", norm_w, norm_g, norm_m, norm_v)
    for r, k in enumerate(NORMS):
        grad[k] = norm_g[r].reshape(w[k].shape)
        delta[k], new_m[k], new_v[k] = (a[r].reshape(w[k].shape) for a in (nd, nm, nv))
    cs = d // N_CHIPS
    gcw = lax.dynamic_slice(tot[5:8], (0, chip * cs), (3, cs))
    cd, cm, cv = adamw("adamw_conv_w", _pad_rows(w["conv_w"], 8), _pad_rows(gcw, 8), _pad_rows(m["conv_w"], 8),
                       jnp.concatenate([v["conv_w"], jnp.ones((5, cs), F32)], axis=0))
    grad["conv_w"], delta["conv_w"], new_m["conv_w"], new_v["conv_w"] = gcw, cd[:3], cm[:3], cv[:3]
    return loss, dx, grad, delta, new_m, new_v


def kernel(x, p, ffn1_norm, ffn1_w_in, ffn1_w_out, mix_norm, w_mix_in, conv_w, w_conv_out, w_attn_out, w_mix_out, ffn2_norm, ffn2_w_in, ffn2_w_out, ple_norm, w_ple_gate, w_ple_proj, final_norm, loss_target, m_ffn1_norm, m_ffn1_w_in, m_ffn1_w_out, m_mix_norm, m_w_mix_in, m_conv_w, m_w_conv_out, m_w_attn_out, m_w_mix_out, m_ffn2_norm, m_ffn2_w_in, m_ffn2_w_out, m_ple_norm, m_w_ple_gate, m_w_ple_proj, m_final_norm, v_ffn1_norm, v_ffn1_w_in, v_ffn1_w_out, v_mix_norm, v_w_mix_in, v_conv_w, v_w_conv_out, v_w_attn_out, v_w_mix_out, v_ffn2_norm, v_ffn2_w_in, v_ffn2_w_out, v_ple_norm, v_w_ple_gate, v_w_ple_proj, v_final_norm):
    ws = (ffn1_norm, ffn1_w_in, ffn1_w_out, mix_norm, w_mix_in, conv_w, w_conv_out, w_attn_out, w_mix_out, ffn2_norm,
          ffn2_w_in, ffn2_w_out, ple_norm, w_ple_gate, w_ple_proj, final_norm)
    ms = (m_ffn1_norm, m_ffn1_w_in, m_ffn1_w_out, m_mix_norm, m_w_mix_in, m_conv_w, m_w_conv_out, m_w_attn_out,
          m_w_mix_out, m_ffn2_norm, m_ffn2_w_in, m_ffn2_w_out, m_ple_norm, m_w_ple_gate, m_w_ple_proj, m_final_norm)
    vs = (v_ffn1_norm, v_ffn1_w_in, v_ffn1_w_out, v_mix_norm, v_w_mix_in, v_conv_w, v_w_conv_out, v_w_attn_out,
          v_w_mix_out, v_ffn2_norm, v_ffn2_w_in, v_ffn2_w_out, v_ple_norm, v_w_ple_gate, v_w_ple_proj, v_final_norm)
    assert x.shape[0] == 1 and p.shape[:2] == (1, 1), "one sequence and one layer per device"

    def strip(a):
        return a[0] if a.ndim == 3 or (a.ndim == 2 and a.shape[0] == 1) else a

    w = {k: strip(a) for k, a in zip(WEIGHTS, ws)}
    m = {k: strip(a) for k, a in zip(WEIGHTS, ms)}
    v = {k: strip(a) for k, a in zip(WEIGHTS, vs)}
    loss, dx, grad, delta, new_m, new_v = _step(x[0], p[0, 0], loss_target[0], w, m, v)
    shapes = [a.shape for a in ws]
    outs = [loss, dx[None]]
    for res in (grad, delta, new_m, new_v):
        outs += [res[k].reshape(s) for k, s in zip(WEIGHTS, shapes)]
    return tuple(outs)
```

```python
import functools
import math

import jax
import jax.numpy as jnp
from jax import lax
from jax.experimental import pallas as pl
from jax.experimental.pallas import tpu as pltpu

F32 = jnp.float32
BF16 = jnp.bfloat16
MESH = pl.DeviceIdType.MESH
ANY = pl.BlockSpec(memory_space=pl.ANY)

HEAD_DIM = 128
NORM_EPS = 1e-6
N_CHIPS = 4
N_DEV = 8
BF16_ROWS = 16
VMEM_LIMIT = 56 * 1024 * 1024
ACC_BYTES = 8 * 1024 * 1024
STICK_EXIT = 110.0

ADAM_LR = 0.001
ADAM_B1 = 0.9
ADAM_B2 = 0.999
ADAM_EPS = 1e-08
ADAM_WD = 0.01
ADAM_STEP = 10

NN = (((1,), (0,)), ((), ()))
NT = (((1,), (1,)), ((), ()))
TN = (((0,), (0,)), ((), ()))


def _params(sem=None, **kw):
    if sem is not None:
        kw["dimension_semantics"] = sem
    return pltpu.CompilerParams(vmem_limit_bytes=VMEM_LIMIT, **kw)


def _pcall(body, **kw):
    return pl.pallas_call(body, **kw)


def _tile(n, pref, mult=8):
    best = None
    for d in range(mult, min(n, pref) + 1, mult):
        if n % d == 0:
            best = d
    return best if best is not None else n


def _dot(a, b, dims):
    return lax.dot_general(a, b, dims, preferred_element_type=F32)


def _call(name, body, grid, in_specs, out_specs, out_shape, args, scratch=(), sem=None, comm=None):
    n_in, n_out, n_sc = len(in_specs), len(out_specs), len(scratch)
    if comm is None:
        def plain(*refs):
            body(refs[:n_in], refs[n_in:n_in + n_out], refs[n_in + n_out:])

        return _pcall(plain, name=name, out_shape=list(out_shape), grid=grid, in_specs=list(in_specs),
                      out_specs=list(out_specs), scratch_shapes=list(scratch), compiler_params=_params(sem))(*args)
    n_cin, n_cout = len(comm.ins), len(comm.outs)
    steps = math.prod(grid)

    def hosted(*refs):
        ins, c_ins = refs[:n_in], refs[n_in:n_in + n_cin]
        outs = refs[n_in + n_cin:n_in + n_cin + n_out]
        c_outs = refs[n_in + n_cin + n_out:n_in + n_cin + n_out + n_cout]
        rest = refs[n_in + n_cin + n_out + n_cout:]
        sems = rest[n_sc:]
        step = pl.program_id(0)
        for ax in range(1, len(grid)):
            step = step * grid[ax] + pl.program_id(ax)

        @pl.when(step == 0)
        def _():
            comm.first(c_ins, c_outs, sems)

        body(ins, outs, rest[:n_sc])

        @pl.when(step == (3 * steps) // 4)
        def _():
            comm.mid(c_ins, c_outs, sems)

        @pl.when(step == steps - 1)
        def _():
            comm.last(c_ins, c_outs, sems)

    res = _pcall(hosted, name=name, out_shape=list(out_shape) + comm.outs, grid=grid,
                 in_specs=list(in_specs) + [ANY] * n_cin, out_specs=list(out_specs) + [ANY] * n_cout,
                 input_output_aliases={n_in + k: n_out + v for k, v in comm.aliases.items()},
                 scratch_shapes=list(scratch) + comm.sems,
                 compiler_params=_params(("arbitrary",) * len(grid)))(*args, *comm.ins)
    return list(res[:n_out]), list(res[n_out:])


NORM_CHUNK = 256


def _norm_bwd_tile(read_dn, rows, first, h_ref, g_ref, dr_ref, dh_ref, dhb_ref, dg_ref, alpha):
    @pl.when(first)
    def _():
        dg_ref[...] = jnp.zeros_like(dg_ref)

    gv = g_ref[...]
    tot = jnp.zeros_like(gv)
    for c0 in range(0, rows, NORM_CHUNK):
        sl = slice(c0, min(rows, c0 + NORM_CHUNK))
        hv = h_ref[sl, :]
        rs = _rstd(hv)
        hn = hv * rs
        dnv = read_dn(sl)
        gy = dnv * gv
        dh = dr_ref[sl, :] + rs * (gy - hn * jnp.mean(gy * hn, axis=-1, keepdims=True))
        dh_ref[sl, :] = dh
        dhb_ref[sl, :] = (alpha * dh).astype(BF16)
        tot = tot + jnp.sum(dnv * hn, axis=0, keepdims=True)
    dg_ref[...] += tot


def _mm(name, a, b, out_sds, grid, a_spec, b_spec, o_spec, dims, acc_shape, res=None, alpha=1.0, comm=None,
        norm=None):
    nk = grid[2]

    def body(ins, outs, scratch):
        a_ref, b_ref = ins[:2]
        r_ref = ins[2] if res is not None else None
        o_ref = outs[0]

        def finish(read):
            if norm is not None:
                first = jnp.logical_and(pl.program_id(0) == 0, pl.program_id(1) == 0)
                _norm_bwd_tile(read, o_ref.shape[0], first, *ins[2:5], *outs, alpha)
                return
            r = read(slice(None))
            if alpha != 1.0:
                r = r * alpha
            if r_ref is not None:
                r = r_ref[...] + r
            if len(o_ref.shape) == 3:
                half = o_ref.shape[1]
                o_ref[0] = r[:half].astype(o_ref.dtype)
                o_ref[1] = r[half:].astype(o_ref.dtype)
            else:
                o_ref[...] = r.astype(o_ref.dtype)

        if nk == 1:
            part = _dot(a_ref[...].astype(BF16), b_ref[...].astype(BF16), dims)
            finish(lambda sl: part[sl])
        else:
            acc_ref = scratch[0]
            kk = pl.program_id(2)

            @pl.when(kk == 0)
            def _():
                acc_ref[...] = jnp.zeros_like(acc_ref)

            acc_ref[...] += _dot(a_ref[...].astype(BF16), b_ref[...].astype(BF16), dims)

            @pl.when(kk == nk - 1)
            def _():
                finish(lambda sl: acc_ref[sl, :])

    in_specs = [a_spec, b_spec]
    args = [a, b]
    out_specs, out_shape = [o_spec], [out_sds]
    sem = ("parallel", "parallel", "arbitrary")
    if res is not None:
        in_specs.append(o_spec)
        args.append(res)
    if norm is not None:
        width = out_sds.shape[1]
        whole = pl.BlockSpec((1, width), lambda i, j, r: (0, 0))
        in_specs += [o_spec, whole, o_spec]
        args += list(norm)
        out_specs = [o_spec, o_spec, whole]
        out_shape = [jax.ShapeDtypeStruct(out_sds.shape, F32), jax.ShapeDtypeStruct(out_sds.shape, BF16),
                     jax.ShapeDtypeStruct((1, width), F32)]
        sem = ("arbitrary", "arbitrary", "arbitrary")
    scratch = [] if nk == 1 else [pltpu.VMEM(acc_shape, F32)]
    got = _call(name, body, grid, in_specs, out_specs, out_shape, args, scratch, sem, comm)
    if norm is not None:
        return got if comm is None else (got[0], got[1])
    return got[0] if comm is None else (got[0][0], got[1])


def ffn_in_act(name, n, w4, tm, comm=None, gain=None):
    t, d = n.shape
    cs = w4.shape[2]

    def body(ins, outs, scratch):
        wg_ref, wu_ref = ins[-2:]
        a_ref, s_ref = outs[:2]
        if gain is None:
            nv = ins[0][...]
        else:
            @pl.when(pl.program_id(1) == 0)
            def _():
                hv = ins[0][...]
                scratch[0][...] = (hv * _rstd(hv) * ins[1][...]).astype(BF16)
                outs[2][...] = scratch[0][...]

            nv = scratch[0][...]
        gate = _dot(nv, wg_ref[...], NN)
        up = _dot(nv, wu_ref[...], NN)
        a_ref[0] = gate.astype(BF16)
        a_ref[1] = up.astype(BF16)
        s_ref[...] = (gate * jax.nn.sigmoid(gate) * up).astype(BF16)

    rows = pl.BlockSpec((tm, d), lambda i, j: (i, 0))
    in_specs = [rows] + ([] if gain is None else [pl.BlockSpec((1, d), lambda i, j: (0, 0))])
    in_specs += [pl.BlockSpec((None, d, cs), lambda i, j: (j, 0, 0)),
                 pl.BlockSpec((None, d, cs), lambda i, j: (2 + j, 0, 0))]
    out_specs = [pl.BlockSpec((2, tm, cs), lambda i, j: (0, i, j)), pl.BlockSpec((tm, cs), lambda i, j: (i, j))]
    out_shape = [jax.ShapeDtypeStruct((2, t, 2 * cs), BF16), jax.ShapeDtypeStruct((t, 2 * cs), BF16)]
    if gain is not None:
        out_specs.append(rows)
        out_shape.append(jax.ShapeDtypeStruct((t, d), BF16))
    got = _call(name, body, (t // tm, 2), in_specs, out_specs, out_shape,
                [n] + ([] if gain is None else [gain]) + [w4, w4],
                [] if gain is None else [pltpu.VMEM((tm, d), BF16)], ("parallel", "arbitrary"), comm)
    return got if comm is None else (got[0], got[1])


def ffn_ds_dact(name, df, w_out, a3, tm):
    t, d = df.shape
    f = w_out.shape[0]
    cs = f // 2

    def body(ins, outs, scratch):
        df_ref, w_ref, a_ref = ins
        ds = _dot(df_ref[...], w_ref[...], NT)
        for c0 in range(0, tm, NORM_CHUNK):
            sl = slice(c0, min(tm, c0 + NORM_CHUNK))
            gate = a_ref[0, sl, :].astype(F32)
            up = a_ref[1, sl, :].astype(F32)
            sg = jax.nn.sigmoid(gate)
            outs[0][0, sl, :] = (ds[sl] * up * sg * (1.0 + gate * (1.0 - sg))).astype(BF16)
            outs[0][1, sl, :] = (ds[sl] * gate * sg).astype(BF16)

    blk = pl.BlockSpec((2, tm, cs), lambda i, j: (0, i, j))
    return _call(name, body, (t // tm, 2),
                 [pl.BlockSpec((tm, d), lambda i, j: (i, 0)), pl.BlockSpec((cs, d), lambda i, j: (j, 0)), blk],
                 [blk], [jax.ShapeDtypeStruct((2, t, f), BF16)], [df, w_out, a3], (), ("parallel", "parallel"))[0]


def _part_ranges(parts, d):
    out, lo = [], 0
    for p in parts:
        out.append((lo, p.shape[1] // d))
        lo += p.shape[1] // d
    return out, lo


def mm_nt_parts(name, parts, w4, tm, norm, alpha, comm=None):
    m = parts[0].shape[0]
    d, cs = w4.shape[1], w4.shape[2]
    per = cs // d
    ranges, nblk = _part_ranges(parts, d)
    np_ = len(parts)

    def body(ins, outs, scratch):
        w_ref, acc = ins[np_], scratch[0]
        r = pl.program_id(1)

        @pl.when(r == 0)
        def _():
            acc[...] = jnp.zeros_like(acc)

        for (lo, n), a_ref in zip(ranges, ins[:np_]):
            @pl.when(jnp.logical_and(r >= lo, r < lo + n))
            def _(a_ref=a_ref):
                acc[...] += _dot(a_ref[...], w_ref[...], NT)

        @pl.when(r == nblk - 1)
        def _():
            _norm_bwd_tile(lambda sl: acc[sl, :], tm, pl.program_id(0) == 0, *ins[np_ + 1:], *outs, alpha)

    rows = pl.BlockSpec((tm, d), lambda i, r: (i, 0))
    whole = pl.BlockSpec((1, d), lambda i, r: (0, 0))
    specs = [pl.BlockSpec((tm, d), lambda i, r, lo=lo, n=n: (i, jnp.clip(r - lo, 0, n - 1))) for lo, n in ranges]
    specs += [pl.BlockSpec((None, d, d), lambda i, r: (r // per, 0, r % per)), rows, whole, rows]
    got = _call(name, body, (m // tm, nblk), specs, [rows, rows, whole],
                [jax.ShapeDtypeStruct((m, d), F32), jax.ShapeDtypeStruct((m, d), BF16),
                 jax.ShapeDtypeStruct((1, d), F32)],
                list(parts) + [w4] + list(norm), [pltpu.VMEM((tm, d), F32)], ("arbitrary", "arbitrary"), comm)
    return got if comm is None else (got[0], got[1])


def mm_tn_parts(name, xa, parts, tt, comm=None):
    t, k = xa.shape
    d = k
    pr = k // 2
    ranges, nblk = _part_ranges(parts, d)
    per = nblk // N_CHIPS

    def body(ins, outs, scratch):
        x_ref, acc = ins[0], scratch[0]
        jb, r = pl.program_id(0), pl.program_id(1)

        @pl.when(r == 0)
        def _():
            acc[...] = jnp.zeros_like(acc)

        for (lo, n), p_ref in zip(ranges, ins[1:]):
            @pl.when(jnp.logical_and(jb >= lo, jb < lo + n))
            def _(p_ref=p_ref):
                acc[...] += _dot(x_ref[...], p_ref[...], TN)

        @pl.when(r == t // tt - 1)
        def _():
            outs[0][0] = acc[:pr].astype(BF16)
            outs[0][1] = acc[pr:].astype(BF16)

    def part_spec(lo, n):
        return pl.BlockSpec((tt, d), lambda jb, r: (jnp.where(jnp.logical_and(jb >= lo, jb < lo + n), r, 0),
                                                    jnp.clip(jb - lo, 0, n - 1)))

    specs = [pl.BlockSpec((tt, k), lambda jb, r: (r, 0))] + [part_spec(lo, n) for lo, n in ranges]
    got = _call(name, body, (nblk, t // tt), specs,
                [pl.BlockSpec((None, 2, pr, d), lambda jb, r: (jb // per, 0, 0, jb % per))],
                [jax.ShapeDtypeStruct((N_CHIPS, 2, pr, per * d), BF16)], [xa] + list(parts),
                [pltpu.VMEM((k, d), F32)], ("parallel", "arbitrary"), comm)
    return got[0] if comm is None else (got[0][0], got[1])


def mm_nn(name, a, w, out_dtype, tm, res=None, alpha=1.0):
    m, k = a.shape
    n = w.shape[1]
    return _mm(name, a, w, jax.ShapeDtypeStruct((m, n), out_dtype), (m // tm, 1, 1),
               pl.BlockSpec((tm, k), lambda i, j, r: (i, 0)),
               pl.BlockSpec((k, n), lambda i, j, r: (0, 0)),
               pl.BlockSpec((tm, n), lambda i, j, r: (i, 0)), NN, None, res=res, alpha=alpha)


def mm_nn_stacked(name, a, w4, out_dtype, tm, tn, j0=0, nj=None, comm=None):
    m, k = a.shape
    cs = w4.shape[2]
    per = cs // tn
    nj = N_CHIPS * per - j0 if nj is None else nj
    return _mm(name, a, w4, jax.ShapeDtypeStruct((m, nj * tn), out_dtype), (m // tm, nj, 1),
               pl.BlockSpec((tm, k), lambda i, j, r: (i, 0)),
               pl.BlockSpec((None, k, tn), lambda i, j, r: ((j + j0) // per, 0, (j + j0) % per)),
               pl.BlockSpec((tm, tn), lambda i, j, r: (i, j)), NN, None, comm=comm)


def mm_nt(name, dy, w, out_dtype, tm, tko, norm=None, alpha=1.0):
    m, n = dy.shape
    k = w.shape[0]
    return _mm(name, dy, w, jax.ShapeDtypeStruct((m, k), out_dtype), (m // tm, k // tko, 1),
               pl.BlockSpec((tm, n), lambda i, j, r: (i, 0)),
               pl.BlockSpec((tko, n), lambda i, j, r: (j, 0)),
               pl.BlockSpec((tm, tko), lambda i, j, r: (i, j)), NT, None, norm=norm, alpha=alpha)


def mm_nt_stacked(name, dy, w4, out_dtype, tm, tn, comm=None, norm=None, alpha=1.0):
    m = dy.shape[-2]
    k, cs = w4.shape[1], w4.shape[2]
    per = cs // tn
    if dy.ndim == 3:
        dy_spec = pl.BlockSpec((None, tm, cs), lambda i, j, r: (r // 2, i, r % 2))
    else:
        dy_spec = pl.BlockSpec((tm, tn), lambda i, j, r: (i, r))
    return _mm(name, dy, w4, jax.ShapeDtypeStruct((m, k), out_dtype), (m // tm, 1, N_CHIPS * per), dy_spec,
               pl.BlockSpec((None, k, tn), lambda i, j, r: (r // per, 0, r % per)),
               pl.BlockSpec((tm, k), lambda i, j, r: (i, 0)), NT, (tm, k), comm=comm, norm=norm, alpha=alpha)


def mm_tn_rows(name, xa, dy, tt):
    t, k = xa.shape
    n = dy.shape[1]
    tkr = k if k * n * 4 <= ACC_BYTES else k // 2
    return _mm(name, xa, dy, jax.ShapeDtypeStruct((k, n), BF16), (k // tkr, 1, t // tt),
               pl.BlockSpec((tt, tkr), lambda i, j, r: (r, i)),
               pl.BlockSpec((tt, n), lambda i, j, r: (r, 0)),
               pl.BlockSpec((tkr, n), lambda i, j, r: (i, 0)), TN, (tkr, n))


def mm_tn_whole(name, xa, dy, tt):
    t, k = xa.shape
    n = dy.shape[1]
    return _mm(name, xa, dy, jax.ShapeDtypeStruct((k, n), BF16), (1, 1, t // tt),
               pl.BlockSpec((tt, k), lambda i, j, r: (r, 0)),
               pl.BlockSpec((tt, n), lambda i, j, r: (r, 0)),
               pl.BlockSpec((k, n), lambda i, j, r: (0, 0)), TN, (k, n))


def mm_tn_cols(name, xa, dy, tt, comm=None):
    t, k = xa.shape
    pr = k // 2
    if dy.ndim == 3:
        cs = dy.shape[2] // 2
        dy_spec = pl.BlockSpec((None, tt, cs), lambda i, j, r: (j // 2, r, j % 2))
    else:
        cs = dy.shape[1] // N_CHIPS
        dy_spec = pl.BlockSpec((tt, cs), lambda i, j, r: (r, j))
    return _mm(name, xa, dy, jax.ShapeDtypeStruct((N_CHIPS, 2, pr, cs), BF16), (1, N_CHIPS, t // tt),
               pl.BlockSpec((tt, k), lambda i, j, r: (r, 0)), dy_spec,
               pl.BlockSpec((None, 2, pr, cs), lambda i, j, r: (j, 0, 0, 0)), TN, (k, cs), comm=comm)


def _rows(tt, w, col=0):
    return pl.BlockSpec((tt, w), lambda i: (i, col))


def _whole(shape):
    return pl.BlockSpec(shape, lambda i: (0,) * len(shape))


def _rstd(h):
    return lax.rsqrt(jnp.mean(h * h, axis=-1, keepdims=True) + NORM_EPS)


def rms_fwd(name, h, g, tt, comm=None):
    t, d = h.shape

    def body(ins, outs, scratch):
        hv = ins[0][...]
        outs[0][...] = (hv * _rstd(hv) * ins[1][...]).astype(BF16)

    got = _call(name, body, (t // tt,), [_rows(tt, d), _whole((1, d))], [_rows(tt, d)],
                [jax.ShapeDtypeStruct((t, d), BF16)], [h, g], (), ("parallel",), comm)
    return got[0] if comm is None else (got[0][0], got[1])


def mix_out_fwd(name, gates, yc, ya, h, w, tt):
    t, d = yc.shape

    def body(g_ref, yc_ref, ya_ref, h_ref, w_ref, m_ref, o_ref):
        merged = (jax.nn.sigmoid(g_ref[:, :d].astype(F32)) * yc_ref[...].astype(F32)
                  + jax.nn.sigmoid(g_ref[:, d:].astype(F32)) * ya_ref[...].astype(F32)).astype(BF16)
        m_ref[...] = merged
        o_ref[...] = h_ref[...] + _dot(merged, w_ref[...], NN)

    return _pcall(body, name=name,
                  out_shape=(jax.ShapeDtypeStruct((t, d), BF16), jax.ShapeDtypeStruct((t, d), F32)),
                  grid=(t // tt,),
                  in_specs=[_rows(tt, 2 * d), _rows(tt, d), _rows(tt, d), _rows(tt, d), _whole((d, d))],
                  out_specs=(_rows(tt, d), _rows(tt, d)),
                  compiler_params=_params(("parallel",)))(gates, yc, ya, h, w)


def mix_out_bwd(name, dh, w, gates, yc, ya, tt):
    t, d = yc.shape

    def body(dh_ref, w_ref, g_ref, yc_ref, ya_ref, dyc_ref, dya_ref, dg_ref):
        dmv = _dot(dh_ref[...], w_ref[...], NT)
        sc = jax.nn.sigmoid(g_ref[:, :d].astype(F32))
        sa = jax.nn.sigmoid(g_ref[:, d:].astype(F32))
        dyc_ref[...] = (dmv * sc).astype(BF16)
        dya_ref[...] = (dmv * sa).astype(BF16)
        dg_ref[:, :d] = (dmv * yc_ref[...].astype(F32) * sc * (1.0 - sc)).astype(BF16)
        dg_ref[:, d:] = (dmv * ya_ref[...].astype(F32) * sa * (1.0 - sa)).astype(BF16)

    return _pcall(body, name=name,
                  out_shape=(jax.ShapeDtypeStruct((t, d), BF16), jax.ShapeDtypeStruct((t, d), BF16),
                             jax.ShapeDtypeStruct((t, 2 * d), BF16)),
                  grid=(t // tt,),
                  in_specs=[_rows(tt, d), _whole((d, d)), _rows(tt, 2 * d), _rows(tt, d), _rows(tt, d)],
                  out_specs=(_rows(tt, d), _rows(tt, d), _rows(tt, 2 * d)),
                  compiler_params=_params(("parallel",)))(dh, w, gates, yc, ya)


def _shift_down(cur, prev8, s):
    tt = cur.shape[0]
    rolled = pltpu.roll(cur, s, 0)
    row8 = lax.broadcasted_iota(jnp.int32, prev8.shape, 0)
    first8 = jnp.where(row8 < s, pltpu.roll(prev8, s, 0), rolled[:8])
    return jnp.concatenate([first8, rolled[8:]], axis=0) if tt > 8 else first8


def _shift_up(cur, next8, s):
    tt = cur.shape[0]
    rolled = pltpu.roll(cur, tt - s, 0)
    row8 = lax.broadcasted_iota(jnp.int32, next8.shape, 0)
    last8 = jnp.where(row8 >= 8 - s, pltpu.roll(next8, 8 - s, 0), rolled[tt - 8:])
    return jnp.concatenate([rolled[:tt - 8], last8], axis=0) if tt > 8 else last8


def _prev8(tt, d, col):
    return pl.BlockSpec((8, d), lambda i: (jnp.maximum(i * (tt // 8) - 1, 0), col))


def _next8(tt, d, col, t):
    return pl.BlockSpec((8, d), lambda i: (jnp.minimum((i + 1) * (tt // 8), t // 8 - 1), col))


def conv_out_fwd(name, cbx, cw8, w_out, tt):
    t, d3 = cbx.shape
    d = d3 // 3

    def body(cb_ref, cc_ref, cx_ref, pc_ref, px_ref, w_ref, wo_ref, o_ref, y_ref):
        has_prev = (pl.program_id(0) > 0).astype(F32)
        cc = cc_ref[...] * cx_ref[...]
        prev = pc_ref[...] * px_ref[...] * has_prev
        w = w_ref[...]
        conv = w[0:1] * _shift_down(cc, prev, 2) + w[1:2] * _shift_down(cc, prev, 1) + w[2:3] * cc
        ycin = (cb_ref[...] * conv).astype(BF16)
        o_ref[...] = ycin
        y_ref[...] = _dot(ycin, wo_ref[...], NN).astype(BF16)

    out = jax.ShapeDtypeStruct((t, d), BF16)
    return _pcall(body, name=name, out_shape=(out, out), grid=(t // tt,),
                  in_specs=[_rows(tt, d, 0), _rows(tt, d, 1), _rows(tt, d, 2), _prev8(tt, d, 1), _prev8(tt, d, 2),
                            _whole((8, d)), _whole((d, d))],
                  out_specs=(_rows(tt, d), _rows(tt, d)),
                  compiler_params=_params(("parallel",)))(cbx, cbx, cbx, cbx, cbx, cw8, w_out)


def conv_out_bwd(name, dyc, w_out, cbx, cw8, tt):
    t, d3 = cbx.shape
    d = d3 // 3
    n = t // tt

    def body(dy_ref, ndy_ref, wo_ref, cb_ref, cc_ref, cx_ref, pc_ref, px_ref, ncb_ref, w_ref, o_ref, dw_ref):
        i = pl.program_id(0)
        has_prev = (i > 0).astype(F32)
        has_next = (i < n - 1).astype(F32)
        cb = cb_ref[...]
        cc = cc_ref[...] * cx_ref[...]
        prev = pc_ref[...] * px_ref[...] * has_prev
        w = w_ref[...]
        cc1 = _shift_down(cc, prev, 1)
        cc2 = _shift_down(cc, prev, 2)
        conv = w[0:1] * cc2 + w[1:2] * cc1 + w[2:3] * cc
        dyv = _dot(dy_ref[...], wo_ref[...], NT)
        dconv = dyv * cb
        dnext = _dot(ndy_ref[...], wo_ref[...], NT)[:8] * ncb_ref[...] * has_next
        dcc = w[2:3] * dconv + w[1:2] * _shift_up(dconv, dnext, 1) + w[0:1] * _shift_up(dconv, dnext, 2)
        o_ref[:, :d] = (dyv * conv).astype(BF16)
        o_ref[:, d:2 * d] = (dcc * cx_ref[...]).astype(BF16)
        o_ref[:, 2 * d:] = (dcc * cc_ref[...]).astype(BF16)

        @pl.when(i == 0)
        def _():
            dw_ref[...] = jnp.zeros_like(dw_ref)

        dw_ref[0:1, :] += jnp.sum(dconv * cc2, axis=0, keepdims=True)
        dw_ref[1:2, :] += jnp.sum(dconv * cc1, axis=0, keepdims=True)
        dw_ref[2:3, :] += jnp.sum(dconv * cc, axis=0, keepdims=True)

    return _pcall(body, name=name,
                  out_shape=(jax.ShapeDtypeStruct((t, d3), BF16), jax.ShapeDtypeStruct((8, d), F32)),
                  grid=(n,),
                  in_specs=[_rows(tt, d),
                            pl.BlockSpec((BF16_ROWS, d), lambda i: (jnp.minimum((i + 1) * (tt // BF16_ROWS),
                                                                                t // BF16_ROWS - 1), 0)),
                            _whole((d, d)), _rows(tt, d, 0), _rows(tt, d, 1), _rows(tt, d, 2),
                            _prev8(tt, d, 1), _prev8(tt, d, 2), _next8(tt, d, 0, t), _whole((8, d))],
                  out_specs=(_rows(tt, d3), _whole((8, d))),
                  compiler_params=_params(("arbitrary",)))(dyc, dyc, w_out, cbx, cbx, cbx, cbx, cbx, cbx, cw8)


def tail(name, h3, p, tgt, gp, gf, w_gate, w_proj, tt):
    t, d = h3.shape
    pd = p.shape[1]

    def body(h_ref, p_ref, tg_ref, gp_ref, gf_ref, wg_ref, wp_ref, np_ref, dh_ref, dpp_ref, dzg_ref, dgf_ref,
             loss_ref):
        hv = h_ref[...]
        npl = (hv * _rstd(hv) * gp_ref[...]).astype(BF16)
        np_ref[...] = npl
        pg = jax.nn.sigmoid(_dot(npl, wg_ref[...], NN))
        ppv = _dot(p_ref[...].astype(BF16), wp_ref[...], NN)
        h4 = hv + pg * ppv
        r4 = _rstd(h4)
        hn = h4 * r4
        gfv = gf_ref[...]
        err = hn * gfv - tg_ref[...]
        dy = err * (1.0 / d)
        gy = dy * gfv
        dh4 = r4 * (gy - hn * jnp.mean(gy * hn, axis=-1, keepdims=True))
        dh_ref[...] = dh4
        dpp_ref[...] = (dh4 * pg).astype(BF16)
        dzg_ref[...] = (dh4 * ppv * pg * (1.0 - pg)).astype(BF16)

        @pl.when(pl.program_id(0) == 0)
        def _():
            dgf_ref[...] = jnp.zeros_like(dgf_ref)
            loss_ref[...] = jnp.zeros_like(loss_ref)

        dgf_ref[...] += jnp.sum(dy * hn, axis=0, keepdims=True)
        tok = jnp.mean(err * err, axis=-1, keepdims=True)
        loss_ref[...] += 0.5 * jnp.sum(tok, axis=0, keepdims=True) * jnp.ones((1, loss_ref.shape[1]), F32)

    return _pcall(body, name=name,
                  out_shape=(jax.ShapeDtypeStruct((t, d), BF16), jax.ShapeDtypeStruct((t, d), F32),
                             jax.ShapeDtypeStruct((t, d), BF16), jax.ShapeDtypeStruct((t, d), BF16),
                             jax.ShapeDtypeStruct((1, d), F32), jax.ShapeDtypeStruct((1, d), F32)),
                  grid=(t // tt,),
                  in_specs=[_rows(tt, d), _rows(tt, pd), _rows(tt, d), _whole((1, d)), _whole((1, d)),
                            _whole((d, d)), _whole((pd, d))],
                  out_specs=(_rows(tt, d), _rows(tt, d), _rows(tt, d), _rows(tt, d), _whole((1, d)),
                             _whole((1, d))),
                  compiler_params=_params(("arbitrary",)))(h3, p, tgt, gp, gf, w_gate, w_proj)


SCALE = 1.0 / math.sqrt(HEAD_DIM)


def _log_stick(z):
    return -(jnp.maximum(z, 0.0) + jnp.log(1.0 + jnp.exp(-jnp.abs(z))))


def _tri_sum(x, tri):
    hi = x.astype(BF16)
    lo = (x - hi.astype(F32)).astype(BF16)
    return _dot(hi, tri, NN) + _dot(lo, tri, NN)


KEY_BLOCK = 128
NEAR = 3


def _sb_near(qs, jds, k_ref, below, upper):
    pairs = [(s, b) for s in range(len(qs)) for b in range(NEAR)]
    rows = {(s, b): _block_rows(jnp.maximum(jds[s] - b, 0), KEY_BLOCK) for s, b in pairs}
    z = {(s, b): _dot(qs[s], k_ref[rows[s, b], :], NT) * SCALE for s, b in pairs}
    lg = {(s, b): jnp.where(below, _log_stick(z[s, b]), 0.0) if b == 0 else _log_stick(z[s, b]) for s, b in pairs}
    cum = {(s, b): _tri_sum(lg[s, b], upper) for s, b in pairs}
    out, carries = [], []
    for s in range(len(qs)):
        c = cum[s, 0][:, 0:1]
        blocks = [(rows[s, 0], z[s, 0], jnp.exp(jnp.where(below, z[s, 0] + cum[s, 0], -1e30)))]
        for b in range(1, NEAR):
            live = jds[s] >= b
            blocks.append((rows[s, b], z[s, b], jnp.exp(z[s, b] + cum[s, b] + (c + jnp.where(live, 0.0, -1e30)))))
            c = c + jnp.where(live, cum[s, b][:, 0:1], 0.0)
        out.append(blocks)
        carries.append(c)
    return out, carries


def _sb_far(q, kj, upper, c):
    z = _dot(q, kj, NT) * SCALE
    cum = _tri_sum(_log_stick(z), upper)
    return z, jnp.exp(z + cum + c), c + cum[:, 0:1]


def _block_rows(j, size):
    return pl.ds(pl.multiple_of(j * size, size), size)


def _sweep_on(st):
    return jnp.logical_and(st[0] >= 0, jnp.max(st[1]) > -STICK_EXIT)


def attn_fwd(name, qkv, tq):
    t, d3 = qkv.shape
    d = d3 // 3
    nh = d // HEAD_DIM
    nq = t // tq
    tb = KEY_BLOCK
    nsub = tq // tb

    def body(q_ref, k_ref, v_ref, o_ref):
        i = pl.program_id(1)
        row = lax.broadcasted_iota(jnp.int32, (tb, tb), 0)
        col = lax.broadcasted_iota(jnp.int32, (tb, tb), 1)
        upper = (row >= col).astype(BF16)
        qs = [q_ref[s * tb:(s + 1) * tb, :] for s in range(nsub)]
        jds = [i * nsub + s for s in range(nsub)]
        near, carries = _sb_near(qs, jds, k_ref, col < row, upper)
        state = []
        for s in range(nsub):
            acc = jnp.zeros((tb, HEAD_DIM), F32)
            for rows, _, a in near[s]:
                acc = acc + _dot(a.astype(BF16), v_ref[rows, :], NN)
            state.append((qs[s], jds[s], carries[s], acc))
        for s, (q, jd, c, acc) in enumerate(state):

            def step(st, q=q):
                rows = _block_rows(st[0], tb)
                _, a, c2 = _sb_far(q, k_ref[rows, :], upper, st[1])
                return st[0] - 1, c2, st[2] + _dot(a.astype(BF16), v_ref[rows, :], NN)

            _, _, acc = lax.while_loop(_sweep_on, step, (jd - NEAR, c, acc))
            o_ref[s * tb:(s + 1) * tb, :] = acc.astype(o_ref.dtype)

    return _pcall(body, name=name, out_shape=jax.ShapeDtypeStruct((t, d), BF16), grid=(nh, nq),
                  in_specs=[pl.BlockSpec((tq, HEAD_DIM), lambda h, i: (i, h)),
                            pl.BlockSpec((t, HEAD_DIM), lambda h, i: (0, nh + h)),
                            pl.BlockSpec((t, HEAD_DIM), lambda h, i: (0, 2 * nh + h))],
                  out_specs=pl.BlockSpec((tq, HEAD_DIM), lambda h, i: (i, h)),
                  compiler_params=_params(("parallel", "arbitrary")))(qkv, qkv, qkv)


def attn_bwd(name, qkv, do, tq):
    t, d3 = qkv.shape
    d = d3 // 3
    nh = d // HEAD_DIM
    nq = t // tq
    tb = KEY_BLOCK
    nsub = tq // tb

    def body(q_ref, k_ref, v_ref, do_ref, dq_ref, dk_ref, dv_ref, dk_acc, dv_acc, g_buf, z_buf):
        i = pl.program_id(1)

        @pl.when(i == 0)
        def _():
            dk_acc[...] = jnp.zeros_like(dk_acc)
            dv_acc[...] = jnp.zeros_like(dv_acc)

        row = lax.broadcasted_iota(jnp.int32, (tb, tb), 0)
        col = lax.broadcasted_iota(jnp.int32, (tb, tb), 1)
        below = col < row
        upper = (row >= col).astype(BF16)
        lower = (row <= col).astype(BF16)

        qs = [q_ref[s * tb:(s + 1) * tb, :] for s in range(nsub)]
        dos = [do_ref[s * tb:(s + 1) * tb, :] for s in range(nsub)]
        jds = [i * nsub + s for s in range(nsub)]
        near, carries = _sb_near(qs, jds, k_ref, below, upper)
        da = [[_dot(dos[s], v_ref[rows, :], NT) for rows, _, _ in near[s]] for s in range(nsub)]
        state = []
        for s in range(nsub):
            kept = [(rows, z, da[s][b] * a) for b, (rows, z, a) in enumerate(near[s])]
            for rows, _, a in near[s]:
                dv_acc[rows, :] += _dot(a.astype(BF16), dos[s], TN)
            state.append((qs[s], dos[s], jds[s], carries[s], kept))

        carried = []
        for s, (q, dov, jd, c, kept) in enumerate(state):
            def step(st, s=s, q=q, dov=dov, jd=jd):
                j = st[0]
                rows = _block_rows(j, tb)
                z, a, c2 = _sb_far(q, k_ref[rows, :], upper, st[1])
                g_buf[jd - j] = _dot(dov, v_ref[rows, :], NT) * a
                z_buf[jd - j] = z
                dv_acc[rows, :] += _dot(a.astype(BF16), dov, TN)
                return j - 1, c2

            j_stop, _ = lax.while_loop(_sweep_on, step, (jd - NEAR, c))

            def far(j, st, s=s, q=q, jd=jd):
                run, dq = st
                rows = _block_rows(j, tb)
                g = g_buf[jd - j]
                dz = (g - jax.nn.sigmoid(z_buf[jd - j]) * (run + _tri_sum(g, lower))).astype(BF16)
                dk_acc[rows, :] += _dot(dz, q, TN)
                return run + jnp.sum(g, axis=1, keepdims=True), dq + _dot(dz, k_ref[rows, :], NN)

            carried.append(lax.fori_loop(j_stop + 1, jd - NEAR + 1, far,
                                         (jnp.zeros((tb, 1), F32), jnp.zeros((tb, HEAD_DIM), F32))))

        tri = [[_tri_sum(g, lower) for _, _, g in st[4]] for st in state]
        sig = [[jax.nn.sigmoid(z) for _, z, _ in st[4]] for st in state]
        for s, (q, dov, jd, c, kept) in enumerate(state):
            run, dq = carried[s]
            for b in reversed(range(NEAR)):
                rows, z, g = kept[b]
                dz = g - sig[s][b] * (run + tri[s][b])
                if b == 0:
                    dz = jnp.where(below, dz, 0.0)
                dz = dz.astype(BF16)
                dk_acc[rows, :] += _dot(dz, q, TN)
                dq = dq + _dot(dz, k_ref[rows, :], NN)
                if b:
                    run = run + jnp.sum(g, axis=1, keepdims=True)
            dq_ref[s * tb:(s + 1) * tb, :] = (dq * SCALE).astype(BF16)

        @pl.when(i == nq - 1)
        def _():
            dk_ref[...] = (dk_acc[...] * SCALE).astype(BF16)
            dv_ref[...] = dv_acc[...].astype(BF16)

    blk = pl.BlockSpec((tq, HEAD_DIM), lambda h, i: (i, h))
    col_h = pl.BlockSpec((t, HEAD_DIM), lambda h, i: (0, h))
    out = jax.ShapeDtypeStruct((t, d), BF16)
    return _pcall(body, name=name, out_shape=(out, out, out), grid=(nh, nq),
                  in_specs=[blk,
                            pl.BlockSpec((t, HEAD_DIM), lambda h, i: (0, nh + h)),
                            pl.BlockSpec((t, HEAD_DIM), lambda h, i: (0, 2 * nh + h)),
                            blk],
                  out_specs=(blk, col_h, col_h),
                  scratch_shapes=[pltpu.VMEM((t, HEAD_DIM), F32), pltpu.VMEM((t, HEAD_DIM), F32),
                                  pltpu.VMEM((t // tb, tb, tb), F32), pltpu.VMEM((t // tb, tb, tb), F32)],
                  compiler_params=_params(("parallel", "arbitrary")))(qkv, qkv, qkv, do)


def _place():
    x, y, c = lax.axis_index("x"), lax.axis_index("y"), lax.axis_index("c")
    chips = [(1 - x, y), (x, 1 - y), (1 - x, 1 - y)]
    return x, y, c, chips


def _remote(src, dst, send_sem, recv_sem, dev):
    return pltpu.make_async_remote_copy(src_ref=src, dst_ref=dst, send_sem=send_sem, recv_sem=recv_sem,
                                        device_id=dev, device_id_type=MESH)


def place_shard(name, w, chip):
    r, cdim = w.shape
    tr = _tile(r, max(BF16_ROWS, (1 << 19) // cdim), BF16_ROWS)

    def body(chip_ref, w_ref, o_ref):
        o_ref[...] = w_ref[...].astype(BF16)

    spec = pltpu.PrefetchScalarGridSpec(
        num_scalar_prefetch=1, grid=(r // tr,),
        in_specs=[pl.BlockSpec((tr, cdim), lambda i, s: (i, 0))],
        out_specs=pl.BlockSpec((None, tr, cdim), lambda i, s: (s[0], i, 0)))
    return _pcall(body, name=name, out_shape=jax.ShapeDtypeStruct((N_CHIPS, r, cdim), BF16), grid_spec=spec,
                  compiler_params=_params(("parallel",)))(chip, w)


class Comm:
    def __init__(self, ins, outs, aliases, sems, first, mid, last):
        self.ins, self.outs, self.aliases, self.sems = list(ins), list(outs), dict(aliases), list(sems)
        self.first, self.mid, self.last = first, mid, last


def run_comm(name, comm):
    ni, no = len(comm.ins), len(comm.outs)

    def body(*refs):
        ins, outs, sems = refs[:ni], refs[ni:ni + no], refs[ni + no:]
        comm.first(ins, outs, sems)
        comm.mid(ins, outs, sems)
        comm.last(ins, outs, sems)

    return _pcall(body, name=name, out_shape=comm.outs, in_specs=[ANY] * ni, out_specs=[ANY] * no,
                  input_output_aliases=comm.aliases, scratch_shapes=comm.sems, compiler_params=_params())(*comm.ins)


def gather_comm(bufs):
    n = len(bufs)

    def half(out, w, which):
        pr = out[w].shape[1] // 2
        return pl.ds(pl.multiple_of(which * pr, BF16_ROWS), pr)

    def first(ins, out, sems):
        isend, irecv, _, _ = sems
        x, y, c, chips = _place()
        for w in range(n):
            mine = out[w].at[2 * x + y, half(out, w, c)]
            for j, (cx, cy) in enumerate(chips):
                _remote(mine, mine, isend.at[3 * w + j], irecv.at[3 * w + j], (cx, cy, c)).start()

    def mid(ins, out, sems):
        isend, irecv, dsend, drecv = sems
        x, y, c, chips = _place()
        sib = (x, y, 1 - c)
        for w in range(n):
            for j, (cx, cy) in enumerate(chips):
                landed = out[w].at[2 * cx + cy, half(out, w, c)]
                _remote(landed, landed, isend.at[3 * w + j], irecv.at[3 * w + j], sib).wait_recv()
                _remote(landed, landed, dsend.at[3 * w + j], drecv.at[3 * w + j], sib).start()

    def last(ins, out, sems):
        isend, irecv, dsend, drecv = sems
        x, y, c, chips = _place()
        sib = (x, y, 1 - c)
        for w in range(n):
            for j, (cx, cy) in enumerate(chips):
                landed = out[w].at[2 * cx + cy, half(out, w, 1 - c)]
                _remote(landed, landed, dsend.at[3 * w + j], drecv.at[3 * w + j], sib).wait_recv()
        for w in range(n):
            sent = out[w].at[0, half(out, w, c)]
            for j in range(3):
                _remote(sent, sent, isend.at[3 * w + j], irecv.at[3 * w + j], sib).wait_send()
                _remote(sent, sent, dsend.at[3 * w + j], drecv.at[3 * w + j], sib).wait_send()

    return Comm(bufs, [jax.ShapeDtypeStruct(s.shape, s.dtype) for s in bufs], {w: w for w in range(n)},
                [pltpu.SemaphoreType.DMA((3 * n,))] * 4, first, mid, last)


def _nothing(ins, outs, sems):
    return None


def join_comms(a, b):
    ni, no, ns = len(a.ins), len(a.outs), len(a.sems)

    def both(f, g):
        def hook(ins, outs, sems):
            f(ins[:ni], outs[:no], sems[:ns])
            g(ins[ni:], outs[no:], sems[ns:])
        return hook

    aliases = dict(a.aliases)
    aliases.update({ni + k: no + v for k, v in b.aliases.items()})
    return Comm(a.ins + b.ins, a.outs + b.outs, aliases, a.sems + b.sems,
                both(a.first, b.first), both(a.mid, b.mid), both(a.last, b.last))


def exchange_comm(pieces):
    n = len(pieces)

    def copies(src, out, sems):
        x, y, c, _ = _place()
        return [_remote(src[w].at[k, 1 - c], out[w].at[k], sems[0].at[N_CHIPS * w + k], sems[1].at[N_CHIPS * w + k],
                        (x, y, 1 - c)) for w in range(n) for k in range(N_CHIPS)]

    def first(src, out, sems):
        for cp in copies(src, out, sems):
            cp.start()

    def last(src, out, sems):
        for cp in copies(src, out, sems):
            cp.wait()

    return Comm(pieces, [jax.ShapeDtypeStruct((N_CHIPS,) + s.shape[2:], s.dtype) for s in pieces], {},
                [pltpu.SemaphoreType.DMA((N_CHIPS * n,))] * 2, first, _nothing, last)


def scatter_comm(parts):
    n = len(parts)

    def copies(src, out, sems):
        x, y, c, chips = _place()
        return [_remote(src[w].at[2 * cx + cy], out[w].at[j], sems[0].at[3 * w + j], sems[1].at[3 * w + j], (cx, cy, c))
                for w in range(n) for j, (cx, cy) in enumerate(chips)]

    def first(src, out, sems):
        for cp in copies(src, out, sems):
            cp.start()

    def last(src, out, sems):
        for cp in copies(src, out, sems):
            cp.wait()

    return Comm(parts, [jax.ShapeDtypeStruct((3,) + s.shape[1:], s.dtype) for s in parts], {},
                [pltpu.SemaphoreType.DMA((3 * n,))] * 2, first, _nothing, last)


def share_comm(halves):
    n = len(halves)

    def first(ins, buf, sems):
        x, y, c, _ = _place()
        for w in range(n):
            _remote(buf[w].at[c], buf[w].at[c], sems[0].at[w], sems[1].at[w], (x, y, 1 - c)).start()

    def last(ins, buf, sems):
        x, y, c, _ = _place()
        for w in range(n):
            landed = buf[w].at[1 - c]
            _remote(landed, landed, sems[0].at[w], sems[1].at[w], (x, y, 1 - c)).wait_recv()
        for w in range(n):
            _remote(buf[w].at[c], buf[w].at[c], sems[0].at[w], sems[1].at[w], (x, y, 1 - c)).wait_send()

    return Comm(halves, [jax.ShapeDtypeStruct(s.shape, s.dtype) for s in halves], {w: w for w in range(n)},
                [pltpu.SemaphoreType.DMA((n,))] * 2, first, _nothing, last)


def gather_small(name, blk, reduce):
    r, cdim = blk.shape

    def body(in_ref, out_ref, *rest):
        if reduce:
            buf, send_sem, recv_sem = rest
        else:
            buf = out_ref
            send_sem, recv_sem = rest
        x, y, c, _ = _place()
        me = 4 * x + 2 * y + c
        buf[me] = in_ref[...]
        peers = []
        for dx in range(2):
            for dy in range(2):
                for dc in range(2):
                    if dx or dy or dc:
                        peers.append((dx, dy, dc))
        copies = []
        for s, (dx, dy, dc) in enumerate(peers):
            cp = _remote(in_ref, buf.at[me], send_sem.at[s], recv_sem.at[s],
                         ((1 - x if dx else x), (1 - y if dy else y), (1 - c if dc else c)))
            cp.start()
            copies.append(cp)
        for s, (dx, dy, dc) in enumerate(peers):
            px, py, pc_ = (1 - x if dx else x), (1 - y if dy else y), (1 - c if dc else c)
            landed = buf.at[4 * px + 2 * py + pc_]
            _remote(landed, landed, send_sem.at[s], recv_sem.at[s], (x, y, c)).wait_recv()
        for cp in copies:
            cp.wait_send()
        if reduce:
            tot = buf[0]
            for s in range(1, N_DEV):
                tot = tot + buf[s]
            out_ref[...] = tot

    vm = pl.BlockSpec(memory_space=pltpu.VMEM)
    out_shape = jax.ShapeDtypeStruct((r, cdim) if reduce else (N_DEV, r, cdim), F32)
    scratch = ([pltpu.VMEM((N_DEV, r, cdim), F32)] if reduce else []) + [pltpu.SemaphoreType.DMA((N_DEV - 1,))] * 2
    return _pcall(body, name=name, out_shape=out_shape, in_specs=[vm], out_specs=vm, scratch_shapes=scratch,
                  compiler_params=_params())(blk)


def sum_cores(name, own, got, place):
    _, _, pr, pc = own.shape
    tr = _tile(pr, max(BF16_ROWS, (1 << 19) // pc), BF16_ROWS)

    def body(place_ref, own_ref, got_ref, o_ref):
        o_ref[...] = (own_ref[...].astype(F32) + got_ref[...].astype(F32)).astype(o_ref.dtype)

    spec = pltpu.PrefetchScalarGridSpec(
        num_scalar_prefetch=1, grid=(N_CHIPS, pr // tr),
        in_specs=[pl.BlockSpec((None, None, tr, pc), lambda k, i, s: (k, s[1], i, 0)),
                  pl.BlockSpec((None, tr, pc), lambda k, i, s: (k, i, 0))],
        out_specs=pl.BlockSpec((None, tr, pc), lambda k, i, s: (k, i, 0)))
    return _pcall(body, name=name, out_shape=jax.ShapeDtypeStruct((N_CHIPS, pr, pc), BF16), grid_spec=spec,
                  compiler_params=_params(("parallel", "parallel")))(place, own, got)


def sum_chips(name, part, got, place):
    _, pr, pc = part.shape
    tr = _tile(pr, max(BF16_ROWS, (1 << 18) // pc), BF16_ROWS)

    def body(place_ref, part_ref, got_ref, o_ref):
        tot = part_ref[...].astype(F32)
        for j in range(3):
            tot = tot + got_ref[j].astype(F32)
        o_ref[...] = tot

    spec = pltpu.PrefetchScalarGridSpec(
        num_scalar_prefetch=1, grid=(pr // tr,),
        in_specs=[pl.BlockSpec((None, tr, pc), lambda i, s: (s[0], i, 0)),
                  pl.BlockSpec((3, tr, pc), lambda i, s: (0, i, 0))],
        out_specs=pl.BlockSpec((None, tr, pc), lambda i, s: (s[1], i, 0)))
    return _pcall(body, name=name, out_shape=jax.ShapeDtypeStruct((2, pr, pc), F32), grid_spec=spec,
                  compiler_params=_params(("parallel",)))(place, part, got)


def adamw(name, w, g, m, v):
    rows, cols = w.shape
    tr = _tile(rows, max(8, (1 << 18) // cols))
    c1 = 1.0 / (1.0 - ADAM_B1 ** ADAM_STEP)
    c2 = 1.0 / (1.0 - ADAM_B2 ** ADAM_STEP)

    def body(w_ref, g_ref, m_ref, v_ref, d_ref, nm_ref, nv_ref):
        gv = g_ref[...]
        nm = ADAM_B1 * m_ref[...] + (1.0 - ADAM_B1) * gv
        nv = ADAM_B2 * v_ref[...] + (1.0 - ADAM_B2) * (gv * gv)
        nm_ref[...] = nm
        nv_ref[...] = nv
        d_ref[...] = -ADAM_LR * ((nm * c1) / (jnp.sqrt(nv * c2) + ADAM_EPS) + ADAM_WD * w_ref[...])

    spec = pl.BlockSpec((tr, cols), lambda i: (i, 0))
    sds = jax.ShapeDtypeStruct((rows, cols), F32)
    return _pcall(body, name=name, out_shape=(sds, sds, sds), grid=(rows // tr,),
                  in_specs=[spec] * 4, out_specs=(spec, spec, spec),
                  compiler_params=_params(("parallel",)))(w, g, m, v)


MATS = ["ffn1_w_in", "ffn1_w_out", "w_mix_in", "w_conv_out", "w_attn_out", "w_mix_out", "ffn2_w_in", "ffn2_w_out",
        "w_ple_gate", "w_ple_proj"]
COL_SHARDED = {"ffn1_w_in", "w_mix_in", "ffn2_w_in", "w_ple_proj"}
NORMS = ["ffn1_norm", "mix_norm", "ffn2_norm", "ple_norm", "final_norm"]
WEIGHTS = ["ffn1_norm", "ffn1_w_in", "ffn1_w_out", "mix_norm", "w_mix_in", "conv_w", "w_conv_out", "w_attn_out",
           "w_mix_out", "ffn2_norm", "ffn2_w_in", "ffn2_w_out", "ple_norm", "w_ple_gate", "w_ple_proj", "final_norm"]


def _pad_rows(a, rows):
    return jnp.concatenate([a, jnp.zeros((rows - a.shape[0],) + a.shape[1:], a.dtype)], axis=0)


def _step(x, p, tgt, w, m, v):
    t, d = x.shape
    tt = _tile(t, 256)
    tm = _tile(t, 512)
    tm2 = _tile(t, 1024)
    tq = _tile(t, 1024)

    chip = 2 * lax.axis_index("x") + lax.axis_index("y")
    place = jnp.stack([chip, lax.axis_index("c")]).astype(jnp.int32)

    placed = {k: place_shard("place_" + k, w[k], place) for k in MATS}
    full = {}

    def keep(names, bufs):
        for k, buf in zip(names, bufs):
            full[k] = buf if k in COL_SHARDED else buf.reshape(-1, buf.shape[2])

    def gather_of(names):
        return gather_comm([placed[k] for k in names])

    cw_all = gather_small("gather_conv_w", _pad_rows(w["conv_w"], 8), False)
    cw8 = jnp.concatenate([cw_all[2 * k] for k in range(N_CHIPS)], axis=1)
    g1, gm, g2, gp, gf = (w[k].reshape(1, d) for k in NORMS)

    def ffn_fwd(tag, h, g, first, w_in_name, w_out_name, riders):
        if first:
            n, bufs = rms_fwd(tag + "_norm", h, g, tt, comm=gather_of(first))
            keep(first, bufs)
            (a, s), bufs = ffn_in_act(tag + "_in", n, full[w_in_name], tm, comm=gather_of(riders))
            keep(riders, bufs)
        else:
            a, s, n = ffn_in_act(tag + "_in", h, full[w_in_name], tm, gain=g)
        return n, a, s, mm_nn(tag + "_out", s, full[w_out_name], F32, tm, res=h, alpha=0.5)

    n1, a1, s1, h1 = ffn_fwd("ffn1", x, g1, ["ffn1_w_in"], "ffn1_w_in", "ffn1_w_out", ["ffn1_w_out", "w_mix_in"])
    u = rms_fwd("mix_norm", h1, gm, tt)
    wmix = full["w_mix_in"]
    riders = [["w_conv_out", "w_attn_out", "w_mix_out"], ["ffn2_w_in"], ["ffn2_w_out", "w_ple_gate", "w_ple_proj"]]
    cbx, bufs = mm_nn_stacked("mix_in_conv", u, wmix, F32, tm2, d, 0, 3, comm=gather_of(riders[0]))
    keep(riders[0], bufs)
    qkv, bufs = mm_nn_stacked("mix_in_qkv", u, wmix, BF16, tm2, d, 3, 3, comm=gather_of(riders[1]))
    keep(riders[1], bufs)
    gates, bufs = mm_nn_stacked("mix_in_gates", u, wmix, BF16, tm2, d, 6, 2, comm=gather_of(riders[2]))
    keep(riders[2], bufs)
    wpp = full["w_ple_proj"]
    wpp = jnp.transpose(wpp, (1, 0, 2)).reshape(wpp.shape[1], -1)
    ycin, y_conv = conv_out_fwd("conv_out", cbx, cw8, full["w_conv_out"], tt)
    o = attn_fwd("attn", qkv, tq)
    y_attn = mm_nn("attn_out", o, full["w_attn_out"], BF16, tm)
    merged, h2 = mix_out_fwd("mix_out", gates, y_conv, y_attn, h1, full["w_mix_out"], tm)
    n2, a2, s2, h3 = ffn_fwd("ffn2", h2, g2, [], "ffn2_w_in", "ffn2_w_out", [])

    pieces, chip_sums, halves = {}, {}, {}

    def as_pieces(k):
        pc = pieces[k]
        return pc if k in COL_SHARDED else pc.reshape(N_CHIPS, 2, pc.shape[0] // (2 * N_CHIPS), pc.shape[1])

    def sum_siblings(tag, names):
        pcs = [as_pieces(k) for k in names]
        got = run_comm("exchange_" + tag, exchange_comm(pcs))
        for k, a, b in zip(names, pcs, got):
            chip_sums[k] = sum_cores("sum_cores_" + k, a, b, place)

    def scatter_of(names):
        return scatter_comm([chip_sums[k] for k in names])

    def sum_landed(names, landed):
        for k, b in zip(names, landed):
            halves[k] = sum_chips("sum_chips_" + k, chip_sums[k], b, place)

    npl, dh4, dpp, dzg, dgf, loss_row = tail("tail", h3, p, tgt, gp, gf, full["w_ple_gate"], wpp, tt)
    dwpp = mm_tn_whole("ple_proj_dw", p, dpp, tm2)
    pieces["w_ple_proj"] = jnp.transpose(dwpp.reshape(2, p.shape[1] // 2, N_CHIPS, d // N_CHIPS), (2, 0, 1, 3))
    pieces["w_ple_gate"] = mm_tn_rows("ple_gate_dw", npl, dzg, tm2)
    dh3, df2, dgp = mm_nt("ple_gate_dx", dzg, full["w_ple_gate"], F32, tm, d, norm=(h3, gp, dh4), alpha=0.5)
    w_in, w_out = full["ffn2_w_in"], full["ffn2_w_out"]
    pieces["ffn2_w_out"] = mm_tn_rows("ffn2_dwout", s2, df2, tm2)
    da2 = ffn_ds_dact("ffn2_ds", df2, w_out, a2, tm2)
    pieces["ffn2_w_in"] = mm_tn_cols("ffn2_dwin", n2, da2, tm2)
    dh2, dh2b, dg2 = mm_nt_stacked("ffn2_dn", da2, w_in, F32, tm2, w_in.shape[2], norm=(h2, g2, dh3))
    pieces["w_mix_out"] = mm_tn_rows("mix_out_dw", merged, dh2b, tm2)
    dyc, dya, dgates = mix_out_bwd("mix_out_dx", dh2b, full["w_mix_out"], gates, y_conv, y_attn, tm)
    pieces["w_conv_out"] = mm_tn_rows("conv_out_dw", ycin, dyc, tm2)
    dcbx, dcw8 = conv_out_bwd("conv_out_dx", dyc, full["w_conv_out"], cbx, cw8, tt)
    pieces["w_attn_out"] = mm_tn_rows("attn_out_dw", o, dya, tm2)
    do = mm_nt("attn_out_dx", dya, full["w_attn_out"], BF16, tm, d)
    dq, dk, dv = attn_bwd("attn_bwd", qkv, do, tq)
    dmix = [dcbx, dq, dk, dv, dgates]
    early = ["ffn2_w_in", "ffn2_w_out", "w_ple_gate", "w_ple_proj", "w_mix_out", "w_conv_out", "w_attn_out"]
    swap = exchange_comm([as_pieces(k) for k in early])
    pieces["w_mix_in"], got = mm_tn_parts("mix_in_dw", u, dmix, tm2, comm=swap)
    for k, a, b in zip(early, swap.ins, got):
        chip_sums[k] = sum_cores("sum_cores_" + k, a, b, place)
    swap = exchange_comm([as_pieces("w_mix_in")])
    (dh1, df1, dgm), landed = mm_nt_parts("mix_in_dx", dmix, wmix, tm, (h1, gm, dh2), 0.5,
                                          comm=join_comms(scatter_of(early), swap))
    sum_landed(early, landed[:len(early)])
    chip_sums["w_mix_in"] = sum_cores("sum_cores_w_mix_in", swap.ins[0], landed[len(early)], place)
    w_in, w_out = full["ffn1_w_in"], full["ffn1_w_out"]
    pieces["ffn1_w_out"] = mm_tn_rows("ffn1_dwout", s1, df1, tm2)
    da1 = ffn_ds_dact("ffn1_ds", df1, w_out, a1, tm2)
    pieces["ffn1_w_in"], landed = mm_tn_cols("ffn1_dwin", n1, da1, tm2, comm=scatter_of(["w_mix_in"]))
    sum_landed(["w_mix_in"], landed)
    late = ["ffn1_w_in", "ffn1_w_out"]
    sum_siblings("late", late)
    done = early + ["w_mix_in"]
    (dx, _, dg1), landed = mm_nt_stacked(
        "ffn1_dn", da1, w_in, F32, tm2, w_in.shape[2], norm=(x, g1, dh1),
        comm=join_comms(scatter_of(late), share_comm([halves[k] for k in done])))
    sum_landed(late, landed[:len(late)])
    shared = dict(zip(done, landed[len(late):]))

    shared.update(zip(late, run_comm("share_halves", share_comm([halves[k] for k in late]))))
    grad, delta, new_m, new_v = {}, {}, {}, {}
    for k in MATS:
        grad[k] = shared[k].reshape(w[k].shape)
        delta[k], new_m[k], new_v[k] = adamw("adamw_" + k, w[k], grad[k], m[k], v[k])

    small = jnp.concatenate([dg1, dgm, dg2, dgp, dgf, dcw8[:3], loss_row, jnp.zeros((7, d), F32)], axis=0)
    tot = gather_small("sum_small", small, True)
    loss = tot[8, 0]
    norm_w = jnp.concatenate([w[k].reshape(1, d) for k in NORMS] + [jnp.zeros((3, d), F32)], axis=0)
    norm_m = jnp.concatenate([m[k].reshape(1, d) for k in NORMS] + [jnp.zeros((3, d), F32)], axis=0)
    norm_v = jnp.concatenate([v[k].reshape(1, d) for k in NORMS] + [jnp.ones((3, d), F32)], axis=0)
    norm_g = jnp.concatenate([tot[0:5], jnp.zeros((3, d), F32)], axis=0)
    nd, nm, nv = adamw("adamw_norms", norm_w, norm_g, norm_m, norm_v)
    for r, k in enumerate(NORMS):
        grad[k] = norm_g[r].reshape(w[k].shape)
        delta[k], new_m[k], new_v[k] = (a[r].reshape(w[k].shape) for a in (nd, nm, nv))
    cs = d // N_CHIPS
    gcw = lax.dynamic_slice(tot[5:8], (0, chip * cs), (3, cs))
    cd, cm, cv = adamw("adamw_conv_w", _pad_rows(w["conv_w"], 8), _pad_rows(gcw, 8), _pad_rows(m["conv_w"], 8),
                       jnp.concatenate([v["conv_w"], jnp.ones((5, cs), F32)], axis=0))
    grad["conv_w"], delta["conv_w"], new_m["conv_w"], new_v["conv_w"] = gcw, cd[:3], cm[:3], cv[:3]
    return loss, dx, grad, delta, new_m, new_v


def kernel(x, p, ffn1_norm, ffn1_w_in, ffn1_w_out, mix_norm, w_mix_in, conv_w, w_conv_out, w_attn_out, w_mix_out, ffn2_norm, ffn2_w_in, ffn2_w_out, ple_norm, w_ple_gate, w_ple_proj, final_norm, loss_target, m_ffn1_norm, m_ffn1_w_in, m_ffn1_w_out, m_mix_norm, m_w_mix_in, m_conv_w, m_w_conv_out, m_w_attn_out, m_w_mix_out, m_ffn2_norm, m_ffn2_w_in, m_ffn2_w_out, m_ple_norm, m_w_ple_gate, m_w_ple_proj, m_final_norm, v_ffn1_norm, v_ffn1_w_in, v_ffn1_w_out, v_mix_norm, v_w_mix_in, v_conv_w, v_w_conv_out, v_w_attn_out, v_w_mix_out, v_ffn2_norm, v_ffn2_w_in, v_ffn2_w_out, v_ple_norm, v_w_ple_gate, v_w_ple_proj, v_final_norm):
    ws = (ffn1_norm, ffn1_w_in, ffn1_w_out, mix_norm, w_mix_in, conv_w, w_conv_out, w_attn_out, w_mix_out, ffn2_norm,
          ffn2_w_in, ffn2_w_out, ple_norm, w_ple_gate, w_ple_proj, final_norm)
    ms = (m_ffn1_norm, m_ffn1_w_in, m_ffn1_w_out, m_mix_norm, m_w_mix_in, m_conv_w, m_w_conv_out, m_w_attn_out,
          m_w_mix_out, m_ffn2_norm, m_ffn2_w_in, m_ffn2_w_out, m_ple_norm, m_w_ple_gate, m_w_ple_proj, m_final_norm)
    vs = (v_ffn1_norm, v_ffn1_w_in, v_ffn1_w_out, v_mix_norm, v_w_mix_in, v_conv_w, v_w_conv_out, v_w_attn_out,
          v_w_mix_out, v_ffn2_norm, v_ffn2_w_in, v_ffn2_w_out, v_ple_norm, v_w_ple_gate, v_w_ple_proj, v_final_norm)
    assert x.shape[0] == 1 and p.shape[:2] == (1, 1), "one sequence and one layer per device"

    def strip(a):
        return a[0] if a.ndim == 3 or (a.ndim == 2 and a.shape[0] == 1) else a

    w = {k: strip(a) for k, a in zip(WEIGHTS, ws)}
    m = {k: strip(a) for k, a in zip(WEIGHTS, ms)}
    v = {k: strip(a) for k, a in zip(WEIGHTS, vs)}
    loss, dx, grad, delta, new_m, new_v = _step(x[0], p[0, 0], loss_target[0], w, m, v)
    shapes = [a.shape for a in ws]
    outs = [loss, dx[None]]
    for res in (grad, delta, new_m, new_v):
        outs += [res[k].reshape(s) for k, s in zip(WEIGHTS, shapes)]
    return tuple(outs)
```

```python
import functools
import math

import jax
import jax.numpy as jnp
from jax import lax
from jax.experimental import pallas as pl
from jax.experimental.pallas import tpu as pltpu

F32 = jnp.float32
BF16 = jnp.bfloat16
MESH = pl.DeviceIdType.MESH
ANY = pl.BlockSpec(memory_space=pl.ANY)

HEAD_DIM = 128
NORM_EPS = 1e-6
N_CHIPS = 4
N_DEV = 8
BF16_ROWS = 16
VMEM_LIMIT = 56 * 1024 * 1024
ACC_BYTES = 8 * 1024 * 1024
STICK_EXIT = 110.0

ADAM_LR = 0.001
ADAM_B1 = 0.9
ADAM_B2 = 0.999
ADAM_EPS = 1e-08
ADAM_WD = 0.01
ADAM_STEP = 10

NN = (((1,), (0,)), ((), ()))
NT = (((1,), (1,)), ((), ()))
TN = (((0,), (0,)), ((), ()))


def _params(sem=None, **kw):
    if sem is not None:
        kw["dimension_semantics"] = sem
    return pltpu.CompilerParams(vmem_limit_bytes=VMEM_LIMIT, **kw)


def _pcall(body, **kw):
    return pl.pallas_call(body, **kw)


def _tile(n, pref, mult=8):
    best = None
    for d in range(mult, min(n, pref) + 1, mult):
        if n % d == 0:
            best = d
    return best if best is not None else n


def _dot(a, b, dims):
    return lax.dot_general(a, b, dims, preferred_element_type=F32)


def _call(name, body, grid, in_specs, out_specs, out_shape, args, scratch=(), sem=None, comm=None):
    n_in, n_out, n_sc = len(in_specs), len(out_specs), len(scratch)
    if comm is None:
        def plain(*refs):
            body(refs[:n_in], refs[n_in:n_in + n_out], refs[n_in + n_out:])

        return _pcall(plain, name=name, out_shape=list(out_shape), grid=grid, in_specs=list(in_specs),
                      out_specs=list(out_specs), scratch_shapes=list(scratch), compiler_params=_params(sem))(*args)
    n_cin, n_cout = len(comm.ins), len(comm.outs)
    steps = math.prod(grid)

    def hosted(*refs):
        ins, c_ins = refs[:n_in], refs[n_in:n_in + n_cin]
        outs = refs[n_in + n_cin:n_in + n_cin + n_out]
        c_outs = refs[n_in + n_cin + n_out:n_in + n_cin + n_out + n_cout]
        rest = refs[n_in + n_cin + n_out + n_cout:]
        sems = rest[n_sc:]
        step = pl.program_id(0)
        for ax in range(1, len(grid)):
            step = step * grid[ax] + pl.program_id(ax)

        @pl.when(step == 0)
        def _():
            comm.first(c_ins, c_outs, sems)

        body(ins, outs, rest[:n_sc])

        @pl.when(step == (3 * steps) // 4)
        def _():
            comm.mid(c_ins, c_outs, sems)

        @pl.when(step == steps - 1)
        def _():
            comm.last(c_ins, c_outs, sems)

    res = _pcall(hosted, name=name, out_shape=list(out_shape) + comm.outs, grid=grid,
                 in_specs=list(in_specs) + [ANY] * n_cin, out_specs=list(out_specs) + [ANY] * n_cout,
                 input_output_aliases={n_in + k: n_out + v for k, v in comm.aliases.items()},
                 scratch_shapes=list(scratch) + comm.sems,
                 compiler_params=_params(("arbitrary",) * len(grid)))(*args, *comm.ins)
    return list(res[:n_out]), list(res[n_out:])


NORM_CHUNK = 256


def _norm_bwd_tile(read_dn, rows, first, h_ref, g_ref, dr_ref, dh_ref, dhb_ref, dg_ref, alpha):
    @pl.when(first)
    def _():
        dg_ref[...] = jnp.zeros_like(dg_ref)

    gv = g_ref[...]
    tot = jnp.zeros_like(gv)
    for c0 in range(0, rows, NORM_CHUNK):
        sl = slice(c0, min(rows, c0 + NORM_CHUNK))
        hv = h_ref[sl, :]
        rs = _rstd(hv)
        hn = hv * rs
        dnv = read_dn(sl)
        gy = dnv * gv
        dh = dr_ref[sl, :] + rs * (gy - hn * jnp.mean(gy * hn, axis=-1, keepdims=True))
        dh_ref[sl, :] = dh
        dhb_ref[sl, :] = (alpha * dh).astype(BF16)
        tot = tot + jnp.sum(dnv * hn, axis=0, keepdims=True)
    dg_ref[...] += tot


def _mm(name, a, b, out_sds, grid, a_spec, b_spec, o_spec, dims, acc_shape, res=None, alpha=1.0, comm=None,
        norm=None):
    nk = grid[2]

    def body(ins, outs, scratch):
        a_ref, b_ref = ins[:2]
        r_ref = ins[2] if res is not None else None
        o_ref = outs[0]

        def finish(read):
            if norm is not None:
                first = jnp.logical_and(pl.program_id(0) == 0, pl.program_id(1) == 0)
                _norm_bwd_tile(read, o_ref.shape[0], first, *ins[2:5], *outs, alpha)
                return
            r = read(slice(None))
            if alpha != 1.0:
                r = r * alpha
            if r_ref is not None:
                r = r_ref[...] + r
            if len(o_ref.shape) == 3:
                half = o_ref.shape[1]
                o_ref[0] = r[:half].astype(o_ref.dtype)
                o_ref[1] = r[half:].astype(o_ref.dtype)
            else:
                o_ref[...] = r.astype(o_ref.dtype)

        if nk == 1:
            part = _dot(a_ref[...].astype(BF16), b_ref[...].astype(BF16), dims)
            finish(lambda sl: part[sl])
        else:
            acc_ref = scratch[0]
            kk = pl.program_id(2)

            @pl.when(kk == 0)
            def _():
                acc_ref[...] = jnp.zeros_like(acc_ref)

            acc_ref[...] += _dot(a_ref[...].astype(BF16), b_ref[...].astype(BF16), dims)

            @pl.when(kk == nk - 1)
            def _():
                finish(lambda sl: acc_ref[sl, :])

    in_specs = [a_spec, b_spec]
    args = [a, b]
    out_specs, out_shape = [o_spec], [out_sds]
    sem = ("parallel", "parallel", "arbitrary")
    if res is not None:
        in_specs.append(o_spec)
        args.append(res)
    if norm is not None:
        width = out_sds.shape[1]
        whole = pl.BlockSpec((1, width), lambda i, j, r: (0, 0))
        in_specs += [o_spec, whole, o_spec]
        args += list(norm)
        out_specs = [o_spec, o_spec, whole]
        out_shape = [jax.ShapeDtypeStruct(out_sds.shape, F32), jax.ShapeDtypeStruct(out_sds.shape, BF16),
                     jax.ShapeDtypeStruct((1, width), F32)]
        sem = ("arbitrary", "arbitrary", "arbitrary")
    scratch = [] if nk == 1 else [pltpu.VMEM(acc_shape, F32)]
    got = _call(name, body, grid, in_specs, out_specs, out_shape, args, scratch, sem, comm)
    if norm is not None:
        return got if comm is None else (got[0], got[1])
    return got[0] if comm is None else (got[0][0], got[1])


def ffn_in_act(name, n, w4, tm, comm=None, gain=None):
    t, d = n.shape
    cs = w4.shape[2]

    def body(ins, outs, scratch):
        wg_ref, wu_ref = ins[-2:]
        a_ref, s_ref = outs[:2]
        if gain is None:
            nv = ins[0][...]
        else:
            @pl.when(pl.program_id(1) == 0)
            def _():
                hv = ins[0][...]
                scratch[0][...] = (hv * _rstd(hv) * ins[1][...]).astype(BF16)
                outs[2][...] = scratch[0][...]

            nv = scratch[0][...]
        gate = _dot(nv, wg_ref[...], NN)
        up = _dot(nv, wu_ref[...], NN)
        a_ref[0] = gate.astype(BF16)
        a_ref[1] = up.astype(BF16)
        s_ref[...] = (gate * jax.nn.sigmoid(gate) * up).astype(BF16)

    rows = pl.BlockSpec((tm, d), lambda i, j: (i, 0))
    in_specs = [rows] + ([] if gain is None else [pl.BlockSpec((1, d), lambda i, j: (0, 0))])
    in_specs += [pl.BlockSpec((None, d, cs), lambda i, j: (j, 0, 0)),
                 pl.BlockSpec((None, d, cs), lambda i, j: (2 + j, 0, 0))]
    out_specs = [pl.BlockSpec((2, tm, cs), lambda i, j: (0, i, j)), pl.BlockSpec((tm, cs), lambda i, j: (i, j))]
    out_shape = [jax.ShapeDtypeStruct((2, t, 2 * cs), BF16), jax.ShapeDtypeStruct((t, 2 * cs), BF16)]
    if gain is not None:
        out_specs.append(rows)
        out_shape.append(jax.ShapeDtypeStruct((t, d), BF16))
    got = _call(name, body, (t // tm, 2), in_specs, out_specs, out_shape,
                [n] + ([] if gain is None else [gain]) + [w4, w4],
                [] if gain is None else [pltpu.VMEM((tm, d), BF16)], ("parallel", "arbitrary"), comm)
    return got if comm is None else (got[0], got[1])


def ffn_ds_dact(name, df, w_out, a3, tm):
    t, d = df.shape
    f = w_out.shape[0]
    cs = f // 2

    def body(ins, outs, scratch):
        df_ref, w_ref, a_ref = ins
        ds = _dot(df_ref[...], w_ref[...], NT)
        for c0 in range(0, tm, NORM_CHUNK):
            sl = slice(c0, min(tm, c0 + NORM_CHUNK))
            gate = a_ref[0, sl, :].astype(F32)
            up = a_ref[1, sl, :].astype(F32)
            sg = jax.nn.sigmoid(gate)
            outs[0][0, sl, :] = (ds[sl] * up * sg * (1.0 + gate * (1.0 - sg))).astype(BF16)
            outs[0][1, sl, :] = (ds[sl] * gate * sg).astype(BF16)

    blk = pl.BlockSpec((2, tm, cs), lambda i, j: (0, i, j))
    return _call(name, body, (t // tm, 2),
                 [pl.BlockSpec((tm, d), lambda i, j: (i, 0)), pl.BlockSpec((cs, d), lambda i, j: (j, 0)), blk],
                 [blk], [jax.ShapeDtypeStruct((2, t, f), BF16)], [df, w_out, a3], (), ("parallel", "parallel"))[0]


def _part_ranges(parts, d):
    out, lo = [], 0
    for p in parts:
        out.append((lo, p.shape[1] // d))
        lo += p.shape[1] // d
    return out, lo


def mm_nt_parts(name, parts, w4, tm, norm, alpha, comm=None):
    m = parts[0].shape[0]
    d, cs = w4.shape[1], w4.shape[2]
    per = cs // d
    ranges, nblk = _part_ranges(parts, d)
    np_ = len(parts)

    def body(ins, outs, scratch):
        w_ref, acc = ins[np_], scratch[0]
        r = pl.program_id(1)

        @pl.when(r == 0)
        def _():
            acc[...] = jnp.zeros_like(acc)

        for (lo, n), a_ref in zip(ranges, ins[:np_]):
            @pl.when(jnp.logical_and(r >= lo, r < lo + n))
            def _(a_ref=a_ref):
                acc[...] += _dot(a_ref[...], w_ref[...], NT)

        @pl.when(r == nblk - 1)
        def _():
            _norm_bwd_tile(lambda sl: acc[sl, :], tm, pl.program_id(0) == 0, *ins[np_ + 1:], *outs, alpha)

    rows = pl.BlockSpec((tm, d), lambda i, r: (i, 0))
    whole = pl.BlockSpec((1, d), lambda i, r: (0, 0))
    specs = [pl.BlockSpec((tm, d), lambda i, r, lo=lo, n=n: (i, jnp.clip(r - lo, 0, n - 1))) for lo, n in ranges]
    specs += [pl.BlockSpec((None, d, d), lambda i, r: (r // per, 0, r % per)), rows, whole, rows]
    got = _call(name, body, (m // tm, nblk), specs, [rows, rows, whole],
                [jax.ShapeDtypeStruct((m, d), F32), jax.ShapeDtypeStruct((m, d), BF16),
                 jax.ShapeDtypeStruct((1, d), F32)],
                list(parts) + [w4] + list(norm), [pltpu.VMEM((tm, d), F32)], ("arbitrary", "arbitrary"), comm)
    return got if comm is None else (got[0], got[1])


def mm_tn_parts(name, xa, parts, tt, comm=None):
    t, k = xa.shape
    d = k
    pr = k // 2
    ranges, nblk = _part_ranges(parts, d)
    per = nblk // N_CHIPS

    def body(ins, outs, scratch):
        x_ref, acc = ins[0], scratch[0]
        jb, r = pl.program_id(0), pl.program_id(1)

        @pl.when(r == 0)
        def _():
            acc[...] = jnp.zeros_like(acc)

        for (lo, n), p_ref in zip(ranges, ins[1:]):
            @pl.when(jnp.logical_and(jb >= lo, jb < lo + n))
            def _(p_ref=p_ref):
                acc[...] += _dot(x_ref[...], p_ref[...], TN)

        @pl.when(r == t // tt - 1)
        def _():
            outs[0][0] = acc[:pr].astype(BF16)
            outs[0][1] = acc[pr:].astype(BF16)

    def part_spec(lo, n):
        return pl.BlockSpec((tt, d), lambda jb, r: (jnp.where(jnp.logical_and(jb >= lo, jb < lo + n), r, 0),
                                                    jnp.clip(jb - lo, 0, n - 1)))

    specs = [pl.BlockSpec((tt, k), lambda jb, r: (r, 0))] + [part_spec(lo, n) for lo, n in ranges]
    got = _call(name, body, (nblk, t // tt), specs,
                [pl.BlockSpec((None, 2, pr, d), lambda jb, r: (jb // per, 0, 0, jb % per))],
                [jax.ShapeDtypeStruct((N_CHIPS, 2, pr, per * d), BF16)], [xa] + list(parts),
                [pltpu.VMEM((k, d), F32)], ("parallel", "arbitrary"), comm)
    return got[0] if comm is None else (got[0][0], got[1])


def mm_nn(name, a, w, out_dtype, tm, res=None, alpha=1.0):
    m, k = a.shape
    n = w.shape[1]
    return _mm(name, a, w, jax.ShapeDtypeStruct((m, n), out_dtype), (m // tm, 1, 1),
               pl.BlockSpec((tm, k), lambda i, j, r: (i, 0)),
               pl.BlockSpec((k, n), lambda i, j, r: (0, 0)),
               pl.BlockSpec((tm, n), lambda i, j, r: (i, 0)), NN, None, res=res, alpha=alpha)


def mm_nn_stacked(name, a, w4, out_dtype, tm, tn, j0=0, nj=None, comm=None):
    m, k = a.shape
    cs = w4.shape[2]
    per = cs // tn
    nj = N_CHIPS * per - j0 if nj is None else nj
    return _mm(name, a, w4, jax.ShapeDtypeStruct((m, nj * tn), out_dtype), (m // tm, nj, 1),
               pl.BlockSpec((tm, k), lambda i, j, r: (i, 0)),
               pl.BlockSpec((None, k, tn), lambda i, j, r: ((j + j0) // per, 0, (j + j0) % per)),
               pl.BlockSpec((tm, tn), lambda i, j, r: (i, j)), NN, None, comm=comm)


def mm_nt(name, dy, w, out_dtype, tm, tko, norm=None, alpha=1.0):
    m, n = dy.shape
    k = w.shape[0]
    return _mm(name, dy, w, jax.ShapeDtypeStruct((m, k), out_dtype), (m // tm, k // tko, 1),
               pl.BlockSpec((tm, n), lambda i, j, r: (i, 0)),
               pl.BlockSpec((tko, n), lambda i, j, r: (j, 0)),
               pl.BlockSpec((tm, tko), lambda i, j, r: (i, j)), NT, None, norm=norm, alpha=alpha)


def mm_nt_stacked(name, dy, w4, out_dtype, tm, tn, comm=None, norm=None, alpha=1.0):
    m = dy.shape[-2]
    k, cs = w4.shape[1], w4.shape[2]
    per = cs // tn
    if dy.ndim == 3:
        dy_spec = pl.BlockSpec((None, tm, cs), lambda i, j, r: (r // 2, i, r % 2))
    else:
        dy_spec = pl.BlockSpec((tm, tn), lambda i, j, r: (i, r))
    return _mm(name, dy, w4, jax.ShapeDtypeStruct((m, k), out_dtype), (m // tm, 1, N_CHIPS * per), dy_spec,
               pl.BlockSpec((None, k, tn), lambda i, j, r: (r // per, 0, r % per)),
               pl.BlockSpec((tm, k), lambda i, j, r: (i, 0)), NT, (tm, k), comm=comm, norm=norm, alpha=alpha)


def mm_tn_rows(name, xa, dy, tt):
    t, k = xa.shape
    n = dy.shape[1]
    tkr = k if k * n * 4 <= ACC_BYTES else k // 2
    return _mm(name, xa, dy, jax.ShapeDtypeStruct((k, n), BF16), (k // tkr, 1, t // tt),
               pl.BlockSpec((tt, tkr), lambda i, j, r: (r, i)),
               pl.BlockSpec((tt, n), lambda i, j, r: (r, 0)),
               pl.BlockSpec((tkr, n), lambda i, j, r: (i, 0)), TN, (tkr, n))


def mm_tn_whole(name, xa, dy, tt):
    t, k = xa.shape
    n = dy.shape[1]
    return _mm(name, xa, dy, jax.ShapeDtypeStruct((k, n), BF16), (1, 1, t // tt),
               pl.BlockSpec((tt, k), lambda i, j, r: (r, 0)),
               pl.BlockSpec((tt, n), lambda i, j, r: (r, 0)),
               pl.BlockSpec((k, n), lambda i, j, r: (0, 0)), TN, (k, n))


def mm_tn_cols(name, xa, dy, tt, comm=None):
    t, k = xa.shape
    pr = k // 2
    if dy.ndim == 3:
        cs = dy.shape[2] // 2
        dy_spec = pl.BlockSpec((None, tt, cs), lambda i, j, r: (j // 2, r, j % 2))
    else:
        cs = dy.shape[1] // N_CHIPS
        dy_spec = pl.BlockSpec((tt, cs), lambda i, j, r: (r, j))
    return _mm(name, xa, dy, jax.ShapeDtypeStruct((N_CHIPS, 2, pr, cs), BF16), (1, N_CHIPS, t // tt),
               pl.BlockSpec((tt, k), lambda i, j, r: (r, 0)), dy_spec,
               pl.BlockSpec((None, 2, pr, cs), lambda i, j, r: (j, 0, 0, 0)), TN, (k, cs), comm=comm)


def _rows(tt, w, col=0):
    return pl.BlockSpec((tt, w), lambda i: (i, col))


def _whole(shape):
    return pl.BlockSpec(shape, lambda i: (0,) * len(shape))


def _rstd(h):
    return lax.rsqrt(jnp.mean(h * h, axis=-1, keepdims=True) + NORM_EPS)


def rms_fwd(name, h, g, tt, comm=None):
    t, d = h.shape

    def body(ins, outs, scratch):
        hv = ins[0][...]
        outs[0][...] = (hv * _rstd(hv) * ins[1][...]).astype(BF16)

    got = _call(name, body, (t // tt,), [_rows(tt, d), _whole((1, d))], [_rows(tt, d)],
                [jax.ShapeDtypeStruct((t, d), BF16)], [h, g], (), ("parallel",), comm)
    return got[0] if comm is None else (got[0][0], got[1])


def mix_out_fwd(name, gates, yc, ya, h, w, tt):
    t, d = yc.shape

    def body(g_ref, yc_ref, ya_ref, h_ref, w_ref, m_ref, o_ref):
        merged = (jax.nn.sigmoid(g_ref[:, :d].astype(F32)) * yc_ref[...].astype(F32)
                  + jax.nn.sigmoid(g_ref[:, d:].astype(F32)) * ya_ref[...].astype(F32)).astype(BF16)
        m_ref[...] = merged
        o_ref[...] = h_ref[...] + _dot(merged, w_ref[...], NN)

    return _pcall(body, name=name,
                  out_shape=(jax.ShapeDtypeStruct((t, d), BF16), jax.ShapeDtypeStruct((t, d), F32)),
                  grid=(t // tt,),
                  in_specs=[_rows(tt, 2 * d), _rows(tt, d), _rows(tt, d), _rows(tt, d), _whole((d, d))],
                  out_specs=(_rows(tt, d), _rows(tt, d)),
                  compiler_params=_params(("parallel",)))(gates, yc, ya, h, w)


def mix_out_bwd(name, dh, w, gates, yc, ya, tt):
    t, d = yc.shape

    def body(dh_ref, w_ref, g_ref, yc_ref, ya_ref, dyc_ref, dya_ref, dg_ref):
        dmv = _dot(dh_ref[...], w_ref[...], NT)
        sc = jax.nn.sigmoid(g_ref[:, :d].astype(F32))
        sa = jax.nn.sigmoid(g_ref[:, d:].astype(F32))
        dyc_ref[...] = (dmv * sc).astype(BF16)
        dya_ref[...] = (dmv * sa).astype(BF16)
        dg_ref[:, :d] = (dmv * yc_ref[...].astype(F32) * sc * (1.0 - sc)).astype(BF16)
        dg_ref[:, d:] = (dmv * ya_ref[...].astype(F32) * sa * (1.0 - sa)).astype(BF16)

    return _pcall(body, name=name,
                  out_shape=(jax.ShapeDtypeStruct((t, d), BF16), jax.ShapeDtypeStruct((t, d), BF16),
                             jax.ShapeDtypeStruct((t, 2 * d), BF16)),
                  grid=(t // tt,),
                  in_specs=[_rows(tt, d), _whole((d, d)), _rows(tt, 2 * d), _rows(tt, d), _rows(tt, d)],
                  out_specs=(_rows(tt, d), _rows(tt, d), _rows(tt, 2 * d)),
                  compiler_params=_params(("parallel",)))(dh, w, gates, yc, ya)


def _shift_down(cur, prev8, s):
    tt = cur.shape[0]
    rolled = pltpu.roll(cur, s, 0)
    row8 = lax.broadcasted_iota(jnp.int32, prev8.shape, 0)
    first8 = jnp.where(row8 < s, pltpu.roll(prev8, s, 0), rolled[:8])
    return jnp.concatenate([first8, rolled[8:]], axis=0) if tt > 8 else first8


def _shift_up(cur, next8, s):
    tt = cur.shape[0]
    rolled = pltpu.roll(cur, tt - s, 0)
    row8 = lax.broadcasted_iota(jnp.int32, next8.shape, 0)
    last8 = jnp.where(row8 >= 8 - s, pltpu.roll(next8, 8 - s, 0), rolled[tt - 8:])
    return jnp.concatenate([rolled[:tt - 8], last8], axis=0) if tt > 8 else last8


def _prev8(tt, d, col):
    return pl.BlockSpec((8, d), lambda i: (jnp.maximum(i * (tt // 8) - 1, 0), col))


def _next8(tt, d, col, t):
    return pl.BlockSpec((8, d), lambda i: (jnp.minimum((i + 1) * (tt // 8), t // 8 - 1), col))


def conv_out_fwd(name, cbx, cw8, w_out, tt):
    t, d3 = cbx.shape
    d = d3 // 3

    def body(cb_ref, cc_ref, cx_ref, pc_ref, px_ref, w_ref, wo_ref, o_ref, y_ref):
        has_prev = (pl.program_id(0) > 0).astype(F32)
        cc = cc_ref[...] * cx_ref[...]
        prev = pc_ref[...] * px_ref[...] * has_prev
        w = w_ref[...]
        conv = w[0:1] * _shift_down(cc, prev, 2) + w[1:2] * _shift_down(cc, prev, 1) + w[2:3] * cc
        ycin = (cb_ref[...] * conv).astype(BF16)
        o_ref[...] = ycin
        y_ref[...] = _dot(ycin, wo_ref[...], NN).astype(BF16)

    out = jax.ShapeDtypeStruct((t, d), BF16)
    return _pcall(body, name=name, out_shape=(out, out), grid=(t // tt,),
                  in_specs=[_rows(tt, d, 0), _rows(tt, d, 1), _rows(tt, d, 2), _prev8(tt, d, 1), _prev8(tt, d, 2),
                            _whole((8, d)), _whole((d, d))],
                  out_specs=(_rows(tt, d), _rows(tt, d)),
                  compiler_params=_params(("parallel",)))(cbx, cbx, cbx, cbx, cbx, cw8, w_out)


def conv_out_bwd(name, dyc, w_out, cbx, cw8, tt):
    t, d3 = cbx.shape
    d = d3 // 3
    n = t // tt

    def body(dy_ref, ndy_ref, wo_ref, cb_ref, cc_ref, cx_ref, pc_ref, px_ref, ncb_ref, w_ref, o_ref, dw_ref):
        i = pl.program_id(0)
        has_prev = (i > 0).astype(F32)
        has_next = (i < n - 1).astype(F32)
        cb = cb_ref[...]
        cc = cc_ref[...] * cx_ref[...]
        prev = pc_ref[...] * px_ref[...] * has_prev
        w = w_ref[...]
        cc1 = _shift_down(cc, prev, 1)
        cc2 = _shift_down(cc, prev, 2)
        conv = w[0:1] * cc2 + w[1:2] * cc1 + w[2:3] * cc
        dyv = _dot(dy_ref[...], wo_ref[...], NT)
        dconv = dyv * cb
        dnext = _dot(ndy_ref[...], wo_ref[...], NT)[:8] * ncb_ref[...] * has_next
        dcc = w[2:3] * dconv + w[1:2] * _shift_up(dconv, dnext, 1) + w[0:1] * _shift_up(dconv, dnext, 2)
        o_ref[:, :d] = (dyv * conv).astype(BF16)
        o_ref[:, d:2 * d] = (dcc * cx_ref[...]).astype(BF16)
        o_ref[:, 2 * d:] = (dcc * cc_ref[...]).astype(BF16)

        @pl.when(i == 0)
        def _():
            dw_ref[...] = jnp.zeros_like(dw_ref)

        dw_ref[0:1, :] += jnp.sum(dconv * cc2, axis=0, keepdims=True)
        dw_ref[1:2, :] += jnp.sum(dconv * cc1, axis=0, keepdims=True)
        dw_ref[2:3, :] += jnp.sum(dconv * cc, axis=0, keepdims=True)

    return _pcall(body, name=name,
                  out_shape=(jax.ShapeDtypeStruct((t, d3), BF16), jax.ShapeDtypeStruct((8, d), F32)),
                  grid=(n,),
                  in_specs=[_rows(tt, d),
                            pl.BlockSpec((BF16_ROWS, d), lambda i: (jnp.minimum((i + 1) * (tt // BF16_ROWS),
                                                                                t // BF16_ROWS - 1), 0)),
                            _whole((d, d)), _rows(tt, d, 0), _rows(tt, d, 1), _rows(tt, d, 2),
                            _prev8(tt, d, 1), _prev8(tt, d, 2), _next8(tt, d, 0, t), _whole((8, d))],
                  out_specs=(_rows(tt, d3), _whole((8, d))),
                  compiler_params=_params(("arbitrary",)))(dyc, dyc, w_out, cbx, cbx, cbx, cbx, cbx, cbx, cw8)


def tail(name, h3, p, tgt, gp, gf, w_gate, w_proj, tt):
    t, d = h3.shape
    pd = p.shape[1]

    def body(h_ref, p_ref, tg_ref, gp_ref, gf_ref, wg_ref, wp_ref, np_ref, dh_ref, dpp_ref, dzg_ref, dgf_ref,
             loss_ref):
        hv = h_ref[...]
        npl = (hv * _rstd(hv) * gp_ref[...]).astype(BF16)
        np_ref[...] = npl
        pg = jax.nn.sigmoid(_dot(npl, wg_ref[...], NN))
        ppv = _dot(p_ref[...].astype(BF16), wp_ref[...], NN)
        h4 = hv + pg * ppv
        r4 = _rstd(h4)
        hn = h4 * r4
        gfv = gf_ref[...]
        err = hn * gfv - tg_ref[...]
        dy = err * (1.0 / d)
        gy = dy * gfv
        dh4 = r4 * (gy - hn * jnp.mean(gy * hn, axis=-1, keepdims=True))
        dh_ref[...] = dh4
        dpp_ref[...] = (dh4 * pg).astype(BF16)
        dzg_ref[...] = (dh4 * ppv * pg * (1.0 - pg)).astype(BF16)

        @pl.when(pl.program_id(0) == 0)
        def _():
            dgf_ref[...] = jnp.zeros_like(dgf_ref)
            loss_ref[...] = jnp.zeros_like(loss_ref)

        dgf_ref[...] += jnp.sum(dy * hn, axis=0, keepdims=True)
        tok = jnp.mean(err * err, axis=-1, keepdims=True)
        loss_ref[...] += 0.5 * jnp.sum(tok, axis=0, keepdims=True) * jnp.ones((1, loss_ref.shape[1]), F32)

    return _pcall(body, name=name,
                  out_shape=(jax.ShapeDtypeStruct((t, d), BF16), jax.ShapeDtypeStruct((t, d), F32),
                             jax.ShapeDtypeStruct((t, d), BF16), jax.ShapeDtypeStruct((t, d), BF16),
                             jax.ShapeDtypeStruct((1, d), F32), jax.ShapeDtypeStruct((1, d), F32)),
                  grid=(t // tt,),
                  in_specs=[_rows(tt, d), _rows(tt, pd), _rows(tt, d), _whole((1, d)), _whole((1, d)),
                            _whole((d, d)), _whole((pd, d))],
                  out_specs=(_rows(tt, d), _rows(tt, d), _rows(tt, d), _rows(tt, d), _whole((1, d)),
                             _whole((1, d))),
                  compiler_params=_params(("arbitrary",)))(h3, p, tgt, gp, gf, w_gate, w_proj)


SCALE = 1.0 / math.sqrt(HEAD_DIM)


def _log_stick(z):
    return -(jnp.maximum(z, 0.0) + jnp.log(1.0 + jnp.exp(-jnp.abs(z))))


def _tri_sum(x, tri):
    hi = x.astype(BF16)
    lo = (x - hi.astype(F32)).astype(BF16)
    return _dot(hi, tri, NN) + _dot(lo, tri, NN)


KEY_BLOCK = 128
NEAR = 3
THIN_ROWS = 32


def _pad_block(x):
    n = x.shape[0]
    return x if n == KEY_BLOCK else jnp.concatenate([x, jnp.zeros((KEY_BLOCK - n, x.shape[1]), x.dtype)], axis=0)


def _sb_near(qs, jds, k_ref, below, upper, last_rows):
    near_rows = (KEY_BLOCK,) * (NEAR - 1) + (last_rows,)
    pairs = [(s, b) for s in range(len(qs)) for b in range(NEAR)]
    rows = {(s, b): _block_rows(jnp.maximum(jds[s] - b, 0), KEY_BLOCK) for s, b in pairs}
    z = {(s, b): _dot(qs[s][:near_rows[b]], k_ref[rows[s, b], :], NT) * SCALE for s, b in pairs}
    lg = {(s, b): jnp.where(below, _log_stick(z[s, b]), 0.0) if b == 0 else _log_stick(z[s, b]) for s, b in pairs}
    cum = {(s, b): _tri_sum(lg[s, b], upper) for s, b in pairs}
    out, carries = [], []
    for s in range(len(qs)):
        c = cum[s, 0][:, 0:1]
        blocks = [(rows[s, 0], z[s, 0], jnp.exp(jnp.where(below, z[s, 0] + cum[s, 0], -1e30)))]
        for b in range(1, NEAR):
            live = jds[s] >= b
            off = c[:near_rows[b]] + jnp.where(live, 0.0, -1e30)
            blocks.append((rows[s, b], z[s, b], jnp.exp(z[s, b] + cum[s, b] + off)))
            c = c + _pad_block(jnp.where(live, cum[s, b][:, 0:1], 0.0))
        out.append(blocks)
        carries.append(c)
    return out, carries


def _sb_far(q, kj, upper, c, skip):
    z = _dot(q, kj, NT) * SCALE
    cum = _tri_sum(_log_stick(z), upper)
    return z, jnp.exp(z + cum + (c + jnp.where(skip, -1e30, 0.0))), c + jnp.where(skip, 0.0, cum[:, 0:1])


def _took_it(j, jd, last_rows):
    first = lax.broadcasted_iota(jnp.int32, (KEY_BLOCK, 1), 0) < last_rows
    return jnp.logical_and(j == jd - (NEAR - 1), first)


def _block_rows(j, size):
    return pl.ds(pl.multiple_of(j * size, size), size)


def _sweep_on(st):
    return jnp.logical_and(st[0] >= 0, jnp.max(st[1]) > -STICK_EXIT)


def attn_fwd(name, qkv, tq):
    t, d3 = qkv.shape
    d = d3 // 3
    nh = d // HEAD_DIM
    nq = t // tq
    tb = KEY_BLOCK
    nsub = tq // tb

    def body(q_ref, k_ref, v_ref, o_ref):
        i = pl.program_id(1)
        row = lax.broadcasted_iota(jnp.int32, (tb, tb), 0)
        col = lax.broadcasted_iota(jnp.int32, (tb, tb), 1)
        upper = (row >= col).astype(BF16)
        qs = [q_ref[s * tb:(s + 1) * tb, :] for s in range(nsub)]
        jds = [i * nsub + s for s in range(nsub)]
        near, carries = _sb_near(qs, jds, k_ref, col < row, upper, THIN_ROWS)
        state = []
        for s in range(nsub):
            acc = jnp.zeros((tb, HEAD_DIM), F32)
            for rows, _, a in near[s]:
                acc = acc + _pad_block(_dot(a.astype(BF16), v_ref[rows, :], NN))
            state.append((qs[s], jds[s], carries[s], acc))
        for s, (q, jd, c, acc) in enumerate(state):

            def step(st, q=q, jd=jd):
                rows = _block_rows(st[0], tb)
                _, a, c2 = _sb_far(q, k_ref[rows, :], upper, st[1], _took_it(st[0], jd, THIN_ROWS))
                return st[0] - 1, c2, st[2] + _dot(a.astype(BF16), v_ref[rows, :], NN)

            _, _, acc = lax.while_loop(_sweep_on, step, (jd - (NEAR - 1), c, acc))
            o_ref[s * tb:(s + 1) * tb, :] = acc.astype(o_ref.dtype)

    return _pcall(body, name=name, out_shape=jax.ShapeDtypeStruct((t, d), BF16), grid=(nh, nq),
                  in_specs=[pl.BlockSpec((tq, HEAD_DIM), lambda h, i: (i, h)),
                            pl.BlockSpec((t, HEAD_DIM), lambda h, i: (0, nh + h)),
                            pl.BlockSpec((t, HEAD_DIM), lambda h, i: (0, 2 * nh + h))],
                  out_specs=pl.BlockSpec((tq, HEAD_DIM), lambda h, i: (i, h)),
                  compiler_params=_params(("parallel", "arbitrary")))(qkv, qkv, qkv)


def attn_bwd(name, qkv, do, tq):
    t, d3 = qkv.shape
    d = d3 // 3
    nh = d // HEAD_DIM
    nq = t // tq
    tb = KEY_BLOCK
    nsub = tq // tb

    def body(q_ref, k_ref, v_ref, do_ref, dq_ref, dk_ref, dv_ref, dk_acc, dv_acc, g_buf, z_buf):
        i = pl.program_id(1)

        @pl.when(i == 0)
        def _():
            dk_acc[...] = jnp.zeros_like(dk_acc)
            dv_acc[...] = jnp.zeros_like(dv_acc)

        row = lax.broadcasted_iota(jnp.int32, (tb, tb), 0)
        col = lax.broadcasted_iota(jnp.int32, (tb, tb), 1)
        below = col < row
        upper = (row >= col).astype(BF16)
        lower = (row <= col).astype(BF16)

        qs = [q_ref[s * tb:(s + 1) * tb, :] for s in range(nsub)]
        dos = [do_ref[s * tb:(s + 1) * tb, :] for s in range(nsub)]
        jds = [i * nsub + s for s in range(nsub)]
        near, carries = _sb_near(qs, jds, k_ref, below, upper, KEY_BLOCK)
        da = [[_dot(dos[s][:a.shape[0]], v_ref[rows, :], NT) for rows, _, a in near[s]] for s in range(nsub)]
        state = []
        for s in range(nsub):
            kept = [(rows, z, da[s][b] * a) for b, (rows, z, a) in enumerate(near[s])]
            for rows, _, a in near[s]:
                dv_acc[rows, :] += _dot(a.astype(BF16), dos[s][:a.shape[0]], TN)
            state.append((qs[s], dos[s], jds[s], carries[s], kept))

        carried = []
        for s, (q, dov, jd, c, kept) in enumerate(state):
            def step(st, s=s, q=q, dov=dov, jd=jd):
                j = st[0]
                rows = _block_rows(j, tb)
                z, a, c2 = _sb_far(q, k_ref[rows, :], upper, st[1], _took_it(j, jd, KEY_BLOCK))
                g_buf[jd - j] = _dot(dov, v_ref[rows, :], NT) * a
                z_buf[jd - j] = z
                dv_acc[rows, :] += _dot(a.astype(BF16), dov, TN)
                return j - 1, c2

            j_stop, _ = lax.while_loop(_sweep_on, step, (jd - (NEAR - 1), c))

            def far(j, st, s=s, q=q, jd=jd):
                run, dq = st
                rows = _block_rows(j, tb)
                g = g_buf[jd - j]
                dz = (g - jax.nn.sigmoid(z_buf[jd - j]) * (run + _tri_sum(g, lower))).astype(BF16)
                dk_acc[rows, :] += _dot(dz, q, TN)
                return run + jnp.sum(g, axis=1, keepdims=True), dq + _dot(dz, k_ref[rows, :], NN)

            carried.append(lax.fori_loop(j_stop + 1, jd - (NEAR - 1) + 1, far,
                                         (jnp.zeros((tb, 1), F32), jnp.zeros((tb, HEAD_DIM), F32))))

        tri = [[_dot(g.astype(BF16), lower, NN) for _, _, g in st[4]] for st in state]
        sig = [[jax.nn.sigmoid(z) for _, z, _ in st[4]] for st in state]
        for s, (q, dov, jd, c, kept) in enumerate(state):
            run, dq = carried[s]
            for b in reversed(range(NEAR)):
                rows, z, g = kept[b]
                n = g.shape[0]
                dz = g - sig[s][b] * (run[:n] + tri[s][b])
                if b == 0:
                    dz = jnp.where(below, dz, 0.0)
                dz = dz.astype(BF16)
                dk_acc[rows, :] += _dot(dz, q[:n], TN)
                dq = dq + _pad_block(_dot(dz, k_ref[rows, :], NN))
                if b:
                    run = run + _pad_block(jnp.sum(g, axis=1, keepdims=True))
            dq_ref[s * tb:(s + 1) * tb, :] = (dq * SCALE).astype(BF16)

        @pl.when(i == nq - 1)
        def _():
            dk_ref[...] = (dk_acc[...] * SCALE).astype(BF16)
            dv_ref[...] = dv_acc[...].astype(BF16)

    blk = pl.BlockSpec((tq, HEAD_DIM), lambda h, i: (i, h))
    col_h = pl.BlockSpec((t, HEAD_DIM), lambda h, i: (0, h))
    out = jax.ShapeDtypeStruct((t, d), BF16)
    return _pcall(body, name=name, out_shape=(out, out, out), grid=(nh, nq),
                  in_specs=[blk,
                            pl.BlockSpec((t, HEAD_DIM), lambda h, i: (0, nh + h)),
                            pl.BlockSpec((t, HEAD_DIM), lambda h, i: (0, 2 * nh + h)),
                            blk],
                  out_specs=(blk, col_h, col_h),
                  scratch_shapes=[pltpu.VMEM((t, HEAD_DIM), F32), pltpu.VMEM((t, HEAD_DIM), F32),
                                  pltpu.VMEM((t // tb, tb, tb), F32), pltpu.VMEM((t // tb, tb, tb), F32)],
                  compiler_params=_params(("parallel", "arbitrary")))(qkv, qkv, qkv, do)


def _place():
    x, y, c = lax.axis_index("x"), lax.axis_index("y"), lax.axis_index("c")
    chips = [(1 - x, y), (x, 1 - y), (1 - x, 1 - y)]
    return x, y, c, chips


def _remote(src, dst, send_sem, recv_sem, dev):
    return pltpu.make_async_remote_copy(src_ref=src, dst_ref=dst, send_sem=send_sem, recv_sem=recv_sem,
                                        device_id=dev, device_id_type=MESH)


def place_shards(name, ws, chip):
    tiles, steps = _job_tiles([w.shape for w in ws], 1 << 19, BF16_ROWS)
    nj = len(ws)

    def body(chip_ref, *refs):
        i = pl.program_id(0)
        for k, (_, n) in enumerate(tiles):
            @pl.when(i < n)
            def _(w_ref=refs[k], o_ref=refs[nj + k]):
                o_ref[...] = w_ref[...].astype(BF16)

    spec = pltpu.PrefetchScalarGridSpec(
        num_scalar_prefetch=1, grid=(steps,),
        in_specs=[pl.BlockSpec((tr, w.shape[1]), lambda i, s, n=n: (jnp.minimum(i, n - 1), 0))
                  for w, (tr, n) in zip(ws, tiles)],
        out_specs=[pl.BlockSpec((None, tr, w.shape[1]), lambda i, s, n=n: (s[0], jnp.minimum(i, n - 1), 0))
                   for w, (tr, n) in zip(ws, tiles)])
    return _pcall(body, name=name, out_shape=[jax.ShapeDtypeStruct((N_CHIPS,) + w.shape, BF16) for w in ws],
                  grid_spec=spec, compiler_params=_params(("arbitrary",)))(chip, *ws)


class Comm:
    def __init__(self, ins, outs, aliases, sems, first, mid, last):
        self.ins, self.outs, self.aliases, self.sems = list(ins), list(outs), dict(aliases), list(sems)
        self.first, self.mid, self.last = first, mid, last


def run_comm(name, comm):
    ni, no = len(comm.ins), len(comm.outs)

    def body(*refs):
        ins, outs, sems = refs[:ni], refs[ni:ni + no], refs[ni + no:]
        comm.first(ins, outs, sems)
        comm.mid(ins, outs, sems)
        comm.last(ins, outs, sems)

    return _pcall(body, name=name, out_shape=comm.outs, in_specs=[ANY] * ni, out_specs=[ANY] * no,
                  input_output_aliases=comm.aliases, scratch_shapes=comm.sems, compiler_params=_params())(*comm.ins)


def gather_comm(bufs):
    n = len(bufs)

    def half(out, w, which):
        pr = out[w].shape[1] // 2
        return pl.ds(pl.multiple_of(which * pr, BF16_ROWS), pr)

    def first(ins, out, sems):
        isend, irecv, _, _ = sems
        x, y, c, chips = _place()
        for w in range(n):
            mine = out[w].at[2 * x + y, half(out, w, c)]
            for j, (cx, cy) in enumerate(chips):
                _remote(mine, mine, isend.at[3 * w + j], irecv.at[3 * w + j], (cx, cy, c)).start()

    def mid(ins, out, sems):
        isend, irecv, dsend, drecv = sems
        x, y, c, chips = _place()
        sib = (x, y, 1 - c)
        for w in range(n):
            for j, (cx, cy) in enumerate(chips):
                landed = out[w].at[2 * cx + cy, half(out, w, c)]
                _remote(landed, landed, isend.at[3 * w + j], irecv.at[3 * w + j], sib).wait_recv()
                _remote(landed, landed, dsend.at[3 * w + j], drecv.at[3 * w + j], sib).start()

    def last(ins, out, sems):
        isend, irecv, dsend, drecv = sems
        x, y, c, chips = _place()
        sib = (x, y, 1 - c)
        for w in range(n):
            for j, (cx, cy) in enumerate(chips):
                landed = out[w].at[2 * cx + cy, half(out, w, 1 - c)]
                _remote(landed, landed, dsend.at[3 * w + j], drecv.at[3 * w + j], sib).wait_recv()
        for w in range(n):
            sent = out[w].at[0, half(out, w, c)]
            for j in range(3):
                _remote(sent, sent, isend.at[3 * w + j], irecv.at[3 * w + j], sib).wait_send()
                _remote(sent, sent, dsend.at[3 * w + j], drecv.at[3 * w + j], sib).wait_send()

    return Comm(bufs, [jax.ShapeDtypeStruct(s.shape, s.dtype) for s in bufs], {w: w for w in range(n)},
                [pltpu.SemaphoreType.DMA((3 * n,))] * 4, first, mid, last)


def _nothing(ins, outs, sems):
    return None


def join_comms(a, b):
    ni, no, ns = len(a.ins), len(a.outs), len(a.sems)

    def both(f, g):
        def hook(ins, outs, sems):
            f(ins[:ni], outs[:no], sems[:ns])
            g(ins[ni:], outs[no:], sems[ns:])
        return hook

    aliases = dict(a.aliases)
    aliases.update({ni + k: no + v for k, v in b.aliases.items()})
    return Comm(a.ins + b.ins, a.outs + b.outs, aliases, a.sems + b.sems,
                both(a.first, b.first), both(a.mid, b.mid), both(a.last, b.last))


def exchange_comm(pieces):
    n = len(pieces)

    def copies(src, out, sems):
        x, y, c, _ = _place()
        return [_remote(src[w].at[k, 1 - c], out[w].at[k], sems[0].at[N_CHIPS * w + k], sems[1].at[N_CHIPS * w + k],
                        (x, y, 1 - c)) for w in range(n) for k in range(N_CHIPS)]

    def first(src, out, sems):
        for cp in copies(src, out, sems):
            cp.start()

    def last(src, out, sems):
        for cp in copies(src, out, sems):
            cp.wait()

    return Comm(pieces, [jax.ShapeDtypeStruct((N_CHIPS,) + s.shape[2:], s.dtype) for s in pieces], {},
                [pltpu.SemaphoreType.DMA((N_CHIPS * n,))] * 2, first, _nothing, last)


def scatter_comm(parts):
    n = len(parts)

    def copies(src, out, sems):
        x, y, c, chips = _place()
        return [_remote(src[w].at[2 * cx + cy], out[w].at[j], sems[0].at[3 * w + j], sems[1].at[3 * w + j], (cx, cy, c))
                for w in range(n) for j, (cx, cy) in enumerate(chips)]

    def first(src, out, sems):
        for cp in copies(src, out, sems):
            cp.start()

    def last(src, out, sems):
        for cp in copies(src, out, sems):
            cp.wait()

    return Comm(parts, [jax.ShapeDtypeStruct((3,) + s.shape[1:], s.dtype) for s in parts], {},
                [pltpu.SemaphoreType.DMA((3 * n,))] * 2, first, _nothing, last)


def share_comm(halves):
    n = len(halves)

    def first(ins, buf, sems):
        x, y, c, _ = _place()
        for w in range(n):
            _remote(buf[w].at[c], buf[w].at[c], sems[0].at[w], sems[1].at[w], (x, y, 1 - c)).start()

    def last(ins, buf, sems):
        x, y, c, _ = _place()
        for w in range(n):
            landed = buf[w].at[1 - c]
            _remote(landed, landed, sems[0].at[w], sems[1].at[w], (x, y, 1 - c)).wait_recv()
        for w in range(n):
            _remote(buf[w].at[c], buf[w].at[c], sems[0].at[w], sems[1].at[w], (x, y, 1 - c)).wait_send()

    return Comm(halves, [jax.ShapeDtypeStruct(s.shape, s.dtype) for s in halves], {w: w for w in range(n)},
                [pltpu.SemaphoreType.DMA((n,))] * 2, first, _nothing, last)


def gather_small(name, blk, reduce):
    r, cdim = blk.shape

    def body(in_ref, out_ref, *rest):
        if reduce:
            buf, send_sem, recv_sem = rest
        else:
            buf = out_ref
            send_sem, recv_sem = rest
        x, y, c, _ = _place()
        me = 4 * x + 2 * y + c
        buf[me] = in_ref[...]
        peers = []
        for dx in range(2):
            for dy in range(2):
                for dc in range(2):
                    if dx or dy or dc:
                        peers.append((dx, dy, dc))
        copies = []
        for s, (dx, dy, dc) in enumerate(peers):
            cp = _remote(in_ref, buf.at[me], send_sem.at[s], recv_sem.at[s],
                         ((1 - x if dx else x), (1 - y if dy else y), (1 - c if dc else c)))
            cp.start()
            copies.append(cp)
        for s, (dx, dy, dc) in enumerate(peers):
            px, py, pc_ = (1 - x if dx else x), (1 - y if dy else y), (1 - c if dc else c)
            landed = buf.at[4 * px + 2 * py + pc_]
            _remote(landed, landed, send_sem.at[s], recv_sem.at[s], (x, y, c)).wait_recv()
        for cp in copies:
            cp.wait_send()
        if reduce:
            tot = buf[0]
            for s in range(1, N_DEV):
                tot = tot + buf[s]
            out_ref[...] = tot

    vm = pl.BlockSpec(memory_space=pltpu.VMEM)
    out_shape = jax.ShapeDtypeStruct((r, cdim) if reduce else (N_DEV, r, cdim), F32)
    scratch = ([pltpu.VMEM((N_DEV, r, cdim), F32)] if reduce else []) + [pltpu.SemaphoreType.DMA((N_DEV - 1,))] * 2
    return _pcall(body, name=name, out_shape=out_shape, in_specs=[vm], out_specs=vm, scratch_shapes=scratch,
                  compiler_params=_params())(blk)


def _job_tiles(shapes, tile_bytes, mult):
    tiles = []
    for rows, cols in shapes:
        tr = _tile(rows, max(mult, tile_bytes // (4 * cols)), mult)
        tiles.append((tr, rows // tr))
    return tiles, max(n for _, n in tiles)


def sum_cores(name, owns, gots, place):
    nj = len(owns)
    tiles, _ = _job_tiles([o.shape[2:] for o in owns], 1 << 19, BF16_ROWS)
    steps = max(N_CHIPS * n for _, n in tiles)

    def body(place_ref, *refs):
        i = pl.program_id(0)
        for k, (_, n) in enumerate(tiles):
            @pl.when(i < N_CHIPS * n)
            def _(own_ref=refs[2 * k], got_ref=refs[2 * k + 1], o_ref=refs[2 * nj + k]):
                o_ref[...] = (own_ref[...].astype(F32) + got_ref[...].astype(F32)).astype(o_ref.dtype)

    in_specs, out_specs, out_shape, args = [], [], [], []
    for own, got, (tr, n) in zip(owns, gots, tiles):
        pc = own.shape[3]
        last = N_CHIPS * n - 1
        in_specs += [pl.BlockSpec((None, None, tr, pc),
                                  lambda i, s, n=n, last=last: (jnp.minimum(i, last) // n, s[1], jnp.minimum(i, last) % n, 0)),
                     pl.BlockSpec((None, tr, pc),
                                  lambda i, s, n=n, last=last: (jnp.minimum(i, last) // n, jnp.minimum(i, last) % n, 0))]
        out_specs.append(pl.BlockSpec((None, tr, pc),
                                      lambda i, s, n=n, last=last: (jnp.minimum(i, last) // n, jnp.minimum(i, last) % n, 0)))
        out_shape.append(jax.ShapeDtypeStruct(got.shape, BF16))
        args += [own, got]
    spec = pltpu.PrefetchScalarGridSpec(num_scalar_prefetch=1, grid=(steps,), in_specs=in_specs, out_specs=out_specs)
    return _pcall(body, name=name, out_shape=out_shape, grid_spec=spec,
                  compiler_params=_params(("arbitrary",)))(place, *args)


def sum_chips(name, parts, gots, place):
    nj = len(parts)
    tiles, steps = _job_tiles([p.shape[1:] for p in parts], 1 << 18, BF16_ROWS)

    def body(place_ref, *refs):
        i = pl.program_id(0)
        for k, (_, n) in enumerate(tiles):
            @pl.when(i < n)
            def _(part_ref=refs[2 * k], got_ref=refs[2 * k + 1], o_ref=refs[2 * nj + k]):
                tot = part_ref[...].astype(F32)
                for j in range(3):
                    tot = tot + got_ref[j].astype(F32)
                o_ref[...] = tot

    in_specs, out_specs, out_shape, args = [], [], [], []
    for part, got, (tr, n) in zip(parts, gots, tiles):
        pc = part.shape[2]
        in_specs += [pl.BlockSpec((None, tr, pc), lambda i, s, n=n: (s[0], jnp.minimum(i, n - 1), 0)),
                     pl.BlockSpec((3, tr, pc), lambda i, s, n=n: (0, jnp.minimum(i, n - 1), 0))]
        out_specs.append(pl.BlockSpec((None, tr, pc), lambda i, s, n=n: (s[1], jnp.minimum(i, n - 1), 0)))
        out_shape.append(jax.ShapeDtypeStruct((2,) + part.shape[1:], F32))
        args += [part, got]
    spec = pltpu.PrefetchScalarGridSpec(num_scalar_prefetch=1, grid=(steps,), in_specs=in_specs, out_specs=out_specs)
    return _pcall(body, name=name, out_shape=out_shape, grid_spec=spec,
                  compiler_params=_params(("arbitrary",)))(place, *args)


def adamw(name, jobs):
    c1 = 1.0 / (1.0 - ADAM_B1 ** ADAM_STEP)
    c2 = 1.0 / (1.0 - ADAM_B2 ** ADAM_STEP)
    nj = len(jobs)
    tiles, steps = _job_tiles([j[0].shape for j in jobs], 1 << 18, 8)

    def body(*refs):
        i = pl.program_id(0)
        for k, (_, n) in enumerate(tiles):
            w_ref, g_ref, m_ref, v_ref = refs[4 * k:4 * k + 4]
            d_ref, nm_ref, nv_ref = refs[4 * nj + 3 * k:4 * nj + 3 * k + 3]

            @pl.when(i < n)
            def _(w_ref=w_ref, g_ref=g_ref, m_ref=m_ref, v_ref=v_ref, d_ref=d_ref, nm_ref=nm_ref, nv_ref=nv_ref):
                gv = g_ref[...]
                nm = ADAM_B1 * m_ref[...] + (1.0 - ADAM_B1) * gv
                nv = ADAM_B2 * v_ref[...] + (1.0 - ADAM_B2) * (gv * gv)
                nm_ref[...] = nm
                nv_ref[...] = nv
                d_ref[...] = -ADAM_LR * ((nm * c1) / (jnp.sqrt(nv * c2) + ADAM_EPS) + ADAM_WD * w_ref[...])

    in_specs, out_specs, out_shape, args = [], [], [], []
    for (w, g, m, v), (tr, n) in zip(jobs, tiles):
        spec = pl.BlockSpec((tr, w.shape[1]), lambda i, n=n: (jnp.minimum(i, n - 1), 0))
        in_specs += [spec] * 4
        out_specs += [spec] * 3
        out_shape += [jax.ShapeDtypeStruct(w.shape, F32)] * 3
        args += [w, g, m, v]
    res = _pcall(body, name=name, out_shape=out_shape, grid=(steps,), in_specs=in_specs, out_specs=out_specs,
                 compiler_params=_params(("arbitrary",)))(*args)
    return [tuple(res[3 * k:3 * k + 3]) for k in range(nj)]


MATS = ["ffn1_w_in", "ffn1_w_out", "w_mix_in", "w_conv_out", "w_attn_out", "w_mix_out", "ffn2_w_in", "ffn2_w_out",
        "w_ple_gate", "w_ple_proj"]
COL_SHARDED = {"ffn1_w_in", "w_mix_in", "ffn2_w_in", "w_ple_proj"}
NORMS = ["ffn1_norm", "mix_norm", "ffn2_norm", "ple_norm", "final_norm"]
WEIGHTS = ["ffn1_norm", "ffn1_w_in", "ffn1_w_out", "mix_norm", "w_mix_in", "conv_w", "w_conv_out", "w_attn_out",
           "w_mix_out", "ffn2_norm", "ffn2_w_in", "ffn2_w_out", "ple_norm", "w_ple_gate", "w_ple_proj", "final_norm"]


def _pad_rows(a, rows):
    return jnp.concatenate([a, jnp.zeros((rows - a.shape[0],) + a.shape[1:], a.dtype)], axis=0)


def _step(x, p, tgt, w, m, v):
    t, d = x.shape
    tt = _tile(t, 256)
    tm = _tile(t, 512)
    tm2 = _tile(t, 1024)
    tq = _tile(t, 1024)

    chip = 2 * lax.axis_index("x") + lax.axis_index("y")
    place = jnp.stack([chip, lax.axis_index("c")]).astype(jnp.int32)

    placed = dict(zip(MATS, place_shards("place_shards", [w[k] for k in MATS], place)))
    full = {}

    def keep(names, bufs):
        for k, buf in zip(names, bufs):
            full[k] = buf if k in COL_SHARDED else buf.reshape(-1, buf.shape[2])

    def gather_of(names):
        return gather_comm([placed[k] for k in names])

    cw_all = gather_small("gather_conv_w", _pad_rows(w["conv_w"], 8), False)
    cw8 = jnp.concatenate([cw_all[2 * k] for k in range(N_CHIPS)], axis=1)
    g1, gm, g2, gp, gf = (w[k].reshape(1, d) for k in NORMS)

    def ffn_fwd(tag, h, g, first, w_in_name, w_out_name, riders):
        if first:
            n, bufs = rms_fwd(tag + "_norm", h, g, tt, comm=gather_of(first))
            keep(first, bufs)
            (a, s), bufs = ffn_in_act(tag + "_in", n, full[w_in_name], tm, comm=gather_of(riders))
            keep(riders, bufs)
        else:
            a, s, n = ffn_in_act(tag + "_in", h, full[w_in_name], tm, gain=g)
        return n, a, s, mm_nn(tag + "_out", s, full[w_out_name], F32, tm, res=h, alpha=0.5)

    n1, a1, s1, h1 = ffn_fwd("ffn1", x, g1, ["ffn1_w_in"], "ffn1_w_in", "ffn1_w_out", ["ffn1_w_out", "w_mix_in"])
    u = rms_fwd("mix_norm", h1, gm, tt)
    wmix = full["w_mix_in"]
    riders = [["w_conv_out", "w_attn_out", "w_mix_out"], ["ffn2_w_in"], ["ffn2_w_out", "w_ple_gate", "w_ple_proj"]]
    cbx, bufs = mm_nn_stacked("mix_in_conv", u, wmix, F32, tm2, d, 0, 3, comm=gather_of(riders[0]))
    keep(riders[0], bufs)
    qkv, bufs = mm_nn_stacked("mix_in_qkv", u, wmix, BF16, tm2, d, 3, 3, comm=gather_of(riders[1]))
    keep(riders[1], bufs)
    gates, bufs = mm_nn_stacked("mix_in_gates", u, wmix, BF16, tm2, d, 6, 2, comm=gather_of(riders[2]))
    keep(riders[2], bufs)
    wpp = full["w_ple_proj"]
    wpp = jnp.transpose(wpp, (1, 0, 2)).reshape(wpp.shape[1], -1)
    ycin, y_conv = conv_out_fwd("conv_out", cbx, cw8, full["w_conv_out"], tt)
    o = attn_fwd("attn", qkv, tq)
    y_attn = mm_nn("attn_out", o, full["w_attn_out"], BF16, tm)
    merged, h2 = mix_out_fwd("mix_out", gates, y_conv, y_attn, h1, full["w_mix_out"], tm)
    n2, a2, s2, h3 = ffn_fwd("ffn2", h2, g2, [], "ffn2_w_in", "ffn2_w_out", [])

    pieces, chip_sums, halves = {}, {}, {}

    def as_pieces(k):
        pc = pieces[k]
        return pc if k in COL_SHARDED else pc.reshape(N_CHIPS, 2, pc.shape[0] // (2 * N_CHIPS), pc.shape[1])

    def sum_siblings(tag, names):
        pcs = [as_pieces(k) for k in names]
        got = run_comm("exchange_" + tag, exchange_comm(pcs))
        chip_sums.update(zip(names, sum_cores("sum_cores_" + tag, pcs, got, place)))

    def scatter_of(names):
        return scatter_comm([chip_sums[k] for k in names])

    def sum_landed(tag, names, landed):
        halves.update(zip(names, sum_chips("sum_chips_" + tag, [chip_sums[k] for k in names], landed, place)))

    npl, dh4, dpp, dzg, dgf, loss_row = tail("tail", h3, p, tgt, gp, gf, full["w_ple_gate"], wpp, tt)
    dwpp = mm_tn_whole("ple_proj_dw", p, dpp, tm2)
    pieces["w_ple_proj"] = jnp.transpose(dwpp.reshape(2, p.shape[1] // 2, N_CHIPS, d // N_CHIPS), (2, 0, 1, 3))
    pieces["w_ple_gate"] = mm_tn_rows("ple_gate_dw", npl, dzg, tm2)
    dh3, df2, dgp = mm_nt("ple_gate_dx", dzg, full["w_ple_gate"], F32, tm, d, norm=(h3, gp, dh4), alpha=0.5)
    w_in, w_out = full["ffn2_w_in"], full["ffn2_w_out"]
    pieces["ffn2_w_out"] = mm_tn_rows("ffn2_dwout", s2, df2, tm2)
    da2 = ffn_ds_dact("ffn2_ds", df2, w_out, a2, tm2)
    pieces["ffn2_w_in"] = mm_tn_cols("ffn2_dwin", n2, da2, tm2)
    dh2, dh2b, dg2 = mm_nt_stacked("ffn2_dn", da2, w_in, F32, tm2, w_in.shape[2], norm=(h2, g2, dh3))
    pieces["w_mix_out"] = mm_tn_rows("mix_out_dw", merged, dh2b, tm2)
    dyc, dya, dgates = mix_out_bwd("mix_out_dx", dh2b, full["w_mix_out"], gates, y_conv, y_attn, tm)
    pieces["w_conv_out"] = mm_tn_rows("conv_out_dw", ycin, dyc, tm2)
    dcbx, dcw8 = conv_out_bwd("conv_out_dx", dyc, full["w_conv_out"], cbx, cw8, tt)
    pieces["w_attn_out"] = mm_tn_rows("attn_out_dw", o, dya, tm2)
    do = mm_nt("attn_out_dx", dya, full["w_attn_out"], BF16, tm, d)
    dq, dk, dv = attn_bwd("attn_bwd", qkv, do, tq)
    dmix = [dcbx, dq, dk, dv, dgates]
    early = ["ffn2_w_in", "ffn2_w_out", "w_ple_gate", "w_ple_proj", "w_mix_out", "w_conv_out", "w_attn_out"]
    swap = exchange_comm([as_pieces(k) for k in early])
    pieces["w_mix_in"], got = mm_tn_parts("mix_in_dw", u, dmix, tm2, comm=swap)
    chip_sums.update(zip(early, sum_cores("sum_cores_early", swap.ins, got, place)))
    swap = exchange_comm([as_pieces("w_mix_in")])
    (dh1, df1, dgm), landed = mm_nt_parts("mix_in_dx", dmix, wmix, tm, (h1, gm, dh2), 0.5,
                                          comm=join_comms(scatter_of(early), swap))
    sum_landed("early", early, landed[:len(early)])
    chip_sums["w_mix_in"] = sum_cores("sum_cores_mix", swap.ins, landed[len(early):], place)[0]
    w_in, w_out = full["ffn1_w_in"], full["ffn1_w_out"]
    pieces["ffn1_w_out"] = mm_tn_rows("ffn1_dwout", s1, df1, tm2)
    da1 = ffn_ds_dact("ffn1_ds", df1, w_out, a1, tm2)
    pieces["ffn1_w_in"], landed = mm_tn_cols("ffn1_dwin", n1, da1, tm2, comm=scatter_of(["w_mix_in"]))
    sum_landed("mix", ["w_mix_in"], landed)
    late = ["ffn1_w_in", "ffn1_w_out"]
    sum_siblings("late", late)
    done = early + ["w_mix_in"]
    (dx, _, dg1), landed = mm_nt_stacked(
        "ffn1_dn", da1, w_in, F32, tm2, w_in.shape[2], norm=(x, g1, dh1),
        comm=join_comms(scatter_of(late), share_comm([halves[k] for k in done])))
    sum_landed("late", late, landed[:len(late)])
    shared = dict(zip(done, landed[len(late):]))

    shared.update(zip(late, run_comm("share_halves", share_comm([halves[k] for k in late]))))
    grad, delta, new_m, new_v = {}, {}, {}, {}
    for k in MATS:
        grad[k] = shared[k].reshape(w[k].shape)

    small = jnp.concatenate([dg1, dgm, dg2, dgp, dgf, dcw8[:3], loss_row, jnp.zeros((7, d), F32)], axis=0)
    tot = gather_small("sum_small", small, True)
    loss = tot[8, 0]
    norm_w = jnp.concatenate([w[k].reshape(1, d) for k in NORMS] + [jnp.zeros((3, d), F32)], axis=0)
    norm_m = jnp.concatenate([m[k].reshape(1, d) for k in NORMS] + [jnp.zeros((3, d), F32)], axis=0)
    norm_v = jnp.concatenate([v[k].reshape(1, d) for k in NORMS] + [jnp.ones((3, d), F32)], axis=0)
    norm_g = jnp.concatenate([tot[0:5], jnp.zeros((3, d), F32)], axis=0)
    cs = d // N_CHIPS
    gcw = lax.dynamic_slice(tot[5:8], (0, chip * cs), (3, cs))
    conv_job = (_pad_rows(w["conv_w"], 8), _pad_rows(gcw, 8), _pad_rows(m["conv_w"], 8),
                jnp.concatenate([v["conv_w"], jnp.ones((5, cs), F32)], axis=0))

    steps = adamw("adamw", [(w[k], grad[k], m[k], v[k]) for k in MATS]
                  + [(norm_w, norm_g, norm_m, norm_v), conv_job])
    for k, res in zip(MATS, steps):
        delta[k], new_m[k], new_v[k] = res
    nd, nm, nv = steps[len(MATS)]
    for r, k in enumerate(NORMS):
        grad[k] = norm_g[r].reshape(w[k].shape)
        delta[k], new_m[k], new_v[k] = (a[r].reshape(w[k].shape) for a in (nd, nm, nv))
    cd, cm, cv = steps[len(MATS) + 1]
    grad["conv_w"], delta["conv_w"], new_m["conv_w"], new_v["conv_w"] = gcw, cd[:3], cm[:3], cv[:3]
    return loss, dx, grad, delta, new_m, new_v


def kernel(x, p, ffn1_norm, ffn1_w_in, ffn1_w_out, mix_norm, w_mix_in, conv_w, w_conv_out, w_attn_out, w_mix_out, ffn2_norm, ffn2_w_in, ffn2_w_out, ple_norm, w_ple_gate, w_ple_proj, final_norm, loss_target, m_ffn1_norm, m_ffn1_w_in, m_ffn1_w_out, m_mix_norm, m_w_mix_in, m_conv_w, m_w_conv_out, m_w_attn_out, m_w_mix_out, m_ffn2_norm, m_ffn2_w_in, m_ffn2_w_out, m_ple_norm, m_w_ple_gate, m_w_ple_proj, m_final_norm, v_ffn1_norm, v_ffn1_w_in, v_ffn1_w_out, v_mix_norm, v_w_mix_in, v_conv_w, v_w_conv_out, v_w_attn_out, v_w_mix_out, v_ffn2_norm, v_ffn2_w_in, v_ffn2_w_out, v_ple_norm, v_w_ple_gate, v_w_ple_proj, v_final_norm):
    ws = (ffn1_norm, ffn1_w_in, ffn1_w_out, mix_norm, w_mix_in, conv_w, w_conv_out, w_attn_out, w_mix_out, ffn2_norm,
          ffn2_w_in, ffn2_w_out, ple_norm, w_ple_gate, w_ple_proj, final_norm)
    ms = (m_ffn1_norm, m_ffn1_w_in, m_ffn1_w_out, m_mix_norm, m_w_mix_in, m_conv_w, m_w_conv_out, m_w_attn_out,
          m_w_mix_out, m_ffn2_norm, m_ffn2_w_in, m_ffn2_w_out, m_ple_norm, m_w_ple_gate, m_w_ple_proj, m_final_norm)
    vs = (v_ffn1_norm, v_ffn1_w_in, v_ffn1_w_out, v_mix_norm, v_w_mix_in, v_conv_w, v_w_conv_out, v_w_attn_out,
          v_w_mix_out, v_ffn2_norm, v_ffn2_w_in, v_ffn2_w_out, v_ple_norm, v_w_ple_gate, v_w_ple_proj, v_final_norm)
    assert x.shape[0] == 1 and p.shape[:2] == (1, 1), "one sequence and one layer per device"

    def strip(a):
        return a[0] if a.ndim == 3 or (a.ndim == 2 and a.shape[0] == 1) else a

    w = {k: strip(a) for k, a in zip(WEIGHTS, ws)}
    m = {k: strip(a) for k, a in zip(WEIGHTS, ms)}
    v = {k: strip(a) for k, a in zip(WEIGHTS, vs)}
    loss, dx, grad, delta, new_m, new_v = _step(x[0], p[0, 0], loss_target[0], w, m, v)
    shapes = [a.shape for a in ws]
    outs = [loss, dx[None]]
    for res in (grad, delta, new_m, new_v):
        outs += [res[k].reshape(s) for k, s in zip(WEIGHTS, shapes)]
    return tuple(outs)
```

```python
import functools
import math

import jax
import jax.numpy as jnp
from jax import lax
from jax.experimental import pallas as pl
from jax.experimental.pallas import tpu as pltpu

F32 = jnp.float32
BF16 = jnp.bfloat16
MESH = pl.DeviceIdType.MESH
ANY = pl.BlockSpec(memory_space=pl.ANY)

HEAD_DIM = 128
NORM_EPS = 1e-6
N_CHIPS = 4
N_DEV = 8
BF16_ROWS = 16
VMEM_LIMIT = 56 * 1024 * 1024
ACC_BYTES = 8 * 1024 * 1024
STICK_EXIT = 110.0

ADAM_LR = 0.001
ADAM_B1 = 0.9
ADAM_B2 = 0.999
ADAM_EPS = 1e-08
ADAM_WD = 0.01
ADAM_STEP = 10

NN = (((1,), (0,)), ((), ()))
NT = (((1,), (1,)), ((), ()))
TN = (((0,), (0,)), ((), ()))


def _params(sem=None, **kw):
    if sem is not None:
        kw["dimension_semantics"] = sem
    return pltpu.CompilerParams(vmem_limit_bytes=VMEM_LIMIT, **kw)


def _pcall(body, **kw):
    return pl.pallas_call(body, **kw)


def _tile(n, pref, mult=8):
    best = None
    for d in range(mult, min(n, pref) + 1, mult):
        if n % d == 0:
            best = d
    return best if best is not None else n


def _dot(a, b, dims):
    return lax.dot_general(a, b, dims, preferred_element_type=F32)


def _call(name, body, grid, in_specs, out_specs, out_shape, args, scratch=(), sem=None, comm=None):
    n_in, n_out, n_sc = len(in_specs), len(out_specs), len(scratch)
    if comm is None:
        def plain(*refs):
            body(refs[:n_in], refs[n_in:n_in + n_out], refs[n_in + n_out:])

        return _pcall(plain, name=name, out_shape=list(out_shape), grid=grid, in_specs=list(in_specs),
                      out_specs=list(out_specs), scratch_shapes=list(scratch), compiler_params=_params(sem))(*args)
    n_cin, n_cout = len(comm.ins), len(comm.outs)
    steps = math.prod(grid)

    def hosted(*refs):
        ins, c_ins = refs[:n_in], refs[n_in:n_in + n_cin]
        outs = refs[n_in + n_cin:n_in + n_cin + n_out]
        c_outs = refs[n_in + n_cin + n_out:n_in + n_cin + n_out + n_cout]
        rest = refs[n_in + n_cin + n_out + n_cout:]
        sems = rest[n_sc:]
        step = pl.program_id(0)
        for ax in range(1, len(grid)):
            step = step * grid[ax] + pl.program_id(ax)

        @pl.when(step == 0)
        def _():
            comm.first(c_ins, c_outs, sems)

        body(ins, outs, rest[:n_sc])

        @pl.when(step == (3 * steps) // 4)
        def _():
            comm.mid(c_ins, c_outs, sems)

        @pl.when(step == steps - 1)
        def _():
            comm.last(c_ins, c_outs, sems)

    res = _pcall(hosted, name=name, out_shape=list(out_shape) + comm.outs, grid=grid,
                 in_specs=list(in_specs) + [ANY] * n_cin, out_specs=list(out_specs) + [ANY] * n_cout,
                 input_output_aliases={n_in + k: n_out + v for k, v in comm.aliases.items()},
                 scratch_shapes=list(scratch) + comm.sems,
                 compiler_params=_params(("arbitrary",) * len(grid)))(*args, *comm.ins)
    return list(res[:n_out]), list(res[n_out:])


NORM_CHUNK = 256


def _norm_bwd_tile(read_dn, rows, first, h_ref, g_ref, dr_ref, dh_ref, dhb_ref, dg_ref, alpha):
    @pl.when(first)
    def _():
        dg_ref[...] = jnp.zeros_like(dg_ref)

    gv = g_ref[...]
    tot = jnp.zeros_like(gv)
    for c0 in range(0, rows, NORM_CHUNK):
        sl = slice(c0, min(rows, c0 + NORM_CHUNK))
        hv = h_ref[sl, :]
        rs = _rstd(hv)
        hn = hv * rs
        dnv = read_dn(sl)
        gy = dnv * gv
        dh = dr_ref[sl, :] + rs * (gy - hn * jnp.mean(gy * hn, axis=-1, keepdims=True))
        dh_ref[sl, :] = dh
        dhb_ref[sl, :] = (alpha * dh).astype(BF16)
        tot = tot + jnp.sum(dnv * hn, axis=0, keepdims=True)
    dg_ref[...] += tot


def _mm(name, a, b, out_sds, grid, a_spec, b_spec, o_spec, dims, acc_shape, res=None, alpha=1.0, comm=None,
        norm=None):
    nk = grid[2]

    def body(ins, outs, scratch):
        a_ref, b_ref = ins[:2]
        r_ref = ins[2] if res is not None else None
        o_ref = outs[0]

        def finish(read):
            if norm is not None:
                first = jnp.logical_and(pl.program_id(0) == 0, pl.program_id(1) == 0)
                _norm_bwd_tile(read, o_ref.shape[0], first, *ins[2:5], *outs, alpha)
                return
            r = read(slice(None))
            if alpha != 1.0:
                r = r * alpha
            if r_ref is not None:
                r = r_ref[...] + r
            if len(o_ref.shape) == 3:
                half = o_ref.shape[1]
                o_ref[0] = r[:half].astype(o_ref.dtype)
                o_ref[1] = r[half:].astype(o_ref.dtype)
            else:
                o_ref[...] = r.astype(o_ref.dtype)

        if nk == 1:
            part = _dot(a_ref[...].astype(BF16), b_ref[...].astype(BF16), dims)
            finish(lambda sl: part[sl])
        else:
            acc_ref = scratch[0]
            kk = pl.program_id(2)

            @pl.when(kk == 0)
            def _():
                acc_ref[...] = jnp.zeros_like(acc_ref)

            acc_ref[...] += _dot(a_ref[...].astype(BF16), b_ref[...].astype(BF16), dims)

            @pl.when(kk == nk - 1)
            def _():
                finish(lambda sl: acc_ref[sl, :])

    in_specs = [a_spec, b_spec]
    args = [a, b]
    out_specs, out_shape = [o_spec], [out_sds]
    sem = ("parallel", "parallel", "arbitrary")
    if res is not None:
        in_specs.append(o_spec)
        args.append(res)
    if norm is not None:
        width = out_sds.shape[1]
        whole = pl.BlockSpec((1, width), lambda i, j, r: (0, 0))
        in_specs += [o_spec, whole, o_spec]
        args += list(norm)
        out_specs = [o_spec, o_spec, whole]
        out_shape = [jax.ShapeDtypeStruct(out_sds.shape, F32), jax.ShapeDtypeStruct(out_sds.shape, BF16),
                     jax.ShapeDtypeStruct((1, width), F32)]
        sem = ("arbitrary", "arbitrary", "arbitrary")
    scratch = [] if nk == 1 else [pltpu.VMEM(acc_shape, F32)]
    got = _call(name, body, grid, in_specs, out_specs, out_shape, args, scratch, sem, comm)
    if norm is not None:
        return got if comm is None else (got[0], got[1])
    return got[0] if comm is None else (got[0][0], got[1])


def ffn_in_act(name, n, w4, tm, comm=None, gain=None):
    t, d = n.shape
    cs = w4.shape[2]

    def body(ins, outs, scratch):
        wg_ref, wu_ref = ins[-2:]
        a_ref, s_ref = outs[:2]
        if gain is None:
            nv = ins[0][...]
        else:
            @pl.when(pl.program_id(1) == 0)
            def _():
                hv = ins[0][...]
                scratch[0][...] = (hv * _rstd(hv) * ins[1][...]).astype(BF16)
                outs[2][...] = scratch[0][...]

            nv = scratch[0][...]
        gate = _dot(nv, wg_ref[...], NN)
        up = _dot(nv, wu_ref[...], NN)
        a_ref[0] = gate.astype(BF16)
        a_ref[1] = up.astype(BF16)
        s_ref[...] = (gate * jax.nn.sigmoid(gate) * up).astype(BF16)

    rows = pl.BlockSpec((tm, d), lambda i, j: (i, 0))
    in_specs = [rows] + ([] if gain is None else [pl.BlockSpec((1, d), lambda i, j: (0, 0))])
    in_specs += [pl.BlockSpec((None, d, cs), lambda i, j: (j, 0, 0)),
                 pl.BlockSpec((None, d, cs), lambda i, j: (2 + j, 0, 0))]
    out_specs = [pl.BlockSpec((2, tm, cs), lambda i, j: (0, i, j)), pl.BlockSpec((tm, cs), lambda i, j: (i, j))]
    out_shape = [jax.ShapeDtypeStruct((2, t, 2 * cs), BF16), jax.ShapeDtypeStruct((t, 2 * cs), BF16)]
    if gain is not None:
        out_specs.append(rows)
        out_shape.append(jax.ShapeDtypeStruct((t, d), BF16))
    got = _call(name, body, (t // tm, 2), in_specs, out_specs, out_shape,
                [n] + ([] if gain is None else [gain]) + [w4, w4],
                [] if gain is None else [pltpu.VMEM((tm, d), BF16)], ("parallel", "arbitrary"), comm)
    return got if comm is None else (got[0], got[1])


def ffn_ds_dact(name, df, w_out, a3, tm):
    t, d = df.shape
    f = w_out.shape[0]
    cs = f // 2

    def body(ins, outs, scratch):
        df_ref, w_ref, a_ref = ins
        ds = _dot(df_ref[...], w_ref[...], NT)
        for c0 in range(0, tm, NORM_CHUNK):
            sl = slice(c0, min(tm, c0 + NORM_CHUNK))
            gate = a_ref[0, sl, :].astype(F32)
            up = a_ref[1, sl, :].astype(F32)
            sg = jax.nn.sigmoid(gate)
            outs[0][0, sl, :] = (ds[sl] * up * sg * (1.0 + gate * (1.0 - sg))).astype(BF16)
            outs[0][1, sl, :] = (ds[sl] * gate * sg).astype(BF16)

    blk = pl.BlockSpec((2, tm, cs), lambda i, j: (0, i, j))
    return _call(name, body, (t // tm, 2),
                 [pl.BlockSpec((tm, d), lambda i, j: (i, 0)), pl.BlockSpec((cs, d), lambda i, j: (j, 0)), blk],
                 [blk], [jax.ShapeDtypeStruct((2, t, f), BF16)], [df, w_out, a3], (), ("parallel", "parallel"))[0]


def _part_ranges(parts, d):
    out, lo = [], 0
    for p in parts:
        out.append((lo, p.shape[1] // d))
        lo += p.shape[1] // d
    return out, lo


def mm_nt_parts(name, parts, w4, tm, norm, alpha, comm=None):
    m = parts[0].shape[0]
    d, cs = w4.shape[1], w4.shape[2]
    per = cs // d
    ranges, nblk = _part_ranges(parts, d)
    np_ = len(parts)

    def body(ins, outs, scratch):
        w_ref, acc = ins[np_], scratch[0]
        r = pl.program_id(1)

        @pl.when(r == 0)
        def _():
            acc[...] = jnp.zeros_like(acc)

        for (lo, n), a_ref in zip(ranges, ins[:np_]):
            @pl.when(jnp.logical_and(r >= lo, r < lo + n))
            def _(a_ref=a_ref):
                acc[...] += _dot(a_ref[...], w_ref[...], NT)

        @pl.when(r == nblk - 1)
        def _():
            _norm_bwd_tile(lambda sl: acc[sl, :], tm, pl.program_id(0) == 0, *ins[np_ + 1:], *outs, alpha)

    rows = pl.BlockSpec((tm, d), lambda i, r: (i, 0))
    whole = pl.BlockSpec((1, d), lambda i, r: (0, 0))
    specs = [pl.BlockSpec((tm, d), lambda i, r, lo=lo, n=n: (i, jnp.clip(r - lo, 0, n - 1))) for lo, n in ranges]
    specs += [pl.BlockSpec((None, d, d), lambda i, r: (r // per, 0, r % per)), rows, whole, rows]
    got = _call(name, body, (m // tm, nblk), specs, [rows, rows, whole],
                [jax.ShapeDtypeStruct((m, d), F32), jax.ShapeDtypeStruct((m, d), BF16),
                 jax.ShapeDtypeStruct((1, d), F32)],
                list(parts) + [w4] + list(norm), [pltpu.VMEM((tm, d), F32)], ("arbitrary", "arbitrary"), comm)
    return got if comm is None else (got[0], got[1])


def mm_tn_parts(name, xa, parts, tt, comm=None):
    t, k = xa.shape
    d = k
    pr = k // 2
    ranges, nblk = _part_ranges(parts, d)
    per = nblk // N_CHIPS

    def body(ins, outs, scratch):
        x_ref, acc = ins[0], scratch[0]
        jb, r = pl.program_id(0), pl.program_id(1)

        @pl.when(r == 0)
        def _():
            acc[...] = jnp.zeros_like(acc)

        for (lo, n), p_ref in zip(ranges, ins[1:]):
            @pl.when(jnp.logical_and(jb >= lo, jb < lo + n))
            def _(p_ref=p_ref):
                acc[...] += _dot(x_ref[...], p_ref[...], TN)

        @pl.when(r == t // tt - 1)
        def _():
            outs[0][0] = acc[:pr].astype(BF16)
            outs[0][1] = acc[pr:].astype(BF16)

    def part_spec(lo, n):
        return pl.BlockSpec((tt, d), lambda jb, r: (jnp.where(jnp.logical_and(jb >= lo, jb < lo + n), r, 0),
                                                    jnp.clip(jb - lo, 0, n - 1)))

    specs = [pl.BlockSpec((tt, k), lambda jb, r: (r, 0))] + [part_spec(lo, n) for lo, n in ranges]
    got = _call(name, body, (nblk, t // tt), specs,
                [pl.BlockSpec((None, 2, pr, d), lambda jb, r: (jb // per, 0, 0, jb % per))],
                [jax.ShapeDtypeStruct((N_CHIPS, 2, pr, per * d), BF16)], [xa] + list(parts),
                [pltpu.VMEM((k, d), F32)], ("parallel", "arbitrary"), comm)
    return got[0] if comm is None else (got[0][0], got[1])


def mm_nn(name, a, w, out_dtype, tm, res=None, alpha=1.0):
    m, k = a.shape
    n = w.shape[1]
    return _mm(name, a, w, jax.ShapeDtypeStruct((m, n), out_dtype), (m // tm, 1, 1),
               pl.BlockSpec((tm, k), lambda i, j, r: (i, 0)),
               pl.BlockSpec((k, n), lambda i, j, r: (0, 0)),
               pl.BlockSpec((tm, n), lambda i, j, r: (i, 0)), NN, None, res=res, alpha=alpha)


def mm_nn_stacked(name, a, w4, out_dtype, tm, tn, j0=0, nj=None, comm=None):
    m, k = a.shape
    cs = w4.shape[2]
    per = cs // tn
    nj = N_CHIPS * per - j0 if nj is None else nj
    return _mm(name, a, w4, jax.ShapeDtypeStruct((m, nj * tn), out_dtype), (m // tm, nj, 1),
               pl.BlockSpec((tm, k), lambda i, j, r: (i, 0)),
               pl.BlockSpec((None, k, tn), lambda i, j, r: ((j + j0) // per, 0, (j + j0) % per)),
               pl.BlockSpec((tm, tn), lambda i, j, r: (i, j)), NN, None, comm=comm)


def mm_nt(name, dy, w, out_dtype, tm, tko, norm=None, alpha=1.0):
    m, n = dy.shape
    k = w.shape[0]
    return _mm(name, dy, w, jax.ShapeDtypeStruct((m, k), out_dtype), (m // tm, k // tko, 1),
               pl.BlockSpec((tm, n), lambda i, j, r: (i, 0)),
               pl.BlockSpec((tko, n), lambda i, j, r: (j, 0)),
               pl.BlockSpec((tm, tko), lambda i, j, r: (i, j)), NT, None, norm=norm, alpha=alpha)


def mm_nt_stacked(name, dy, w4, out_dtype, tm, tn, comm=None, norm=None, alpha=1.0):
    m = dy.shape[-2]
    k, cs = w4.shape[1], w4.shape[2]
    per = cs // tn
    if dy.ndim == 3:
        dy_spec = pl.BlockSpec((None, tm, cs), lambda i, j, r: (r // 2, i, r % 2))
    else:
        dy_spec = pl.BlockSpec((tm, tn), lambda i, j, r: (i, r))
    return _mm(name, dy, w4, jax.ShapeDtypeStruct((m, k), out_dtype), (m // tm, 1, N_CHIPS * per), dy_spec,
               pl.BlockSpec((None, k, tn), lambda i, j, r: (r // per, 0, r % per)),
               pl.BlockSpec((tm, k), lambda i, j, r: (i, 0)), NT, (tm, k), comm=comm, norm=norm, alpha=alpha)


def mm_tn_rows(name, xa, dy, tt):
    t, k = xa.shape
    n = dy.shape[1]
    tkr = k if k * n * 4 <= ACC_BYTES else k // 2
    return _mm(name, xa, dy, jax.ShapeDtypeStruct((k, n), BF16), (k // tkr, 1, t // tt),
               pl.BlockSpec((tt, tkr), lambda i, j, r: (r, i)),
               pl.BlockSpec((tt, n), lambda i, j, r: (r, 0)),
               pl.BlockSpec((tkr, n), lambda i, j, r: (i, 0)), TN, (tkr, n))


def mm_tn_whole(name, xa, dy, tt):
    t, k = xa.shape
    n = dy.shape[1]
    return _mm(name, xa, dy, jax.ShapeDtypeStruct((k, n), BF16), (1, 1, t // tt),
               pl.BlockSpec((tt, k), lambda i, j, r: (r, 0)),
               pl.BlockSpec((tt, n), lambda i, j, r: (r, 0)),
               pl.BlockSpec((k, n), lambda i, j, r: (0, 0)), TN, (k, n))


def mm_tn_cols(name, xa, dy, tt, comm=None):
    t, k = xa.shape
    pr = k // 2
    if dy.ndim == 3:
        cs = dy.shape[2] // 2
        dy_spec = pl.BlockSpec((None, tt, cs), lambda i, j, r: (j // 2, r, j % 2))
    else:
        cs = dy.shape[1] // N_CHIPS
        dy_spec = pl.BlockSpec((tt, cs), lambda i, j, r: (r, j))
    return _mm(name, xa, dy, jax.ShapeDtypeStruct((N_CHIPS, 2, pr, cs), BF16), (1, N_CHIPS, t // tt),
               pl.BlockSpec((tt, k), lambda i, j, r: (r, 0)), dy_spec,
               pl.BlockSpec((None, 2, pr, cs), lambda i, j, r: (j, 0, 0, 0)), TN, (k, cs), comm=comm)


def _rows(tt, w, col=0):
    return pl.BlockSpec((tt, w), lambda i: (i, col))


def _whole(shape):
    return pl.BlockSpec(shape, lambda i: (0,) * len(shape))


def _rstd(h):
    return lax.rsqrt(jnp.mean(h * h, axis=-1, keepdims=True) + NORM_EPS)


def rms_fwd(name, h, g, tt, comm=None):
    t, d = h.shape

    def body(ins, outs, scratch):
        hv = ins[0][...]
        outs[0][...] = (hv * _rstd(hv) * ins[1][...]).astype(BF16)

    got = _call(name, body, (t // tt,), [_rows(tt, d), _whole((1, d))], [_rows(tt, d)],
                [jax.ShapeDtypeStruct((t, d), BF16)], [h, g], (), ("parallel",), comm)
    return got[0] if comm is None else (got[0][0], got[1])


def mix_out_fwd(name, gates, yc, ya, h, w, tt):
    t, d = yc.shape

    def body(g_ref, yc_ref, ya_ref, h_ref, w_ref, m_ref, o_ref):
        merged = (jax.nn.sigmoid(g_ref[:, :d].astype(F32)) * yc_ref[...].astype(F32)
                  + jax.nn.sigmoid(g_ref[:, d:].astype(F32)) * ya_ref[...].astype(F32)).astype(BF16)
        m_ref[...] = merged
        o_ref[...] = h_ref[...] + _dot(merged, w_ref[...], NN)

    return _pcall(body, name=name,
                  out_shape=(jax.ShapeDtypeStruct((t, d), BF16), jax.ShapeDtypeStruct((t, d), F32)),
                  grid=(t // tt,),
                  in_specs=[_rows(tt, 2 * d), _rows(tt, d), _rows(tt, d), _rows(tt, d), _whole((d, d))],
                  out_specs=(_rows(tt, d), _rows(tt, d)),
                  compiler_params=_params(("parallel",)))(gates, yc, ya, h, w)


def mix_out_bwd(name, dh, w, gates, yc, ya, tt):
    t, d = yc.shape

    def body(dh_ref, w_ref, g_ref, yc_ref, ya_ref, dyc_ref, dya_ref, dg_ref):
        dmv = _dot(dh_ref[...], w_ref[...], NT)
        sc = jax.nn.sigmoid(g_ref[:, :d].astype(F32))
        sa = jax.nn.sigmoid(g_ref[:, d:].astype(F32))
        dyc_ref[...] = (dmv * sc).astype(BF16)
        dya_ref[...] = (dmv * sa).astype(BF16)
        dg_ref[:, :d] = (dmv * yc_ref[...].astype(F32) * sc * (1.0 - sc)).astype(BF16)
        dg_ref[:, d:] = (dmv * ya_ref[...].astype(F32) * sa * (1.0 - sa)).astype(BF16)

    return _pcall(body, name=name,
                  out_shape=(jax.ShapeDtypeStruct((t, d), BF16), jax.ShapeDtypeStruct((t, d), BF16),
                             jax.ShapeDtypeStruct((t, 2 * d), BF16)),
                  grid=(t // tt,),
                  in_specs=[_rows(tt, d), _whole((d, d)), _rows(tt, 2 * d), _rows(tt, d), _rows(tt, d)],
                  out_specs=(_rows(tt, d), _rows(tt, d), _rows(tt, 2 * d)),
                  compiler_params=_params(("parallel",)))(dh, w, gates, yc, ya)


def _shift_down(cur, prev8, s):
    tt = cur.shape[0]
    rolled = pltpu.roll(cur, s, 0)
    row8 = lax.broadcasted_iota(jnp.int32, prev8.shape, 0)
    first8 = jnp.where(row8 < s, pltpu.roll(prev8, s, 0), rolled[:8])
    return jnp.concatenate([first8, rolled[8:]], axis=0) if tt > 8 else first8


def _shift_up(cur, next8, s):
    tt = cur.shape[0]
    rolled = pltpu.roll(cur, tt - s, 0)
    row8 = lax.broadcasted_iota(jnp.int32, next8.shape, 0)
    last8 = jnp.where(row8 >= 8 - s, pltpu.roll(next8, 8 - s, 0), rolled[tt - 8:])
    return jnp.concatenate([rolled[:tt - 8], last8], axis=0) if tt > 8 else last8


def _prev_rows(tt, d, col):
    return pl.BlockSpec((BF16_ROWS, d), lambda i: (jnp.maximum(i * (tt // BF16_ROWS) - 1, 0), col))


def _next_rows(tt, d, col, t):
    return pl.BlockSpec((BF16_ROWS, d),
                        lambda i: (jnp.minimum((i + 1) * (tt // BF16_ROWS), t // BF16_ROWS - 1), col))


def conv_out_fwd(name, cbx, cw8, w_out, tt):
    t, d3 = cbx.shape
    d = d3 // 3

    def body(cb_ref, cc_ref, cx_ref, pc_ref, px_ref, w_ref, wo_ref, o_ref, y_ref):
        has_prev = (pl.program_id(0) > 0).astype(F32)
        cc = cc_ref[...].astype(F32) * cx_ref[...].astype(F32)
        prev = pc_ref[...].astype(F32)[8:] * px_ref[...].astype(F32)[8:] * has_prev
        w = w_ref[...]
        conv = w[0:1] * _shift_down(cc, prev, 2) + w[1:2] * _shift_down(cc, prev, 1) + w[2:3] * cc
        ycin = (cb_ref[...].astype(F32) * conv).astype(BF16)
        o_ref[...] = ycin
        y_ref[...] = _dot(ycin, wo_ref[...], NN).astype(BF16)

    out = jax.ShapeDtypeStruct((t, d), BF16)
    return _pcall(body, name=name, out_shape=(out, out), grid=(t // tt,),
                  in_specs=[_rows(tt, d, 0), _rows(tt, d, 1), _rows(tt, d, 2), _prev_rows(tt, d, 1),
                            _prev_rows(tt, d, 2), _whole((8, d)), _whole((d, d))],
                  out_specs=(_rows(tt, d), _rows(tt, d)),
                  compiler_params=_params(("parallel",)))(cbx, cbx, cbx, cbx, cbx, cw8, w_out)


def conv_out_bwd(name, dyc, w_out, cbx, cw8, tt):
    t, d3 = cbx.shape
    d = d3 // 3
    n = t // tt

    def body(dy_ref, ndy_ref, wo_ref, cb_ref, cc_ref, cx_ref, pc_ref, px_ref, ncb_ref, w_ref, o_ref, dw_ref):
        i = pl.program_id(0)
        has_prev = (i > 0).astype(F32)
        has_next = (i < n - 1).astype(F32)
        cb = cb_ref[...].astype(F32)
        ccv = cc_ref[...].astype(F32)
        cxv = cx_ref[...].astype(F32)
        cc = ccv * cxv
        prev = pc_ref[...].astype(F32)[8:] * px_ref[...].astype(F32)[8:] * has_prev
        w = w_ref[...]
        cc1 = _shift_down(cc, prev, 1)
        cc2 = _shift_down(cc, prev, 2)
        conv = w[0:1] * cc2 + w[1:2] * cc1 + w[2:3] * cc
        dyv = _dot(dy_ref[...], wo_ref[...], NT)
        dconv = dyv * cb
        dnext = _dot(ndy_ref[...], wo_ref[...], NT)[:8] * ncb_ref[...].astype(F32)[:8] * has_next
        dcc = w[2:3] * dconv + w[1:2] * _shift_up(dconv, dnext, 1) + w[0:1] * _shift_up(dconv, dnext, 2)
        o_ref[:, :d] = (dyv * conv).astype(BF16)
        o_ref[:, d:2 * d] = (dcc * cxv).astype(BF16)
        o_ref[:, 2 * d:] = (dcc * ccv).astype(BF16)

        @pl.when(i == 0)
        def _():
            dw_ref[...] = jnp.zeros_like(dw_ref)

        dw_ref[0:1, :] += jnp.sum(dconv * cc2, axis=0, keepdims=True)
        dw_ref[1:2, :] += jnp.sum(dconv * cc1, axis=0, keepdims=True)
        dw_ref[2:3, :] += jnp.sum(dconv * cc, axis=0, keepdims=True)

    return _pcall(body, name=name,
                  out_shape=(jax.ShapeDtypeStruct((t, d3), BF16), jax.ShapeDtypeStruct((8, d), F32)),
                  grid=(n,),
                  in_specs=[_rows(tt, d), _next_rows(tt, d, 0, t),
                            _whole((d, d)), _rows(tt, d, 0), _rows(tt, d, 1), _rows(tt, d, 2),
                            _prev_rows(tt, d, 1), _prev_rows(tt, d, 2), _next_rows(tt, d, 0, t), _whole((8, d))],
                  out_specs=(_rows(tt, d3), _whole((8, d))),
                  compiler_params=_params(("arbitrary",)))(dyc, dyc, w_out, cbx, cbx, cbx, cbx, cbx, cbx, cw8)


def tail(name, h3, p, tgt, gp, gf, w_gate, w_proj, tt):
    t, d = h3.shape
    pd = p.shape[1]

    def body(h_ref, p_ref, tg_ref, gp_ref, gf_ref, wg_ref, wp_ref, np_ref, dh_ref, dpp_ref, dzg_ref, dgf_ref,
             loss_ref):
        hv = h_ref[...]
        npl = (hv * _rstd(hv) * gp_ref[...]).astype(BF16)
        np_ref[...] = npl
        pg = jax.nn.sigmoid(_dot(npl, wg_ref[...], NN))
        ppv = _dot(p_ref[...].astype(BF16), wp_ref[...], NN)
        h4 = hv + pg * ppv
        r4 = _rstd(h4)
        hn = h4 * r4
        gfv = gf_ref[...]
        err = hn * gfv - tg_ref[...]
        dy = err * (1.0 / d)
        gy = dy * gfv
        dh4 = r4 * (gy - hn * jnp.mean(gy * hn, axis=-1, keepdims=True))
        dh_ref[...] = dh4
        dpp_ref[...] = (dh4 * pg).astype(BF16)
        dzg_ref[...] = (dh4 * ppv * pg * (1.0 - pg)).astype(BF16)

        @pl.when(pl.program_id(0) == 0)
        def _():
            dgf_ref[...] = jnp.zeros_like(dgf_ref)
            loss_ref[...] = jnp.zeros_like(loss_ref)

        dgf_ref[...] += jnp.sum(dy * hn, axis=0, keepdims=True)
        tok = jnp.mean(err * err, axis=-1, keepdims=True)
        loss_ref[...] += 0.5 * jnp.sum(tok, axis=0, keepdims=True) * jnp.ones((1, loss_ref.shape[1]), F32)

    return _pcall(body, name=name,
                  out_shape=(jax.ShapeDtypeStruct((t, d), BF16), jax.ShapeDtypeStruct((t, d), F32),
                             jax.ShapeDtypeStruct((t, d), BF16), jax.ShapeDtypeStruct((t, d), BF16),
                             jax.ShapeDtypeStruct((1, d), F32), jax.ShapeDtypeStruct((1, d), F32)),
                  grid=(t // tt,),
                  in_specs=[_rows(tt, d), _rows(tt, pd), _rows(tt, d), _whole((1, d)), _whole((1, d)),
                            _whole((d, d)), _whole((pd, d))],
                  out_specs=(_rows(tt, d), _rows(tt, d), _rows(tt, d), _rows(tt, d), _whole((1, d)),
                             _whole((1, d))),
                  compiler_params=_params(("arbitrary",)))(h3, p, tgt, gp, gf, w_gate, w_proj)


SCALE = 1.0 / math.sqrt(HEAD_DIM)


def _log_stick(z):
    return -(jnp.maximum(z, 0.0) + jnp.log(1.0 + jnp.exp(-jnp.abs(z))))


def _tri_sum(x, tri):
    hi = x.astype(BF16)
    lo = (x - hi.astype(F32)).astype(BF16)
    return _dot(hi, tri, NN) + _dot(lo, tri, NN)


KEY_BLOCK = 128
NEAR = 3
THIN_ROWS = 32


def _pad_block(x):
    n = x.shape[0]
    return x if n == KEY_BLOCK else jnp.concatenate([x, jnp.zeros((KEY_BLOCK - n, x.shape[1]), x.dtype)], axis=0)


def _sb_near(qs, jds, k_ref, below, upper, last_rows):
    near_rows = (KEY_BLOCK,) * (NEAR - 1) + (last_rows,)
    pairs = [(s, b) for s in range(len(qs)) for b in range(NEAR)]
    rows = {(s, b): _block_rows(jnp.maximum(jds[s] - b, 0), KEY_BLOCK) for s, b in pairs}
    z = {(s, b): _dot(qs[s][:near_rows[b]], k_ref[rows[s, b], :], NT) * SCALE for s, b in pairs}
    lg = {(s, b): jnp.where(below, _log_stick(z[s, b]), 0.0) if b == 0 else _log_stick(z[s, b]) for s, b in pairs}
    cum = {(s, b): _tri_sum(lg[s, b], upper) for s, b in pairs}
    out, carries = [], []
    for s in range(len(qs)):
        c = cum[s, 0][:, 0:1]
        blocks = [(rows[s, 0], z[s, 0], jnp.exp(jnp.where(below, z[s, 0] + cum[s, 0], -1e30)))]
        for b in range(1, NEAR):
            live = jds[s] >= b
            off = c[:near_rows[b]] + jnp.where(live, 0.0, -1e30)
            blocks.append((rows[s, b], z[s, b], jnp.exp(z[s, b] + cum[s, b] + off)))
            c = c + _pad_block(jnp.where(live, cum[s, b][:, 0:1], 0.0))
        out.append(blocks)
        carries.append(c)
    return out, carries


def _sb_far(q, kj, upper, c, skip):
    z = _dot(q, kj, NT) * SCALE
    cum = _tri_sum(_log_stick(z), upper)
    return z, jnp.exp(z + cum + (c + jnp.where(skip, -1e30, 0.0))), c + jnp.where(skip, 0.0, cum[:, 0:1])


def _took_it(j, jd, last_rows):
    first = lax.broadcasted_iota(jnp.int32, (KEY_BLOCK, 1), 0) < last_rows
    return jnp.logical_and(j == jd - (NEAR - 1), first)


def _block_rows(j, size):
    return pl.ds(pl.multiple_of(j * size, size), size)


def _sweep_on(st):
    return jnp.logical_and(st[0] >= 0, jnp.max(st[1]) > -STICK_EXIT)


def attn_fwd(name, qkv, tq):
    t, d3 = qkv.shape
    d = d3 // 3
    nh = d // HEAD_DIM
    nq = t // tq
    tb = KEY_BLOCK
    nsub = tq // tb

    def body(q_ref, k_ref, v_ref, o_ref):
        i = pl.program_id(1)
        row = lax.broadcasted_iota(jnp.int32, (tb, tb), 0)
        col = lax.broadcasted_iota(jnp.int32, (tb, tb), 1)
        upper = (row >= col).astype(BF16)
        qs = [q_ref[s * tb:(s + 1) * tb, :] for s in range(nsub)]
        jds = [i * nsub + s for s in range(nsub)]
        near, carries = _sb_near(qs, jds, k_ref, col < row, upper, THIN_ROWS)
        state = []
        for s in range(nsub):
            acc = jnp.zeros((tb, HEAD_DIM), F32)
            for rows, _, a in near[s]:
                acc = acc + _pad_block(_dot(a.astype(BF16), v_ref[rows, :], NN))
            state.append((qs[s], jds[s], carries[s], acc))
        for s, (q, jd, c, acc) in enumerate(state):

            def step(st, q=q, jd=jd):
                rows = _block_rows(st[0], tb)
                _, a, c2 = _sb_far(q, k_ref[rows, :], upper, st[1], _took_it(st[0], jd, THIN_ROWS))
                return st[0] - 1, c2, st[2] + _dot(a.astype(BF16), v_ref[rows, :], NN)

            _, _, acc = lax.while_loop(_sweep_on, step, (jd - (NEAR - 1), c, acc))
            o_ref[s * tb:(s + 1) * tb, :] = acc.astype(o_ref.dtype)

    return _pcall(body, name=name, out_shape=jax.ShapeDtypeStruct((t, d), BF16), grid=(nh, nq),
                  in_specs=[pl.BlockSpec((tq, HEAD_DIM), lambda h, i: (i, h)),
                            pl.BlockSpec((t, HEAD_DIM), lambda h, i: (0, nh + h)),
                            pl.BlockSpec((t, HEAD_DIM), lambda h, i: (0, 2 * nh + h))],
                  out_specs=pl.BlockSpec((tq, HEAD_DIM), lambda h, i: (i, h)),
                  compiler_params=_params(("parallel", "arbitrary")))(qkv, qkv, qkv)


def attn_bwd(name, qkv, do, tq):
    t, d3 = qkv.shape
    d = d3 // 3
    nh = d // HEAD_DIM
    nq = t // tq
    tb = KEY_BLOCK
    nsub = tq // tb

    def body(q_ref, k_ref, v_ref, do_ref, dq_ref, dk_ref, dv_ref, dk_acc, dv_acc, g_buf, z_buf):
        i = pl.program_id(1)

        @pl.when(i == 0)
        def _():
            dk_acc[...] = jnp.zeros_like(dk_acc)
            dv_acc[...] = jnp.zeros_like(dv_acc)

        row = lax.broadcasted_iota(jnp.int32, (tb, tb), 0)
        col = lax.broadcasted_iota(jnp.int32, (tb, tb), 1)
        below = col < row
        upper = (row >= col).astype(BF16)
        lower = (row <= col).astype(BF16)

        qs = [q_ref[s * tb:(s + 1) * tb, :] for s in range(nsub)]
        dos = [do_ref[s * tb:(s + 1) * tb, :] for s in range(nsub)]
        jds = [i * nsub + s for s in range(nsub)]
        near, carries = _sb_near(qs, jds, k_ref, below, upper, KEY_BLOCK)
        da = [[_dot(dos[s][:a.shape[0]], v_ref[rows, :], NT) for rows, _, a in near[s]] for s in range(nsub)]
        state = []
        for s in range(nsub):
            kept = [(rows, z, da[s][b] * a) for b, (rows, z, a) in enumerate(near[s])]
            for rows, _, a in near[s]:
                dv_acc[rows, :] += _dot(a.astype(BF16), dos[s][:a.shape[0]], TN)
            state.append((qs[s], dos[s], jds[s], carries[s], kept))

        carried = []
        for s, (q, dov, jd, c, kept) in enumerate(state):
            def step(st, s=s, q=q, dov=dov, jd=jd):
                j = st[0]
                rows = _block_rows(j, tb)
                z, a, c2 = _sb_far(q, k_ref[rows, :], upper, st[1], _took_it(j, jd, KEY_BLOCK))
                g_buf[jd - j] = _dot(dov, v_ref[rows, :], NT) * a
                z_buf[jd - j] = z
                dv_acc[rows, :] += _dot(a.astype(BF16), dov, TN)
                return j - 1, c2

            j_stop, _ = lax.while_loop(_sweep_on, step, (jd - (NEAR - 1), c))

            def far(j, st, s=s, q=q, jd=jd):
                run, dq = st
                rows = _block_rows(j, tb)
                g = g_buf[jd - j]
                dz = (g - jax.nn.sigmoid(z_buf[jd - j]) * (run + _tri_sum(g, lower))).astype(BF16)
                dk_acc[rows, :] += _dot(dz, q, TN)
                return run + jnp.sum(g, axis=1, keepdims=True), dq + _dot(dz, k_ref[rows, :], NN)

            carried.append(lax.fori_loop(j_stop + 1, jd - (NEAR - 1) + 1, far,
                                         (jnp.zeros((tb, 1), F32), jnp.zeros((tb, HEAD_DIM), F32))))

        tri = [[_dot(g.astype(BF16), lower, NN) for _, _, g in st[4]] for st in state]
        sig = [[jax.nn.sigmoid(z) for _, z, _ in st[4]] for st in state]
        for s, (q, dov, jd, c, kept) in enumerate(state):
            run, dq = carried[s]
            for b in reversed(range(NEAR)):
                rows, z, g = kept[b]
                n = g.shape[0]
                dz = g - sig[s][b] * (run[:n] + tri[s][b])
                if b == 0:
                    dz = jnp.where(below, dz, 0.0)
                dz = dz.astype(BF16)
                dk_acc[rows, :] += _dot(dz, q[:n], TN)
                dq = dq + _pad_block(_dot(dz, k_ref[rows, :], NN))
                if b:
                    run = run + _pad_block(jnp.sum(g, axis=1, keepdims=True))
            dq_ref[s * tb:(s + 1) * tb, :] = (dq * SCALE).astype(BF16)

        @pl.when(i == nq - 1)
        def _():
            dk_ref[...] = (dk_acc[...] * SCALE).astype(BF16)
            dv_ref[...] = dv_acc[...].astype(BF16)

    blk = pl.BlockSpec((tq, HEAD_DIM), lambda h, i: (i, h))
    col_h = pl.BlockSpec((t, HEAD_DIM), lambda h, i: (0, h))
    out = jax.ShapeDtypeStruct((t, d), BF16)
    return _pcall(body, name=name, out_shape=(out, out, out), grid=(nh, nq),
                  in_specs=[blk,
                            pl.BlockSpec((t, HEAD_DIM), lambda h, i: (0, nh + h)),
                            pl.BlockSpec((t, HEAD_DIM), lambda h, i: (0, 2 * nh + h)),
                            blk],
                  out_specs=(blk, col_h, col_h),
                  scratch_shapes=[pltpu.VMEM((t, HEAD_DIM), F32), pltpu.VMEM((t, HEAD_DIM), F32),
                                  pltpu.VMEM((t // tb, tb, tb), F32), pltpu.VMEM((t // tb, tb, tb), F32)],
                  compiler_params=_params(("parallel", "arbitrary")))(qkv, qkv, qkv, do)


def _place():
    x, y, c = lax.axis_index("x"), lax.axis_index("y"), lax.axis_index("c")
    chips = [(1 - x, y), (x, 1 - y), (1 - x, 1 - y)]
    return x, y, c, chips


def _remote(src, dst, send_sem, recv_sem, dev):
    return pltpu.make_async_remote_copy(src_ref=src, dst_ref=dst, send_sem=send_sem, recv_sem=recv_sem,
                                        device_id=dev, device_id_type=MESH)


def place_shards(name, ws, chip):
    tiles, steps = _job_tiles([w.shape for w in ws], 1 << 20, BF16_ROWS)
    nj = len(ws)

    def body(chip_ref, *refs):
        i = pl.program_id(0)
        for k, (_, n) in enumerate(tiles):
            @pl.when(i < n)
            def _(w_ref=refs[k], o_ref=refs[nj + k]):
                o_ref[...] = w_ref[...].astype(BF16)

    spec = pltpu.PrefetchScalarGridSpec(
        num_scalar_prefetch=1, grid=(steps,),
        in_specs=[pl.BlockSpec((tr, w.shape[1]), lambda i, s, n=n: (jnp.minimum(i, n - 1), 0))
                  for w, (tr, n) in zip(ws, tiles)],
        out_specs=[pl.BlockSpec((None, tr, w.shape[1]), lambda i, s, n=n: (s[0], jnp.minimum(i, n - 1), 0))
                   for w, (tr, n) in zip(ws, tiles)])
    return _pcall(body, name=name, out_shape=[jax.ShapeDtypeStruct((N_CHIPS,) + w.shape, BF16) for w in ws],
                  grid_spec=spec, compiler_params=_params(("arbitrary",)))(chip, *ws)


class Comm:
    def __init__(self, ins, outs, aliases, sems, first, mid, last):
        self.ins, self.outs, self.aliases, self.sems = list(ins), list(outs), dict(aliases), list(sems)
        self.first, self.mid, self.last = first, mid, last


def run_comm(name, comm):
    ni, no = len(comm.ins), len(comm.outs)

    def body(*refs):
        ins, outs, sems = refs[:ni], refs[ni:ni + no], refs[ni + no:]
        comm.first(ins, outs, sems)
        comm.mid(ins, outs, sems)
        comm.last(ins, outs, sems)

    return _pcall(body, name=name, out_shape=comm.outs, in_specs=[ANY] * ni, out_specs=[ANY] * no,
                  input_output_aliases=comm.aliases, scratch_shapes=comm.sems, compiler_params=_params())(*comm.ins)


def gather_comm(bufs):
    n = len(bufs)

    def half(out, w, which):
        pr = out[w].shape[1] // 2
        return pl.ds(pl.multiple_of(which * pr, BF16_ROWS), pr)

    def first(ins, out, sems):
        isend, irecv, _, _ = sems
        x, y, c, chips = _place()
        for w in range(n):
            mine = out[w].at[2 * x + y, half(out, w, c)]
            for j, (cx, cy) in enumerate(chips):
                _remote(mine, mine, isend.at[3 * w + j], irecv.at[3 * w + j], (cx, cy, c)).start()

    def mid(ins, out, sems):
        isend, irecv, dsend, drecv = sems
        x, y, c, chips = _place()
        sib = (x, y, 1 - c)
        for w in range(n):
            for j, (cx, cy) in enumerate(chips):
                landed = out[w].at[2 * cx + cy, half(out, w, c)]
                _remote(landed, landed, isend.at[3 * w + j], irecv.at[3 * w + j], sib).wait_recv()
                _remote(landed, landed, dsend.at[3 * w + j], drecv.at[3 * w + j], sib).start()

    def last(ins, out, sems):
        isend, irecv, dsend, drecv = sems
        x, y, c, chips = _place()
        sib = (x, y, 1 - c)
        for w in range(n):
            for j, (cx, cy) in enumerate(chips):
                landed = out[w].at[2 * cx + cy, half(out, w, 1 - c)]
                _remote(landed, landed, dsend.at[3 * w + j], drecv.at[3 * w + j], sib).wait_recv()
        for w in range(n):
            sent = out[w].at[0, half(out, w, c)]
            for j in range(3):
                _remote(sent, sent, isend.at[3 * w + j], irecv.at[3 * w + j], sib).wait_send()
                _remote(sent, sent, dsend.at[3 * w + j], drecv.at[3 * w + j], sib).wait_send()

    return Comm(bufs, [jax.ShapeDtypeStruct(s.shape, s.dtype) for s in bufs], {w: w for w in range(n)},
                [pltpu.SemaphoreType.DMA((3 * n,))] * 4, first, mid, last)


def _nothing(ins, outs, sems):
    return None


def join_comms(a, b):
    ni, no, ns = len(a.ins), len(a.outs), len(a.sems)

    def both(f, g):
        def hook(ins, outs, sems):
            f(ins[:ni], outs[:no], sems[:ns])
            g(ins[ni:], outs[no:], sems[ns:])
        return hook

    aliases = dict(a.aliases)
    aliases.update({ni + k: no + v for k, v in b.aliases.items()})
    return Comm(a.ins + b.ins, a.outs + b.outs, aliases, a.sems + b.sems,
                both(a.first, b.first), both(a.mid, b.mid), both(a.last, b.last))


def exchange_comm(pieces):
    n = len(pieces)

    def copies(src, out, sems):
        x, y, c, _ = _place()
        return [_remote(src[w].at[k, 1 - c], out[w].at[k], sems[0].at[N_CHIPS * w + k], sems[1].at[N_CHIPS * w + k],
                        (x, y, 1 - c)) for w in range(n) for k in range(N_CHIPS)]

    def first(src, out, sems):
        for cp in copies(src, out, sems):
            cp.start()

    def last(src, out, sems):
        for cp in copies(src, out, sems):
            cp.wait()

    return Comm(pieces, [jax.ShapeDtypeStruct((N_CHIPS,) + s.shape[2:], s.dtype) for s in pieces], {},
                [pltpu.SemaphoreType.DMA((N_CHIPS * n,))] * 2, first, _nothing, last)


def scatter_comm(parts):
    n = len(parts)

    def copies(src, out, sems):
        x, y, c, chips = _place()
        return [_remote(src[w].at[2 * cx + cy], out[w].at[j], sems[0].at[3 * w + j], sems[1].at[3 * w + j], (cx, cy, c))
                for w in range(n) for j, (cx, cy) in enumerate(chips)]

    def first(src, out, sems):
        for cp in copies(src, out, sems):
            cp.start()

    def last(src, out, sems):
        for cp in copies(src, out, sems):
            cp.wait()

    return Comm(parts, [jax.ShapeDtypeStruct((3,) + s.shape[1:], s.dtype) for s in parts], {},
                [pltpu.SemaphoreType.DMA((3 * n,))] * 2, first, _nothing, last)


def share_comm(halves):
    n = len(halves)

    def first(ins, buf, sems):
        x, y, c, _ = _place()
        for w in range(n):
            _remote(buf[w].at[c], buf[w].at[c], sems[0].at[w], sems[1].at[w], (x, y, 1 - c)).start()

    def last(ins, buf, sems):
        x, y, c, _ = _place()
        for w in range(n):
            landed = buf[w].at[1 - c]
            _remote(landed, landed, sems[0].at[w], sems[1].at[w], (x, y, 1 - c)).wait_recv()
        for w in range(n):
            _remote(buf[w].at[c], buf[w].at[c], sems[0].at[w], sems[1].at[w], (x, y, 1 - c)).wait_send()

    return Comm(halves, [jax.ShapeDtypeStruct(s.shape, s.dtype) for s in halves], {w: w for w in range(n)},
                [pltpu.SemaphoreType.DMA((n,))] * 2, first, _nothing, last)


def gather_small(name, blk, reduce):
    r, cdim = blk.shape

    def body(in_ref, out_ref, *rest):
        if reduce:
            buf, send_sem, recv_sem = rest
        else:
            buf = out_ref
            send_sem, recv_sem = rest
        x, y, c, _ = _place()
        me = 4 * x + 2 * y + c
        buf[me] = in_ref[...]
        peers = []
        for dx in range(2):
            for dy in range(2):
                for dc in range(2):
                    if dx or dy or dc:
                        peers.append((dx, dy, dc))
        copies = []
        for s, (dx, dy, dc) in enumerate(peers):
            cp = _remote(in_ref, buf.at[me], send_sem.at[s], recv_sem.at[s],
                         ((1 - x if dx else x), (1 - y if dy else y), (1 - c if dc else c)))
            cp.start()
            copies.append(cp)
        for s, (dx, dy, dc) in enumerate(peers):
            px, py, pc_ = (1 - x if dx else x), (1 - y if dy else y), (1 - c if dc else c)
            landed = buf.at[4 * px + 2 * py + pc_]
            _remote(landed, landed, send_sem.at[s], recv_sem.at[s], (x, y, c)).wait_recv()
        for cp in copies:
            cp.wait_send()
        if reduce:
            tot = buf[0]
            for s in range(1, N_DEV):
                tot = tot + buf[s]
            out_ref[...] = tot

    vm = pl.BlockSpec(memory_space=pltpu.VMEM)
    out_shape = jax.ShapeDtypeStruct((r, cdim) if reduce else (N_DEV, r, cdim), F32)
    scratch = ([pltpu.VMEM((N_DEV, r, cdim), F32)] if reduce else []) + [pltpu.SemaphoreType.DMA((N_DEV - 1,))] * 2
    return _pcall(body, name=name, out_shape=out_shape, in_specs=[vm], out_specs=vm, scratch_shapes=scratch,
                  compiler_params=_params())(blk)


def _job_tiles(shapes, tile_bytes, mult):
    tiles = []
    for rows, cols in shapes:
        tr = _tile(rows, max(mult, tile_bytes // (4 * cols)), mult)
        tiles.append((tr, rows // tr))
    return tiles, max(n for _, n in tiles)


def sum_cores(name, owns, gots, place):
    nj = len(owns)
    tiles, _ = _job_tiles([o.shape[2:] for o in owns], 1 << 21, BF16_ROWS)
    steps = max(N_CHIPS * n for _, n in tiles)

    def body(place_ref, *refs):
        i = pl.program_id(0)
        for k, (_, n) in enumerate(tiles):
            @pl.when(i < N_CHIPS * n)
            def _(own_ref=refs[2 * k], got_ref=refs[2 * k + 1], o_ref=refs[2 * nj + k]):
                o_ref[...] = (own_ref[...].astype(F32) + got_ref[...].astype(F32)).astype(o_ref.dtype)

    in_specs, out_specs, out_shape, args = [], [], [], []
    for own, got, (tr, n) in zip(owns, gots, tiles):
        pc = own.shape[3]
        last = N_CHIPS * n - 1
        in_specs += [pl.BlockSpec((None, None, tr, pc),
                                  lambda i, s, n=n, last=last: (jnp.minimum(i, last) // n, s[1], jnp.minimum(i, last) % n, 0)),
                     pl.BlockSpec((None, tr, pc),
                                  lambda i, s, n=n, last=last: (jnp.minimum(i, last) // n, jnp.minimum(i, last) % n, 0))]
        out_specs.append(pl.BlockSpec((None, tr, pc),
                                      lambda i, s, n=n, last=last: (jnp.minimum(i, last) // n, jnp.minimum(i, last) % n, 0)))
        out_shape.append(jax.ShapeDtypeStruct(got.shape, BF16))
        args += [own, got]
    spec = pltpu.PrefetchScalarGridSpec(num_scalar_prefetch=1, grid=(steps,), in_specs=in_specs, out_specs=out_specs)
    return _pcall(body, name=name, out_shape=out_shape, grid_spec=spec,
                  compiler_params=_params(("arbitrary",)))(place, *args)


def sum_chips(name, parts, gots, place):
    nj = len(parts)
    tiles, steps = _job_tiles([p.shape[1:] for p in parts], 1 << 20, BF16_ROWS)

    def body(place_ref, *refs):
        i = pl.program_id(0)
        for k, (_, n) in enumerate(tiles):
            @pl.when(i < n)
            def _(part_ref=refs[2 * k], got_ref=refs[2 * k + 1], o_ref=refs[2 * nj + k]):
                tot = part_ref[...].astype(F32)
                for j in range(3):
                    tot = tot + got_ref[j].astype(F32)
                o_ref[...] = tot

    in_specs, out_specs, out_shape, args = [], [], [], []
    for part, got, (tr, n) in zip(parts, gots, tiles):
        pc = part.shape[2]
        in_specs += [pl.BlockSpec((None, tr, pc), lambda i, s, n=n: (s[0], jnp.minimum(i, n - 1), 0)),
                     pl.BlockSpec((3, tr, pc), lambda i, s, n=n: (0, jnp.minimum(i, n - 1), 0))]
        out_specs.append(pl.BlockSpec((None, tr, pc), lambda i, s, n=n: (s[1], jnp.minimum(i, n - 1), 0)))
        out_shape.append(jax.ShapeDtypeStruct((2,) + part.shape[1:], F32))
        args += [part, got]
    spec = pltpu.PrefetchScalarGridSpec(num_scalar_prefetch=1, grid=(steps,), in_specs=in_specs, out_specs=out_specs)
    return _pcall(body, name=name, out_shape=out_shape, grid_spec=spec,
                  compiler_params=_params(("arbitrary",)))(place, *args)


def adamw(name, jobs):
    c1 = 1.0 / (1.0 - ADAM_B1 ** ADAM_STEP)
    c2 = 1.0 / (1.0 - ADAM_B2 ** ADAM_STEP)
    nj = len(jobs)
    tiles, steps = _job_tiles([j[0].shape for j in jobs], 1 << 18, 8)

    def body(*refs):
        i = pl.program_id(0)
        for k, (_, n) in enumerate(tiles):
            w_ref, g_ref, m_ref, v_ref = refs[4 * k:4 * k + 4]
            d_ref, nm_ref, nv_ref = refs[4 * nj + 3 * k:4 * nj + 3 * k + 3]

            @pl.when(i < n)
            def _(w_ref=w_ref, g_ref=g_ref, m_ref=m_ref, v_ref=v_ref, d_ref=d_ref, nm_ref=nm_ref, nv_ref=nv_ref):
                gv = g_ref[...]
                nm = ADAM_B1 * m_ref[...] + (1.0 - ADAM_B1) * gv
                nv = ADAM_B2 * v_ref[...] + (1.0 - ADAM_B2) * (gv * gv)
                nm_ref[...] = nm
                nv_ref[...] = nv
                d_ref[...] = -ADAM_LR * ((nm * c1) / (jnp.sqrt(nv * c2) + ADAM_EPS) + ADAM_WD * w_ref[...])

    in_specs, out_specs, out_shape, args = [], [], [], []
    for (w, g, m, v), (tr, n) in zip(jobs, tiles):
        spec = pl.BlockSpec((tr, w.shape[1]), lambda i, n=n: (jnp.minimum(i, n - 1), 0))
        in_specs += [spec] * 4
        out_specs += [spec] * 3
        out_shape += [jax.ShapeDtypeStruct(w.shape, F32)] * 3
        args += [w, g, m, v]
    res = _pcall(body, name=name, out_shape=out_shape, grid=(steps,), in_specs=in_specs, out_specs=out_specs,
                 compiler_params=_params(("arbitrary",)))(*args)
    return [tuple(res[3 * k:3 * k + 3]) for k in range(nj)]


MATS = ["ffn1_w_in", "ffn1_w_out", "w_mix_in", "w_conv_out", "w_attn_out", "w_mix_out", "ffn2_w_in", "ffn2_w_out",
        "w_ple_gate", "w_ple_proj"]
COL_SHARDED = {"ffn1_w_in", "w_mix_in", "ffn2_w_in", "w_ple_proj"}
NORMS = ["ffn1_norm", "mix_norm", "ffn2_norm", "ple_norm", "final_norm"]
WEIGHTS = ["ffn1_norm", "ffn1_w_in", "ffn1_w_out", "mix_norm", "w_mix_in", "conv_w", "w_conv_out", "w_attn_out",
           "w_mix_out", "ffn2_norm", "ffn2_w_in", "ffn2_w_out", "ple_norm", "w_ple_gate", "w_ple_proj", "final_norm"]


def _pad_rows(a, rows):
    return jnp.concatenate([a, jnp.zeros((rows - a.shape[0],) + a.shape[1:], a.dtype)], axis=0)


def _step(x, p, tgt, w, m, v):
    t, d = x.shape
    tt = _tile(t, 256)
    tm = _tile(t, 512)
    tm2 = _tile(t, 1024)
    tq = _tile(t, 1024)

    chip = 2 * lax.axis_index("x") + lax.axis_index("y")
    place = jnp.stack([chip, lax.axis_index("c")]).astype(jnp.int32)

    placed = dict(zip(MATS, place_shards("place_shards", [w[k] for k in MATS], place)))
    full = {}

    def keep(names, bufs):
        for k, buf in zip(names, bufs):
            full[k] = buf if k in COL_SHARDED else buf.reshape(-1, buf.shape[2])

    def gather_of(names):
        return gather_comm([placed[k] for k in names])

    cw_all = gather_small("gather_conv_w", _pad_rows(w["conv_w"], 8), False)
    cw8 = jnp.concatenate([cw_all[2 * k] for k in range(N_CHIPS)], axis=1)
    g1, gm, g2, gp, gf = (w[k].reshape(1, d) for k in NORMS)

    def ffn_fwd(tag, h, g, first, w_in_name, w_out_name, riders):
        if first:
            n, bufs = rms_fwd(tag + "_norm", h, g, tt, comm=gather_of(first))
            keep(first, bufs)
            (a, s), bufs = ffn_in_act(tag + "_in", n, full[w_in_name], tm, comm=gather_of(riders))
            keep(riders, bufs)
        else:
            a, s, n = ffn_in_act(tag + "_in", h, full[w_in_name], tm, gain=g)
        return n, a, s, mm_nn(tag + "_out", s, full[w_out_name], F32, tm, res=h, alpha=0.5)

    n1, a1, s1, h1 = ffn_fwd("ffn1", x, g1, ["ffn1_w_in"], "ffn1_w_in", "ffn1_w_out", ["ffn1_w_out", "w_mix_in"])
    u = rms_fwd("mix_norm", h1, gm, tt)
    wmix = full["w_mix_in"]
    riders = [["w_conv_out", "w_attn_out", "w_mix_out"], ["ffn2_w_in"], ["ffn2_w_out", "w_ple_gate", "w_ple_proj"]]
    cbx, bufs = mm_nn_stacked("mix_in_conv", u, wmix, BF16, tm2, d, 0, 3, comm=gather_of(riders[0]))
    keep(riders[0], bufs)
    qkv, bufs = mm_nn_stacked("mix_in_qkv", u, wmix, BF16, tm2, d, 3, 3, comm=gather_of(riders[1]))
    keep(riders[1], bufs)
    gates, bufs = mm_nn_stacked("mix_in_gates", u, wmix, BF16, tm2, d, 6, 2, comm=gather_of(riders[2]))
    keep(riders[2], bufs)
    wpp = full["w_ple_proj"]
    wpp = jnp.transpose(wpp, (1, 0, 2)).reshape(wpp.shape[1], -1)
    ycin, y_conv = conv_out_fwd("conv_out", cbx, cw8, full["w_conv_out"], tt)
    o = attn_fwd("attn", qkv, tq)
    y_attn = mm_nn("attn_out", o, full["w_attn_out"], BF16, tm)
    merged, h2 = mix_out_fwd("mix_out", gates, y_conv, y_attn, h1, full["w_mix_out"], tm)
    n2, a2, s2, h3 = ffn_fwd("ffn2", h2, g2, [], "ffn2_w_in", "ffn2_w_out", [])

    pieces, chip_sums, halves = {}, {}, {}

    def as_pieces(k):
        pc = pieces[k]
        return pc if k in COL_SHARDED else pc.reshape(N_CHIPS, 2, pc.shape[0] // (2 * N_CHIPS), pc.shape[1])

    def sum_siblings(tag, names):
        pcs = [as_pieces(k) for k in names]
        got = run_comm("exchange_" + tag, exchange_comm(pcs))
        chip_sums.update(zip(names, sum_cores("sum_cores_" + tag, pcs, got, place)))

    def scatter_of(names):
        return scatter_comm([chip_sums[k] for k in names])

    def sum_landed(tag, names, landed):
        halves.update(zip(names, sum_chips("sum_chips_" + tag, [chip_sums[k] for k in names], landed, place)))

    npl, dh4, dpp, dzg, dgf, loss_row = tail("tail", h3, p, tgt, gp, gf, full["w_ple_gate"], wpp, tt)
    dwpp = mm_tn_whole("ple_proj_dw", p, dpp, tm2)
    pieces["w_ple_proj"] = jnp.transpose(dwpp.reshape(2, p.shape[1] // 2, N_CHIPS, d // N_CHIPS), (2, 0, 1, 3))
    pieces["w_ple_gate"] = mm_tn_rows("ple_gate_dw", npl, dzg, tm2)
    dh3, df2, dgp = mm_nt("ple_gate_dx", dzg, full["w_ple_gate"], F32, tm, d, norm=(h3, gp, dh4), alpha=0.5)
    w_in, w_out = full["ffn2_w_in"], full["ffn2_w_out"]
    pieces["ffn2_w_out"] = mm_tn_rows("ffn2_dwout", s2, df2, tm2)
    da2 = ffn_ds_dact("ffn2_ds", df2, w_out, a2, tm2)
    pieces["ffn2_w_in"] = mm_tn_cols("ffn2_dwin", n2, da2, tm2)
    dh2, dh2b, dg2 = mm_nt_stacked("ffn2_dn", da2, w_in, F32, tm2, w_in.shape[2], norm=(h2, g2, dh3))
    pieces["w_mix_out"] = mm_tn_rows("mix_out_dw", merged, dh2b, tm2)
    dyc, dya, dgates = mix_out_bwd("mix_out_dx", dh2b, full["w_mix_out"], gates, y_conv, y_attn, tm)
    pieces["w_conv_out"] = mm_tn_rows("conv_out_dw", ycin, dyc, tm2)
    dcbx, dcw8 = conv_out_bwd("conv_out_dx", dyc, full["w_conv_out"], cbx, cw8, tt)
    pieces["w_attn_out"] = mm_tn_rows("attn_out_dw", o, dya, tm2)
    do = mm_nt("attn_out_dx", dya, full["w_attn_out"], BF16, tm, d)
    dq, dk, dv = attn_bwd("attn_bwd", qkv, do, tq)
    dmix = [dcbx, dq, dk, dv, dgates]
    early = ["ffn2_w_in", "ffn2_w_out", "w_ple_gate", "w_ple_proj", "w_mix_out", "w_conv_out", "w_attn_out"]
    swap = exchange_comm([as_pieces(k) for k in early])
    pieces["w_mix_in"], got = mm_tn_parts("mix_in_dw", u, dmix, tm2, comm=swap)
    chip_sums.update(zip(early, sum_cores("sum_cores_early", swap.ins, got, place)))
    swap = exchange_comm([as_pieces("w_mix_in")])
    (dh1, df1, dgm), landed = mm_nt_parts("mix_in_dx", dmix, wmix, tm, (h1, gm, dh2), 0.5,
                                          comm=join_comms(scatter_of(early), swap))
    sum_landed("early", early, landed[:len(early)])
    chip_sums["w_mix_in"] = sum_cores("sum_cores_mix", swap.ins, landed[len(early):], place)[0]
    w_in, w_out = full["ffn1_w_in"], full["ffn1_w_out"]
    pieces["ffn1_w_out"] = mm_tn_rows("ffn1_dwout", s1, df1, tm2)
    da1 = ffn_ds_dact("ffn1_ds", df1, w_out, a1, tm2)
    pieces["ffn1_w_in"], landed = mm_tn_cols("ffn1_dwin", n1, da1, tm2, comm=scatter_of(["w_mix_in"]))
    sum_landed("mix", ["w_mix_in"], landed)
    late = ["ffn1_w_in", "ffn1_w_out"]
    sum_siblings("late", late)
    done = early + ["w_mix_in"]
    (dx, _, dg1), landed = mm_nt_stacked(
        "ffn1_dn", da1, w_in, F32, tm2, w_in.shape[2], norm=(x, g1, dh1),
        comm=join_comms(scatter_of(late), share_comm([halves[k] for k in done])))
    sum_landed("late", late, landed[:len(late)])
    shared = dict(zip(done, landed[len(late):]))

    shared.update(zip(late, run_comm("share_halves", share_comm([halves[k] for k in late]))))
    grad, delta, new_m, new_v = {}, {}, {}, {}
    for k in MATS:
        grad[k] = shared[k].reshape(w[k].shape)

    small = jnp.concatenate([dg1, dgm, dg2, dgp, dgf, dcw8[:3], loss_row, jnp.zeros((7, d), F32)], axis=0)
    tot = gather_small("sum_small", small, True)
    loss = tot[8, 0]
    norm_w = jnp.concatenate([w[k].reshape(1, d) for k in NORMS] + [jnp.zeros((3, d), F32)], axis=0)
    norm_m = jnp.concatenate([m[k].reshape(1, d) for k in NORMS] + [jnp.zeros((3, d), F32)], axis=0)
    norm_v = jnp.concatenate([v[k].reshape(1, d) for k in NORMS] + [jnp.ones((3, d), F32)], axis=0)
    norm_g = jnp.concatenate([tot[0:5], jnp.zeros((3, d), F32)], axis=0)
    cs = d // N_CHIPS
    gcw = lax.dynamic_slice(tot[5:8], (0, chip * cs), (3, cs))
    conv_job = (_pad_rows(w["conv_w"], 8), _pad_rows(gcw, 8), _pad_rows(m["conv_w"], 8),
                jnp.concatenate([v["conv_w"], jnp.ones((5, cs), F32)], axis=0))

    steps = adamw("adamw", [(w[k], grad[k], m[k], v[k]) for k in MATS]
                  + [(norm_w, norm_g, norm_m, norm_v), conv_job])
    for k, res in zip(MATS, steps):
        delta[k], new_m[k], new_v[k] = res
    nd, nm, nv = steps[len(MATS)]
    for r, k in enumerate(NORMS):
        grad[k] = norm_g[r].reshape(w[k].shape)
        delta[k], new_m[k], new_v[k] = (a[r].reshape(w[k].shape) for a in (nd, nm, nv))
    cd, cm, cv = steps[len(MATS) + 1]
    grad["conv_w"], delta["conv_w"], new_m["conv_w"], new_v["conv_w"] = gcw, cd[:3], cm[:3], cv[:3]
    return loss, dx, grad, delta, new_m, new_v


def kernel(x, p, ffn1_norm, ffn1_w_in, ffn1_w_out, mix_norm, w_mix_in, conv_w, w_conv_out, w_attn_out, w_mix_out, ffn2_norm, ffn2_w_in, ffn2_w_out, ple_norm, w_ple_gate, w_ple_proj, final_norm, loss_target, m_ffn1_norm, m_ffn1_w_in, m_ffn1_w_out, m_mix_norm, m_w_mix_in, m_conv_w, m_w_conv_out, m_w_attn_out, m_w_mix_out, m_ffn2_norm, m_ffn2_w_in, m_ffn2_w_out, m_ple_norm, m_w_ple_gate, m_w_ple_proj, m_final_norm, v_ffn1_norm, v_ffn1_w_in, v_ffn1_w_out, v_mix_norm, v_w_mix_in, v_conv_w, v_w_conv_out, v_w_attn_out, v_w_mix_out, v_ffn2_norm, v_ffn2_w_in, v_ffn2_w_out, v_ple_norm, v_w_ple_gate, v_w_ple_proj, v_final_norm):
    ws = (ffn1_norm, ffn1_w_in, ffn1_w_out, mix_norm, w_mix_in, conv_w, w_conv_out, w_attn_out, w_mix_out, ffn2_norm,
          ffn2_w_in, ffn2_w_out, ple_norm, w_ple_gate, w_ple_proj, final_norm)
    ms = (m_ffn1_norm, m_ffn1_w_in, m_ffn1_w_out, m_mix_norm, m_w_mix_in, m_conv_w, m_w_conv_out, m_w_attn_out,
          m_w_mix_out, m_ffn2_norm, m_ffn2_w_in, m_ffn2_w_out, m_ple_norm, m_w_ple_gate, m_w_ple_proj, m_final_norm)
    vs = (v_ffn1_norm, v_ffn1_w_in, v_ffn1_w_out, v_mix_norm, v_w_mix_in, v_conv_w, v_w_conv_out, v_w_attn_out,
          v_w_mix_out, v_ffn2_norm, v_ffn2_w_in, v_ffn2_w_out, v_ple_norm, v_w_ple_gate, v_w_ple_proj, v_final_norm)
    assert x.shape[0] == 1 and p.shape[:2] == (1, 1), "one sequence and one layer per device"

    def strip(a):
        return a[0] if a.ndim == 3 or (a.ndim == 2 and a.shape[0] == 1) else a

    w = {k: strip(a) for k, a in zip(WEIGHTS, ws)}
    m = {k: strip(a) for k, a in zip(WEIGHTS, ms)}
    v = {k: strip(a) for k, a in zip(WEIGHTS, vs)}
    loss, dx, grad, delta, new_m, new_v = _step(x[0], p[0, 0], loss_target[0], w, m, v)
    shapes = [a.shape for a in ws]
    outs = [loss, dx[None]]
    for res in (grad, delta, new_m, new_v):
        outs += [res[k].reshape(s) for k, s in zip(WEIGHTS, shapes)]
    return tuple(outs)
```

```python
import functools
import math

import jax
import jax.numpy as jnp
from jax import lax
from jax.experimental import pallas as pl
from jax.experimental.pallas import tpu as pltpu

F32 = jnp.float32
BF16 = jnp.bfloat16
MESH = pl.DeviceIdType.MESH
ANY = pl.BlockSpec(memory_space=pl.ANY)

HEAD_DIM = 128
NORM_EPS = 1e-6
N_CHIPS = 4
N_DEV = 8
BF16_ROWS = 16
VMEM_LIMIT = 56 * 1024 * 1024
ACC_BYTES = 8 * 1024 * 1024
STICK_EXIT = 110.0

ADAM_LR = 0.001
ADAM_B1 = 0.9
ADAM_B2 = 0.999
ADAM_EPS = 1e-08
ADAM_WD = 0.01
ADAM_STEP = 10

NN = (((1,), (0,)), ((), ()))
NT = (((1,), (1,)), ((), ()))
TN = (((0,), (0,)), ((), ()))


def _params(sem=None, **kw):
    if sem is not None:
        kw["dimension_semantics"] = sem
    return pltpu.CompilerParams(vmem_limit_bytes=VMEM_LIMIT, **kw)


def _pcall(body, **kw):
    return pl.pallas_call(body, **kw)


def _tile(n, pref, mult=8):
    best = None
    for d in range(mult, min(n, pref) + 1, mult):
        if n % d == 0:
            best = d
    return best if best is not None else n


def _dot(a, b, dims):
    return lax.dot_general(a, b, dims, preferred_element_type=F32)


def _call(name, body, grid, in_specs, out_specs, out_shape, args, scratch=(), sem=None, comm=None):
    n_in, n_out, n_sc = len(in_specs), len(out_specs), len(scratch)
    if comm is None:
        def plain(*refs):
            body(refs[:n_in], refs[n_in:n_in + n_out], refs[n_in + n_out:])

        return _pcall(plain, name=name, out_shape=list(out_shape), grid=grid, in_specs=list(in_specs),
                      out_specs=list(out_specs), scratch_shapes=list(scratch), compiler_params=_params(sem))(*args)
    n_cin, n_cout = len(comm.ins), len(comm.outs)
    steps = math.prod(grid)

    def hosted(*refs):
        ins, c_ins = refs[:n_in], refs[n_in:n_in + n_cin]
        outs = refs[n_in + n_cin:n_in + n_cin + n_out]
        c_outs = refs[n_in + n_cin + n_out:n_in + n_cin + n_out + n_cout]
        rest = refs[n_in + n_cin + n_out + n_cout:]
        sems = rest[n_sc:]
        step = pl.program_id(0)
        for ax in range(1, len(grid)):
            step = step * grid[ax] + pl.program_id(ax)

        @pl.when(step == 0)
        def _():
            comm.first(c_ins, c_outs, sems)

        body(ins, outs, rest[:n_sc])

        @pl.when(step == (3 * steps) // 4)
        def _():
            comm.mid(c_ins, c_outs, sems)

        @pl.when(step == steps - 1)
        def _():
            comm.last(c_ins, c_outs, sems)

    res = _pcall(hosted, name=name, out_shape=list(out_shape) + comm.outs, grid=grid,
                 in_specs=list(in_specs) + [ANY] * n_cin, out_specs=list(out_specs) + [ANY] * n_cout,
                 input_output_aliases={n_in + k: n_out + v for k, v in comm.aliases.items()},
                 scratch_shapes=list(scratch) + comm.sems,
                 compiler_params=_params(("arbitrary",) * len(grid)))(*args, *comm.ins)
    return list(res[:n_out]), list(res[n_out:])


NORM_CHUNK = 256


def _norm_bwd_tile(read_dn, rows, first, h_ref, g_ref, dr_ref, dh_ref, dhb_ref, dg_ref, alpha):
    @pl.when(first)
    def _():
        dg_ref[...] = jnp.zeros_like(dg_ref)

    gv = g_ref[...]
    tot = jnp.zeros_like(gv)
    for c0 in range(0, rows, NORM_CHUNK):
        sl = slice(c0, min(rows, c0 + NORM_CHUNK))
        hv = h_ref[sl, :]
        rs = _rstd(hv)
        hn = hv * rs
        dnv = read_dn(sl)
        gy = dnv * gv
        dh = dr_ref[sl, :] + rs * (gy - hn * jnp.mean(gy * hn, axis=-1, keepdims=True))
        dh_ref[sl, :] = dh
        dhb_ref[sl, :] = (alpha * dh).astype(BF16)
        tot = tot + jnp.sum(dnv * hn, axis=0, keepdims=True)
    dg_ref[...] += tot


def _mm(name, a, b, out_sds, grid, a_spec, b_spec, o_spec, dims, acc_shape, res=None, alpha=1.0, comm=None,
        norm=None, gain=None):
    nk = grid[2]

    def body(ins, outs, scratch):
        a_ref, b_ref = ins[:2]
        r_ref = ins[2] if res is not None else None
        o_ref = outs[0]
        if gain is not None:
            n_ref = scratch[-1]

            @pl.when(jnp.logical_and(pl.program_id(1) == 0, pl.program_id(2) == 0))
            def _():
                hv = a_ref[...]
                n_ref[...] = (hv * _rstd(hv) * ins[-1][...]).astype(BF16)
                outs[-1][...] = n_ref[...]

            a_ref = n_ref

        def finish(read):
            if norm is not None:
                first = jnp.logical_and(pl.program_id(0) == 0, pl.program_id(1) == 0)
                _norm_bwd_tile(read, o_ref.shape[0], first, *ins[2:5], *outs, alpha)
                return
            r = read(slice(None))
            if alpha != 1.0:
                r = r * alpha
            if r_ref is not None:
                r = r_ref[...] + r
            if len(o_ref.shape) == 3:
                half = o_ref.shape[1]
                o_ref[0] = r[:half].astype(o_ref.dtype)
                o_ref[1] = r[half:].astype(o_ref.dtype)
            else:
                o_ref[...] = r.astype(o_ref.dtype)

        if nk == 1:
            part = _dot(a_ref[...].astype(BF16), b_ref[...].astype(BF16), dims)
            finish(lambda sl: part[sl])
        else:
            acc_ref = scratch[0]
            kk = pl.program_id(2)

            @pl.when(kk == 0)
            def _():
                acc_ref[...] = jnp.zeros_like(acc_ref)

            acc_ref[...] += _dot(a_ref[...].astype(BF16), b_ref[...].astype(BF16), dims)

            @pl.when(kk == nk - 1)
            def _():
                finish(lambda sl: acc_ref[sl, :])

    in_specs = [a_spec, b_spec]
    args = [a, b]
    out_specs, out_shape = [o_spec], [out_sds]
    sem = ("parallel", "parallel", "arbitrary")
    if res is not None:
        in_specs.append(o_spec)
        args.append(res)
    if norm is not None:
        width = out_sds.shape[1]
        whole = pl.BlockSpec((1, width), lambda i, j, r: (0, 0))
        in_specs += [o_spec, whole, o_spec]
        args += list(norm)
        out_specs = [o_spec, o_spec, whole]
        out_shape = [jax.ShapeDtypeStruct(out_sds.shape, F32), jax.ShapeDtypeStruct(out_sds.shape, BF16),
                     jax.ShapeDtypeStruct((1, width), F32)]
        sem = ("arbitrary", "arbitrary", "arbitrary")
    scratch = [] if nk == 1 else [pltpu.VMEM(acc_shape, F32)]
    if gain is not None:
        in_specs.append(pl.BlockSpec((1, a.shape[1]), lambda i, j, r: (0, 0)))
        args.append(gain)
        out_specs.append(a_spec)
        out_shape.append(jax.ShapeDtypeStruct(a.shape, BF16))
        scratch.append(pltpu.VMEM(a_spec.block_shape, BF16))
        sem = ("parallel", "arbitrary", "arbitrary")
    got = _call(name, body, grid, in_specs, out_specs, out_shape, args, scratch, sem, comm)
    if norm is not None or gain is not None:
        return got if comm is None else (got[0], got[1])
    return got[0] if comm is None else (got[0][0], got[1])


def ffn_in_act(name, n, w4, tm, comm=None, gain=None):
    t, d = n.shape
    cs = w4.shape[2]

    def body(ins, outs, scratch):
        wg_ref, wu_ref = ins[-2:]
        a_ref, s_ref = outs[:2]
        if gain is None:
            nv = ins[0][...]
        else:
            hv = ins[0][...]
            nv = (hv * _rstd(hv) * ins[1][...]).astype(BF16)

            @pl.when(pl.program_id(0) == 0)
            def _():
                outs[2][...] = nv
        gate = _dot(nv, wg_ref[...], NN)
        up = _dot(nv, wu_ref[...], NN)
        a_ref[0] = gate.astype(BF16)
        a_ref[1] = up.astype(BF16)
        s_ref[...] = (gate * jax.nn.sigmoid(gate) * up).astype(BF16)

    rows = pl.BlockSpec((tm, d), lambda j, i: (i, 0))
    in_specs = [rows] + ([] if gain is None else [pl.BlockSpec((1, d), lambda j, i: (0, 0))])
    in_specs += [pl.BlockSpec((None, d, cs), lambda j, i: (j, 0, 0)),
                 pl.BlockSpec((None, d, cs), lambda j, i: (2 + j, 0, 0))]
    out_specs = [pl.BlockSpec((2, tm, cs), lambda j, i: (0, i, j)), pl.BlockSpec((tm, cs), lambda j, i: (i, j))]
    out_shape = [jax.ShapeDtypeStruct((2, t, 2 * cs), BF16), jax.ShapeDtypeStruct((t, 2 * cs), BF16)]
    if gain is not None:
        out_specs.append(pl.BlockSpec((tm, d), lambda j, i: (jnp.where(j == 0, i, t // tm - 1), 0)))
        out_shape.append(jax.ShapeDtypeStruct((t, d), BF16))
    got = _call(name, body, (2, t // tm), in_specs, out_specs, out_shape,
                [n] + ([] if gain is None else [gain]) + [w4, w4], (), ("arbitrary", "arbitrary"), comm)
    return got if comm is None else (got[0], got[1])


def ffn_ds_dact(name, df, w_out, a3, tm):
    t, d = df.shape
    f = w_out.shape[0]
    cs = f // 2

    def body(ins, outs, scratch):
        df_ref, w_ref, a_ref = ins
        ds = _dot(df_ref[...], w_ref[...], NT)
        for c0 in range(0, tm, NORM_CHUNK):
            sl = slice(c0, min(tm, c0 + NORM_CHUNK))
            gate = a_ref[0, sl, :].astype(F32)
            up = a_ref[1, sl, :].astype(F32)
            sg = jax.nn.sigmoid(gate)
            outs[0][0, sl, :] = (ds[sl] * up * sg * (1.0 + gate * (1.0 - sg))).astype(BF16)
            outs[0][1, sl, :] = (ds[sl] * gate * sg).astype(BF16)

    blk = pl.BlockSpec((2, tm, cs), lambda i, j: (0, i, j))
    return _call(name, body, (t // tm, 2),
                 [pl.BlockSpec((tm, d), lambda i, j: (i, 0)), pl.BlockSpec((cs, d), lambda i, j: (j, 0)), blk],
                 [blk], [jax.ShapeDtypeStruct((2, t, f), BF16)], [df, w_out, a3], (), ("parallel", "parallel"))[0]


def _part_ranges(parts, d):
    out, lo = [], 0
    for p in parts:
        out.append((lo, p.shape[1] // d))
        lo += p.shape[1] // d
    return out, lo


def mm_nt_parts(name, parts, w4, tm, norm, alpha, comm=None):
    m = parts[0].shape[0]
    d, cs = w4.shape[1], w4.shape[2]
    per = cs // d
    ranges, nblk = _part_ranges(parts, d)
    np_ = len(parts)

    def body(ins, outs, scratch):
        w_ref, acc = ins[np_], scratch[0]
        r = pl.program_id(1)

        @pl.when(r == 0)
        def _():
            acc[...] = jnp.zeros_like(acc)

        for (lo, n), a_ref in zip(ranges, ins[:np_]):
            @pl.when(jnp.logical_and(r >= lo, r < lo + n))
            def _(a_ref=a_ref):
                acc[...] += _dot(a_ref[...], w_ref[...], NT)

        @pl.when(r == nblk - 1)
        def _():
            _norm_bwd_tile(lambda sl: acc[sl, :], tm, pl.program_id(0) == 0, *ins[np_ + 1:], *outs, alpha)

    rows = pl.BlockSpec((tm, d), lambda i, r: (i, 0))
    whole = pl.BlockSpec((1, d), lambda i, r: (0, 0))
    specs = [pl.BlockSpec((tm, d), lambda i, r, lo=lo, n=n: (i, jnp.clip(r - lo, 0, n - 1))) for lo, n in ranges]
    specs += [pl.BlockSpec((None, d, d), lambda i, r: (r // per, 0, r % per)), rows, whole, rows]
    got = _call(name, body, (m // tm, nblk), specs, [rows, rows, whole],
                [jax.ShapeDtypeStruct((m, d), F32), jax.ShapeDtypeStruct((m, d), BF16),
                 jax.ShapeDtypeStruct((1, d), F32)],
                list(parts) + [w4] + list(norm), [pltpu.VMEM((tm, d), F32)], ("arbitrary", "arbitrary"), comm)
    return got if comm is None else (got[0], got[1])


def mm_tn_parts(name, xa, parts, tt, comm=None):
    t, k = xa.shape
    d = k
    pr = k // 2
    ranges, nblk = _part_ranges(parts, d)
    per = nblk // N_CHIPS

    def body(ins, outs, scratch):
        x_ref, acc = ins[0], scratch[0]
        jb, r = pl.program_id(0), pl.program_id(1)

        @pl.when(r == 0)
        def _():
            acc[...] = jnp.zeros_like(acc)

        for (lo, n), p_ref in zip(ranges, ins[1:]):
            @pl.when(jnp.logical_and(jb >= lo, jb < lo + n))
            def _(p_ref=p_ref):
                acc[...] += _dot(x_ref[...], p_ref[...], TN)

        @pl.when(r == t // tt - 1)
        def _():
            outs[0][0] = acc[:pr].astype(BF16)
            outs[0][1] = acc[pr:].astype(BF16)

    def part_spec(lo, n):
        return pl.BlockSpec((tt, d), lambda jb, r: (jnp.where(jnp.logical_and(jb >= lo, jb < lo + n), r, 0),
                                                    jnp.clip(jb - lo, 0, n - 1)))

    specs = [pl.BlockSpec((tt, k), lambda jb, r: (r, 0))] + [part_spec(lo, n) for lo, n in ranges]
    got = _call(name, body, (nblk, t // tt), specs,
                [pl.BlockSpec((None, 2, pr, d), lambda jb, r: (jb // per, 0, 0, jb % per))],
                [jax.ShapeDtypeStruct((N_CHIPS, 2, pr, per * d), BF16)], [xa] + list(parts),
                [pltpu.VMEM((k, d), F32)], ("parallel", "arbitrary"), comm)
    return got[0] if comm is None else (got[0][0], got[1])


def mm_nn(name, a, w, out_dtype, tm, res=None, alpha=1.0):
    m, k = a.shape
    n = w.shape[1]
    return _mm(name, a, w, jax.ShapeDtypeStruct((m, n), out_dtype), (m // tm, 1, 1),
               pl.BlockSpec((tm, k), lambda i, j, r: (i, 0)),
               pl.BlockSpec((k, n), lambda i, j, r: (0, 0)),
               pl.BlockSpec((tm, n), lambda i, j, r: (i, 0)), NN, None, res=res, alpha=alpha)


def mm_nn_stacked(name, a, w4, out_dtype, tm, tn, j0=0, nj=None, comm=None, gain=None):
    m, k = a.shape
    cs = w4.shape[2]
    per = cs // tn
    nj = N_CHIPS * per - j0 if nj is None else nj
    return _mm(name, a, w4, jax.ShapeDtypeStruct((m, nj * tn), out_dtype), (m // tm, nj, 1),
               pl.BlockSpec((tm, k), lambda i, j, r: (i, 0)),
               pl.BlockSpec((None, k, tn), lambda i, j, r: ((j + j0) // per, 0, (j + j0) % per)),
               pl.BlockSpec((tm, tn), lambda i, j, r: (i, j)), NN, None, comm=comm, gain=gain)


def mm_nt(name, dy, w, out_dtype, tm, tko, norm=None, alpha=1.0):
    m, n = dy.shape
    k = w.shape[0]
    return _mm(name, dy, w, jax.ShapeDtypeStruct((m, k), out_dtype), (m // tm, k // tko, 1),
               pl.BlockSpec((tm, n), lambda i, j, r: (i, 0)),
               pl.BlockSpec((tko, n), lambda i, j, r: (j, 0)),
               pl.BlockSpec((tm, tko), lambda i, j, r: (i, j)), NT, None, norm=norm, alpha=alpha)


def mm_nt_stacked(name, dy, w4, out_dtype, tm, tn, comm=None, norm=None, alpha=1.0):
    m = dy.shape[-2]
    k, cs = w4.shape[1], w4.shape[2]
    per = cs // tn
    if dy.ndim == 3:
        dy_spec = pl.BlockSpec((None, tm, cs), lambda i, j, r: (r // 2, i, r % 2))
    else:
        dy_spec = pl.BlockSpec((tm, tn), lambda i, j, r: (i, r))
    return _mm(name, dy, w4, jax.ShapeDtypeStruct((m, k), out_dtype), (m // tm, 1, N_CHIPS * per), dy_spec,
               pl.BlockSpec((None, k, tn), lambda i, j, r: (r // per, 0, r % per)),
               pl.BlockSpec((tm, k), lambda i, j, r: (i, 0)), NT, (tm, k), comm=comm, norm=norm, alpha=alpha)


def mm_tn_rows(name, xa, dy, tt):
    t, k = xa.shape
    n = dy.shape[1]
    tkr = k if k * n * 4 <= ACC_BYTES else k // 2
    return _mm(name, xa, dy, jax.ShapeDtypeStruct((k, n), BF16), (k // tkr, 1, t // tt),
               pl.BlockSpec((tt, tkr), lambda i, j, r: (r, i)),
               pl.BlockSpec((tt, n), lambda i, j, r: (r, 0)),
               pl.BlockSpec((tkr, n), lambda i, j, r: (i, 0)), TN, (tkr, n))


def mm_tn_whole(name, xa, dy, tt):
    t, k = xa.shape
    n = dy.shape[1]
    return _mm(name, xa, dy, jax.ShapeDtypeStruct((k, n), BF16), (1, 1, t // tt),
               pl.BlockSpec((tt, k), lambda i, j, r: (r, 0)),
               pl.BlockSpec((tt, n), lambda i, j, r: (r, 0)),
               pl.BlockSpec((k, n), lambda i, j, r: (0, 0)), TN, (k, n))


def mm_tn_cols(name, xa, dy, tt, comm=None):
    t, k = xa.shape
    pr = k // 2
    if dy.ndim == 3:
        cs = dy.shape[2] // 2
        dy_spec = pl.BlockSpec((None, tt, cs), lambda i, j, r: (j // 2, r, j % 2))
    else:
        cs = dy.shape[1] // N_CHIPS
        dy_spec = pl.BlockSpec((tt, cs), lambda i, j, r: (r, j))
    return _mm(name, xa, dy, jax.ShapeDtypeStruct((N_CHIPS, 2, pr, cs), BF16), (1, N_CHIPS, t // tt),
               pl.BlockSpec((tt, k), lambda i, j, r: (r, 0)), dy_spec,
               pl.BlockSpec((None, 2, pr, cs), lambda i, j, r: (j, 0, 0, 0)), TN, (k, cs), comm=comm)


def _rows(tt, w, col=0):
    return pl.BlockSpec((tt, w), lambda i: (i, col))


def _whole(shape):
    return pl.BlockSpec(shape, lambda i: (0,) * len(shape))


def _rstd(h):
    return lax.rsqrt(jnp.mean(h * h, axis=-1, keepdims=True) + NORM_EPS)


def rms_fwd(name, h, g, tt, comm=None):
    t, d = h.shape

    def body(ins, outs, scratch):
        hv = ins[0][...]
        outs[0][...] = (hv * _rstd(hv) * ins[1][...]).astype(BF16)

    got = _call(name, body, (t // tt,), [_rows(tt, d), _whole((1, d))], [_rows(tt, d)],
                [jax.ShapeDtypeStruct((t, d), BF16)], [h, g], (), ("parallel",), comm)
    return got[0] if comm is None else (got[0][0], got[1])


def mix_out_fwd(name, gates, yc, ya, h, w, tt):
    t, d = yc.shape

    def body(g_ref, yc_ref, ya_ref, h_ref, w_ref, m_ref, o_ref):
        merged = (jax.nn.sigmoid(g_ref[:, :d].astype(F32)) * yc_ref[...].astype(F32)
                  + jax.nn.sigmoid(g_ref[:, d:].astype(F32)) * ya_ref[...].astype(F32)).astype(BF16)
        m_ref[...] = merged
        o_ref[...] = h_ref[...] + _dot(merged, w_ref[...], NN)

    return _pcall(body, name=name,
                  out_shape=(jax.ShapeDtypeStruct((t, d), BF16), jax.ShapeDtypeStruct((t, d), F32)),
                  grid=(t // tt,),
                  in_specs=[_rows(tt, 2 * d), _rows(tt, d), _rows(tt, d), _rows(tt, d), _whole((d, d))],
                  out_specs=(_rows(tt, d), _rows(tt, d)),
                  compiler_params=_params(("parallel",)))(gates, yc, ya, h, w)


def mix_out_bwd(name, dh, w, gates, yc, ya, tt):
    t, d = yc.shape

    def body(dh_ref, w_ref, g_ref, yc_ref, ya_ref, dyc_ref, dya_ref, dg_ref):
        dmv = _dot(dh_ref[...], w_ref[...], NT)
        sc = jax.nn.sigmoid(g_ref[:, :d].astype(F32))
        sa = jax.nn.sigmoid(g_ref[:, d:].astype(F32))
        dyc_ref[...] = (dmv * sc).astype(BF16)
        dya_ref[...] = (dmv * sa).astype(BF16)
        dg_ref[:, :d] = (dmv * yc_ref[...].astype(F32) * sc * (1.0 - sc)).astype(BF16)
        dg_ref[:, d:] = (dmv * ya_ref[...].astype(F32) * sa * (1.0 - sa)).astype(BF16)

    return _pcall(body, name=name,
                  out_shape=(jax.ShapeDtypeStruct((t, d), BF16), jax.ShapeDtypeStruct((t, d), BF16),
                             jax.ShapeDtypeStruct((t, 2 * d), BF16)),
                  grid=(t // tt,),
                  in_specs=[_rows(tt, d), _whole((d, d)), _rows(tt, 2 * d), _rows(tt, d), _rows(tt, d)],
                  out_specs=(_rows(tt, d), _rows(tt, d), _rows(tt, 2 * d)),
                  compiler_params=_params(("parallel",)))(dh, w, gates, yc, ya)


def _shift_down(cur, prev8, s):
    tt = cur.shape[0]
    rolled = pltpu.roll(cur, s, 0)
    row8 = lax.broadcasted_iota(jnp.int32, prev8.shape, 0)
    first8 = jnp.where(row8 < s, pltpu.roll(prev8, s, 0), rolled[:8])
    return jnp.concatenate([first8, rolled[8:]], axis=0) if tt > 8 else first8


def _shift_up(cur, next8, s):
    tt = cur.shape[0]
    rolled = pltpu.roll(cur, tt - s, 0)
    row8 = lax.broadcasted_iota(jnp.int32, next8.shape, 0)
    last8 = jnp.where(row8 >= 8 - s, pltpu.roll(next8, 8 - s, 0), rolled[tt - 8:])
    return jnp.concatenate([rolled[:tt - 8], last8], axis=0) if tt > 8 else last8


def _prev_rows(tt, d, col):
    return pl.BlockSpec((BF16_ROWS, d), lambda i: (jnp.maximum(i * (tt // BF16_ROWS) - 1, 0), col))


def _next_rows(tt, d, col, t):
    return pl.BlockSpec((BF16_ROWS, d),
                        lambda i: (jnp.minimum((i + 1) * (tt // BF16_ROWS), t // BF16_ROWS - 1), col))


def conv_out_fwd(name, cbx, cw8, w_out, tt):
    t, d3 = cbx.shape
    d = d3 // 3

    def body(cb_ref, cc_ref, cx_ref, pc_ref, px_ref, w_ref, wo_ref, o_ref, y_ref):
        has_prev = (pl.program_id(0) > 0).astype(F32)
        cc = cc_ref[...].astype(F32) * cx_ref[...].astype(F32)
        prev = pc_ref[...].astype(F32)[8:] * px_ref[...].astype(F32)[8:] * has_prev
        w = w_ref[...]
        conv = w[0:1] * _shift_down(cc, prev, 2) + w[1:2] * _shift_down(cc, prev, 1) + w[2:3] * cc
        ycin = (cb_ref[...].astype(F32) * conv).astype(BF16)
        o_ref[...] = ycin
        y_ref[...] = _dot(ycin, wo_ref[...], NN).astype(BF16)

    out = jax.ShapeDtypeStruct((t, d), BF16)
    return _pcall(body, name=name, out_shape=(out, out), grid=(t // tt,),
                  in_specs=[_rows(tt, d, 0), _rows(tt, d, 1), _rows(tt, d, 2), _prev_rows(tt, d, 1),
                            _prev_rows(tt, d, 2), _whole((8, d)), _whole((d, d))],
                  out_specs=(_rows(tt, d), _rows(tt, d)),
                  compiler_params=_params(("parallel",)))(cbx, cbx, cbx, cbx, cbx, cw8, w_out)


def conv_out_bwd(name, dyc, w_out, cbx, cw8, tt):
    t, d3 = cbx.shape
    d = d3 // 3
    n = t // tt

    def body(dy_ref, ndy_ref, wo_ref, cb_ref, cc_ref, cx_ref, pc_ref, px_ref, ncb_ref, w_ref, o_ref, dw_ref):
        i = pl.program_id(0)
        has_prev = (i > 0).astype(F32)
        has_next = (i < n - 1).astype(F32)
        cb = cb_ref[...].astype(F32)
        ccv = cc_ref[...].astype(F32)
        cxv = cx_ref[...].astype(F32)
        cc = ccv * cxv
        prev = pc_ref[...].astype(F32)[8:] * px_ref[...].astype(F32)[8:] * has_prev
        w = w_ref[...]
        cc1 = _shift_down(cc, prev, 1)
        cc2 = _shift_down(cc, prev, 2)
        conv = w[0:1] * cc2 + w[1:2] * cc1 + w[2:3] * cc
        dyv = _dot(dy_ref[...], wo_ref[...], NT)
        dconv = dyv * cb
        dnext = _dot(ndy_ref[...], wo_ref[...], NT)[:8] * ncb_ref[...].astype(F32)[:8] * has_next
        dcc = w[2:3] * dconv + w[1:2] * _shift_up(dconv, dnext, 1) + w[0:1] * _shift_up(dconv, dnext, 2)
        o_ref[:, :d] = (dyv * conv).astype(BF16)
        o_ref[:, d:2 * d] = (dcc * cxv).astype(BF16)
        o_ref[:, 2 * d:] = (dcc * ccv).astype(BF16)

        @pl.when(i == 0)
        def _():
            dw_ref[...] = jnp.zeros_like(dw_ref)

        dw_ref[0:1, :] += jnp.sum(dconv * cc2, axis=0, keepdims=True)
        dw_ref[1:2, :] += jnp.sum(dconv * cc1, axis=0, keepdims=True)
        dw_ref[2:3, :] += jnp.sum(dconv * cc, axis=0, keepdims=True)

    return _pcall(body, name=name,
                  out_shape=(jax.ShapeDtypeStruct((t, d3), BF16), jax.ShapeDtypeStruct((8, d), F32)),
                  grid=(n,),
                  in_specs=[_rows(tt, d), _next_rows(tt, d, 0, t),
                            _whole((d, d)), _rows(tt, d, 0), _rows(tt, d, 1), _rows(tt, d, 2),
                            _prev_rows(tt, d, 1), _prev_rows(tt, d, 2), _next_rows(tt, d, 0, t), _whole((8, d))],
                  out_specs=(_rows(tt, d3), _whole((8, d))),
                  compiler_params=_params(("arbitrary",)))(dyc, dyc, w_out, cbx, cbx, cbx, cbx, cbx, cbx, cw8)


def tail(name, h3, p, tgt, gp, gf, w_gate, w_proj, tt):
    t, d = h3.shape
    pd = p.shape[1]

    def body(h_ref, p_ref, tg_ref, gp_ref, gf_ref, wg_ref, wp_ref, np_ref, dh_ref, dpp_ref, dzg_ref, dgf_ref,
             loss_ref):
        hv = h_ref[...]
        npl = (hv * _rstd(hv) * gp_ref[...]).astype(BF16)
        np_ref[...] = npl
        pg = jax.nn.sigmoid(_dot(npl, wg_ref[...], NN))
        ppv = _dot(p_ref[...].astype(BF16), wp_ref[...], NN)
        h4 = hv + pg * ppv
        r4 = _rstd(h4)
        hn = h4 * r4
        gfv = gf_ref[...]
        err = hn * gfv - tg_ref[...]
        dy = err * (1.0 / d)
        gy = dy * gfv
        dh4 = r4 * (gy - hn * jnp.mean(gy * hn, axis=-1, keepdims=True))
        dh_ref[...] = dh4
        dpp_ref[...] = (dh4 * pg).astype(BF16)
        dzg_ref[...] = (dh4 * ppv * pg * (1.0 - pg)).astype(BF16)

        @pl.when(pl.program_id(0) == 0)
        def _():
            dgf_ref[...] = jnp.zeros_like(dgf_ref)
            loss_ref[...] = jnp.zeros_like(loss_ref)

        dgf_ref[...] += jnp.sum(dy * hn, axis=0, keepdims=True)
        tok = jnp.mean(err * err, axis=-1, keepdims=True)
        loss_ref[...] += 0.5 * jnp.sum(tok, axis=0, keepdims=True) * jnp.ones((1, loss_ref.shape[1]), F32)

    return _pcall(body, name=name,
                  out_shape=(jax.ShapeDtypeStruct((t, d), BF16), jax.ShapeDtypeStruct((t, d), F32),
                             jax.ShapeDtypeStruct((t, d), BF16), jax.ShapeDtypeStruct((t, d), BF16),
                             jax.ShapeDtypeStruct((1, d), F32), jax.ShapeDtypeStruct((1, d), F32)),
                  grid=(t // tt,),
                  in_specs=[_rows(tt, d), _rows(tt, pd), _rows(tt, d), _whole((1, d)), _whole((1, d)),
                            _whole((d, d)), _whole((pd, d))],
                  out_specs=(_rows(tt, d), _rows(tt, d), _rows(tt, d), _rows(tt, d), _whole((1, d)),
                             _whole((1, d))),
                  compiler_params=_params(("arbitrary",)))(h3, p, tgt, gp, gf, w_gate, w_proj)


SCALE = 1.0 / math.sqrt(HEAD_DIM)


def _log_stick(z):
    return -(jnp.maximum(z, 0.0) + jnp.log(1.0 + jnp.exp(-jnp.abs(z))))


def _tri_sum(x, tri):
    hi = x.astype(BF16)
    lo = (x - hi.astype(F32)).astype(BF16)
    return _dot(hi, tri, NN) + _dot(lo, tri, NN)


KEY_BLOCK = 128
NEAR = 3
THIN_ROWS = 32


def _pad_block(x):
    n = x.shape[0]
    return x if n == KEY_BLOCK else jnp.concatenate([x, jnp.zeros((KEY_BLOCK - n, x.shape[1]), x.dtype)], axis=0)


def _sb_near(qs, jds, k_ref, below, upper, last_rows):
    near_rows = (KEY_BLOCK,) * (NEAR - 1) + (last_rows,)
    pairs = [(s, b) for s in range(len(qs)) for b in range(NEAR)]
    rows = {(s, b): _block_rows(jnp.maximum(jds[s] - b, 0), KEY_BLOCK) for s, b in pairs}
    z = {(s, b): _dot(qs[s][:near_rows[b]], k_ref[rows[s, b], :], NT) * SCALE for s, b in pairs}
    lg = {(s, b): jnp.where(below, _log_stick(z[s, b]), 0.0) if b == 0 else _log_stick(z[s, b]) for s, b in pairs}
    cum = {(s, b): _tri_sum(lg[s, b], upper) for s, b in pairs}
    out, carries = [], []
    for s in range(len(qs)):
        c = cum[s, 0][:, 0:1]
        blocks = [(rows[s, 0], z[s, 0], jnp.exp(jnp.where(below, z[s, 0] + cum[s, 0], -1e30)))]
        for b in range(1, NEAR):
            live = jds[s] >= b
            off = c[:near_rows[b]] + jnp.where(live, 0.0, -1e30)
            blocks.append((rows[s, b], z[s, b], jnp.exp(z[s, b] + cum[s, b] + off)))
            c = c + _pad_block(jnp.where(live, cum[s, b][:, 0:1], 0.0))
        out.append(blocks)
        carries.append(c)
    return out, carries


def _sb_far(q, kj, upper, c, skip):
    z = _dot(q, kj, NT) * SCALE
    cum = _tri_sum(_log_stick(z), upper)
    return z, jnp.exp(z + cum + (c + jnp.where(skip, -1e30, 0.0))), c + jnp.where(skip, 0.0, cum[:, 0:1])


def _took_it(j, jd, last_rows):
    first = lax.broadcasted_iota(jnp.int32, (KEY_BLOCK, 1), 0) < last_rows
    return jnp.logical_and(j == jd - (NEAR - 1), first)


def _block_rows(j, size):
    return pl.ds(pl.multiple_of(j * size, size), size)


def _sweep_on(st):
    return jnp.logical_and(st[0] >= 0, jnp.max(st[1]) > -STICK_EXIT)


def attn_fwd(name, qkv, tq):
    t, d3 = qkv.shape
    d = d3 // 3
    nh = d // HEAD_DIM
    nq = t // tq
    tb = KEY_BLOCK
    nsub = tq // tb

    def body(q_ref, k_ref, v_ref, o_ref):
        i = pl.program_id(1)
        row = lax.broadcasted_iota(jnp.int32, (tb, tb), 0)
        col = lax.broadcasted_iota(jnp.int32, (tb, tb), 1)
        upper = (row >= col).astype(BF16)
        qs = [q_ref[s * tb:(s + 1) * tb, :] for s in range(nsub)]
        jds = [i * nsub + s for s in range(nsub)]
        near, carries = _sb_near(qs, jds, k_ref, col < row, upper, THIN_ROWS)
        state = []
        for s in range(nsub):
            acc = jnp.zeros((tb, HEAD_DIM), F32)
            for rows, _, a in near[s]:
                acc = acc + _pad_block(_dot(a.astype(BF16), v_ref[rows, :], NN))
            state.append((qs[s], jds[s], carries[s], acc))
        for s, (q, jd, c, acc) in enumerate(state):

            def step(st, q=q, jd=jd):
                rows = _block_rows(st[0], tb)
                _, a, c2 = _sb_far(q, k_ref[rows, :], upper, st[1], _took_it(st[0], jd, THIN_ROWS))
                return st[0] - 1, c2, st[2] + _dot(a.astype(BF16), v_ref[rows, :], NN)

            _, _, acc = lax.while_loop(_sweep_on, step, (jd - (NEAR - 1), c, acc))
            o_ref[s * tb:(s + 1) * tb, :] = acc.astype(o_ref.dtype)

    return _pcall(body, name=name, out_shape=jax.ShapeDtypeStruct((t, d), BF16), grid=(nh, nq),
                  in_specs=[pl.BlockSpec((tq, HEAD_DIM), lambda h, i: (i, h)),
                            pl.BlockSpec((t, HEAD_DIM), lambda h, i: (0, nh + h)),
                            pl.BlockSpec((t, HEAD_DIM), lambda h, i: (0, 2 * nh + h))],
                  out_specs=pl.BlockSpec((tq, HEAD_DIM), lambda h, i: (i, h)),
                  compiler_params=_params(("parallel", "arbitrary")))(qkv, qkv, qkv)


def attn_bwd(name, qkv, do, tq):
    t, d3 = qkv.shape
    d = d3 // 3
    nh = d // HEAD_DIM
    nq = t // tq
    tb = KEY_BLOCK
    nsub = tq // tb

    def body(q_ref, k_ref, v_ref, do_ref, dq_ref, dk_ref, dv_ref, dk_acc, dv_acc, g_buf, z_buf):
        i = pl.program_id(1)

        @pl.when(i == 0)
        def _():
            dk_acc[...] = jnp.zeros_like(dk_acc)
            dv_acc[...] = jnp.zeros_like(dv_acc)

        row = lax.broadcasted_iota(jnp.int32, (tb, tb), 0)
        col = lax.broadcasted_iota(jnp.int32, (tb, tb), 1)
        below = col < row
        upper = (row >= col).astype(BF16)
        lower = (row <= col).astype(BF16)

        qs = [q_ref[s * tb:(s + 1) * tb, :] for s in range(nsub)]
        dos = [do_ref[s * tb:(s + 1) * tb, :] for s in range(nsub)]
        jds = [i * nsub + s for s in range(nsub)]
        near, carries = _sb_near(qs, jds, k_ref, below, upper, KEY_BLOCK)
        da = [[_dot(dos[s][:a.shape[0]], v_ref[rows, :], NT) for rows, _, a in near[s]] for s in range(nsub)]
        state = []
        for s in range(nsub):
            kept = [(rows, z, da[s][b] * a) for b, (rows, z, a) in enumerate(near[s])]
            for rows, _, a in near[s]:
                dv_acc[rows, :] += _dot(a.astype(BF16), dos[s][:a.shape[0]], TN)
            state.append((qs[s], dos[s], jds[s], carries[s], kept))

        carried = []
        for s, (q, dov, jd, c, kept) in enumerate(state):
            def step(st, s=s, q=q, dov=dov, jd=jd):
                j = st[0]
                rows = _block_rows(j, tb)
                z, a, c2 = _sb_far(q, k_ref[rows, :], upper, st[1], _took_it(j, jd, KEY_BLOCK))
                g_buf[jd - j] = _dot(dov, v_ref[rows, :], NT) * a
                z_buf[jd - j] = z
                dv_acc[rows, :] += _dot(a.astype(BF16), dov, TN)
                return j - 1, c2

            j_stop, _ = lax.while_loop(_sweep_on, step, (jd - (NEAR - 1), c))

            def far(j, st, s=s, q=q, jd=jd):
                run, dq = st
                rows = _block_rows(j, tb)
                g = g_buf[jd - j]
                dz = (g - jax.nn.sigmoid(z_buf[jd - j]) * (run + _tri_sum(g, lower))).astype(BF16)
                dk_acc[rows, :] += _dot(dz, q, TN)
                return run + jnp.sum(g, axis=1, keepdims=True), dq + _dot(dz, k_ref[rows, :], NN)

            carried.append(lax.fori_loop(j_stop + 1, jd - (NEAR - 1) + 1, far,
                                         (jnp.zeros((tb, 1), F32), jnp.zeros((tb, HEAD_DIM), F32))))

        tri = [[_dot(g.astype(BF16), lower, NN) for _, _, g in st[4]] for st in state]
        sig = [[jax.nn.sigmoid(z) for _, z, _ in st[4]] for st in state]
        for s, (q, dov, jd, c, kept) in enumerate(state):
            run, dq = carried[s]
            for b in reversed(range(NEAR)):
                rows, z, g = kept[b]
                n = g.shape[0]
                dz = g - sig[s][b] * (run[:n] + tri[s][b])
                if b == 0:
                    dz = jnp.where(below, dz, 0.0)
                dz = dz.astype(BF16)
                dk_acc[rows, :] += _dot(dz, q[:n], TN)
                dq = dq + _pad_block(_dot(dz, k_ref[rows, :], NN))
                if b:
                    run = run + _pad_block(jnp.sum(g, axis=1, keepdims=True))
            dq_ref[s * tb:(s + 1) * tb, :] = (dq * SCALE).astype(BF16)

        @pl.when(i == nq - 1)
        def _():
            dk_ref[...] = (dk_acc[...] * SCALE).astype(BF16)
            dv_ref[...] = dv_acc[...].astype(BF16)

    blk = pl.BlockSpec((tq, HEAD_DIM), lambda h, i: (i, h))
    col_h = pl.BlockSpec((t, HEAD_DIM), lambda h, i: (0, h))
    out = jax.ShapeDtypeStruct((t, d), BF16)
    return _pcall(body, name=name, out_shape=(out, out, out), grid=(nh, nq),
                  in_specs=[blk,
                            pl.BlockSpec((t, HEAD_DIM), lambda h, i: (0, nh + h)),
                            pl.BlockSpec((t, HEAD_DIM), lambda h, i: (0, 2 * nh + h)),
                            blk],
                  out_specs=(blk, col_h, col_h),
                  scratch_shapes=[pltpu.VMEM((t, HEAD_DIM), F32), pltpu.VMEM((t, HEAD_DIM), F32),
                                  pltpu.VMEM((t // tb, tb, tb), F32), pltpu.VMEM((t // tb, tb, tb), F32)],
                  compiler_params=_params(("parallel", "arbitrary")))(qkv, qkv, qkv, do)


def _place():
    x, y, c = lax.axis_index("x"), lax.axis_index("y"), lax.axis_index("c")
    chips = [(1 - x, y), (x, 1 - y), (1 - x, 1 - y)]
    return x, y, c, chips


def _remote(src, dst, send_sem, recv_sem, dev):
    return pltpu.make_async_remote_copy(src_ref=src, dst_ref=dst, send_sem=send_sem, recv_sem=recv_sem,
                                        device_id=dev, device_id_type=MESH)


def place_shards(name, ws, chip):
    tiles, steps = _job_tiles([w.shape for w in ws], 1 << 20, BF16_ROWS)
    nj = len(ws)

    def body(chip_ref, *refs):
        i = pl.program_id(0)
        for k, (_, n) in enumerate(tiles):
            @pl.when(i < n)
            def _(w_ref=refs[k], o_ref=refs[nj + k]):
                o_ref[...] = w_ref[...].astype(BF16)

    spec = pltpu.PrefetchScalarGridSpec(
        num_scalar_prefetch=1, grid=(steps,),
        in_specs=[pl.BlockSpec((tr, w.shape[1]), lambda i, s, n=n: (jnp.minimum(i, n - 1), 0))
                  for w, (tr, n) in zip(ws, tiles)],
        out_specs=[pl.BlockSpec((None, tr, w.shape[1]), lambda i, s, n=n: (s[0], jnp.minimum(i, n - 1), 0))
                   for w, (tr, n) in zip(ws, tiles)])
    return _pcall(body, name=name, out_shape=[jax.ShapeDtypeStruct((N_CHIPS,) + w.shape, BF16) for w in ws],
                  grid_spec=spec, compiler_params=_params(("arbitrary",)))(chip, *ws)


class Comm:
    def __init__(self, ins, outs, aliases, sems, first, mid, last):
        self.ins, self.outs, self.aliases, self.sems = list(ins), list(outs), dict(aliases), list(sems)
        self.first, self.mid, self.last = first, mid, last


def run_comm(name, comm):
    ni, no = len(comm.ins), len(comm.outs)

    def body(*refs):
        ins, outs, sems = refs[:ni], refs[ni:ni + no], refs[ni + no:]
        comm.first(ins, outs, sems)
        comm.mid(ins, outs, sems)
        comm.last(ins, outs, sems)

    return _pcall(body, name=name, out_shape=comm.outs, in_specs=[ANY] * ni, out_specs=[ANY] * no,
                  input_output_aliases=comm.aliases, scratch_shapes=comm.sems, compiler_params=_params())(*comm.ins)


def gather_comm(bufs):
    n = len(bufs)

    def half(out, w, which):
        pr = out[w].shape[1] // 2
        return pl.ds(pl.multiple_of(which * pr, BF16_ROWS), pr)

    def first(ins, out, sems):
        isend, irecv, _, _ = sems
        x, y, c, chips = _place()
        for w in range(n):
            mine = out[w].at[2 * x + y, half(out, w, c)]
            for j, (cx, cy) in enumerate(chips):
                _remote(mine, mine, isend.at[3 * w + j], irecv.at[3 * w + j], (cx, cy, c)).start()

    def mid(ins, out, sems):
        isend, irecv, dsend, drecv = sems
        x, y, c, chips = _place()
        sib = (x, y, 1 - c)
        for w in range(n):
            for j, (cx, cy) in enumerate(chips):
                landed = out[w].at[2 * cx + cy, half(out, w, c)]
                _remote(landed, landed, isend.at[3 * w + j], irecv.at[3 * w + j], sib).wait_recv()
                _remote(landed, landed, dsend.at[3 * w + j], drecv.at[3 * w + j], sib).start()

    def last(ins, out, sems):
        isend, irecv, dsend, drecv = sems
        x, y, c, chips = _place()
        sib = (x, y, 1 - c)
        for w in range(n):
            for j, (cx, cy) in enumerate(chips):
                landed = out[w].at[2 * cx + cy, half(out, w, 1 - c)]
                _remote(landed, landed, dsend.at[3 * w + j], drecv.at[3 * w + j], sib).wait_recv()
        for w in range(n):
            sent = out[w].at[0, half(out, w, c)]
            for j in range(3):
                _remote(sent, sent, isend.at[3 * w + j], irecv.at[3 * w + j], sib).wait_send()
                _remote(sent, sent, dsend.at[3 * w + j], drecv.at[3 * w + j], sib).wait_send()

    return Comm(bufs, [jax.ShapeDtypeStruct(s.shape, s.dtype) for s in bufs], {w: w for w in range(n)},
                [pltpu.SemaphoreType.DMA((3 * n,))] * 4, first, mid, last)


def _nothing(ins, outs, sems):
    return None


def join_comms(a, b):
    ni, no, ns = len(a.ins), len(a.outs), len(a.sems)

    def both(f, g):
        def hook(ins, outs, sems):
            f(ins[:ni], outs[:no], sems[:ns])
            g(ins[ni:], outs[no:], sems[ns:])
        return hook

    aliases = dict(a.aliases)
    aliases.update({ni + k: no + v for k, v in b.aliases.items()})
    return Comm(a.ins + b.ins, a.outs + b.outs, aliases, a.sems + b.sems,
                both(a.first, b.first), both(a.mid, b.mid), both(a.last, b.last))


def exchange_comm(pieces):
    n = len(pieces)

    def copies(src, out, sems):
        x, y, c, _ = _place()
        return [_remote(src[w].at[k, 1 - c], out[w].at[k], sems[0].at[N_CHIPS * w + k], sems[1].at[N_CHIPS * w + k],
                        (x, y, 1 - c)) for w in range(n) for k in range(N_CHIPS)]

    def first(src, out, sems):
        for cp in copies(src, out, sems):
            cp.start()

    def last(src, out, sems):
        for cp in copies(src, out, sems):
            cp.wait()

    return Comm(pieces, [jax.ShapeDtypeStruct((N_CHIPS,) + s.shape[2:], s.dtype) for s in pieces], {},
                [pltpu.SemaphoreType.DMA((N_CHIPS * n,))] * 2, first, _nothing, last)


def scatter_comm(parts):
    n = len(parts)

    def copies(src, out, sems):
        x, y, c, chips = _place()
        return [_remote(src[w].at[2 * cx + cy], out[w].at[j], sems[0].at[3 * w + j], sems[1].at[3 * w + j], (cx, cy, c))
                for w in range(n) for j, (cx, cy) in enumerate(chips)]

    def first(src, out, sems):
        for cp in copies(src, out, sems):
            cp.start()

    def last(src, out, sems):
        for cp in copies(src, out, sems):
            cp.wait()

    return Comm(parts, [jax.ShapeDtypeStruct((3,) + s.shape[1:], s.dtype) for s in parts], {},
                [pltpu.SemaphoreType.DMA((3 * n,))] * 2, first, _nothing, last)


def share_comm(halves):
    n = len(halves)

    def first(ins, buf, sems):
        x, y, c, _ = _place()
        for w in range(n):
            _remote(buf[w].at[c], buf[w].at[c], sems[0].at[w], sems[1].at[w], (x, y, 1 - c)).start()

    def last(ins, buf, sems):
        x, y, c, _ = _place()
        for w in range(n):
            landed = buf[w].at[1 - c]
            _remote(landed, landed, sems[0].at[w], sems[1].at[w], (x, y, 1 - c)).wait_recv()
        for w in range(n):
            _remote(buf[w].at[c], buf[w].at[c], sems[0].at[w], sems[1].at[w], (x, y, 1 - c)).wait_send()

    return Comm(halves, [jax.ShapeDtypeStruct(s.shape, s.dtype) for s in halves], {w: w for w in range(n)},
                [pltpu.SemaphoreType.DMA((n,))] * 2, first, _nothing, last)


def gather_small(name, blk, reduce):
    r, cdim = blk.shape

    def body(in_ref, out_ref, *rest):
        if reduce:
            buf, send_sem, recv_sem = rest
        else:
            buf = out_ref
            send_sem, recv_sem = rest
        x, y, c, _ = _place()
        me = 4 * x + 2 * y + c
        buf[me] = in_ref[...]
        peers = []
        for dx in range(2):
            for dy in range(2):
                for dc in range(2):
                    if dx or dy or dc:
                        peers.append((dx, dy, dc))
        copies = []
        for s, (dx, dy, dc) in enumerate(peers):
            cp = _remote(in_ref, buf.at[me], send_sem.at[s], recv_sem.at[s],
                         ((1 - x if dx else x), (1 - y if dy else y), (1 - c if dc else c)))
            cp.start()
            copies.append(cp)
        for s, (dx, dy, dc) in enumerate(peers):
            px, py, pc_ = (1 - x if dx else x), (1 - y if dy else y), (1 - c if dc else c)
            landed = buf.at[4 * px + 2 * py + pc_]
            _remote(landed, landed, send_sem.at[s], recv_sem.at[s], (x, y, c)).wait_recv()
        for cp in copies:
            cp.wait_send()
        if reduce:
            tot = buf[0]
            for s in range(1, N_DEV):
                tot = tot + buf[s]
            out_ref[...] = tot

    vm = pl.BlockSpec(memory_space=pltpu.VMEM)
    out_shape = jax.ShapeDtypeStruct((r, cdim) if reduce else (N_DEV, r, cdim), F32)
    scratch = ([pltpu.VMEM((N_DEV, r, cdim), F32)] if reduce else []) + [pltpu.SemaphoreType.DMA((N_DEV - 1,))] * 2
    return _pcall(body, name=name, out_shape=out_shape, in_specs=[vm], out_specs=vm, scratch_shapes=scratch,
                  compiler_params=_params())(blk)


def _job_tiles(shapes, tile_bytes, mult):
    tiles = []
    for rows, cols in shapes:
        tr = _tile(rows, max(mult, tile_bytes // (4 * cols)), mult)
        tiles.append((tr, rows // tr))
    return tiles, max(n for _, n in tiles)


def sum_cores(name, owns, gots, place):
    nj = len(owns)
    tiles, _ = _job_tiles([o.shape[2:] for o in owns], 1 << 21, BF16_ROWS)
    steps = max(N_CHIPS * n for _, n in tiles)

    def body(place_ref, *refs):
        i = pl.program_id(0)
        for k, (_, n) in enumerate(tiles):
            @pl.when(i < N_CHIPS * n)
            def _(own_ref=refs[2 * k], got_ref=refs[2 * k + 1], o_ref=refs[2 * nj + k]):
                o_ref[...] = (own_ref[...].astype(F32) + got_ref[...].astype(F32)).astype(o_ref.dtype)

    in_specs, out_specs, out_shape, args = [], [], [], []
    for own, got, (tr, n) in zip(owns, gots, tiles):
        pc = own.shape[3]
        last = N_CHIPS * n - 1
        in_specs += [pl.BlockSpec((None, None, tr, pc),
                                  lambda i, s, n=n, last=last: (jnp.minimum(i, last) // n, s[1], jnp.minimum(i, last) % n, 0)),
                     pl.BlockSpec((None, tr, pc),
                                  lambda i, s, n=n, last=last: (jnp.minimum(i, last) // n, jnp.minimum(i, last) % n, 0))]
        out_specs.append(pl.BlockSpec((None, tr, pc),
                                      lambda i, s, n=n, last=last: (jnp.minimum(i, last) // n, jnp.minimum(i, last) % n, 0)))
        out_shape.append(jax.ShapeDtypeStruct(got.shape, BF16))
        args += [own, got]
    spec = pltpu.PrefetchScalarGridSpec(num_scalar_prefetch=1, grid=(steps,), in_specs=in_specs, out_specs=out_specs)
    return _pcall(body, name=name, out_shape=out_shape, grid_spec=spec,
                  compiler_params=_params(("arbitrary",)))(place, *args)


def sum_chips(name, parts, gots, place):
    nj = len(parts)
    tiles, steps = _job_tiles([p.shape[1:] for p in parts], 1 << 20, BF16_ROWS)

    def body(place_ref, *refs):
        i = pl.program_id(0)
        for k, (_, n) in enumerate(tiles):
            @pl.when(i < n)
            def _(part_ref=refs[2 * k], got_ref=refs[2 * k + 1], o_ref=refs[2 * nj + k]):
                tot = part_ref[...].astype(F32)
                for j in range(3):
                    tot = tot + got_ref[j].astype(F32)
                o_ref[...] = tot

    in_specs, out_specs, out_shape, args = [], [], [], []
    for part, got, (tr, n) in zip(parts, gots, tiles):
        pc = part.shape[2]
        in_specs += [pl.BlockSpec((None, tr, pc), lambda i, s, n=n: (s[0], jnp.minimum(i, n - 1), 0)),
                     pl.BlockSpec((3, tr, pc), lambda i, s, n=n: (0, jnp.minimum(i, n - 1), 0))]
        out_specs.append(pl.BlockSpec((None, tr, pc), lambda i, s, n=n: (s[1], jnp.minimum(i, n - 1), 0)))
        out_shape.append(jax.ShapeDtypeStruct((2,) + part.shape[1:], F32))
        args += [part, got]
    spec = pltpu.PrefetchScalarGridSpec(num_scalar_prefetch=1, grid=(steps,), in_specs=in_specs, out_specs=out_specs)
    return _pcall(body, name=name, out_shape=out_shape, grid_spec=spec,
                  compiler_params=_params(("arbitrary",)))(place, *args)


def adamw(name, jobs):
    c1 = 1.0 / (1.0 - ADAM_B1 ** ADAM_STEP)
    c2 = 1.0 / (1.0 - ADAM_B2 ** ADAM_STEP)
    nj = len(jobs)
    tiles, steps = _job_tiles([j[0].shape for j in jobs], 1 << 18, 8)

    def body(*refs):
        i = pl.program_id(0)
        for k, (_, n) in enumerate(tiles):
            w_ref, g_ref, m_ref, v_ref = refs[4 * k:4 * k + 4]
            d_ref, nm_ref, nv_ref = refs[4 * nj + 3 * k:4 * nj + 3 * k + 3]

            @pl.when(i < n)
            def _(w_ref=w_ref, g_ref=g_ref, m_ref=m_ref, v_ref=v_ref, d_ref=d_ref, nm_ref=nm_ref, nv_ref=nv_ref):
                gv = g_ref[...]
                nm = ADAM_B1 * m_ref[...] + (1.0 - ADAM_B1) * gv
                nv = ADAM_B2 * v_ref[...] + (1.0 - ADAM_B2) * (gv * gv)
                nm_ref[...] = nm
                nv_ref[...] = nv
                d_ref[...] = -ADAM_LR * ((nm * c1) / (jnp.sqrt(nv * c2) + ADAM_EPS) + ADAM_WD * w_ref[...])

    in_specs, out_specs, out_shape, args = [], [], [], []
    for (w, g, m, v), (tr, n) in zip(jobs, tiles):
        spec = pl.BlockSpec((tr, w.shape[1]), lambda i, n=n: (jnp.minimum(i, n - 1), 0))
        in_specs += [spec] * 4
        out_specs += [spec] * 3
        out_shape += [jax.ShapeDtypeStruct(w.shape, F32)] * 3
        args += [w, g, m, v]
    res = _pcall(body, name=name, out_shape=out_shape, grid=(steps,), in_specs=in_specs, out_specs=out_specs,
                 compiler_params=_params(("arbitrary",)))(*args)
    return [tuple(res[3 * k:3 * k + 3]) for k in range(nj)]


MATS = ["ffn1_w_in", "ffn1_w_out", "w_mix_in", "w_conv_out", "w_attn_out", "w_mix_out", "ffn2_w_in", "ffn2_w_out",
        "w_ple_gate", "w_ple_proj"]
COL_SHARDED = {"ffn1_w_in", "w_mix_in", "ffn2_w_in", "w_ple_proj"}
NORMS = ["ffn1_norm", "mix_norm", "ffn2_norm", "ple_norm", "final_norm"]
WEIGHTS = ["ffn1_norm", "ffn1_w_in", "ffn1_w_out", "mix_norm", "w_mix_in", "conv_w", "w_conv_out", "w_attn_out",
           "w_mix_out", "ffn2_norm", "ffn2_w_in", "ffn2_w_out", "ple_norm", "w_ple_gate", "w_ple_proj", "final_norm"]


def _pad_rows(a, rows):
    return jnp.concatenate([a, jnp.zeros((rows - a.shape[0],) + a.shape[1:], a.dtype)], axis=0)


def _step(x, p, tgt, w, m, v):
    t, d = x.shape
    tt = _tile(t, 256)
    tm = _tile(t, 512)
    tm2 = _tile(t, 1024)
    tq = _tile(t, 1024)

    chip = 2 * lax.axis_index("x") + lax.axis_index("y")
    place = jnp.stack([chip, lax.axis_index("c")]).astype(jnp.int32)

    placed = dict(zip(MATS, place_shards("place_shards", [w[k] for k in MATS], place)))
    full = {}

    def keep(names, bufs):
        for k, buf in zip(names, bufs):
            full[k] = buf if k in COL_SHARDED else buf.reshape(-1, buf.shape[2])

    def gather_of(names):
        return gather_comm([placed[k] for k in names])

    cw_all = gather_small("gather_conv_w", _pad_rows(w["conv_w"], 8), False)
    cw8 = jnp.concatenate([cw_all[2 * k] for k in range(N_CHIPS)], axis=1)
    g1, gm, g2, gp, gf = (w[k].reshape(1, d) for k in NORMS)

    def ffn_fwd(tag, h, g, first, w_in_name, w_out_name, riders):
        if first:
            n, bufs = rms_fwd(tag + "_norm", h, g, tt, comm=gather_of(first))
            keep(first, bufs)
            (a, s), bufs = ffn_in_act(tag + "_in", n, full[w_in_name], tm, comm=gather_of(riders))
            keep(riders, bufs)
        else:
            a, s, n = ffn_in_act(tag + "_in", h, full[w_in_name], tm, gain=g)
        return n, a, s, mm_nn(tag + "_out", s, full[w_out_name], F32, tm, res=h, alpha=0.5)

    n1, a1, s1, h1 = ffn_fwd("ffn1", x, g1, ["ffn1_w_in"], "ffn1_w_in", "ffn1_w_out", ["ffn1_w_out", "w_mix_in"])
    wmix = full["w_mix_in"]
    riders = [["w_conv_out", "w_attn_out", "w_mix_out"], ["ffn2_w_in"], ["ffn2_w_out", "w_ple_gate", "w_ple_proj"]]
    (cbx, u), bufs = mm_nn_stacked("mix_in_conv", h1, wmix, BF16, tm2, d, 0, 3, comm=gather_of(riders[0]), gain=gm)
    keep(riders[0], bufs)
    qkv, bufs = mm_nn_stacked("mix_in_qkv", u, wmix, BF16, tm2, d, 3, 3, comm=gather_of(riders[1]))
    keep(riders[1], bufs)
    gates, bufs = mm_nn_stacked("mix_in_gates", u, wmix, BF16, tm2, d, 6, 2, comm=gather_of(riders[2]))
    keep(riders[2], bufs)
    wpp = full["w_ple_proj"]
    wpp = jnp.transpose(wpp, (1, 0, 2)).reshape(wpp.shape[1], -1)
    ycin, y_conv = conv_out_fwd("conv_out", cbx, cw8, full["w_conv_out"], tt)
    o = attn_fwd("attn", qkv, tq)
    y_attn = mm_nn("attn_out", o, full["w_attn_out"], BF16, tm)
    merged, h2 = mix_out_fwd("mix_out", gates, y_conv, y_attn, h1, full["w_mix_out"], tm)
    n2, a2, s2, h3 = ffn_fwd("ffn2", h2, g2, [], "ffn2_w_in", "ffn2_w_out", [])

    pieces, chip_sums, halves = {}, {}, {}

    def as_pieces(k):
        pc = pieces[k]
        return pc if k in COL_SHARDED else pc.reshape(N_CHIPS, 2, pc.shape[0] // (2 * N_CHIPS), pc.shape[1])

    def sum_siblings(tag, names):
        pcs = [as_pieces(k) for k in names]
        got = run_comm("exchange_" + tag, exchange_comm(pcs))
        chip_sums.update(zip(names, sum_cores("sum_cores_" + tag, pcs, got, place)))

    def scatter_of(names):
        return scatter_comm([chip_sums[k] for k in names])

    def sum_landed(tag, names, landed):
        halves.update(zip(names, sum_chips("sum_chips_" + tag, [chip_sums[k] for k in names], landed, place)))

    npl, dh4, dpp, dzg, dgf, loss_row = tail("tail", h3, p, tgt, gp, gf, full["w_ple_gate"], wpp, tt)
    dwpp = mm_tn_whole("ple_proj_dw", p, dpp, tm2)
    pieces["w_ple_proj"] = jnp.transpose(dwpp.reshape(2, p.shape[1] // 2, N_CHIPS, d // N_CHIPS), (2, 0, 1, 3))
    pieces["w_ple_gate"] = mm_tn_rows("ple_gate_dw", npl, dzg, tm2)
    dh3, df2, dgp = mm_nt("ple_gate_dx", dzg, full["w_ple_gate"], F32, tm, d, norm=(h3, gp, dh4), alpha=0.5)
    w_in, w_out = full["ffn2_w_in"], full["ffn2_w_out"]
    pieces["ffn2_w_out"] = mm_tn_rows("ffn2_dwout", s2, df2, tm2)
    da2 = ffn_ds_dact("ffn2_ds", df2, w_out, a2, tm2)
    pieces["ffn2_w_in"] = mm_tn_cols("ffn2_dwin", n2, da2, tm2)
    dh2, dh2b, dg2 = mm_nt_stacked("ffn2_dn", da2, w_in, F32, tm2, w_in.shape[2], norm=(h2, g2, dh3))
    pieces["w_mix_out"] = mm_tn_rows("mix_out_dw", merged, dh2b, tm2)
    dyc, dya, dgates = mix_out_bwd("mix_out_dx", dh2b, full["w_mix_out"], gates, y_conv, y_attn, tm)
    pieces["w_conv_out"] = mm_tn_rows("conv_out_dw", ycin, dyc, tm2)
    dcbx, dcw8 = conv_out_bwd("conv_out_dx", dyc, full["w_conv_out"], cbx, cw8, tt)
    pieces["w_attn_out"] = mm_tn_rows("attn_out_dw", o, dya, tm2)
    do = mm_nt("attn_out_dx", dya, full["w_attn_out"], BF16, tm, d)
    dq, dk, dv = attn_bwd("attn_bwd", qkv, do, tq)
    dmix = [dcbx, dq, dk, dv, dgates]
    early = ["ffn2_w_in", "ffn2_w_out", "w_ple_gate", "w_ple_proj", "w_mix_out", "w_conv_out", "w_attn_out"]
    swap = exchange_comm([as_pieces(k) for k in early])
    pieces["w_mix_in"], got = mm_tn_parts("mix_in_dw", u, dmix, tm2, comm=swap)
    chip_sums.update(zip(early, sum_cores("sum_cores_early", swap.ins, got, place)))
    swap = exchange_comm([as_pieces("w_mix_in")])
    (dh1, df1, dgm), landed = mm_nt_parts("mix_in_dx", dmix, wmix, tm, (h1, gm, dh2), 0.5,
                                          comm=join_comms(scatter_of(early), swap))
    sum_landed("early", early, landed[:len(early)])
    chip_sums["w_mix_in"] = sum_cores("sum_cores_mix", swap.ins, landed[len(early):], place)[0]
    w_in, w_out = full["ffn1_w_in"], full["ffn1_w_out"]
    pieces["ffn1_w_out"] = mm_tn_rows("ffn1_dwout", s1, df1, tm2)
    da1 = ffn_ds_dact("ffn1_ds", df1, w_out, a1, tm2)
    pieces["ffn1_w_in"], landed = mm_tn_cols("ffn1_dwin", n1, da1, tm2, comm=scatter_of(["w_mix_in"]))
    sum_landed("mix", ["w_mix_in"], landed)
    late = ["ffn1_w_in", "ffn1_w_out"]
    sum_siblings("late", late)
    done = early + ["w_mix_in"]
    (dx, _, dg1), landed = mm_nt_stacked(
        "ffn1_dn", da1, w_in, F32, tm2, w_in.shape[2], norm=(x, g1, dh1),
        comm=join_comms(scatter_of(late), share_comm([halves[k] for k in done])))
    sum_landed("late", late, landed[:len(late)])
    shared = dict(zip(done, landed[len(late):]))

    shared.update(zip(late, run_comm("share_halves", share_comm([halves[k] for k in late]))))
    grad, delta, new_m, new_v = {}, {}, {}, {}
    for k in MATS:
        grad[k] = shared[k].reshape(w[k].shape)

    small = jnp.concatenate([dg1, dgm, dg2, dgp, dgf, dcw8[:3], loss_row, jnp.zeros((7, d), F32)], axis=0)
    tot = gather_small("sum_small", small, True)
    loss = tot[8, 0]
    norm_w = jnp.concatenate([w[k].reshape(1, d) for k in NORMS] + [jnp.zeros((3, d), F32)], axis=0)
    norm_m = jnp.concatenate([m[k].reshape(1, d) for k in NORMS] + [jnp.zeros((3, d), F32)], axis=0)
    norm_v = jnp.concatenate([v[k].reshape(1, d) for k in NORMS] + [jnp.ones((3, d), F32)], axis=0)
    norm_g = jnp.concatenate([tot[0:5], jnp.zeros((3, d), F32)], axis=0)
    cs = d // N_CHIPS
    gcw = lax.dynamic_slice(tot[5:8], (0, chip * cs), (3, cs))
    conv_job = (_pad_rows(w["conv_w"], 8), _pad_rows(gcw, 8), _pad_rows(m["conv_w"], 8),
                jnp.concatenate([v["conv_w"], jnp.ones((5, cs), F32)], axis=0))

    steps = adamw("adamw", [(w[k], grad[k], m[k], v[k]) for k in MATS]
                  + [(norm_w, norm_g, norm_m, norm_v), conv_job])
    for k, res in zip(MATS, steps):
        delta[k], new_m[k], new_v[k] = res
    nd, nm, nv = steps[len(MATS)]
    for r, k in enumerate(NORMS):
        grad[k] = norm_g[r].reshape(w[k].shape)
        delta[k], new_m[k], new_v[k] = (a[r].reshape(w[k].shape) for a in (nd, nm, nv))
    cd, cm, cv = steps[len(MATS) + 1]
    grad["conv_w"], delta["conv_w"], new_m["conv_w"], new_v["conv_w"] = gcw, cd[:3], cm[:3], cv[:3]
    return loss, dx, grad, delta, new_m, new_v


def kernel(x, p, ffn1_norm, ffn1_w_in, ffn1_w_out, mix_norm, w_mix_in, conv_w, w_conv_out, w_attn_out, w_mix_out, ffn2_norm, ffn2_w_in, ffn2_w_out, ple_norm, w_ple_gate, w_ple_proj, final_norm, loss_target, m_ffn1_norm, m_ffn1_w_in, m_ffn1_w_out, m_mix_norm, m_w_mix_in, m_conv_w, m_w_conv_out, m_w_attn_out, m_w_mix_out, m_ffn2_norm, m_ffn2_w_in, m_ffn2_w_out, m_ple_norm, m_w_ple_gate, m_w_ple_proj, m_final_norm, v_ffn1_norm, v_ffn1_w_in, v_ffn1_w_out, v_mix_norm, v_w_mix_in, v_conv_w, v_w_conv_out, v_w_attn_out, v_w_mix_out, v_ffn2_norm, v_ffn2_w_in, v_ffn2_w_out, v_ple_norm, v_w_ple_gate, v_w_ple_proj, v_final_norm):
    ws = (ffn1_norm, ffn1_w_in, ffn1_w_out, mix_norm, w_mix_in, conv_w, w_conv_out, w_attn_out, w_mix_out, ffn2_norm,
          ffn2_w_in, ffn2_w_out, ple_norm, w_ple_gate, w_ple_proj, final_norm)
    ms = (m_ffn1_norm, m_ffn1_w_in, m_ffn1_w_out, m_mix_norm, m_w_mix_in, m_conv_w, m_w_conv_out, m_w_attn_out,
          m_w_mix_out, m_ffn2_norm, m_ffn2_w_in, m_ffn2_w_out, m_ple_norm, m_w_ple_gate, m_w_ple_proj, m_final_norm)
    vs = (v_ffn1_norm, v_ffn1_w_in, v_ffn1_w_out, v_mix_norm, v_w_mix_in, v_conv_w, v_w_conv_out, v_w_attn_out,
          v_w_mix_out, v_ffn2_norm, v_ffn2_w_in, v_ffn2_w_out, v_ple_norm, v_w_ple_gate, v_w_ple_proj, v_final_norm)
    assert x.shape[0] == 1 and p.shape[:2] == (1, 1), "one sequence and one layer per device"

    def strip(a):
        return a[0] if a.ndim == 3 or (a.ndim == 2 and a.shape[0] == 1) else a

    w = {k: strip(a) for k, a in zip(WEIGHTS, ws)}
    m = {k: strip(a) for k, a in zip(WEIGHTS, ms)}
    v = {k: strip(a) for k, a in zip(WEIGHTS, vs)}
    loss, dx, grad, delta, new_m, new_v = _step(x[0], p[0, 0], loss_target[0], w, m, v)
    shapes = [a.shape for a in ws]
    outs = [loss, dx[None]]
    for res in (grad, delta, new_m, new_v):
        outs += [res[k].reshape(s) for k, s in zip(WEIGHTS, shapes)]
    return tuple(outs)
```

```python
import functools
import math

import jax
import jax.numpy as jnp
from jax import lax
from jax.experimental import pallas as pl
from jax.experimental.pallas import tpu as pltpu

F32 = jnp.float32
BF16 = jnp.bfloat16
MESH = pl.DeviceIdType.MESH
ANY = pl.BlockSpec(memory_space=pl.ANY)

HEAD_DIM = 128
NORM_EPS = 1e-6
N_CHIPS = 4
N_DEV = 8
BF16_ROWS = 16
VMEM_LIMIT = 56 * 1024 * 1024
ACC_BYTES = 8 * 1024 * 1024
STICK_EXIT = 110.0

ADAM_LR = 0.001
ADAM_B1 = 0.9
ADAM_B2 = 0.999
ADAM_EPS = 1e-08
ADAM_WD = 0.01
ADAM_STEP = 10

NN = (((1,), (0,)), ((), ()))
NT = (((1,), (1,)), ((), ()))
TN = (((0,), (0,)), ((), ()))


def _params(sem=None, **kw):
    if sem is not None:
        kw["dimension_semantics"] = sem
    return pltpu.CompilerParams(vmem_limit_bytes=VMEM_LIMIT, **kw)


def _pcall(body, **kw):
    return pl.pallas_call(body, **kw)


def _tile(n, pref, mult=8):
    best = None
    for d in range(mult, min(n, pref) + 1, mult):
        if n % d == 0:
            best = d
    return best if best is not None else n


def _dot(a, b, dims):
    return lax.dot_general(a, b, dims, preferred_element_type=F32)


def _call(name, body, grid, in_specs, out_specs, out_shape, args, scratch=(), sem=None, comm=None):
    n_in, n_out, n_sc = len(in_specs), len(out_specs), len(scratch)
    if comm is None:
        def plain(*refs):
            body(refs[:n_in], refs[n_in:n_in + n_out], refs[n_in + n_out:])

        return _pcall(plain, name=name, out_shape=list(out_shape), grid=grid, in_specs=list(in_specs),
                      out_specs=list(out_specs), scratch_shapes=list(scratch), compiler_params=_params(sem))(*args)
    n_cin, n_cout = len(comm.ins), len(comm.outs)
    steps = math.prod(grid)

    def hosted(*refs):
        ins, c_ins = refs[:n_in], refs[n_in:n_in + n_cin]
        outs = refs[n_in + n_cin:n_in + n_cin + n_out]
        c_outs = refs[n_in + n_cin + n_out:n_in + n_cin + n_out + n_cout]
        rest = refs[n_in + n_cin + n_out + n_cout:]
        sems = rest[n_sc:]
        step = pl.program_id(0)
        for ax in range(1, len(grid)):
            step = step * grid[ax] + pl.program_id(ax)

        @pl.when(step == 0)
        def _():
            comm.first(c_ins, c_outs, sems)

        body(ins, outs, rest[:n_sc])

        @pl.when(step == (3 * steps) // 4)
        def _():
            comm.mid(c_ins, c_outs, sems)

        @pl.when(step == steps - 1)
        def _():
            comm.last(c_ins, c_outs, sems)

    res = _pcall(hosted, name=name, out_shape=list(out_shape) + comm.outs, grid=grid,
                 in_specs=list(in_specs) + [ANY] * n_cin, out_specs=list(out_specs) + [ANY] * n_cout,
                 input_output_aliases={n_in + k: n_out + v for k, v in comm.aliases.items()},
                 scratch_shapes=list(scratch) + comm.sems,
                 compiler_params=_params(("arbitrary",) * len(grid)))(*args, *comm.ins)
    return list(res[:n_out]), list(res[n_out:])


NORM_CHUNK = 256


def _norm_bwd_tile(read_dn, rows, first, h_ref, g_ref, dr_ref, dh_ref, dhb_ref, dg_ref, alpha):
    @pl.when(first)
    def _():
        dg_ref[...] = jnp.zeros_like(dg_ref)

    gv = g_ref[...]
    tot = jnp.zeros_like(gv)
    for c0 in range(0, rows, NORM_CHUNK):
        sl = slice(c0, min(rows, c0 + NORM_CHUNK))
        hv = h_ref[sl, :]
        rs = _rstd(hv)
        hn = hv * rs
        dnv = read_dn(sl)
        gy = dnv * gv
        dh = dr_ref[sl, :] + rs * (gy - hn * jnp.mean(gy * hn, axis=-1, keepdims=True))
        dh_ref[sl, :] = dh
        dhb_ref[sl, :] = (alpha * dh).astype(BF16)
        tot = tot + jnp.sum(dnv * hn, axis=0, keepdims=True)
    dg_ref[...] += tot


def _mm(name, a, b, out_sds, grid, a_spec, b_spec, o_spec, dims, acc_shape, res=None, alpha=1.0, comm=None,
        norm=None, gain=None):
    nk = grid[2]

    def body(ins, outs, scratch):
        a_ref, b_ref = ins[:2]
        r_ref = ins[2] if res is not None else None
        o_ref = outs[0]
        if gain is not None:
            n_ref = scratch[-1]

            @pl.when(jnp.logical_and(pl.program_id(1) == 0, pl.program_id(2) == 0))
            def _():
                hv = a_ref[...]
                n_ref[...] = (hv * _rstd(hv) * ins[-1][...]).astype(BF16)
                outs[-1][...] = n_ref[...]

            a_ref = n_ref

        def finish(read):
            if norm is not None:
                first = jnp.logical_and(pl.program_id(0) == 0, pl.program_id(1) == 0)
                _norm_bwd_tile(read, o_ref.shape[0], first, *ins[2:5], *outs, alpha)
                return
            r = read(slice(None))
            if alpha != 1.0:
                r = r * alpha
            if r_ref is not None:
                r = r_ref[...] + r
            if len(o_ref.shape) == 3:
                half = o_ref.shape[1]
                o_ref[0] = r[:half].astype(o_ref.dtype)
                o_ref[1] = r[half:].astype(o_ref.dtype)
            else:
                o_ref[...] = r.astype(o_ref.dtype)

        if nk == 1:
            part = _dot(a_ref[...].astype(BF16), b_ref[...].astype(BF16), dims)
            finish(lambda sl: part[sl])
        else:
            acc_ref = scratch[0]
            kk = pl.program_id(2)

            @pl.when(kk == 0)
            def _():
                acc_ref[...] = jnp.zeros_like(acc_ref)

            acc_ref[...] += _dot(a_ref[...].astype(BF16), b_ref[...].astype(BF16), dims)

            @pl.when(kk == nk - 1)
            def _():
                finish(lambda sl: acc_ref[sl, :])

    in_specs = [a_spec, b_spec]
    args = [a, b]
    out_specs, out_shape = [o_spec], [out_sds]
    sem = ("parallel", "parallel", "arbitrary")
    if res is not None:
        in_specs.append(o_spec)
        args.append(res)
    if norm is not None:
        width = out_sds.shape[1]
        whole = pl.BlockSpec((1, width), lambda i, j, r: (0, 0))
        in_specs += [o_spec, whole, o_spec]
        args += list(norm)
        out_specs = [o_spec, o_spec, whole]
        out_shape = [jax.ShapeDtypeStruct(out_sds.shape, F32), jax.ShapeDtypeStruct(out_sds.shape, BF16),
                     jax.ShapeDtypeStruct((1, width), F32)]
        sem = ("arbitrary", "arbitrary", "arbitrary")
    scratch = [] if nk == 1 else [pltpu.VMEM(acc_shape, F32)]
    if gain is not None:
        in_specs.append(pl.BlockSpec((1, a.shape[1]), lambda i, j, r: (0, 0)))
        args.append(gain)
        out_specs.append(a_spec)
        out_shape.append(jax.ShapeDtypeStruct(a.shape, BF16))
        scratch.append(pltpu.VMEM(a_spec.block_shape, BF16))
        sem = ("parallel", "arbitrary", "arbitrary")
    got = _call(name, body, grid, in_specs, out_specs, out_shape, args, scratch, sem, comm)
    if norm is not None or gain is not None:
        return got if comm is None else (got[0], got[1])
    return got[0] if comm is None else (got[0][0], got[1])


def ffn_in_act(name, n, w4, tm, comm=None, gain=None):
    t, d = n.shape
    cs = w4.shape[2]

    def body(ins, outs, scratch):
        wg_ref, wu_ref = ins[-2:]
        a_ref, s_ref = outs[:2]
        if gain is None:
            nv = ins[0][...]
        else:
            hv = ins[0][...]
            nv = (hv * _rstd(hv) * ins[1][...]).astype(BF16)

            @pl.when(pl.program_id(0) == 0)
            def _():
                outs[2][...] = nv
        gate = _dot(nv, wg_ref[...], NN)
        up = _dot(nv, wu_ref[...], NN)
        a_ref[0] = gate.astype(BF16)
        a_ref[1] = up.astype(BF16)
        s_ref[...] = (gate * jax.nn.sigmoid(gate) * up).astype(BF16)

    rows = pl.BlockSpec((tm, d), lambda j, i: (i, 0))
    in_specs = [rows] + ([] if gain is None else [pl.BlockSpec((1, d), lambda j, i: (0, 0))])
    in_specs += [pl.BlockSpec((None, d, cs), lambda j, i: (j, 0, 0)),
                 pl.BlockSpec((None, d, cs), lambda j, i: (2 + j, 0, 0))]
    out_specs = [pl.BlockSpec((2, tm, cs), lambda j, i: (0, i, j)), pl.BlockSpec((tm, cs), lambda j, i: (i, j))]
    out_shape = [jax.ShapeDtypeStruct((2, t, 2 * cs), BF16), jax.ShapeDtypeStruct((t, 2 * cs), BF16)]
    if gain is not None:
        out_specs.append(pl.BlockSpec((tm, d), lambda j, i: (jnp.where(j == 0, i, t // tm - 1), 0)))
        out_shape.append(jax.ShapeDtypeStruct((t, d), BF16))
    got = _call(name, body, (2, t // tm), in_specs, out_specs, out_shape,
                [n] + ([] if gain is None else [gain]) + [w4, w4], (), ("arbitrary", "arbitrary"), comm)
    return got if comm is None else (got[0], got[1])


def ffn_ds_dact(name, df, w_out, a3, tm):
    t, d = df.shape
    f = w_out.shape[0]
    cs = f // 2

    def body(ins, outs, scratch):
        df_ref, w_ref, a_ref = ins
        ds = _dot(df_ref[...], w_ref[...], NT)
        for c0 in range(0, tm, NORM_CHUNK):
            sl = slice(c0, min(tm, c0 + NORM_CHUNK))
            gate = a_ref[0, sl, :].astype(F32)
            up = a_ref[1, sl, :].astype(F32)
            sg = jax.nn.sigmoid(gate)
            outs[0][0, sl, :] = (ds[sl] * up * sg * (1.0 + gate * (1.0 - sg))).astype(BF16)
            outs[0][1, sl, :] = (ds[sl] * gate * sg).astype(BF16)

    blk = pl.BlockSpec((2, tm, cs), lambda i, j: (0, i, j))
    return _call(name, body, (t // tm, 2),
                 [pl.BlockSpec((tm, d), lambda i, j: (i, 0)), pl.BlockSpec((cs, d), lambda i, j: (j, 0)), blk],
                 [blk], [jax.ShapeDtypeStruct((2, t, f), BF16)], [df, w_out, a3], (), ("parallel", "parallel"))[0]


def _part_ranges(parts, d):
    out, lo = [], 0
    for p in parts:
        out.append((lo, p.shape[1] // d))
        lo += p.shape[1] // d
    return out, lo


def mm_nt_parts(name, parts, w4, tm, norm, alpha, comm=None):
    m = parts[0].shape[0]
    d, cs = w4.shape[1], w4.shape[2]
    per = cs // d
    ranges, nblk = _part_ranges(parts, d)
    np_ = len(parts)
    nt = m // tm
    chunk = tm // nblk

    def body(ins, outs, scratch):
        w_ref, acc = ins[np_], scratch[0]
        i, r = pl.program_id(0), pl.program_id(1)

        @pl.when(jnp.logical_and(i < nt, r == 0))
        def _():
            acc[i % 2] = jnp.zeros(acc.shape[1:], F32)

        for (lo, n), a_ref in zip(ranges, ins[:np_]):
            @pl.when(jnp.logical_and(i < nt, jnp.logical_and(r >= lo, r < lo + n)))
            def _(a_ref=a_ref):
                acc[i % 2] += _dot(a_ref[...], w_ref[...], NT)

        @pl.when(i > 0)
        def _():
            rows = pl.ds(pl.multiple_of(r * chunk, chunk), chunk)
            first = jnp.logical_and(i == 1, r == 0)
            _norm_bwd_tile(lambda sl: acc[(i - 1) % 2, rows, :][sl], chunk, first, *ins[np_ + 1:], *outs, alpha)

    def ahead(i, r):
        return jnp.where(i < nt, r, nblk - 1)

    rows = pl.BlockSpec((chunk, d), lambda i, r: (jnp.where(i == 0, 0, (i - 1) * nblk + r), 0))
    whole = pl.BlockSpec((1, d), lambda i, r: (0, 0))
    specs = [pl.BlockSpec((tm, d), lambda i, r, lo=lo, n=n: (jnp.minimum(i, nt - 1), jnp.clip(ahead(i, r) - lo, 0, n - 1)))
             for lo, n in ranges]
    specs += [pl.BlockSpec((None, d, d), lambda i, r: (ahead(i, r) // per, 0, ahead(i, r) % per)), rows, whole, rows]
    got = _call(name, body, (nt + 1, nblk), specs, [rows, rows, whole],
                [jax.ShapeDtypeStruct((m, d), F32), jax.ShapeDtypeStruct((m, d), BF16),
                 jax.ShapeDtypeStruct((1, d), F32)],
                list(parts) + [w4] + list(norm), [pltpu.VMEM((2, tm, d), F32)], ("arbitrary", "arbitrary"), comm)
    return got if comm is None else (got[0], got[1])


def mm_tn_parts(name, xa, parts, tt, comm=None):
    t, k = xa.shape
    d = k
    pr = k // 2
    ranges, nblk = _part_ranges(parts, d)
    per = nblk // N_CHIPS

    def body(ins, outs, scratch):
        x_ref, acc = ins[0], scratch[0]
        jb, r = pl.program_id(0), pl.program_id(1)

        @pl.when(r == 0)
        def _():
            acc[...] = jnp.zeros_like(acc)

        for (lo, n), p_ref in zip(ranges, ins[1:]):
            @pl.when(jnp.logical_and(jb >= lo, jb < lo + n))
            def _(p_ref=p_ref):
                acc[...] += _dot(x_ref[...], p_ref[...], TN)

        @pl.when(r == t // tt - 1)
        def _():
            outs[0][0] = acc[:pr].astype(BF16)
            outs[0][1] = acc[pr:].astype(BF16)

    def part_spec(lo, n):
        return pl.BlockSpec((tt, d), lambda jb, r: (jnp.where(jnp.logical_and(jb >= lo, jb < lo + n), r, 0),
                                                    jnp.clip(jb - lo, 0, n - 1)))

    specs = [pl.BlockSpec((tt, k), lambda jb, r: (r, 0))] + [part_spec(lo, n) for lo, n in ranges]
    got = _call(name, body, (nblk, t // tt), specs,
                [pl.BlockSpec((None, 2, pr, d), lambda jb, r: (jb // per, 0, 0, jb % per))],
                [jax.ShapeDtypeStruct((N_CHIPS, 2, pr, per * d), BF16)], [xa] + list(parts),
                [pltpu.VMEM((k, d), F32)], ("parallel", "arbitrary"), comm)
    return got[0] if comm is None else (got[0][0], got[1])


def mm_nn(name, a, w, out_dtype, tm, res=None, alpha=1.0):
    m, k = a.shape
    n = w.shape[1]
    return _mm(name, a, w, jax.ShapeDtypeStruct((m, n), out_dtype), (m // tm, 1, 1),
               pl.BlockSpec((tm, k), lambda i, j, r: (i, 0)),
               pl.BlockSpec((k, n), lambda i, j, r: (0, 0)),
               pl.BlockSpec((tm, n), lambda i, j, r: (i, 0)), NN, None, res=res, alpha=alpha)


def mm_nn_stacked(name, a, w4, out_dtype, tm, tn, j0=0, nj=None, comm=None, gain=None):
    m, k = a.shape
    cs = w4.shape[2]
    per = cs // tn
    nj = N_CHIPS * per - j0 if nj is None else nj
    return _mm(name, a, w4, jax.ShapeDtypeStruct((m, nj * tn), out_dtype), (m // tm, nj, 1),
               pl.BlockSpec((tm, k), lambda i, j, r: (i, 0)),
               pl.BlockSpec((None, k, tn), lambda i, j, r: ((j + j0) // per, 0, (j + j0) % per)),
               pl.BlockSpec((tm, tn), lambda i, j, r: (i, j)), NN, None, comm=comm, gain=gain)


def mm_nt(name, dy, w, out_dtype, tm, tko, norm=None, alpha=1.0):
    m, n = dy.shape
    k = w.shape[0]
    return _mm(name, dy, w, jax.ShapeDtypeStruct((m, k), out_dtype), (m // tm, k // tko, 1),
               pl.BlockSpec((tm, n), lambda i, j, r: (i, 0)),
               pl.BlockSpec((tko, n), lambda i, j, r: (j, 0)),
               pl.BlockSpec((tm, tko), lambda i, j, r: (i, j)), NT, None, norm=norm, alpha=alpha)


def mm_nt_stacked(name, dy, w4, tm, norm, alpha=1.0, comm=None):
    m = dy.shape[1]
    k, cs = w4.shape[1], w4.shape[2]
    nt = m // tm
    chunk = tm // N_CHIPS

    def body(ins, outs, scratch):
        dy_ref, w_ref, h_ref, g_ref, dr_ref = ins
        dh_ref, dhb_ref, dg_ref = outs
        acc = scratch[0]
        i, r = pl.program_id(0), pl.program_id(1)

        @pl.when(i < nt)
        def _():
            part = _dot(dy_ref[...], w_ref[...], NT)

            @pl.when(r == 0)
            def _():
                acc[i % 2] = part

            @pl.when(r > 0)
            def _():
                acc[i % 2] += part

        @pl.when(i > 0)
        def _():
            rows = pl.ds(pl.multiple_of(r * chunk, chunk), chunk)
            first = jnp.logical_and(i == 1, r == 0)
            _norm_bwd_tile(lambda sl: acc[(i - 1) % 2, rows, :][sl], chunk, first, h_ref, g_ref, dr_ref,
                           dh_ref, dhb_ref, dg_ref, alpha)

    def behind(i, r):
        return (jnp.where(i == 0, 0, (i - 1) * N_CHIPS + r), 0)

    def ahead(i, r):
        return jnp.where(i < nt, r, N_CHIPS - 1)

    rows = pl.BlockSpec((chunk, k), behind)
    whole = pl.BlockSpec((1, k), lambda i, r: (0, 0))
    got = _call(name, body, (nt + 1, N_CHIPS),
                [pl.BlockSpec((None, tm, cs), lambda i, r: (ahead(i, r) // 2, jnp.minimum(i, nt - 1), ahead(i, r) % 2)),
                 pl.BlockSpec((None, k, cs), lambda i, r: (ahead(i, r), 0, 0)), rows, whole, rows],
                [rows, rows, whole],
                [jax.ShapeDtypeStruct((m, k), F32), jax.ShapeDtypeStruct((m, k), BF16),
                 jax.ShapeDtypeStruct((1, k), F32)],
                [dy, w4] + list(norm), [pltpu.VMEM((2, tm, k), F32)], ("arbitrary", "arbitrary"), comm)
    return got if comm is None else (got[0], got[1])


def mm_tn_rows(name, xa, dy, tt):
    t, k = xa.shape
    n = dy.shape[1]
    tkr = k if k * n * 4 <= ACC_BYTES else k // 2
    return _mm(name, xa, dy, jax.ShapeDtypeStruct((k, n), BF16), (k // tkr, 1, t // tt),
               pl.BlockSpec((tt, tkr), lambda i, j, r: (r, i)),
               pl.BlockSpec((tt, n), lambda i, j, r: (r, 0)),
               pl.BlockSpec((tkr, n), lambda i, j, r: (i, 0)), TN, (tkr, n))


def mm_tn_whole(name, xa, dy, tt):
    t, k = xa.shape
    n = dy.shape[1]
    return _mm(name, xa, dy, jax.ShapeDtypeStruct((k, n), BF16), (1, 1, t // tt),
               pl.BlockSpec((tt, k), lambda i, j, r: (r, 0)),
               pl.BlockSpec((tt, n), lambda i, j, r: (r, 0)),
               pl.BlockSpec((k, n), lambda i, j, r: (0, 0)), TN, (k, n))


def mm_tn_cols(name, xa, dy, tt, comm=None):
    t, k = xa.shape
    pr = k // 2
    if dy.ndim == 3:
        cs = dy.shape[2] // 2
        dy_spec = pl.BlockSpec((None, tt, cs), lambda i, j, r: (j // 2, r, j % 2))
    else:
        cs = dy.shape[1] // N_CHIPS
        dy_spec = pl.BlockSpec((tt, cs), lambda i, j, r: (r, j))
    return _mm(name, xa, dy, jax.ShapeDtypeStruct((N_CHIPS, 2, pr, cs), BF16), (1, N_CHIPS, t // tt),
               pl.BlockSpec((tt, k), lambda i, j, r: (r, 0)), dy_spec,
               pl.BlockSpec((None, 2, pr, cs), lambda i, j, r: (j, 0, 0, 0)), TN, (k, cs), comm=comm)


def _rows(tt, w, col=0):
    return pl.BlockSpec((tt, w), lambda i: (i, col))


def _whole(shape):
    return pl.BlockSpec(shape, lambda i: (0,) * len(shape))


def _rstd(h):
    return lax.rsqrt(jnp.mean(h * h, axis=-1, keepdims=True) + NORM_EPS)


def rms_fwd(name, h, g, tt, comm=None):
    t, d = h.shape

    def body(ins, outs, scratch):
        hv = ins[0][...]
        outs[0][...] = (hv * _rstd(hv) * ins[1][...]).astype(BF16)

    got = _call(name, body, (t // tt,), [_rows(tt, d), _whole((1, d))], [_rows(tt, d)],
                [jax.ShapeDtypeStruct((t, d), BF16)], [h, g], (), ("parallel",), comm)
    return got[0] if comm is None else (got[0][0], got[1])


def mix_out_fwd(name, gates, yc, ya, h, w, tt):
    t, d = yc.shape

    def body(g_ref, yc_ref, ya_ref, h_ref, w_ref, m_ref, o_ref):
        merged = (jax.nn.sigmoid(g_ref[:, :d].astype(F32)) * yc_ref[...].astype(F32)
                  + jax.nn.sigmoid(g_ref[:, d:].astype(F32)) * ya_ref[...].astype(F32)).astype(BF16)
        m_ref[...] = merged
        o_ref[...] = h_ref[...] + _dot(merged, w_ref[...], NN)

    return _pcall(body, name=name,
                  out_shape=(jax.ShapeDtypeStruct((t, d), BF16), jax.ShapeDtypeStruct((t, d), F32)),
                  grid=(t // tt,),
                  in_specs=[_rows(tt, 2 * d), _rows(tt, d), _rows(tt, d), _rows(tt, d), _whole((d, d))],
                  out_specs=(_rows(tt, d), _rows(tt, d)),
                  compiler_params=_params(("parallel",)))(gates, yc, ya, h, w)


def mix_out_bwd(name, dh, w, gates, yc, ya, tt):
    t, d = yc.shape

    def body(dh_ref, w_ref, g_ref, yc_ref, ya_ref, dyc_ref, dya_ref, dg_ref):
        dmv = _dot(dh_ref[...], w_ref[...], NT)
        sc = jax.nn.sigmoid(g_ref[:, :d].astype(F32))
        sa = jax.nn.sigmoid(g_ref[:, d:].astype(F32))
        dyc_ref[...] = (dmv * sc).astype(BF16)
        dya_ref[...] = (dmv * sa).astype(BF16)
        dg_ref[:, :d] = (dmv * yc_ref[...].astype(F32) * sc * (1.0 - sc)).astype(BF16)
        dg_ref[:, d:] = (dmv * ya_ref[...].astype(F32) * sa * (1.0 - sa)).astype(BF16)

    return _pcall(body, name=name,
                  out_shape=(jax.ShapeDtypeStruct((t, d), BF16), jax.ShapeDtypeStruct((t, d), BF16),
                             jax.ShapeDtypeStruct((t, 2 * d), BF16)),
                  grid=(t // tt,),
                  in_specs=[_rows(tt, d), _whole((d, d)), _rows(tt, 2 * d), _rows(tt, d), _rows(tt, d)],
                  out_specs=(_rows(tt, d), _rows(tt, d), _rows(tt, 2 * d)),
                  compiler_params=_params(("parallel",)))(dh, w, gates, yc, ya)


def _shift_down(cur, prev8, s):
    tt = cur.shape[0]
    rolled = pltpu.roll(cur, s, 0)
    row8 = lax.broadcasted_iota(jnp.int32, prev8.shape, 0)
    first8 = jnp.where(row8 < s, pltpu.roll(prev8, s, 0), rolled[:8])
    return jnp.concatenate([first8, rolled[8:]], axis=0) if tt > 8 else first8


def _shift_up(cur, next8, s):
    tt = cur.shape[0]
    rolled = pltpu.roll(cur, tt - s, 0)
    row8 = lax.broadcasted_iota(jnp.int32, next8.shape, 0)
    last8 = jnp.where(row8 >= 8 - s, pltpu.roll(next8, 8 - s, 0), rolled[tt - 8:])
    return jnp.concatenate([rolled[:tt - 8], last8], axis=0) if tt > 8 else last8


def _prev_rows(tt, d, col):
    return pl.BlockSpec((BF16_ROWS, d), lambda i: (jnp.maximum(i * (tt // BF16_ROWS) - 1, 0), col))


def _next_rows(tt, d, col, t):
    return pl.BlockSpec((BF16_ROWS, d),
                        lambda i: (jnp.minimum((i + 1) * (tt // BF16_ROWS), t // BF16_ROWS - 1), col))


def conv_out_fwd(name, cbx, cw8, w_out, tt):
    t, d3 = cbx.shape
    d = d3 // 3

    def body(cb_ref, cc_ref, cx_ref, pc_ref, px_ref, w_ref, wo_ref, o_ref, y_ref):
        has_prev = (pl.program_id(0) > 0).astype(F32)
        cc = cc_ref[...].astype(F32) * cx_ref[...].astype(F32)
        prev = pc_ref[...].astype(F32)[8:] * px_ref[...].astype(F32)[8:] * has_prev
        w = w_ref[...]
        conv = w[0:1] * _shift_down(cc, prev, 2) + w[1:2] * _shift_down(cc, prev, 1) + w[2:3] * cc
        ycin = (cb_ref[...].astype(F32) * conv).astype(BF16)
        o_ref[...] = ycin
        y_ref[...] = _dot(ycin, wo_ref[...], NN).astype(BF16)

    out = jax.ShapeDtypeStruct((t, d), BF16)
    return _pcall(body, name=name, out_shape=(out, out), grid=(t // tt,),
                  in_specs=[_rows(tt, d, 0), _rows(tt, d, 1), _rows(tt, d, 2), _prev_rows(tt, d, 1),
                            _prev_rows(tt, d, 2), _whole((8, d)), _whole((d, d))],
                  out_specs=(_rows(tt, d), _rows(tt, d)),
                  compiler_params=_params(("parallel",)))(cbx, cbx, cbx, cbx, cbx, cw8, w_out)


def conv_out_bwd(name, dyc, w_out, cbx, cw8, tt):
    t, d3 = cbx.shape
    d = d3 // 3
    n = t // tt

    def body(dy_ref, ndy_ref, wo_ref, cb_ref, cc_ref, cx_ref, pc_ref, px_ref, ncb_ref, w_ref, o_ref, dw_ref):
        i = pl.program_id(0)
        has_prev = (i > 0).astype(F32)
        has_next = (i < n - 1).astype(F32)
        cb = cb_ref[...].astype(F32)
        ccv = cc_ref[...].astype(F32)
        cxv = cx_ref[...].astype(F32)
        cc = ccv * cxv
        prev = pc_ref[...].astype(F32)[8:] * px_ref[...].astype(F32)[8:] * has_prev
        w = w_ref[...]
        cc1 = _shift_down(cc, prev, 1)
        cc2 = _shift_down(cc, prev, 2)
        conv = w[0:1] * cc2 + w[1:2] * cc1 + w[2:3] * cc
        dyv = _dot(dy_ref[...], wo_ref[...], NT)
        dconv = dyv * cb
        dnext = _dot(ndy_ref[...], wo_ref[...], NT)[:8] * ncb_ref[...].astype(F32)[:8] * has_next
        dcc = w[2:3] * dconv + w[1:2] * _shift_up(dconv, dnext, 1) + w[0:1] * _shift_up(dconv, dnext, 2)
        o_ref[:, :d] = (dyv * conv).astype(BF16)
        o_ref[:, d:2 * d] = (dcc * cxv).astype(BF16)
        o_ref[:, 2 * d:] = (dcc * ccv).astype(BF16)

        @pl.when(i == 0)
        def _():
            dw_ref[...] = jnp.zeros_like(dw_ref)

        dw_ref[0:1, :] += jnp.sum(dconv * cc2, axis=0, keepdims=True)
        dw_ref[1:2, :] += jnp.sum(dconv * cc1, axis=0, keepdims=True)
        dw_ref[2:3, :] += jnp.sum(dconv * cc, axis=0, keepdims=True)

    return _pcall(body, name=name,
                  out_shape=(jax.ShapeDtypeStruct((t, d3), BF16), jax.ShapeDtypeStruct((8, d), F32)),
                  grid=(n,),
                  in_specs=[_rows(tt, d), _next_rows(tt, d, 0, t),
                            _whole((d, d)), _rows(tt, d, 0), _rows(tt, d, 1), _rows(tt, d, 2),
                            _prev_rows(tt, d, 1), _prev_rows(tt, d, 2), _next_rows(tt, d, 0, t), _whole((8, d))],
                  out_specs=(_rows(tt, d3), _whole((8, d))),
                  compiler_params=_params(("arbitrary",)))(dyc, dyc, w_out, cbx, cbx, cbx, cbx, cbx, cbx, cw8)


def tail(name, h3, p, tgt, gp, gf, w_gate, w_proj, tt):
    t, d = h3.shape
    pd = p.shape[1]

    def body(h_ref, p_ref, tg_ref, gp_ref, gf_ref, wg_ref, wp_ref, np_ref, dh_ref, dpp_ref, dzg_ref, dgf_ref,
             loss_ref):
        hv = h_ref[...]
        npl = (hv * _rstd(hv) * gp_ref[...]).astype(BF16)
        np_ref[...] = npl
        pg = jax.nn.sigmoid(_dot(npl, wg_ref[...], NN))
        ppv = _dot(p_ref[...].astype(BF16), wp_ref[...], NN)
        h4 = hv + pg * ppv
        r4 = _rstd(h4)
        hn = h4 * r4
        gfv = gf_ref[...]
        err = hn * gfv - tg_ref[...]
        dy = err * (1.0 / d)
        gy = dy * gfv
        dh4 = r4 * (gy - hn * jnp.mean(gy * hn, axis=-1, keepdims=True))
        dh_ref[...] = dh4
        dpp_ref[...] = (dh4 * pg).astype(BF16)
        dzg_ref[...] = (dh4 * ppv * pg * (1.0 - pg)).astype(BF16)

        @pl.when(pl.program_id(0) == 0)
        def _():
            dgf_ref[...] = jnp.zeros_like(dgf_ref)
            loss_ref[...] = jnp.zeros_like(loss_ref)

        dgf_ref[...] += jnp.sum(dy * hn, axis=0, keepdims=True)
        tok = jnp.mean(err * err, axis=-1, keepdims=True)
        loss_ref[...] += 0.5 * jnp.sum(tok, axis=0, keepdims=True) * jnp.ones((1, loss_ref.shape[1]), F32)

    return _pcall(body, name=name,
                  out_shape=(jax.ShapeDtypeStruct((t, d), BF16), jax.ShapeDtypeStruct((t, d), F32),
                             jax.ShapeDtypeStruct((t, d), BF16), jax.ShapeDtypeStruct((t, d), BF16),
                             jax.ShapeDtypeStruct((1, d), F32), jax.ShapeDtypeStruct((1, d), F32)),
                  grid=(t // tt,),
                  in_specs=[_rows(tt, d), _rows(tt, pd), _rows(tt, d), _whole((1, d)), _whole((1, d)),
                            _whole((d, d)), _whole((pd, d))],
                  out_specs=(_rows(tt, d), _rows(tt, d), _rows(tt, d), _rows(tt, d), _whole((1, d)),
                             _whole((1, d))),
                  compiler_params=_params(("arbitrary",)))(h3, p, tgt, gp, gf, w_gate, w_proj)


SCALE = 1.0 / math.sqrt(HEAD_DIM)


def _log_stick(z):
    return -(jnp.maximum(z, 0.0) + jnp.log(1.0 + jnp.exp(-jnp.abs(z))))


def _tri_sum(x, tri):
    hi = x.astype(BF16)
    lo = (x - hi.astype(F32)).astype(BF16)
    return _dot(hi, tri, NN) + _dot(lo, tri, NN)


KEY_BLOCK = 128
NEAR = 3
THIN_ROWS = 32


def _pad_block(x):
    n = x.shape[0]
    return x if n == KEY_BLOCK else jnp.concatenate([x, jnp.zeros((KEY_BLOCK - n, x.shape[1]), x.dtype)], axis=0)


def _sb_near(qs, jds, k_ref, below, upper, last_rows):
    near_rows = (KEY_BLOCK,) * (NEAR - 1) + (last_rows,)
    pairs = [(s, b) for s in range(len(qs)) for b in range(NEAR)]
    rows = {(s, b): _block_rows(jnp.maximum(jds[s] - b, 0), KEY_BLOCK) for s, b in pairs}
    z = {(s, b): _dot(qs[s][:near_rows[b]], k_ref[rows[s, b], :], NT) * SCALE for s, b in pairs}
    lg = {(s, b): jnp.where(below, _log_stick(z[s, b]), 0.0) if b == 0 else _log_stick(z[s, b]) for s, b in pairs}
    cum = {(s, b): _tri_sum(lg[s, b], upper) for s, b in pairs}
    out, carries = [], []
    for s in range(len(qs)):
        c = cum[s, 0][:, 0:1]
        blocks = [(rows[s, 0], z[s, 0], jnp.exp(jnp.where(below, z[s, 0] + cum[s, 0], -1e30)))]
        for b in range(1, NEAR):
            live = jds[s] >= b
            off = c[:near_rows[b]] + jnp.where(live, 0.0, -1e30)
            blocks.append((rows[s, b], z[s, b], jnp.exp(z[s, b] + cum[s, b] + off)))
            c = c + _pad_block(jnp.where(live, cum[s, b][:, 0:1], 0.0))
        out.append(blocks)
        carries.append(c)
    return out, carries


def _sb_far(q, kj, upper, c, skip):
    z = _dot(q, kj, NT) * SCALE
    cum = _tri_sum(_log_stick(z), upper)
    return z, jnp.exp(z + cum + (c + jnp.where(skip, -1e30, 0.0))), c + jnp.where(skip, 0.0, cum[:, 0:1])


def _took_it(j, jd, last_rows):
    first = lax.broadcasted_iota(jnp.int32, (KEY_BLOCK, 1), 0) < last_rows
    return jnp.logical_and(j == jd - (NEAR - 1), first)


def _block_rows(j, size):
    return pl.ds(pl.multiple_of(j * size, size), size)


def _sweep_on(st):
    return jnp.logical_and(st[0] >= 0, jnp.max(st[1]) > -STICK_EXIT)


def attn_fwd(name, qkv, tq):
    t, d3 = qkv.shape
    d = d3 // 3
    nh = d // HEAD_DIM
    nq = t // tq
    tb = KEY_BLOCK
    nsub = tq // tb

    def body(q_ref, k_ref, v_ref, o_ref):
        i = pl.program_id(1)
        row = lax.broadcasted_iota(jnp.int32, (tb, tb), 0)
        col = lax.broadcasted_iota(jnp.int32, (tb, tb), 1)
        upper = (row >= col).astype(BF16)
        qs = [q_ref[s * tb:(s + 1) * tb, :] for s in range(nsub)]
        jds = [i * nsub + s for s in range(nsub)]
        near, carries = _sb_near(qs, jds, k_ref, col < row, upper, THIN_ROWS)
        state = []
        for s in range(nsub):
            acc = jnp.zeros((tb, HEAD_DIM), F32)
            for rows, _, a in near[s]:
                acc = acc + _pad_block(_dot(a.astype(BF16), v_ref[rows, :], NN))
            state.append((qs[s], jds[s], carries[s], acc))
        for s, (q, jd, c, acc) in enumerate(state):

            def step(st, q=q, jd=jd):
                rows = _block_rows(st[0], tb)
                _, a, c2 = _sb_far(q, k_ref[rows, :], upper, st[1], _took_it(st[0], jd, THIN_ROWS))
                return st[0] - 1, c2, st[2] + _dot(a.astype(BF16), v_ref[rows, :], NN)

            _, _, acc = lax.while_loop(_sweep_on, step, (jd - (NEAR - 1), c, acc))
            o_ref[s * tb:(s + 1) * tb, :] = acc.astype(o_ref.dtype)

    return _pcall(body, name=name, out_shape=jax.ShapeDtypeStruct((t, d), BF16), grid=(nh, nq),
                  in_specs=[pl.BlockSpec((tq, HEAD_DIM), lambda h, i: (i, h)),
                            pl.BlockSpec((t, HEAD_DIM), lambda h, i: (0, nh + h)),
                            pl.BlockSpec((t, HEAD_DIM), lambda h, i: (0, 2 * nh + h))],
                  out_specs=pl.BlockSpec((tq, HEAD_DIM), lambda h, i: (i, h)),
                  compiler_params=_params(("parallel", "arbitrary")))(qkv, qkv, qkv)


def attn_bwd(name, qkv, do, tq):
    t, d3 = qkv.shape
    d = d3 // 3
    nh = d // HEAD_DIM
    nq = t // tq
    tb = KEY_BLOCK
    nsub = tq // tb

    def body(q_ref, k_ref, v_ref, do_ref, dq_ref, dk_ref, dv_ref, dk_acc, dv_acc, g_buf, z_buf):
        i = pl.program_id(1)

        @pl.when(i == 0)
        def _():
            dk_acc[...] = jnp.zeros_like(dk_acc)
            dv_acc[...] = jnp.zeros_like(dv_acc)

        row = lax.broadcasted_iota(jnp.int32, (tb, tb), 0)
        col = lax.broadcasted_iota(jnp.int32, (tb, tb), 1)
        below = col < row
        upper = (row >= col).astype(BF16)
        lower = (row <= col).astype(BF16)

        qs = [q_ref[s * tb:(s + 1) * tb, :] for s in range(nsub)]
        dos = [do_ref[s * tb:(s + 1) * tb, :] for s in range(nsub)]
        jds = [i * nsub + s for s in range(nsub)]
        near, carries = _sb_near(qs, jds, k_ref, below, upper, KEY_BLOCK)
        da = [[_dot(dos[s][:a.shape[0]], v_ref[rows, :], NT) for rows, _, a in near[s]] for s in range(nsub)]
        state = []
        for s in range(nsub):
            kept = [(rows, z, da[s][b] * a) for b, (rows, z, a) in enumerate(near[s])]
            for rows, _, a in near[s]:
                dv_acc[rows, :] += _dot(a.astype(BF16), dos[s][:a.shape[0]], TN)
            state.append((qs[s], dos[s], jds[s], carries[s], kept))

        carried = []
        for s, (q, dov, jd, c, kept) in enumerate(state):
            def step(st, s=s, q=q, dov=dov, jd=jd):
                j = st[0]
                rows = _block_rows(j, tb)
                z, a, c2 = _sb_far(q, k_ref[rows, :], upper, st[1], _took_it(j, jd, KEY_BLOCK))
                g_buf[jd - j] = _dot(dov, v_ref[rows, :], NT) * a
                z_buf[jd - j] = z
                dv_acc[rows, :] += _dot(a.astype(BF16), dov, TN)
                return j - 1, c2

            j_stop, _ = lax.while_loop(_sweep_on, step, (jd - (NEAR - 1), c))

            def far(j, st, s=s, q=q, jd=jd):
                run, dq = st
                rows = _block_rows(j, tb)
                g = g_buf[jd - j]
                dz = (g - jax.nn.sigmoid(z_buf[jd - j]) * (run + _tri_sum(g, lower))).astype(BF16)
                dk_acc[rows, :] += _dot(dz, q, TN)
                return run + jnp.sum(g, axis=1, keepdims=True), dq + _dot(dz, k_ref[rows, :], NN)

            carried.append(lax.fori_loop(j_stop + 1, jd - (NEAR - 1) + 1, far,
                                         (jnp.zeros((tb, 1), F32), jnp.zeros((tb, HEAD_DIM), F32))))

        tri = [[_dot(g.astype(BF16), lower, NN) for _, _, g in st[4]] for st in state]
        sig = [[jax.nn.sigmoid(z) for _, z, _ in st[4]] for st in state]
        for s, (q, dov, jd, c, kept) in enumerate(state):
            run, dq = carried[s]
            for b in reversed(range(NEAR)):
                rows, z, g = kept[b]
                n = g.shape[0]
                dz = g - sig[s][b] * (run[:n] + tri[s][b])
                if b == 0:
                    dz = jnp.where(below, dz, 0.0)
                dz = dz.astype(BF16)
                dk_acc[rows, :] += _dot(dz, q[:n], TN)
                dq = dq + _pad_block(_dot(dz, k_ref[rows, :], NN))
                if b:
                    run = run + _pad_block(jnp.sum(g, axis=1, keepdims=True))
            dq_ref[s * tb:(s + 1) * tb, :] = (dq * SCALE).astype(BF16)

        @pl.when(i == nq - 1)
        def _():
            dk_ref[...] = (dk_acc[...] * SCALE).astype(BF16)
            dv_ref[...] = dv_acc[...].astype(BF16)

    blk = pl.BlockSpec((tq, HEAD_DIM), lambda h, i: (i, h))
    col_h = pl.BlockSpec((t, HEAD_DIM), lambda h, i: (0, h))
    out = jax.ShapeDtypeStruct((t, d), BF16)
    return _pcall(body, name=name, out_shape=(out, out, out), grid=(nh, nq),
                  in_specs=[blk,
                            pl.BlockSpec((t, HEAD_DIM), lambda h, i: (0, nh + h)),
                            pl.BlockSpec((t, HEAD_DIM), lambda h, i: (0, 2 * nh + h)),
                            blk],
                  out_specs=(blk, col_h, col_h),
                  scratch_shapes=[pltpu.VMEM((t, HEAD_DIM), F32), pltpu.VMEM((t, HEAD_DIM), F32),
                                  pltpu.VMEM((t // tb, tb, tb), F32), pltpu.VMEM((t // tb, tb, tb), F32)],
                  compiler_params=_params(("parallel", "arbitrary")))(qkv, qkv, qkv, do)


def _place():
    x, y, c = lax.axis_index("x"), lax.axis_index("y"), lax.axis_index("c")
    chips = [(1 - x, y), (x, 1 - y), (1 - x, 1 - y)]
    return x, y, c, chips


def _remote(src, dst, send_sem, recv_sem, dev):
    return pltpu.make_async_remote_copy(src_ref=src, dst_ref=dst, send_sem=send_sem, recv_sem=recv_sem,
                                        device_id=dev, device_id_type=MESH)


def place_shards(name, ws, chip):
    tiles, steps = _job_tiles([w.shape for w in ws], 1 << 20, BF16_ROWS)
    nj = len(ws)

    def body(chip_ref, *refs):
        i = pl.program_id(0)
        for k, (_, n) in enumerate(tiles):
            @pl.when(i < n)
            def _(w_ref=refs[k], o_ref=refs[nj + k]):
                o_ref[...] = w_ref[...].astype(BF16)

    spec = pltpu.PrefetchScalarGridSpec(
        num_scalar_prefetch=1, grid=(steps,),
        in_specs=[pl.BlockSpec((tr, w.shape[1]), lambda i, s, n=n: (jnp.minimum(i, n - 1), 0))
                  for w, (tr, n) in zip(ws, tiles)],
        out_specs=[pl.BlockSpec((None, tr, w.shape[1]), lambda i, s, n=n: (s[0], jnp.minimum(i, n - 1), 0))
                   for w, (tr, n) in zip(ws, tiles)])
    return _pcall(body, name=name, out_shape=[jax.ShapeDtypeStruct((N_CHIPS,) + w.shape, BF16) for w in ws],
                  grid_spec=spec, compiler_params=_params(("arbitrary",)))(chip, *ws)


class Comm:
    def __init__(self, ins, outs, aliases, sems, first, mid, last):
        self.ins, self.outs, self.aliases, self.sems = list(ins), list(outs), dict(aliases), list(sems)
        self.first, self.mid, self.last = first, mid, last


def run_comm(name, comm):
    ni, no = len(comm.ins), len(comm.outs)

    def body(*refs):
        ins, outs, sems = refs[:ni], refs[ni:ni + no], refs[ni + no:]
        comm.first(ins, outs, sems)
        comm.mid(ins, outs, sems)
        comm.last(ins, outs, sems)

    return _pcall(body, name=name, out_shape=comm.outs, in_specs=[ANY] * ni, out_specs=[ANY] * no,
                  input_output_aliases=comm.aliases, scratch_shapes=comm.sems, compiler_params=_params())(*comm.ins)


def gather_comm(bufs):
    n = len(bufs)

    def half(out, w, which):
        pr = out[w].shape[1] // 2
        return pl.ds(pl.multiple_of(which * pr, BF16_ROWS), pr)

    def first(ins, out, sems):
        isend, irecv, _, _ = sems
        x, y, c, chips = _place()
        for w in range(n):
            mine = out[w].at[2 * x + y, half(out, w, c)]
            for j, (cx, cy) in enumerate(chips):
                _remote(mine, mine, isend.at[3 * w + j], irecv.at[3 * w + j], (cx, cy, c)).start()

    def mid(ins, out, sems):
        isend, irecv, dsend, drecv = sems
        x, y, c, chips = _place()
        sib = (x, y, 1 - c)
        for w in range(n):
            for j, (cx, cy) in enumerate(chips):
                landed = out[w].at[2 * cx + cy, half(out, w, c)]
                _remote(landed, landed, isend.at[3 * w + j], irecv.at[3 * w + j], sib).wait_recv()
                _remote(landed, landed, dsend.at[3 * w + j], drecv.at[3 * w + j], sib).start()

    def last(ins, out, sems):
        isend, irecv, dsend, drecv = sems
        x, y, c, chips = _place()
        sib = (x, y, 1 - c)
        for w in range(n):
            for j, (cx, cy) in enumerate(chips):
                landed = out[w].at[2 * cx + cy, half(out, w, 1 - c)]
                _remote(landed, landed, dsend.at[3 * w + j], drecv.at[3 * w + j], sib).wait_recv()
        for w in range(n):
            sent = out[w].at[0, half(out, w, c)]
            for j in range(3):
                _remote(sent, sent, isend.at[3 * w + j], irecv.at[3 * w + j], sib).wait_send()
                _remote(sent, sent, dsend.at[3 * w + j], drecv.at[3 * w + j], sib).wait_send()

    return Comm(bufs, [jax.ShapeDtypeStruct(s.shape, s.dtype) for s in bufs], {w: w for w in range(n)},
                [pltpu.SemaphoreType.DMA((3 * n,))] * 4, first, mid, last)


def _nothing(ins, outs, sems):
    return None


def join_comms(a, b):
    ni, no, ns = len(a.ins), len(a.outs), len(a.sems)

    def both(f, g):
        def hook(ins, outs, sems):
            f(ins[:ni], outs[:no], sems[:ns])
            g(ins[ni:], outs[no:], sems[ns:])
        return hook

    aliases = dict(a.aliases)
    aliases.update({ni + k: no + v for k, v in b.aliases.items()})
    return Comm(a.ins + b.ins, a.outs + b.outs, aliases, a.sems + b.sems,
                both(a.first, b.first), both(a.mid, b.mid), both(a.last, b.last))


def exchange_comm(pieces):
    n = len(pieces)

    def copies(src, out, sems):
        x, y, c, _ = _place()
        return [_remote(src[w].at[k, 1 - c], out[w].at[k], sems[0].at[N_CHIPS * w + k], sems[1].at[N_CHIPS * w + k],
                        (x, y, 1 - c)) for w in range(n) for k in range(N_CHIPS)]

    def first(src, out, sems):
        for cp in copies(src, out, sems):
            cp.start()

    def last(src, out, sems):
        for cp in copies(src, out, sems):
            cp.wait()

    return Comm(pieces, [jax.ShapeDtypeStruct((N_CHIPS,) + s.shape[2:], s.dtype) for s in pieces], {},
                [pltpu.SemaphoreType.DMA((N_CHIPS * n,))] * 2, first, _nothing, last)


def scatter_comm(parts):
    n = len(parts)

    def copies(src, out, sems):
        x, y, c, chips = _place()
        return [_remote(src[w].at[2 * cx + cy], out[w].at[j], sems[0].at[3 * w + j], sems[1].at[3 * w + j], (cx, cy, c))
                for w in range(n) for j, (cx, cy) in enumerate(chips)]

    def first(src, out, sems):
        for cp in copies(src, out, sems):
            cp.start()

    def last(src, out, sems):
        for cp in copies(src, out, sems):
            cp.wait()

    return Comm(parts, [jax.ShapeDtypeStruct((3,) + s.shape[1:], s.dtype) for s in parts], {},
                [pltpu.SemaphoreType.DMA((3 * n,))] * 2, first, _nothing, last)


def share_comm(halves):
    n = len(halves)

    def first(ins, buf, sems):
        x, y, c, _ = _place()
        for w in range(n):
            _remote(buf[w].at[c], buf[w].at[c], sems[0].at[w], sems[1].at[w], (x, y, 1 - c)).start()

    def last(ins, buf, sems):
        x, y, c, _ = _place()
        for w in range(n):
            landed = buf[w].at[1 - c]
            _remote(landed, landed, sems[0].at[w], sems[1].at[w], (x, y, 1 - c)).wait_recv()
        for w in range(n):
            _remote(buf[w].at[c], buf[w].at[c], sems[0].at[w], sems[1].at[w], (x, y, 1 - c)).wait_send()

    return Comm(halves, [jax.ShapeDtypeStruct(s.shape, s.dtype) for s in halves], {w: w for w in range(n)},
                [pltpu.SemaphoreType.DMA((n,))] * 2, first, _nothing, last)


def gather_small(name, blk, reduce):
    r, cdim = blk.shape

    def body(in_ref, out_ref, *rest):
        if reduce:
            buf, send_sem, recv_sem = rest
        else:
            buf = out_ref
            send_sem, recv_sem = rest
        x, y, c, _ = _place()
        me = 4 * x + 2 * y + c
        buf[me] = in_ref[...]
        peers = []
        for dx in range(2):
            for dy in range(2):
                for dc in range(2):
                    if dx or dy or dc:
                        peers.append((dx, dy, dc))
        copies = []
        for s, (dx, dy, dc) in enumerate(peers):
            cp = _remote(in_ref, buf.at[me], send_sem.at[s], recv_sem.at[s],
                         ((1 - x if dx else x), (1 - y if dy else y), (1 - c if dc else c)))
            cp.start()
            copies.append(cp)
        for s, (dx, dy, dc) in enumerate(peers):
            px, py, pc_ = (1 - x if dx else x), (1 - y if dy else y), (1 - c if dc else c)
            landed = buf.at[4 * px + 2 * py + pc_]
            _remote(landed, landed, send_sem.at[s], recv_sem.at[s], (x, y, c)).wait_recv()
        for cp in copies:
            cp.wait_send()
        if reduce:
            tot = buf[0]
            for s in range(1, N_DEV):
                tot = tot + buf[s]
            out_ref[...] = tot

    vm = pl.BlockSpec(memory_space=pltpu.VMEM)
    out_shape = jax.ShapeDtypeStruct((r, cdim) if reduce else (N_DEV, r, cdim), F32)
    scratch = ([pltpu.VMEM((N_DEV, r, cdim), F32)] if reduce else []) + [pltpu.SemaphoreType.DMA((N_DEV - 1,))] * 2
    return _pcall(body, name=name, out_shape=out_shape, in_specs=[vm], out_specs=vm, scratch_shapes=scratch,
                  compiler_params=_params())(blk)


def _job_tiles(shapes, tile_bytes, mult):
    tiles = []
    for rows, cols in shapes:
        tr = _tile(rows, max(mult, tile_bytes // (4 * cols)), mult)
        tiles.append((tr, rows // tr))
    return tiles, max(n for _, n in tiles)


def sum_cores(name, owns, gots, place):
    nj = len(owns)
    tiles, _ = _job_tiles([o.shape[2:] for o in owns], 1 << 21, BF16_ROWS)
    steps = max(N_CHIPS * n for _, n in tiles)

    def body(place_ref, *refs):
        i = pl.program_id(0)
        for k, (_, n) in enumerate(tiles):
            @pl.when(i < N_CHIPS * n)
            def _(own_ref=refs[2 * k], got_ref=refs[2 * k + 1], o_ref=refs[2 * nj + k]):
                o_ref[...] = (own_ref[...].astype(F32) + got_ref[...].astype(F32)).astype(o_ref.dtype)

    in_specs, out_specs, out_shape, args = [], [], [], []
    for own, got, (tr, n) in zip(owns, gots, tiles):
        pc = own.shape[3]
        last = N_CHIPS * n - 1
        in_specs += [pl.BlockSpec((None, None, tr, pc),
                                  lambda i, s, n=n, last=last: (jnp.minimum(i, last) // n, s[1], jnp.minimum(i, last) % n, 0)),
                     pl.BlockSpec((None, tr, pc),
                                  lambda i, s, n=n, last=last: (jnp.minimum(i, last) // n, jnp.minimum(i, last) % n, 0))]
        out_specs.append(pl.BlockSpec((None, tr, pc),
                                      lambda i, s, n=n, last=last: (jnp.minimum(i, last) // n, jnp.minimum(i, last) % n, 0)))
        out_shape.append(jax.ShapeDtypeStruct(got.shape, BF16))
        args += [own, got]
    spec = pltpu.PrefetchScalarGridSpec(num_scalar_prefetch=1, grid=(steps,), in_specs=in_specs, out_specs=out_specs)
    return _pcall(body, name=name, out_shape=out_shape, grid_spec=spec,
                  compiler_params=_params(("arbitrary",)))(place, *args)


def sum_chips(name, parts, gots, place):
    nj = len(parts)
    tiles, steps = _job_tiles([p.shape[1:] for p in parts], 1 << 20, BF16_ROWS)

    def body(place_ref, *refs):
        i = pl.program_id(0)
        for k, (_, n) in enumerate(tiles):
            @pl.when(i < n)
            def _(part_ref=refs[2 * k], got_ref=refs[2 * k + 1], o_ref=refs[2 * nj + k]):
                tot = part_ref[...].astype(F32)
                for j in range(3):
                    tot = tot + got_ref[j].astype(F32)
                o_ref[...] = tot

    in_specs, out_specs, out_shape, args = [], [], [], []
    for part, got, (tr, n) in zip(parts, gots, tiles):
        pc = part.shape[2]
        in_specs += [pl.BlockSpec((None, tr, pc), lambda i, s, n=n: (s[0], jnp.minimum(i, n - 1), 0)),
                     pl.BlockSpec((3, tr, pc), lambda i, s, n=n: (0, jnp.minimum(i, n - 1), 0))]
        out_specs.append(pl.BlockSpec((None, tr, pc), lambda i, s, n=n: (s[1], jnp.minimum(i, n - 1), 0)))
        out_shape.append(jax.ShapeDtypeStruct((2,) + part.shape[1:], F32))
        args += [part, got]
    spec = pltpu.PrefetchScalarGridSpec(num_scalar_prefetch=1, grid=(steps,), in_specs=in_specs, out_specs=out_specs)
    return _pcall(body, name=name, out_shape=out_shape, grid_spec=spec,
                  compiler_params=_params(("arbitrary",)))(place, *args)


def adamw(name, jobs):
    c1 = 1.0 / (1.0 - ADAM_B1 ** ADAM_STEP)
    c2 = 1.0 / (1.0 - ADAM_B2 ** ADAM_STEP)
    nj = len(jobs)
    tiles, steps = _job_tiles([j[0].shape for j in jobs], 1 << 18, 8)

    def body(*refs):
        i = pl.program_id(0)
        for k, (_, n) in enumerate(tiles):
            w_ref, g_ref, m_ref, v_ref = refs[4 * k:4 * k + 4]
            d_ref, nm_ref, nv_ref = refs[4 * nj + 3 * k:4 * nj + 3 * k + 3]

            @pl.when(i < n)
            def _(w_ref=w_ref, g_ref=g_ref, m_ref=m_ref, v_ref=v_ref, d_ref=d_ref, nm_ref=nm_ref, nv_ref=nv_ref):
                gv = g_ref[...]
                nm = ADAM_B1 * m_ref[...] + (1.0 - ADAM_B1) * gv
                nv = ADAM_B2 * v_ref[...] + (1.0 - ADAM_B2) * (gv * gv)
                nm_ref[...] = nm
                nv_ref[...] = nv
                d_ref[...] = -ADAM_LR * ((nm * c1) / (jnp.sqrt(nv * c2) + ADAM_EPS) + ADAM_WD * w_ref[...])

    in_specs, out_specs, out_shape, args = [], [], [], []
    for (w, g, m, v), (tr, n) in zip(jobs, tiles):
        spec = pl.BlockSpec((tr, w.shape[1]), lambda i, n=n: (jnp.minimum(i, n - 1), 0))
        in_specs += [spec] * 4
        out_specs += [spec] * 3
        out_shape += [jax.ShapeDtypeStruct(w.shape, F32)] * 3
        args += [w, g, m, v]
    res = _pcall(body, name=name, out_shape=out_shape, grid=(steps,), in_specs=in_specs, out_specs=out_specs,
                 compiler_params=_params(("arbitrary",)))(*args)
    return [tuple(res[3 * k:3 * k + 3]) for k in range(nj)]


MATS = ["ffn1_w_in", "ffn1_w_out", "w_mix_in", "w_conv_out", "w_attn_out", "w_mix_out", "ffn2_w_in", "ffn2_w_out",
        "w_ple_gate", "w_ple_proj"]
COL_SHARDED = {"ffn1_w_in", "w_mix_in", "ffn2_w_in", "w_ple_proj"}
NORMS = ["ffn1_norm", "mix_norm", "ffn2_norm", "ple_norm", "final_norm"]
WEIGHTS = ["ffn1_norm", "ffn1_w_in", "ffn1_w_out", "mix_norm", "w_mix_in", "conv_w", "w_conv_out", "w_attn_out",
           "w_mix_out", "ffn2_norm", "ffn2_w_in", "ffn2_w_out", "ple_norm", "w_ple_gate", "w_ple_proj", "final_norm"]


def _pad_rows(a, rows):
    return jnp.concatenate([a, jnp.zeros((rows - a.shape[0],) + a.shape[1:], a.dtype)], axis=0)


def _step(x, p, tgt, w, m, v):
    t, d = x.shape
    tt = _tile(t, 256)
    tm = _tile(t, 512)
    tm2 = _tile(t, 1024)
    tq = _tile(t, 1024)

    chip = 2 * lax.axis_index("x") + lax.axis_index("y")
    place = jnp.stack([chip, lax.axis_index("c")]).astype(jnp.int32)

    placed = dict(zip(MATS, place_shards("place_shards", [w[k] for k in MATS], place)))
    full = {}

    def keep(names, bufs):
        for k, buf in zip(names, bufs):
            full[k] = buf if k in COL_SHARDED else buf.reshape(-1, buf.shape[2])

    def gather_of(names):
        return gather_comm([placed[k] for k in names])

    cw_all = gather_small("gather_conv_w", _pad_rows(w["conv_w"], 8), False)
    cw8 = jnp.concatenate([cw_all[2 * k] for k in range(N_CHIPS)], axis=1)
    g1, gm, g2, gp, gf = (w[k].reshape(1, d) for k in NORMS)

    def ffn_fwd(tag, h, g, first, w_in_name, w_out_name, riders):
        if first:
            n, bufs = rms_fwd(tag + "_norm", h, g, tt, comm=gather_of(first))
            keep(first, bufs)
            (a, s), bufs = ffn_in_act(tag + "_in", n, full[w_in_name], tm, comm=gather_of(riders))
            keep(riders, bufs)
        else:
            a, s, n = ffn_in_act(tag + "_in", h, full[w_in_name], tm, gain=g)
        return n, a, s, mm_nn(tag + "_out", s, full[w_out_name], F32, tm, res=h, alpha=0.5)

    n1, a1, s1, h1 = ffn_fwd("ffn1", x, g1, ["ffn1_w_in"], "ffn1_w_in", "ffn1_w_out", ["ffn1_w_out", "w_mix_in"])
    wmix = full["w_mix_in"]
    riders = [["w_conv_out", "w_attn_out", "w_mix_out"], ["ffn2_w_in"], ["ffn2_w_out", "w_ple_gate", "w_ple_proj"]]
    (cbx, u), bufs = mm_nn_stacked("mix_in_conv", h1, wmix, BF16, tm2, d, 0, 3, comm=gather_of(riders[0]), gain=gm)
    keep(riders[0], bufs)
    qkv, bufs = mm_nn_stacked("mix_in_qkv", u, wmix, BF16, tm2, d, 3, 3, comm=gather_of(riders[1]))
    keep(riders[1], bufs)
    gates, bufs = mm_nn_stacked("mix_in_gates", u, wmix, BF16, tm2, d, 6, 2, comm=gather_of(riders[2]))
    keep(riders[2], bufs)
    wpp = full["w_ple_proj"]
    wpp = jnp.transpose(wpp, (1, 0, 2)).reshape(wpp.shape[1], -1)
    ycin, y_conv = conv_out_fwd("conv_out", cbx, cw8, full["w_conv_out"], tt)
    o = attn_fwd("attn", qkv, tq)
    y_attn = mm_nn("attn_out", o, full["w_attn_out"], BF16, tm)
    merged, h2 = mix_out_fwd("mix_out", gates, y_conv, y_attn, h1, full["w_mix_out"], tm)
    n2, a2, s2, h3 = ffn_fwd("ffn2", h2, g2, [], "ffn2_w_in", "ffn2_w_out", [])

    pieces, chip_sums, halves = {}, {}, {}

    def as_pieces(k):
        pc = pieces[k]
        return pc if k in COL_SHARDED else pc.reshape(N_CHIPS, 2, pc.shape[0] // (2 * N_CHIPS), pc.shape[1])

    def sum_siblings(tag, names):
        pcs = [as_pieces(k) for k in names]
        got = run_comm("exchange_" + tag, exchange_comm(pcs))
        chip_sums.update(zip(names, sum_cores("sum_cores_" + tag, pcs, got, place)))

    def scatter_of(names):
        return scatter_comm([chip_sums[k] for k in names])

    def sum_landed(tag, names, landed):
        halves.update(zip(names, sum_chips("sum_chips_" + tag, [chip_sums[k] for k in names], landed, place)))

    npl, dh4, dpp, dzg, dgf, loss_row = tail("tail", h3, p, tgt, gp, gf, full["w_ple_gate"], wpp, tt)
    dwpp = mm_tn_whole("ple_proj_dw", p, dpp, tm2)
    pieces["w_ple_proj"] = jnp.transpose(dwpp.reshape(2, p.shape[1] // 2, N_CHIPS, d // N_CHIPS), (2, 0, 1, 3))
    pieces["w_ple_gate"] = mm_tn_rows("ple_gate_dw", npl, dzg, tm2)
    dh3, df2, dgp = mm_nt("ple_gate_dx", dzg, full["w_ple_gate"], F32, tm, d, norm=(h3, gp, dh4), alpha=0.5)
    w_in, w_out = full["ffn2_w_in"], full["ffn2_w_out"]
    pieces["ffn2_w_out"] = mm_tn_rows("ffn2_dwout", s2, df2, tm2)
    da2 = ffn_ds_dact("ffn2_ds", df2, w_out, a2, tm2)
    pieces["ffn2_w_in"] = mm_tn_cols("ffn2_dwin", n2, da2, tm2)
    dh2, dh2b, dg2 = mm_nt_stacked("ffn2_dn", da2, w_in, tm2, (h2, g2, dh3))
    pieces["w_mix_out"] = mm_tn_rows("mix_out_dw", merged, dh2b, tm2)
    dyc, dya, dgates = mix_out_bwd("mix_out_dx", dh2b, full["w_mix_out"], gates, y_conv, y_attn, tm)
    pieces["w_conv_out"] = mm_tn_rows("conv_out_dw", ycin, dyc, tm2)
    dcbx, dcw8 = conv_out_bwd("conv_out_dx", dyc, full["w_conv_out"], cbx, cw8, tt)
    pieces["w_attn_out"] = mm_tn_rows("attn_out_dw", o, dya, tm2)
    do = mm_nt("attn_out_dx", dya, full["w_attn_out"], BF16, tm, d)
    dq, dk, dv = attn_bwd("attn_bwd", qkv, do, tq)
    dmix = [dcbx, dq, dk, dv, dgates]
    early = ["ffn2_w_in", "ffn2_w_out", "w_ple_gate", "w_ple_proj", "w_mix_out", "w_conv_out", "w_attn_out"]
    swap = exchange_comm([as_pieces(k) for k in early])
    pieces["w_mix_in"], got = mm_tn_parts("mix_in_dw", u, dmix, tm2, comm=swap)
    chip_sums.update(zip(early, sum_cores("sum_cores_early", swap.ins, got, place)))
    swap = exchange_comm([as_pieces("w_mix_in")])
    (dh1, df1, dgm), landed = mm_nt_parts("mix_in_dx", dmix, wmix, tm2, (h1, gm, dh2), 0.5,
                                          comm=join_comms(scatter_of(early), swap))
    sum_landed("early", early, landed[:len(early)])
    chip_sums["w_mix_in"] = sum_cores("sum_cores_mix", swap.ins, landed[len(early):], place)[0]
    w_in, w_out = full["ffn1_w_in"], full["ffn1_w_out"]
    pieces["ffn1_w_out"] = mm_tn_rows("ffn1_dwout", s1, df1, tm2)
    da1 = ffn_ds_dact("ffn1_ds", df1, w_out, a1, tm2)
    pieces["ffn1_w_in"], landed = mm_tn_cols("ffn1_dwin", n1, da1, tm2, comm=scatter_of(["w_mix_in"]))
    sum_landed("mix", ["w_mix_in"], landed)
    late = ["ffn1_w_in", "ffn1_w_out"]
    sum_siblings("late", late)
    done = early + ["w_mix_in"]
    (dx, _, dg1), landed = mm_nt_stacked(
        "ffn1_dn", da1, w_in, tm2, (x, g1, dh1),
        comm=join_comms(scatter_of(late), share_comm([halves[k] for k in done])))
    sum_landed("late", late, landed[:len(late)])
    shared = dict(zip(done, landed[len(late):]))

    shared.update(zip(late, run_comm("share_halves", share_comm([halves[k] for k in late]))))
    grad, delta, new_m, new_v = {}, {}, {}, {}
    for k in MATS:
        grad[k] = shared[k].reshape(w[k].shape)

    small = jnp.concatenate([dg1, dgm, dg2, dgp, dgf, dcw8[:3], loss_row, jnp.zeros((7, d), F32)], axis=0)
    tot = gather_small("sum_small", small, True)
    loss = tot[8, 0]
    norm_w = jnp.concatenate([w[k].reshape(1, d) for k in NORMS] + [jnp.zeros((3, d), F32)], axis=0)
    norm_m = jnp.concatenate([m[k].reshape(1, d) for k in NORMS] + [jnp.zeros((3, d), F32)], axis=0)
    norm_v = jnp.concatenate([v[k].reshape(1, d) for k in NORMS] + [jnp.ones((3, d), F32)], axis=0)
    norm_g = jnp.concatenate([tot[0:5], jnp.zeros((3, d), F32)], axis=0)
    cs = d // N_CHIPS
    gcw = lax.dynamic_slice(tot[5:8], (0, chip * cs), (3, cs))
    conv_job = (_pad_rows(w["conv_w"], 8), _pad_rows(gcw, 8), _pad_rows(m["conv_w"], 8),
                jnp.concatenate([v["conv_w"], jnp.ones((5, cs), F32)], axis=0))

    steps = adamw("adamw", [(w[k], grad[k], m[k], v[k]) for k in MATS]
                  + [(norm_w, norm_g, norm_m, norm_v), conv_job])
    for k, res in zip(MATS, steps):
        delta[k], new_m[k], new_v[k] = res
    nd, nm, nv = steps[len(MATS)]
    for r, k in enumerate(NORMS):
        grad[k] = norm_g[r].reshape(w[k].shape)
        delta[k], new_m[k], new_v[k] = (a[r].reshape(w[k].shape) for a in (nd, nm, nv))
    cd, cm, cv = steps[len(MATS) + 1]
    grad["conv_w"], delta["conv_w"], new_m["conv_w"], new_v["conv_w"] = gcw, cd[:3], cm[:3], cv[:3]
    return loss, dx, grad, delta, new_m, new_v


def kernel(x, p, ffn1_norm, ffn1_w_in, ffn1_w_out, mix_norm, w_mix_in, conv_w, w_conv_out, w_attn_out, w_mix_out, ffn2_norm, ffn2_w_in, ffn2_w_out, ple_norm, w_ple_gate, w_ple_proj, final_norm, loss_target, m_ffn1_norm, m_ffn1_w_in, m_ffn1_w_out, m_mix_norm, m_w_mix_in, m_conv_w, m_w_conv_out, m_w_attn_out, m_w_mix_out, m_ffn2_norm, m_ffn2_w_in, m_ffn2_w_out, m_ple_norm, m_w_ple_gate, m_w_ple_proj, m_final_norm, v_ffn1_norm, v_ffn1_w_in, v_ffn1_w_out, v_mix_norm, v_w_mix_in, v_conv_w, v_w_conv_out, v_w_attn_out, v_w_mix_out, v_ffn2_norm, v_ffn2_w_in, v_ffn2_w_out, v_ple_norm, v_w_ple_gate, v_w_ple_proj, v_final_norm):
    ws = (ffn1_norm, ffn1_w_in, ffn1_w_out, mix_norm, w_mix_in, conv_w, w_conv_out, w_attn_out, w_mix_out, ffn2_norm,
          ffn2_w_in, ffn2_w_out, ple_norm, w_ple_gate, w_ple_proj, final_norm)
    ms = (m_ffn1_norm, m_ffn1_w_in, m_ffn1_w_out, m_mix_norm, m_w_mix_in, m_conv_w, m_w_conv_out, m_w_attn_out,
          m_w_mix_out, m_ffn2_norm, m_ffn2_w_in, m_ffn2_w_out, m_ple_norm, m_w_ple_gate, m_w_ple_proj, m_final_norm)
    vs = (v_ffn1_norm, v_ffn1_w_in, v_ffn1_w_out, v_mix_norm, v_w_mix_in, v_conv_w, v_w_conv_out, v_w_attn_out,
          v_w_mix_out, v_ffn2_norm, v_ffn2_w_in, v_ffn2_w_out, v_ple_norm, v_w_ple_gate, v_w_ple_proj, v_final_norm)
    assert x.shape[0] == 1 and p.shape[:2] == (1, 1), "one sequence and one layer per device"

    def strip(a):
        return a[0] if a.ndim == 3 or (a.ndim == 2 and a.shape[0] == 1) else a

    w = {k: strip(a) for k, a in zip(WEIGHTS, ws)}
    m = {k: strip(a) for k, a in zip(WEIGHTS, ms)}
    v = {k: strip(a) for k, a in zip(WEIGHTS, vs)}
    loss, dx, grad, delta, new_m, new_v = _step(x[0], p[0, 0], loss_target[0], w, m, v)
    shapes = [a.shape for a in ws]
    outs = [loss, dx[None]]
    for res in (grad, delta, new_m, new_v):
        outs += [res[k].reshape(s) for k, s in zip(WEIGHTS, shapes)]
    return tuple(outs)
```

```python
import functools
import math

import jax
import jax.numpy as jnp
from jax import lax
from jax.experimental import pallas as pl
from jax.experimental.pallas import tpu as pltpu

F32 = jnp.float32
BF16 = jnp.bfloat16
MESH = pl.DeviceIdType.MESH
ANY = pl.BlockSpec(memory_space=pl.ANY)

HEAD_DIM = 128
NORM_EPS = 1e-6
N_CHIPS = 4
N_DEV = 8
BF16_ROWS = 16
VMEM_LIMIT = 56 * 1024 * 1024
ACC_BYTES = 8 * 1024 * 1024
STICK_EXIT = 110.0

ADAM_LR = 0.001
ADAM_B1 = 0.9
ADAM_B2 = 0.999
ADAM_EPS = 1e-08
ADAM_WD = 0.01
ADAM_STEP = 10

NN = (((1,), (0,)), ((), ()))
NT = (((1,), (1,)), ((), ()))
TN = (((0,), (0,)), ((), ()))


def _params(sem=None, **kw):
    if sem is not None:
        kw["dimension_semantics"] = sem
    return pltpu.CompilerParams(vmem_limit_bytes=VMEM_LIMIT, **kw)


def _pcall(body, **kw):
    return pl.pallas_call(body, **kw)


def _tile(n, pref, mult=8):
    best = None
    for d in range(mult, min(n, pref) + 1, mult):
        if n % d == 0:
            best = d
    return best if best is not None else n


def _dot(a, b, dims):
    return lax.dot_general(a, b, dims, preferred_element_type=F32)


def _call(name, body, grid, in_specs, out_specs, out_shape, args, scratch=(), sem=None, comm=None):
    n_in, n_out, n_sc = len(in_specs), len(out_specs), len(scratch)
    if comm is None:
        def plain(*refs):
            body(refs[:n_in], refs[n_in:n_in + n_out], refs[n_in + n_out:])

        return _pcall(plain, name=name, out_shape=list(out_shape), grid=grid, in_specs=list(in_specs),
                      out_specs=list(out_specs), scratch_shapes=list(scratch), compiler_params=_params(sem))(*args)
    n_cin, n_cout = len(comm.ins), len(comm.outs)
    steps = math.prod(grid)

    def hosted(*refs):
        ins, c_ins = refs[:n_in], refs[n_in:n_in + n_cin]
        outs = refs[n_in + n_cin:n_in + n_cin + n_out]
        c_outs = refs[n_in + n_cin + n_out:n_in + n_cin + n_out + n_cout]
        rest = refs[n_in + n_cin + n_out + n_cout:]
        sems = rest[n_sc:]
        step = pl.program_id(0)
        for ax in range(1, len(grid)):
            step = step * grid[ax] + pl.program_id(ax)

        @pl.when(step == 0)
        def _():
            comm.first(c_ins, c_outs, sems)

        body(ins, outs, rest[:n_sc])

        @pl.when(step == (3 * steps) // 4)
        def _():
            comm.mid(c_ins, c_outs, sems)

        @pl.when(step == steps - 1)
        def _():
            comm.last(c_ins, c_outs, sems)

    res = _pcall(hosted, name=name, out_shape=list(out_shape) + comm.outs, grid=grid,
                 in_specs=list(in_specs) + [ANY] * n_cin, out_specs=list(out_specs) + [ANY] * n_cout,
                 input_output_aliases={n_in + k: n_out + v for k, v in comm.aliases.items()},
                 scratch_shapes=list(scratch) + comm.sems,
                 compiler_params=_params(("arbitrary",) * len(grid)))(*args, *comm.ins)
    return list(res[:n_out]), list(res[n_out:])


NORM_CHUNK = 256


def _norm_bwd_tile(read_dn, rows, first, h_ref, g_ref, dr_ref, dh_ref, dhb_ref, dg_ref, alpha):
    @pl.when(first)
    def _():
        dg_ref[...] = jnp.zeros_like(dg_ref)

    gv = g_ref[...]
    tot = jnp.zeros_like(gv)
    for c0 in range(0, rows, NORM_CHUNK):
        sl = slice(c0, min(rows, c0 + NORM_CHUNK))
        hv = h_ref[sl, :]
        rs = _rstd(hv)
        hn = hv * rs
        dnv = read_dn(sl)
        gy = dnv * gv
        dh = dr_ref[sl, :] + rs * (gy - hn * jnp.mean(gy * hn, axis=-1, keepdims=True))
        dh_ref[sl, :] = dh
        dhb_ref[sl, :] = (alpha * dh).astype(BF16)
        tot = tot + jnp.sum(dnv * hn, axis=0, keepdims=True)
    dg_ref[...] += tot


def _mm(name, a, b, out_sds, grid, a_spec, b_spec, o_spec, dims, acc_shape, res=None, alpha=1.0, comm=None,
        norm=None, gain=None):
    nk = grid[2]

    def body(ins, outs, scratch):
        a_ref, b_ref = ins[:2]
        r_ref = ins[2] if res is not None else None
        o_ref = outs[0]
        if gain is not None:
            n_ref = scratch[-1]

            @pl.when(jnp.logical_and(pl.program_id(1) == 0, pl.program_id(2) == 0))
            def _():
                hv = a_ref[...]
                n_ref[...] = (hv * _rstd(hv) * ins[-1][...]).astype(BF16)
                outs[-1][...] = n_ref[...]

            a_ref = n_ref

        def finish(read):
            if norm is not None:
                first = jnp.logical_and(pl.program_id(0) == 0, pl.program_id(1) == 0)
                _norm_bwd_tile(read, o_ref.shape[0], first, *ins[2:5], *outs, alpha)
                return
            r = read(slice(None))
            if alpha != 1.0:
                r = r * alpha
            if r_ref is not None:
                r = r_ref[...] + r
            if len(o_ref.shape) == 3:
                half = o_ref.shape[1]
                o_ref[0] = r[:half].astype(o_ref.dtype)
                o_ref[1] = r[half:].astype(o_ref.dtype)
            else:
                o_ref[...] = r.astype(o_ref.dtype)

        if nk == 1:
            part = _dot(a_ref[...].astype(BF16), b_ref[...].astype(BF16), dims)
            finish(lambda sl: part[sl])
        else:
            acc_ref = scratch[0]
            kk = pl.program_id(2)

            @pl.when(kk == 0)
            def _():
                acc_ref[...] = jnp.zeros_like(acc_ref)

            acc_ref[...] += _dot(a_ref[...].astype(BF16), b_ref[...].astype(BF16), dims)

            @pl.when(kk == nk - 1)
            def _():
                finish(lambda sl: acc_ref[sl, :])

    in_specs = [a_spec, b_spec]
    args = [a, b]
    out_specs, out_shape = [o_spec], [out_sds]
    sem = ("parallel", "parallel", "arbitrary")
    if res is not None:
        in_specs.append(o_spec)
        args.append(res)
    if norm is not None:
        width = out_sds.shape[1]
        whole = pl.BlockSpec((1, width), lambda i, j, r: (0, 0))
        in_specs += [o_spec, whole, o_spec]
        args += list(norm)
        out_specs = [o_spec, o_spec, whole]
        out_shape = [jax.ShapeDtypeStruct(out_sds.shape, F32), jax.ShapeDtypeStruct(out_sds.shape, BF16),
                     jax.ShapeDtypeStruct((1, width), F32)]
        sem = ("arbitrary", "arbitrary", "arbitrary")
    scratch = [] if nk == 1 else [pltpu.VMEM(acc_shape, F32)]
    if gain is not None:
        in_specs.append(pl.BlockSpec((1, a.shape[1]), lambda i, j, r: (0, 0)))
        args.append(gain)
        out_specs.append(a_spec)
        out_shape.append(jax.ShapeDtypeStruct(a.shape, BF16))
        scratch.append(pltpu.VMEM(a_spec.block_shape, BF16))
        sem = ("parallel", "arbitrary", "arbitrary")
    got = _call(name, body, grid, in_specs, out_specs, out_shape, args, scratch, sem, comm)
    if norm is not None or gain is not None:
        return got if comm is None else (got[0], got[1])
    return got[0] if comm is None else (got[0][0], got[1])


def ffn_in_act(name, n, w4, tm, comm=None, gain=None):
    t, d = n.shape
    cs = w4.shape[2]

    def body(ins, outs, scratch):
        wg_ref, wu_ref = ins[-2:]
        a_ref, s_ref = outs[:2]
        if gain is None:
            nv = ins[0][...]
        else:
            hv = ins[0][...]
            nv = (hv * _rstd(hv) * ins[1][...]).astype(BF16)

            @pl.when(pl.program_id(0) == 0)
            def _():
                outs[2][...] = nv
        gate = _dot(nv, wg_ref[...], NN)
        up = _dot(nv, wu_ref[...], NN)
        a_ref[0] = gate.astype(BF16)
        a_ref[1] = up.astype(BF16)
        s_ref[...] = (gate * jax.nn.sigmoid(gate) * up).astype(BF16)

    rows = pl.BlockSpec((tm, d), lambda j, i: (i, 0))
    in_specs = [rows] + ([] if gain is None else [pl.BlockSpec((1, d), lambda j, i: (0, 0))])
    in_specs += [pl.BlockSpec((None, d, cs), lambda j, i: (j, 0, 0)),
                 pl.BlockSpec((None, d, cs), lambda j, i: (2 + j, 0, 0))]
    out_specs = [pl.BlockSpec((2, tm, cs), lambda j, i: (0, i, j)), pl.BlockSpec((tm, cs), lambda j, i: (i, j))]
    out_shape = [jax.ShapeDtypeStruct((2, t, 2 * cs), BF16), jax.ShapeDtypeStruct((t, 2 * cs), BF16)]
    if gain is not None:
        out_specs.append(pl.BlockSpec((tm, d), lambda j, i: (jnp.where(j == 0, i, t // tm - 1), 0)))
        out_shape.append(jax.ShapeDtypeStruct((t, d), BF16))
    got = _call(name, body, (2, t // tm), in_specs, out_specs, out_shape,
                [n] + ([] if gain is None else [gain]) + [w4, w4], (), ("arbitrary", "arbitrary"), comm)
    return got if comm is None else (got[0], got[1])


def ffn_ds_dact(name, df, w_out, a3, tm):
    t, d = df.shape
    f = w_out.shape[0]
    cs = f // 2

    def body(ins, outs, scratch):
        df_ref, w_ref, a_ref = ins
        ds = _dot(df_ref[...], w_ref[...], NT)
        for c0 in range(0, tm, NORM_CHUNK):
            sl = slice(c0, min(tm, c0 + NORM_CHUNK))
            gate = a_ref[0, sl, :].astype(F32)
            up = a_ref[1, sl, :].astype(F32)
            sg = jax.nn.sigmoid(gate)
            outs[0][0, sl, :] = (ds[sl] * up * sg * (1.0 + gate * (1.0 - sg))).astype(BF16)
            outs[0][1, sl, :] = (ds[sl] * gate * sg).astype(BF16)

    blk = pl.BlockSpec((2, tm, cs), lambda i, j: (0, i, j))
    return _call(name, body, (t // tm, 2),
                 [pl.BlockSpec((tm, d), lambda i, j: (i, 0)), pl.BlockSpec((cs, d), lambda i, j: (j, 0)), blk],
                 [blk], [jax.ShapeDtypeStruct((2, t, f), BF16)], [df, w_out, a3], (), ("parallel", "parallel"))[0]


def _part_ranges(parts, d):
    out, lo = [], 0
    for p in parts:
        out.append((lo, p.shape[1] // d))
        lo += p.shape[1] // d
    return out, lo


def mm_nt_parts(name, parts, w4, tm, norm, alpha, comm=None):
    m = parts[0].shape[0]
    d, cs = w4.shape[1], w4.shape[2]
    per = cs // d
    ranges, nblk = _part_ranges(parts, d)
    np_ = len(parts)
    nt = m // tm
    chunk = tm // nblk

    def body(ins, outs, scratch):
        w_ref, acc = ins[np_], scratch[0]
        i, r = pl.program_id(0), pl.program_id(1)

        @pl.when(jnp.logical_and(i < nt, r == 0))
        def _():
            acc[i % 2] = jnp.zeros(acc.shape[1:], F32)

        for (lo, n), a_ref in zip(ranges, ins[:np_]):
            @pl.when(jnp.logical_and(i < nt, jnp.logical_and(r >= lo, r < lo + n)))
            def _(a_ref=a_ref):
                acc[i % 2] += _dot(a_ref[...], w_ref[...], NT)

        @pl.when(i > 0)
        def _():
            rows = pl.ds(pl.multiple_of(r * chunk, chunk), chunk)
            first = jnp.logical_and(i == 1, r == 0)
            _norm_bwd_tile(lambda sl: acc[(i - 1) % 2, rows, :][sl], chunk, first, *ins[np_ + 1:], *outs, alpha)

    def ahead(i, r):
        return jnp.where(i < nt, r, nblk - 1)

    rows = pl.BlockSpec((chunk, d), lambda i, r: (jnp.where(i == 0, 0, (i - 1) * nblk + r), 0))
    whole = pl.BlockSpec((1, d), lambda i, r: (0, 0))
    specs = [pl.BlockSpec((tm, d), lambda i, r, lo=lo, n=n: (jnp.minimum(i, nt - 1), jnp.clip(ahead(i, r) - lo, 0, n - 1)))
             for lo, n in ranges]
    specs += [pl.BlockSpec((None, d, d), lambda i, r: (ahead(i, r) // per, 0, ahead(i, r) % per)), rows, whole, rows]
    got = _call(name, body, (nt + 1, nblk), specs, [rows, rows, whole],
                [jax.ShapeDtypeStruct((m, d), F32), jax.ShapeDtypeStruct((m, d), BF16),
                 jax.ShapeDtypeStruct((1, d), F32)],
                list(parts) + [w4] + list(norm), [pltpu.VMEM((2, tm, d), F32)], ("arbitrary", "arbitrary"), comm)
    return got if comm is None else (got[0], got[1])


def mm_tn_parts(name, xa, parts, tt, comm=None):
    t, k = xa.shape
    d = k
    pr = k // 2
    ranges, nblk = _part_ranges(parts, d)
    per = nblk // N_CHIPS

    def body(ins, outs, scratch):
        x_ref, acc = ins[0], scratch[0]
        jb, r = pl.program_id(0), pl.program_id(1)

        @pl.when(r == 0)
        def _():
            acc[...] = jnp.zeros_like(acc)

        for (lo, n), p_ref in zip(ranges, ins[1:]):
            @pl.when(jnp.logical_and(jb >= lo, jb < lo + n))
            def _(p_ref=p_ref):
                acc[...] += _dot(x_ref[...], p_ref[...], TN)

        @pl.when(r == t // tt - 1)
        def _():
            outs[0][0] = acc[:pr].astype(BF16)
            outs[0][1] = acc[pr:].astype(BF16)

    def part_spec(lo, n):
        return pl.BlockSpec((tt, d), lambda jb, r: (jnp.where(jnp.logical_and(jb >= lo, jb < lo + n), r, 0),
                                                    jnp.clip(jb - lo, 0, n - 1)))

    specs = [pl.BlockSpec((tt, k), lambda jb, r: (r, 0))] + [part_spec(lo, n) for lo, n in ranges]
    got = _call(name, body, (nblk, t // tt), specs,
                [pl.BlockSpec((None, 2, pr, d), lambda jb, r: (jb // per, 0, 0, jb % per))],
                [jax.ShapeDtypeStruct((N_CHIPS, 2, pr, per * d), BF16)], [xa] + list(parts),
                [pltpu.VMEM((k, d), F32)], ("parallel", "arbitrary"), comm)
    return got[0] if comm is None else (got[0][0], got[1])


def mm_nn(name, a, w, out_dtype, tm, res=None, alpha=1.0):
    m, k = a.shape
    n = w.shape[1]
    return _mm(name, a, w, jax.ShapeDtypeStruct((m, n), out_dtype), (m // tm, 1, 1),
               pl.BlockSpec((tm, k), lambda i, j, r: (i, 0)),
               pl.BlockSpec((k, n), lambda i, j, r: (0, 0)),
               pl.BlockSpec((tm, n), lambda i, j, r: (i, 0)), NN, None, res=res, alpha=alpha)


def mm_nn_stacked(name, a, w4, out_dtype, tm, tn, j0=0, nj=None, comm=None, gain=None):
    m, k = a.shape
    cs = w4.shape[2]
    per = cs // tn
    nj = N_CHIPS * per - j0 if nj is None else nj
    return _mm(name, a, w4, jax.ShapeDtypeStruct((m, nj * tn), out_dtype), (m // tm, nj, 1),
               pl.BlockSpec((tm, k), lambda i, j, r: (i, 0)),
               pl.BlockSpec((None, k, tn), lambda i, j, r: ((j + j0) // per, 0, (j + j0) % per)),
               pl.BlockSpec((tm, tn), lambda i, j, r: (i, j)), NN, None, comm=comm, gain=gain)


def mm_nt(name, dy, w, out_dtype, tm, tko, norm=None, alpha=1.0):
    m, n = dy.shape
    k = w.shape[0]
    return _mm(name, dy, w, jax.ShapeDtypeStruct((m, k), out_dtype), (m // tm, k // tko, 1),
               pl.BlockSpec((tm, n), lambda i, j, r: (i, 0)),
               pl.BlockSpec((tko, n), lambda i, j, r: (j, 0)),
               pl.BlockSpec((tm, tko), lambda i, j, r: (i, j)), NT, None, norm=norm, alpha=alpha)


def mm_nt_stacked(name, dy, w4, tm, norm, alpha=1.0, comm=None):
    m = dy.shape[1]
    k, cs = w4.shape[1], w4.shape[2]
    nt = m // tm
    chunk = tm // N_CHIPS

    def body(ins, outs, scratch):
        dy_ref, w_ref, h_ref, g_ref, dr_ref = ins
        dh_ref, dhb_ref, dg_ref = outs
        acc = scratch[0]
        i, r = pl.program_id(0), pl.program_id(1)

        @pl.when(jnp.logical_and(i < nt, r == 0))
        def _():
            acc[i % 2] = jnp.zeros(acc.shape[1:], F32)

        @pl.when(i < nt)
        def _():
            acc[i % 2] += _dot(dy_ref[...], w_ref[...], NT)

        @pl.when(i > 0)
        def _():
            rows = pl.ds(pl.multiple_of(r * chunk, chunk), chunk)
            first = jnp.logical_and(i == 1, r == 0)
            _norm_bwd_tile(lambda sl: acc[(i - 1) % 2, rows, :][sl], chunk, first, h_ref, g_ref, dr_ref,
                           dh_ref, dhb_ref, dg_ref, alpha)

    def behind(i, r):
        return (jnp.where(i == 0, 0, (i - 1) * N_CHIPS + r), 0)

    def ahead(i, r):
        return jnp.where(i < nt, r, N_CHIPS - 1)

    rows = pl.BlockSpec((chunk, k), behind)
    whole = pl.BlockSpec((1, k), lambda i, r: (0, 0))
    got = _call(name, body, (nt + 1, N_CHIPS),
                [pl.BlockSpec((None, tm, cs), lambda i, r: (ahead(i, r) // 2, jnp.minimum(i, nt - 1), ahead(i, r) % 2)),
                 pl.BlockSpec((None, k, cs), lambda i, r: (ahead(i, r), 0, 0)), rows, whole, rows],
                [rows, rows, whole],
                [jax.ShapeDtypeStruct((m, k), F32), jax.ShapeDtypeStruct((m, k), BF16),
                 jax.ShapeDtypeStruct((1, k), F32)],
                [dy, w4] + list(norm), [pltpu.VMEM((2, tm, k), F32)], ("arbitrary", "arbitrary"), comm)
    return got if comm is None else (got[0], got[1])


def mm_tn_rows(name, xa, dy, tt):
    t, k = xa.shape
    n = dy.shape[1]
    tkr = k if k * n * 4 <= ACC_BYTES else k // 2
    return _mm(name, xa, dy, jax.ShapeDtypeStruct((k, n), BF16), (k // tkr, 1, t // tt),
               pl.BlockSpec((tt, tkr), lambda i, j, r: (r, i)),
               pl.BlockSpec((tt, n), lambda i, j, r: (r, 0)),
               pl.BlockSpec((tkr, n), lambda i, j, r: (i, 0)), TN, (tkr, n))


def mm_tn_whole(name, xa, dy, tt):
    t, k = xa.shape
    n = dy.shape[1]
    return _mm(name, xa, dy, jax.ShapeDtypeStruct((k, n), BF16), (1, 1, t // tt),
               pl.BlockSpec((tt, k), lambda i, j, r: (r, 0)),
               pl.BlockSpec((tt, n), lambda i, j, r: (r, 0)),
               pl.BlockSpec((k, n), lambda i, j, r: (0, 0)), TN, (k, n))


def mm_tn_cols(name, xa, dy, tt, comm=None):
    t, k = xa.shape
    pr = k // 2
    if dy.ndim == 3:
        cs = dy.shape[2] // 2
        dy_spec = pl.BlockSpec((None, tt, cs), lambda i, j, r: (j // 2, r, j % 2))
    else:
        cs = dy.shape[1] // N_CHIPS
        dy_spec = pl.BlockSpec((tt, cs), lambda i, j, r: (r, j))
    return _mm(name, xa, dy, jax.ShapeDtypeStruct((N_CHIPS, 2, pr, cs), BF16), (1, N_CHIPS, t // tt),
               pl.BlockSpec((tt, k), lambda i, j, r: (r, 0)), dy_spec,
               pl.BlockSpec((None, 2, pr, cs), lambda i, j, r: (j, 0, 0, 0)), TN, (k, cs), comm=comm)


def _rows(tt, w, col=0):
    return pl.BlockSpec((tt, w), lambda i: (i, col))


def _whole(shape):
    return pl.BlockSpec(shape, lambda i: (0,) * len(shape))


def _rstd(h):
    return lax.rsqrt(jnp.mean(h * h, axis=-1, keepdims=True) + NORM_EPS)


def rms_fwd(name, h, g, tt, comm=None):
    t, d = h.shape

    def body(ins, outs, scratch):
        hv = ins[0][...]
        outs[0][...] = (hv * _rstd(hv) * ins[1][...]).astype(BF16)

    got = _call(name, body, (t // tt,), [_rows(tt, d), _whole((1, d))], [_rows(tt, d)],
                [jax.ShapeDtypeStruct((t, d), BF16)], [h, g], (), ("parallel",), comm)
    return got[0] if comm is None else (got[0][0], got[1])


def mix_out_fwd(name, gates, yc, ya, h, w, tt):
    t, d = yc.shape

    def body(g_ref, yc_ref, ya_ref, h_ref, w_ref, m_ref, o_ref):
        merged = (jax.nn.sigmoid(g_ref[:, :d].astype(F32)) * yc_ref[...].astype(F32)
                  + jax.nn.sigmoid(g_ref[:, d:].astype(F32)) * ya_ref[...].astype(F32)).astype(BF16)
        m_ref[...] = merged
        o_ref[...] = h_ref[...] + _dot(merged, w_ref[...], NN)

    return _pcall(body, name=name,
                  out_shape=(jax.ShapeDtypeStruct((t, d), BF16), jax.ShapeDtypeStruct((t, d), F32)),
                  grid=(t // tt,),
                  in_specs=[_rows(tt, 2 * d), _rows(tt, d), _rows(tt, d), _rows(tt, d), _whole((d, d))],
                  out_specs=(_rows(tt, d), _rows(tt, d)),
                  compiler_params=_params(("parallel",)))(gates, yc, ya, h, w)


def mix_out_bwd(name, dh, w, gates, yc, ya, tt):
    t, d = yc.shape

    def body(dh_ref, w_ref, g_ref, yc_ref, ya_ref, dyc_ref, dya_ref, dg_ref):
        dmv = _dot(dh_ref[...], w_ref[...], NT)
        sc = jax.nn.sigmoid(g_ref[:, :d].astype(F32))
        sa = jax.nn.sigmoid(g_ref[:, d:].astype(F32))
        dyc_ref[...] = (dmv * sc).astype(BF16)
        dya_ref[...] = (dmv * sa).astype(BF16)
        dg_ref[:, :d] = (dmv * yc_ref[...].astype(F32) * sc * (1.0 - sc)).astype(BF16)
        dg_ref[:, d:] = (dmv * ya_ref[...].astype(F32) * sa * (1.0 - sa)).astype(BF16)

    return _pcall(body, name=name,
                  out_shape=(jax.ShapeDtypeStruct((t, d), BF16), jax.ShapeDtypeStruct((t, d), BF16),
                             jax.ShapeDtypeStruct((t, 2 * d), BF16)),
                  grid=(t // tt,),
                  in_specs=[_rows(tt, d), _whole((d, d)), _rows(tt, 2 * d), _rows(tt, d), _rows(tt, d)],
                  out_specs=(_rows(tt, d), _rows(tt, d), _rows(tt, 2 * d)),
                  compiler_params=_params(("parallel",)))(dh, w, gates, yc, ya)


def _shift_down(cur, prev8, s):
    tt = cur.shape[0]
    rolled = pltpu.roll(cur, s, 0)
    row8 = lax.broadcasted_iota(jnp.int32, prev8.shape, 0)
    first8 = jnp.where(row8 < s, pltpu.roll(prev8, s, 0), rolled[:8])
    return jnp.concatenate([first8, rolled[8:]], axis=0) if tt > 8 else first8


def _shift_up(cur, next8, s):
    tt = cur.shape[0]
    rolled = pltpu.roll(cur, tt - s, 0)
    row8 = lax.broadcasted_iota(jnp.int32, next8.shape, 0)
    last8 = jnp.where(row8 >= 8 - s, pltpu.roll(next8, 8 - s, 0), rolled[tt - 8:])
    return jnp.concatenate([rolled[:tt - 8], last8], axis=0) if tt > 8 else last8


def _prev_rows(tt, d, col):
    return pl.BlockSpec((BF16_ROWS, d), lambda i: (jnp.maximum(i * (tt // BF16_ROWS) - 1, 0), col))


def _next_rows(tt, d, col, t):
    return pl.BlockSpec((BF16_ROWS, d),
                        lambda i: (jnp.minimum((i + 1) * (tt // BF16_ROWS), t // BF16_ROWS - 1), col))


def conv_out_fwd(name, cbx, cw8, w_out, tt):
    t, d3 = cbx.shape
    d = d3 // 3

    def body(cb_ref, cc_ref, cx_ref, pc_ref, px_ref, w_ref, wo_ref, o_ref, y_ref):
        has_prev = (pl.program_id(0) > 0).astype(F32)
        cc = cc_ref[...].astype(F32) * cx_ref[...].astype(F32)
        prev = pc_ref[...].astype(F32)[8:] * px_ref[...].astype(F32)[8:] * has_prev
        w = w_ref[...]
        conv = w[0:1] * _shift_down(cc, prev, 2) + w[1:2] * _shift_down(cc, prev, 1) + w[2:3] * cc
        ycin = (cb_ref[...].astype(F32) * conv).astype(BF16)
        o_ref[...] = ycin
        y_ref[...] = _dot(ycin, wo_ref[...], NN).astype(BF16)

    out = jax.ShapeDtypeStruct((t, d), BF16)
    return _pcall(body, name=name, out_shape=(out, out), grid=(t // tt,),
                  in_specs=[_rows(tt, d, 0), _rows(tt, d, 1), _rows(tt, d, 2), _prev_rows(tt, d, 1),
                            _prev_rows(tt, d, 2), _whole((8, d)), _whole((d, d))],
                  out_specs=(_rows(tt, d), _rows(tt, d)),
                  compiler_params=_params(("parallel",)))(cbx, cbx, cbx, cbx, cbx, cw8, w_out)


def conv_out_bwd(name, dyc, w_out, cbx, cw8, tt):
    t, d3 = cbx.shape
    d = d3 // 3
    n = t // tt

    def body(dy_ref, ndy_ref, wo_ref, cb_ref, cc_ref, cx_ref, pc_ref, px_ref, ncb_ref, w_ref, o_ref, dw_ref):
        i = pl.program_id(0)
        has_prev = (i > 0).astype(F32)
        has_next = (i < n - 1).astype(F32)
        cb = cb_ref[...].astype(F32)
        ccv = cc_ref[...].astype(F32)
        cxv = cx_ref[...].astype(F32)
        cc = ccv * cxv
        prev = pc_ref[...].astype(F32)[8:] * px_ref[...].astype(F32)[8:] * has_prev
        w = w_ref[...]
        cc1 = _shift_down(cc, prev, 1)
        cc2 = _shift_down(cc, prev, 2)
        conv = w[0:1] * cc2 + w[1:2] * cc1 + w[2:3] * cc
        dyv = _dot(dy_ref[...], wo_ref[...], NT)
        dconv = dyv * cb
        dnext = _dot(ndy_ref[...], wo_ref[...], NT)[:8] * ncb_ref[...].astype(F32)[:8] * has_next
        dcc = w[2:3] * dconv + w[1:2] * _shift_up(dconv, dnext, 1) + w[0:1] * _shift_up(dconv, dnext, 2)
        o_ref[:, :d] = (dyv * conv).astype(BF16)
        o_ref[:, d:2 * d] = (dcc * cxv).astype(BF16)
        o_ref[:, 2 * d:] = (dcc * ccv).astype(BF16)

        @pl.when(i == 0)
        def _():
            dw_ref[...] = jnp.zeros_like(dw_ref)

        dw_ref[0:1, :] += jnp.sum(dconv * cc2, axis=0, keepdims=True)
        dw_ref[1:2, :] += jnp.sum(dconv * cc1, axis=0, keepdims=True)
        dw_ref[2:3, :] += jnp.sum(dconv * cc, axis=0, keepdims=True)

    return _pcall(body, name=name,
                  out_shape=(jax.ShapeDtypeStruct((t, d3), BF16), jax.ShapeDtypeStruct((8, d), F32)),
                  grid=(n,),
                  in_specs=[_rows(tt, d), _next_rows(tt, d, 0, t),
                            _whole((d, d)), _rows(tt, d, 0), _rows(tt, d, 1), _rows(tt, d, 2),
                            _prev_rows(tt, d, 1), _prev_rows(tt, d, 2), _next_rows(tt, d, 0, t), _whole((8, d))],
                  out_specs=(_rows(tt, d3), _whole((8, d))),
                  compiler_params=_params(("arbitrary",)))(dyc, dyc, w_out, cbx, cbx, cbx, cbx, cbx, cbx, cw8)


def tail(name, h3, p, tgt, gp, gf, w_gate, w_proj, tt):
    t, d = h3.shape
    pd = p.shape[1]

    def body(h_ref, p_ref, tg_ref, gp_ref, gf_ref, wg_ref, wp_ref, np_ref, dh_ref, dpp_ref, dzg_ref, dgf_ref,
             loss_ref):
        hv = h_ref[...]
        npl = (hv * _rstd(hv) * gp_ref[...]).astype(BF16)
        np_ref[...] = npl
        pg = jax.nn.sigmoid(_dot(npl, wg_ref[...], NN))
        ppv = _dot(p_ref[...].astype(BF16), wp_ref[...], NN)
        h4 = hv + pg * ppv
        r4 = _rstd(h4)
        hn = h4 * r4
        gfv = gf_ref[...]
        err = hn * gfv - tg_ref[...]
        dy = err * (1.0 / d)
        gy = dy * gfv
        dh4 = r4 * (gy - hn * jnp.mean(gy * hn, axis=-1, keepdims=True))
        dh_ref[...] = dh4
        dpp_ref[...] = (dh4 * pg).astype(BF16)
        dzg_ref[...] = (dh4 * ppv * pg * (1.0 - pg)).astype(BF16)

        @pl.when(pl.program_id(0) == 0)
        def _():
            dgf_ref[...] = jnp.zeros_like(dgf_ref)
            loss_ref[...] = jnp.zeros_like(loss_ref)

        dgf_ref[...] += jnp.sum(dy * hn, axis=0, keepdims=True)
        tok = jnp.mean(err * err, axis=-1, keepdims=True)
        loss_ref[...] += 0.5 * jnp.sum(tok, axis=0, keepdims=True) * jnp.ones((1, loss_ref.shape[1]), F32)

    return _pcall(body, name=name,
                  out_shape=(jax.ShapeDtypeStruct((t, d), BF16), jax.ShapeDtypeStruct((t, d), F32),
                             jax.ShapeDtypeStruct((t, d), BF16), jax.ShapeDtypeStruct((t, d), BF16),
                             jax.ShapeDtypeStruct((1, d), F32), jax.ShapeDtypeStruct((1, d), F32)),
                  grid=(t // tt,),
                  in_specs=[_rows(tt, d), _rows(tt, pd), _rows(tt, d), _whole((1, d)), _whole((1, d)),
                            _whole((d, d)), _whole((pd, d))],
                  out_specs=(_rows(tt, d), _rows(tt, d), _rows(tt, d), _rows(tt, d), _whole((1, d)),
                             _whole((1, d))),
                  compiler_params=_params(("arbitrary",)))(h3, p, tgt, gp, gf, w_gate, w_proj)


SCALE = 1.0 / math.sqrt(HEAD_DIM)


def _log_stick(z):
    return -(jnp.maximum(z, 0.0) + jnp.log(1.0 + jnp.exp(-jnp.abs(z))))


def _tri_sum(x, tri):
    hi = x.astype(BF16)
    lo = (x - hi.astype(F32)).astype(BF16)
    return _dot(hi, tri, NN) + _dot(lo, tri, NN)


KEY_BLOCK = 128
NEAR = 3
THIN_ROWS = 32


def _pad_block(x):
    n = x.shape[0]
    return x if n == KEY_BLOCK else jnp.concatenate([x, jnp.zeros((KEY_BLOCK - n, x.shape[1]), x.dtype)], axis=0)


def _sb_near(qs, jds, k_ref, below, upper, last_rows):
    near_rows = (KEY_BLOCK,) * (NEAR - 1) + (last_rows,)
    pairs = [(s, b) for s in range(len(qs)) for b in range(NEAR)]
    rows = {(s, b): _block_rows(jnp.maximum(jds[s] - b, 0), KEY_BLOCK) for s, b in pairs}
    z = {(s, b): _dot(qs[s][:near_rows[b]], k_ref[rows[s, b], :], NT) * SCALE for s, b in pairs}
    lg = {(s, b): jnp.where(below, _log_stick(z[s, b]), 0.0) if b == 0 else _log_stick(z[s, b]) for s, b in pairs}
    cum = {(s, b): _tri_sum(lg[s, b], upper) for s, b in pairs}
    out, carries = [], []
    for s in range(len(qs)):
        c = cum[s, 0][:, 0:1]
        blocks = [(rows[s, 0], z[s, 0], jnp.exp(jnp.where(below, z[s, 0] + cum[s, 0], -1e30)))]
        for b in range(1, NEAR):
            live = jds[s] >= b
            off = c[:near_rows[b]] + jnp.where(live, 0.0, -1e30)
            blocks.append((rows[s, b], z[s, b], jnp.exp(z[s, b] + cum[s, b] + off)))
            c = c + _pad_block(jnp.where(live, cum[s, b][:, 0:1], 0.0))
        out.append(blocks)
        carries.append(c)
    return out, carries


def _sb_far(q, kj, upper, c, skip):
    z = _dot(q, kj, NT) * SCALE
    cum = _tri_sum(_log_stick(z), upper)
    return z, jnp.exp(z + cum + (c + jnp.where(skip, -1e30, 0.0))), c + jnp.where(skip, 0.0, cum[:, 0:1])


def _took_it(j, jd, last_rows):
    first = lax.broadcasted_iota(jnp.int32, (KEY_BLOCK, 1), 0) < last_rows
    return jnp.logical_and(j == jd - (NEAR - 1), first)


def _block_rows(j, size):
    return pl.ds(pl.multiple_of(j * size, size), size)


def _sweep_on(st):
    return jnp.logical_and(st[0] >= 0, jnp.max(st[1]) > -STICK_EXIT)


def attn_fwd(name, qkv, tq):
    t, d3 = qkv.shape
    d = d3 // 3
    nh = d // HEAD_DIM
    nq = t // tq
    tb = KEY_BLOCK
    nsub = tq // tb

    def body(q_ref, k_ref, v_ref, o_ref):
        i = pl.program_id(1)
        row = lax.broadcasted_iota(jnp.int32, (tb, tb), 0)
        col = lax.broadcasted_iota(jnp.int32, (tb, tb), 1)
        upper = (row >= col).astype(BF16)
        qs = [q_ref[s * tb:(s + 1) * tb, :] for s in range(nsub)]
        jds = [i * nsub + s for s in range(nsub)]
        near, carries = _sb_near(qs, jds, k_ref, col < row, upper, THIN_ROWS)
        state = []
        for s in range(nsub):
            acc = jnp.zeros((tb, HEAD_DIM), F32)
            for rows, _, a in near[s]:
                acc = acc + _pad_block(_dot(a.astype(BF16), v_ref[rows, :], NN))
            state.append((qs[s], jds[s], carries[s], acc))
        for s, (q, jd, c, acc) in enumerate(state):

            def step(st, q=q, jd=jd):
                rows = _block_rows(st[0], tb)
                _, a, c2 = _sb_far(q, k_ref[rows, :], upper, st[1], _took_it(st[0], jd, THIN_ROWS))
                return st[0] - 1, c2, st[2] + _dot(a.astype(BF16), v_ref[rows, :], NN)

            _, _, acc = lax.while_loop(_sweep_on, step, (jd - (NEAR - 1), c, acc))
            o_ref[s * tb:(s + 1) * tb, :] = acc.astype(o_ref.dtype)

    return _pcall(body, name=name, out_shape=jax.ShapeDtypeStruct((t, d), BF16), grid=(nh, nq),
                  in_specs=[pl.BlockSpec((tq, HEAD_DIM), lambda h, i: (i, h)),
                            pl.BlockSpec((t, HEAD_DIM), lambda h, i: (0, nh + h)),
                            pl.BlockSpec((t, HEAD_DIM), lambda h, i: (0, 2 * nh + h))],
                  out_specs=pl.BlockSpec((tq, HEAD_DIM), lambda h, i: (i, h)),
                  compiler_params=_params(("parallel", "arbitrary")))(qkv, qkv, qkv)


def attn_bwd(name, qkv, do, tq):
    t, d3 = qkv.shape
    d = d3 // 3
    nh = d // HEAD_DIM
    nq = t // tq
    tb = KEY_BLOCK
    nsub = tq // tb

    def body(q_ref, k_ref, v_ref, do_ref, dq_ref, dk_ref, dv_ref, dk_acc, dv_acc, g_buf, z_buf):
        i = pl.program_id(1)

        @pl.when(i == 0)
        def _():
            dk_acc[...] = jnp.zeros_like(dk_acc)
            dv_acc[...] = jnp.zeros_like(dv_acc)

        row = lax.broadcasted_iota(jnp.int32, (tb, tb), 0)
        col = lax.broadcasted_iota(jnp.int32, (tb, tb), 1)
        below = col < row
        upper = (row >= col).astype(BF16)
        lower = (row <= col).astype(BF16)

        qs = [q_ref[s * tb:(s + 1) * tb, :] for s in range(nsub)]
        dos = [do_ref[s * tb:(s + 1) * tb, :] for s in range(nsub)]
        jds = [i * nsub + s for s in range(nsub)]
        near, carries = _sb_near(qs, jds, k_ref, below, upper, KEY_BLOCK)
        da = [[_dot(dos[s][:a.shape[0]], v_ref[rows, :], NT) for rows, _, a in near[s]] for s in range(nsub)]
        state = []
        for s in range(nsub):
            kept = [(rows, z, da[s][b] * a) for b, (rows, z, a) in enumerate(near[s])]
            for rows, _, a in near[s]:
                dv_acc[rows, :] += _dot(a.astype(BF16), dos[s][:a.shape[0]], TN)
            state.append((qs[s], dos[s], jds[s], carries[s], kept))

        carried = []
        for s, (q, dov, jd, c, kept) in enumerate(state):
            def step(st, s=s, q=q, dov=dov, jd=jd):
                j = st[0]
                rows = _block_rows(j, tb)
                z, a, c2 = _sb_far(q, k_ref[rows, :], upper, st[1], _took_it(j, jd, KEY_BLOCK))
                g_buf[jd - j] = _dot(dov, v_ref[rows, :], NT) * a
                z_buf[jd - j] = z
                dv_acc[rows, :] += _dot(a.astype(BF16), dov, TN)
                return j - 1, c2

            j_stop, _ = lax.while_loop(_sweep_on, step, (jd - (NEAR - 1), c))

            def far(j, st, s=s, q=q, jd=jd):
                run, dq = st
                rows = _block_rows(j, tb)
                g = g_buf[jd - j]
                dz = (g - jax.nn.sigmoid(z_buf[jd - j]) * (run + _tri_sum(g, lower))).astype(BF16)
                dk_acc[rows, :] += _dot(dz, q, TN)
                return run + jnp.sum(g, axis=1, keepdims=True), dq + _dot(dz, k_ref[rows, :], NN)

            carried.append(lax.fori_loop(j_stop + 1, jd - (NEAR - 1) + 1, far,
                                         (jnp.zeros((tb, 1), F32), jnp.zeros((tb, HEAD_DIM), F32))))

        tri = [[_dot(g.astype(BF16), lower, NN) for _, _, g in st[4]] for st in state]
        sig = [[jax.nn.sigmoid(z) for _, z, _ in st[4]] for st in state]
        for s, (q, dov, jd, c, kept) in enumerate(state):
            run, dq = carried[s]
            for b in reversed(range(NEAR)):
                rows, z, g = kept[b]
                n = g.shape[0]
                dz = g - sig[s][b] * (run[:n] + tri[s][b])
                if b == 0:
                    dz = jnp.where(below, dz, 0.0)
                dz = dz.astype(BF16)
                dk_acc[rows, :] += _dot(dz, q[:n], TN)
                dq = dq + _pad_block(_dot(dz, k_ref[rows, :], NN))
                if b:
                    run = run + _pad_block(jnp.sum(g, axis=1, keepdims=True))
            dq_ref[s * tb:(s + 1) * tb, :] = (dq * SCALE).astype(BF16)

        @pl.when(i == nq - 1)
        def _():
            dk_ref[...] = (dk_acc[...] * SCALE).astype(BF16)
            dv_ref[...] = dv_acc[...].astype(BF16)

    blk = pl.BlockSpec((tq, HEAD_DIM), lambda h, i: (i, h))
    col_h = pl.BlockSpec((t, HEAD_DIM), lambda h, i: (0, h))
    out = jax.ShapeDtypeStruct((t, d), BF16)
    return _pcall(body, name=name, out_shape=(out, out, out), grid=(nh, nq),
                  in_specs=[blk,
                            pl.BlockSpec((t, HEAD_DIM), lambda h, i: (0, nh + h)),
                            pl.BlockSpec((t, HEAD_DIM), lambda h, i: (0, 2 * nh + h)),
                            blk],
                  out_specs=(blk, col_h, col_h),
                  scratch_shapes=[pltpu.VMEM((t, HEAD_DIM), F32), pltpu.VMEM((t, HEAD_DIM), F32),
                                  pltpu.VMEM((t // tb, tb, tb), F32), pltpu.VMEM((t // tb, tb, tb), F32)],
                  compiler_params=_params(("parallel", "arbitrary")))(qkv, qkv, qkv, do)


def _place():
    x, y, c = lax.axis_index("x"), lax.axis_index("y"), lax.axis_index("c")
    chips = [(1 - x, y), (x, 1 - y), (1 - x, 1 - y)]
    return x, y, c, chips


def _remote(src, dst, send_sem, recv_sem, dev):
    return pltpu.make_async_remote_copy(src_ref=src, dst_ref=dst, send_sem=send_sem, recv_sem=recv_sem,
                                        device_id=dev, device_id_type=MESH)


def place_shards(name, ws, chip):
    tiles, steps = _job_tiles([w.shape for w in ws], 1 << 20, BF16_ROWS)
    nj = len(ws)

    def body(chip_ref, *refs):
        i = pl.program_id(0)
        for k, (_, n) in enumerate(tiles):
            @pl.when(i < n)
            def _(w_ref=refs[k], o_ref=refs[nj + k]):
                o_ref[...] = w_ref[...].astype(BF16)

    spec = pltpu.PrefetchScalarGridSpec(
        num_scalar_prefetch=1, grid=(steps,),
        in_specs=[pl.BlockSpec((tr, w.shape[1]), lambda i, s, n=n: (jnp.minimum(i, n - 1), 0))
                  for w, (tr, n) in zip(ws, tiles)],
        out_specs=[pl.BlockSpec((None, tr, w.shape[1]), lambda i, s, n=n: (s[0], jnp.minimum(i, n - 1), 0))
                   for w, (tr, n) in zip(ws, tiles)])
    return _pcall(body, name=name, out_shape=[jax.ShapeDtypeStruct((N_CHIPS,) + w.shape, BF16) for w in ws],
                  grid_spec=spec, compiler_params=_params(("arbitrary",)))(chip, *ws)


class Comm:
    def __init__(self, ins, outs, aliases, sems, first, mid, last):
        self.ins, self.outs, self.aliases, self.sems = list(ins), list(outs), dict(aliases), list(sems)
        self.first, self.mid, self.last = first, mid, last


def run_comm(name, comm):
    ni, no = len(comm.ins), len(comm.outs)

    def body(*refs):
        ins, outs, sems = refs[:ni], refs[ni:ni + no], refs[ni + no:]
        comm.first(ins, outs, sems)
        comm.mid(ins, outs, sems)
        comm.last(ins, outs, sems)

    return _pcall(body, name=name, out_shape=comm.outs, in_specs=[ANY] * ni, out_specs=[ANY] * no,
                  input_output_aliases=comm.aliases, scratch_shapes=comm.sems, compiler_params=_params())(*comm.ins)


def gather_comm(bufs):
    n = len(bufs)

    def half(out, w, which):
        pr = out[w].shape[1] // 2
        return pl.ds(pl.multiple_of(which * pr, BF16_ROWS), pr)

    def first(ins, out, sems):
        isend, irecv, _, _ = sems
        x, y, c, chips = _place()
        for w in range(n):
            mine = out[w].at[2 * x + y, half(out, w, c)]
            for j, (cx, cy) in enumerate(chips):
                _remote(mine, mine, isend.at[3 * w + j], irecv.at[3 * w + j], (cx, cy, c)).start()

    def mid(ins, out, sems):
        isend, irecv, dsend, drecv = sems
        x, y, c, chips = _place()
        sib = (x, y, 1 - c)
        for w in range(n):
            for j, (cx, cy) in enumerate(chips):
                landed = out[w].at[2 * cx + cy, half(out, w, c)]
                _remote(landed, landed, isend.at[3 * w + j], irecv.at[3 * w + j], sib).wait_recv()
                _remote(landed, landed, dsend.at[3 * w + j], drecv.at[3 * w + j], sib).start()

    def last(ins, out, sems):
        isend, irecv, dsend, drecv = sems
        x, y, c, chips = _place()
        sib = (x, y, 1 - c)
        for w in range(n):
            for j, (cx, cy) in enumerate(chips):
                landed = out[w].at[2 * cx + cy, half(out, w, 1 - c)]
                _remote(landed, landed, dsend.at[3 * w + j], drecv.at[3 * w + j], sib).wait_recv()
        for w in range(n):
            sent = out[w].at[0, half(out, w, c)]
            for j in range(3):
                _remote(sent, sent, isend.at[3 * w + j], irecv.at[3 * w + j], sib).wait_send()
                _remote(sent, sent, dsend.at[3 * w + j], drecv.at[3 * w + j], sib).wait_send()

    return Comm(bufs, [jax.ShapeDtypeStruct(s.shape, s.dtype) for s in bufs], {w: w for w in range(n)},
                [pltpu.SemaphoreType.DMA((3 * n,))] * 4, first, mid, last)


def _nothing(ins, outs, sems):
    return None


def join_comms(a, b):
    ni, no, ns = len(a.ins), len(a.outs), len(a.sems)

    def both(f, g):
        def hook(ins, outs, sems):
            f(ins[:ni], outs[:no], sems[:ns])
            g(ins[ni:], outs[no:], sems[ns:])
        return hook

    aliases = dict(a.aliases)
    aliases.update({ni + k: no + v for k, v in b.aliases.items()})
    return Comm(a.ins + b.ins, a.outs + b.outs, aliases, a.sems + b.sems,
                both(a.first, b.first), both(a.mid, b.mid), both(a.last, b.last))


def exchange_comm(pieces):
    n = len(pieces)

    def copies(src, out, sems):
        x, y, c, _ = _place()
        return [_remote(src[w].at[k, 1 - c], out[w].at[k], sems[0].at[N_CHIPS * w + k], sems[1].at[N_CHIPS * w + k],
                        (x, y, 1 - c)) for w in range(n) for k in range(N_CHIPS)]

    def first(src, out, sems):
        for cp in copies(src, out, sems):
            cp.start()

    def last(src, out, sems):
        for cp in copies(src, out, sems):
            cp.wait()

    return Comm(pieces, [jax.ShapeDtypeStruct((N_CHIPS,) + s.shape[2:], s.dtype) for s in pieces], {},
                [pltpu.SemaphoreType.DMA((N_CHIPS * n,))] * 2, first, _nothing, last)


def scatter_comm(parts):
    n = len(parts)

    def copies(src, out, sems):
        x, y, c, chips = _place()
        return [_remote(src[w].at[2 * cx + cy], out[w].at[j], sems[0].at[3 * w + j], sems[1].at[3 * w + j], (cx, cy, c))
                for w in range(n) for j, (cx, cy) in enumerate(chips)]

    def first(src, out, sems):
        for cp in copies(src, out, sems):
            cp.start()

    def last(src, out, sems):
        for cp in copies(src, out, sems):
            cp.wait()

    return Comm(parts, [jax.ShapeDtypeStruct((3,) + s.shape[1:], s.dtype) for s in parts], {},
                [pltpu.SemaphoreType.DMA((3 * n,))] * 2, first, _nothing, last)


def share_comm(halves):
    n = len(halves)

    def first(ins, buf, sems):
        x, y, c, _ = _place()
        for w in range(n):
            _remote(buf[w].at[c], buf[w].at[c], sems[0].at[w], sems[1].at[w], (x, y, 1 - c)).start()

    def last(ins, buf, sems):
        x, y, c, _ = _place()
        for w in range(n):
            landed = buf[w].at[1 - c]
            _remote(landed, landed, sems[0].at[w], sems[1].at[w], (x, y, 1 - c)).wait_recv()
        for w in range(n):
            _remote(buf[w].at[c], buf[w].at[c], sems[0].at[w], sems[1].at[w], (x, y, 1 - c)).wait_send()

    return Comm(halves, [jax.ShapeDtypeStruct(s.shape, s.dtype) for s in halves], {w: w for w in range(n)},
                [pltpu.SemaphoreType.DMA((n,))] * 2, first, _nothing, last)


def gather_small(name, blk, reduce):
    r, cdim = blk.shape

    def body(in_ref, out_ref, *rest):
        if reduce:
            buf, send_sem, recv_sem = rest
        else:
            buf = out_ref
            send_sem, recv_sem = rest
        x, y, c, _ = _place()
        me = 4 * x + 2 * y + c
        buf[me] = in_ref[...]
        peers = []
        for dx in range(2):
            for dy in range(2):
                for dc in range(2):
                    if dx or dy or dc:
                        peers.append((dx, dy, dc))
        copies = []
        for s, (dx, dy, dc) in enumerate(peers):
            cp = _remote(in_ref, buf.at[me], send_sem.at[s], recv_sem.at[s],
                         ((1 - x if dx else x), (1 - y if dy else y), (1 - c if dc else c)))
            cp.start()
            copies.append(cp)
        for s, (dx, dy, dc) in enumerate(peers):
            px, py, pc_ = (1 - x if dx else x), (1 - y if dy else y), (1 - c if dc else c)
            landed = buf.at[4 * px + 2 * py + pc_]
            _remote(landed, landed, send_sem.at[s], recv_sem.at[s], (x, y, c)).wait_recv()
        for cp in copies:
            cp.wait_send()
        if reduce:
            tot = buf[0]
            for s in range(1, N_DEV):
                tot = tot + buf[s]
            out_ref[...] = tot

    vm = pl.BlockSpec(memory_space=pltpu.VMEM)
    out_shape = jax.ShapeDtypeStruct((r, cdim) if reduce else (N_DEV, r, cdim), F32)
    scratch = ([pltpu.VMEM((N_DEV, r, cdim), F32)] if reduce else []) + [pltpu.SemaphoreType.DMA((N_DEV - 1,))] * 2
    return _pcall(body, name=name, out_shape=out_shape, in_specs=[vm], out_specs=vm, scratch_shapes=scratch,
                  compiler_params=_params())(blk)


def _job_tiles(shapes, tile_bytes, mult):
    tiles = []
    for rows, cols in shapes:
        tr = _tile(rows, max(mult, tile_bytes // (4 * cols)), mult)
        tiles.append((tr, rows // tr))
    return tiles, max(n for _, n in tiles)


def sum_cores(name, owns, gots, place):
    nj = len(owns)
    tiles, _ = _job_tiles([o.shape[2:] for o in owns], 1 << 21, BF16_ROWS)
    steps = max(N_CHIPS * n for _, n in tiles)

    def body(place_ref, *refs):
        i = pl.program_id(0)
        for k, (_, n) in enumerate(tiles):
            @pl.when(i < N_CHIPS * n)
            def _(own_ref=refs[2 * k], got_ref=refs[2 * k + 1], o_ref=refs[2 * nj + k]):
                o_ref[...] = (own_ref[...].astype(F32) + got_ref[...].astype(F32)).astype(o_ref.dtype)

    in_specs, out_specs, out_shape, args = [], [], [], []
    for own, got, (tr, n) in zip(owns, gots, tiles):
        pc = own.shape[3]
        last = N_CHIPS * n - 1
        in_specs += [pl.BlockSpec((None, None, tr, pc),
                                  lambda i, s, n=n, last=last: (jnp.minimum(i, last) // n, s[1], jnp.minimum(i, last) % n, 0)),
                     pl.BlockSpec((None, tr, pc),
                                  lambda i, s, n=n, last=last: (jnp.minimum(i, last) // n, jnp.minimum(i, last) % n, 0))]
        out_specs.append(pl.BlockSpec((None, tr, pc),
                                      lambda i, s, n=n, last=last: (jnp.minimum(i, last) // n, jnp.minimum(i, last) % n, 0)))
        out_shape.append(jax.ShapeDtypeStruct(got.shape, BF16))
        args += [own, got]
    spec = pltpu.PrefetchScalarGridSpec(num_scalar_prefetch=1, grid=(steps,), in_specs=in_specs, out_specs=out_specs)
    return _pcall(body, name=name, out_shape=out_shape, grid_spec=spec,
                  compiler_params=_params(("arbitrary",)))(place, *args)


def sum_chips(name, parts, gots, place):
    nj = len(parts)
    tiles, steps = _job_tiles([p.shape[1:] for p in parts], 1 << 20, BF16_ROWS)

    def body(place_ref, *refs):
        i = pl.program_id(0)
        for k, (_, n) in enumerate(tiles):
            @pl.when(i < n)
            def _(part_ref=refs[2 * k], got_ref=refs[2 * k + 1], o_ref=refs[2 * nj + k]):
                tot = part_ref[...].astype(F32)
                for j in range(3):
                    tot = tot + got_ref[j].astype(F32)
                o_ref[...] = tot

    in_specs, out_specs, out_shape, args = [], [], [], []
    for part, got, (tr, n) in zip(parts, gots, tiles):
        pc = part.shape[2]
        in_specs += [pl.BlockSpec((None, tr, pc), lambda i, s, n=n: (s[0], jnp.minimum(i, n - 1), 0)),
                     pl.BlockSpec((3, tr, pc), lambda i, s, n=n: (0, jnp.minimum(i, n - 1), 0))]
        out_specs.append(pl.BlockSpec((None, tr, pc), lambda i, s, n=n: (s[1], jnp.minimum(i, n - 1), 0)))
        out_shape.append(jax.ShapeDtypeStruct((2,) + part.shape[1:], F32))
        args += [part, got]
    spec = pltpu.PrefetchScalarGridSpec(num_scalar_prefetch=1, grid=(steps,), in_specs=in_specs, out_specs=out_specs)
    return _pcall(body, name=name, out_shape=out_shape, grid_spec=spec,
                  compiler_params=_params(("arbitrary",)))(place, *args)


def adamw(name, jobs):
    c1 = 1.0 / (1.0 - ADAM_B1 ** ADAM_STEP)
    c2 = 1.0 / (1.0 - ADAM_B2 ** ADAM_STEP)
    nj = len(jobs)
    tiles, steps = _job_tiles([j[0].shape for j in jobs], 1 << 18, 8)

    def body(*refs):
        i = pl.program_id(0)
        for k, (_, n) in enumerate(tiles):
            w_ref, g_ref, m_ref, v_ref = refs[4 * k:4 * k + 4]
            d_ref, nm_ref, nv_ref = refs[4 * nj + 3 * k:4 * nj + 3 * k + 3]

            @pl.when(i < n)
            def _(w_ref=w_ref, g_ref=g_ref, m_ref=m_ref, v_ref=v_ref, d_ref=d_ref, nm_ref=nm_ref, nv_ref=nv_ref):
                gv = g_ref[...]
                nm = ADAM_B1 * m_ref[...] + (1.0 - ADAM_B1) * gv
                nv = ADAM_B2 * v_ref[...] + (1.0 - ADAM_B2) * (gv * gv)
                nm_ref[...] = nm
                nv_ref[...] = nv
                d_ref[...] = -ADAM_LR * ((nm * c1) / (jnp.sqrt(nv * c2) + ADAM_EPS) + ADAM_WD * w_ref[...])

    in_specs, out_specs, out_shape, args = [], [], [], []
    for (w, g, m, v), (tr, n) in zip(jobs, tiles):
        spec = pl.BlockSpec((tr, w.shape[1]), lambda i, n=n: (jnp.minimum(i, n - 1), 0))
        in_specs += [spec] * 4
        out_specs += [spec] * 3
        out_shape += [jax.ShapeDtypeStruct(w.shape, F32)] * 3
        args += [w, g, m, v]
    res = _pcall(body, name=name, out_shape=out_shape, grid=(steps,), in_specs=in_specs, out_specs=out_specs,
                 compiler_params=_params(("arbitrary",)))(*args)
    return [tuple(res[3 * k:3 * k + 3]) for k in range(nj)]


MATS = ["ffn1_w_in", "ffn1_w_out", "w_mix_in", "w_conv_out", "w_attn_out", "w_mix_out", "ffn2_w_in", "ffn2_w_out",
        "w_ple_gate", "w_ple_proj"]
COL_SHARDED = {"ffn1_w_in", "w_mix_in", "ffn2_w_in", "w_ple_proj"}
NORMS = ["ffn1_norm", "mix_norm", "ffn2_norm", "ple_norm", "final_norm"]
WEIGHTS = ["ffn1_norm", "ffn1_w_in", "ffn1_w_out", "mix_norm", "w_mix_in", "conv_w", "w_conv_out", "w_attn_out",
           "w_mix_out", "ffn2_norm", "ffn2_w_in", "ffn2_w_out", "ple_norm", "w_ple_gate", "w_ple_proj", "final_norm"]


def _pad_rows(a, rows):
    return jnp.concatenate([a, jnp.zeros((rows - a.shape[0],) + a.shape[1:], a.dtype)], axis=0)


def _step(x, p, tgt, w, m, v):
    t, d = x.shape
    tt = _tile(t, 256)
    tm = _tile(t, 512)
    tm2 = _tile(t, 1024)
    tq = _tile(t, 1024)

    chip = 2 * lax.axis_index("x") + lax.axis_index("y")
    place = jnp.stack([chip, lax.axis_index("c")]).astype(jnp.int32)

    placed = dict(zip(MATS, place_shards("place_shards", [w[k] for k in MATS], place)))
    full = {}

    def keep(names, bufs):
        for k, buf in zip(names, bufs):
            full[k] = buf if k in COL_SHARDED else buf.reshape(-1, buf.shape[2])

    def gather_of(names):
        return gather_comm([placed[k] for k in names])

    cw_all = gather_small("gather_conv_w", _pad_rows(w["conv_w"], 8), False)
    cw8 = jnp.concatenate([cw_all[2 * k] for k in range(N_CHIPS)], axis=1)
    g1, gm, g2, gp, gf = (w[k].reshape(1, d) for k in NORMS)

    def ffn_fwd(tag, h, g, first, w_in_name, w_out_name, riders):
        if first:
            n, bufs = rms_fwd(tag + "_norm", h, g, tt, comm=gather_of(first))
            keep(first, bufs)
            (a, s), bufs = ffn_in_act(tag + "_in", n, full[w_in_name], tm, comm=gather_of(riders))
            keep(riders, bufs)
        else:
            a, s, n = ffn_in_act(tag + "_in", h, full[w_in_name], tm, gain=g)
        return n, a, s, mm_nn(tag + "_out", s, full[w_out_name], F32, tm, res=h, alpha=0.5)

    n1, a1, s1, h1 = ffn_fwd("ffn1", x, g1, ["ffn1_w_in"], "ffn1_w_in", "ffn1_w_out", ["ffn1_w_out", "w_mix_in"])
    wmix = full["w_mix_in"]
    riders = [["w_conv_out", "w_attn_out", "w_mix_out"], ["ffn2_w_in"], ["ffn2_w_out", "w_ple_gate", "w_ple_proj"]]
    (cbx, u), bufs = mm_nn_stacked("mix_in_conv", h1, wmix, BF16, tm2, d, 0, 3, comm=gather_of(riders[0]), gain=gm)
    keep(riders[0], bufs)
    qkv, bufs = mm_nn_stacked("mix_in_qkv", u, wmix, BF16, tm2, d, 3, 3, comm=gather_of(riders[1]))
    keep(riders[1], bufs)
    gates, bufs = mm_nn_stacked("mix_in_gates", u, wmix, BF16, tm2, d, 6, 2, comm=gather_of(riders[2]))
    keep(riders[2], bufs)
    wpp = full["w_ple_proj"]
    wpp = jnp.transpose(wpp, (1, 0, 2)).reshape(wpp.shape[1], -1)
    ycin, y_conv = conv_out_fwd("conv_out", cbx, cw8, full["w_conv_out"], tt)
    o = attn_fwd("attn", qkv, tq)
    y_attn = mm_nn("attn_out", o, full["w_attn_out"], BF16, tm)
    merged, h2 = mix_out_fwd("mix_out", gates, y_conv, y_attn, h1, full["w_mix_out"], tm)
    n2, a2, s2, h3 = ffn_fwd("ffn2", h2, g2, [], "ffn2_w_in", "ffn2_w_out", [])

    pieces, chip_sums, halves = {}, {}, {}

    def as_pieces(k):
        pc = pieces[k]
        return pc if k in COL_SHARDED else pc.reshape(N_CHIPS, 2, pc.shape[0] // (2 * N_CHIPS), pc.shape[1])

    def sum_siblings(tag, names):
        pcs = [as_pieces(k) for k in names]
        got = run_comm("exchange_" + tag, exchange_comm(pcs))
        chip_sums.update(zip(names, sum_cores("sum_cores_" + tag, pcs, got, place)))

    def scatter_of(names):
        return scatter_comm([chip_sums[k] for k in names])

    def sum_landed(tag, names, landed):
        halves.update(zip(names, sum_chips("sum_chips_" + tag, [chip_sums[k] for k in names], landed, place)))

    npl, dh4, dpp, dzg, dgf, loss_row = tail("tail", h3, p, tgt, gp, gf, full["w_ple_gate"], wpp, tt)
    dwpp = mm_tn_whole("ple_proj_dw", p, dpp, tm2)
    pieces["w_ple_proj"] = jnp.transpose(dwpp.reshape(2, p.shape[1] // 2, N_CHIPS, d // N_CHIPS), (2, 0, 1, 3))
    pieces["w_ple_gate"] = mm_tn_rows("ple_gate_dw", npl, dzg, tm2)
    dh3, df2, dgp = mm_nt("ple_gate_dx", dzg, full["w_ple_gate"], F32, tm, d, norm=(h3, gp, dh4), alpha=0.5)
    w_in, w_out = full["ffn2_w_in"], full["ffn2_w_out"]
    pieces["ffn2_w_out"] = mm_tn_rows("ffn2_dwout", s2, df2, tm2)
    da2 = ffn_ds_dact("ffn2_ds", df2, w_out, a2, tm2)
    pieces["ffn2_w_in"] = mm_tn_cols("ffn2_dwin", n2, da2, tm2)
    dh2, dh2b, dg2 = mm_nt_stacked("ffn2_dn", da2, w_in, tm2, (h2, g2, dh3))
    pieces["w_mix_out"] = mm_tn_rows("mix_out_dw", merged, dh2b, tm2)
    dyc, dya, dgates = mix_out_bwd("mix_out_dx", dh2b, full["w_mix_out"], gates, y_conv, y_attn, tm)
    pieces["w_conv_out"] = mm_tn_rows("conv_out_dw", ycin, dyc, tm2)
    dcbx, dcw8 = conv_out_bwd("conv_out_dx", dyc, full["w_conv_out"], cbx, cw8, tt)
    pieces["w_attn_out"] = mm_tn_rows("attn_out_dw", o, dya, tm2)
    do = mm_nt("attn_out_dx", dya, full["w_attn_out"], BF16, tm, d)
    dq, dk, dv = attn_bwd("attn_bwd", qkv, do, tq)
    dmix = [dcbx, dq, dk, dv, dgates]
    early = ["ffn2_w_in", "ffn2_w_out", "w_ple_gate", "w_ple_proj", "w_mix_out", "w_conv_out", "w_attn_out"]
    swap = exchange_comm([as_pieces(k) for k in early])
    pieces["w_mix_in"], got = mm_tn_parts("mix_in_dw", u, dmix, tm2, comm=swap)
    chip_sums.update(zip(early, sum_cores("sum_cores_early", swap.ins, got, place)))
    swap = exchange_comm([as_pieces("w_mix_in")])
    (dh1, df1, dgm), landed = mm_nt_parts("mix_in_dx", dmix, wmix, tm2, (h1, gm, dh2), 0.5,
                                          comm=join_comms(scatter_of(early), swap))
    sum_landed("early", early, landed[:len(early)])
    chip_sums["w_mix_in"] = sum_cores("sum_cores_mix", swap.ins, landed[len(early):], place)[0]
    w_in, w_out = full["ffn1_w_in"], full["ffn1_w_out"]
    pieces["ffn1_w_out"] = mm_tn_rows("ffn1_dwout", s1, df1, tm2)
    da1 = ffn_ds_dact("ffn1_ds", df1, w_out, a1, tm2)
    pieces["ffn1_w_in"], landed = mm_tn_cols("ffn1_dwin", n1, da1, tm2, comm=scatter_of(["w_mix_in"]))
    sum_landed("mix", ["w_mix_in"], landed)
    late = ["ffn1_w_in", "ffn1_w_out"]
    sum_siblings("late", late)
    done = early + ["w_mix_in"]
    (dx, _, dg1), landed = mm_nt_stacked(
        "ffn1_dn", da1, w_in, tm2, (x, g1, dh1),
        comm=join_comms(scatter_of(late), share_comm([halves[k] for k in done])))
    sum_landed("late", late, landed[:len(late)])
    shared = dict(zip(done, landed[len(late):]))

    shared.update(zip(late, run_comm("share_halves", share_comm([halves[k] for k in late]))))
    grad, delta, new_m, new_v = {}, {}, {}, {}
    for k in MATS:
        grad[k] = shared[k].reshape(w[k].shape)

    small = jnp.concatenate([dg1, dgm, dg2, dgp, dgf, dcw8[:3], loss_row, jnp.zeros((7, d), F32)], axis=0)
    tot = gather_small("sum_small", small, True)
    loss = tot[8, 0]
    norm_w = jnp.concatenate([w[k].reshape(1, d) for k in NORMS] + [jnp.zeros((3, d), F32)], axis=0)
    norm_m = jnp.concatenate([m[k].reshape(1, d) for k in NORMS] + [jnp.zeros((3, d), F32)], axis=0)
    norm_v = jnp.concatenate([v[k].reshape(1, d) for k in NORMS] + [jnp.ones((3, d), F32)], axis=0)
    norm_g = jnp.concatenate([tot[0:5], jnp.zeros((3, d), F32)], axis=0)
    cs = d // N_CHIPS
    gcw = lax.dynamic_slice(tot[5:8], (0, chip * cs), (3, cs))
    conv_job = (_pad_rows(w["conv_w"], 8), _pad_rows(gcw, 8), _pad_rows(m["conv_w"], 8),
                jnp.concatenate([v["conv_w"], jnp.ones((5, cs), F32)], axis=0))

    steps = adamw("adamw", [(w[k], grad[k], m[k], v[k]) for k in MATS]
                  + [(norm_w, norm_g, norm_m, norm_v), conv_job])
    for k, res in zip(MATS, steps):
        delta[k], new_m[k], new_v[k] = res
    nd, nm, nv = steps[len(MATS)]
    for r, k in enumerate(NORMS):
        grad[k] = norm_g[r].reshape(w[k].shape)
        delta[k], new_m[k], new_v[k] = (a[r].reshape(w[k].shape) for a in (nd, nm, nv))
    cd, cm, cv = steps[len(MATS) + 1]
    grad["conv_w"], delta["conv_w"], new_m["conv_w"], new_v["conv_w"] = gcw, cd[:3], cm[:3], cv[:3]
    return loss, dx, grad, delta, new_m, new_v


def kernel(x, p, ffn1_norm, ffn1_w_in, ffn1_w_out, mix_norm, w_mix_in, conv_w, w_conv_out, w_attn_out, w_mix_out, ffn2_norm, ffn2_w_in, ffn2_w_out, ple_norm, w_ple_gate, w_ple_proj, final_norm, loss_target, m_ffn1_norm, m_ffn1_w_in, m_ffn1_w_out, m_mix_norm, m_w_mix_in, m_conv_w, m_w_conv_out, m_w_attn_out, m_w_mix_out, m_ffn2_norm, m_ffn2_w_in, m_ffn2_w_out, m_ple_norm, m_w_ple_gate, m_w_ple_proj, m_final_norm, v_ffn1_norm, v_ffn1_w_in, v_ffn1_w_out, v_mix_norm, v_w_mix_in, v_conv_w, v_w_conv_out, v_w_attn_out, v_w_mix_out, v_ffn2_norm, v_ffn2_w_in, v_ffn2_w_out, v_ple_norm, v_w_ple_gate, v_w_ple_proj, v_final_norm):
    ws = (ffn1_norm, ffn1_w_in, ffn1_w_out, mix_norm, w_mix_in, conv_w, w_conv_out, w_attn_out, w_mix_out, ffn2_norm,
          ffn2_w_in, ffn2_w_out, ple_norm, w_ple_gate, w_ple_proj, final_norm)
    ms = (m_ffn1_norm, m_ffn1_w_in, m_ffn1_w_out, m_mix_norm, m_w_mix_in, m_conv_w, m_w_conv_out, m_w_attn_out,
          m_w_mix_out, m_ffn2_norm, m_ffn2_w_in, m_ffn2_w_out, m_ple_norm, m_w_ple_gate, m_w_ple_proj, m_final_norm)
    vs = (v_ffn1_norm, v_ffn1_w_in, v_ffn1_w_out, v_mix_norm, v_w_mix_in, v_conv_w, v_w_conv_out, v_w_attn_out,
          v_w_mix_out, v_ffn2_norm, v_ffn2_w_in, v_ffn2_w_out, v_ple_norm, v_w_ple_gate, v_w_ple_proj, v_final_norm)
    assert x.shape[0] == 1 and p.shape[:2] == (1, 1), "one sequence and one layer per device"

    def strip(a):
        return a[0] if a.ndim == 3 or (a.ndim == 2 and a.shape[0] == 1) else a

    w = {k: strip(a) for k, a in zip(WEIGHTS, ws)}
    m = {k: strip(a) for k, a in zip(WEIGHTS, ms)}
    v = {k: strip(a) for k, a in zip(WEIGHTS, vs)}
    loss, dx, grad, delta, new_m, new_v = _step(x[0], p[0, 0], loss_target[0], w, m, v)
    shapes = [a.shape for a in ws]
    outs = [loss, dx[None]]
    for res in (grad, delta, new_m, new_v):
        outs += [res[k].reshape(s) for k, s in zip(WEIGHTS, shapes)]
    return tuple(outs)
```

```python
import functools
import math

import jax
import jax.numpy as jnp
from jax import lax
from jax.experimental import pallas as pl
from jax.experimental.pallas import tpu as pltpu

F32 = jnp.float32
BF16 = jnp.bfloat16
MESH = pl.DeviceIdType.MESH
ANY = pl.BlockSpec(memory_space=pl.ANY)

HEAD_DIM = 128
NORM_EPS = 1e-6
N_CHIPS = 4
N_DEV = 8
BF16_ROWS = 16
VMEM_LIMIT = 56 * 1024 * 1024
ACC_BYTES = 8 * 1024 * 1024
STICK_EXIT = 110.0

ADAM_LR = 0.001
ADAM_B1 = 0.9
ADAM_B2 = 0.999
ADAM_EPS = 1e-08
ADAM_WD = 0.01
ADAM_STEP = 10

NN = (((1,), (0,)), ((), ()))
NT = (((1,), (1,)), ((), ()))
TN = (((0,), (0,)), ((), ()))


def _params(sem=None, **kw):
    if sem is not None:
        kw["dimension_semantics"] = sem
    return pltpu.CompilerParams(vmem_limit_bytes=VMEM_LIMIT, **kw)


def _pcall(body, **kw):
    return pl.pallas_call(body, **kw)


def _tile(n, pref, mult=8):
    best = None
    for d in range(mult, min(n, pref) + 1, mult):
        if n % d == 0:
            best = d
    return best if best is not None else n


def _dot(a, b, dims):
    return lax.dot_general(a, b, dims, preferred_element_type=F32)


def _call(name, body, grid, in_specs, out_specs, out_shape, args, scratch=(), sem=None, comm=None):
    n_in, n_out, n_sc = len(in_specs), len(out_specs), len(scratch)
    if comm is None:
        def plain(*refs):
            body(refs[:n_in], refs[n_in:n_in + n_out], refs[n_in + n_out:])

        return _pcall(plain, name=name, out_shape=list(out_shape), grid=grid, in_specs=list(in_specs),
                      out_specs=list(out_specs), scratch_shapes=list(scratch), compiler_params=_params(sem))(*args)
    n_cin, n_cout = len(comm.ins), len(comm.outs)
    steps = math.prod(grid)

    def hosted(*refs):
        ins, c_ins = refs[:n_in], refs[n_in:n_in + n_cin]
        outs = refs[n_in + n_cin:n_in + n_cin + n_out]
        c_outs = refs[n_in + n_cin + n_out:n_in + n_cin + n_out + n_cout]
        rest = refs[n_in + n_cin + n_out + n_cout:]
        sems = rest[n_sc:]
        step = pl.program_id(0)
        for ax in range(1, len(grid)):
            step = step * grid[ax] + pl.program_id(ax)

        @pl.when(step == 0)
        def _():
            comm.first(c_ins, c_outs, sems)

        body(ins, outs, rest[:n_sc])

        @pl.when(step == (3 * steps) // 4)
        def _():
            comm.mid(c_ins, c_outs, sems)

        @pl.when(step == steps - 1)
        def _():
            comm.last(c_ins, c_outs, sems)

    res = _pcall(hosted, name=name, out_shape=list(out_shape) + comm.outs, grid=grid,
                 in_specs=list(in_specs) + [ANY] * n_cin, out_specs=list(out_specs) + [ANY] * n_cout,
                 input_output_aliases={n_in + k: n_out + v for k, v in comm.aliases.items()},
                 scratch_shapes=list(scratch) + comm.sems,
                 compiler_params=_params(("arbitrary",) * len(grid)))(*args, *comm.ins)
    return list(res[:n_out]), list(res[n_out:])


NORM_CHUNK = 256


def _norm_bwd_tile(read_dn, rows, first, h_ref, g_ref, dr_ref, dh_ref, dhb_ref, dg_ref, alpha):
    @pl.when(first)
    def _():
        dg_ref[...] = jnp.zeros_like(dg_ref)

    gv = g_ref[...]
    tot = jnp.zeros_like(gv)
    for c0 in range(0, rows, NORM_CHUNK):
        sl = slice(c0, min(rows, c0 + NORM_CHUNK))
        hv = h_ref[sl, :]
        rs = _rstd(hv)
        hn = hv * rs
        dnv = read_dn(sl)
        gy = dnv * gv
        dh = dr_ref[sl, :] + rs * (gy - hn * jnp.mean(gy * hn, axis=-1, keepdims=True))
        dh_ref[sl, :] = dh
        dhb_ref[sl, :] = (alpha * dh).astype(BF16)
        tot = tot + jnp.sum(dnv * hn, axis=0, keepdims=True)
    dg_ref[...] += tot


def _mm(name, a, b, out_sds, grid, a_spec, b_spec, o_spec, dims, acc_shape, res=None, alpha=1.0, comm=None,
        norm=None, gain=None):
    nk = grid[2]

    def body(ins, outs, scratch):
        a_ref, b_ref = ins[:2]
        r_ref = ins[2] if res is not None else None
        o_ref = outs[0]
        if gain is not None:
            n_ref = scratch[-1]

            @pl.when(jnp.logical_and(pl.program_id(1) == 0, pl.program_id(2) == 0))
            def _():
                hv = a_ref[...]
                n_ref[...] = (hv * _rstd(hv) * ins[-1][...]).astype(BF16)
                outs[-1][...] = n_ref[...]

            a_ref = n_ref

        def finish(read):
            if norm is not None:
                first = jnp.logical_and(pl.program_id(0) == 0, pl.program_id(1) == 0)
                _norm_bwd_tile(read, o_ref.shape[0], first, *ins[2:5], *outs, alpha)
                return
            r = read(slice(None))
            if alpha != 1.0:
                r = r * alpha
            if r_ref is not None:
                r = r_ref[...] + r
            if len(o_ref.shape) == 3:
                half = o_ref.shape[1]
                o_ref[0] = r[:half].astype(o_ref.dtype)
                o_ref[1] = r[half:].astype(o_ref.dtype)
            else:
                o_ref[...] = r.astype(o_ref.dtype)

        if nk == 1:
            part = _dot(a_ref[...].astype(BF16), b_ref[...].astype(BF16), dims)
            finish(lambda sl: part[sl])
        else:
            acc_ref = scratch[0]
            kk = pl.program_id(2)

            @pl.when(kk == 0)
            def _():
                acc_ref[...] = jnp.zeros_like(acc_ref)

            acc_ref[...] += _dot(a_ref[...].astype(BF16), b_ref[...].astype(BF16), dims)

            @pl.when(kk == nk - 1)
            def _():
                finish(lambda sl: acc_ref[sl, :])

    in_specs = [a_spec, b_spec]
    args = [a, b]
    out_specs, out_shape = [o_spec], [out_sds]
    sem = ("parallel", "parallel", "arbitrary")
    if res is not None:
        in_specs.append(o_spec)
        args.append(res)
    if norm is not None:
        width = out_sds.shape[1]
        whole = pl.BlockSpec((1, width), lambda i, j, r: (0, 0))
        in_specs += [o_spec, whole, o_spec]
        args += list(norm)
        out_specs = [o_spec, o_spec, whole]
        out_shape = [jax.ShapeDtypeStruct(out_sds.shape, F32), jax.ShapeDtypeStruct(out_sds.shape, BF16),
                     jax.ShapeDtypeStruct((1, width), F32)]
        sem = ("arbitrary", "arbitrary", "arbitrary")
    scratch = [] if nk == 1 else [pltpu.VMEM(acc_shape, F32)]
    if gain is not None:
        in_specs.append(pl.BlockSpec((1, a.shape[1]), lambda i, j, r: (0, 0)))
        args.append(gain)
        out_specs.append(a_spec)
        out_shape.append(jax.ShapeDtypeStruct(a.shape, BF16))
        scratch.append(pltpu.VMEM(a_spec.block_shape, BF16))
        sem = ("parallel", "arbitrary", "arbitrary")
    got = _call(name, body, grid, in_specs, out_specs, out_shape, args, scratch, sem, comm)
    if norm is not None or gain is not None:
        return got if comm is None else (got[0], got[1])
    return got[0] if comm is None else (got[0][0], got[1])


def ffn_in_act(name, n, w4, tm, comm=None, gain=None):
    t, d = n.shape
    cs = w4.shape[2]

    def body(ins, outs, scratch):
        wg_ref, wu_ref = ins[-2:]
        a_ref, s_ref = outs[:2]
        if gain is None:
            nv = ins[0][...]
        else:
            hv = ins[0][...]
            nv = (hv * _rstd(hv) * ins[1][...]).astype(BF16)

            @pl.when(pl.program_id(0) == 0)
            def _():
                outs[2][...] = nv
        gate = _dot(nv, wg_ref[...], NN)
        up = _dot(nv, wu_ref[...], NN)
        a_ref[0] = gate.astype(BF16)
        a_ref[1] = up.astype(BF16)
        s_ref[...] = (gate * jax.nn.sigmoid(gate) * up).astype(BF16)

    rows = pl.BlockSpec((tm, d), lambda j, i: (i, 0))
    in_specs = [rows] + ([] if gain is None else [pl.BlockSpec((1, d), lambda j, i: (0, 0))])
    in_specs += [pl.BlockSpec((None, d, cs), lambda j, i: (j, 0, 0)),
                 pl.BlockSpec((None, d, cs), lambda j, i: (2 + j, 0, 0))]
    out_specs = [pl.BlockSpec((2, tm, cs), lambda j, i: (0, i, j)), pl.BlockSpec((tm, cs), lambda j, i: (i, j))]
    out_shape = [jax.ShapeDtypeStruct((2, t, 2 * cs), BF16), jax.ShapeDtypeStruct((t, 2 * cs), BF16)]
    if gain is not None:
        out_specs.append(pl.BlockSpec((tm, d), lambda j, i: (jnp.where(j == 0, i, t // tm - 1), 0)))
        out_shape.append(jax.ShapeDtypeStruct((t, d), BF16))
    got = _call(name, body, (2, t // tm), in_specs, out_specs, out_shape,
                [n] + ([] if gain is None else [gain]) + [w4, w4], (), ("arbitrary", "arbitrary"), comm)
    return got if comm is None else (got[0], got[1])


def ffn_ds_dact(name, df, w_out, a3, tm):
    t, d = df.shape
    f = w_out.shape[0]
    cs = f // 2
    tiles = 2 * (t // tm)

    def body(ins, outs, scratch):
        df_ref, w_ref, a_ref = ins
        ds_buf = scratch[0]
        s = pl.program_id(0)

        @pl.when(s < tiles)
        def _():
            ds_buf[s % 2] = _dot(df_ref[...], w_ref[...], NT)

        @pl.when(s > 0)
        def _():
            for c0 in range(0, tm, NORM_CHUNK):
                sl = slice(c0, min(tm, c0 + NORM_CHUNK))
                ds = ds_buf[(s - 1) % 2, sl, :]
                gate = a_ref[0, sl, :].astype(F32)
                up = a_ref[1, sl, :].astype(F32)
                sg = jax.nn.sigmoid(gate)
                outs[0][0, sl, :] = (ds * up * sg * (1.0 + gate * (1.0 - sg))).astype(BF16)
                outs[0][1, sl, :] = (ds * gate * sg).astype(BF16)

    def ahead(s):
        return jnp.minimum(s, tiles - 1)

    def behind(s):
        return jnp.maximum(s - 1, 0)

    blk = pl.BlockSpec((2, tm, cs), lambda s: (0, behind(s) // 2, behind(s) % 2))
    return _call(name, body, (tiles + 1,),
                 [pl.BlockSpec((tm, d), lambda s: (ahead(s) // 2, 0)),
                  pl.BlockSpec((cs, d), lambda s: (ahead(s) % 2, 0)), blk],
                 [blk], [jax.ShapeDtypeStruct((2, t, f), BF16)], [df, w_out, a3],
                 [pltpu.VMEM((2, tm, cs), F32)], ("arbitrary",))[0]


def _part_ranges(parts, d):
    out, lo = [], 0
    for p in parts:
        out.append((lo, p.shape[1] // d))
        lo += p.shape[1] // d
    return out, lo


def mm_nt_parts(name, parts, w4, tm, norm, alpha, comm=None):
    m = parts[0].shape[0]
    d, cs = w4.shape[1], w4.shape[2]
    per = cs // d
    ranges, nblk = _part_ranges(parts, d)
    np_ = len(parts)
    nt = m // tm
    chunk = tm // nblk

    def body(ins, outs, scratch):
        w_ref, acc = ins[np_], scratch[0]
        i, r = pl.program_id(0), pl.program_id(1)

        @pl.when(jnp.logical_and(i < nt, r == 0))
        def _():
            acc[i % 2] = jnp.zeros(acc.shape[1:], F32)

        for (lo, n), a_ref in zip(ranges, ins[:np_]):
            @pl.when(jnp.logical_and(i < nt, jnp.logical_and(r >= lo, r < lo + n)))
            def _(a_ref=a_ref):
                acc[i % 2] += _dot(a_ref[...], w_ref[...], NT)

        @pl.when(i > 0)
        def _():
            rows = pl.ds(pl.multiple_of(r * chunk, chunk), chunk)
            first = jnp.logical_and(i == 1, r == 0)
            _norm_bwd_tile(lambda sl: acc[(i - 1) % 2, rows, :][sl], chunk, first, *ins[np_ + 1:], *outs, alpha)

    def ahead(i, r):
        return jnp.where(i < nt, r, nblk - 1)

    rows = pl.BlockSpec((chunk, d), lambda i, r: (jnp.where(i == 0, 0, (i - 1) * nblk + r), 0))
    whole = pl.BlockSpec((1, d), lambda i, r: (0, 0))
    specs = [pl.BlockSpec((tm, d), lambda i, r, lo=lo, n=n: (jnp.minimum(i, nt - 1), jnp.clip(ahead(i, r) - lo, 0, n - 1)))
             for lo, n in ranges]
    specs += [pl.BlockSpec((None, d, d), lambda i, r: (ahead(i, r) // per, 0, ahead(i, r) % per)), rows, whole, rows]
    got = _call(name, body, (nt + 1, nblk), specs, [rows, rows, whole],
                [jax.ShapeDtypeStruct((m, d), F32), jax.ShapeDtypeStruct((m, d), BF16),
                 jax.ShapeDtypeStruct((1, d), F32)],
                list(parts) + [w4] + list(norm), [pltpu.VMEM((2, tm, d), F32)], ("arbitrary", "arbitrary"), comm)
    return got if comm is None else (got[0], got[1])


def mm_tn_parts(name, xa, parts, tt, comm=None):
    t, k = xa.shape
    d = k
    pr = k // 2
    ranges, nblk = _part_ranges(parts, d)
    per = nblk // N_CHIPS

    def body(ins, outs, scratch):
        x_ref, acc = ins[0], scratch[0]
        jb, r = pl.program_id(0), pl.program_id(1)

        @pl.when(r == 0)
        def _():
            acc[...] = jnp.zeros_like(acc)

        for (lo, n), p_ref in zip(ranges, ins[1:]):
            @pl.when(jnp.logical_and(jb >= lo, jb < lo + n))
            def _(p_ref=p_ref):
                acc[...] += _dot(x_ref[...], p_ref[...], TN)

        @pl.when(r == t // tt - 1)
        def _():
            outs[0][0] = acc[:pr].astype(BF16)
            outs[0][1] = acc[pr:].astype(BF16)

    def part_spec(lo, n):
        return pl.BlockSpec((tt, d), lambda jb, r: (jnp.where(jnp.logical_and(jb >= lo, jb < lo + n), r, 0),
                                                    jnp.clip(jb - lo, 0, n - 1)))

    specs = [pl.BlockSpec((tt, k), lambda jb, r: (r, 0))] + [part_spec(lo, n) for lo, n in ranges]
    got = _call(name, body, (nblk, t // tt), specs,
                [pl.BlockSpec((None, 2, pr, d), lambda jb, r: (jb // per, 0, 0, jb % per))],
                [jax.ShapeDtypeStruct((N_CHIPS, 2, pr, per * d), BF16)], [xa] + list(parts),
                [pltpu.VMEM((k, d), F32)], ("parallel", "arbitrary"), comm)
    return got[0] if comm is None else (got[0][0], got[1])


def mm_nn(name, a, w, out_dtype, tm, res=None, alpha=1.0):
    m, k = a.shape
    n = w.shape[1]
    return _mm(name, a, w, jax.ShapeDtypeStruct((m, n), out_dtype), (m // tm, 1, 1),
               pl.BlockSpec((tm, k), lambda i, j, r: (i, 0)),
               pl.BlockSpec((k, n), lambda i, j, r: (0, 0)),
               pl.BlockSpec((tm, n), lambda i, j, r: (i, 0)), NN, None, res=res, alpha=alpha)


def mm_nn_stacked(name, a, w4, out_dtype, tm, tn, j0=0, nj=None, comm=None, gain=None):
    m, k = a.shape
    cs = w4.shape[2]
    per = cs // tn
    nj = N_CHIPS * per - j0 if nj is None else nj
    return _mm(name, a, w4, jax.ShapeDtypeStruct((m, nj * tn), out_dtype), (m // tm, nj, 1),
               pl.BlockSpec((tm, k), lambda i, j, r: (i, 0)),
               pl.BlockSpec((None, k, tn), lambda i, j, r: ((j + j0) // per, 0, (j + j0) % per)),
               pl.BlockSpec((tm, tn), lambda i, j, r: (i, j)), NN, None, comm=comm, gain=gain)


def mm_nt(name, dy, w, out_dtype, tm, tko, norm=None, alpha=1.0):
    m, n = dy.shape
    k = w.shape[0]
    return _mm(name, dy, w, jax.ShapeDtypeStruct((m, k), out_dtype), (m // tm, k // tko, 1),
               pl.BlockSpec((tm, n), lambda i, j, r: (i, 0)),
               pl.BlockSpec((tko, n), lambda i, j, r: (j, 0)),
               pl.BlockSpec((tm, tko), lambda i, j, r: (i, j)), NT, None, norm=norm, alpha=alpha)


def mm_nt_stacked(name, dy, w4, tm, norm, alpha=1.0, comm=None):
    m = dy.shape[1]
    k, cs = w4.shape[1], w4.shape[2]
    nt = m // tm
    chunk = tm // N_CHIPS

    def body(ins, outs, scratch):
        dy_ref, w_ref, h_ref, g_ref, dr_ref = ins
        dh_ref, dhb_ref, dg_ref = outs
        acc = scratch[0]
        i, r = pl.program_id(0), pl.program_id(1)

        @pl.when(jnp.logical_and(i < nt, r == 0))
        def _():
            acc[i % 2] = jnp.zeros(acc.shape[1:], F32)

        @pl.when(i < nt)
        def _():
            acc[i % 2] += _dot(dy_ref[...], w_ref[...], NT)

        @pl.when(i > 0)
        def _():
            rows = pl.ds(pl.multiple_of(r * chunk, chunk), chunk)
            first = jnp.logical_and(i == 1, r == 0)
            _norm_bwd_tile(lambda sl: acc[(i - 1) % 2, rows, :][sl], chunk, first, h_ref, g_ref, dr_ref,
                           dh_ref, dhb_ref, dg_ref, alpha)

    def behind(i, r):
        return (jnp.where(i == 0, 0, (i - 1) * N_CHIPS + r), 0)

    def ahead(i, r):
        return jnp.where(i < nt, r, N_CHIPS - 1)

    rows = pl.BlockSpec((chunk, k), behind)
    whole = pl.BlockSpec((1, k), lambda i, r: (0, 0))
    got = _call(name, body, (nt + 1, N_CHIPS),
                [pl.BlockSpec((None, tm, cs), lambda i, r: (ahead(i, r) // 2, jnp.minimum(i, nt - 1), ahead(i, r) % 2)),
                 pl.BlockSpec((None, k, cs), lambda i, r: (ahead(i, r), 0, 0)), rows, whole, rows],
                [rows, rows, whole],
                [jax.ShapeDtypeStruct((m, k), F32), jax.ShapeDtypeStruct((m, k), BF16),
                 jax.ShapeDtypeStruct((1, k), F32)],
                [dy, w4] + list(norm), [pltpu.VMEM((2, tm, k), F32)], ("arbitrary", "arbitrary"), comm)
    return got if comm is None else (got[0], got[1])


def mm_tn_rows(name, xa, dy, tt):
    t, k = xa.shape
    n = dy.shape[1]
    tkr = k if k * n * 4 <= ACC_BYTES else k // 2
    return _mm(name, xa, dy, jax.ShapeDtypeStruct((k, n), BF16), (k // tkr, 1, t // tt),
               pl.BlockSpec((tt, tkr), lambda i, j, r: (r, i)),
               pl.BlockSpec((tt, n), lambda i, j, r: (r, 0)),
               pl.BlockSpec((tkr, n), lambda i, j, r: (i, 0)), TN, (tkr, n))


def mm_tn_whole(name, xa, dy, tt):
    t, k = xa.shape
    n = dy.shape[1]
    return _mm(name, xa, dy, jax.ShapeDtypeStruct((k, n), BF16), (1, 1, t // tt),
               pl.BlockSpec((tt, k), lambda i, j, r: (r, 0)),
               pl.BlockSpec((tt, n), lambda i, j, r: (r, 0)),
               pl.BlockSpec((k, n), lambda i, j, r: (0, 0)), TN, (k, n))


def mm_tn_cols(name, xa, dy, tt, comm=None):
    t, k = xa.shape
    pr = k // 2
    if dy.ndim == 3:
        cs = dy.shape[2] // 2
        dy_spec = pl.BlockSpec((None, tt, cs), lambda i, j, r: (j // 2, r, j % 2))
    else:
        cs = dy.shape[1] // N_CHIPS
        dy_spec = pl.BlockSpec((tt, cs), lambda i, j, r: (r, j))
    return _mm(name, xa, dy, jax.ShapeDtypeStruct((N_CHIPS, 2, pr, cs), BF16), (1, N_CHIPS, t // tt),
               pl.BlockSpec((tt, k), lambda i, j, r: (r, 0)), dy_spec,
               pl.BlockSpec((None, 2, pr, cs), lambda i, j, r: (j, 0, 0, 0)), TN, (k, cs), comm=comm)


def _rows(tt, w, col=0):
    return pl.BlockSpec((tt, w), lambda i: (i, col))


def _whole(shape):
    return pl.BlockSpec(shape, lambda i: (0,) * len(shape))


def _rstd(h):
    return lax.rsqrt(jnp.mean(h * h, axis=-1, keepdims=True) + NORM_EPS)


def rms_fwd(name, h, g, tt, comm=None):
    t, d = h.shape

    def body(ins, outs, scratch):
        hv = ins[0][...]
        outs[0][...] = (hv * _rstd(hv) * ins[1][...]).astype(BF16)

    got = _call(name, body, (t // tt,), [_rows(tt, d), _whole((1, d))], [_rows(tt, d)],
                [jax.ShapeDtypeStruct((t, d), BF16)], [h, g], (), ("parallel",), comm)
    return got[0] if comm is None else (got[0][0], got[1])


def mix_out_fwd(name, gates, yc, ya, h, w, tt):
    t, d = yc.shape

    def body(g_ref, yc_ref, ya_ref, h_ref, w_ref, m_ref, o_ref):
        merged = (jax.nn.sigmoid(g_ref[:, :d].astype(F32)) * yc_ref[...].astype(F32)
                  + jax.nn.sigmoid(g_ref[:, d:].astype(F32)) * ya_ref[...].astype(F32)).astype(BF16)
        m_ref[...] = merged
        o_ref[...] = h_ref[...] + _dot(merged, w_ref[...], NN)

    return _pcall(body, name=name,
                  out_shape=(jax.ShapeDtypeStruct((t, d), BF16), jax.ShapeDtypeStruct((t, d), F32)),
                  grid=(t // tt,),
                  in_specs=[_rows(tt, 2 * d), _rows(tt, d), _rows(tt, d), _rows(tt, d), _whole((d, d))],
                  out_specs=(_rows(tt, d), _rows(tt, d)),
                  compiler_params=_params(("parallel",)))(gates, yc, ya, h, w)


def mix_out_bwd(name, dh, w, gates, yc, ya, tt):
    t, d = yc.shape

    def body(dh_ref, w_ref, g_ref, yc_ref, ya_ref, dyc_ref, dya_ref, dg_ref):
        dmv = _dot(dh_ref[...], w_ref[...], NT)
        sc = jax.nn.sigmoid(g_ref[:, :d].astype(F32))
        sa = jax.nn.sigmoid(g_ref[:, d:].astype(F32))
        dyc_ref[...] = (dmv * sc).astype(BF16)
        dya_ref[...] = (dmv * sa).astype(BF16)
        dg_ref[:, :d] = (dmv * yc_ref[...].astype(F32) * sc * (1.0 - sc)).astype(BF16)
        dg_ref[:, d:] = (dmv * ya_ref[...].astype(F32) * sa * (1.0 - sa)).astype(BF16)

    return _pcall(body, name=name,
                  out_shape=(jax.ShapeDtypeStruct((t, d), BF16), jax.ShapeDtypeStruct((t, d), BF16),
                             jax.ShapeDtypeStruct((t, 2 * d), BF16)),
                  grid=(t // tt,),
                  in_specs=[_rows(tt, d), _whole((d, d)), _rows(tt, 2 * d), _rows(tt, d), _rows(tt, d)],
                  out_specs=(_rows(tt, d), _rows(tt, d), _rows(tt, 2 * d)),
                  compiler_params=_params(("parallel",)))(dh, w, gates, yc, ya)


def _shift_down(cur, prev8, s):
    tt = cur.shape[0]
    rolled = pltpu.roll(cur, s, 0)
    row8 = lax.broadcasted_iota(jnp.int32, prev8.shape, 0)
    first8 = jnp.where(row8 < s, pltpu.roll(prev8, s, 0), rolled[:8])
    return jnp.concatenate([first8, rolled[8:]], axis=0) if tt > 8 else first8


def _shift_up(cur, next8, s):
    tt = cur.shape[0]
    rolled = pltpu.roll(cur, tt - s, 0)
    row8 = lax.broadcasted_iota(jnp.int32, next8.shape, 0)
    last8 = jnp.where(row8 >= 8 - s, pltpu.roll(next8, 8 - s, 0), rolled[tt - 8:])
    return jnp.concatenate([rolled[:tt - 8], last8], axis=0) if tt > 8 else last8


def _prev_rows(tt, d, col):
    return pl.BlockSpec((BF16_ROWS, d), lambda i: (jnp.maximum(i * (tt // BF16_ROWS) - 1, 0), col))


def _next_rows(tt, d, col, t):
    return pl.BlockSpec((BF16_ROWS, d),
                        lambda i: (jnp.minimum((i + 1) * (tt // BF16_ROWS), t // BF16_ROWS - 1), col))


def conv_out_fwd(name, cbx, cw8, w_out, tt):
    t, d3 = cbx.shape
    d = d3 // 3

    def body(cb_ref, cc_ref, cx_ref, pc_ref, px_ref, w_ref, wo_ref, o_ref, y_ref):
        has_prev = (pl.program_id(0) > 0).astype(F32)
        cc = cc_ref[...].astype(F32) * cx_ref[...].astype(F32)
        prev = pc_ref[...].astype(F32)[8:] * px_ref[...].astype(F32)[8:] * has_prev
        w = w_ref[...]
        conv = w[0:1] * _shift_down(cc, prev, 2) + w[1:2] * _shift_down(cc, prev, 1) + w[2:3] * cc
        ycin = (cb_ref[...].astype(F32) * conv).astype(BF16)
        o_ref[...] = ycin
        y_ref[...] = _dot(ycin, wo_ref[...], NN).astype(BF16)

    out = jax.ShapeDtypeStruct((t, d), BF16)
    return _pcall(body, name=name, out_shape=(out, out), grid=(t // tt,),
                  in_specs=[_rows(tt, d, 0), _rows(tt, d, 1), _rows(tt, d, 2), _prev_rows(tt, d, 1),
                            _prev_rows(tt, d, 2), _whole((8, d)), _whole((d, d))],
                  out_specs=(_rows(tt, d), _rows(tt, d)),
                  compiler_params=_params(("parallel",)))(cbx, cbx, cbx, cbx, cbx, cw8, w_out)


def conv_out_bwd(name, dyc, w_out, cbx, cw8, tt):
    t, d3 = cbx.shape
    d = d3 // 3
    n = t // tt

    def body(dy_ref, ndy_ref, wo_ref, cb_ref, cc_ref, cx_ref, pc_ref, px_ref, ncb_ref, w_ref, o_ref, dw_ref):
        i = pl.program_id(0)
        has_prev = (i > 0).astype(F32)
        has_next = (i < n - 1).astype(F32)
        cb = cb_ref[...].astype(F32)
        ccv = cc_ref[...].astype(F32)
        cxv = cx_ref[...].astype(F32)
        cc = ccv * cxv
        prev = pc_ref[...].astype(F32)[8:] * px_ref[...].astype(F32)[8:] * has_prev
        w = w_ref[...]
        cc1 = _shift_down(cc, prev, 1)
        cc2 = _shift_down(cc, prev, 2)
        conv = w[0:1] * cc2 + w[1:2] * cc1 + w[2:3] * cc
        dyv = _dot(dy_ref[...], wo_ref[...], NT)
        dconv = dyv * cb
        dnext = _dot(ndy_ref[...], wo_ref[...], NT)[:8] * ncb_ref[...].astype(F32)[:8] * has_next
        dcc = w[2:3] * dconv + w[1:2] * _shift_up(dconv, dnext, 1) + w[0:1] * _shift_up(dconv, dnext, 2)
        o_ref[:, :d] = (dyv * conv).astype(BF16)
        o_ref[:, d:2 * d] = (dcc * cxv).astype(BF16)
        o_ref[:, 2 * d:] = (dcc * ccv).astype(BF16)

        @pl.when(i == 0)
        def _():
            dw_ref[...] = jnp.zeros_like(dw_ref)

        dw_ref[0:1, :] += jnp.sum(dconv * cc2, axis=0, keepdims=True)
        dw_ref[1:2, :] += jnp.sum(dconv * cc1, axis=0, keepdims=True)
        dw_ref[2:3, :] += jnp.sum(dconv * cc, axis=0, keepdims=True)

    return _pcall(body, name=name,
                  out_shape=(jax.ShapeDtypeStruct((t, d3), BF16), jax.ShapeDtypeStruct((8, d), F32)),
                  grid=(n,),
                  in_specs=[_rows(tt, d), _next_rows(tt, d, 0, t),
                            _whole((d, d)), _rows(tt, d, 0), _rows(tt, d, 1), _rows(tt, d, 2),
                            _prev_rows(tt, d, 1), _prev_rows(tt, d, 2), _next_rows(tt, d, 0, t), _whole((8, d))],
                  out_specs=(_rows(tt, d3), _whole((8, d))),
                  compiler_params=_params(("arbitrary",)))(dyc, dyc, w_out, cbx, cbx, cbx, cbx, cbx, cbx, cw8)


def tail(name, h3, p, tgt, gp, gf, w_gate, w_proj, tt):
    t, d = h3.shape
    pd = p.shape[1]

    def body(h_ref, p_ref, tg_ref, gp_ref, gf_ref, wg_ref, wp_ref, np_ref, dh_ref, dpp_ref, dzg_ref, dgf_ref,
             loss_ref):
        hv = h_ref[...]
        npl = (hv * _rstd(hv) * gp_ref[...]).astype(BF16)
        np_ref[...] = npl
        pg = jax.nn.sigmoid(_dot(npl, wg_ref[...], NN))
        ppv = _dot(p_ref[...].astype(BF16), wp_ref[...], NN)
        h4 = hv + pg * ppv
        r4 = _rstd(h4)
        hn = h4 * r4
        gfv = gf_ref[...]
        err = hn * gfv - tg_ref[...]
        dy = err * (1.0 / d)
        gy = dy * gfv
        dh4 = r4 * (gy - hn * jnp.mean(gy * hn, axis=-1, keepdims=True))
        dh_ref[...] = dh4
        dpp_ref[...] = (dh4 * pg).astype(BF16)
        dzg_ref[...] = (dh4 * ppv * pg * (1.0 - pg)).astype(BF16)

        @pl.when(pl.program_id(0) == 0)
        def _():
            dgf_ref[...] = jnp.zeros_like(dgf_ref)
            loss_ref[...] = jnp.zeros_like(loss_ref)

        dgf_ref[...] += jnp.sum(dy * hn, axis=0, keepdims=True)
        tok = jnp.mean(err * err, axis=-1, keepdims=True)
        loss_ref[...] += 0.5 * jnp.sum(tok, axis=0, keepdims=True) * jnp.ones((1, loss_ref.shape[1]), F32)

    return _pcall(body, name=name,
                  out_shape=(jax.ShapeDtypeStruct((t, d), BF16), jax.ShapeDtypeStruct((t, d), F32),
                             jax.ShapeDtypeStruct((t, d), BF16), jax.ShapeDtypeStruct((t, d), BF16),
                             jax.ShapeDtypeStruct((1, d), F32), jax.ShapeDtypeStruct((1, d), F32)),
                  grid=(t // tt,),
                  in_specs=[_rows(tt, d), _rows(tt, pd), _rows(tt, d), _whole((1, d)), _whole((1, d)),
                            _whole((d, d)), _whole((pd, d))],
                  out_specs=(_rows(tt, d), _rows(tt, d), _rows(tt, d), _rows(tt, d), _whole((1, d)),
                             _whole((1, d))),
                  compiler_params=_params(("arbitrary",)))(h3, p, tgt, gp, gf, w_gate, w_proj)


SCALE = 1.0 / math.sqrt(HEAD_DIM)


def _log_stick(z):
    return -(jnp.maximum(z, 0.0) + jnp.log(1.0 + jnp.exp(-jnp.abs(z))))


def _tri_sum(x, tri):
    hi = x.astype(BF16)
    lo = (x - hi.astype(F32)).astype(BF16)
    return _dot(hi, tri, NN) + _dot(lo, tri, NN)


KEY_BLOCK = 128
NEAR = 3
THIN_ROWS = 32


def _pad_block(x):
    n = x.shape[0]
    return x if n == KEY_BLOCK else jnp.concatenate([x, jnp.zeros((KEY_BLOCK - n, x.shape[1]), x.dtype)], axis=0)


def _sb_near(qs, jds, k_ref, below, upper, last_rows):
    near_rows = (KEY_BLOCK,) * (NEAR - 1) + (last_rows,)
    pairs = [(s, b) for s in range(len(qs)) for b in range(NEAR)]
    rows = {(s, b): _block_rows(jnp.maximum(jds[s] - b, 0), KEY_BLOCK) for s, b in pairs}
    z = {(s, b): _dot(qs[s][:near_rows[b]], k_ref[rows[s, b], :], NT) * SCALE for s, b in pairs}
    lg = {(s, b): jnp.where(below, _log_stick(z[s, b]), 0.0) if b == 0 else _log_stick(z[s, b]) for s, b in pairs}
    cum = {(s, b): _tri_sum(lg[s, b], upper) for s, b in pairs}
    out, carries = [], []
    for s in range(len(qs)):
        c = cum[s, 0][:, 0:1]
        blocks = [(rows[s, 0], z[s, 0], jnp.exp(jnp.where(below, z[s, 0] + cum[s, 0], -1e30)))]
        for b in range(1, NEAR):
            live = jds[s] >= b
            off = c[:near_rows[b]] + jnp.where(live, 0.0, -1e30)
            blocks.append((rows[s, b], z[s, b], jnp.exp(z[s, b] + cum[s, b] + off)))
            c = c + _pad_block(jnp.where(live, cum[s, b][:, 0:1], 0.0))
        out.append(blocks)
        carries.append(c)
    return out, carries


def _sb_far(q, kj, upper, c, skip):
    z = _dot(q, kj, NT) * SCALE
    cum = _tri_sum(_log_stick(z), upper)
    return z, jnp.exp(z + cum + (c + jnp.where(skip, -1e30, 0.0))), c + jnp.where(skip, 0.0, cum[:, 0:1])


def _took_it(j, jd, last_rows):
    first = lax.broadcasted_iota(jnp.int32, (KEY_BLOCK, 1), 0) < last_rows
    return jnp.logical_and(j == jd - (NEAR - 1), first)


def _block_rows(j, size):
    return pl.ds(pl.multiple_of(j * size, size), size)


def _sweep_on(st):
    return jnp.logical_and(st[0] >= 0, jnp.max(st[1]) > -STICK_EXIT)


def attn_fwd(name, qkv, tq):
    t, d3 = qkv.shape
    d = d3 // 3
    nh = d // HEAD_DIM
    nq = t // tq
    tb = KEY_BLOCK
    nsub = tq // tb

    def body(q_ref, k_ref, v_ref, o_ref):
        i = pl.program_id(1)
        row = lax.broadcasted_iota(jnp.int32, (tb, tb), 0)
        col = lax.broadcasted_iota(jnp.int32, (tb, tb), 1)
        upper = (row >= col).astype(BF16)
        qs = [q_ref[s * tb:(s + 1) * tb, :] for s in range(nsub)]
        jds = [i * nsub + s for s in range(nsub)]
        near, carries = _sb_near(qs, jds, k_ref, col < row, upper, THIN_ROWS)
        state = []
        for s in range(nsub):
            acc = jnp.zeros((tb, HEAD_DIM), F32)
            for rows, _, a in near[s]:
                acc = acc + _pad_block(_dot(a.astype(BF16), v_ref[rows, :], NN))
            state.append((qs[s], jds[s], carries[s], acc))
        for s, (q, jd, c, acc) in enumerate(state):

            def step(st, q=q, jd=jd):
                rows = _block_rows(st[0], tb)
                _, a, c2 = _sb_far(q, k_ref[rows, :], upper, st[1], _took_it(st[0], jd, THIN_ROWS))
                return st[0] - 1, c2, st[2] + _dot(a.astype(BF16), v_ref[rows, :], NN)

            _, _, acc = lax.while_loop(_sweep_on, step, (jd - (NEAR - 1), c, acc))
            o_ref[s * tb:(s + 1) * tb, :] = acc.astype(o_ref.dtype)

    return _pcall(body, name=name, out_shape=jax.ShapeDtypeStruct((t, d), BF16), grid=(nh, nq),
                  in_specs=[pl.BlockSpec((tq, HEAD_DIM), lambda h, i: (i, h)),
                            pl.BlockSpec((t, HEAD_DIM), lambda h, i: (0, nh + h)),
                            pl.BlockSpec((t, HEAD_DIM), lambda h, i: (0, 2 * nh + h))],
                  out_specs=pl.BlockSpec((tq, HEAD_DIM), lambda h, i: (i, h)),
                  compiler_params=_params(("parallel", "arbitrary")))(qkv, qkv, qkv)


def attn_bwd(name, qkv, do, tq):
    t, d3 = qkv.shape
    d = d3 // 3
    nh = d // HEAD_DIM
    nq = t // tq
    tb = KEY_BLOCK
    nsub = tq // tb

    def body(q_ref, k_ref, v_ref, do_ref, dq_ref, dk_ref, dv_ref, dk_acc, dv_acc, g_buf, z_buf):
        i = pl.program_id(1)

        @pl.when(i == 0)
        def _():
            dk_acc[...] = jnp.zeros_like(dk_acc)
            dv_acc[...] = jnp.zeros_like(dv_acc)

        row = lax.broadcasted_iota(jnp.int32, (tb, tb), 0)
        col = lax.broadcasted_iota(jnp.int32, (tb, tb), 1)
        below = col < row
        upper = (row >= col).astype(BF16)
        lower = (row <= col).astype(BF16)

        qs = [q_ref[s * tb:(s + 1) * tb, :] for s in range(nsub)]
        dos = [do_ref[s * tb:(s + 1) * tb, :] for s in range(nsub)]
        jds = [i * nsub + s for s in range(nsub)]
        near, carries = _sb_near(qs, jds, k_ref, below, upper, KEY_BLOCK)
        da = [[_dot(dos[s][:a.shape[0]], v_ref[rows, :], NT) for rows, _, a in near[s]] for s in range(nsub)]
        state = []
        for s in range(nsub):
            kept = [(rows, z, da[s][b] * a) for b, (rows, z, a) in enumerate(near[s])]
            for rows, _, a in near[s]:
                dv_acc[rows, :] += _dot(a.astype(BF16), dos[s][:a.shape[0]], TN)
            state.append((qs[s], dos[s], jds[s], carries[s], kept))

        carried = []
        for s, (q, dov, jd, c, kept) in enumerate(state):
            def step(st, s=s, q=q, dov=dov, jd=jd):
                j = st[0]
                rows = _block_rows(j, tb)
                z, a, c2 = _sb_far(q, k_ref[rows, :], upper, st[1], _took_it(j, jd, KEY_BLOCK))
                g_buf[jd - j] = _dot(dov, v_ref[rows, :], NT) * a
                z_buf[jd - j] = z
                dv_acc[rows, :] += _dot(a.astype(BF16), dov, TN)
                return j - 1, c2

            j_stop, _ = lax.while_loop(_sweep_on, step, (jd - (NEAR - 1), c))

            def far(j, st, s=s, q=q, jd=jd):
                run, dq = st
                rows = _block_rows(j, tb)
                g = g_buf[jd - j]
                dz = (g - jax.nn.sigmoid(z_buf[jd - j]) * (run + _tri_sum(g, lower))).astype(BF16)
                dk_acc[rows, :] += _dot(dz, q, TN)
                return run + jnp.sum(g, axis=1, keepdims=True), dq + _dot(dz, k_ref[rows, :], NN)

            carried.append(lax.fori_loop(j_stop + 1, jd - (NEAR - 1) + 1, far,
                                         (jnp.zeros((tb, 1), F32), jnp.zeros((tb, HEAD_DIM), F32))))

        tri = [[_dot(g.astype(BF16), lower, NN) for _, _, g in st[4]] for st in state]
        sig = [[jax.nn.sigmoid(z) for _, z, _ in st[4]] for st in state]
        for s, (q, dov, jd, c, kept) in enumerate(state):
            run, dq = carried[s]
            for b in reversed(range(NEAR)):
                rows, z, g = kept[b]
                n = g.shape[0]
                dz = g - sig[s][b] * (run[:n] + tri[s][b])
                if b == 0:
                    dz = jnp.where(below, dz, 0.0)
                dz = dz.astype(BF16)
                dk_acc[rows, :] += _dot(dz, q[:n], TN)
                dq = dq + _pad_block(_dot(dz, k_ref[rows, :], NN))
                if b:
                    run = run + _pad_block(jnp.sum(g, axis=1, keepdims=True))
            dq_ref[s * tb:(s + 1) * tb, :] = (dq * SCALE).astype(BF16)

        @pl.when(i == nq - 1)
        def _():
            dk_ref[...] = (dk_acc[...] * SCALE).astype(BF16)
            dv_ref[...] = dv_acc[...].astype(BF16)

    blk = pl.BlockSpec((tq, HEAD_DIM), lambda h, i: (i, h))
    col_h = pl.BlockSpec((t, HEAD_DIM), lambda h, i: (0, h))
    out = jax.ShapeDtypeStruct((t, d), BF16)
    return _pcall(body, name=name, out_shape=(out, out, out), grid=(nh, nq),
                  in_specs=[blk,
                            pl.BlockSpec((t, HEAD_DIM), lambda h, i: (0, nh + h)),
                            pl.BlockSpec((t, HEAD_DIM), lambda h, i: (0, 2 * nh + h)),
                            blk],
                  out_specs=(blk, col_h, col_h),
                  scratch_shapes=[pltpu.VMEM((t, HEAD_DIM), F32), pltpu.VMEM((t, HEAD_DIM), F32),
                                  pltpu.VMEM((t // tb, tb, tb), F32), pltpu.VMEM((t // tb, tb, tb), F32)],
                  compiler_params=_params(("parallel", "arbitrary")))(qkv, qkv, qkv, do)


def _place():
    x, y, c = lax.axis_index("x"), lax.axis_index("y"), lax.axis_index("c")
    chips = [(1 - x, y), (x, 1 - y), (1 - x, 1 - y)]
    return x, y, c, chips


def _remote(src, dst, send_sem, recv_sem, dev):
    return pltpu.make_async_remote_copy(src_ref=src, dst_ref=dst, send_sem=send_sem, recv_sem=recv_sem,
                                        device_id=dev, device_id_type=MESH)


def place_shards(name, ws, chip):
    tiles, steps = _job_tiles([w.shape for w in ws], 1 << 20, BF16_ROWS)
    nj = len(ws)

    def body(chip_ref, *refs):
        i = pl.program_id(0)
        for k, (_, n) in enumerate(tiles):
            @pl.when(i < n)
            def _(w_ref=refs[k], o_ref=refs[nj + k]):
                o_ref[...] = w_ref[...].astype(BF16)

    spec = pltpu.PrefetchScalarGridSpec(
        num_scalar_prefetch=1, grid=(steps,),
        in_specs=[pl.BlockSpec((tr, w.shape[1]), lambda i, s, n=n: (jnp.minimum(i, n - 1), 0))
                  for w, (tr, n) in zip(ws, tiles)],
        out_specs=[pl.BlockSpec((None, tr, w.shape[1]), lambda i, s, n=n: (s[0], jnp.minimum(i, n - 1), 0))
                   for w, (tr, n) in zip(ws, tiles)])
    return _pcall(body, name=name, out_shape=[jax.ShapeDtypeStruct((N_CHIPS,) + w.shape, BF16) for w in ws],
                  grid_spec=spec, compiler_params=_params(("arbitrary",)))(chip, *ws)


class Comm:
    def __init__(self, ins, outs, aliases, sems, first, mid, last):
        self.ins, self.outs, self.aliases, self.sems = list(ins), list(outs), dict(aliases), list(sems)
        self.first, self.mid, self.last = first, mid, last


def run_comm(name, comm):
    ni, no = len(comm.ins), len(comm.outs)

    def body(*refs):
        ins, outs, sems = refs[:ni], refs[ni:ni + no], refs[ni + no:]
        comm.first(ins, outs, sems)
        comm.mid(ins, outs, sems)
        comm.last(ins, outs, sems)

    return _pcall(body, name=name, out_shape=comm.outs, in_specs=[ANY] * ni, out_specs=[ANY] * no,
                  input_output_aliases=comm.aliases, scratch_shapes=comm.sems, compiler_params=_params())(*comm.ins)


def gather_comm(bufs):
    n = len(bufs)

    def half(out, w, which):
        pr = out[w].shape[1] // 2
        return pl.ds(pl.multiple_of(which * pr, BF16_ROWS), pr)

    def first(ins, out, sems):
        isend, irecv, _, _ = sems
        x, y, c, chips = _place()
        for w in range(n):
            mine = out[w].at[2 * x + y, half(out, w, c)]
            for j, (cx, cy) in enumerate(chips):
                _remote(mine, mine, isend.at[3 * w + j], irecv.at[3 * w + j], (cx, cy, c)).start()

    def mid(ins, out, sems):
        isend, irecv, dsend, drecv = sems
        x, y, c, chips = _place()
        sib = (x, y, 1 - c)
        for w in range(n):
            for j, (cx, cy) in enumerate(chips):
                landed = out[w].at[2 * cx + cy, half(out, w, c)]
                _remote(landed, landed, isend.at[3 * w + j], irecv.at[3 * w + j], sib).wait_recv()
                _remote(landed, landed, dsend.at[3 * w + j], drecv.at[3 * w + j], sib).start()

    def last(ins, out, sems):
        isend, irecv, dsend, drecv = sems
        x, y, c, chips = _place()
        sib = (x, y, 1 - c)
        for w in range(n):
            for j, (cx, cy) in enumerate(chips):
                landed = out[w].at[2 * cx + cy, half(out, w, 1 - c)]
                _remote(landed, landed, dsend.at[3 * w + j], drecv.at[3 * w + j], sib).wait_recv()
        for w in range(n):
            sent = out[w].at[0, half(out, w, c)]
            for j in range(3):
                _remote(sent, sent, isend.at[3 * w + j], irecv.at[3 * w + j], sib).wait_send()
                _remote(sent, sent, dsend.at[3 * w + j], drecv.at[3 * w + j], sib).wait_send()

    return Comm(bufs, [jax.ShapeDtypeStruct(s.shape, s.dtype) for s in bufs], {w: w for w in range(n)},
                [pltpu.SemaphoreType.DMA((3 * n,))] * 4, first, mid, last)


def _nothing(ins, outs, sems):
    return None


def join_comms(a, b):
    ni, no, ns = len(a.ins), len(a.outs), len(a.sems)

    def both(f, g):
        def hook(ins, outs, sems):
            f(ins[:ni], outs[:no], sems[:ns])
            g(ins[ni:], outs[no:], sems[ns:])
        return hook

    aliases = dict(a.aliases)
    aliases.update({ni + k: no + v for k, v in b.aliases.items()})
    return Comm(a.ins + b.ins, a.outs + b.outs, aliases, a.sems + b.sems,
                both(a.first, b.first), both(a.mid, b.mid), both(a.last, b.last))


def exchange_comm(pieces):
    n = len(pieces)

    def copies(src, out, sems):
        x, y, c, _ = _place()
        return [_remote(src[w].at[k, 1 - c], out[w].at[k], sems[0].at[N_CHIPS * w + k], sems[1].at[N_CHIPS * w + k],
                        (x, y, 1 - c)) for w in range(n) for k in range(N_CHIPS)]

    def first(src, out, sems):
        for cp in copies(src, out, sems):
            cp.start()

    def last(src, out, sems):
        for cp in copies(src, out, sems):
            cp.wait()

    return Comm(pieces, [jax.ShapeDtypeStruct((N_CHIPS,) + s.shape[2:], s.dtype) for s in pieces], {},
                [pltpu.SemaphoreType.DMA((N_CHIPS * n,))] * 2, first, _nothing, last)


def scatter_comm(parts):
    n = len(parts)

    def copies(src, out, sems):
        x, y, c, chips = _place()
        return [_remote(src[w].at[2 * cx + cy], out[w].at[j], sems[0].at[3 * w + j], sems[1].at[3 * w + j], (cx, cy, c))
                for w in range(n) for j, (cx, cy) in enumerate(chips)]

    def first(src, out, sems):
        for cp in copies(src, out, sems):
            cp.start()

    def last(src, out, sems):
        for cp in copies(src, out, sems):
            cp.wait()

    return Comm(parts, [jax.ShapeDtypeStruct((3,) + s.shape[1:], s.dtype) for s in parts], {},
                [pltpu.SemaphoreType.DMA((3 * n,))] * 2, first, _nothing, last)


def share_comm(halves):
    n = len(halves)

    def first(ins, buf, sems):
        x, y, c, _ = _place()
        for w in range(n):
            _remote(buf[w].at[c], buf[w].at[c], sems[0].at[w], sems[1].at[w], (x, y, 1 - c)).start()

    def last(ins, buf, sems):
        x, y, c, _ = _place()
        for w in range(n):
            landed = buf[w].at[1 - c]
            _remote(landed, landed, sems[0].at[w], sems[1].at[w], (x, y, 1 - c)).wait_recv()
        for w in range(n):
            _remote(buf[w].at[c], buf[w].at[c], sems[0].at[w], sems[1].at[w], (x, y, 1 - c)).wait_send()

    return Comm(halves, [jax.ShapeDtypeStruct(s.shape, s.dtype) for s in halves], {w: w for w in range(n)},
                [pltpu.SemaphoreType.DMA((n,))] * 2, first, _nothing, last)


def gather_small(name, blk, reduce):
    r, cdim = blk.shape

    def body(in_ref, out_ref, *rest):
        if reduce:
            buf, send_sem, recv_sem = rest
        else:
            buf = out_ref
            send_sem, recv_sem = rest
        x, y, c, _ = _place()
        me = 4 * x + 2 * y + c
        buf[me] = in_ref[...]
        peers = []
        for dx in range(2):
            for dy in range(2):
                for dc in range(2):
                    if dx or dy or dc:
                        peers.append((dx, dy, dc))
        copies = []
        for s, (dx, dy, dc) in enumerate(peers):
            cp = _remote(in_ref, buf.at[me], send_sem.at[s], recv_sem.at[s],
                         ((1 - x if dx else x), (1 - y if dy else y), (1 - c if dc else c)))
            cp.start()
            copies.append(cp)
        for s, (dx, dy, dc) in enumerate(peers):
            px, py, pc_ = (1 - x if dx else x), (1 - y if dy else y), (1 - c if dc else c)
            landed = buf.at[4 * px + 2 * py + pc_]
            _remote(landed, landed, send_sem.at[s], recv_sem.at[s], (x, y, c)).wait_recv()
        for cp in copies:
            cp.wait_send()
        if reduce:
            tot = buf[0]
            for s in range(1, N_DEV):
                tot = tot + buf[s]
            out_ref[...] = tot

    vm = pl.BlockSpec(memory_space=pltpu.VMEM)
    out_shape = jax.ShapeDtypeStruct((r, cdim) if reduce else (N_DEV, r, cdim), F32)
    scratch = ([pltpu.VMEM((N_DEV, r, cdim), F32)] if reduce else []) + [pltpu.SemaphoreType.DMA((N_DEV - 1,))] * 2
    return _pcall(body, name=name, out_shape=out_shape, in_specs=[vm], out_specs=vm, scratch_shapes=scratch,
                  compiler_params=_params())(blk)


def _job_tiles(shapes, tile_bytes, mult):
    tiles = []
    for rows, cols in shapes:
        tr = _tile(rows, max(mult, tile_bytes // (4 * cols)), mult)
        tiles.append((tr, rows // tr))
    return tiles, max(n for _, n in tiles)


def sum_cores(name, owns, gots, place):
    nj = len(owns)
    tiles, _ = _job_tiles([o.shape[2:] for o in owns], 1 << 21, BF16_ROWS)
    steps = max(N_CHIPS * n for _, n in tiles)

    def body(place_ref, *refs):
        i = pl.program_id(0)
        for k, (_, n) in enumerate(tiles):
            @pl.when(i < N_CHIPS * n)
            def _(own_ref=refs[2 * k], got_ref=refs[2 * k + 1], o_ref=refs[2 * nj + k]):
                o_ref[...] = (own_ref[...].astype(F32) + got_ref[...].astype(F32)).astype(o_ref.dtype)

    in_specs, out_specs, out_shape, args = [], [], [], []
    for own, got, (tr, n) in zip(owns, gots, tiles):
        pc = own.shape[3]
        last = N_CHIPS * n - 1
        in_specs += [pl.BlockSpec((None, None, tr, pc),
                                  lambda i, s, n=n, last=last: (jnp.minimum(i, last) // n, s[1], jnp.minimum(i, last) % n, 0)),
                     pl.BlockSpec((None, tr, pc),
                                  lambda i, s, n=n, last=last: (jnp.minimum(i, last) // n, jnp.minimum(i, last) % n, 0))]
        out_specs.append(pl.BlockSpec((None, tr, pc),
                                      lambda i, s, n=n, last=last: (jnp.minimum(i, last) // n, jnp.minimum(i, last) % n, 0)))
        out_shape.append(jax.ShapeDtypeStruct(got.shape, BF16))
        args += [own, got]
    spec = pltpu.PrefetchScalarGridSpec(num_scalar_prefetch=1, grid=(steps,), in_specs=in_specs, out_specs=out_specs)
    return _pcall(body, name=name, out_shape=out_shape, grid_spec=spec,
                  compiler_params=_params(("arbitrary",)))(place, *args)


def sum_chips(name, parts, gots, place):
    nj = len(parts)
    tiles, steps = _job_tiles([p.shape[1:] for p in parts], 1 << 20, BF16_ROWS)

    def body(place_ref, *refs):
        i = pl.program_id(0)
        for k, (_, n) in enumerate(tiles):
            @pl.when(i < n)
            def _(part_ref=refs[2 * k], got_ref=refs[2 * k + 1], o_ref=refs[2 * nj + k]):
                tot = part_ref[...].astype(F32)
                for j in range(3):
                    tot = tot + got_ref[j].astype(F32)
                o_ref[...] = tot

    in_specs, out_specs, out_shape, args = [], [], [], []
    for part, got, (tr, n) in zip(parts, gots, tiles):
        pc = part.shape[2]
        in_specs += [pl.BlockSpec((None, tr, pc), lambda i, s, n=n: (s[0], jnp.minimum(i, n - 1), 0)),
                     pl.BlockSpec((3, tr, pc), lambda i, s, n=n: (0, jnp.minimum(i, n - 1), 0))]
        out_specs.append(pl.BlockSpec((None, tr, pc), lambda i, s, n=n: (s[1], jnp.minimum(i, n - 1), 0)))
        out_shape.append(jax.ShapeDtypeStruct((2,) + part.shape[1:], F32))
        args += [part, got]
    spec = pltpu.PrefetchScalarGridSpec(num_scalar_prefetch=1, grid=(steps,), in_specs=in_specs, out_specs=out_specs)
    return _pcall(body, name=name, out_shape=out_shape, grid_spec=spec,
                  compiler_params=_params(("arbitrary",)))(place, *args)


def adamw(name, jobs):
    c1 = 1.0 / (1.0 - ADAM_B1 ** ADAM_STEP)
    c2 = 1.0 / (1.0 - ADAM_B2 ** ADAM_STEP)
    nj = len(jobs)
    tiles, steps = _job_tiles([j[0].shape for j in jobs], 1 << 18, 8)

    def body(*refs):
        i = pl.program_id(0)
        for k, (_, n) in enumerate(tiles):
            w_ref, g_ref, m_ref, v_ref = refs[4 * k:4 * k + 4]
            d_ref, nm_ref, nv_ref = refs[4 * nj + 3 * k:4 * nj + 3 * k + 3]

            @pl.when(i < n)
            def _(w_ref=w_ref, g_ref=g_ref, m_ref=m_ref, v_ref=v_ref, d_ref=d_ref, nm_ref=nm_ref, nv_ref=nv_ref):
                gv = g_ref[...]
                nm = ADAM_B1 * m_ref[...] + (1.0 - ADAM_B1) * gv
                nv = ADAM_B2 * v_ref[...] + (1.0 - ADAM_B2) * (gv * gv)
                nm_ref[...] = nm
                nv_ref[...] = nv
                d_ref[...] = -ADAM_LR * ((nm * c1) / (jnp.sqrt(nv * c2) + ADAM_EPS) + ADAM_WD * w_ref[...])

    in_specs, out_specs, out_shape, args = [], [], [], []
    for (w, g, m, v), (tr, n) in zip(jobs, tiles):
        spec = pl.BlockSpec((tr, w.shape[1]), lambda i, n=n: (jnp.minimum(i, n - 1), 0))
        in_specs += [spec] * 4
        out_specs += [spec] * 3
        out_shape += [jax.ShapeDtypeStruct(w.shape, F32)] * 3
        args += [w, g, m, v]
    res = _pcall(body, name=name, out_shape=out_shape, grid=(steps,), in_specs=in_specs, out_specs=out_specs,
                 compiler_params=_params(("arbitrary",)))(*args)
    return [tuple(res[3 * k:3 * k + 3]) for k in range(nj)]


MATS = ["ffn1_w_in", "ffn1_w_out", "w_mix_in", "w_conv_out", "w_attn_out", "w_mix_out", "ffn2_w_in", "ffn2_w_out",
        "w_ple_gate", "w_ple_proj"]
COL_SHARDED = {"ffn1_w_in", "w_mix_in", "ffn2_w_in", "w_ple_proj"}
NORMS = ["ffn1_norm", "mix_norm", "ffn2_norm", "ple_norm", "final_norm"]
WEIGHTS = ["ffn1_norm", "ffn1_w_in", "ffn1_w_out", "mix_norm", "w_mix_in", "conv_w", "w_conv_out", "w_attn_out",
           "w_mix_out", "ffn2_norm", "ffn2_w_in", "ffn2_w_out", "ple_norm", "w_ple_gate", "w_ple_proj", "final_norm"]


def _pad_rows(a, rows):
    return jnp.concatenate([a, jnp.zeros((rows - a.shape[0],) + a.shape[1:], a.dtype)], axis=0)


def _step(x, p, tgt, w, m, v):
    t, d = x.shape
    tt = _tile(t, 256)
    tm = _tile(t, 512)
    tm2 = _tile(t, 1024)
    tq = _tile(t, 1024)

    chip = 2 * lax.axis_index("x") + lax.axis_index("y")
    place = jnp.stack([chip, lax.axis_index("c")]).astype(jnp.int32)

    placed = dict(zip(MATS, place_shards("place_shards", [w[k] for k in MATS], place)))
    full = {}

    def keep(names, bufs):
        for k, buf in zip(names, bufs):
            full[k] = buf if k in COL_SHARDED else buf.reshape(-1, buf.shape[2])

    def gather_of(names):
        return gather_comm([placed[k] for k in names])

    cw_all = gather_small("gather_conv_w", _pad_rows(w["conv_w"], 8), False)
    cw8 = jnp.concatenate([cw_all[2 * k] for k in range(N_CHIPS)], axis=1)
    g1, gm, g2, gp, gf = (w[k].reshape(1, d) for k in NORMS)

    def ffn_fwd(tag, h, g, first, w_in_name, w_out_name, riders):
        if first:
            n, bufs = rms_fwd(tag + "_norm", h, g, tt, comm=gather_of(first))
            keep(first, bufs)
            (a, s), bufs = ffn_in_act(tag + "_in", n, full[w_in_name], tm, comm=gather_of(riders))
            keep(riders, bufs)
        else:
            a, s, n = ffn_in_act(tag + "_in", h, full[w_in_name], tm, gain=g)
        return n, a, s, mm_nn(tag + "_out", s, full[w_out_name], F32, tm, res=h, alpha=0.5)

    n1, a1, s1, h1 = ffn_fwd("ffn1", x, g1, ["ffn1_w_in"], "ffn1_w_in", "ffn1_w_out", ["ffn1_w_out", "w_mix_in"])
    wmix = full["w_mix_in"]
    riders = [["w_conv_out", "w_attn_out", "w_mix_out"], ["ffn2_w_in"], ["ffn2_w_out", "w_ple_gate", "w_ple_proj"]]
    (cbx, u), bufs = mm_nn_stacked("mix_in_conv", h1, wmix, BF16, tm2, d, 0, 3, comm=gather_of(riders[0]), gain=gm)
    keep(riders[0], bufs)
    qkv, bufs = mm_nn_stacked("mix_in_qkv", u, wmix, BF16, tm2, d, 3, 3, comm=gather_of(riders[1]))
    keep(riders[1], bufs)
    gates, bufs = mm_nn_stacked("mix_in_gates", u, wmix, BF16, tm2, d, 6, 2, comm=gather_of(riders[2]))
    keep(riders[2], bufs)
    wpp = full["w_ple_proj"]
    wpp = jnp.transpose(wpp, (1, 0, 2)).reshape(wpp.shape[1], -1)
    ycin, y_conv = conv_out_fwd("conv_out", cbx, cw8, full["w_conv_out"], tt)
    o = attn_fwd("attn", qkv, tq)
    y_attn = mm_nn("attn_out", o, full["w_attn_out"], BF16, tm)
    merged, h2 = mix_out_fwd("mix_out", gates, y_conv, y_attn, h1, full["w_mix_out"], tm)
    n2, a2, s2, h3 = ffn_fwd("ffn2", h2, g2, [], "ffn2_w_in", "ffn2_w_out", [])

    pieces, chip_sums, halves = {}, {}, {}

    def as_pieces(k):
        pc = pieces[k]
        return pc if k in COL_SHARDED else pc.reshape(N_CHIPS, 2, pc.shape[0] // (2 * N_CHIPS), pc.shape[1])

    def sum_siblings(tag, names):
        pcs = [as_pieces(k) for k in names]
        got = run_comm("exchange_" + tag, exchange_comm(pcs))
        chip_sums.update(zip(names, sum_cores("sum_cores_" + tag, pcs, got, place)))

    def scatter_of(names):
        return scatter_comm([chip_sums[k] for k in names])

    def sum_landed(tag, names, landed):
        halves.update(zip(names, sum_chips("sum_chips_" + tag, [chip_sums[k] for k in names], landed, place)))

    npl, dh4, dpp, dzg, dgf, loss_row = tail("tail", h3, p, tgt, gp, gf, full["w_ple_gate"], wpp, tm)
    dwpp = mm_tn_whole("ple_proj_dw", p, dpp, tm2)
    pieces["w_ple_proj"] = jnp.transpose(dwpp.reshape(2, p.shape[1] // 2, N_CHIPS, d // N_CHIPS), (2, 0, 1, 3))
    pieces["w_ple_gate"] = mm_tn_rows("ple_gate_dw", npl, dzg, tm2)
    dh3, df2, dgp = mm_nt("ple_gate_dx", dzg, full["w_ple_gate"], F32, tm, d, norm=(h3, gp, dh4), alpha=0.5)
    w_in, w_out = full["ffn2_w_in"], full["ffn2_w_out"]
    pieces["ffn2_w_out"] = mm_tn_rows("ffn2_dwout", s2, df2, tm2)
    da2 = ffn_ds_dact("ffn2_ds", df2, w_out, a2, tm2)
    pieces["ffn2_w_in"] = mm_tn_cols("ffn2_dwin", n2, da2, tm2)
    dh2, dh2b, dg2 = mm_nt_stacked("ffn2_dn", da2, w_in, tm2, (h2, g2, dh3))
    pieces["w_mix_out"] = mm_tn_rows("mix_out_dw", merged, dh2b, tm2)
    dyc, dya, dgates = mix_out_bwd("mix_out_dx", dh2b, full["w_mix_out"], gates, y_conv, y_attn, tm)
    pieces["w_conv_out"] = mm_tn_rows("conv_out_dw", ycin, dyc, tm2)
    dcbx, dcw8 = conv_out_bwd("conv_out_dx", dyc, full["w_conv_out"], cbx, cw8, tt)
    pieces["w_attn_out"] = mm_tn_rows("attn_out_dw", o, dya, tm2)
    do = mm_nt("attn_out_dx", dya, full["w_attn_out"], BF16, tm, d)
    dq, dk, dv = attn_bwd("attn_bwd", qkv, do, tq)
    dmix = [dcbx, dq, dk, dv, dgates]
    early = ["ffn2_w_in", "ffn2_w_out", "w_ple_gate", "w_ple_proj", "w_mix_out", "w_conv_out", "w_attn_out"]
    swap = exchange_comm([as_pieces(k) for k in early])
    pieces["w_mix_in"], got = mm_tn_parts("mix_in_dw", u, dmix, tm2, comm=swap)
    chip_sums.update(zip(early, sum_cores("sum_cores_early", swap.ins, got, place)))
    swap = exchange_comm([as_pieces("w_mix_in")])
    (dh1, df1, dgm), landed = mm_nt_parts("mix_in_dx", dmix, wmix, tm2, (h1, gm, dh2), 0.5,
                                          comm=join_comms(scatter_of(early), swap))
    sum_landed("early", early, landed[:len(early)])
    chip_sums["w_mix_in"] = sum_cores("sum_cores_mix", swap.ins, landed[len(early):], place)[0]
    w_in, w_out = full["ffn1_w_in"], full["ffn1_w_out"]
    pieces["ffn1_w_out"] = mm_tn_rows("ffn1_dwout", s1, df1, tm2)
    da1 = ffn_ds_dact("ffn1_ds", df1, w_out, a1, tm2)
    pieces["ffn1_w_in"], landed = mm_tn_cols("ffn1_dwin", n1, da1, tm2, comm=scatter_of(["w_mix_in"]))
    sum_landed("mix", ["w_mix_in"], landed)
    late = ["ffn1_w_in", "ffn1_w_out"]
    sum_siblings("late", late)
    done = early + ["w_mix_in"]
    (dx, _, dg1), landed = mm_nt_stacked(
        "ffn1_dn", da1, w_in, tm2, (x, g1, dh1),
        comm=join_comms(scatter_of(late), share_comm([halves[k] for k in done])))
    sum_landed("late", late, landed[:len(late)])
    shared = dict(zip(done, landed[len(late):]))

    shared.update(zip(late, run_comm("share_halves", share_comm([halves[k] for k in late]))))
    grad, delta, new_m, new_v = {}, {}, {}, {}
    for k in MATS:
        grad[k] = shared[k].reshape(w[k].shape)

    small = jnp.concatenate([dg1, dgm, dg2, dgp, dgf, dcw8[:3], loss_row, jnp.zeros((7, d), F32)], axis=0)
    tot = gather_small("sum_small", small, True)
    loss = tot[8, 0]
    norm_w = jnp.concatenate([w[k].reshape(1, d) for k in NORMS] + [jnp.zeros((3, d), F32)], axis=0)
    norm_m = jnp.concatenate([m[k].reshape(1, d) for k in NORMS] + [jnp.zeros((3, d), F32)], axis=0)
    norm_v = jnp.concatenate([v[k].reshape(1, d) for k in NORMS] + [jnp.ones((3, d), F32)], axis=0)
    norm_g = jnp.concatenate([tot[0:5], jnp.zeros((3, d), F32)], axis=0)
    cs = d // N_CHIPS
    gcw = lax.dynamic_slice(tot[5:8], (0, chip * cs), (3, cs))
    conv_job = (_pad_rows(w["conv_w"], 8), _pad_rows(gcw, 8), _pad_rows(m["conv_w"], 8),
                jnp.concatenate([v["conv_w"], jnp.ones((5, cs), F32)], axis=0))

    steps = adamw("adamw", [(w[k], grad[k], m[k], v[k]) for k in MATS]
                  + [(norm_w, norm_g, norm_m, norm_v), conv_job])
    for k, res in zip(MATS, steps):
        delta[k], new_m[k], new_v[k] = res
    nd, nm, nv = steps[len(MATS)]
    for r, k in enumerate(NORMS):
        grad[k] = norm_g[r].reshape(w[k].shape)
        delta[k], new_m[k], new_v[k] = (a[r].reshape(w[k].shape) for a in (nd, nm, nv))
    cd, cm, cv = steps[len(MATS) + 1]
    grad["conv_w"], delta["conv_w"], new_m["conv_w"], new_v["conv_w"] = gcw, cd[:3], cm[:3], cv[:3]
    return loss, dx, grad, delta, new_m, new_v


def kernel(x, p, ffn1_norm, ffn1_w_in, ffn1_w_out, mix_norm, w_mix_in, conv_w, w_conv_out, w_attn_out, w_mix_out, ffn2_norm, ffn2_w_in, ffn2_w_out, ple_norm, w_ple_gate, w_ple_proj, final_norm, loss_target, m_ffn1_norm, m_ffn1_w_in, m_ffn1_w_out, m_mix_norm, m_w_mix_in, m_conv_w, m_w_conv_out, m_w_attn_out, m_w_mix_out, m_ffn2_norm, m_ffn2_w_in, m_ffn2_w_out, m_ple_norm, m_w_ple_gate, m_w_ple_proj, m_final_norm, v_ffn1_norm, v_ffn1_w_in, v_ffn1_w_out, v_mix_norm, v_w_mix_in, v_conv_w, v_w_conv_out, v_w_attn_out, v_w_mix_out, v_ffn2_norm, v_ffn2_w_in, v_ffn2_w_out, v_ple_norm, v_w_ple_gate, v_w_ple_proj, v_final_norm):
    ws = (ffn1_norm, ffn1_w_in, ffn1_w_out, mix_norm, w_mix_in, conv_w, w_conv_out, w_attn_out, w_mix_out, ffn2_norm,
          ffn2_w_in, ffn2_w_out, ple_norm, w_ple_gate, w_ple_proj, final_norm)
    ms = (m_ffn1_norm, m_ffn1_w_in, m_ffn1_w_out, m_mix_norm, m_w_mix_in, m_conv_w, m_w_conv_out, m_w_attn_out,
          m_w_mix_out, m_ffn2_norm, m_ffn2_w_in, m_ffn2_w_out, m_ple_norm, m_w_ple_gate, m_w_ple_proj, m_final_norm)
    vs = (v_ffn1_norm, v_ffn1_w_in, v_ffn1_w_out, v_mix_norm, v_w_mix_in, v_conv_w, v_w_conv_out, v_w_attn_out,
          v_w_mix_out, v_ffn2_norm, v_ffn2_w_in, v_ffn2_w_out, v_ple_norm, v_w_ple_gate, v_w_ple_proj, v_final_norm)
    assert x.shape[0] == 1 and p.shape[:2] == (1, 1), "one sequence and one layer per device"

    def strip(a):
        return a[0] if a.ndim == 3 or (a.ndim == 2 and a.shape[0] == 1) else a

    w = {k: strip(a) for k, a in zip(WEIGHTS, ws)}
    m = {k: strip(a) for k, a in zip(WEIGHTS, ms)}
    v = {k: strip(a) for k, a in zip(WEIGHTS, vs)}
    loss, dx, grad, delta, new_m, new_v = _step(x[0], p[0, 0], loss_target[0], w, m, v)
    shapes = [a.shape for a in ws]
    outs = [loss, dx[None]]
    for res in (grad, delta, new_m, new_v):
        outs += [res[k].reshape(s) for k, s in zip(WEIGHTS, shapes)]
    return tuple(outs)
```

```python
import functools
import math

import jax
import jax.numpy as jnp
from jax import lax
from jax.experimental import pallas as pl
from jax.experimental.pallas import tpu as pltpu

F32 = jnp.float32
BF16 = jnp.bfloat16
MESH = pl.DeviceIdType.MESH
ANY = pl.BlockSpec(memory_space=pl.ANY)

HEAD_DIM = 128
NORM_EPS = 1e-6
N_CHIPS = 4
N_DEV = 8
BF16_ROWS = 16
VMEM_LIMIT = 56 * 1024 * 1024
ACC_BYTES = 8 * 1024 * 1024
STICK_EXIT = 110.0

ADAM_LR = 0.001
ADAM_B1 = 0.9
ADAM_B2 = 0.999
ADAM_EPS = 1e-08
ADAM_WD = 0.01
ADAM_STEP = 10

NN = (((1,), (0,)), ((), ()))
NT = (((1,), (1,)), ((), ()))
TN = (((0,), (0,)), ((), ()))


def _params(sem=None, **kw):
    if sem is not None:
        kw["dimension_semantics"] = sem
    return pltpu.CompilerParams(vmem_limit_bytes=VMEM_LIMIT, **kw)


def _pcall(body, **kw):
    return pl.pallas_call(body, **kw)


def _tile(n, pref, mult=8):
    best = None
    for d in range(mult, min(n, pref) + 1, mult):
        if n % d == 0:
            best = d
    return best if best is not None else n


def _dot(a, b, dims):
    return lax.dot_general(a, b, dims, preferred_element_type=F32)


def _call(name, body, grid, in_specs, out_specs, out_shape, args, scratch=(), sem=None, comm=None):
    n_in, n_out, n_sc = len(in_specs), len(out_specs), len(scratch)
    if comm is None:
        def plain(*refs):
            body(refs[:n_in], refs[n_in:n_in + n_out], refs[n_in + n_out:])

        return _pcall(plain, name=name, out_shape=list(out_shape), grid=grid, in_specs=list(in_specs),
                      out_specs=list(out_specs), scratch_shapes=list(scratch), compiler_params=_params(sem))(*args)
    n_cin, n_cout = len(comm.ins), len(comm.outs)
    steps = math.prod(grid)

    def hosted(*refs):
        ins, c_ins = refs[:n_in], refs[n_in:n_in + n_cin]
        outs = refs[n_in + n_cin:n_in + n_cin + n_out]
        c_outs = refs[n_in + n_cin + n_out:n_in + n_cin + n_out + n_cout]
        rest = refs[n_in + n_cin + n_out + n_cout:]
        sems = rest[n_sc:]
        step = pl.program_id(0)
        for ax in range(1, len(grid)):
            step = step * grid[ax] + pl.program_id(ax)

        @pl.when(step == 0)
        def _():
            comm.first(c_ins, c_outs, sems)

        body(ins, outs, rest[:n_sc])

        @pl.when(step == (3 * steps) // 4)
        def _():
            comm.mid(c_ins, c_outs, sems)

        @pl.when(step == steps - 1)
        def _():
            comm.last(c_ins, c_outs, sems)

    res = _pcall(hosted, name=name, out_shape=list(out_shape) + comm.outs, grid=grid,
                 in_specs=list(in_specs) + [ANY] * n_cin, out_specs=list(out_specs) + [ANY] * n_cout,
                 input_output_aliases={n_in + k: n_out + v for k, v in comm.aliases.items()},
                 scratch_shapes=list(scratch) + comm.sems,
                 compiler_params=_params(("arbitrary",) * len(grid)))(*args, *comm.ins)
    return list(res[:n_out]), list(res[n_out:])


NORM_CHUNK = 256


def _norm_bwd_tile(read_dn, rows, first, h_ref, g_ref, dr_ref, dh_ref, dhb_ref, dg_ref, alpha):
    @pl.when(first)
    def _():
        dg_ref[...] = jnp.zeros_like(dg_ref)

    gv = g_ref[...]
    tot = jnp.zeros_like(gv)
    for c0 in range(0, rows, NORM_CHUNK):
        sl = slice(c0, min(rows, c0 + NORM_CHUNK))
        hv = h_ref[sl, :]
        rs = _rstd(hv)
        hn = hv * rs
        dnv = read_dn(sl)
        gy = dnv * gv
        dh = dr_ref[sl, :] + rs * (gy - hn * jnp.mean(gy * hn, axis=-1, keepdims=True))
        dh_ref[sl, :] = dh
        dhb_ref[sl, :] = (alpha * dh).astype(BF16)
        tot = tot + jnp.sum(dnv * hn, axis=0, keepdims=True)
    dg_ref[...] += tot


def _mm(name, a, b, out_sds, grid, a_spec, b_spec, o_spec, dims, acc_shape, res=None, alpha=1.0, comm=None,
        norm=None, gain=None):
    nk = grid[2]

    def body(ins, outs, scratch):
        a_ref, b_ref = ins[:2]
        r_ref = ins[2] if res is not None else None
        o_ref = outs[0]
        if gain is not None:
            n_ref = scratch[-1]

            @pl.when(jnp.logical_and(pl.program_id(1) == 0, pl.program_id(2) == 0))
            def _():
                hv = a_ref[...]
                n_ref[...] = (hv * _rstd(hv) * ins[-1][...]).astype(BF16)
                outs[-1][...] = n_ref[...]

            a_ref = n_ref

        def finish(read):
            if norm is not None:
                first = jnp.logical_and(pl.program_id(0) == 0, pl.program_id(1) == 0)
                _norm_bwd_tile(read, o_ref.shape[0], first, *ins[2:5], *outs, alpha)
                return
            r = read(slice(None))
            if alpha != 1.0:
                r = r * alpha
            if r_ref is not None:
                r = r_ref[...] + r
            if len(o_ref.shape) == 3:
                half = o_ref.shape[1]
                o_ref[0] = r[:half].astype(o_ref.dtype)
                o_ref[1] = r[half:].astype(o_ref.dtype)
            else:
                o_ref[...] = r.astype(o_ref.dtype)

        if nk == 1:
            part = _dot(a_ref[...].astype(BF16), b_ref[...].astype(BF16), dims)
            finish(lambda sl: part[sl])
        else:
            acc_ref = scratch[0]
            kk = pl.program_id(2)

            @pl.when(kk == 0)
            def _():
                acc_ref[...] = jnp.zeros_like(acc_ref)

            acc_ref[...] += _dot(a_ref[...].astype(BF16), b_ref[...].astype(BF16), dims)

            @pl.when(kk == nk - 1)
            def _():
                finish(lambda sl: acc_ref[sl, :])

    in_specs = [a_spec, b_spec]
    args = [a, b]
    out_specs, out_shape = [o_spec], [out_sds]
    sem = ("parallel", "parallel", "arbitrary")
    if res is not None:
        in_specs.append(o_spec)
        args.append(res)
    if norm is not None:
        width = out_sds.shape[1]
        whole = pl.BlockSpec((1, width), lambda i, j, r: (0, 0))
        in_specs += [o_spec, whole, o_spec]
        args += list(norm)
        out_specs = [o_spec, o_spec, whole]
        out_shape = [jax.ShapeDtypeStruct(out_sds.shape, F32), jax.ShapeDtypeStruct(out_sds.shape, BF16),
                     jax.ShapeDtypeStruct((1, width), F32)]
        sem = ("arbitrary", "arbitrary", "arbitrary")
    scratch = [] if nk == 1 else [pltpu.VMEM(acc_shape, F32)]
    if gain is not None:
        in_specs.append(pl.BlockSpec((1, a.shape[1]), lambda i, j, r: (0, 0)))
        args.append(gain)
        out_specs.append(a_spec)
        out_shape.append(jax.ShapeDtypeStruct(a.shape, BF16))
        scratch.append(pltpu.VMEM(a_spec.block_shape, BF16))
        sem = ("parallel", "arbitrary", "arbitrary")
    got = _call(name, body, grid, in_specs, out_specs, out_shape, args, scratch, sem, comm)
    if norm is not None or gain is not None:
        return got if comm is None else (got[0], got[1])
    return got[0] if comm is None else (got[0][0], got[1])


def ffn_in_act(name, n, w4, tm, comm=None, gain=None):
    t, d = n.shape
    cs = w4.shape[2]

    def body(ins, outs, scratch):
        wg_ref, wu_ref = ins[-2:]
        a_ref, s_ref = outs[:2]
        if gain is None:
            nv = ins[0][...]
        else:
            hv = ins[0][...]
            nv = (hv * _rstd(hv) * ins[1][...]).astype(BF16)

            @pl.when(pl.program_id(0) == 0)
            def _():
                outs[2][...] = nv
        gate = _dot(nv, wg_ref[...], NN)
        up = _dot(nv, wu_ref[...], NN)
        a_ref[0] = gate.astype(BF16)
        a_ref[1] = up.astype(BF16)
        s_ref[...] = (gate * jax.nn.sigmoid(gate) * up).astype(BF16)

    rows = pl.BlockSpec((tm, d), lambda j, i: (i, 0))
    in_specs = [rows] + ([] if gain is None else [pl.BlockSpec((1, d), lambda j, i: (0, 0))])
    in_specs += [pl.BlockSpec((None, d, cs), lambda j, i: (j, 0, 0)),
                 pl.BlockSpec((None, d, cs), lambda j, i: (2 + j, 0, 0))]
    out_specs = [pl.BlockSpec((2, tm, cs), lambda j, i: (0, i, j)), pl.BlockSpec((tm, cs), lambda j, i: (i, j))]
    out_shape = [jax.ShapeDtypeStruct((2, t, 2 * cs), BF16), jax.ShapeDtypeStruct((t, 2 * cs), BF16)]
    if gain is not None:
        out_specs.append(pl.BlockSpec((tm, d), lambda j, i: (jnp.where(j == 0, i, t // tm - 1), 0)))
        out_shape.append(jax.ShapeDtypeStruct((t, d), BF16))
    got = _call(name, body, (2, t // tm), in_specs, out_specs, out_shape,
                [n] + ([] if gain is None else [gain]) + [w4, w4], (), ("arbitrary", "arbitrary"), comm)
    return got if comm is None else (got[0], got[1])


def ffn_ds_dact(name, df, w_out, a3, tm):
    t, d = df.shape
    f = w_out.shape[0]
    cs = f // 2

    def body(ins, outs, scratch):
        df_ref, w_ref, a_ref = ins
        ds = _dot(df_ref[...], w_ref[...], NT)
        for c0 in range(0, tm, NORM_CHUNK):
            sl = slice(c0, min(tm, c0 + NORM_CHUNK))
            gate = a_ref[0, sl, :].astype(F32)
            up = a_ref[1, sl, :].astype(F32)
            sg = jax.nn.sigmoid(gate)
            outs[0][0, sl, :] = (ds[sl] * up * sg * (1.0 + gate * (1.0 - sg))).astype(BF16)
            outs[0][1, sl, :] = (ds[sl] * gate * sg).astype(BF16)

    blk = pl.BlockSpec((2, tm, cs), lambda i, j: (0, i, j))
    return _call(name, body, (t // tm, 2),
                 [pl.BlockSpec((tm, d), lambda i, j: (i, 0)), pl.BlockSpec((cs, d), lambda i, j: (j, 0)), blk],
                 [blk], [jax.ShapeDtypeStruct((2, t, f), BF16)], [df, w_out, a3], (), ("parallel", "parallel"))[0]


def _part_ranges(parts, d):
    out, lo = [], 0
    for p in parts:
        out.append((lo, p.shape[1] // d))
        lo += p.shape[1] // d
    return out, lo


def mm_nt_parts(name, parts, w4, tm, norm, alpha, comm=None):
    m = parts[0].shape[0]
    d, cs = w4.shape[1], w4.shape[2]
    per = cs // d
    ranges, nblk = _part_ranges(parts, d)
    np_ = len(parts)
    nt = m // tm
    chunk = tm // nblk

    def body(ins, outs, scratch):
        w_ref, acc = ins[np_], scratch[0]
        i, r = pl.program_id(0), pl.program_id(1)

        @pl.when(jnp.logical_and(i < nt, r == 0))
        def _():
            acc[i % 2] = jnp.zeros(acc.shape[1:], F32)

        for (lo, n), a_ref in zip(ranges, ins[:np_]):
            @pl.when(jnp.logical_and(i < nt, jnp.logical_and(r >= lo, r < lo + n)))
            def _(a_ref=a_ref):
                acc[i % 2] += _dot(a_ref[...], w_ref[...], NT)

        @pl.when(i > 0)
        def _():
            rows = pl.ds(pl.multiple_of(r * chunk, chunk), chunk)
            first = jnp.logical_and(i == 1, r == 0)
            _norm_bwd_tile(lambda sl: acc[(i - 1) % 2, rows, :][sl], chunk, first, *ins[np_ + 1:], *outs, alpha)

    def ahead(i, r):
        return jnp.where(i < nt, r, nblk - 1)

    rows = pl.BlockSpec((chunk, d), lambda i, r: (jnp.where(i == 0, 0, (i - 1) * nblk + r), 0))
    whole = pl.BlockSpec((1, d), lambda i, r: (0, 0))
    specs = [pl.BlockSpec((tm, d), lambda i, r, lo=lo, n=n: (jnp.minimum(i, nt - 1), jnp.clip(ahead(i, r) - lo, 0, n - 1)))
             for lo, n in ranges]
    specs += [pl.BlockSpec((None, d, d), lambda i, r: (ahead(i, r) // per, 0, ahead(i, r) % per)), rows, whole, rows]
    got = _call(name, body, (nt + 1, nblk), specs, [rows, rows, whole],
                [jax.ShapeDtypeStruct((m, d), F32), jax.ShapeDtypeStruct((m, d), BF16),
                 jax.ShapeDtypeStruct((1, d), F32)],
                list(parts) + [w4] + list(norm), [pltpu.VMEM((2, tm, d), F32)], ("arbitrary", "arbitrary"), comm)
    return got if comm is None else (got[0], got[1])


def mm_tn_parts(name, xa, parts, tt, comm=None):
    t, k = xa.shape
    d = k
    pr = k // 2
    ranges, nblk = _part_ranges(parts, d)
    per = nblk // N_CHIPS

    def body(ins, outs, scratch):
        x_ref, acc = ins[0], scratch[0]
        jb, r = pl.program_id(0), pl.program_id(1)

        @pl.when(r == 0)
        def _():
            acc[...] = jnp.zeros_like(acc)

        for (lo, n), p_ref in zip(ranges, ins[1:]):
            @pl.when(jnp.logical_and(jb >= lo, jb < lo + n))
            def _(p_ref=p_ref):
                acc[...] += _dot(x_ref[...], p_ref[...], TN)

        @pl.when(r == t // tt - 1)
        def _():
            outs[0][0] = acc[:pr].astype(BF16)
            outs[0][1] = acc[pr:].astype(BF16)

    def part_spec(lo, n):
        return pl.BlockSpec((tt, d), lambda jb, r: (jnp.where(jnp.logical_and(jb >= lo, jb < lo + n), r, 0),
                                                    jnp.clip(jb - lo, 0, n - 1)))

    specs = [pl.BlockSpec((tt, k), lambda jb, r: (r, 0))] + [part_spec(lo, n) for lo, n in ranges]
    got = _call(name, body, (nblk, t // tt), specs,
                [pl.BlockSpec((None, 2, pr, d), lambda jb, r: (jb // per, 0, 0, jb % per))],
                [jax.ShapeDtypeStruct((N_CHIPS, 2, pr, per * d), BF16)], [xa] + list(parts),
                [pltpu.VMEM((k, d), F32)], ("parallel", "arbitrary"), comm)
    return got[0] if comm is None else (got[0][0], got[1])


def mm_nn(name, a, w, out_dtype, tm, res=None, alpha=1.0):
    m, k = a.shape
    n = w.shape[1]
    return _mm(name, a, w, jax.ShapeDtypeStruct((m, n), out_dtype), (m // tm, 1, 1),
               pl.BlockSpec((tm, k), lambda i, j, r: (i, 0)),
               pl.BlockSpec((k, n), lambda i, j, r: (0, 0)),
               pl.BlockSpec((tm, n), lambda i, j, r: (i, 0)), NN, None, res=res, alpha=alpha)


def mm_nn_stacked(name, a, w4, out_dtype, tm, tn, j0=0, nj=None, comm=None, gain=None):
    m, k = a.shape
    cs = w4.shape[2]
    per = cs // tn
    nj = N_CHIPS * per - j0 if nj is None else nj
    return _mm(name, a, w4, jax.ShapeDtypeStruct((m, nj * tn), out_dtype), (m // tm, nj, 1),
               pl.BlockSpec((tm, k), lambda i, j, r: (i, 0)),
               pl.BlockSpec((None, k, tn), lambda i, j, r: ((j + j0) // per, 0, (j + j0) % per)),
               pl.BlockSpec((tm, tn), lambda i, j, r: (i, j)), NN, None, comm=comm, gain=gain)


def mm_nt(name, dy, w, out_dtype, tm, tko, norm=None, alpha=1.0):
    m, n = dy.shape
    k = w.shape[0]
    return _mm(name, dy, w, jax.ShapeDtypeStruct((m, k), out_dtype), (m // tm, k // tko, 1),
               pl.BlockSpec((tm, n), lambda i, j, r: (i, 0)),
               pl.BlockSpec((tko, n), lambda i, j, r: (j, 0)),
               pl.BlockSpec((tm, tko), lambda i, j, r: (i, j)), NT, None, norm=norm, alpha=alpha)


def mm_nt_stacked(name, dy, w4, tm, norm, alpha=1.0, comm=None):
    m = dy.shape[1]
    k, cs = w4.shape[1], w4.shape[2]
    nt = m // tm
    chunk = tm // N_CHIPS

    def body(ins, outs, scratch):
        dy_ref, w_ref, h_ref, g_ref, dr_ref = ins
        dh_ref, dhb_ref, dg_ref = outs
        acc = scratch[0]
        i, r = pl.program_id(0), pl.program_id(1)

        @pl.when(jnp.logical_and(i < nt, r == 0))
        def _():
            acc[i % 2] = jnp.zeros(acc.shape[1:], F32)

        @pl.when(i < nt)
        def _():
            acc[i % 2] += _dot(dy_ref[...], w_ref[...], NT)

        @pl.when(i > 0)
        def _():
            rows = pl.ds(pl.multiple_of(r * chunk, chunk), chunk)
            first = jnp.logical_and(i == 1, r == 0)
            _norm_bwd_tile(lambda sl: acc[(i - 1) % 2, rows, :][sl], chunk, first, h_ref, g_ref, dr_ref,
                           dh_ref, dhb_ref, dg_ref, alpha)

    def behind(i, r):
        return (jnp.where(i == 0, 0, (i - 1) * N_CHIPS + r), 0)

    def ahead(i, r):
        return jnp.where(i < nt, r, N_CHIPS - 1)

    rows = pl.BlockSpec((chunk, k), behind)
    whole = pl.BlockSpec((1, k), lambda i, r: (0, 0))
    got = _call(name, body, (nt + 1, N_CHIPS),
                [pl.BlockSpec((None, tm, cs), lambda i, r: (ahead(i, r) // 2, jnp.minimum(i, nt - 1), ahead(i, r) % 2)),
                 pl.BlockSpec((None, k, cs), lambda i, r: (ahead(i, r), 0, 0)), rows, whole, rows],
                [rows, rows, whole],
                [jax.ShapeDtypeStruct((m, k), F32), jax.ShapeDtypeStruct((m, k), BF16),
                 jax.ShapeDtypeStruct((1, k), F32)],
                [dy, w4] + list(norm), [pltpu.VMEM((2, tm, k), F32)], ("arbitrary", "arbitrary"), comm)
    return got if comm is None else (got[0], got[1])


def mm_tn_rows(name, xa, dy, tt):
    t, k = xa.shape
    n = dy.shape[1]
    tkr = k if k * n * 4 <= ACC_BYTES else k // 2
    return _mm(name, xa, dy, jax.ShapeDtypeStruct((k, n), BF16), (k // tkr, 1, t // tt),
               pl.BlockSpec((tt, tkr), lambda i, j, r: (r, i)),
               pl.BlockSpec((tt, n), lambda i, j, r: (r, 0)),
               pl.BlockSpec((tkr, n), lambda i, j, r: (i, 0)), TN, (tkr, n))


def mm_tn_whole(name, xa, dy, tt):
    t, k = xa.shape
    n = dy.shape[1]
    return _mm(name, xa, dy, jax.ShapeDtypeStruct((k, n), BF16), (1, 1, t // tt),
               pl.BlockSpec((tt, k), lambda i, j, r: (r, 0)),
               pl.BlockSpec((tt, n), lambda i, j, r: (r, 0)),
               pl.BlockSpec((k, n), lambda i, j, r: (0, 0)), TN, (k, n))


def mm_tn_cols(name, xa, dy, tt, comm=None):
    t, k = xa.shape
    pr = k // 2
    if dy.ndim == 3:
        cs = dy.shape[2] // 2
        dy_spec = pl.BlockSpec((None, tt, cs), lambda i, j, r: (j // 2, r, j % 2))
    else:
        cs = dy.shape[1] // N_CHIPS
        dy_spec = pl.BlockSpec((tt, cs), lambda i, j, r: (r, j))
    return _mm(name, xa, dy, jax.ShapeDtypeStruct((N_CHIPS, 2, pr, cs), BF16), (1, N_CHIPS, t // tt),
               pl.BlockSpec((tt, k), lambda i, j, r: (r, 0)), dy_spec,
               pl.BlockSpec((None, 2, pr, cs), lambda i, j, r: (j, 0, 0, 0)), TN, (k, cs), comm=comm)


def _rows(tt, w, col=0):
    return pl.BlockSpec((tt, w), lambda i: (i, col))


def _whole(shape):
    return pl.BlockSpec(shape, lambda i: (0,) * len(shape))


def _rstd(h):
    return lax.rsqrt(jnp.mean(h * h, axis=-1, keepdims=True) + NORM_EPS)


def rms_fwd(name, h, g, tt, comm=None):
    t, d = h.shape

    def body(ins, outs, scratch):
        hv = ins[0][...]
        outs[0][...] = (hv * _rstd(hv) * ins[1][...]).astype(BF16)

    got = _call(name, body, (t // tt,), [_rows(tt, d), _whole((1, d))], [_rows(tt, d)],
                [jax.ShapeDtypeStruct((t, d), BF16)], [h, g], (), ("parallel",), comm)
    return got[0] if comm is None else (got[0][0], got[1])


def mix_out_fwd(name, gates, yc, ya, h, w, tt):
    t, d = yc.shape

    def body(g_ref, yc_ref, ya_ref, h_ref, w_ref, m_ref, o_ref):
        merged = (jax.nn.sigmoid(g_ref[:, :d].astype(F32)) * yc_ref[...].astype(F32)
                  + jax.nn.sigmoid(g_ref[:, d:].astype(F32)) * ya_ref[...].astype(F32)).astype(BF16)
        m_ref[...] = merged
        o_ref[...] = h_ref[...] + _dot(merged, w_ref[...], NN)

    return _pcall(body, name=name,
                  out_shape=(jax.ShapeDtypeStruct((t, d), BF16), jax.ShapeDtypeStruct((t, d), F32)),
                  grid=(t // tt,),
                  in_specs=[_rows(tt, 2 * d), _rows(tt, d), _rows(tt, d), _rows(tt, d), _whole((d, d))],
                  out_specs=(_rows(tt, d), _rows(tt, d)),
                  compiler_params=_params(("parallel",)))(gates, yc, ya, h, w)


def mix_out_bwd(name, dh, w, gates, yc, ya, tt):
    t, d = yc.shape

    def body(dh_ref, w_ref, g_ref, yc_ref, ya_ref, dyc_ref, dya_ref, dg_ref):
        dmv = _dot(dh_ref[...], w_ref[...], NT)
        sc = jax.nn.sigmoid(g_ref[:, :d].astype(F32))
        sa = jax.nn.sigmoid(g_ref[:, d:].astype(F32))
        dyc_ref[...] = (dmv * sc).astype(BF16)
        dya_ref[...] = (dmv * sa).astype(BF16)
        dg_ref[:, :d] = (dmv * yc_ref[...].astype(F32) * sc * (1.0 - sc)).astype(BF16)
        dg_ref[:, d:] = (dmv * ya_ref[...].astype(F32) * sa * (1.0 - sa)).astype(BF16)

    return _pcall(body, name=name,
                  out_shape=(jax.ShapeDtypeStruct((t, d), BF16), jax.ShapeDtypeStruct((t, d), BF16),
                             jax.ShapeDtypeStruct((t, 2 * d), BF16)),
                  grid=(t // tt,),
                  in_specs=[_rows(tt, d), _whole((d, d)), _rows(tt, 2 * d), _rows(tt, d), _rows(tt, d)],
                  out_specs=(_rows(tt, d), _rows(tt, d), _rows(tt, 2 * d)),
                  compiler_params=_params(("parallel",)))(dh, w, gates, yc, ya)


def _shift_down(cur, prev8, s):
    tt = cur.shape[0]
    rolled = pltpu.roll(cur, s, 0)
    row8 = lax.broadcasted_iota(jnp.int32, prev8.shape, 0)
    first8 = jnp.where(row8 < s, pltpu.roll(prev8, s, 0), rolled[:8])
    return jnp.concatenate([first8, rolled[8:]], axis=0) if tt > 8 else first8


def _shift_up(cur, next8, s):
    tt = cur.shape[0]
    rolled = pltpu.roll(cur, tt - s, 0)
    row8 = lax.broadcasted_iota(jnp.int32, next8.shape, 0)
    last8 = jnp.where(row8 >= 8 - s, pltpu.roll(next8, 8 - s, 0), rolled[tt - 8:])
    return jnp.concatenate([rolled[:tt - 8], last8], axis=0) if tt > 8 else last8


def _prev_rows(tt, d, col):
    return pl.BlockSpec((BF16_ROWS, d), lambda i: (jnp.maximum(i * (tt // BF16_ROWS) - 1, 0), col))


def _next_rows(tt, d, col, t):
    return pl.BlockSpec((BF16_ROWS, d),
                        lambda i: (jnp.minimum((i + 1) * (tt // BF16_ROWS), t // BF16_ROWS - 1), col))


def conv_out_fwd(name, cbx, cw8, w_out, tt):
    t, d3 = cbx.shape
    d = d3 // 3

    def body(cb_ref, cc_ref, cx_ref, pc_ref, px_ref, w_ref, wo_ref, o_ref, y_ref):
        has_prev = (pl.program_id(0) > 0).astype(F32)
        cc = cc_ref[...].astype(F32) * cx_ref[...].astype(F32)
        prev = pc_ref[...].astype(F32)[8:] * px_ref[...].astype(F32)[8:] * has_prev
        w = w_ref[...]
        conv = w[0:1] * _shift_down(cc, prev, 2) + w[1:2] * _shift_down(cc, prev, 1) + w[2:3] * cc
        ycin = (cb_ref[...].astype(F32) * conv).astype(BF16)
        o_ref[...] = ycin
        y_ref[...] = _dot(ycin, wo_ref[...], NN).astype(BF16)

    out = jax.ShapeDtypeStruct((t, d), BF16)
    return _pcall(body, name=name, out_shape=(out, out), grid=(t // tt,),
                  in_specs=[_rows(tt, d, 0), _rows(tt, d, 1), _rows(tt, d, 2), _prev_rows(tt, d, 1),
                            _prev_rows(tt, d, 2), _whole((8, d)), _whole((d, d))],
                  out_specs=(_rows(tt, d), _rows(tt, d)),
                  compiler_params=_params(("parallel",)))(cbx, cbx, cbx, cbx, cbx, cw8, w_out)


def conv_out_bwd(name, dyc, w_out, cbx, cw8, tt):
    t, d3 = cbx.shape
    d = d3 // 3
    n = t // tt

    def body(dy_ref, ndy_ref, wo_ref, cb_ref, cc_ref, cx_ref, pc_ref, px_ref, ncb_ref, w_ref, o_ref, dw_ref):
        i = pl.program_id(0)
        has_prev = (i > 0).astype(F32)
        has_next = (i < n - 1).astype(F32)
        cb = cb_ref[...].astype(F32)
        ccv = cc_ref[...].astype(F32)
        cxv = cx_ref[...].astype(F32)
        cc = ccv * cxv
        prev = pc_ref[...].astype(F32)[8:] * px_ref[...].astype(F32)[8:] * has_prev
        w = w_ref[...]
        cc1 = _shift_down(cc, prev, 1)
        cc2 = _shift_down(cc, prev, 2)
        conv = w[0:1] * cc2 + w[1:2] * cc1 + w[2:3] * cc
        dyv = _dot(dy_ref[...], wo_ref[...], NT)
        dconv = dyv * cb
        dnext = _dot(ndy_ref[...], wo_ref[...], NT)[:8] * ncb_ref[...].astype(F32)[:8] * has_next
        dcc = w[2:3] * dconv + w[1:2] * _shift_up(dconv, dnext, 1) + w[0:1] * _shift_up(dconv, dnext, 2)
        o_ref[:, :d] = (dyv * conv).astype(BF16)
        o_ref[:, d:2 * d] = (dcc * cxv).astype(BF16)
        o_ref[:, 2 * d:] = (dcc * ccv).astype(BF16)

        @pl.when(i == 0)
        def _():
            dw_ref[...] = jnp.zeros_like(dw_ref)

        dw_ref[0:1, :] += jnp.sum(dconv * cc2, axis=0, keepdims=True)
        dw_ref[1:2, :] += jnp.sum(dconv * cc1, axis=0, keepdims=True)
        dw_ref[2:3, :] += jnp.sum(dconv * cc, axis=0, keepdims=True)

    return _pcall(body, name=name,
                  out_shape=(jax.ShapeDtypeStruct((t, d3), BF16), jax.ShapeDtypeStruct((8, d), F32)),
                  grid=(n,),
                  in_specs=[_rows(tt, d), _next_rows(tt, d, 0, t),
                            _whole((d, d)), _rows(tt, d, 0), _rows(tt, d, 1), _rows(tt, d, 2),
                            _prev_rows(tt, d, 1), _prev_rows(tt, d, 2), _next_rows(tt, d, 0, t), _whole((8, d))],
                  out_specs=(_rows(tt, d3), _whole((8, d))),
                  compiler_params=_params(("arbitrary",)))(dyc, dyc, w_out, cbx, cbx, cbx, cbx, cbx, cbx, cw8)


def tail(name, h3, p, tgt, gp, gf, w_gate, w_proj, tt):
    t, d = h3.shape
    pd = p.shape[1]

    def body(h_ref, p_ref, tg_ref, gp_ref, gf_ref, wg_ref, wp_ref, np_ref, dh_ref, dpp_ref, dzg_ref, dgf_ref,
             loss_ref):
        hv = h_ref[...]
        npl = (hv * _rstd(hv) * gp_ref[...]).astype(BF16)
        np_ref[...] = npl
        pg = jax.nn.sigmoid(_dot(npl, wg_ref[...], NN))
        ppv = _dot(p_ref[...].astype(BF16), wp_ref[...], NN)
        h4 = hv + pg * ppv
        r4 = _rstd(h4)
        hn = h4 * r4
        gfv = gf_ref[...]
        err = hn * gfv - tg_ref[...]
        dy = err * (1.0 / d)
        gy = dy * gfv
        dh4 = r4 * (gy - hn * jnp.mean(gy * hn, axis=-1, keepdims=True))
        dh_ref[...] = dh4
        dpp_ref[...] = (dh4 * pg).astype(BF16)
        dzg_ref[...] = (dh4 * ppv * pg * (1.0 - pg)).astype(BF16)

        @pl.when(pl.program_id(0) == 0)
        def _():
            dgf_ref[...] = jnp.zeros_like(dgf_ref)
            loss_ref[...] = jnp.zeros_like(loss_ref)

        dgf_ref[...] += jnp.sum(dy * hn, axis=0, keepdims=True)
        tok = jnp.mean(err * err, axis=-1, keepdims=True)
        loss_ref[...] += 0.5 * jnp.sum(tok, axis=0, keepdims=True) * jnp.ones((1, loss_ref.shape[1]), F32)

    return _pcall(body, name=name,
                  out_shape=(jax.ShapeDtypeStruct((t, d), BF16), jax.ShapeDtypeStruct((t, d), F32),
                             jax.ShapeDtypeStruct((t, d), BF16), jax.ShapeDtypeStruct((t, d), BF16),
                             jax.ShapeDtypeStruct((1, d), F32), jax.ShapeDtypeStruct((1, d), F32)),
                  grid=(t // tt,),
                  in_specs=[_rows(tt, d), _rows(tt, pd), _rows(tt, d), _whole((1, d)), _whole((1, d)),
                            _whole((d, d)), _whole((pd, d))],
                  out_specs=(_rows(tt, d), _rows(tt, d), _rows(tt, d), _rows(tt, d), _whole((1, d)),
                             _whole((1, d))),
                  compiler_params=_params(("arbitrary",)))(h3, p, tgt, gp, gf, w_gate, w_proj)


SCALE = 1.0 / math.sqrt(HEAD_DIM)


def _log_stick(z):
    return -(jnp.maximum(z, 0.0) + jnp.log(1.0 + jnp.exp(-jnp.abs(z))))


def _tri_sum(x, tri):
    hi = x.astype(BF16)
    lo = (x - hi.astype(F32)).astype(BF16)
    return _dot(hi, tri, NN) + _dot(lo, tri, NN)


KEY_BLOCK = 128
NEAR = 3
THIN_ROWS = 32


def _pad_block(x):
    n = x.shape[0]
    return x if n == KEY_BLOCK else jnp.concatenate([x, jnp.zeros((KEY_BLOCK - n, x.shape[1]), x.dtype)], axis=0)


def _sb_near(qs, jds, k_ref, below, upper, last_rows):
    near_rows = (KEY_BLOCK,) * (NEAR - 1) + (last_rows,)
    pairs = [(s, b) for s in range(len(qs)) for b in range(NEAR)]
    rows = {(s, b): _block_rows(jnp.maximum(jds[s] - b, 0), KEY_BLOCK) for s, b in pairs}
    z = {(s, b): _dot(qs[s][:near_rows[b]], k_ref[rows[s, b], :], NT) * SCALE for s, b in pairs}
    lg = {(s, b): jnp.where(below, _log_stick(z[s, b]), 0.0) if b == 0 else _log_stick(z[s, b]) for s, b in pairs}
    cum = {(s, b): _tri_sum(lg[s, b], upper) for s, b in pairs}
    out, carries = [], []
    for s in range(len(qs)):
        c = cum[s, 0][:, 0:1]
        blocks = [(rows[s, 0], z[s, 0], jnp.exp(jnp.where(below, z[s, 0] + cum[s, 0], -1e30)))]
        for b in range(1, NEAR):
            live = jds[s] >= b
            off = c[:near_rows[b]] + jnp.where(live, 0.0, -1e30)
            blocks.append((rows[s, b], z[s, b], jnp.exp(z[s, b] + cum[s, b] + off)))
            c = c + _pad_block(jnp.where(live, cum[s, b][:, 0:1], 0.0))
        out.append(blocks)
        carries.append(c)
    return out, carries


def _sb_far(q, kj, upper, c, skip):
    z = _dot(q, kj, NT) * SCALE
    cum = _tri_sum(_log_stick(z), upper)
    return z, jnp.exp(z + cum + (c + jnp.where(skip, -1e30, 0.0))), c + jnp.where(skip, 0.0, cum[:, 0:1])


def _took_it(j, jd, last_rows):
    first = lax.broadcasted_iota(jnp.int32, (KEY_BLOCK, 1), 0) < last_rows
    return jnp.logical_and(j == jd - (NEAR - 1), first)


def _block_rows(j, size):
    return pl.ds(pl.multiple_of(j * size, size), size)


def _sweep_on(st):
    return jnp.logical_and(st[0] >= 0, jnp.max(st[1]) > -STICK_EXIT)


def attn_fwd(name, qkv, tq):
    t, d3 = qkv.shape
    d = d3 // 3
    nh = d // HEAD_DIM
    nq = t // tq
    tb = KEY_BLOCK
    nsub = tq // tb

    def body(q_ref, k_ref, v_ref, o_ref):
        i = pl.program_id(1)
        row = lax.broadcasted_iota(jnp.int32, (tb, tb), 0)
        col = lax.broadcasted_iota(jnp.int32, (tb, tb), 1)
        upper = (row >= col).astype(BF16)
        qs = [q_ref[s * tb:(s + 1) * tb, :] for s in range(nsub)]
        jds = [i * nsub + s for s in range(nsub)]
        near, carries = _sb_near(qs, jds, k_ref, col < row, upper, THIN_ROWS)
        state = []
        for s in range(nsub):
            acc = jnp.zeros((tb, HEAD_DIM), F32)
            for rows, _, a in near[s]:
                acc = acc + _pad_block(_dot(a.astype(BF16), v_ref[rows, :], NN))
            state.append((qs[s], jds[s], carries[s], acc))
        for s, (q, jd, c, acc) in enumerate(state):

            def step(st, q=q, jd=jd):
                rows = _block_rows(st[0], tb)
                _, a, c2 = _sb_far(q, k_ref[rows, :], upper, st[1], _took_it(st[0], jd, THIN_ROWS))
                return st[0] - 1, c2, st[2] + _dot(a.astype(BF16), v_ref[rows, :], NN)

            _, _, acc = lax.while_loop(_sweep_on, step, (jd - (NEAR - 1), c, acc))
            o_ref[s * tb:(s + 1) * tb, :] = acc.astype(o_ref.dtype)

    return _pcall(body, name=name, out_shape=jax.ShapeDtypeStruct((t, d), BF16), grid=(nh, nq),
                  in_specs=[pl.BlockSpec((tq, HEAD_DIM), lambda h, i: (i, h)),
                            pl.BlockSpec((t, HEAD_DIM), lambda h, i: (0, nh + h)),
                            pl.BlockSpec((t, HEAD_DIM), lambda h, i: (0, 2 * nh + h))],
                  out_specs=pl.BlockSpec((tq, HEAD_DIM), lambda h, i: (i, h)),
                  compiler_params=_params(("parallel", "arbitrary")))(qkv, qkv, qkv)


def attn_bwd(name, qkv, do, tq):
    t, d3 = qkv.shape
    d = d3 // 3
    nh = d // HEAD_DIM
    nq = t // tq
    tb = KEY_BLOCK
    nsub = tq // tb

    def body(q_ref, k_ref, v_ref, do_ref, dq_ref, dk_ref, dv_ref, dk_acc, dv_acc, g_buf, z_buf):
        i = pl.program_id(1)

        @pl.when(i == 0)
        def _():
            dk_acc[...] = jnp.zeros_like(dk_acc)
            dv_acc[...] = jnp.zeros_like(dv_acc)

        row = lax.broadcasted_iota(jnp.int32, (tb, tb), 0)
        col = lax.broadcasted_iota(jnp.int32, (tb, tb), 1)
        below = col < row
        upper = (row >= col).astype(BF16)
        lower = (row <= col).astype(BF16)

        qs = [q_ref[s * tb:(s + 1) * tb, :] for s in range(nsub)]
        dos = [do_ref[s * tb:(s + 1) * tb, :] for s in range(nsub)]
        jds = [i * nsub + s for s in range(nsub)]
        near, carries = _sb_near(qs, jds, k_ref, below, upper, KEY_BLOCK)
        da = [[_dot(dos[s][:a.shape[0]], v_ref[rows, :], NT) for rows, _, a in near[s]] for s in range(nsub)]
        state = []
        for s in range(nsub):
            kept = [(rows, z, da[s][b] * a) for b, (rows, z, a) in enumerate(near[s])]
            for rows, _, a in near[s]:
                dv_acc[rows, :] += _dot(a.astype(BF16), dos[s][:a.shape[0]], TN)
            state.append((qs[s], dos[s], jds[s], carries[s], kept))

        carried = []
        for s, (q, dov, jd, c, kept) in enumerate(state):
            def step(st, s=s, q=q, dov=dov, jd=jd):
                j = st[0]
                rows = _block_rows(j, tb)
                z, a, c2 = _sb_far(q, k_ref[rows, :], upper, st[1], _took_it(j, jd, KEY_BLOCK))
                g_buf[jd - j] = _dot(dov, v_ref[rows, :], NT) * a
                z_buf[jd - j] = z
                dv_acc[rows, :] += _dot(a.astype(BF16), dov, TN)
                return j - 1, c2

            j_stop, _ = lax.while_loop(_sweep_on, step, (jd - (NEAR - 1), c))

            def far(j, st, s=s, q=q, jd=jd):
                run, dq = st
                rows = _block_rows(j, tb)
                g = g_buf[jd - j]
                dz = (g - jax.nn.sigmoid(z_buf[jd - j]) * (run + _tri_sum(g, lower))).astype(BF16)
                dk_acc[rows, :] += _dot(dz, q, TN)
                return run + jnp.sum(g, axis=1, keepdims=True), dq + _dot(dz, k_ref[rows, :], NN)

            carried.append(lax.fori_loop(j_stop + 1, jd - (NEAR - 1) + 1, far,
                                         (jnp.zeros((tb, 1), F32), jnp.zeros((tb, HEAD_DIM), F32))))

        tri = [[_dot(g.astype(BF16), lower, NN) for _, _, g in st[4]] for st in state]
        sig = [[jax.nn.sigmoid(z) for _, z, _ in st[4]] for st in state]
        for s, (q, dov, jd, c, kept) in enumerate(state):
            run, dq = carried[s]
            for b in reversed(range(NEAR)):
                rows, z, g = kept[b]
                n = g.shape[0]
                dz = g - sig[s][b] * (run[:n] + tri[s][b])
                if b == 0:
                    dz = jnp.where(below, dz, 0.0)
                dz = dz.astype(BF16)
                dk_acc[rows, :] += _dot(dz, q[:n], TN)
                dq = dq + _pad_block(_dot(dz, k_ref[rows, :], NN))
                if b:
                    run = run + _pad_block(jnp.sum(g, axis=1, keepdims=True))
            dq_ref[s * tb:(s + 1) * tb, :] = (dq * SCALE).astype(BF16)

        @pl.when(i == nq - 1)
        def _():
            dk_ref[...] = (dk_acc[...] * SCALE).astype(BF16)
            dv_ref[...] = dv_acc[...].astype(BF16)

    blk = pl.BlockSpec((tq, HEAD_DIM), lambda h, i: (i, h))
    col_h = pl.BlockSpec((t, HEAD_DIM), lambda h, i: (0, h))
    out = jax.ShapeDtypeStruct((t, d), BF16)
    return _pcall(body, name=name, out_shape=(out, out, out), grid=(nh, nq),
                  in_specs=[blk,
                            pl.BlockSpec((t, HEAD_DIM), lambda h, i: (0, nh + h)),
                            pl.BlockSpec((t, HEAD_DIM), lambda h, i: (0, 2 * nh + h)),
                            blk],
                  out_specs=(blk, col_h, col_h),
                  scratch_shapes=[pltpu.VMEM((t, HEAD_DIM), F32), pltpu.VMEM((t, HEAD_DIM), F32),
                                  pltpu.VMEM((t // tb, tb, tb), F32), pltpu.VMEM((t // tb, tb, tb), F32)],
                  compiler_params=_params(("parallel", "arbitrary")))(qkv, qkv, qkv, do)


def _place():
    x, y, c = lax.axis_index("x"), lax.axis_index("y"), lax.axis_index("c")
    chips = [(1 - x, y), (x, 1 - y), (1 - x, 1 - y)]
    return x, y, c, chips


def _remote(src, dst, send_sem, recv_sem, dev):
    return pltpu.make_async_remote_copy(src_ref=src, dst_ref=dst, send_sem=send_sem, recv_sem=recv_sem,
                                        device_id=dev, device_id_type=MESH)


def place_shards(name, ws, chip):
    tiles, steps = _job_tiles([w.shape for w in ws], 1 << 20, BF16_ROWS)
    nj = len(ws)

    def body(chip_ref, *refs):
        i = pl.program_id(0)
        for k, (_, n) in enumerate(tiles):
            @pl.when(i < n)
            def _(w_ref=refs[k], o_ref=refs[nj + k]):
                o_ref[...] = w_ref[...].astype(BF16)

    spec = pltpu.PrefetchScalarGridSpec(
        num_scalar_prefetch=1, grid=(steps,),
        in_specs=[pl.BlockSpec((tr, w.shape[1]), lambda i, s, n=n: (jnp.minimum(i, n - 1), 0))
                  for w, (tr, n) in zip(ws, tiles)],
        out_specs=[pl.BlockSpec((None, tr, w.shape[1]), lambda i, s, n=n: (s[0], jnp.minimum(i, n - 1), 0))
                   for w, (tr, n) in zip(ws, tiles)])
    return _pcall(body, name=name, out_shape=[jax.ShapeDtypeStruct((N_CHIPS,) + w.shape, BF16) for w in ws],
                  grid_spec=spec, compiler_params=_params(("arbitrary",)))(chip, *ws)


class Comm:
    def __init__(self, ins, outs, aliases, sems, first, mid, last):
        self.ins, self.outs, self.aliases, self.sems = list(ins), list(outs), dict(aliases), list(sems)
        self.first, self.mid, self.last = first, mid, last


def run_comm(name, comm):
    ni, no = len(comm.ins), len(comm.outs)

    def body(*refs):
        ins, outs, sems = refs[:ni], refs[ni:ni + no], refs[ni + no:]
        comm.first(ins, outs, sems)
        comm.mid(ins, outs, sems)
        comm.last(ins, outs, sems)

    return _pcall(body, name=name, out_shape=comm.outs, in_specs=[ANY] * ni, out_specs=[ANY] * no,
                  input_output_aliases=comm.aliases, scratch_shapes=comm.sems, compiler_params=_params())(*comm.ins)


def gather_comm(bufs):
    n = len(bufs)

    def half(out, w, which):
        pr = out[w].shape[1] // 2
        return pl.ds(pl.multiple_of(which * pr, BF16_ROWS), pr)

    def first(ins, out, sems):
        isend, irecv, _, _ = sems
        x, y, c, chips = _place()
        for w in range(n):
            mine = out[w].at[2 * x + y, half(out, w, c)]
            for j, (cx, cy) in enumerate(chips):
                _remote(mine, mine, isend.at[3 * w + j], irecv.at[3 * w + j], (cx, cy, c)).start()

    def mid(ins, out, sems):
        isend, irecv, dsend, drecv = sems
        x, y, c, chips = _place()
        sib = (x, y, 1 - c)
        for w in range(n):
            for j, (cx, cy) in enumerate(chips):
                landed = out[w].at[2 * cx + cy, half(out, w, c)]
                _remote(landed, landed, isend.at[3 * w + j], irecv.at[3 * w + j], sib).wait_recv()
                _remote(landed, landed, dsend.at[3 * w + j], drecv.at[3 * w + j], sib).start()

    def last(ins, out, sems):
        isend, irecv, dsend, drecv = sems
        x, y, c, chips = _place()
        sib = (x, y, 1 - c)
        for w in range(n):
            for j, (cx, cy) in enumerate(chips):
                landed = out[w].at[2 * cx + cy, half(out, w, 1 - c)]
                _remote(landed, landed, dsend.at[3 * w + j], drecv.at[3 * w + j], sib).wait_recv()
        for w in range(n):
            sent = out[w].at[0, half(out, w, c)]
            for j in range(3):
                _remote(sent, sent, isend.at[3 * w + j], irecv.at[3 * w + j], sib).wait_send()
                _remote(sent, sent, dsend.at[3 * w + j], drecv.at[3 * w + j], sib).wait_send()

    return Comm(bufs, [jax.ShapeDtypeStruct(s.shape, s.dtype) for s in bufs], {w: w for w in range(n)},
                [pltpu.SemaphoreType.DMA((3 * n,))] * 4, first, mid, last)


def _nothing(ins, outs, sems):
    return None


def join_comms(a, b):
    ni, no, ns = len(a.ins), len(a.outs), len(a.sems)

    def both(f, g):
        def hook(ins, outs, sems):
            f(ins[:ni], outs[:no], sems[:ns])
            g(ins[ni:], outs[no:], sems[ns:])
        return hook

    aliases = dict(a.aliases)
    aliases.update({ni + k: no + v for k, v in b.aliases.items()})
    return Comm(a.ins + b.ins, a.outs + b.outs, aliases, a.sems + b.sems,
                both(a.first, b.first), both(a.mid, b.mid), both(a.last, b.last))


def exchange_comm(pieces):
    n = len(pieces)

    def copies(src, out, sems):
        x, y, c, _ = _place()
        return [_remote(src[w].at[k, 1 - c], out[w].at[k], sems[0].at[N_CHIPS * w + k], sems[1].at[N_CHIPS * w + k],
                        (x, y, 1 - c)) for w in range(n) for k in range(N_CHIPS)]

    def first(src, out, sems):
        for cp in copies(src, out, sems):
            cp.start()

    def last(src, out, sems):
        for cp in copies(src, out, sems):
            cp.wait()

    return Comm(pieces, [jax.ShapeDtypeStruct((N_CHIPS,) + s.shape[2:], s.dtype) for s in pieces], {},
                [pltpu.SemaphoreType.DMA((N_CHIPS * n,))] * 2, first, _nothing, last)


def scatter_comm(parts):
    n = len(parts)

    def copies(src, out, sems):
        x, y, c, chips = _place()
        return [_remote(src[w].at[2 * cx + cy], out[w].at[j], sems[0].at[3 * w + j], sems[1].at[3 * w + j], (cx, cy, c))
                for w in range(n) for j, (cx, cy) in enumerate(chips)]

    def first(src, out, sems):
        for cp in copies(src, out, sems):
            cp.start()

    def last(src, out, sems):
        for cp in copies(src, out, sems):
            cp.wait()

    return Comm(parts, [jax.ShapeDtypeStruct((3,) + s.shape[1:], s.dtype) for s in parts], {},
                [pltpu.SemaphoreType.DMA((3 * n,))] * 2, first, _nothing, last)


def share_comm(halves):
    n = len(halves)

    def first(ins, buf, sems):
        x, y, c, _ = _place()
        for w in range(n):
            _remote(buf[w].at[c], buf[w].at[c], sems[0].at[w], sems[1].at[w], (x, y, 1 - c)).start()

    def last(ins, buf, sems):
        x, y, c, _ = _place()
        for w in range(n):
            landed = buf[w].at[1 - c]
            _remote(landed, landed, sems[0].at[w], sems[1].at[w], (x, y, 1 - c)).wait_recv()
        for w in range(n):
            _remote(buf[w].at[c], buf[w].at[c], sems[0].at[w], sems[1].at[w], (x, y, 1 - c)).wait_send()

    return Comm(halves, [jax.ShapeDtypeStruct(s.shape, s.dtype) for s in halves], {w: w for w in range(n)},
                [pltpu.SemaphoreType.DMA((n,))] * 2, first, _nothing, last)


def gather_small(name, blk, reduce):
    r, cdim = blk.shape

    def body(in_ref, out_ref, *rest):
        if reduce:
            buf, send_sem, recv_sem = rest
        else:
            buf = out_ref
            send_sem, recv_sem = rest
        x, y, c, _ = _place()
        me = 4 * x + 2 * y + c
        buf[me] = in_ref[...]
        peers = []
        for dx in range(2):
            for dy in range(2):
                for dc in range(2):
                    if dx or dy or dc:
                        peers.append((dx, dy, dc))
        copies = []
        for s, (dx, dy, dc) in enumerate(peers):
            cp = _remote(in_ref, buf.at[me], send_sem.at[s], recv_sem.at[s],
                         ((1 - x if dx else x), (1 - y if dy else y), (1 - c if dc else c)))
            cp.start()
            copies.append(cp)
        for s, (dx, dy, dc) in enumerate(peers):
            px, py, pc_ = (1 - x if dx else x), (1 - y if dy else y), (1 - c if dc else c)
            landed = buf.at[4 * px + 2 * py + pc_]
            _remote(landed, landed, send_sem.at[s], recv_sem.at[s], (x, y, c)).wait_recv()
        for cp in copies:
            cp.wait_send()
        if reduce:
            tot = buf[0]
            for s in range(1, N_DEV):
                tot = tot + buf[s]
            out_ref[...] = tot

    vm = pl.BlockSpec(memory_space=pltpu.VMEM)
    out_shape = jax.ShapeDtypeStruct((r, cdim) if reduce else (N_DEV, r, cdim), F32)
    scratch = ([pltpu.VMEM((N_DEV, r, cdim), F32)] if reduce else []) + [pltpu.SemaphoreType.DMA((N_DEV - 1,))] * 2
    return _pcall(body, name=name, out_shape=out_shape, in_specs=[vm], out_specs=vm, scratch_shapes=scratch,
                  compiler_params=_params())(blk)


def _job_tiles(shapes, tile_bytes, mult):
    tiles = []
    for rows, cols in shapes:
        tr = _tile(rows, max(mult, tile_bytes // (4 * cols)), mult)
        tiles.append((tr, rows // tr))
    return tiles, max(n for _, n in tiles)


def sum_cores(name, owns, gots, place):
    nj = len(owns)
    tiles, _ = _job_tiles([o.shape[2:] for o in owns], 1 << 21, BF16_ROWS)
    steps = max(N_CHIPS * n for _, n in tiles)

    def body(place_ref, *refs):
        i = pl.program_id(0)
        for k, (_, n) in enumerate(tiles):
            @pl.when(i < N_CHIPS * n)
            def _(own_ref=refs[2 * k], got_ref=refs[2 * k + 1], o_ref=refs[2 * nj + k]):
                o_ref[...] = (own_ref[...].astype(F32) + got_ref[...].astype(F32)).astype(o_ref.dtype)

    in_specs, out_specs, out_shape, args = [], [], [], []
    for own, got, (tr, n) in zip(owns, gots, tiles):
        pc = own.shape[3]
        last = N_CHIPS * n - 1
        in_specs += [pl.BlockSpec((None, None, tr, pc),
                                  lambda i, s, n=n, last=last: (jnp.minimum(i, last) // n, s[1], jnp.minimum(i, last) % n, 0)),
                     pl.BlockSpec((None, tr, pc),
                                  lambda i, s, n=n, last=last: (jnp.minimum(i, last) // n, jnp.minimum(i, last) % n, 0))]
        out_specs.append(pl.BlockSpec((None, tr, pc),
                                      lambda i, s, n=n, last=last: (jnp.minimum(i, last) // n, jnp.minimum(i, last) % n, 0)))
        out_shape.append(jax.ShapeDtypeStruct(got.shape, BF16))
        args += [own, got]
    spec = pltpu.PrefetchScalarGridSpec(num_scalar_prefetch=1, grid=(steps,), in_specs=in_specs, out_specs=out_specs)
    return _pcall(body, name=name, out_shape=out_shape, grid_spec=spec,
                  compiler_params=_params(("arbitrary",)))(place, *args)


def sum_chips(name, parts, gots, place):
    nj = len(parts)
    tiles, steps = _job_tiles([p.shape[1:] for p in parts], 1 << 20, BF16_ROWS)

    def body(place_ref, *refs):
        i = pl.program_id(0)
        for k, (_, n) in enumerate(tiles):
            @pl.when(i < n)
            def _(part_ref=refs[2 * k], got_ref=refs[2 * k + 1], o_ref=refs[2 * nj + k]):
                tot = part_ref[...].astype(F32)
                for j in range(3):
                    tot = tot + got_ref[j].astype(F32)
                o_ref[...] = tot

    in_specs, out_specs, out_shape, args = [], [], [], []
    for part, got, (tr, n) in zip(parts, gots, tiles):
        pc = part.shape[2]
        in_specs += [pl.BlockSpec((None, tr, pc), lambda i, s, n=n: (s[0], jnp.minimum(i, n - 1), 0)),
                     pl.BlockSpec((3, tr, pc), lambda i, s, n=n: (0, jnp.minimum(i, n - 1), 0))]
        out_specs.append(pl.BlockSpec((None, tr, pc), lambda i, s, n=n: (s[1], jnp.minimum(i, n - 1), 0)))
        out_shape.append(jax.ShapeDtypeStruct((2,) + part.shape[1:], F32))
        args += [part, got]
    spec = pltpu.PrefetchScalarGridSpec(num_scalar_prefetch=1, grid=(steps,), in_specs=in_specs, out_specs=out_specs)
    return _pcall(body, name=name, out_shape=out_shape, grid_spec=spec,
                  compiler_params=_params(("arbitrary",)))(place, *args)


def adamw(name, jobs):
    c1 = 1.0 / (1.0 - ADAM_B1 ** ADAM_STEP)
    c2 = 1.0 / (1.0 - ADAM_B2 ** ADAM_STEP)
    nj = len(jobs)
    tiles, steps = _job_tiles([j[0].shape for j in jobs], 1 << 18, 8)

    def body(*refs):
        i = pl.program_id(0)
        for k, (_, n) in enumerate(tiles):
            w_ref, g_ref, m_ref, v_ref = refs[4 * k:4 * k + 4]
            d_ref, nm_ref, nv_ref = refs[4 * nj + 3 * k:4 * nj + 3 * k + 3]

            @pl.when(i < n)
            def _(w_ref=w_ref, g_ref=g_ref, m_ref=m_ref, v_ref=v_ref, d_ref=d_ref, nm_ref=nm_ref, nv_ref=nv_ref):
                gv = g_ref[...]
                nm = ADAM_B1 * m_ref[...] + (1.0 - ADAM_B1) * gv
                nv = ADAM_B2 * v_ref[...] + (1.0 - ADAM_B2) * (gv * gv)
                nm_ref[...] = nm
                nv_ref[...] = nv
                d_ref[...] = -ADAM_LR * ((nm * c1) / (jnp.sqrt(nv * c2) + ADAM_EPS) + ADAM_WD * w_ref[...])

    in_specs, out_specs, out_shape, args = [], [], [], []
    for (w, g, m, v), (tr, n) in zip(jobs, tiles):
        spec = pl.BlockSpec((tr, w.shape[1]), lambda i, n=n: (jnp.minimum(i, n - 1), 0))
        in_specs += [spec] * 4
        out_specs += [spec] * 3
        out_shape += [jax.ShapeDtypeStruct(w.shape, F32)] * 3
        args += [w, g, m, v]
    res = _pcall(body, name=name, out_shape=out_shape, grid=(steps,), in_specs=in_specs, out_specs=out_specs,
                 compiler_params=_params(("arbitrary",)))(*args)
    return [tuple(res[3 * k:3 * k + 3]) for k in range(nj)]


MATS = ["ffn1_w_in", "ffn1_w_out", "w_mix_in", "w_conv_out", "w_attn_out", "w_mix_out", "ffn2_w_in", "ffn2_w_out",
        "w_ple_gate", "w_ple_proj"]
COL_SHARDED = {"ffn1_w_in", "w_mix_in", "ffn2_w_in", "w_ple_proj"}
NORMS = ["ffn1_norm", "mix_norm", "ffn2_norm", "ple_norm", "final_norm"]
WEIGHTS = ["ffn1_norm", "ffn1_w_in", "ffn1_w_out", "mix_norm", "w_mix_in", "conv_w", "w_conv_out", "w_attn_out",
           "w_mix_out", "ffn2_norm", "ffn2_w_in", "ffn2_w_out", "ple_norm", "w_ple_gate", "w_ple_proj", "final_norm"]


def _pad_rows(a, rows):
    return jnp.concatenate([a, jnp.zeros((rows - a.shape[0],) + a.shape[1:], a.dtype)], axis=0)


def _step(x, p, tgt, w, m, v):
    t, d = x.shape
    tt = _tile(t, 256)
    tm = _tile(t, 512)
    tm2 = _tile(t, 1024)
    tq = _tile(t, 1024)

    chip = 2 * lax.axis_index("x") + lax.axis_index("y")
    place = jnp.stack([chip, lax.axis_index("c")]).astype(jnp.int32)

    placed = dict(zip(MATS, place_shards("place_shards", [w[k] for k in MATS], place)))
    full = {}

    def keep(names, bufs):
        for k, buf in zip(names, bufs):
            full[k] = buf if k in COL_SHARDED else buf.reshape(-1, buf.shape[2])

    def gather_of(names):
        return gather_comm([placed[k] for k in names])

    cw_all = gather_small("gather_conv_w", _pad_rows(w["conv_w"], 8), False)
    cw8 = jnp.concatenate([cw_all[2 * k] for k in range(N_CHIPS)], axis=1)
    g1, gm, g2, gp, gf = (w[k].reshape(1, d) for k in NORMS)

    def ffn_fwd(tag, h, g, first, w_in_name, w_out_name, riders):
        if first:
            n, bufs = rms_fwd(tag + "_norm", h, g, tt, comm=gather_of(first))
            keep(first, bufs)
            (a, s), bufs = ffn_in_act(tag + "_in", n, full[w_in_name], tm, comm=gather_of(riders))
            keep(riders, bufs)
        else:
            a, s, n = ffn_in_act(tag + "_in", h, full[w_in_name], tm, gain=g)
        return n, a, s, mm_nn(tag + "_out", s, full[w_out_name], F32, tm, res=h, alpha=0.5)

    n1, a1, s1, h1 = ffn_fwd("ffn1", x, g1, ["ffn1_w_in"], "ffn1_w_in", "ffn1_w_out", ["ffn1_w_out", "w_mix_in"])
    wmix = full["w_mix_in"]
    riders = [["w_conv_out", "w_attn_out", "w_mix_out"], ["ffn2_w_in"], ["ffn2_w_out", "w_ple_gate", "w_ple_proj"]]
    (cbx, u), bufs = mm_nn_stacked("mix_in_conv", h1, wmix, BF16, tm2, d, 0, 3, comm=gather_of(riders[0]), gain=gm)
    keep(riders[0], bufs)
    qkv, bufs = mm_nn_stacked("mix_in_qkv", u, wmix, BF16, tm2, d, 3, 3, comm=gather_of(riders[1]))
    keep(riders[1], bufs)
    gates, bufs = mm_nn_stacked("mix_in_gates", u, wmix, BF16, tm2, d, 6, 2, comm=gather_of(riders[2]))
    keep(riders[2], bufs)
    wpp = full["w_ple_proj"]
    wpp = jnp.transpose(wpp, (1, 0, 2)).reshape(wpp.shape[1], -1)
    ycin, y_conv = conv_out_fwd("conv_out", cbx, cw8, full["w_conv_out"], tt)
    o = attn_fwd("attn", qkv, tq)
    y_attn = mm_nn("attn_out", o, full["w_attn_out"], BF16, tm)
    merged, h2 = mix_out_fwd("mix_out", gates, y_conv, y_attn, h1, full["w_mix_out"], tm)
    n2, a2, s2, h3 = ffn_fwd("ffn2", h2, g2, [], "ffn2_w_in", "ffn2_w_out", [])

    pieces, chip_sums, halves = {}, {}, {}

    def as_pieces(k):
        pc = pieces[k]
        return pc if k in COL_SHARDED else pc.reshape(N_CHIPS, 2, pc.shape[0] // (2 * N_CHIPS), pc.shape[1])

    def sum_siblings(tag, names):
        pcs = [as_pieces(k) for k in names]
        got = run_comm("exchange_" + tag, exchange_comm(pcs))
        chip_sums.update(zip(names, sum_cores("sum_cores_" + tag, pcs, got, place)))

    def scatter_of(names):
        return scatter_comm([chip_sums[k] for k in names])

    def sum_landed(tag, names, landed):
        halves.update(zip(names, sum_chips("sum_chips_" + tag, [chip_sums[k] for k in names], landed, place)))

    npl, dh4, dpp, dzg, dgf, loss_row = tail("tail", h3, p, tgt, gp, gf, full["w_ple_gate"], wpp, tm)
    dwpp = mm_tn_whole("ple_proj_dw", p, dpp, tm2)
    pieces["w_ple_proj"] = jnp.transpose(dwpp.reshape(2, p.shape[1] // 2, N_CHIPS, d // N_CHIPS), (2, 0, 1, 3))
    pieces["w_ple_gate"] = mm_tn_rows("ple_gate_dw", npl, dzg, tm2)
    dh3, df2, dgp = mm_nt("ple_gate_dx", dzg, full["w_ple_gate"], F32, tm, d, norm=(h3, gp, dh4), alpha=0.5)
    w_in, w_out = full["ffn2_w_in"], full["ffn2_w_out"]
    pieces["ffn2_w_out"] = mm_tn_rows("ffn2_dwout", s2, df2, tm2)
    da2 = ffn_ds_dact("ffn2_ds", df2, w_out, a2, tm2)
    pieces["ffn2_w_in"] = mm_tn_cols("ffn2_dwin", n2, da2, tm2)
    dh2, dh2b, dg2 = mm_nt_stacked("ffn2_dn", da2, w_in, tm2, (h2, g2, dh3))
    pieces["w_mix_out"] = mm_tn_rows("mix_out_dw", merged, dh2b, tm2)
    dyc, dya, dgates = mix_out_bwd("mix_out_dx", dh2b, full["w_mix_out"], gates, y_conv, y_attn, tm)
    pieces["w_conv_out"] = mm_tn_rows("conv_out_dw", ycin, dyc, tm2)
    dcbx, dcw8 = conv_out_bwd("conv_out_dx", dyc, full["w_conv_out"], cbx, cw8, tt)
    pieces["w_attn_out"] = mm_tn_rows("attn_out_dw", o, dya, tm2)
    do = mm_nt("attn_out_dx", dya, full["w_attn_out"], BF16, tm, d)
    dq, dk, dv = attn_bwd("attn_bwd", qkv, do, tq)
    dmix = [dcbx, dq, dk, dv, dgates]
    early = ["ffn2_w_in", "ffn2_w_out", "w_ple_gate", "w_ple_proj", "w_mix_out", "w_conv_out", "w_attn_out"]
    swap = exchange_comm([as_pieces(k) for k in early])
    pieces["w_mix_in"], got = mm_tn_parts("mix_in_dw", u, dmix, tm2, comm=swap)
    chip_sums.update(zip(early, sum_cores("sum_cores_early", swap.ins, got, place)))
    swap = exchange_comm([as_pieces("w_mix_in")])
    (dh1, df1, dgm), landed = mm_nt_parts("mix_in_dx", dmix, wmix, tm2, (h1, gm, dh2), 0.5,
                                          comm=join_comms(scatter_of(early), swap))
    sum_landed("early", early, landed[:len(early)])
    chip_sums["w_mix_in"] = sum_cores("sum_cores_mix", swap.ins, landed[len(early):], place)[0]
    w_in, w_out = full["ffn1_w_in"], full["ffn1_w_out"]
    pieces["ffn1_w_out"] = mm_tn_rows("ffn1_dwout", s1, df1, tm2)
    da1 = ffn_ds_dact("ffn1_ds", df1, w_out, a1, tm2)
    pieces["ffn1_w_in"], landed = mm_tn_cols("ffn1_dwin", n1, da1, tm2, comm=scatter_of(["w_mix_in"]))
    sum_landed("mix", ["w_mix_in"], landed)
    late = ["ffn1_w_in", "ffn1_w_out"]
    sum_siblings("late", late)
    done = early + ["w_mix_in"]
    (dx, _, dg1), landed = mm_nt_stacked(
        "ffn1_dn", da1, w_in, tm2, (x, g1, dh1),
        comm=join_comms(scatter_of(late), share_comm([halves[k] for k in done])))
    sum_landed("late", late, landed[:len(late)])
    shared = dict(zip(done, landed[len(late):]))

    shared.update(zip(late, run_comm("share_halves", share_comm([halves[k] for k in late]))))
    grad, delta, new_m, new_v = {}, {}, {}, {}
    for k in MATS:
        grad[k] = shared[k].reshape(w[k].shape)

    small = jnp.concatenate([dg1, dgm, dg2, dgp, dgf, dcw8[:3], loss_row, jnp.zeros((7, d), F32)], axis=0)
    tot = gather_small("sum_small", small, True)
    loss = tot[8, 0]
    norm_w = jnp.concatenate([w[k].reshape(1, d) for k in NORMS] + [jnp.zeros((3, d), F32)], axis=0)
    norm_m = jnp.concatenate([m[k].reshape(1, d) for k in NORMS] + [jnp.zeros((3, d), F32)], axis=0)
    norm_v = jnp.concatenate([v[k].reshape(1, d) for k in NORMS] + [jnp.ones((3, d), F32)], axis=0)
    norm_g = jnp.concatenate([tot[0:5], jnp.zeros((3, d), F32)], axis=0)
    cs = d // N_CHIPS
    gcw = lax.dynamic_slice(tot[5:8], (0, chip * cs), (3, cs))
    conv_job = (_pad_rows(w["conv_w"], 8), _pad_rows(gcw, 8), _pad_rows(m["conv_w"], 8),
                jnp.concatenate([v["conv_w"], jnp.ones((5, cs), F32)], axis=0))

    steps = adamw("adamw", [(w[k], grad[k], m[k], v[k]) for k in MATS]
                  + [(norm_w, norm_g, norm_m, norm_v), conv_job])
    for k, res in zip(MATS, steps):
        delta[k], new_m[k], new_v[k] = res
    nd, nm, nv = steps[len(MATS)]
    for r, k in enumerate(NORMS):
        grad[k] = norm_g[r].reshape(w[k].shape)
        delta[k], new_m[k], new_v[k] = (a[r].reshape(w[k].shape) for a in (nd, nm, nv))
    cd, cm, cv = steps[len(MATS) + 1]
    grad["conv_w"], delta["conv_w"], new_m["conv_w"], new_v["conv_w"] = gcw, cd[:3], cm[:3], cv[:3]
    return loss, dx, grad, delta, new_m, new_v


def kernel(x, p, ffn1_norm, ffn1_w_in, ffn1_w_out, mix_norm, w_mix_in, conv_w, w_conv_out, w_attn_out, w_mix_out, ffn2_norm, ffn2_w_in, ffn2_w_out, ple_norm, w_ple_gate, w_ple_proj, final_norm, loss_target, m_ffn1_norm, m_ffn1_w_in, m_ffn1_w_out, m_mix_norm, m_w_mix_in, m_conv_w, m_w_conv_out, m_w_attn_out, m_w_mix_out, m_ffn2_norm, m_ffn2_w_in, m_ffn2_w_out, m_ple_norm, m_w_ple_gate, m_w_ple_proj, m_final_norm, v_ffn1_norm, v_ffn1_w_in, v_ffn1_w_out, v_mix_norm, v_w_mix_in, v_conv_w, v_w_conv_out, v_w_attn_out, v_w_mix_out, v_ffn2_norm, v_ffn2_w_in, v_ffn2_w_out, v_ple_norm, v_w_ple_gate, v_w_ple_proj, v_final_norm):
    ws = (ffn1_norm, ffn1_w_in, ffn1_w_out, mix_norm, w_mix_in, conv_w, w_conv_out, w_attn_out, w_mix_out, ffn2_norm,
          ffn2_w_in, ffn2_w_out, ple_norm, w_ple_gate, w_ple_proj, final_norm)
    ms = (m_ffn1_norm, m_ffn1_w_in, m_ffn1_w_out, m_mix_norm, m_w_mix_in, m_conv_w, m_w_conv_out, m_w_attn_out,
          m_w_mix_out, m_ffn2_norm, m_ffn2_w_in, m_ffn2_w_out, m_ple_norm, m_w_ple_gate, m_w_ple_proj, m_final_norm)
    vs = (v_ffn1_norm, v_ffn1_w_in, v_ffn1_w_out, v_mix_norm, v_w_mix_in, v_conv_w, v_w_conv_out, v_w_attn_out,
          v_w_mix_out, v_ffn2_norm, v_ffn2_w_in, v_ffn2_w_out, v_ple_norm, v_w_ple_gate, v_w_ple_proj, v_final_norm)
    assert x.shape[0] == 1 and p.shape[:2] == (1, 1), "one sequence and one layer per device"

    def strip(a):
        return a[0] if a.ndim == 3 or (a.ndim == 2 and a.shape[0] == 1) else a

    w = {k: strip(a) for k, a in zip(WEIGHTS, ws)}
    m = {k: strip(a) for k, a in zip(WEIGHTS, ms)}
    v = {k: strip(a) for k, a in zip(WEIGHTS, vs)}
    loss, dx, grad, delta, new_m, new_v = _step(x[0], p[0, 0], loss_target[0], w, m, v)
    shapes = [a.shape for a in ws]
    outs = [loss, dx[None]]
    for res in (grad, delta, new_m, new_v):
        outs += [res[k].reshape(s) for k, s in zip(WEIGHTS, shapes)]
    return tuple(outs)
```

```python
import functools
import math

import jax
import jax.numpy as jnp
from jax import lax
from jax.experimental import pallas as pl
from jax.experimental.pallas import tpu as pltpu

F32 = jnp.float32
BF16 = jnp.bfloat16
MESH = pl.DeviceIdType.MESH
ANY = pl.BlockSpec(memory_space=pl.ANY)

HEAD_DIM = 128
NORM_EPS = 1e-6
N_CHIPS = 4
N_DEV = 8
BF16_ROWS = 16
VMEM_LIMIT = 56 * 1024 * 1024
ACC_BYTES = 8 * 1024 * 1024
STICK_EXIT = 110.0

ADAM_LR = 0.001
ADAM_B1 = 0.9
ADAM_B2 = 0.999
ADAM_EPS = 1e-08
ADAM_WD = 0.01
ADAM_STEP = 10

NN = (((1,), (0,)), ((), ()))
NT = (((1,), (1,)), ((), ()))
TN = (((0,), (0,)), ((), ()))


def _params(sem=None, **kw):
    if sem is not None:
        kw["dimension_semantics"] = sem
    return pltpu.CompilerParams(vmem_limit_bytes=VMEM_LIMIT, **kw)


def _pcall(body, **kw):
    return pl.pallas_call(body, **kw)


def _tile(n, pref, mult=8):
    best = None
    for d in range(mult, min(n, pref) + 1, mult):
        if n % d == 0:
            best = d
    return best if best is not None else n


def _dot(a, b, dims):
    return lax.dot_general(a, b, dims, preferred_element_type=F32)


def _call(name, body, grid, in_specs, out_specs, out_shape, args, scratch=(), sem=None, comm=None):
    n_in, n_out, n_sc = len(in_specs), len(out_specs), len(scratch)
    if comm is None:
        def plain(*refs):
            body(refs[:n_in], refs[n_in:n_in + n_out], refs[n_in + n_out:])

        return _pcall(plain, name=name, out_shape=list(out_shape), grid=grid, in_specs=list(in_specs),
                      out_specs=list(out_specs), scratch_shapes=list(scratch), compiler_params=_params(sem))(*args)
    n_cin, n_cout = len(comm.ins), len(comm.outs)
    steps = math.prod(grid)

    def hosted(*refs):
        ins, c_ins = refs[:n_in], refs[n_in:n_in + n_cin]
        outs = refs[n_in + n_cin:n_in + n_cin + n_out]
        c_outs = refs[n_in + n_cin + n_out:n_in + n_cin + n_out + n_cout]
        rest = refs[n_in + n_cin + n_out + n_cout:]
        sems = rest[n_sc:]
        step = pl.program_id(0)
        for ax in range(1, len(grid)):
            step = step * grid[ax] + pl.program_id(ax)

        @pl.when(step == 0)
        def _():
            comm.first(c_ins, c_outs, sems)

        body(ins, outs, rest[:n_sc])

        @pl.when(step == (3 * steps) // 4)
        def _():
            comm.mid(c_ins, c_outs, sems)

        @pl.when(step == steps - 1)
        def _():
            comm.last(c_ins, c_outs, sems)

    res = _pcall(hosted, name=name, out_shape=list(out_shape) + comm.outs, grid=grid,
                 in_specs=list(in_specs) + [ANY] * n_cin, out_specs=list(out_specs) + [ANY] * n_cout,
                 input_output_aliases={n_in + k: n_out + v for k, v in comm.aliases.items()},
                 scratch_shapes=list(scratch) + comm.sems,
                 compiler_params=_params(("arbitrary",) * len(grid)))(*args, *comm.ins)
    return list(res[:n_out]), list(res[n_out:])


NORM_CHUNK = 256


def _norm_bwd_tile(read_dn, rows, first, h_ref, g_ref, dr_ref, dh_ref, dhb_ref, dg_ref, alpha):
    @pl.when(first)
    def _():
        dg_ref[...] = jnp.zeros_like(dg_ref)

    gv = g_ref[...]
    tot = jnp.zeros_like(gv)
    for c0 in range(0, rows, NORM_CHUNK):
        sl = slice(c0, min(rows, c0 + NORM_CHUNK))
        hv = h_ref[sl, :]
        rs = _rstd(hv)
        hn = hv * rs
        dnv = read_dn(sl)
        gy = dnv * gv
        dh = dr_ref[sl, :] + rs * (gy - hn * jnp.mean(gy * hn, axis=-1, keepdims=True))
        dh_ref[sl, :] = dh
        dhb_ref[sl, :] = (alpha * dh).astype(BF16)
        tot = tot + jnp.sum(dnv * hn, axis=0, keepdims=True)
    dg_ref[...] += tot


def _mm(name, a, b, out_sds, grid, a_spec, b_spec, o_spec, dims, acc_shape, res=None, alpha=1.0, comm=None,
        norm=None, gain=None):
    nk = grid[2]

    def body(ins, outs, scratch):
        a_ref, b_ref = ins[:2]
        r_ref = ins[2] if res is not None else None
        o_ref = outs[0]
        if gain is not None:
            n_ref = scratch[-1]

            @pl.when(jnp.logical_and(pl.program_id(1) == 0, pl.program_id(2) == 0))
            def _():
                hv = a_ref[...]
                n_ref[...] = (hv * _rstd(hv) * ins[-1][...]).astype(BF16)
                outs[-1][...] = n_ref[...]

            a_ref = n_ref

        def finish(read):
            if norm is not None:
                first = jnp.logical_and(pl.program_id(0) == 0, pl.program_id(1) == 0)
                _norm_bwd_tile(read, o_ref.shape[0], first, *ins[2:5], *outs, alpha)
                return
            r = read(slice(None))
            if alpha != 1.0:
                r = r * alpha
            if r_ref is not None:
                r = r_ref[...] + r
            if len(o_ref.shape) == 3:
                half = o_ref.shape[1]
                o_ref[0] = r[:half].astype(o_ref.dtype)
                o_ref[1] = r[half:].astype(o_ref.dtype)
            else:
                o_ref[...] = r.astype(o_ref.dtype)

        if nk == 1:
            part = _dot(a_ref[...].astype(BF16), b_ref[...].astype(BF16), dims)
            finish(lambda sl: part[sl])
        else:
            acc_ref = scratch[0]
            kk = pl.program_id(2)

            @pl.when(kk == 0)
            def _():
                acc_ref[...] = jnp.zeros_like(acc_ref)

            acc_ref[...] += _dot(a_ref[...].astype(BF16), b_ref[...].astype(BF16), dims)

            @pl.when(kk == nk - 1)
            def _():
                finish(lambda sl: acc_ref[sl, :])

    in_specs = [a_spec, b_spec]
    args = [a, b]
    out_specs, out_shape = [o_spec], [out_sds]
    sem = ("parallel", "parallel", "arbitrary")
    if res is not None:
        in_specs.append(o_spec)
        args.append(res)
    if norm is not None:
        width = out_sds.shape[1]
        whole = pl.BlockSpec((1, width), lambda i, j, r: (0, 0))
        in_specs += [o_spec, whole, o_spec]
        args += list(norm)
        out_specs = [o_spec, o_spec, whole]
        out_shape = [jax.ShapeDtypeStruct(out_sds.shape, F32), jax.ShapeDtypeStruct(out_sds.shape, BF16),
                     jax.ShapeDtypeStruct((1, width), F32)]
        sem = ("arbitrary", "arbitrary", "arbitrary")
    scratch = [] if nk == 1 else [pltpu.VMEM(acc_shape, F32)]
    if gain is not None:
        in_specs.append(pl.BlockSpec((1, a.shape[1]), lambda i, j, r: (0, 0)))
        args.append(gain)
        out_specs.append(a_spec)
        out_shape.append(jax.ShapeDtypeStruct(a.shape, BF16))
        scratch.append(pltpu.VMEM(a_spec.block_shape, BF16))
        sem = ("parallel", "arbitrary", "arbitrary")
    got = _call(name, body, grid, in_specs, out_specs, out_shape, args, scratch, sem, comm)
    if norm is not None or gain is not None:
        return got if comm is None else (got[0], got[1])
    return got[0] if comm is None else (got[0][0], got[1])


def ffn_in_act(name, n, w4, tm, comm=None, gain=None):
    t, d = n.shape
    cs = w4.shape[2]

    def body(ins, outs, scratch):
        wg_ref, wu_ref = ins[-2:]
        a_ref, s_ref = outs[:2]
        if gain is None:
            nv = ins[0][...]
        else:
            hv = ins[0][...]
            nv = (hv * _rstd(hv) * ins[1][...]).astype(BF16)

            @pl.when(pl.program_id(0) == 0)
            def _():
                outs[2][...] = nv
        gate = _dot(nv, wg_ref[...], NN)
        up = _dot(nv, wu_ref[...], NN)
        a_ref[0] = gate.astype(BF16)
        a_ref[1] = up.astype(BF16)
        s_ref[...] = (gate * jax.nn.sigmoid(gate) * up).astype(BF16)

    rows = pl.BlockSpec((tm, d), lambda j, i: (i, 0))
    in_specs = [rows] + ([] if gain is None else [pl.BlockSpec((1, d), lambda j, i: (0, 0))])
    in_specs += [pl.BlockSpec((None, d, cs), lambda j, i: (j, 0, 0)),
                 pl.BlockSpec((None, d, cs), lambda j, i: (2 + j, 0, 0))]
    out_specs = [pl.BlockSpec((2, tm, cs), lambda j, i: (0, i, j)), pl.BlockSpec((tm, cs), lambda j, i: (i, j))]
    out_shape = [jax.ShapeDtypeStruct((2, t, 2 * cs), BF16), jax.ShapeDtypeStruct((t, 2 * cs), BF16)]
    if gain is not None:
        out_specs.append(pl.BlockSpec((tm, d), lambda j, i: (jnp.where(j == 0, i, t // tm - 1), 0)))
        out_shape.append(jax.ShapeDtypeStruct((t, d), BF16))
    got = _call(name, body, (2, t // tm), in_specs, out_specs, out_shape,
                [n] + ([] if gain is None else [gain]) + [w4, w4], (), ("arbitrary", "arbitrary"), comm)
    return got if comm is None else (got[0], got[1])


def ffn_ds_dact(name, df, w_out, a3, tm):
    t, d = df.shape
    f = w_out.shape[0]
    cs = f // 2

    def body(ins, outs, scratch):
        df_ref, w_ref, a_ref = ins
        ds = _dot(df_ref[...], w_ref[...], NT)
        for c0 in range(0, tm, NORM_CHUNK):
            sl = slice(c0, min(tm, c0 + NORM_CHUNK))
            gate = a_ref[0, sl, :].astype(F32)
            up = a_ref[1, sl, :].astype(F32)
            sg = jax.nn.sigmoid(gate)
            outs[0][0, sl, :] = (ds[sl] * up * sg * (1.0 + gate * (1.0 - sg))).astype(BF16)
            outs[0][1, sl, :] = (ds[sl] * gate * sg).astype(BF16)

    blk = pl.BlockSpec((2, tm, cs), lambda i, j: (0, i, j))
    return _call(name, body, (t // tm, 2),
                 [pl.BlockSpec((tm, d), lambda i, j: (i, 0)), pl.BlockSpec((cs, d), lambda i, j: (j, 0)), blk],
                 [blk], [jax.ShapeDtypeStruct((2, t, f), BF16)], [df, w_out, a3], (), ("parallel", "parallel"))[0]


def _part_ranges(parts, d):
    out, lo = [], 0
    for p in parts:
        out.append((lo, p.shape[1] // d))
        lo += p.shape[1] // d
    return out, lo


def mm_nt_parts(name, parts, w4, tm, norm, alpha, comm=None):
    m = parts[0].shape[0]
    d, cs = w4.shape[1], w4.shape[2]
    per = cs // d
    ranges, nblk = _part_ranges(parts, d)
    np_ = len(parts)
    nt = m // tm
    chunk = tm // nblk

    def body(ins, outs, scratch):
        w_ref, acc = ins[np_], scratch[0]
        i, r = pl.program_id(0), pl.program_id(1)

        @pl.when(jnp.logical_and(i < nt, r == 0))
        def _():
            acc[i % 2] = jnp.zeros(acc.shape[1:], F32)

        for (lo, n), a_ref in zip(ranges, ins[:np_]):
            @pl.when(jnp.logical_and(i < nt, jnp.logical_and(r >= lo, r < lo + n)))
            def _(a_ref=a_ref):
                acc[i % 2] += _dot(a_ref[...], w_ref[...], NT)

        @pl.when(i > 0)
        def _():
            rows = pl.ds(pl.multiple_of(r * chunk, chunk), chunk)
            first = jnp.logical_and(i == 1, r == 0)
            _norm_bwd_tile(lambda sl: acc[(i - 1) % 2, rows, :][sl], chunk, first, *ins[np_ + 1:], *outs, alpha)

    def ahead(i, r):
        return jnp.where(i < nt, r, nblk - 1)

    rows = pl.BlockSpec((chunk, d), lambda i, r: (jnp.where(i == 0, 0, (i - 1) * nblk + r), 0))
    whole = pl.BlockSpec((1, d), lambda i, r: (0, 0))
    specs = [pl.BlockSpec((tm, d), lambda i, r, lo=lo, n=n: (jnp.minimum(i, nt - 1), jnp.clip(ahead(i, r) - lo, 0, n - 1)))
             for lo, n in ranges]
    specs += [pl.BlockSpec((None, d, d), lambda i, r: (ahead(i, r) // per, 0, ahead(i, r) % per)), rows, whole, rows]
    got = _call(name, body, (nt + 1, nblk), specs, [rows, rows, whole],
                [jax.ShapeDtypeStruct((m, d), F32), jax.ShapeDtypeStruct((m, d), BF16),
                 jax.ShapeDtypeStruct((1, d), F32)],
                list(parts) + [w4] + list(norm), [pltpu.VMEM((2, tm, d), F32)], ("arbitrary", "arbitrary"), comm)
    return got if comm is None else (got[0], got[1])


def mm_tn_parts(name, xa, parts, tt, comm=None):
    t, k = xa.shape
    d = k
    pr = k // 2
    ranges, nblk = _part_ranges(parts, d)
    per = nblk // N_CHIPS

    def body(ins, outs, scratch):
        x_ref, acc = ins[0], scratch[0]
        jb, r = pl.program_id(0), pl.program_id(1)

        @pl.when(r == 0)
        def _():
            acc[...] = jnp.zeros_like(acc)

        for (lo, n), p_ref in zip(ranges, ins[1:]):
            @pl.when(jnp.logical_and(jb >= lo, jb < lo + n))
            def _(p_ref=p_ref):
                acc[...] += _dot(x_ref[...], p_ref[...], TN)

        @pl.when(r == t // tt - 1)
        def _():
            outs[0][0] = acc[:pr].astype(BF16)
            outs[0][1] = acc[pr:].astype(BF16)

    def part_spec(lo, n):
        return pl.BlockSpec((tt, d), lambda jb, r: (jnp.where(jnp.logical_and(jb >= lo, jb < lo + n), r, 0),
                                                    jnp.clip(jb - lo, 0, n - 1)))

    specs = [pl.BlockSpec((tt, k), lambda jb, r: (r, 0))] + [part_spec(lo, n) for lo, n in ranges]
    got = _call(name, body, (nblk, t // tt), specs,
                [pl.BlockSpec((None, 2, pr, d), lambda jb, r: (jb // per, 0, 0, jb % per))],
                [jax.ShapeDtypeStruct((N_CHIPS, 2, pr, per * d), BF16)], [xa] + list(parts),
                [pltpu.VMEM((k, d), F32)], ("parallel", "arbitrary"), comm)
    return got[0] if comm is None else (got[0][0], got[1])


def mm_nn(name, a, w, out_dtype, tm, res=None, alpha=1.0):
    m, k = a.shape
    n = w.shape[1]
    return _mm(name, a, w, jax.ShapeDtypeStruct((m, n), out_dtype), (m // tm, 1, 1),
               pl.BlockSpec((tm, k), lambda i, j, r: (i, 0)),
               pl.BlockSpec((k, n), lambda i, j, r: (0, 0)),
               pl.BlockSpec((tm, n), lambda i, j, r: (i, 0)), NN, None, res=res, alpha=alpha)


def mm_nn_stacked(name, a, w4, out_dtype, tm, tn, j0=0, nj=None, comm=None, gain=None):
    m, k = a.shape
    cs = w4.shape[2]
    per = cs // tn
    nj = N_CHIPS * per - j0 if nj is None else nj
    return _mm(name, a, w4, jax.ShapeDtypeStruct((m, nj * tn), out_dtype), (m // tm, nj, 1),
               pl.BlockSpec((tm, k), lambda i, j, r: (i, 0)),
               pl.BlockSpec((None, k, tn), lambda i, j, r: ((j + j0) // per, 0, (j + j0) % per)),
               pl.BlockSpec((tm, tn), lambda i, j, r: (i, j)), NN, None, comm=comm, gain=gain)


def mm_nt(name, dy, w, out_dtype, tm, tko, norm=None, alpha=1.0):
    m, n = dy.shape
    k = w.shape[0]
    return _mm(name, dy, w, jax.ShapeDtypeStruct((m, k), out_dtype), (m // tm, k // tko, 1),
               pl.BlockSpec((tm, n), lambda i, j, r: (i, 0)),
               pl.BlockSpec((tko, n), lambda i, j, r: (j, 0)),
               pl.BlockSpec((tm, tko), lambda i, j, r: (i, j)), NT, None, norm=norm, alpha=alpha)


def mm_nt_stacked(name, dy, w4, tm, norm, alpha=1.0, comm=None):
    m = dy.shape[1]
    k, cs = w4.shape[1], w4.shape[2]
    nt = m // tm
    chunk = tm // N_CHIPS

    def body(ins, outs, scratch):
        dy_ref, w_ref, h_ref, g_ref, dr_ref = ins
        dh_ref, dhb_ref, dg_ref = outs
        acc = scratch[0]
        i, r = pl.program_id(0), pl.program_id(1)

        @pl.when(jnp.logical_and(i < nt, r == 0))
        def _():
            acc[i % 2] = jnp.zeros(acc.shape[1:], F32)

        @pl.when(i < nt)
        def _():
            acc[i % 2] += _dot(dy_ref[...], w_ref[...], NT)

        @pl.when(i > 0)
        def _():
            rows = pl.ds(pl.multiple_of(r * chunk, chunk), chunk)
            first = jnp.logical_and(i == 1, r == 0)
            _norm_bwd_tile(lambda sl: acc[(i - 1) % 2, rows, :][sl], chunk, first, h_ref, g_ref, dr_ref,
                           dh_ref, dhb_ref, dg_ref, alpha)

    def behind(i, r):
        return (jnp.where(i == 0, 0, (i - 1) * N_CHIPS + r), 0)

    def ahead(i, r):
        return jnp.where(i < nt, r, N_CHIPS - 1)

    rows = pl.BlockSpec((chunk, k), behind)
    whole = pl.BlockSpec((1, k), lambda i, r: (0, 0))
    got = _call(name, body, (nt + 1, N_CHIPS),
                [pl.BlockSpec((None, tm, cs), lambda i, r: (ahead(i, r) // 2, jnp.minimum(i, nt - 1), ahead(i, r) % 2)),
                 pl.BlockSpec((None, k, cs), lambda i, r: (ahead(i, r), 0, 0)), rows, whole, rows],
                [rows, rows, whole],
                [jax.ShapeDtypeStruct((m, k), F32), jax.ShapeDtypeStruct((m, k), BF16),
                 jax.ShapeDtypeStruct((1, k), F32)],
                [dy, w4] + list(norm), [pltpu.VMEM((2, tm, k), F32)], ("arbitrary", "arbitrary"), comm)
    return got if comm is None else (got[0], got[1])


def mm_tn_rows(name, xa, dy, tt):
    t, k = xa.shape
    n = dy.shape[1]
    tkr = k if k * n * 4 <= ACC_BYTES else k // 2
    return _mm(name, xa, dy, jax.ShapeDtypeStruct((k, n), BF16), (k // tkr, 1, t // tt),
               pl.BlockSpec((tt, tkr), lambda i, j, r: (r, i)),
               pl.BlockSpec((tt, n), lambda i, j, r: (r, 0)),
               pl.BlockSpec((tkr, n), lambda i, j, r: (i, 0)), TN, (tkr, n))


def mm_tn_whole(name, xa, dy, tt):
    t, k = xa.shape
    n = dy.shape[1]
    return _mm(name, xa, dy, jax.ShapeDtypeStruct((k, n), BF16), (1, 1, t // tt),
               pl.BlockSpec((tt, k), lambda i, j, r: (r, 0)),
               pl.BlockSpec((tt, n), lambda i, j, r: (r, 0)),
               pl.BlockSpec((k, n), lambda i, j, r: (0, 0)), TN, (k, n))


def mm_tn_cols(name, xa, dy, tt, comm=None):
    t, k = xa.shape
    pr = k // 2
    if dy.ndim == 3:
        cs = dy.shape[2] // 2
        dy_spec = pl.BlockSpec((None, tt, cs), lambda i, j, r: (j // 2, r, j % 2))
    else:
        cs = dy.shape[1] // N_CHIPS
        dy_spec = pl.BlockSpec((tt, cs), lambda i, j, r: (r, j))
    return _mm(name, xa, dy, jax.ShapeDtypeStruct((N_CHIPS, 2, pr, cs), BF16), (1, N_CHIPS, t // tt),
               pl.BlockSpec((tt, k), lambda i, j, r: (r, 0)), dy_spec,
               pl.BlockSpec((None, 2, pr, cs), lambda i, j, r: (j, 0, 0, 0)), TN, (k, cs), comm=comm)


def _rows(tt, w, col=0):
    return pl.BlockSpec((tt, w), lambda i: (i, col))


def _whole(shape):
    return pl.BlockSpec(shape, lambda i: (0,) * len(shape))


def _rstd(h):
    return lax.rsqrt(jnp.mean(h * h, axis=-1, keepdims=True) + NORM_EPS)


def rms_fwd(name, h, g, tt, comm=None):
    t, d = h.shape

    def body(ins, outs, scratch):
        hv = ins[0][...]
        outs[0][...] = (hv * _rstd(hv) * ins[1][...]).astype(BF16)

    got = _call(name, body, (t // tt,), [_rows(tt, d), _whole((1, d))], [_rows(tt, d)],
                [jax.ShapeDtypeStruct((t, d), BF16)], [h, g], (), ("parallel",), comm)
    return got[0] if comm is None else (got[0][0], got[1])


def mix_out_fwd(name, gates, yc, ya, h, w, tt, gcol=0):
    t, d = yc.shape

    def body(g_ref, yc_ref, ya_ref, h_ref, w_ref, m_ref, o_ref):
        merged = (jax.nn.sigmoid(g_ref[:, :d].astype(F32)) * yc_ref[...].astype(F32)
                  + jax.nn.sigmoid(g_ref[:, d:].astype(F32)) * ya_ref[...].astype(F32)).astype(BF16)
        m_ref[...] = merged
        o_ref[...] = h_ref[...] + _dot(merged, w_ref[...], NN)

    return _pcall(body, name=name,
                  out_shape=(jax.ShapeDtypeStruct((t, d), BF16), jax.ShapeDtypeStruct((t, d), F32)),
                  grid=(t // tt,),
                  in_specs=[_rows(tt, 2 * d, gcol), _rows(tt, d), _rows(tt, d), _rows(tt, d), _whole((d, d))],
                  out_specs=(_rows(tt, d), _rows(tt, d)),
                  compiler_params=_params(("parallel",)))(gates, yc, ya, h, w)


def mix_out_bwd(name, dh, w, gates, yc, ya, tt, gcol=0):
    t, d = yc.shape

    def body(dh_ref, w_ref, g_ref, yc_ref, ya_ref, dyc_ref, dya_ref, dg_ref):
        dmv = _dot(dh_ref[...], w_ref[...], NT)
        sc = jax.nn.sigmoid(g_ref[:, :d].astype(F32))
        sa = jax.nn.sigmoid(g_ref[:, d:].astype(F32))
        dyc_ref[...] = (dmv * sc).astype(BF16)
        dya_ref[...] = (dmv * sa).astype(BF16)
        dg_ref[:, :d] = (dmv * yc_ref[...].astype(F32) * sc * (1.0 - sc)).astype(BF16)
        dg_ref[:, d:] = (dmv * ya_ref[...].astype(F32) * sa * (1.0 - sa)).astype(BF16)

    return _pcall(body, name=name,
                  out_shape=(jax.ShapeDtypeStruct((t, d), BF16), jax.ShapeDtypeStruct((t, d), BF16),
                             jax.ShapeDtypeStruct((t, 2 * d), BF16)),
                  grid=(t // tt,),
                  in_specs=[_rows(tt, d), _whole((d, d)), _rows(tt, 2 * d, gcol), _rows(tt, d), _rows(tt, d)],
                  out_specs=(_rows(tt, d), _rows(tt, d), _rows(tt, 2 * d)),
                  compiler_params=_params(("parallel",)))(dh, w, gates, yc, ya)


def _shift_down(cur, prev8, s):
    tt = cur.shape[0]
    rolled = pltpu.roll(cur, s, 0)
    row8 = lax.broadcasted_iota(jnp.int32, prev8.shape, 0)
    first8 = jnp.where(row8 < s, pltpu.roll(prev8, s, 0), rolled[:8])
    return jnp.concatenate([first8, rolled[8:]], axis=0) if tt > 8 else first8


def _shift_up(cur, next8, s):
    tt = cur.shape[0]
    rolled = pltpu.roll(cur, tt - s, 0)
    row8 = lax.broadcasted_iota(jnp.int32, next8.shape, 0)
    last8 = jnp.where(row8 >= 8 - s, pltpu.roll(next8, 8 - s, 0), rolled[tt - 8:])
    return jnp.concatenate([rolled[:tt - 8], last8], axis=0) if tt > 8 else last8


def _prev_rows(tt, d, col):
    return pl.BlockSpec((BF16_ROWS, d), lambda i: (jnp.maximum(i * (tt // BF16_ROWS) - 1, 0), col))


def _next_rows(tt, d, col, t):
    return pl.BlockSpec((BF16_ROWS, d),
                        lambda i: (jnp.minimum((i + 1) * (tt // BF16_ROWS), t // BF16_ROWS - 1), col))


def conv_out_fwd(name, cbx, cw8, w_out, tt):
    t = cbx.shape[0]
    d = w_out.shape[0]
    d3 = 3 * d

    def body(cb_ref, cc_ref, cx_ref, pc_ref, px_ref, w_ref, wo_ref, o_ref, y_ref):
        has_prev = (pl.program_id(0) > 0).astype(F32)
        cc = cc_ref[...].astype(F32) * cx_ref[...].astype(F32)
        prev = pc_ref[...].astype(F32)[8:] * px_ref[...].astype(F32)[8:] * has_prev
        w = w_ref[...]
        conv = w[0:1] * _shift_down(cc, prev, 2) + w[1:2] * _shift_down(cc, prev, 1) + w[2:3] * cc
        ycin = (cb_ref[...].astype(F32) * conv).astype(BF16)
        o_ref[...] = ycin
        y_ref[...] = _dot(ycin, wo_ref[...], NN).astype(BF16)

    out = jax.ShapeDtypeStruct((t, d), BF16)
    return _pcall(body, name=name, out_shape=(out, out), grid=(t // tt,),
                  in_specs=[_rows(tt, d, 0), _rows(tt, d, 1), _rows(tt, d, 2), _prev_rows(tt, d, 1),
                            _prev_rows(tt, d, 2), _whole((8, d)), _whole((d, d))],
                  out_specs=(_rows(tt, d), _rows(tt, d)),
                  compiler_params=_params(("parallel",)))(cbx, cbx, cbx, cbx, cbx, cw8, w_out)


def conv_out_bwd(name, dyc, w_out, cbx, cw8, tt):
    t = cbx.shape[0]
    d = w_out.shape[0]
    d3 = 3 * d
    n = t // tt

    def body(dy_ref, ndy_ref, wo_ref, cb_ref, cc_ref, cx_ref, pc_ref, px_ref, ncb_ref, w_ref, o_ref, dw_ref):
        i = pl.program_id(0)
        has_prev = (i > 0).astype(F32)
        has_next = (i < n - 1).astype(F32)
        cb = cb_ref[...].astype(F32)
        ccv = cc_ref[...].astype(F32)
        cxv = cx_ref[...].astype(F32)
        cc = ccv * cxv
        prev = pc_ref[...].astype(F32)[8:] * px_ref[...].astype(F32)[8:] * has_prev
        w = w_ref[...]
        cc1 = _shift_down(cc, prev, 1)
        cc2 = _shift_down(cc, prev, 2)
        conv = w[0:1] * cc2 + w[1:2] * cc1 + w[2:3] * cc
        dyv = _dot(dy_ref[...], wo_ref[...], NT)
        dconv = dyv * cb
        dnext = _dot(ndy_ref[...], wo_ref[...], NT)[:8] * ncb_ref[...].astype(F32)[:8] * has_next
        dcc = w[2:3] * dconv + w[1:2] * _shift_up(dconv, dnext, 1) + w[0:1] * _shift_up(dconv, dnext, 2)
        o_ref[:, :d] = (dyv * conv).astype(BF16)
        o_ref[:, d:2 * d] = (dcc * cxv).astype(BF16)
        o_ref[:, 2 * d:] = (dcc * ccv).astype(BF16)

        @pl.when(i == 0)
        def _():
            dw_ref[...] = jnp.zeros_like(dw_ref)

        dw_ref[0:1, :] += jnp.sum(dconv * cc2, axis=0, keepdims=True)
        dw_ref[1:2, :] += jnp.sum(dconv * cc1, axis=0, keepdims=True)
        dw_ref[2:3, :] += jnp.sum(dconv * cc, axis=0, keepdims=True)

    return _pcall(body, name=name,
                  out_shape=(jax.ShapeDtypeStruct((t, d3), BF16), jax.ShapeDtypeStruct((8, d), F32)),
                  grid=(n,),
                  in_specs=[_rows(tt, d), _next_rows(tt, d, 0, t),
                            _whole((d, d)), _rows(tt, d, 0), _rows(tt, d, 1), _rows(tt, d, 2),
                            _prev_rows(tt, d, 1), _prev_rows(tt, d, 2), _next_rows(tt, d, 0, t), _whole((8, d))],
                  out_specs=(_rows(tt, d3), _whole((8, d))),
                  compiler_params=_params(("arbitrary",)))(dyc, dyc, w_out, cbx, cbx, cbx, cbx, cbx, cbx, cw8)


def tail(name, h3, p, tgt, gp, gf, w_gate, w_proj, tt):
    t, d = h3.shape
    pd = p.shape[1]

    def body(h_ref, p_ref, tg_ref, gp_ref, gf_ref, wg_ref, wp_ref, np_ref, dh_ref, dpp_ref, dzg_ref, dgf_ref,
             loss_ref):
        hv = h_ref[...]
        npl = (hv * _rstd(hv) * gp_ref[...]).astype(BF16)
        np_ref[...] = npl
        pg = jax.nn.sigmoid(_dot(npl, wg_ref[...], NN))
        ppv = _dot(p_ref[...].astype(BF16), wp_ref[...], NN)
        h4 = hv + pg * ppv
        r4 = _rstd(h4)
        hn = h4 * r4
        gfv = gf_ref[...]
        err = hn * gfv - tg_ref[...]
        dy = err * (1.0 / d)
        gy = dy * gfv
        dh4 = r4 * (gy - hn * jnp.mean(gy * hn, axis=-1, keepdims=True))
        dh_ref[...] = dh4
        dpp_ref[...] = (dh4 * pg).astype(BF16)
        dzg_ref[...] = (dh4 * ppv * pg * (1.0 - pg)).astype(BF16)

        @pl.when(pl.program_id(0) == 0)
        def _():
            dgf_ref[...] = jnp.zeros_like(dgf_ref)
            loss_ref[...] = jnp.zeros_like(loss_ref)

        dgf_ref[...] += jnp.sum(dy * hn, axis=0, keepdims=True)
        tok = jnp.mean(err * err, axis=-1, keepdims=True)
        loss_ref[...] += 0.5 * jnp.sum(tok, axis=0, keepdims=True) * jnp.ones((1, loss_ref.shape[1]), F32)

    return _pcall(body, name=name,
                  out_shape=(jax.ShapeDtypeStruct((t, d), BF16), jax.ShapeDtypeStruct((t, d), F32),
                             jax.ShapeDtypeStruct((t, d), BF16), jax.ShapeDtypeStruct((t, d), BF16),
                             jax.ShapeDtypeStruct((1, d), F32), jax.ShapeDtypeStruct((1, d), F32)),
                  grid=(t // tt,),
                  in_specs=[_rows(tt, d), _rows(tt, pd), _rows(tt, d), _whole((1, d)), _whole((1, d)),
                            _whole((d, d)), _whole((pd, d))],
                  out_specs=(_rows(tt, d), _rows(tt, d), _rows(tt, d), _rows(tt, d), _whole((1, d)),
                             _whole((1, d))),
                  compiler_params=_params(("arbitrary",)))(h3, p, tgt, gp, gf, w_gate, w_proj)


SCALE = 1.0 / math.sqrt(HEAD_DIM)


def _log_stick(z):
    return -(jnp.maximum(z, 0.0) + jnp.log(1.0 + jnp.exp(-jnp.abs(z))))


def _tri_sum(x, tri):
    hi = x.astype(BF16)
    lo = (x - hi.astype(F32)).astype(BF16)
    return _dot(hi, tri, NN) + _dot(lo, tri, NN)


KEY_BLOCK = 128
NEAR = 3
THIN_ROWS = 32


def _pad_block(x):
    n = x.shape[0]
    return x if n == KEY_BLOCK else jnp.concatenate([x, jnp.zeros((KEY_BLOCK - n, x.shape[1]), x.dtype)], axis=0)


def _sb_near(qs, jds, k_ref, below, upper, last_rows):
    near_rows = (KEY_BLOCK,) * (NEAR - 1) + (last_rows,)
    pairs = [(s, b) for s in range(len(qs)) for b in range(NEAR)]
    rows = {(s, b): _block_rows(jnp.maximum(jds[s] - b, 0), KEY_BLOCK) for s, b in pairs}
    z = {(s, b): _dot(qs[s][:near_rows[b]], k_ref[rows[s, b], :], NT) * SCALE for s, b in pairs}
    lg = {(s, b): jnp.where(below, _log_stick(z[s, b]), 0.0) if b == 0 else _log_stick(z[s, b]) for s, b in pairs}
    cum = {(s, b): _tri_sum(lg[s, b], upper) for s, b in pairs}
    out, carries = [], []
    for s in range(len(qs)):
        c = cum[s, 0][:, 0:1]
        blocks = [(rows[s, 0], z[s, 0], jnp.exp(jnp.where(below, z[s, 0] + cum[s, 0], -1e30)))]
        for b in range(1, NEAR):
            live = jds[s] >= b
            off = c[:near_rows[b]] + jnp.where(live, 0.0, -1e30)
            blocks.append((rows[s, b], z[s, b], jnp.exp(z[s, b] + cum[s, b] + off)))
            c = c + _pad_block(jnp.where(live, cum[s, b][:, 0:1], 0.0))
        out.append(blocks)
        carries.append(c)
    return out, carries


def _sb_far(q, kj, upper, c, skip):
    z = _dot(q, kj, NT) * SCALE
    cum = _tri_sum(_log_stick(z), upper)
    return z, jnp.exp(z + cum + (c + jnp.where(skip, -1e30, 0.0))), c + jnp.where(skip, 0.0, cum[:, 0:1])


def _took_it(j, jd, last_rows):
    first = lax.broadcasted_iota(jnp.int32, (KEY_BLOCK, 1), 0) < last_rows
    return jnp.logical_and(j == jd - (NEAR - 1), first)


def _block_rows(j, size):
    return pl.ds(pl.multiple_of(j * size, size), size)


def _sweep_on(st):
    return jnp.logical_and(st[0] >= 0, jnp.max(st[1]) > -STICK_EXIT)


def attn_fwd(name, qkv, tq, d, col0=0):
    t = qkv.shape[0]
    nh = d // HEAD_DIM
    q0 = col0 // HEAD_DIM
    nq = t // tq
    tb = KEY_BLOCK
    nsub = tq // tb

    def body(q_ref, k_ref, v_ref, o_ref):
        i = pl.program_id(1)
        row = lax.broadcasted_iota(jnp.int32, (tb, tb), 0)
        col = lax.broadcasted_iota(jnp.int32, (tb, tb), 1)
        upper = (row >= col).astype(BF16)
        qs = [q_ref[s * tb:(s + 1) * tb, :] for s in range(nsub)]
        jds = [i * nsub + s for s in range(nsub)]
        near, carries = _sb_near(qs, jds, k_ref, col < row, upper, THIN_ROWS)
        state = []
        for s in range(nsub):
            acc = jnp.zeros((tb, HEAD_DIM), F32)
            for rows, _, a in near[s]:
                acc = acc + _pad_block(_dot(a.astype(BF16), v_ref[rows, :], NN))
            state.append((qs[s], jds[s], carries[s], acc))
        for s, (q, jd, c, acc) in enumerate(state):

            def step(st, q=q, jd=jd):
                rows = _block_rows(st[0], tb)
                _, a, c2 = _sb_far(q, k_ref[rows, :], upper, st[1], _took_it(st[0], jd, THIN_ROWS))
                return st[0] - 1, c2, st[2] + _dot(a.astype(BF16), v_ref[rows, :], NN)

            _, _, acc = lax.while_loop(_sweep_on, step, (jd - (NEAR - 1), c, acc))
            o_ref[s * tb:(s + 1) * tb, :] = acc.astype(o_ref.dtype)

    return _pcall(body, name=name, out_shape=jax.ShapeDtypeStruct((t, d), BF16), grid=(nh, nq),
                  in_specs=[pl.BlockSpec((tq, HEAD_DIM), lambda h, i: (i, q0 + h)),
                            pl.BlockSpec((t, HEAD_DIM), lambda h, i: (0, q0 + nh + h)),
                            pl.BlockSpec((t, HEAD_DIM), lambda h, i: (0, q0 + 2 * nh + h))],
                  out_specs=pl.BlockSpec((tq, HEAD_DIM), lambda h, i: (i, h)),
                  compiler_params=_params(("parallel", "arbitrary")))(qkv, qkv, qkv)


def attn_bwd(name, qkv, do, tq, col0=0):
    d = do.shape[1]
    t = qkv.shape[0]
    nh = d // HEAD_DIM
    q0 = col0 // HEAD_DIM
    nq = t // tq
    tb = KEY_BLOCK
    nsub = tq // tb

    def body(q_ref, k_ref, v_ref, do_ref, dq_ref, dk_ref, dv_ref, dk_acc, dv_acc, g_buf, z_buf):
        i = pl.program_id(1)

        @pl.when(i == 0)
        def _():
            dk_acc[...] = jnp.zeros_like(dk_acc)
            dv_acc[...] = jnp.zeros_like(dv_acc)

        row = lax.broadcasted_iota(jnp.int32, (tb, tb), 0)
        col = lax.broadcasted_iota(jnp.int32, (tb, tb), 1)
        below = col < row
        upper = (row >= col).astype(BF16)
        lower = (row <= col).astype(BF16)

        qs = [q_ref[s * tb:(s + 1) * tb, :] for s in range(nsub)]
        dos = [do_ref[s * tb:(s + 1) * tb, :] for s in range(nsub)]
        jds = [i * nsub + s for s in range(nsub)]
        near, carries = _sb_near(qs, jds, k_ref, below, upper, KEY_BLOCK)
        da = [[_dot(dos[s][:a.shape[0]], v_ref[rows, :], NT) for rows, _, a in near[s]] for s in range(nsub)]
        state = []
        for s in range(nsub):
            kept = [(rows, z, da[s][b] * a) for b, (rows, z, a) in enumerate(near[s])]
            for rows, _, a in near[s]:
                dv_acc[rows, :] += _dot(a.astype(BF16), dos[s][:a.shape[0]], TN)
            state.append((qs[s], dos[s], jds[s], carries[s], kept))

        carried = []
        for s, (q, dov, jd, c, kept) in enumerate(state):
            def step(st, s=s, q=q, dov=dov, jd=jd):
                j = st[0]
                rows = _block_rows(j, tb)
                z, a, c2 = _sb_far(q, k_ref[rows, :], upper, st[1], _took_it(j, jd, KEY_BLOCK))
                g_buf[jd - j] = _dot(dov, v_ref[rows, :], NT) * a
                z_buf[jd - j] = z
                dv_acc[rows, :] += _dot(a.astype(BF16), dov, TN)
                return j - 1, c2

            j_stop, _ = lax.while_loop(_sweep_on, step, (jd - (NEAR - 1), c))

            def far(j, st, s=s, q=q, jd=jd):
                run, dq = st
                rows = _block_rows(j, tb)
                g = g_buf[jd - j]
                dz = (g - jax.nn.sigmoid(z_buf[jd - j]) * (run + _tri_sum(g, lower))).astype(BF16)
                dk_acc[rows, :] += _dot(dz, q, TN)
                return run + jnp.sum(g, axis=1, keepdims=True), dq + _dot(dz, k_ref[rows, :], NN)

            carried.append(lax.fori_loop(j_stop + 1, jd - (NEAR - 1) + 1, far,
                                         (jnp.zeros((tb, 1), F32), jnp.zeros((tb, HEAD_DIM), F32))))

        tri = [[_dot(g.astype(BF16), lower, NN) for _, _, g in st[4]] for st in state]
        sig = [[jax.nn.sigmoid(z) for _, z, _ in st[4]] for st in state]
        for s, (q, dov, jd, c, kept) in enumerate(state):
            run, dq = carried[s]
            for b in reversed(range(NEAR)):
                rows, z, g = kept[b]
                n = g.shape[0]
                dz = g - sig[s][b] * (run[:n] + tri[s][b])
                if b == 0:
                    dz = jnp.where(below, dz, 0.0)
                dz = dz.astype(BF16)
                dk_acc[rows, :] += _dot(dz, q[:n], TN)
                dq = dq + _pad_block(_dot(dz, k_ref[rows, :], NN))
                if b:
                    run = run + _pad_block(jnp.sum(g, axis=1, keepdims=True))
            dq_ref[s * tb:(s + 1) * tb, :] = (dq * SCALE).astype(BF16)

        @pl.when(i == nq - 1)
        def _():
            dk_ref[...] = (dk_acc[...] * SCALE).astype(BF16)
            dv_ref[...] = dv_acc[...].astype(BF16)

    blk = pl.BlockSpec((tq, HEAD_DIM), lambda h, i: (i, h))
    col_h = pl.BlockSpec((t, HEAD_DIM), lambda h, i: (0, h))
    out = jax.ShapeDtypeStruct((t, d), BF16)
    return _pcall(body, name=name, out_shape=(out, out, out), grid=(nh, nq),
                  in_specs=[pl.BlockSpec((tq, HEAD_DIM), lambda h, i: (i, q0 + h)),
                            pl.BlockSpec((t, HEAD_DIM), lambda h, i: (0, q0 + nh + h)),
                            pl.BlockSpec((t, HEAD_DIM), lambda h, i: (0, q0 + 2 * nh + h)),
                            blk],
                  out_specs=(blk, col_h, col_h),
                  scratch_shapes=[pltpu.VMEM((t, HEAD_DIM), F32), pltpu.VMEM((t, HEAD_DIM), F32),
                                  pltpu.VMEM((t // tb, tb, tb), F32), pltpu.VMEM((t // tb, tb, tb), F32)],
                  compiler_params=_params(("parallel", "arbitrary")))(qkv, qkv, qkv, do)


def _place():
    x, y, c = lax.axis_index("x"), lax.axis_index("y"), lax.axis_index("c")
    chips = [(1 - x, y), (x, 1 - y), (1 - x, 1 - y)]
    return x, y, c, chips


def _remote(src, dst, send_sem, recv_sem, dev):
    return pltpu.make_async_remote_copy(src_ref=src, dst_ref=dst, send_sem=send_sem, recv_sem=recv_sem,
                                        device_id=dev, device_id_type=MESH)


def place_shards(name, ws, chip):
    tiles, steps = _job_tiles([w.shape for w in ws], 1 << 20, BF16_ROWS)
    nj = len(ws)

    def body(chip_ref, *refs):
        i = pl.program_id(0)
        for k, (_, n) in enumerate(tiles):
            @pl.when(i < n)
            def _(w_ref=refs[k], o_ref=refs[nj + k]):
                o_ref[...] = w_ref[...].astype(BF16)

    spec = pltpu.PrefetchScalarGridSpec(
        num_scalar_prefetch=1, grid=(steps,),
        in_specs=[pl.BlockSpec((tr, w.shape[1]), lambda i, s, n=n: (jnp.minimum(i, n - 1), 0))
                  for w, (tr, n) in zip(ws, tiles)],
        out_specs=[pl.BlockSpec((None, tr, w.shape[1]), lambda i, s, n=n: (s[0], jnp.minimum(i, n - 1), 0))
                   for w, (tr, n) in zip(ws, tiles)])
    return _pcall(body, name=name, out_shape=[jax.ShapeDtypeStruct((N_CHIPS,) + w.shape, BF16) for w in ws],
                  grid_spec=spec, compiler_params=_params(("arbitrary",)))(chip, *ws)


class Comm:
    def __init__(self, ins, outs, aliases, sems, first, mid, last):
        self.ins, self.outs, self.aliases, self.sems = list(ins), list(outs), dict(aliases), list(sems)
        self.first, self.mid, self.last = first, mid, last


def run_comm(name, comm):
    ni, no = len(comm.ins), len(comm.outs)

    def body(*refs):
        ins, outs, sems = refs[:ni], refs[ni:ni + no], refs[ni + no:]
        comm.first(ins, outs, sems)
        comm.mid(ins, outs, sems)
        comm.last(ins, outs, sems)

    return _pcall(body, name=name, out_shape=comm.outs, in_specs=[ANY] * ni, out_specs=[ANY] * no,
                  input_output_aliases=comm.aliases, scratch_shapes=comm.sems, compiler_params=_params())(*comm.ins)


def gather_comm(bufs):
    n = len(bufs)

    def half(out, w, which):
        pr = out[w].shape[1] // 2
        return pl.ds(pl.multiple_of(which * pr, BF16_ROWS), pr)

    def first(ins, out, sems):
        isend, irecv, _, _ = sems
        x, y, c, chips = _place()
        for w in range(n):
            mine = out[w].at[2 * x + y, half(out, w, c)]
            for j, (cx, cy) in enumerate(chips):
                _remote(mine, mine, isend.at[3 * w + j], irecv.at[3 * w + j], (cx, cy, c)).start()

    def mid(ins, out, sems):
        isend, irecv, dsend, drecv = sems
        x, y, c, chips = _place()
        sib = (x, y, 1 - c)
        for w in range(n):
            for j, (cx, cy) in enumerate(chips):
                landed = out[w].at[2 * cx + cy, half(out, w, c)]
                _remote(landed, landed, isend.at[3 * w + j], irecv.at[3 * w + j], sib).wait_recv()
                _remote(landed, landed, dsend.at[3 * w + j], drecv.at[3 * w + j], sib).start()

    def last(ins, out, sems):
        isend, irecv, dsend, drecv = sems
        x, y, c, chips = _place()
        sib = (x, y, 1 - c)
        for w in range(n):
            for j, (cx, cy) in enumerate(chips):
                landed = out[w].at[2 * cx + cy, half(out, w, 1 - c)]
                _remote(landed, landed, dsend.at[3 * w + j], drecv.at[3 * w + j], sib).wait_recv()
        for w in range(n):
            sent = out[w].at[0, half(out, w, c)]
            for j in range(3):
                _remote(sent, sent, isend.at[3 * w + j], irecv.at[3 * w + j], sib).wait_send()
                _remote(sent, sent, dsend.at[3 * w + j], drecv.at[3 * w + j], sib).wait_send()

    return Comm(bufs, [jax.ShapeDtypeStruct(s.shape, s.dtype) for s in bufs], {w: w for w in range(n)},
                [pltpu.SemaphoreType.DMA((3 * n,))] * 4, first, mid, last)


def _nothing(ins, outs, sems):
    return None


def join_comms(a, b):
    ni, no, ns = len(a.ins), len(a.outs), len(a.sems)

    def both(f, g):
        def hook(ins, outs, sems):
            f(ins[:ni], outs[:no], sems[:ns])
            g(ins[ni:], outs[no:], sems[ns:])
        return hook

    aliases = dict(a.aliases)
    aliases.update({ni + k: no + v for k, v in b.aliases.items()})
    return Comm(a.ins + b.ins, a.outs + b.outs, aliases, a.sems + b.sems,
                both(a.first, b.first), both(a.mid, b.mid), both(a.last, b.last))


def exchange_comm(pieces):
    n = len(pieces)

    def copies(src, out, sems):
        x, y, c, _ = _place()
        return [_remote(src[w].at[k, 1 - c], out[w].at[k], sems[0].at[N_CHIPS * w + k], sems[1].at[N_CHIPS * w + k],
                        (x, y, 1 - c)) for w in range(n) for k in range(N_CHIPS)]

    def first(src, out, sems):
        for cp in copies(src, out, sems):
            cp.start()

    def last(src, out, sems):
        for cp in copies(src, out, sems):
            cp.wait()

    return Comm(pieces, [jax.ShapeDtypeStruct((N_CHIPS,) + s.shape[2:], s.dtype) for s in pieces], {},
                [pltpu.SemaphoreType.DMA((N_CHIPS * n,))] * 2, first, _nothing, last)


def scatter_comm(parts):
    n = len(parts)

    def copies(src, out, sems):
        x, y, c, chips = _place()
        return [_remote(src[w].at[2 * cx + cy], out[w].at[j], sems[0].at[3 * w + j], sems[1].at[3 * w + j], (cx, cy, c))
                for w in range(n) for j, (cx, cy) in enumerate(chips)]

    def first(src, out, sems):
        for cp in copies(src, out, sems):
            cp.start()

    def last(src, out, sems):
        for cp in copies(src, out, sems):
            cp.wait()

    return Comm(parts, [jax.ShapeDtypeStruct((3,) + s.shape[1:], s.dtype) for s in parts], {},
                [pltpu.SemaphoreType.DMA((3 * n,))] * 2, first, _nothing, last)


def share_comm(halves):
    n = len(halves)

    def first(ins, buf, sems):
        x, y, c, _ = _place()
        for w in range(n):
            _remote(buf[w].at[c], buf[w].at[c], sems[0].at[w], sems[1].at[w], (x, y, 1 - c)).start()

    def last(ins, buf, sems):
        x, y, c, _ = _place()
        for w in range(n):
            landed = buf[w].at[1 - c]
            _remote(landed, landed, sems[0].at[w], sems[1].at[w], (x, y, 1 - c)).wait_recv()
        for w in range(n):
            _remote(buf[w].at[c], buf[w].at[c], sems[0].at[w], sems[1].at[w], (x, y, 1 - c)).wait_send()

    return Comm(halves, [jax.ShapeDtypeStruct(s.shape, s.dtype) for s in halves], {w: w for w in range(n)},
                [pltpu.SemaphoreType.DMA((n,))] * 2, first, _nothing, last)


def gather_small(name, blk, reduce):
    r, cdim = blk.shape

    def body(in_ref, out_ref, *rest):
        if reduce:
            buf, send_sem, recv_sem = rest
        else:
            buf = out_ref
            send_sem, recv_sem = rest
        x, y, c, _ = _place()
        me = 4 * x + 2 * y + c
        buf[me] = in_ref[...]
        peers = []
        for dx in range(2):
            for dy in range(2):
                for dc in range(2):
                    if dx or dy or dc:
                        peers.append((dx, dy, dc))
        copies = []
        for s, (dx, dy, dc) in enumerate(peers):
            cp = _remote(in_ref, buf.at[me], send_sem.at[s], recv_sem.at[s],
                         ((1 - x if dx else x), (1 - y if dy else y), (1 - c if dc else c)))
            cp.start()
            copies.append(cp)
        for s, (dx, dy, dc) in enumerate(peers):
            px, py, pc_ = (1 - x if dx else x), (1 - y if dy else y), (1 - c if dc else c)
            landed = buf.at[4 * px + 2 * py + pc_]
            _remote(landed, landed, send_sem.at[s], recv_sem.at[s], (x, y, c)).wait_recv()
        for cp in copies:
            cp.wait_send()
        if reduce:
            tot = buf[0]
            for s in range(1, N_DEV):
                tot = tot + buf[s]
            out_ref[...] = tot

    vm = pl.BlockSpec(memory_space=pltpu.VMEM)
    out_shape = jax.ShapeDtypeStruct((r, cdim) if reduce else (N_DEV, r, cdim), F32)
    scratch = ([pltpu.VMEM((N_DEV, r, cdim), F32)] if reduce else []) + [pltpu.SemaphoreType.DMA((N_DEV - 1,))] * 2
    return _pcall(body, name=name, out_shape=out_shape, in_specs=[vm], out_specs=vm, scratch_shapes=scratch,
                  compiler_params=_params())(blk)


def _job_tiles(shapes, tile_bytes, mult):
    tiles = []
    for rows, cols in shapes:
        tr = _tile(rows, max(mult, tile_bytes // (4 * cols)), mult)
        tiles.append((tr, rows // tr))
    return tiles, max(n for _, n in tiles)


def sum_cores(name, owns, gots, place):
    nj = len(owns)
    tiles, _ = _job_tiles([o.shape[2:] for o in owns], 1 << 21, BF16_ROWS)
    steps = max(N_CHIPS * n for _, n in tiles)

    def body(place_ref, *refs):
        i = pl.program_id(0)
        for k, (_, n) in enumerate(tiles):
            @pl.when(i < N_CHIPS * n)
            def _(own_ref=refs[2 * k], got_ref=refs[2 * k + 1], o_ref=refs[2 * nj + k]):
                o_ref[...] = (own_ref[...].astype(F32) + got_ref[...].astype(F32)).astype(o_ref.dtype)

    in_specs, out_specs, out_shape, args = [], [], [], []
    for own, got, (tr, n) in zip(owns, gots, tiles):
        pc = own.shape[3]
        last = N_CHIPS * n - 1
        in_specs += [pl.BlockSpec((None, None, tr, pc),
                                  lambda i, s, n=n, last=last: (jnp.minimum(i, last) // n, s[1], jnp.minimum(i, last) % n, 0)),
                     pl.BlockSpec((None, tr, pc),
                                  lambda i, s, n=n, last=last: (jnp.minimum(i, last) // n, jnp.minimum(i, last) % n, 0))]
        out_specs.append(pl.BlockSpec((None, tr, pc),
                                      lambda i, s, n=n, last=last: (jnp.minimum(i, last) // n, jnp.minimum(i, last) % n, 0)))
        out_shape.append(jax.ShapeDtypeStruct(got.shape, BF16))
        args += [own, got]
    spec = pltpu.PrefetchScalarGridSpec(num_scalar_prefetch=1, grid=(steps,), in_specs=in_specs, out_specs=out_specs)
    return _pcall(body, name=name, out_shape=out_shape, grid_spec=spec,
                  compiler_params=_params(("arbitrary",)))(place, *args)


def sum_chips(name, parts, gots, place):
    nj = len(parts)
    tiles, steps = _job_tiles([p.shape[1:] for p in parts], 1 << 20, BF16_ROWS)

    def body(place_ref, *refs):
        i = pl.program_id(0)
        for k, (_, n) in enumerate(tiles):
            @pl.when(i < n)
            def _(part_ref=refs[2 * k], got_ref=refs[2 * k + 1], o_ref=refs[2 * nj + k]):
                tot = part_ref[...].astype(F32)
                for j in range(3):
                    tot = tot + got_ref[j].astype(F32)
                o_ref[...] = tot

    in_specs, out_specs, out_shape, args = [], [], [], []
    for part, got, (tr, n) in zip(parts, gots, tiles):
        pc = part.shape[2]
        in_specs += [pl.BlockSpec((None, tr, pc), lambda i, s, n=n: (s[0], jnp.minimum(i, n - 1), 0)),
                     pl.BlockSpec((3, tr, pc), lambda i, s, n=n: (0, jnp.minimum(i, n - 1), 0))]
        out_specs.append(pl.BlockSpec((None, tr, pc), lambda i, s, n=n: (s[1], jnp.minimum(i, n - 1), 0)))
        out_shape.append(jax.ShapeDtypeStruct((2,) + part.shape[1:], F32))
        args += [part, got]
    spec = pltpu.PrefetchScalarGridSpec(num_scalar_prefetch=1, grid=(steps,), in_specs=in_specs, out_specs=out_specs)
    return _pcall(body, name=name, out_shape=out_shape, grid_spec=spec,
                  compiler_params=_params(("arbitrary",)))(place, *args)


def adamw(name, jobs):
    c1 = 1.0 / (1.0 - ADAM_B1 ** ADAM_STEP)
    c2 = 1.0 / (1.0 - ADAM_B2 ** ADAM_STEP)
    nj = len(jobs)
    tiles, steps = _job_tiles([j[0].shape for j in jobs], 1 << 18, 8)

    def body(*refs):
        i = pl.program_id(0)
        for k, (_, n) in enumerate(tiles):
            w_ref, g_ref, m_ref, v_ref = refs[4 * k:4 * k + 4]
            d_ref, nm_ref, nv_ref = refs[4 * nj + 3 * k:4 * nj + 3 * k + 3]

            @pl.when(i < n)
            def _(w_ref=w_ref, g_ref=g_ref, m_ref=m_ref, v_ref=v_ref, d_ref=d_ref, nm_ref=nm_ref, nv_ref=nv_ref):
                gv = g_ref[...]
                nm = ADAM_B1 * m_ref[...] + (1.0 - ADAM_B1) * gv
                nv = ADAM_B2 * v_ref[...] + (1.0 - ADAM_B2) * (gv * gv)
                nm_ref[...] = nm
                nv_ref[...] = nv
                d_ref[...] = -ADAM_LR * ((nm * c1) / (jnp.sqrt(nv * c2) + ADAM_EPS) + ADAM_WD * w_ref[...])

    in_specs, out_specs, out_shape, args = [], [], [], []
    for (w, g, m, v), (tr, n) in zip(jobs, tiles):
        spec = pl.BlockSpec((tr, w.shape[1]), lambda i, n=n: (jnp.minimum(i, n - 1), 0))
        in_specs += [spec] * 4
        out_specs += [spec] * 3
        out_shape += [jax.ShapeDtypeStruct(w.shape, F32)] * 3
        args += [w, g, m, v]
    res = _pcall(body, name=name, out_shape=out_shape, grid=(steps,), in_specs=in_specs, out_specs=out_specs,
                 compiler_params=_params(("arbitrary",)))(*args)
    return [tuple(res[3 * k:3 * k + 3]) for k in range(nj)]


MATS = ["ffn1_w_in", "ffn1_w_out", "w_mix_in", "w_conv_out", "w_attn_out", "w_mix_out", "ffn2_w_in", "ffn2_w_out",
        "w_ple_gate", "w_ple_proj"]
COL_SHARDED = {"ffn1_w_in", "w_mix_in", "ffn2_w_in", "w_ple_proj"}
NORMS = ["ffn1_norm", "mix_norm", "ffn2_norm", "ple_norm", "final_norm"]
WEIGHTS = ["ffn1_norm", "ffn1_w_in", "ffn1_w_out", "mix_norm", "w_mix_in", "conv_w", "w_conv_out", "w_attn_out",
           "w_mix_out", "ffn2_norm", "ffn2_w_in", "ffn2_w_out", "ple_norm", "w_ple_gate", "w_ple_proj", "final_norm"]


def _pad_rows(a, rows):
    return jnp.concatenate([a, jnp.zeros((rows - a.shape[0],) + a.shape[1:], a.dtype)], axis=0)


def _step(x, p, tgt, w, m, v):
    t, d = x.shape
    tt = _tile(t, 256)
    tm = _tile(t, 512)
    tm2 = _tile(t, 1024)
    tq = _tile(t, 1024)

    chip = 2 * lax.axis_index("x") + lax.axis_index("y")
    place = jnp.stack([chip, lax.axis_index("c")]).astype(jnp.int32)

    placed = dict(zip(MATS, place_shards("place_shards", [w[k] for k in MATS], place)))
    full = {}

    def keep(names, bufs):
        for k, buf in zip(names, bufs):
            full[k] = buf if k in COL_SHARDED else buf.reshape(-1, buf.shape[2])

    def gather_of(names):
        return gather_comm([placed[k] for k in names])

    cw_all = gather_small("gather_conv_w", _pad_rows(w["conv_w"], 8), False)
    cw8 = jnp.concatenate([cw_all[2 * k] for k in range(N_CHIPS)], axis=1)
    g1, gm, g2, gp, gf = (w[k].reshape(1, d) for k in NORMS)

    def ffn_fwd(tag, h, g, first, w_in_name, w_out_name, riders):
        if first:
            n, bufs = rms_fwd(tag + "_norm", h, g, tt, comm=gather_of(first))
            keep(first, bufs)
            (a, s), bufs = ffn_in_act(tag + "_in", n, full[w_in_name], tm, comm=gather_of(riders))
            keep(riders, bufs)
        else:
            a, s, n = ffn_in_act(tag + "_in", h, full[w_in_name], tm, gain=g)
        return n, a, s, mm_nn(tag + "_out", s, full[w_out_name], F32, tm, res=h, alpha=0.5)

    n1, a1, s1, h1 = ffn_fwd("ffn1", x, g1, ["ffn1_w_in"], "ffn1_w_in", "ffn1_w_out", ["ffn1_w_out", "w_mix_in"])
    wmix = full["w_mix_in"]
    riders = ["w_conv_out", "w_attn_out", "w_mix_out", "ffn2_w_in", "ffn2_w_out", "w_ple_gate", "w_ple_proj"]
    (mixin, u), bufs = mm_nn_stacked("mix_in", h1, wmix, BF16, tm2, d, comm=gather_of(riders), gain=gm)
    keep(riders, bufs)
    cbx = qkv = gates = mixin
    wpp = full["w_ple_proj"]
    wpp = jnp.transpose(wpp, (1, 0, 2)).reshape(wpp.shape[1], -1)
    ycin, y_conv = conv_out_fwd("conv_out", cbx, cw8, full["w_conv_out"], tt)
    o = attn_fwd("attn", qkv, tq, d, 3 * d)
    y_attn = mm_nn("attn_out", o, full["w_attn_out"], BF16, tm)
    merged, h2 = mix_out_fwd("mix_out", gates, y_conv, y_attn, h1, full["w_mix_out"], tm, 3)
    n2, a2, s2, h3 = ffn_fwd("ffn2", h2, g2, [], "ffn2_w_in", "ffn2_w_out", [])

    pieces, chip_sums, halves = {}, {}, {}

    def as_pieces(k):
        pc = pieces[k]
        return pc if k in COL_SHARDED else pc.reshape(N_CHIPS, 2, pc.shape[0] // (2 * N_CHIPS), pc.shape[1])

    def sum_siblings(tag, names):
        pcs = [as_pieces(k) for k in names]
        got = run_comm("exchange_" + tag, exchange_comm(pcs))
        chip_sums.update(zip(names, sum_cores("sum_cores_" + tag, pcs, got, place)))

    def scatter_of(names):
        return scatter_comm([chip_sums[k] for k in names])

    def sum_landed(tag, names, landed):
        halves.update(zip(names, sum_chips("sum_chips_" + tag, [chip_sums[k] for k in names], landed, place)))

    npl, dh4, dpp, dzg, dgf, loss_row = tail("tail", h3, p, tgt, gp, gf, full["w_ple_gate"], wpp, tm)
    dwpp = mm_tn_whole("ple_proj_dw", p, dpp, tm2)
    pieces["w_ple_proj"] = jnp.transpose(dwpp.reshape(2, p.shape[1] // 2, N_CHIPS, d // N_CHIPS), (2, 0, 1, 3))
    pieces["w_ple_gate"] = mm_tn_rows("ple_gate_dw", npl, dzg, tm2)
    dh3, df2, dgp = mm_nt("ple_gate_dx", dzg, full["w_ple_gate"], F32, tm, d, norm=(h3, gp, dh4), alpha=0.5)
    w_in, w_out = full["ffn2_w_in"], full["ffn2_w_out"]
    pieces["ffn2_w_out"] = mm_tn_rows("ffn2_dwout", s2, df2, tm2)
    da2 = ffn_ds_dact("ffn2_ds", df2, w_out, a2, tm2)
    pieces["ffn2_w_in"] = mm_tn_cols("ffn2_dwin", n2, da2, tm2)
    dh2, dh2b, dg2 = mm_nt_stacked("ffn2_dn", da2, w_in, tm2, (h2, g2, dh3))
    pieces["w_mix_out"] = mm_tn_rows("mix_out_dw", merged, dh2b, tm2)
    dyc, dya, dgates = mix_out_bwd("mix_out_dx", dh2b, full["w_mix_out"], gates, y_conv, y_attn, tm, 3)
    pieces["w_conv_out"] = mm_tn_rows("conv_out_dw", ycin, dyc, tm2)
    dcbx, dcw8 = conv_out_bwd("conv_out_dx", dyc, full["w_conv_out"], cbx, cw8, tt)
    pieces["w_attn_out"] = mm_tn_rows("attn_out_dw", o, dya, tm2)
    do = mm_nt("attn_out_dx", dya, full["w_attn_out"], BF16, tm, d)
    dq, dk, dv = attn_bwd("attn_bwd", qkv, do, tq, 3 * d)
    dmix = [dcbx, dq, dk, dv, dgates]
    early = ["ffn2_w_in", "ffn2_w_out", "w_ple_gate", "w_ple_proj", "w_mix_out", "w_conv_out", "w_attn_out"]
    swap = exchange_comm([as_pieces(k) for k in early])
    pieces["w_mix_in"], got = mm_tn_parts("mix_in_dw", u, dmix, tm2, comm=swap)
    chip_sums.update(zip(early, sum_cores("sum_cores_early", swap.ins, got, place)))
    swap = exchange_comm([as_pieces("w_mix_in")])
    (dh1, df1, dgm), landed = mm_nt_parts("mix_in_dx", dmix, wmix, tm2, (h1, gm, dh2), 0.5,
                                          comm=join_comms(scatter_of(early), swap))
    sum_landed("early", early, landed[:len(early)])
    chip_sums["w_mix_in"] = sum_cores("sum_cores_mix", swap.ins, landed[len(early):], place)[0]
    w_in, w_out = full["ffn1_w_in"], full["ffn1_w_out"]
    pieces["ffn1_w_out"] = mm_tn_rows("ffn1_dwout", s1, df1, tm2)
    da1 = ffn_ds_dact("ffn1_ds", df1, w_out, a1, tm2)
    pieces["ffn1_w_in"], landed = mm_tn_cols("ffn1_dwin", n1, da1, tm2, comm=scatter_of(["w_mix_in"]))
    sum_landed("mix", ["w_mix_in"], landed)
    late = ["ffn1_w_in", "ffn1_w_out"]
    sum_siblings("late", late)
    done = early + ["w_mix_in"]
    (dx, _, dg1), landed = mm_nt_stacked(
        "ffn1_dn", da1, w_in, tm2, (x, g1, dh1),
        comm=join_comms(scatter_of(late), share_comm([halves[k] for k in done])))
    sum_landed("late", late, landed[:len(late)])
    shared = dict(zip(done, landed[len(late):]))

    shared.update(zip(late, run_comm("share_halves", share_comm([halves[k] for k in late]))))
    grad, delta, new_m, new_v = {}, {}, {}, {}
    for k in MATS:
        grad[k] = shared[k].reshape(w[k].shape)

    small = jnp.concatenate([dg1, dgm, dg2, dgp, dgf, dcw8[:3], loss_row, jnp.zeros((7, d), F32)], axis=0)
    tot = gather_small("sum_small", small, True)
    loss = tot[8, 0]
    norm_w = jnp.concatenate([w[k].reshape(1, d) for k in NORMS] + [jnp.zeros((3, d), F32)], axis=0)
    norm_m = jnp.concatenate([m[k].reshape(1, d) for k in NORMS] + [jnp.zeros((3, d), F32)], axis=0)
    norm_v = jnp.concatenate([v[k].reshape(1, d) for k in NORMS] + [jnp.ones((3, d), F32)], axis=0)
    norm_g = jnp.concatenate([tot[0:5], jnp.zeros((3, d), F32)], axis=0)
    cs = d // N_CHIPS
    gcw = lax.dynamic_slice(tot[5:8], (0, chip * cs), (3, cs))
    conv_job = (_pad_rows(w["conv_w"], 8), _pad_rows(gcw, 8), _pad_rows(m["conv_w"], 8),
                jnp.concatenate([v["conv_w"], jnp.ones((5, cs), F32)], axis=0))

    steps = adamw("adamw", [(w[k], grad[k], m[k], v[k]) for k in MATS]
                  + [(norm_w, norm_g, norm_m, norm_v), conv_job])
    for k, res in zip(MATS, steps):
        delta[k], new_m[k], new_v[k] = res
    nd, nm, nv = steps[len(MATS)]
    for r, k in enumerate(NORMS):
        grad[k] = norm_g[r].reshape(w[k].shape)
        delta[k], new_m[k], new_v[k] = (a[r].reshape(w[k].shape) for a in (nd, nm, nv))
    cd, cm, cv = steps[len(MATS) + 1]
    grad["conv_w"], delta["conv_w"], new_m["conv_w"], new_v["conv_w"] = gcw, cd[:3], cm[:3], cv[:3]
    return loss, dx, grad, delta, new_m, new_v


def kernel(x, p, ffn1_norm, ffn1_w_in, ffn1_w_out, mix_norm, w_mix_in, conv_w, w_conv_out, w_attn_out, w_mix_out, ffn2_norm, ffn2_w_in, ffn2_w_out, ple_norm, w_ple_gate, w_ple_proj, final_norm, loss_target, m_ffn1_norm, m_ffn1_w_in, m_ffn1_w_out, m_mix_norm, m_w_mix_in, m_conv_w, m_w_conv_out, m_w_attn_out, m_w_mix_out, m_ffn2_norm, m_ffn2_w_in, m_ffn2_w_out, m_ple_norm, m_w_ple_gate, m_w_ple_proj, m_final_norm, v_ffn1_norm, v_ffn1_w_in, v_ffn1_w_out, v_mix_norm, v_w_mix_in, v_conv_w, v_w_conv_out, v_w_attn_out, v_w_mix_out, v_ffn2_norm, v_ffn2_w_in, v_ffn2_w_out, v_ple_norm, v_w_ple_gate, v_w_ple_proj, v_final_norm):
    ws = (ffn1_norm, ffn1_w_in, ffn1_w_out, mix_norm, w_mix_in, conv_w, w_conv_out, w_attn_out, w_mix_out, ffn2_norm,
          ffn2_w_in, ffn2_w_out, ple_norm, w_ple_gate, w_ple_proj, final_norm)
    ms = (m_ffn1_norm, m_ffn1_w_in, m_ffn1_w_out, m_mix_norm, m_w_mix_in, m_conv_w, m_w_conv_out, m_w_attn_out,
          m_w_mix_out, m_ffn2_norm, m_ffn2_w_in, m_ffn2_w_out, m_ple_norm, m_w_ple_gate, m_w_ple_proj, m_final_norm)
    vs = (v_ffn1_norm, v_ffn1_w_in, v_ffn1_w_out, v_mix_norm, v_w_mix_in, v_conv_w, v_w_conv_out, v_w_attn_out,
          v_w_mix_out, v_ffn2_norm, v_ffn2_w_in, v_ffn2_w_out, v_ple_norm, v_w_ple_gate, v_w_ple_proj, v_final_norm)
    assert x.shape[0] == 1 and p.shape[:2] == (1, 1), "one sequence and one layer per device"

    def strip(a):
        return a[0] if a.ndim == 3 or (a.ndim == 2 and a.shape[0] == 1) else a

    w = {k: strip(a) for k, a in zip(WEIGHTS, ws)}
    m = {k: strip(a) for k, a in zip(WEIGHTS, ms)}
    v = {k: strip(a) for k, a in zip(WEIGHTS, vs)}
    loss, dx, grad, delta, new_m, new_v = _step(x[0], p[0, 0], loss_target[0], w, m, v)
    shapes = [a.shape for a in ws]
    outs = [loss, dx[None]]
    for res in (grad, delta, new_m, new_v):
        outs += [res[k].reshape(s) for k, s in zip(WEIGHTS, shapes)]
    return tuple(outs)
```

```python
import functools
import math

import jax
import jax.numpy as jnp
from jax import lax
from jax.experimental import pallas as pl
from jax.experimental.pallas import tpu as pltpu

F32 = jnp.float32
BF16 = jnp.bfloat16
MESH = pl.DeviceIdType.MESH
ANY = pl.BlockSpec(memory_space=pl.ANY)

HEAD_DIM = 128
NORM_EPS = 1e-6
N_CHIPS = 4
N_DEV = 8
BF16_ROWS = 16
VMEM_LIMIT = 56 * 1024 * 1024
ACC_BYTES = 8 * 1024 * 1024
STICK_EXIT = 110.0

ADAM_LR = 0.001
ADAM_B1 = 0.9
ADAM_B2 = 0.999
ADAM_EPS = 1e-08
ADAM_WD = 0.01
ADAM_STEP = 10

NN = (((1,), (0,)), ((), ()))
NT = (((1,), (1,)), ((), ()))
TN = (((0,), (0,)), ((), ()))


def _params(sem=None, **kw):
    if sem is not None:
        kw["dimension_semantics"] = sem
    return pltpu.CompilerParams(vmem_limit_bytes=VMEM_LIMIT, **kw)


def _pcall(body, **kw):
    return pl.pallas_call(body, **kw)


def _tile(n, pref, mult=8):
    best = None
    for d in range(mult, min(n, pref) + 1, mult):
        if n % d == 0:
            best = d
    return best if best is not None else n


def _dot(a, b, dims):
    return lax.dot_general(a, b, dims, preferred_element_type=F32)


def _call(name, body, grid, in_specs, out_specs, out_shape, args, scratch=(), sem=None, comm=None):
    n_in, n_out, n_sc = len(in_specs), len(out_specs), len(scratch)
    if comm is None:
        def plain(*refs):
            body(refs[:n_in], refs[n_in:n_in + n_out], refs[n_in + n_out:])

        return _pcall(plain, name=name, out_shape=list(out_shape), grid=grid, in_specs=list(in_specs),
                      out_specs=list(out_specs), scratch_shapes=list(scratch), compiler_params=_params(sem))(*args)
    n_cin, n_cout = len(comm.ins), len(comm.outs)
    steps = math.prod(grid)

    def hosted(*refs):
        ins, c_ins = refs[:n_in], refs[n_in:n_in + n_cin]
        outs = refs[n_in + n_cin:n_in + n_cin + n_out]
        c_outs = refs[n_in + n_cin + n_out:n_in + n_cin + n_out + n_cout]
        rest = refs[n_in + n_cin + n_out + n_cout:]
        sems = rest[n_sc:]
        step = pl.program_id(0)
        for ax in range(1, len(grid)):
            step = step * grid[ax] + pl.program_id(ax)

        @pl.when(step == 0)
        def _():
            comm.first(c_ins, c_outs, sems)

        body(ins, outs, rest[:n_sc])

        @pl.when(step == (3 * steps) // 4)
        def _():
            comm.mid(c_ins, c_outs, sems)

        @pl.when(step == steps - 1)
        def _():
            comm.last(c_ins, c_outs, sems)

    res = _pcall(hosted, name=name, out_shape=list(out_shape) + comm.outs, grid=grid,
                 in_specs=list(in_specs) + [ANY] * n_cin, out_specs=list(out_specs) + [ANY] * n_cout,
                 input_output_aliases={n_in + k: n_out + v for k, v in comm.aliases.items()},
                 scratch_shapes=list(scratch) + comm.sems,
                 compiler_params=_params(("arbitrary",) * len(grid)))(*args, *comm.ins)
    return list(res[:n_out]), list(res[n_out:])


NORM_CHUNK = 256


def _norm_bwd_tile(read_dn, rows, first, h_ref, g_ref, dr_ref, dh_ref, dhb_ref, dg_ref, alpha):
    @pl.when(first)
    def _():
        dg_ref[...] = jnp.zeros_like(dg_ref)

    gv = g_ref[...]
    tot = jnp.zeros_like(gv)
    for c0 in range(0, rows, NORM_CHUNK):
        sl = slice(c0, min(rows, c0 + NORM_CHUNK))
        hv = h_ref[sl, :]
        rs = _rstd(hv)
        hn = hv * rs
        dnv = read_dn(sl)
        gy = dnv * gv
        dh = dr_ref[sl, :] + rs * (gy - hn * jnp.mean(gy * hn, axis=-1, keepdims=True))
        dh_ref[sl, :] = dh
        dhb_ref[sl, :] = (alpha * dh).astype(BF16)
        tot = tot + jnp.sum(dnv * hn, axis=0, keepdims=True)
    dg_ref[...] += tot


def _mm(name, a, b, out_sds, grid, a_spec, b_spec, o_spec, dims, acc_shape, res=None, alpha=1.0, comm=None,
        norm=None, gain=None):
    nk = grid[2]

    def body(ins, outs, scratch):
        a_ref, b_ref = ins[:2]
        r_ref = ins[2] if res is not None else None
        o_ref = outs[0]
        if gain is not None:
            n_ref = scratch[-1]

            @pl.when(jnp.logical_and(pl.program_id(1) == 0, pl.program_id(2) == 0))
            def _():
                hv = a_ref[...]
                n_ref[...] = (hv * _rstd(hv) * ins[-1][...]).astype(BF16)
                outs[-1][...] = n_ref[...]

            a_ref = n_ref

        def finish(read):
            if norm is not None:
                first = jnp.logical_and(pl.program_id(0) == 0, pl.program_id(1) == 0)
                _norm_bwd_tile(read, o_ref.shape[0], first, *ins[2:5], *outs, alpha)
                return
            r = read(slice(None))
            if alpha != 1.0:
                r = r * alpha
            if r_ref is not None:
                r = r_ref[...] + r
            if len(o_ref.shape) == 3:
                half = o_ref.shape[1]
                o_ref[0] = r[:half].astype(o_ref.dtype)
                o_ref[1] = r[half:].astype(o_ref.dtype)
            else:
                o_ref[...] = r.astype(o_ref.dtype)

        if nk == 1:
            part = _dot(a_ref[...].astype(BF16), b_ref[...].astype(BF16), dims)
            finish(lambda sl: part[sl])
        else:
            acc_ref = scratch[0]
            kk = pl.program_id(2)

            @pl.when(kk == 0)
            def _():
                acc_ref[...] = jnp.zeros_like(acc_ref)

            acc_ref[...] += _dot(a_ref[...].astype(BF16), b_ref[...].astype(BF16), dims)

            @pl.when(kk == nk - 1)
            def _():
                finish(lambda sl: acc_ref[sl, :])

    in_specs = [a_spec, b_spec]
    args = [a, b]
    out_specs, out_shape = [o_spec], [out_sds]
    sem = ("parallel", "parallel", "arbitrary")
    if res is not None:
        in_specs.append(o_spec)
        args.append(res)
    if norm is not None:
        width = out_sds.shape[1]
        whole = pl.BlockSpec((1, width), lambda i, j, r: (0, 0))
        in_specs += [o_spec, whole, o_spec]
        args += list(norm)
        out_specs = [o_spec, o_spec, whole]
        out_shape = [jax.ShapeDtypeStruct(out_sds.shape, F32), jax.ShapeDtypeStruct(out_sds.shape, BF16),
                     jax.ShapeDtypeStruct((1, width), F32)]
        sem = ("arbitrary", "arbitrary", "arbitrary")
    scratch = [] if nk == 1 else [pltpu.VMEM(acc_shape, F32)]
    if gain is not None:
        in_specs.append(pl.BlockSpec((1, a.shape[1]), lambda i, j, r: (0, 0)))
        args.append(gain)
        out_specs.append(a_spec)
        out_shape.append(jax.ShapeDtypeStruct(a.shape, BF16))
        scratch.append(pltpu.VMEM(a_spec.block_shape, BF16))
        sem = ("parallel", "arbitrary", "arbitrary")
    got = _call(name, body, grid, in_specs, out_specs, out_shape, args, scratch, sem, comm)
    if norm is not None or gain is not None:
        return got if comm is None else (got[0], got[1])
    return got[0] if comm is None else (got[0][0], got[1])


def ffn_in_act(name, n, w4, tm, comm=None, gain=None):
    t, d = n.shape
    cs = w4.shape[2]

    def body(ins, outs, scratch):
        wg_ref, wu_ref = ins[-2:]
        a_ref, s_ref = outs[:2]
        if gain is None:
            nv = ins[0][...]
        else:
            hv = ins[0][...]
            nv = (hv * _rstd(hv) * ins[1][...]).astype(BF16)

            @pl.when(pl.program_id(0) == 0)
            def _():
                outs[2][...] = nv
        gate = _dot(nv, wg_ref[...], NN)
        up = _dot(nv, wu_ref[...], NN)
        a_ref[0] = gate.astype(BF16)
        a_ref[1] = up.astype(BF16)
        s_ref[...] = (gate * jax.nn.sigmoid(gate) * up).astype(BF16)

    rows = pl.BlockSpec((tm, d), lambda j, i: (i, 0))
    in_specs = [rows] + ([] if gain is None else [pl.BlockSpec((1, d), lambda j, i: (0, 0))])
    in_specs += [pl.BlockSpec((None, d, cs), lambda j, i: (j, 0, 0)),
                 pl.BlockSpec((None, d, cs), lambda j, i: (2 + j, 0, 0))]
    out_specs = [pl.BlockSpec((2, tm, cs), lambda j, i: (0, i, j)), pl.BlockSpec((tm, cs), lambda j, i: (i, j))]
    out_shape = [jax.ShapeDtypeStruct((2, t, 2 * cs), BF16), jax.ShapeDtypeStruct((t, 2 * cs), BF16)]
    if gain is not None:
        out_specs.append(pl.BlockSpec((tm, d), lambda j, i: (jnp.where(j == 0, i, t // tm - 1), 0)))
        out_shape.append(jax.ShapeDtypeStruct((t, d), BF16))
    got = _call(name, body, (2, t // tm), in_specs, out_specs, out_shape,
                [n] + ([] if gain is None else [gain]) + [w4, w4], (), ("arbitrary", "arbitrary"), comm)
    return got if comm is None else (got[0], got[1])


def ffn_ds_dact(name, df, w_out, a3, tm):
    t, d = df.shape
    f = w_out.shape[0]
    cs = f // 2

    def body(ins, outs, scratch):
        df_ref, w_ref, a_ref = ins
        ds = _dot(df_ref[...], w_ref[...], NT)
        for c0 in range(0, tm, NORM_CHUNK):
            sl = slice(c0, min(tm, c0 + NORM_CHUNK))
            gate = a_ref[0, sl, :].astype(F32)
            up = a_ref[1, sl, :].astype(F32)
            sg = jax.nn.sigmoid(gate)
            outs[0][0, sl, :] = (ds[sl] * up * sg * (1.0 + gate * (1.0 - sg))).astype(BF16)
            outs[0][1, sl, :] = (ds[sl] * gate * sg).astype(BF16)

    blk = pl.BlockSpec((2, tm, cs), lambda i, j: (0, i, j))
    return _call(name, body, (t // tm, 2),
                 [pl.BlockSpec((tm, d), lambda i, j: (i, 0)), pl.BlockSpec((cs, d), lambda i, j: (j, 0)), blk],
                 [blk], [jax.ShapeDtypeStruct((2, t, f), BF16)], [df, w_out, a3], (), ("parallel", "parallel"))[0]


def _part_ranges(parts, d):
    out, lo = [], 0
    for p in parts:
        out.append((lo, p.shape[1] // d))
        lo += p.shape[1] // d
    return out, lo


def mm_nt_parts(name, parts, w4, tm, norm, alpha, comm=None):
    m = parts[0].shape[0]
    d, cs = w4.shape[1], w4.shape[2]
    per = cs // d
    ranges, nblk = _part_ranges(parts, d)
    np_ = len(parts)
    nt = m // tm
    chunk = tm // nblk

    def body(ins, outs, scratch):
        w_ref, acc = ins[np_], scratch[0]
        i, r = pl.program_id(0), pl.program_id(1)

        @pl.when(jnp.logical_and(i < nt, r == 0))
        def _():
            acc[i % 2] = jnp.zeros(acc.shape[1:], F32)

        for (lo, n), a_ref in zip(ranges, ins[:np_]):
            @pl.when(jnp.logical_and(i < nt, jnp.logical_and(r >= lo, r < lo + n)))
            def _(a_ref=a_ref):
                acc[i % 2] += _dot(a_ref[...], w_ref[...], NT)

        @pl.when(i > 0)
        def _():
            rows = pl.ds(pl.multiple_of(r * chunk, chunk), chunk)
            first = jnp.logical_and(i == 1, r == 0)
            _norm_bwd_tile(lambda sl: acc[(i - 1) % 2, rows, :][sl], chunk, first, *ins[np_ + 1:], *outs, alpha)

    def ahead(i, r):
        return jnp.where(i < nt, r, nblk - 1)

    rows = pl.BlockSpec((chunk, d), lambda i, r: (jnp.where(i == 0, 0, (i - 1) * nblk + r), 0))
    whole = pl.BlockSpec((1, d), lambda i, r: (0, 0))
    specs = [pl.BlockSpec((tm, d), lambda i, r, lo=lo, n=n: (jnp.minimum(i, nt - 1), jnp.clip(ahead(i, r) - lo, 0, n - 1)))
             for lo, n in ranges]
    specs += [pl.BlockSpec((None, d, d), lambda i, r: (ahead(i, r) // per, 0, ahead(i, r) % per)), rows, whole, rows]
    got = _call(name, body, (nt + 1, nblk), specs, [rows, rows, whole],
                [jax.ShapeDtypeStruct((m, d), F32), jax.ShapeDtypeStruct((m, d), BF16),
                 jax.ShapeDtypeStruct((1, d), F32)],
                list(parts) + [w4] + list(norm), [pltpu.VMEM((2, tm, d), F32)], ("arbitrary", "arbitrary"), comm)
    return got if comm is None else (got[0], got[1])


def mm_tn_parts(name, xa, parts, tt, comm=None):
    t, k = xa.shape
    d = k
    pr = k // 2
    ranges, nblk = _part_ranges(parts, d)
    per = nblk // N_CHIPS

    def body(ins, outs, scratch):
        x_ref, acc = ins[0], scratch[0]
        jb, r = pl.program_id(0), pl.program_id(1)

        @pl.when(r == 0)
        def _():
            acc[...] = jnp.zeros_like(acc)

        for (lo, n), p_ref in zip(ranges, ins[1:]):
            @pl.when(jnp.logical_and(jb >= lo, jb < lo + n))
            def _(p_ref=p_ref):
                acc[...] += _dot(x_ref[...], p_ref[...], TN)

        @pl.when(r == t // tt - 1)
        def _():
            outs[0][0] = acc[:pr].astype(BF16)
            outs[0][1] = acc[pr:].astype(BF16)

    def part_spec(lo, n):
        return pl.BlockSpec((tt, d), lambda jb, r: (jnp.where(jnp.logical_and(jb >= lo, jb < lo + n), r, 0),
                                                    jnp.clip(jb - lo, 0, n - 1)))

    specs = [pl.BlockSpec((tt, k), lambda jb, r: (r, 0))] + [part_spec(lo, n) for lo, n in ranges]
    got = _call(name, body, (nblk, t // tt), specs,
                [pl.BlockSpec((None, 2, pr, d), lambda jb, r: (jb // per, 0, 0, jb % per))],
                [jax.ShapeDtypeStruct((N_CHIPS, 2, pr, per * d), BF16)], [xa] + list(parts),
                [pltpu.VMEM((k, d), F32)], ("parallel", "arbitrary"), comm)
    return got[0] if comm is None else (got[0][0], got[1])


def mm_nn(name, a, w, out_dtype, tm, res=None, alpha=1.0):
    m, k = a.shape
    n = w.shape[1]
    return _mm(name, a, w, jax.ShapeDtypeStruct((m, n), out_dtype), (m // tm, 1, 1),
               pl.BlockSpec((tm, k), lambda i, j, r: (i, 0)),
               pl.BlockSpec((k, n), lambda i, j, r: (0, 0)),
               pl.BlockSpec((tm, n), lambda i, j, r: (i, 0)), NN, None, res=res, alpha=alpha)


def mm_nn_stacked(name, a, w4, out_dtype, tm, tn, j0=0, nj=None, comm=None, gain=None):
    m, k = a.shape
    cs = w4.shape[2]
    per = cs // tn
    nj = N_CHIPS * per - j0 if nj is None else nj
    return _mm(name, a, w4, jax.ShapeDtypeStruct((m, nj * tn), out_dtype), (m // tm, nj, 1),
               pl.BlockSpec((tm, k), lambda i, j, r: (i, 0)),
               pl.BlockSpec((None, k, tn), lambda i, j, r: ((j + j0) // per, 0, (j + j0) % per)),
               pl.BlockSpec((tm, tn), lambda i, j, r: (i, j)), NN, None, comm=comm, gain=gain)


def mm_nt(name, dy, w, out_dtype, tm, tko, norm=None, alpha=1.0):
    m, n = dy.shape
    k = w.shape[0]
    return _mm(name, dy, w, jax.ShapeDtypeStruct((m, k), out_dtype), (m // tm, k // tko, 1),
               pl.BlockSpec((tm, n), lambda i, j, r: (i, 0)),
               pl.BlockSpec((tko, n), lambda i, j, r: (j, 0)),
               pl.BlockSpec((tm, tko), lambda i, j, r: (i, j)), NT, None, norm=norm, alpha=alpha)


def mm_nt_stacked(name, dy, w4, tm, norm, alpha=1.0, comm=None):
    m = dy.shape[1]
    k, cs = w4.shape[1], w4.shape[2]
    nt = m // tm
    chunk = tm // N_CHIPS

    def body(ins, outs, scratch):
        dy_ref, w_ref, h_ref, g_ref, dr_ref = ins
        dh_ref, dhb_ref, dg_ref = outs
        acc = scratch[0]
        i, r = pl.program_id(0), pl.program_id(1)

        @pl.when(jnp.logical_and(i < nt, r == 0))
        def _():
            acc[i % 2] = jnp.zeros(acc.shape[1:], F32)

        @pl.when(i < nt)
        def _():
            acc[i % 2] += _dot(dy_ref[...], w_ref[...], NT)

        @pl.when(i > 0)
        def _():
            rows = pl.ds(pl.multiple_of(r * chunk, chunk), chunk)
            first = jnp.logical_and(i == 1, r == 0)
            _norm_bwd_tile(lambda sl: acc[(i - 1) % 2, rows, :][sl], chunk, first, h_ref, g_ref, dr_ref,
                           dh_ref, dhb_ref, dg_ref, alpha)

    def behind(i, r):
        return (jnp.where(i == 0, 0, (i - 1) * N_CHIPS + r), 0)

    def ahead(i, r):
        return jnp.where(i < nt, r, N_CHIPS - 1)

    rows = pl.BlockSpec((chunk, k), behind)
    whole = pl.BlockSpec((1, k), lambda i, r: (0, 0))
    got = _call(name, body, (nt + 1, N_CHIPS),
                [pl.BlockSpec((None, tm, cs), lambda i, r: (ahead(i, r) // 2, jnp.minimum(i, nt - 1), ahead(i, r) % 2)),
                 pl.BlockSpec((None, k, cs), lambda i, r: (ahead(i, r), 0, 0)), rows, whole, rows],
                [rows, rows, whole],
                [jax.ShapeDtypeStruct((m, k), F32), jax.ShapeDtypeStruct((m, k), BF16),
                 jax.ShapeDtypeStruct((1, k), F32)],
                [dy, w4] + list(norm), [pltpu.VMEM((2, tm, k), F32)], ("arbitrary", "arbitrary"), comm)
    return got if comm is None else (got[0], got[1])


def mm_tn_rows(name, xa, dy, tt):
    t, k = xa.shape
    n = dy.shape[1]
    tkr = k if k * n * 4 <= ACC_BYTES else k // 2
    return _mm(name, xa, dy, jax.ShapeDtypeStruct((k, n), BF16), (k // tkr, 1, t // tt),
               pl.BlockSpec((tt, tkr), lambda i, j, r: (r, i)),
               pl.BlockSpec((tt, n), lambda i, j, r: (r, 0)),
               pl.BlockSpec((tkr, n), lambda i, j, r: (i, 0)), TN, (tkr, n))


def mm_tn_whole(name, xa, dy, tt):
    t, k = xa.shape
    n = dy.shape[1]
    return _mm(name, xa, dy, jax.ShapeDtypeStruct((k, n), BF16), (1, 1, t // tt),
               pl.BlockSpec((tt, k), lambda i, j, r: (r, 0)),
               pl.BlockSpec((tt, n), lambda i, j, r: (r, 0)),
               pl.BlockSpec((k, n), lambda i, j, r: (0, 0)), TN, (k, n))


def mm_tn_cols(name, xa, dy, tt, comm=None):
    t, k = xa.shape
    pr = k // 2
    if dy.ndim == 3:
        cs = dy.shape[2] // 2
        dy_spec = pl.BlockSpec((None, tt, cs), lambda i, j, r: (j // 2, r, j % 2))
    else:
        cs = dy.shape[1] // N_CHIPS
        dy_spec = pl.BlockSpec((tt, cs), lambda i, j, r: (r, j))
    return _mm(name, xa, dy, jax.ShapeDtypeStruct((N_CHIPS, 2, pr, cs), BF16), (1, N_CHIPS, t // tt),
               pl.BlockSpec((tt, k), lambda i, j, r: (r, 0)), dy_spec,
               pl.BlockSpec((None, 2, pr, cs), lambda i, j, r: (j, 0, 0, 0)), TN, (k, cs), comm=comm)


def _rows(tt, w, col=0):
    return pl.BlockSpec((tt, w), lambda i: (i, col))


def _whole(shape):
    return pl.BlockSpec(shape, lambda i: (0,) * len(shape))


def _rstd(h):
    return lax.rsqrt(jnp.mean(h * h, axis=-1, keepdims=True) + NORM_EPS)


def rms_fwd(name, h, g, tt, comm=None):
    t, d = h.shape

    def body(ins, outs, scratch):
        hv = ins[0][...]
        outs[0][...] = (hv * _rstd(hv) * ins[1][...]).astype(BF16)

    got = _call(name, body, (t // tt,), [_rows(tt, d), _whole((1, d))], [_rows(tt, d)],
                [jax.ShapeDtypeStruct((t, d), BF16)], [h, g], (), ("parallel",), comm)
    return got[0] if comm is None else (got[0][0], got[1])


def mix_out_fwd(name, gates, yc, ya, h, w, tt, gcol=0):
    t, d = yc.shape

    def body(g_ref, yc_ref, ya_ref, h_ref, w_ref, m_ref, o_ref):
        merged = (jax.nn.sigmoid(g_ref[:, :d].astype(F32)) * yc_ref[...].astype(F32)
                  + jax.nn.sigmoid(g_ref[:, d:].astype(F32)) * ya_ref[...].astype(F32)).astype(BF16)
        m_ref[...] = merged
        o_ref[...] = h_ref[...] + _dot(merged, w_ref[...], NN)

    return _pcall(body, name=name,
                  out_shape=(jax.ShapeDtypeStruct((t, d), BF16), jax.ShapeDtypeStruct((t, d), F32)),
                  grid=(t // tt,),
                  in_specs=[_rows(tt, 2 * d, gcol), _rows(tt, d), _rows(tt, d), _rows(tt, d), _whole((d, d))],
                  out_specs=(_rows(tt, d), _rows(tt, d)),
                  compiler_params=_params(("parallel",)))(gates, yc, ya, h, w)


def mix_out_bwd(name, dh, w, gates, yc, ya, tt, gcol=0):
    t, d = yc.shape

    def body(dh_ref, w_ref, g_ref, yc_ref, ya_ref, dyc_ref, dya_ref, dg_ref):
        dmv = _dot(dh_ref[...], w_ref[...], NT)
        sc = jax.nn.sigmoid(g_ref[:, :d].astype(F32))
        sa = jax.nn.sigmoid(g_ref[:, d:].astype(F32))
        dyc_ref[...] = (dmv * sc).astype(BF16)
        dya_ref[...] = (dmv * sa).astype(BF16)
        dg_ref[:, :d] = (dmv * yc_ref[...].astype(F32) * sc * (1.0 - sc)).astype(BF16)
        dg_ref[:, d:] = (dmv * ya_ref[...].astype(F32) * sa * (1.0 - sa)).astype(BF16)

    return _pcall(body, name=name,
                  out_shape=(jax.ShapeDtypeStruct((t, d), BF16), jax.ShapeDtypeStruct((t, d), BF16),
                             jax.ShapeDtypeStruct((t, 2 * d), BF16)),
                  grid=(t // tt,),
                  in_specs=[_rows(tt, d), _whole((d, d)), _rows(tt, 2 * d, gcol), _rows(tt, d), _rows(tt, d)],
                  out_specs=(_rows(tt, d), _rows(tt, d), _rows(tt, 2 * d)),
                  compiler_params=_params(("parallel",)))(dh, w, gates, yc, ya)


def _shift_down(cur, prev8, s):
    tt = cur.shape[0]
    rolled = pltpu.roll(cur, s, 0)
    row8 = lax.broadcasted_iota(jnp.int32, prev8.shape, 0)
    first8 = jnp.where(row8 < s, pltpu.roll(prev8, s, 0), rolled[:8])
    return jnp.concatenate([first8, rolled[8:]], axis=0) if tt > 8 else first8


def _shift_up(cur, next8, s):
    tt = cur.shape[0]
    rolled = pltpu.roll(cur, tt - s, 0)
    row8 = lax.broadcasted_iota(jnp.int32, next8.shape, 0)
    last8 = jnp.where(row8 >= 8 - s, pltpu.roll(next8, 8 - s, 0), rolled[tt - 8:])
    return jnp.concatenate([rolled[:tt - 8], last8], axis=0) if tt > 8 else last8


def _prev_rows(tt, d, col):
    return pl.BlockSpec((BF16_ROWS, d), lambda i: (jnp.maximum(i * (tt // BF16_ROWS) - 1, 0), col))


def _next_rows(tt, d, col, t):
    return pl.BlockSpec((BF16_ROWS, d),
                        lambda i: (jnp.minimum((i + 1) * (tt // BF16_ROWS), t // BF16_ROWS - 1), col))


def conv_out_fwd(name, cbx, cw8, w_out, tt):
    t = cbx.shape[0]
    d = w_out.shape[0]
    d3 = 3 * d

    def body(cb_ref, cc_ref, cx_ref, pc_ref, px_ref, w_ref, wo_ref, o_ref, y_ref):
        has_prev = (pl.program_id(0) > 0).astype(F32)
        cc = cc_ref[...].astype(F32) * cx_ref[...].astype(F32)
        prev = pc_ref[...].astype(F32)[8:] * px_ref[...].astype(F32)[8:] * has_prev
        w = w_ref[...]
        conv = w[0:1] * _shift_down(cc, prev, 2) + w[1:2] * _shift_down(cc, prev, 1) + w[2:3] * cc
        ycin = (cb_ref[...].astype(F32) * conv).astype(BF16)
        o_ref[...] = ycin
        y_ref[...] = _dot(ycin, wo_ref[...], NN).astype(BF16)

    out = jax.ShapeDtypeStruct((t, d), BF16)
    return _pcall(body, name=name, out_shape=(out, out), grid=(t // tt,),
                  in_specs=[_rows(tt, d, 0), _rows(tt, d, 1), _rows(tt, d, 2), _prev_rows(tt, d, 1),
                            _prev_rows(tt, d, 2), _whole((8, d)), _whole((d, d))],
                  out_specs=(_rows(tt, d), _rows(tt, d)),
                  compiler_params=_params(("parallel",)))(cbx, cbx, cbx, cbx, cbx, cw8, w_out)


def conv_out_bwd(name, dyc, w_out, cbx, cw8, tt):
    t = cbx.shape[0]
    d = w_out.shape[0]
    d3 = 3 * d
    n = t // tt

    def body(dy_ref, ndy_ref, wo_ref, cb_ref, cc_ref, cx_ref, pc_ref, px_ref, ncb_ref, w_ref, o_ref, dw_ref):
        i = pl.program_id(0)
        has_prev = (i > 0).astype(F32)
        has_next = (i < n - 1).astype(F32)
        cb = cb_ref[...].astype(F32)
        ccv = cc_ref[...].astype(F32)
        cxv = cx_ref[...].astype(F32)
        cc = ccv * cxv
        prev = pc_ref[...].astype(F32)[8:] * px_ref[...].astype(F32)[8:] * has_prev
        w = w_ref[...]
        cc1 = _shift_down(cc, prev, 1)
        cc2 = _shift_down(cc, prev, 2)
        conv = w[0:1] * cc2 + w[1:2] * cc1 + w[2:3] * cc
        dyv = _dot(dy_ref[...], wo_ref[...], NT)
        dconv = dyv * cb
        dnext = _dot(ndy_ref[...], wo_ref[...], NT)[:8] * ncb_ref[...].astype(F32)[:8] * has_next
        dcc = w[2:3] * dconv + w[1:2] * _shift_up(dconv, dnext, 1) + w[0:1] * _shift_up(dconv, dnext, 2)
        o_ref[:, :d] = (dyv * conv).astype(BF16)
        o_ref[:, d:2 * d] = (dcc * cxv).astype(BF16)
        o_ref[:, 2 * d:] = (dcc * ccv).astype(BF16)

        @pl.when(i == 0)
        def _():
            dw_ref[...] = jnp.zeros_like(dw_ref)

        dw_ref[0:1, :] += jnp.sum(dconv * cc2, axis=0, keepdims=True)
        dw_ref[1:2, :] += jnp.sum(dconv * cc1, axis=0, keepdims=True)
        dw_ref[2:3, :] += jnp.sum(dconv * cc, axis=0, keepdims=True)

    return _pcall(body, name=name,
                  out_shape=(jax.ShapeDtypeStruct((t, d3), BF16), jax.ShapeDtypeStruct((8, d), F32)),
                  grid=(n,),
                  in_specs=[_rows(tt, d), _next_rows(tt, d, 0, t),
                            _whole((d, d)), _rows(tt, d, 0), _rows(tt, d, 1), _rows(tt, d, 2),
                            _prev_rows(tt, d, 1), _prev_rows(tt, d, 2), _next_rows(tt, d, 0, t), _whole((8, d))],
                  out_specs=(_rows(tt, d3), _whole((8, d))),
                  compiler_params=_params(("arbitrary",)))(dyc, dyc, w_out, cbx, cbx, cbx, cbx, cbx, cbx, cw8)


def tail(name, h3, p, tgt, gp, gf, w_gate, w_proj, tt):
    t, d = h3.shape
    pd = p.shape[1]

    def body(h_ref, p_ref, tg_ref, gp_ref, gf_ref, wg_ref, wp_ref, np_ref, dh_ref, dpp_ref, dzg_ref, dgf_ref,
             loss_ref):
        hv = h_ref[...]
        npl = (hv * _rstd(hv) * gp_ref[...]).astype(BF16)
        np_ref[...] = npl
        pg = jax.nn.sigmoid(_dot(npl, wg_ref[...], NN))
        ppv = _dot(p_ref[...].astype(BF16), wp_ref[...], NN)
        h4 = hv + pg * ppv
        r4 = _rstd(h4)
        hn = h4 * r4
        gfv = gf_ref[...]
        err = hn * gfv - tg_ref[...]
        dy = err * (1.0 / d)
        gy = dy * gfv
        dh4 = r4 * (gy - hn * jnp.mean(gy * hn, axis=-1, keepdims=True))
        dh_ref[...] = dh4
        dpp_ref[...] = (dh4 * pg).astype(BF16)
        dzg_ref[...] = (dh4 * ppv * pg * (1.0 - pg)).astype(BF16)

        @pl.when(pl.program_id(0) == 0)
        def _():
            dgf_ref[...] = jnp.zeros_like(dgf_ref)
            loss_ref[...] = jnp.zeros_like(loss_ref)

        dgf_ref[...] += jnp.sum(dy * hn, axis=0, keepdims=True)
        tok = jnp.mean(err * err, axis=-1, keepdims=True)
        loss_ref[...] += 0.5 * jnp.sum(tok, axis=0, keepdims=True) * jnp.ones((1, loss_ref.shape[1]), F32)

    return _pcall(body, name=name,
                  out_shape=(jax.ShapeDtypeStruct((t, d), BF16), jax.ShapeDtypeStruct((t, d), F32),
                             jax.ShapeDtypeStruct((t, d), BF16), jax.ShapeDtypeStruct((t, d), BF16),
                             jax.ShapeDtypeStruct((1, d), F32), jax.ShapeDtypeStruct((1, d), F32)),
                  grid=(t // tt,),
                  in_specs=[_rows(tt, d), _rows(tt, pd), _rows(tt, d), _whole((1, d)), _whole((1, d)),
                            _whole((d, d)), _whole((pd, d))],
                  out_specs=(_rows(tt, d), _rows(tt, d), _rows(tt, d), _rows(tt, d), _whole((1, d)),
                             _whole((1, d))),
                  compiler_params=_params(("arbitrary",)))(h3, p, tgt, gp, gf, w_gate, w_proj)


SCALE = 1.0 / math.sqrt(HEAD_DIM)


def _log_stick(z):
    return -(jnp.maximum(z, 0.0) + jnp.log(1.0 + jnp.exp(-jnp.abs(z))))


def _tri_sum(x, tri):
    hi = x.astype(BF16)
    lo = (x - hi.astype(F32)).astype(BF16)
    return _dot(hi, tri, NN) + _dot(lo, tri, NN)


KEY_BLOCK = 128
NEAR = 3
THIN_ROWS = 32


def _pad_block(x):
    n = x.shape[0]
    return x if n == KEY_BLOCK else jnp.concatenate([x, jnp.zeros((KEY_BLOCK - n, x.shape[1]), x.dtype)], axis=0)


def _sb_near(qs, jds, k_ref, below, upper, last_rows):
    near_rows = (KEY_BLOCK,) * (NEAR - 1) + (last_rows,)
    pairs = [(s, b) for s in range(len(qs)) for b in range(NEAR)]
    rows = {(s, b): _block_rows(jnp.maximum(jds[s] - b, 0), KEY_BLOCK) for s, b in pairs}
    z = {(s, b): _dot(qs[s][:near_rows[b]], k_ref[rows[s, b], :], NT) * SCALE for s, b in pairs}
    lg = {(s, b): jnp.where(below, _log_stick(z[s, b]), 0.0) if b == 0 else _log_stick(z[s, b]) for s, b in pairs}
    cum = {(s, b): _tri_sum(lg[s, b], upper) for s, b in pairs}
    out, carries = [], []
    for s in range(len(qs)):
        c = cum[s, 0][:, 0:1]
        blocks = [(rows[s, 0], z[s, 0], jnp.exp(jnp.where(below, z[s, 0] + cum[s, 0], -1e30)))]
        for b in range(1, NEAR):
            live = jds[s] >= b
            off = c[:near_rows[b]] + jnp.where(live, 0.0, -1e30)
            blocks.append((rows[s, b], z[s, b], jnp.exp(z[s, b] + cum[s, b] + off)))
            c = c + _pad_block(jnp.where(live, cum[s, b][:, 0:1], 0.0))
        out.append(blocks)
        carries.append(c)
    return out, carries


def _sb_far(q, kj, upper, c, skip):
    z = _dot(q, kj, NT) * SCALE
    cum = _tri_sum(_log_stick(z), upper)
    return z, jnp.exp(z + cum + (c + jnp.where(skip, -1e30, 0.0))), c + jnp.where(skip, 0.0, cum[:, 0:1])


def _took_it(j, jd, last_rows):
    first = lax.broadcasted_iota(jnp.int32, (KEY_BLOCK, 1), 0) < last_rows
    return jnp.logical_and(j == jd - (NEAR - 1), first)


def _block_rows(j, size):
    return pl.ds(pl.multiple_of(j * size, size), size)


def _sweep_on(st):
    return jnp.logical_and(st[0] >= 0, jnp.max(st[1]) > -STICK_EXIT)


def attn_fwd(name, qkv, tq, d, col0=0):
    t = qkv.shape[0]
    nh = d // HEAD_DIM
    q0 = col0 // HEAD_DIM
    nq = t // tq
    tb = KEY_BLOCK
    nsub = tq // tb

    def body(q_ref, k_ref, v_ref, o_ref):
        i = pl.program_id(1)
        row = lax.broadcasted_iota(jnp.int32, (tb, tb), 0)
        col = lax.broadcasted_iota(jnp.int32, (tb, tb), 1)
        upper = (row >= col).astype(BF16)
        qs = [q_ref[s * tb:(s + 1) * tb, :] for s in range(nsub)]
        jds = [i * nsub + s for s in range(nsub)]
        near, carries = _sb_near(qs, jds, k_ref, col < row, upper, THIN_ROWS)
        state = []
        for s in range(nsub):
            acc = jnp.zeros((tb, HEAD_DIM), F32)
            for rows, _, a in near[s]:
                acc = acc + _pad_block(_dot(a.astype(BF16), v_ref[rows, :], NN))
            state.append((qs[s], jds[s], carries[s], acc))
        for s, (q, jd, c, acc) in enumerate(state):

            def step(st, q=q, jd=jd):
                rows = _block_rows(st[0], tb)
                _, a, c2 = _sb_far(q, k_ref[rows, :], upper, st[1], _took_it(st[0], jd, THIN_ROWS))
                return st[0] - 1, c2, st[2] + _dot(a.astype(BF16), v_ref[rows, :], NN)

            _, _, acc = lax.while_loop(_sweep_on, step, (jd - (NEAR - 1), c, acc))
            o_ref[s * tb:(s + 1) * tb, :] = acc.astype(o_ref.dtype)

    return _pcall(body, name=name, out_shape=jax.ShapeDtypeStruct((t, d), BF16), grid=(nh, nq),
                  in_specs=[pl.BlockSpec((tq, HEAD_DIM), lambda h, i: (i, q0 + h)),
                            pl.BlockSpec((t, HEAD_DIM), lambda h, i: (0, q0 + nh + h)),
                            pl.BlockSpec((t, HEAD_DIM), lambda h, i: (0, q0 + 2 * nh + h))],
                  out_specs=pl.BlockSpec((tq, HEAD_DIM), lambda h, i: (i, h)),
                  compiler_params=_params(("parallel", "arbitrary")))(qkv, qkv, qkv)


def attn_bwd(name, qkv, do, tq, col0=0):
    d = do.shape[1]
    t = qkv.shape[0]
    nh = d // HEAD_DIM
    q0 = col0 // HEAD_DIM
    nq = t // tq
    tb = KEY_BLOCK
    nsub = tq // tb

    def body(q_ref, k_ref, v_ref, do_ref, dq_ref, dk_ref, dv_ref, dk_acc, dv_acc, g_buf, z_buf):
        i = pl.program_id(1)

        @pl.when(i == 0)
        def _():
            dk_acc[...] = jnp.zeros_like(dk_acc)
            dv_acc[...] = jnp.zeros_like(dv_acc)

        row = lax.broadcasted_iota(jnp.int32, (tb, tb), 0)
        col = lax.broadcasted_iota(jnp.int32, (tb, tb), 1)
        below = col < row
        upper = (row >= col).astype(BF16)
        lower = (row <= col).astype(BF16)

        qs = [q_ref[s * tb:(s + 1) * tb, :] for s in range(nsub)]
        dos = [do_ref[s * tb:(s + 1) * tb, :] for s in range(nsub)]
        jds = [i * nsub + s for s in range(nsub)]
        near, carries = _sb_near(qs, jds, k_ref, below, upper, KEY_BLOCK)
        da = [[_dot(dos[s][:a.shape[0]], v_ref[rows, :], NT) for rows, _, a in near[s]] for s in range(nsub)]
        state = []
        for s in range(nsub):
            kept = [(rows, z, da[s][b] * a) for b, (rows, z, a) in enumerate(near[s])]
            for rows, _, a in near[s]:
                dv_acc[rows, :] += _dot(a.astype(BF16), dos[s][:a.shape[0]], TN)
            state.append((qs[s], dos[s], jds[s], carries[s], kept))

        carried = []
        for s, (q, dov, jd, c, kept) in enumerate(state):
            def step(st, s=s, q=q, dov=dov, jd=jd):
                j = st[0]
                rows = _block_rows(j, tb)
                z, a, c2 = _sb_far(q, k_ref[rows, :], upper, st[1], _took_it(j, jd, KEY_BLOCK))
                g_buf[jd - j] = _dot(dov, v_ref[rows, :], NT) * a
                z_buf[jd - j] = z
                dv_acc[rows, :] += _dot(a.astype(BF16), dov, TN)
                return j - 1, c2

            j_stop, _ = lax.while_loop(_sweep_on, step, (jd - (NEAR - 1), c))

            def far(j, st, s=s, q=q, jd=jd):
                run, dq = st
                rows = _block_rows(j, tb)
                g = g_buf[jd - j]
                dz = (g - jax.nn.sigmoid(z_buf[jd - j]) * (run + _tri_sum(g, lower))).astype(BF16)
                dk_acc[rows, :] += _dot(dz, q, TN)
                return run + jnp.sum(g, axis=1, keepdims=True), dq + _dot(dz, k_ref[rows, :], NN)

            carried.append(lax.fori_loop(j_stop + 1, jd - (NEAR - 1) + 1, far,
                                         (jnp.zeros((tb, 1), F32), jnp.zeros((tb, HEAD_DIM), F32))))

        tri = [[_dot(g.astype(BF16), lower, NN) for _, _, g in st[4]] for st in state]
        sig = [[jax.nn.sigmoid(z) for _, z, _ in st[4]] for st in state]
        for s, (q, dov, jd, c, kept) in enumerate(state):
            run, dq = carried[s]
            for b in reversed(range(NEAR)):
                rows, z, g = kept[b]
                n = g.shape[0]
                dz = g - sig[s][b] * (run[:n] + tri[s][b])
                if b == 0:
                    dz = jnp.where(below, dz, 0.0)
                dz = dz.astype(BF16)
                dk_acc[rows, :] += _dot(dz, q[:n], TN)
                dq = dq + _pad_block(_dot(dz, k_ref[rows, :], NN))
                if b:
                    run = run + _pad_block(jnp.sum(g, axis=1, keepdims=True))
            dq_ref[s * tb:(s + 1) * tb, :] = (dq * SCALE).astype(BF16)

        @pl.when(i == nq - 1)
        def _():
            dk_ref[...] = (dk_acc[...] * SCALE).astype(BF16)
            dv_ref[...] = dv_acc[...].astype(BF16)

    blk = pl.BlockSpec((tq, HEAD_DIM), lambda h, i: (i, h))
    col_h = pl.BlockSpec((t, HEAD_DIM), lambda h, i: (0, h))
    out = jax.ShapeDtypeStruct((t, d), BF16)
    return _pcall(body, name=name, out_shape=(out, out, out), grid=(nh, nq),
                  in_specs=[pl.BlockSpec((tq, HEAD_DIM), lambda h, i: (i, q0 + h)),
                            pl.BlockSpec((t, HEAD_DIM), lambda h, i: (0, q0 + nh + h)),
                            pl.BlockSpec((t, HEAD_DIM), lambda h, i: (0, q0 + 2 * nh + h)),
                            blk],
                  out_specs=(blk, col_h, col_h),
                  scratch_shapes=[pltpu.VMEM((t, HEAD_DIM), F32), pltpu.VMEM((t, HEAD_DIM), F32),
                                  pltpu.VMEM((t // tb, tb, tb), F32), pltpu.VMEM((t // tb, tb, tb), F32)],
                  compiler_params=_params(("parallel", "arbitrary")))(qkv, qkv, qkv, do)


def _place():
    x, y, c = lax.axis_index("x"), lax.axis_index("y"), lax.axis_index("c")
    chips = [(1 - x, y), (x, 1 - y), (1 - x, 1 - y)]
    return x, y, c, chips


def _remote(src, dst, send_sem, recv_sem, dev):
    return pltpu.make_async_remote_copy(src_ref=src, dst_ref=dst, send_sem=send_sem, recv_sem=recv_sem,
                                        device_id=dev, device_id_type=MESH)


def place_shards(name, ws, chip):
    tiles, steps = _job_tiles([w.shape for w in ws], 1 << 20, BF16_ROWS)
    nj = len(ws)

    def body(chip_ref, *refs):
        i = pl.program_id(0)
        for k, (_, n) in enumerate(tiles):
            @pl.when(i < n)
            def _(w_ref=refs[k], o_ref=refs[nj + k]):
                o_ref[...] = w_ref[...].astype(BF16)

    spec = pltpu.PrefetchScalarGridSpec(
        num_scalar_prefetch=1, grid=(steps,),
        in_specs=[pl.BlockSpec((tr, w.shape[1]), lambda i, s, n=n: (jnp.minimum(i, n - 1), 0))
                  for w, (tr, n) in zip(ws, tiles)],
        out_specs=[pl.BlockSpec((None, tr, w.shape[1]), lambda i, s, n=n: (s[0], jnp.minimum(i, n - 1), 0))
                   for w, (tr, n) in zip(ws, tiles)])
    return _pcall(body, name=name, out_shape=[jax.ShapeDtypeStruct((N_CHIPS,) + w.shape, BF16) for w in ws],
                  grid_spec=spec, compiler_params=_params(("arbitrary",)))(chip, *ws)


class Comm:
    def __init__(self, ins, outs, aliases, sems, first, mid, last):
        self.ins, self.outs, self.aliases, self.sems = list(ins), list(outs), dict(aliases), list(sems)
        self.first, self.mid, self.last = first, mid, last


def run_comm(name, comm):
    ni, no = len(comm.ins), len(comm.outs)

    def body(*refs):
        ins, outs, sems = refs[:ni], refs[ni:ni + no], refs[ni + no:]
        comm.first(ins, outs, sems)
        comm.mid(ins, outs, sems)
        comm.last(ins, outs, sems)

    return _pcall(body, name=name, out_shape=comm.outs, in_specs=[ANY] * ni, out_specs=[ANY] * no,
                  input_output_aliases=comm.aliases, scratch_shapes=comm.sems, compiler_params=_params())(*comm.ins)


def gather_comm(bufs):
    n = len(bufs)

    def half(out, w, which):
        pr = out[w].shape[1] // 2
        return pl.ds(pl.multiple_of(which * pr, BF16_ROWS), pr)

    def first(ins, out, sems):
        isend, irecv, _, _ = sems
        x, y, c, chips = _place()
        for w in range(n):
            mine = out[w].at[2 * x + y, half(out, w, c)]
            for j, (cx, cy) in enumerate(chips):
                _remote(mine, mine, isend.at[3 * w + j], irecv.at[3 * w + j], (cx, cy, c)).start()

    def mid(ins, out, sems):
        isend, irecv, dsend, drecv = sems
        x, y, c, chips = _place()
        sib = (x, y, 1 - c)
        for w in range(n):
            for j, (cx, cy) in enumerate(chips):
                landed = out[w].at[2 * cx + cy, half(out, w, c)]
                _remote(landed, landed, isend.at[3 * w + j], irecv.at[3 * w + j], sib).wait_recv()
                _remote(landed, landed, dsend.at[3 * w + j], drecv.at[3 * w + j], sib).start()

    def last(ins, out, sems):
        isend, irecv, dsend, drecv = sems
        x, y, c, chips = _place()
        sib = (x, y, 1 - c)
        for w in range(n):
            for j, (cx, cy) in enumerate(chips):
                landed = out[w].at[2 * cx + cy, half(out, w, 1 - c)]
                _remote(landed, landed, dsend.at[3 * w + j], drecv.at[3 * w + j], sib).wait_recv()
        for w in range(n):
            sent = out[w].at[0, half(out, w, c)]
            for j in range(3):
                _remote(sent, sent, isend.at[3 * w + j], irecv.at[3 * w + j], sib).wait_send()
                _remote(sent, sent, dsend.at[3 * w + j], drecv.at[3 * w + j], sib).wait_send()

    return Comm(bufs, [jax.ShapeDtypeStruct(s.shape, s.dtype) for s in bufs], {w: w for w in range(n)},
                [pltpu.SemaphoreType.DMA((3 * n,))] * 4, first, mid, last)


def _nothing(ins, outs, sems):
    return None


def join_comms(a, b):
    ni, no, ns = len(a.ins), len(a.outs), len(a.sems)

    def both(f, g):
        def hook(ins, outs, sems):
            f(ins[:ni], outs[:no], sems[:ns])
            g(ins[ni:], outs[no:], sems[ns:])
        return hook

    aliases = dict(a.aliases)
    aliases.update({ni + k: no + v for k, v in b.aliases.items()})
    return Comm(a.ins + b.ins, a.outs + b.outs, aliases, a.sems + b.sems,
                both(a.first, b.first), both(a.mid, b.mid), both(a.last, b.last))


def exchange_comm(pieces):
    n = len(pieces)

    def copies(src, out, sems):
        x, y, c, _ = _place()
        return [_remote(src[w].at[k, 1 - c], out[w].at[k], sems[0].at[N_CHIPS * w + k], sems[1].at[N_CHIPS * w + k],
                        (x, y, 1 - c)) for w in range(n) for k in range(N_CHIPS)]

    def first(src, out, sems):
        for cp in copies(src, out, sems):
            cp.start()

    def last(src, out, sems):
        for cp in copies(src, out, sems):
            cp.wait()

    return Comm(pieces, [jax.ShapeDtypeStruct((N_CHIPS,) + s.shape[2:], s.dtype) for s in pieces], {},
                [pltpu.SemaphoreType.DMA((N_CHIPS * n,))] * 2, first, _nothing, last)


def scatter_comm(parts):
    n = len(parts)

    def copies(src, out, sems):
        x, y, c, chips = _place()
        return [_remote(src[w].at[2 * cx + cy], out[w].at[j], sems[0].at[3 * w + j], sems[1].at[3 * w + j], (cx, cy, c))
                for w in range(n) for j, (cx, cy) in enumerate(chips)]

    def first(src, out, sems):
        for cp in copies(src, out, sems):
            cp.start()

    def last(src, out, sems):
        for cp in copies(src, out, sems):
            cp.wait()

    return Comm(parts, [jax.ShapeDtypeStruct((3,) + s.shape[1:], s.dtype) for s in parts], {},
                [pltpu.SemaphoreType.DMA((3 * n,))] * 2, first, _nothing, last)


def share_comm(halves):
    n = len(halves)

    def first(ins, buf, sems):
        x, y, c, _ = _place()
        for w in range(n):
            _remote(buf[w].at[c], buf[w].at[c], sems[0].at[w], sems[1].at[w], (x, y, 1 - c)).start()

    def last(ins, buf, sems):
        x, y, c, _ = _place()
        for w in range(n):
            landed = buf[w].at[1 - c]
            _remote(landed, landed, sems[0].at[w], sems[1].at[w], (x, y, 1 - c)).wait_recv()
        for w in range(n):
            _remote(buf[w].at[c], buf[w].at[c], sems[0].at[w], sems[1].at[w], (x, y, 1 - c)).wait_send()

    return Comm(halves, [jax.ShapeDtypeStruct(s.shape, s.dtype) for s in halves], {w: w for w in range(n)},
                [pltpu.SemaphoreType.DMA((n,))] * 2, first, _nothing, last)


def gather_small(name, blk, reduce):
    r, cdim = blk.shape

    def body(in_ref, out_ref, *rest):
        if reduce:
            buf, send_sem, recv_sem = rest
        else:
            buf = out_ref
            send_sem, recv_sem = rest
        x, y, c, _ = _place()
        me = 4 * x + 2 * y + c
        buf[me] = in_ref[...]
        peers = []
        for dx in range(2):
            for dy in range(2):
                for dc in range(2):
                    if dx or dy or dc:
                        peers.append((dx, dy, dc))
        copies = []
        for s, (dx, dy, dc) in enumerate(peers):
            cp = _remote(in_ref, buf.at[me], send_sem.at[s], recv_sem.at[s],
                         ((1 - x if dx else x), (1 - y if dy else y), (1 - c if dc else c)))
            cp.start()
            copies.append(cp)
        for s, (dx, dy, dc) in enumerate(peers):
            px, py, pc_ = (1 - x if dx else x), (1 - y if dy else y), (1 - c if dc else c)
            landed = buf.at[4 * px + 2 * py + pc_]
            _remote(landed, landed, send_sem.at[s], recv_sem.at[s], (x, y, c)).wait_recv()
        for cp in copies:
            cp.wait_send()
        if reduce:
            tot = buf[0]
            for s in range(1, N_DEV):
                tot = tot + buf[s]
            out_ref[...] = tot

    vm = pl.BlockSpec(memory_space=pltpu.VMEM)
    out_shape = jax.ShapeDtypeStruct((r, cdim) if reduce else (N_DEV, r, cdim), F32)
    scratch = ([pltpu.VMEM((N_DEV, r, cdim), F32)] if reduce else []) + [pltpu.SemaphoreType.DMA((N_DEV - 1,))] * 2
    return _pcall(body, name=name, out_shape=out_shape, in_specs=[vm], out_specs=vm, scratch_shapes=scratch,
                  compiler_params=_params())(blk)


def _job_tiles(shapes, tile_bytes, mult):
    tiles = []
    for rows, cols in shapes:
        tr = _tile(rows, max(mult, tile_bytes // (4 * cols)), mult)
        tiles.append((tr, rows // tr))
    return tiles, max(n for _, n in tiles)


def sum_cores(name, owns, gots, place):
    nj = len(owns)
    tiles, _ = _job_tiles([o.shape[2:] for o in owns], 1 << 21, BF16_ROWS)
    steps = max(N_CHIPS * n for _, n in tiles)

    def body(place_ref, *refs):
        i = pl.program_id(0)
        for k, (_, n) in enumerate(tiles):
            @pl.when(i < N_CHIPS * n)
            def _(own_ref=refs[2 * k], got_ref=refs[2 * k + 1], o_ref=refs[2 * nj + k]):
                o_ref[...] = (own_ref[...].astype(F32) + got_ref[...].astype(F32)).astype(o_ref.dtype)

    in_specs, out_specs, out_shape, args = [], [], [], []
    for own, got, (tr, n) in zip(owns, gots, tiles):
        pc = own.shape[3]
        last = N_CHIPS * n - 1
        in_specs += [pl.BlockSpec((None, None, tr, pc),
                                  lambda i, s, n=n, last=last: (jnp.minimum(i, last) // n, s[1], jnp.minimum(i, last) % n, 0)),
                     pl.BlockSpec((None, tr, pc),
                                  lambda i, s, n=n, last=last: (jnp.minimum(i, last) // n, jnp.minimum(i, last) % n, 0))]
        out_specs.append(pl.BlockSpec((None, tr, pc),
                                      lambda i, s, n=n, last=last: (jnp.minimum(i, last) // n, jnp.minimum(i, last) % n, 0)))
        out_shape.append(jax.ShapeDtypeStruct(got.shape, BF16))
        args += [own, got]
    spec = pltpu.PrefetchScalarGridSpec(num_scalar_prefetch=1, grid=(steps,), in_specs=in_specs, out_specs=out_specs)
    return _pcall(body, name=name, out_shape=out_shape, grid_spec=spec,
                  compiler_params=_params(("arbitrary",)))(place, *args)


def sum_chips(name, parts, gots, place):
    nj = len(parts)
    tiles, steps = _job_tiles([p.shape[1:] for p in parts], 1 << 20, BF16_ROWS)

    def body(place_ref, *refs):
        i = pl.program_id(0)
        for k, (_, n) in enumerate(tiles):
            @pl.when(i < n)
            def _(part_ref=refs[2 * k], got_ref=refs[2 * k + 1], o_ref=refs[2 * nj + k]):
                tot = part_ref[...].astype(F32)
                for j in range(3):
                    tot = tot + got_ref[j].astype(F32)
                o_ref[...] = tot

    in_specs, out_specs, out_shape, args = [], [], [], []
    for part, got, (tr, n) in zip(parts, gots, tiles):
        pc = part.shape[2]
        in_specs += [pl.BlockSpec((None, tr, pc), lambda i, s, n=n: (s[0], jnp.minimum(i, n - 1), 0)),
                     pl.BlockSpec((3, tr, pc), lambda i, s, n=n: (0, jnp.minimum(i, n - 1), 0))]
        out_specs.append(pl.BlockSpec((None, tr, pc), lambda i, s, n=n: (s[1], jnp.minimum(i, n - 1), 0)))
        out_shape.append(jax.ShapeDtypeStruct((2,) + part.shape[1:], F32))
        args += [part, got]
    spec = pltpu.PrefetchScalarGridSpec(num_scalar_prefetch=1, grid=(steps,), in_specs=in_specs, out_specs=out_specs)
    return _pcall(body, name=name, out_shape=out_shape, grid_spec=spec,
                  compiler_params=_params(("arbitrary",)))(place, *args)


def adamw(name, jobs):
    c1 = 1.0 / (1.0 - ADAM_B1 ** ADAM_STEP)
    c2 = 1.0 / (1.0 - ADAM_B2 ** ADAM_STEP)
    nj = len(jobs)
    tiles, steps = _job_tiles([j[0].shape for j in jobs], 1 << 18, 8)

    def body(*refs):
        i = pl.program_id(0)
        for k, (_, n) in enumerate(tiles):
            w_ref, g_ref, m_ref, v_ref = refs[4 * k:4 * k + 4]
            d_ref, nm_ref, nv_ref = refs[4 * nj + 3 * k:4 * nj + 3 * k + 3]

            @pl.when(i < n)
            def _(w_ref=w_ref, g_ref=g_ref, m_ref=m_ref, v_ref=v_ref, d_ref=d_ref, nm_ref=nm_ref, nv_ref=nv_ref):
                gv = g_ref[...]
                nm = ADAM_B1 * m_ref[...] + (1.0 - ADAM_B1) * gv
                nv = ADAM_B2 * v_ref[...] + (1.0 - ADAM_B2) * (gv * gv)
                nm_ref[...] = nm
                nv_ref[...] = nv
                d_ref[...] = -ADAM_LR * ((nm * c1) / (jnp.sqrt(nv * c2) + ADAM_EPS) + ADAM_WD * w_ref[...])

    in_specs, out_specs, out_shape, args = [], [], [], []
    for (w, g, m, v), (tr, n) in zip(jobs, tiles):
        spec = pl.BlockSpec((tr, w.shape[1]), lambda i, n=n: (jnp.minimum(i, n - 1), 0))
        in_specs += [spec] * 4
        out_specs += [spec] * 3
        out_shape += [jax.ShapeDtypeStruct(w.shape, F32)] * 3
        args += [w, g, m, v]
    res = _pcall(body, name=name, out_shape=out_shape, grid=(steps,), in_specs=in_specs, out_specs=out_specs,
                 compiler_params=_params(("arbitrary",)))(*args)
    return [tuple(res[3 * k:3 * k + 3]) for k in range(nj)]


MATS = ["ffn1_w_in", "ffn1_w_out", "w_mix_in", "w_conv_out", "w_attn_out", "w_mix_out", "ffn2_w_in", "ffn2_w_out",
        "w_ple_gate", "w_ple_proj"]
COL_SHARDED = {"ffn1_w_in", "w_mix_in", "ffn2_w_in", "w_ple_proj"}
NORMS = ["ffn1_norm", "mix_norm", "ffn2_norm", "ple_norm", "final_norm"]
WEIGHTS = ["ffn1_norm", "ffn1_w_in", "ffn1_w_out", "mix_norm", "w_mix_in", "conv_w", "w_conv_out", "w_attn_out",
           "w_mix_out", "ffn2_norm", "ffn2_w_in", "ffn2_w_out", "ple_norm", "w_ple_gate", "w_ple_proj", "final_norm"]


def _pad_rows(a, rows):
    return jnp.concatenate([a, jnp.zeros((rows - a.shape[0],) + a.shape[1:], a.dtype)], axis=0)


def _step(x, p, tgt, w, m, v):
    t, d = x.shape
    tt = _tile(t, 256)
    tm = _tile(t, 512)
    tm2 = _tile(t, 1024)
    tq = _tile(t, 1024)

    chip = 2 * lax.axis_index("x") + lax.axis_index("y")
    place = jnp.stack([chip, lax.axis_index("c")]).astype(jnp.int32)

    placed = dict(zip(MATS, place_shards("place_shards", [w[k] for k in MATS], place)))
    full = {}

    def keep(names, bufs):
        for k, buf in zip(names, bufs):
            full[k] = buf if k in COL_SHARDED else buf.reshape(-1, buf.shape[2])

    def gather_of(names):
        return gather_comm([placed[k] for k in names])

    cw_all = gather_small("gather_conv_w", _pad_rows(w["conv_w"], 8), False)
    cw8 = jnp.concatenate([cw_all[2 * k] for k in range(N_CHIPS)], axis=1)
    g1, gm, g2, gp, gf = (w[k].reshape(1, d) for k in NORMS)

    def ffn_fwd(tag, h, g, first, w_in_name, w_out_name, riders):
        if first:
            n, bufs = rms_fwd(tag + "_norm", h, g, tt, comm=gather_of(first))
            keep(first, bufs)
            (a, s), bufs = ffn_in_act(tag + "_in", n, full[w_in_name], tm, comm=gather_of(riders))
            keep(riders, bufs)
        else:
            a, s, n = ffn_in_act(tag + "_in", h, full[w_in_name], tm, gain=g)
        return n, a, s, mm_nn(tag + "_out", s, full[w_out_name], F32, tm2, res=h, alpha=0.5)

    n1, a1, s1, h1 = ffn_fwd("ffn1", x, g1, ["ffn1_w_in"], "ffn1_w_in", "ffn1_w_out", ["ffn1_w_out", "w_mix_in"])
    wmix = full["w_mix_in"]
    riders = ["w_conv_out", "w_attn_out", "w_mix_out", "ffn2_w_in", "ffn2_w_out", "w_ple_gate", "w_ple_proj"]
    (mixin, u), bufs = mm_nn_stacked("mix_in", h1, wmix, BF16, tm2, d, comm=gather_of(riders), gain=gm)
    keep(riders, bufs)
    cbx = qkv = gates = mixin
    wpp = full["w_ple_proj"]
    wpp = jnp.transpose(wpp, (1, 0, 2)).reshape(wpp.shape[1], -1)
    ycin, y_conv = conv_out_fwd("conv_out", cbx, cw8, full["w_conv_out"], tt)
    o = attn_fwd("attn", qkv, tq, d, 3 * d)
    y_attn = mm_nn("attn_out", o, full["w_attn_out"], BF16, tm)
    merged, h2 = mix_out_fwd("mix_out", gates, y_conv, y_attn, h1, full["w_mix_out"], tm, 3)
    n2, a2, s2, h3 = ffn_fwd("ffn2", h2, g2, [], "ffn2_w_in", "ffn2_w_out", [])

    pieces, chip_sums, halves = {}, {}, {}

    def as_pieces(k):
        pc = pieces[k]
        return pc if k in COL_SHARDED else pc.reshape(N_CHIPS, 2, pc.shape[0] // (2 * N_CHIPS), pc.shape[1])

    def sum_siblings(tag, names):
        pcs = [as_pieces(k) for k in names]
        got = run_comm("exchange_" + tag, exchange_comm(pcs))
        chip_sums.update(zip(names, sum_cores("sum_cores_" + tag, pcs, got, place)))

    def scatter_of(names):
        return scatter_comm([chip_sums[k] for k in names])

    def sum_landed(tag, names, landed):
        halves.update(zip(names, sum_chips("sum_chips_" + tag, [chip_sums[k] for k in names], landed, place)))

    npl, dh4, dpp, dzg, dgf, loss_row = tail("tail", h3, p, tgt, gp, gf, full["w_ple_gate"], wpp, tm)
    dwpp = mm_tn_whole("ple_proj_dw", p, dpp, tm2)
    pieces["w_ple_proj"] = jnp.transpose(dwpp.reshape(2, p.shape[1] // 2, N_CHIPS, d // N_CHIPS), (2, 0, 1, 3))
    pieces["w_ple_gate"] = mm_tn_rows("ple_gate_dw", npl, dzg, tm2)
    dh3, df2, dgp = mm_nt("ple_gate_dx", dzg, full["w_ple_gate"], F32, tm, d, norm=(h3, gp, dh4), alpha=0.5)
    w_in, w_out = full["ffn2_w_in"], full["ffn2_w_out"]
    pieces["ffn2_w_out"] = mm_tn_rows("ffn2_dwout", s2, df2, tm2)
    da2 = ffn_ds_dact("ffn2_ds", df2, w_out, a2, tm2)
    pieces["ffn2_w_in"] = mm_tn_cols("ffn2_dwin", n2, da2, tm2)
    dh2, dh2b, dg2 = mm_nt_stacked("ffn2_dn", da2, w_in, tm2, (h2, g2, dh3))
    pieces["w_mix_out"] = mm_tn_rows("mix_out_dw", merged, dh2b, tm2)
    dyc, dya, dgates = mix_out_bwd("mix_out_dx", dh2b, full["w_mix_out"], gates, y_conv, y_attn, tm, 3)
    pieces["w_conv_out"] = mm_tn_rows("conv_out_dw", ycin, dyc, tm2)
    dcbx, dcw8 = conv_out_bwd("conv_out_dx", dyc, full["w_conv_out"], cbx, cw8, tt)
    pieces["w_attn_out"] = mm_tn_rows("attn_out_dw", o, dya, tm2)
    do = mm_nt("attn_out_dx", dya, full["w_attn_out"], BF16, tm, d)
    dq, dk, dv = attn_bwd("attn_bwd", qkv, do, tq, 3 * d)
    dmix = [dcbx, dq, dk, dv, dgates]
    early = ["ffn2_w_in", "ffn2_w_out", "w_ple_gate", "w_ple_proj", "w_mix_out", "w_conv_out", "w_attn_out"]
    swap = exchange_comm([as_pieces(k) for k in early])
    pieces["w_mix_in"], got = mm_tn_parts("mix_in_dw", u, dmix, tm2, comm=swap)
    chip_sums.update(zip(early, sum_cores("sum_cores_early", swap.ins, got, place)))
    swap = exchange_comm([as_pieces("w_mix_in")])
    (dh1, df1, dgm), landed = mm_nt_parts("mix_in_dx", dmix, wmix, tm2, (h1, gm, dh2), 0.5,
                                          comm=join_comms(scatter_of(early), swap))
    sum_landed("early", early, landed[:len(early)])
    chip_sums["w_mix_in"] = sum_cores("sum_cores_mix", swap.ins, landed[len(early):], place)[0]
    w_in, w_out = full["ffn1_w_in"], full["ffn1_w_out"]
    pieces["ffn1_w_out"] = mm_tn_rows("ffn1_dwout", s1, df1, tm2)
    da1 = ffn_ds_dact("ffn1_ds", df1, w_out, a1, tm2)
    pieces["ffn1_w_in"], landed = mm_tn_cols("ffn1_dwin", n1, da1, tm2, comm=scatter_of(["w_mix_in"]))
    sum_landed("mix", ["w_mix_in"], landed)
    late = ["ffn1_w_in", "ffn1_w_out"]
    sum_siblings("late", late)
    done = early + ["w_mix_in"]
    (dx, _, dg1), landed = mm_nt_stacked(
        "ffn1_dn", da1, w_in, tm2, (x, g1, dh1),
        comm=join_comms(scatter_of(late), share_comm([halves[k] for k in done])))
    sum_landed("late", late, landed[:len(late)])
    shared = dict(zip(done, landed[len(late):]))

    shared.update(zip(late, run_comm("share_halves", share_comm([halves[k] for k in late]))))
    grad, delta, new_m, new_v = {}, {}, {}, {}
    for k in MATS:
        grad[k] = shared[k].reshape(w[k].shape)

    small = jnp.concatenate([dg1, dgm, dg2, dgp, dgf, dcw8[:3], loss_row, jnp.zeros((7, d), F32)], axis=0)
    tot = gather_small("sum_small", small, True)
    loss = tot[8, 0]
    norm_w = jnp.concatenate([w[k].reshape(1, d) for k in NORMS] + [jnp.zeros((3, d), F32)], axis=0)
    norm_m = jnp.concatenate([m[k].reshape(1, d) for k in NORMS] + [jnp.zeros((3, d), F32)], axis=0)
    norm_v = jnp.concatenate([v[k].reshape(1, d) for k in NORMS] + [jnp.ones((3, d), F32)], axis=0)
    norm_g = jnp.concatenate([tot[0:5], jnp.zeros((3, d), F32)], axis=0)
    cs = d // N_CHIPS
    gcw = lax.dynamic_slice(tot[5:8], (0, chip * cs), (3, cs))
    conv_job = (_pad_rows(w["conv_w"], 8), _pad_rows(gcw, 8), _pad_rows(m["conv_w"], 8),
                jnp.concatenate([v["conv_w"], jnp.ones((5, cs), F32)], axis=0))

    steps = adamw("adamw", [(w[k], grad[k], m[k], v[k]) for k in MATS]
                  + [(norm_w, norm_g, norm_m, norm_v), conv_job])
    for k, res in zip(MATS, steps):
        delta[k], new_m[k], new_v[k] = res
    nd, nm, nv = steps[len(MATS)]
    for r, k in enumerate(NORMS):
        grad[k] = norm_g[r].reshape(w[k].shape)
        delta[k], new_m[k], new_v[k] = (a[r].reshape(w[k].shape) for a in (nd, nm, nv))
    cd, cm, cv = steps[len(MATS) + 1]
    grad["conv_w"], delta["conv_w"], new_m["conv_w"], new_v["conv_w"] = gcw, cd[:3], cm[:3], cv[:3]
    return loss, dx, grad, delta, new_m, new_v


def kernel(x, p, ffn1_norm, ffn1_w_in, ffn1_w_out, mix_norm, w_mix_in, conv_w, w_conv_out, w_attn_out, w_mix_out, ffn2_norm, ffn2_w_in, ffn2_w_out, ple_norm, w_ple_gate, w_ple_proj, final_norm, loss_target, m_ffn1_norm, m_ffn1_w_in, m_ffn1_w_out, m_mix_norm, m_w_mix_in, m_conv_w, m_w_conv_out, m_w_attn_out, m_w_mix_out, m_ffn2_norm, m_ffn2_w_in, m_ffn2_w_out, m_ple_norm, m_w_ple_gate, m_w_ple_proj, m_final_norm, v_ffn1_norm, v_ffn1_w_in, v_ffn1_w_out, v_mix_norm, v_w_mix_in, v_conv_w, v_w_conv_out, v_w_attn_out, v_w_mix_out, v_ffn2_norm, v_ffn2_w_in, v_ffn2_w_out, v_ple_norm, v_w_ple_gate, v_w_ple_proj, v_final_norm):
    ws = (ffn1_norm, ffn1_w_in, ffn1_w_out, mix_norm, w_mix_in, conv_w, w_conv_out, w_attn_out, w_mix_out, ffn2_norm,
          ffn2_w_in, ffn2_w_out, ple_norm, w_ple_gate, w_ple_proj, final_norm)
    ms = (m_ffn1_norm, m_ffn1_w_in, m_ffn1_w_out, m_mix_norm, m_w_mix_in, m_conv_w, m_w_conv_out, m_w_attn_out,
          m_w_mix_out, m_ffn2_norm, m_ffn2_w_in, m_ffn2_w_out, m_ple_norm, m_w_ple_gate, m_w_ple_proj, m_final_norm)
    vs = (v_ffn1_norm, v_ffn1_w_in, v_ffn1_w_out, v_mix_norm, v_w_mix_in, v_conv_w, v_w_conv_out, v_w_attn_out,
          v_w_mix_out, v_ffn2_norm, v_ffn2_w_in, v_ffn2_w_out, v_ple_norm, v_w_ple_gate, v_w_ple_proj, v_final_norm)
    assert x.shape[0] == 1 and p.shape[:2] == (1, 1), "one sequence and one layer per device"

    def strip(a):
        return a[0] if a.ndim == 3 or (a.ndim == 2 and a.shape[0] == 1) else a

    w = {k: strip(a) for k, a in zip(WEIGHTS, ws)}
    m = {k: strip(a) for k, a in zip(WEIGHTS, ms)}
    v = {k: strip(a) for k, a in zip(WEIGHTS, vs)}
    loss, dx, grad, delta, new_m, new_v = _step(x[0], p[0, 0], loss_target[0], w, m, v)
    shapes = [a.shape for a in ws]
    outs = [loss, dx[None]]
    for res in (grad, delta, new_m, new_v):
        outs += [res[k].reshape(s) for k, s in zip(WEIGHTS, shapes)]
    return tuple(outs)
```

```python
import functools
import math

import jax
import jax.numpy as jnp
from jax import lax
from jax.experimental import pallas as pl
from jax.experimental.pallas import tpu as pltpu

F32 = jnp.float32
BF16 = jnp.bfloat16
MESH = pl.DeviceIdType.MESH
ANY = pl.BlockSpec(memory_space=pl.ANY)

HEAD_DIM = 128
NORM_EPS = 1e-6
N_CHIPS = 4
N_DEV = 8
BF16_ROWS = 16
VMEM_LIMIT = 56 * 1024 * 1024
ACC_BYTES = 8 * 1024 * 1024
STICK_EXIT = 110.0

ADAM_LR = 0.001
ADAM_B1 = 0.9
ADAM_B2 = 0.999
ADAM_EPS = 1e-08
ADAM_WD = 0.01
ADAM_STEP = 10

NN = (((1,), (0,)), ((), ()))
NT = (((1,), (1,)), ((), ()))
TN = (((0,), (0,)), ((), ()))


def _params(sem=None, **kw):
    if sem is not None:
        kw["dimension_semantics"] = sem
    return pltpu.CompilerParams(vmem_limit_bytes=VMEM_LIMIT, **kw)


def _pcall(body, **kw):
    return pl.pallas_call(body, **kw)


def _tile(n, pref, mult=8):
    best = None
    for d in range(mult, min(n, pref) + 1, mult):
        if n % d == 0:
            best = d
    return best if best is not None else n


def _dot(a, b, dims):
    return lax.dot_general(a, b, dims, preferred_element_type=F32)


def _call(name, body, grid, in_specs, out_specs, out_shape, args, scratch=(), sem=None, comm=None):
    n_in, n_out, n_sc = len(in_specs), len(out_specs), len(scratch)
    if comm is None:
        def plain(*refs):
            body(refs[:n_in], refs[n_in:n_in + n_out], refs[n_in + n_out:])

        return _pcall(plain, name=name, out_shape=list(out_shape), grid=grid, in_specs=list(in_specs),
                      out_specs=list(out_specs), scratch_shapes=list(scratch), compiler_params=_params(sem))(*args)
    n_cin, n_cout = len(comm.ins), len(comm.outs)
    steps = math.prod(grid)

    def hosted(*refs):
        ins, c_ins = refs[:n_in], refs[n_in:n_in + n_cin]
        outs = refs[n_in + n_cin:n_in + n_cin + n_out]
        c_outs = refs[n_in + n_cin + n_out:n_in + n_cin + n_out + n_cout]
        rest = refs[n_in + n_cin + n_out + n_cout:]
        sems = rest[n_sc:]
        step = pl.program_id(0)
        for ax in range(1, len(grid)):
            step = step * grid[ax] + pl.program_id(ax)

        @pl.when(step == 0)
        def _():
            comm.first(c_ins, c_outs, sems)

        body(ins, outs, rest[:n_sc])

        @pl.when(step == (3 * steps) // 4)
        def _():
            comm.mid(c_ins, c_outs, sems)

        @pl.when(step == steps - 1)
        def _():
            comm.last(c_ins, c_outs, sems)

    res = _pcall(hosted, name=name, out_shape=list(out_shape) + comm.outs, grid=grid,
                 in_specs=list(in_specs) + [ANY] * n_cin, out_specs=list(out_specs) + [ANY] * n_cout,
                 input_output_aliases={n_in + k: n_out + v for k, v in comm.aliases.items()},
                 scratch_shapes=list(scratch) + comm.sems,
                 compiler_params=_params(("arbitrary",) * len(grid)))(*args, *comm.ins)
    return list(res[:n_out]), list(res[n_out:])


NORM_CHUNK = 256


def _norm_bwd_tile(read_dn, rows, first, h_ref, g_ref, dr_ref, dh_ref, dhb_ref, dg_ref, alpha):
    @pl.when(first)
    def _():
        dg_ref[...] = jnp.zeros_like(dg_ref)

    gv = g_ref[...]
    tot = jnp.zeros_like(gv)
    for c0 in range(0, rows, NORM_CHUNK):
        sl = slice(c0, min(rows, c0 + NORM_CHUNK))
        hv = h_ref[sl, :]
        rs = _rstd(hv)
        hn = hv * rs
        dnv = read_dn(sl)
        gy = dnv * gv
        dh = dr_ref[sl, :] + rs * (gy - hn * jnp.mean(gy * hn, axis=-1, keepdims=True))
        dh_ref[sl, :] = dh
        dhb_ref[sl, :] = (alpha * dh).astype(BF16)
        tot = tot + jnp.sum(dnv * hn, axis=0, keepdims=True)
    dg_ref[...] += tot


def _mm(name, a, b, out_sds, grid, a_spec, b_spec, o_spec, dims, acc_shape, res=None, alpha=1.0, comm=None,
        norm=None, gain=None):
    nk = grid[2]

    def body(ins, outs, scratch):
        a_ref, b_ref = ins[:2]
        r_ref = ins[2] if res is not None else None
        o_ref = outs[0]
        if gain is not None:
            n_ref = scratch[-1]

            @pl.when(jnp.logical_and(pl.program_id(1) == 0, pl.program_id(2) == 0))
            def _():
                hv = a_ref[...]
                n_ref[...] = (hv * _rstd(hv) * ins[-1][...]).astype(BF16)
                outs[-1][...] = n_ref[...]

            a_ref = n_ref

        def finish(read):
            if norm is not None:
                first = jnp.logical_and(pl.program_id(0) == 0, pl.program_id(1) == 0)
                _norm_bwd_tile(read, o_ref.shape[0], first, *ins[2:5], *outs, alpha)
                return
            r = read(slice(None))
            if alpha != 1.0:
                r = r * alpha
            if r_ref is not None:
                r = r_ref[...] + r
            if len(o_ref.shape) == 3:
                half = o_ref.shape[1]
                o_ref[0] = r[:half].astype(o_ref.dtype)
                o_ref[1] = r[half:].astype(o_ref.dtype)
            else:
                o_ref[...] = r.astype(o_ref.dtype)

        if nk == 1:
            part = _dot(a_ref[...].astype(BF16), b_ref[...].astype(BF16), dims)
            finish(lambda sl: part[sl])
        else:
            acc_ref = scratch[0]
            kk = pl.program_id(2)

            @pl.when(kk == 0)
            def _():
                acc_ref[...] = jnp.zeros_like(acc_ref)

            acc_ref[...] += _dot(a_ref[...].astype(BF16), b_ref[...].astype(BF16), dims)

            @pl.when(kk == nk - 1)
            def _():
                finish(lambda sl: acc_ref[sl, :])

    in_specs = [a_spec, b_spec]
    args = [a, b]
    out_specs, out_shape = [o_spec], [out_sds]
    sem = ("parallel", "parallel", "arbitrary")
    if res is not None:
        in_specs.append(o_spec)
        args.append(res)
    if norm is not None:
        width = out_sds.shape[1]
        whole = pl.BlockSpec((1, width), lambda i, j, r: (0, 0))
        in_specs += [o_spec, whole, o_spec]
        args += list(norm)
        out_specs = [o_spec, o_spec, whole]
        out_shape = [jax.ShapeDtypeStruct(out_sds.shape, F32), jax.ShapeDtypeStruct(out_sds.shape, BF16),
                     jax.ShapeDtypeStruct((1, width), F32)]
        sem = ("arbitrary", "arbitrary", "arbitrary")
    scratch = [] if nk == 1 else [pltpu.VMEM(acc_shape, F32)]
    if gain is not None:
        in_specs.append(pl.BlockSpec((1, a.shape[1]), lambda i, j, r: (0, 0)))
        args.append(gain)
        out_specs.append(a_spec)
        out_shape.append(jax.ShapeDtypeStruct(a.shape, BF16))
        scratch.append(pltpu.VMEM(a_spec.block_shape, BF16))
        sem = ("parallel", "arbitrary", "arbitrary")
    got = _call(name, body, grid, in_specs, out_specs, out_shape, args, scratch, sem, comm)
    if norm is not None or gain is not None:
        return got if comm is None else (got[0], got[1])
    return got[0] if comm is None else (got[0][0], got[1])


def ffn_in_act(name, n, w4, tm, comm=None, gain=None):
    t, d = n.shape
    cs = w4.shape[2]

    def body(ins, outs, scratch):
        wg_ref, wu_ref = ins[-2:]
        a_ref, s_ref = outs[:2]
        if gain is None:
            nv = ins[0][...]
        else:
            hv = ins[0][...]
            nv = (hv * _rstd(hv) * ins[1][...]).astype(BF16)

            @pl.when(pl.program_id(0) == 0)
            def _():
                outs[2][...] = nv
        gate = _dot(nv, wg_ref[...], NN)
        up = _dot(nv, wu_ref[...], NN)
        a_ref[0] = gate.astype(BF16)
        a_ref[1] = up.astype(BF16)
        s_ref[...] = (gate * jax.nn.sigmoid(gate) * up).astype(BF16)

    rows = pl.BlockSpec((tm, d), lambda j, i: (i, 0))
    in_specs = [rows] + ([] if gain is None else [pl.BlockSpec((1, d), lambda j, i: (0, 0))])
    in_specs += [pl.BlockSpec((None, d, cs), lambda j, i: (j, 0, 0)),
                 pl.BlockSpec((None, d, cs), lambda j, i: (2 + j, 0, 0))]
    out_specs = [pl.BlockSpec((2, tm, cs), lambda j, i: (0, i, j)), pl.BlockSpec((tm, cs), lambda j, i: (i, j))]
    out_shape = [jax.ShapeDtypeStruct((2, t, 2 * cs), BF16), jax.ShapeDtypeStruct((t, 2 * cs), BF16)]
    if gain is not None:
        out_specs.append(pl.BlockSpec((tm, d), lambda j, i: (jnp.where(j == 0, i, t // tm - 1), 0)))
        out_shape.append(jax.ShapeDtypeStruct((t, d), BF16))
    got = _call(name, body, (2, t // tm), in_specs, out_specs, out_shape,
                [n] + ([] if gain is None else [gain]) + [w4, w4], (), ("arbitrary", "arbitrary"), comm)
    return got if comm is None else (got[0], got[1])


def ffn_ds_dact(name, df, w_out, a3, tm):
    t, d = df.shape
    f = w_out.shape[0]
    cs = f // 2

    def body(ins, outs, scratch):
        df_ref, w_ref, a_ref = ins
        ds = _dot(df_ref[...], w_ref[...], NT)
        for c0 in range(0, tm, NORM_CHUNK):
            sl = slice(c0, min(tm, c0 + NORM_CHUNK))
            gate = a_ref[0, sl, :].astype(F32)
            up = a_ref[1, sl, :].astype(F32)
            sg = jax.nn.sigmoid(gate)
            outs[0][0, sl, :] = (ds[sl] * up * sg * (1.0 + gate * (1.0 - sg))).astype(BF16)
            outs[0][1, sl, :] = (ds[sl] * gate * sg).astype(BF16)

    blk = pl.BlockSpec((2, tm, cs), lambda i, j: (0, i, j))
    return _call(name, body, (t // tm, 2),
                 [pl.BlockSpec((tm, d), lambda i, j: (i, 0)), pl.BlockSpec((cs, d), lambda i, j: (j, 0)), blk],
                 [blk], [jax.ShapeDtypeStruct((2, t, f), BF16)], [df, w_out, a3], (), ("parallel", "parallel"))[0]


def _part_ranges(parts, d):
    out, lo = [], 0
    for p in parts:
        out.append((lo, p.shape[1] // d))
        lo += p.shape[1] // d
    return out, lo


def mm_nt_parts(name, parts, w4, tm, norm, alpha, comm=None):
    m = parts[0].shape[0]
    d, cs = w4.shape[1], w4.shape[2]
    per = cs // d
    ranges, nblk = _part_ranges(parts, d)
    np_ = len(parts)
    nt = m // tm
    chunk = tm // nblk

    def body(ins, outs, scratch):
        w_ref, acc = ins[np_], scratch[0]
        i, r = pl.program_id(0), pl.program_id(1)

        @pl.when(jnp.logical_and(i < nt, r == 0))
        def _():
            acc[i % 2] = jnp.zeros(acc.shape[1:], F32)

        for (lo, n), a_ref in zip(ranges, ins[:np_]):
            @pl.when(jnp.logical_and(i < nt, jnp.logical_and(r >= lo, r < lo + n)))
            def _(a_ref=a_ref):
                acc[i % 2] += _dot(a_ref[...], w_ref[...], NT)

        @pl.when(i > 0)
        def _():
            rows = pl.ds(pl.multiple_of(r * chunk, chunk), chunk)
            first = jnp.logical_and(i == 1, r == 0)
            _norm_bwd_tile(lambda sl: acc[(i - 1) % 2, rows, :][sl], chunk, first, *ins[np_ + 1:], *outs, alpha)

    def ahead(i, r):
        return jnp.where(i < nt, r, nblk - 1)

    rows = pl.BlockSpec((chunk, d), lambda i, r: (jnp.where(i == 0, 0, (i - 1) * nblk + r), 0))
    whole = pl.BlockSpec((1, d), lambda i, r: (0, 0))
    specs = [pl.BlockSpec((tm, d), lambda i, r, lo=lo, n=n: (jnp.minimum(i, nt - 1), jnp.clip(ahead(i, r) - lo, 0, n - 1)))
             for lo, n in ranges]
    specs += [pl.BlockSpec((None, d, d), lambda i, r: (ahead(i, r) // per, 0, ahead(i, r) % per)), rows, whole, rows]
    got = _call(name, body, (nt + 1, nblk), specs, [rows, rows, whole],
                [jax.ShapeDtypeStruct((m, d), F32), jax.ShapeDtypeStruct((m, d), BF16),
                 jax.ShapeDtypeStruct((1, d), F32)],
                list(parts) + [w4] + list(norm), [pltpu.VMEM((2, tm, d), F32)], ("arbitrary", "arbitrary"), comm)
    return got if comm is None else (got[0], got[1])


def mm_tn_parts(name, xa, parts, tt, comm=None):
    t, k = xa.shape
    d = k
    pr = k // 2
    ranges, nblk = _part_ranges(parts, d)
    per = nblk // N_CHIPS

    def body(ins, outs, scratch):
        x_ref, acc = ins[0], scratch[0]
        jb, r = pl.program_id(0), pl.program_id(1)

        @pl.when(r == 0)
        def _():
            acc[...] = jnp.zeros_like(acc)

        for (lo, n), p_ref in zip(ranges, ins[1:]):
            @pl.when(jnp.logical_and(jb >= lo, jb < lo + n))
            def _(p_ref=p_ref):
                acc[...] += _dot(x_ref[...], p_ref[...], TN)

        @pl.when(r == t // tt - 1)
        def _():
            outs[0][0] = acc[:pr].astype(BF16)
            outs[0][1] = acc[pr:].astype(BF16)

    def part_spec(lo, n):
        return pl.BlockSpec((tt, d), lambda jb, r: (jnp.where(jnp.logical_and(jb >= lo, jb < lo + n), r, 0),
                                                    jnp.clip(jb - lo, 0, n - 1)))

    specs = [pl.BlockSpec((tt, k), lambda jb, r: (r, 0))] + [part_spec(lo, n) for lo, n in ranges]
    got = _call(name, body, (nblk, t // tt), specs,
                [pl.BlockSpec((None, 2, pr, d), lambda jb, r: (jb // per, 0, 0, jb % per))],
                [jax.ShapeDtypeStruct((N_CHIPS, 2, pr, per * d), BF16)], [xa] + list(parts),
                [pltpu.VMEM((k, d), F32)], ("parallel", "arbitrary"), comm)
    return got[0] if comm is None else (got[0][0], got[1])


def mm_nn(name, a, w, out_dtype, tm, res=None, alpha=1.0):
    m, k = a.shape
    n = w.shape[1]
    return _mm(name, a, w, jax.ShapeDtypeStruct((m, n), out_dtype), (m // tm, 1, 1),
               pl.BlockSpec((tm, k), lambda i, j, r: (i, 0)),
               pl.BlockSpec((k, n), lambda i, j, r: (0, 0)),
               pl.BlockSpec((tm, n), lambda i, j, r: (i, 0)), NN, None, res=res, alpha=alpha)


def mm_nn_stacked(name, a, w4, out_dtype, tm, tn, j0=0, nj=None, comm=None, gain=None):
    m, k = a.shape
    cs = w4.shape[2]
    per = cs // tn
    nj = N_CHIPS * per - j0 if nj is None else nj
    return _mm(name, a, w4, jax.ShapeDtypeStruct((m, nj * tn), out_dtype), (m // tm, nj, 1),
               pl.BlockSpec((tm, k), lambda i, j, r: (i, 0)),
               pl.BlockSpec((None, k, tn), lambda i, j, r: ((j + j0) // per, 0, (j + j0) % per)),
               pl.BlockSpec((tm, tn), lambda i, j, r: (i, j)), NN, None, comm=comm, gain=gain)


def mm_nt(name, dy, w, out_dtype, tm, tko, norm=None, alpha=1.0):
    m, n = dy.shape
    k = w.shape[0]
    return _mm(name, dy, w, jax.ShapeDtypeStruct((m, k), out_dtype), (m // tm, k // tko, 1),
               pl.BlockSpec((tm, n), lambda i, j, r: (i, 0)),
               pl.BlockSpec((tko, n), lambda i, j, r: (j, 0)),
               pl.BlockSpec((tm, tko), lambda i, j, r: (i, j)), NT, None, norm=norm, alpha=alpha)


def mm_nt_stacked(name, dy, w4, tm, norm, alpha=1.0, comm=None):
    m = dy.shape[1]
    k, cs = w4.shape[1], w4.shape[2]
    nt = m // tm
    chunk = tm // N_CHIPS

    def body(ins, outs, scratch):
        dy_ref, w_ref, h_ref, g_ref, dr_ref = ins
        dh_ref, dhb_ref, dg_ref = outs
        acc = scratch[0]
        i, r = pl.program_id(0), pl.program_id(1)

        @pl.when(jnp.logical_and(i < nt, r == 0))
        def _():
            acc[i % 2] = jnp.zeros(acc.shape[1:], F32)

        @pl.when(i < nt)
        def _():
            acc[i % 2] += _dot(dy_ref[...], w_ref[...], NT)

        @pl.when(i > 0)
        def _():
            rows = pl.ds(pl.multiple_of(r * chunk, chunk), chunk)
            first = jnp.logical_and(i == 1, r == 0)
            _norm_bwd_tile(lambda sl: acc[(i - 1) % 2, rows, :][sl], chunk, first, h_ref, g_ref, dr_ref,
                           dh_ref, dhb_ref, dg_ref, alpha)

    def behind(i, r):
        return (jnp.where(i == 0, 0, (i - 1) * N_CHIPS + r), 0)

    def ahead(i, r):
        return jnp.where(i < nt, r, N_CHIPS - 1)

    rows = pl.BlockSpec((chunk, k), behind)
    whole = pl.BlockSpec((1, k), lambda i, r: (0, 0))
    got = _call(name, body, (nt + 1, N_CHIPS),
                [pl.BlockSpec((None, tm, cs), lambda i, r: (ahead(i, r) // 2, jnp.minimum(i, nt - 1), ahead(i, r) % 2)),
                 pl.BlockSpec((None, k, cs), lambda i, r: (ahead(i, r), 0, 0)), rows, whole, rows],
                [rows, rows, whole],
                [jax.ShapeDtypeStruct((m, k), F32), jax.ShapeDtypeStruct((m, k), BF16),
                 jax.ShapeDtypeStruct((1, k), F32)],
                [dy, w4] + list(norm), [pltpu.VMEM((2, tm, k), F32)], ("arbitrary", "arbitrary"), comm)
    return got if comm is None else (got[0], got[1])


def mm_tn_rows(name, xa, dy, tt):
    t, k = xa.shape
    n = dy.shape[1]
    tkr = k if k * n * 4 <= ACC_BYTES else k // 2
    return _mm(name, xa, dy, jax.ShapeDtypeStruct((k, n), BF16), (k // tkr, 1, t // tt),
               pl.BlockSpec((tt, tkr), lambda i, j, r: (r, i)),
               pl.BlockSpec((tt, n), lambda i, j, r: (r, 0)),
               pl.BlockSpec((tkr, n), lambda i, j, r: (i, 0)), TN, (tkr, n))


def mm_tn_whole(name, xa, dy, tt):
    t, k = xa.shape
    n = dy.shape[1]
    return _mm(name, xa, dy, jax.ShapeDtypeStruct((k, n), BF16), (1, 1, t // tt),
               pl.BlockSpec((tt, k), lambda i, j, r: (r, 0)),
               pl.BlockSpec((tt, n), lambda i, j, r: (r, 0)),
               pl.BlockSpec((k, n), lambda i, j, r: (0, 0)), TN, (k, n))


def mm_tn_cols(name, xa, dy, tt, comm=None):
    t, k = xa.shape
    pr = k // 2
    if dy.ndim == 3:
        cs = dy.shape[2] // 2
        dy_spec = pl.BlockSpec((None, tt, cs), lambda i, j, r: (j // 2, r, j % 2))
    else:
        cs = dy.shape[1] // N_CHIPS
        dy_spec = pl.BlockSpec((tt, cs), lambda i, j, r: (r, j))
    return _mm(name, xa, dy, jax.ShapeDtypeStruct((N_CHIPS, 2, pr, cs), BF16), (1, N_CHIPS, t // tt),
               pl.BlockSpec((tt, k), lambda i, j, r: (r, 0)), dy_spec,
               pl.BlockSpec((None, 2, pr, cs), lambda i, j, r: (j, 0, 0, 0)), TN, (k, cs), comm=comm)


def _rows(tt, w, col=0):
    return pl.BlockSpec((tt, w), lambda i: (i, col))


def _whole(shape):
    return pl.BlockSpec(shape, lambda i: (0,) * len(shape))


def _rstd(h):
    return lax.rsqrt(jnp.mean(h * h, axis=-1, keepdims=True) + NORM_EPS)


def rms_fwd(name, h, g, tt, comm=None):
    t, d = h.shape

    def body(ins, outs, scratch):
        hv = ins[0][...]
        outs[0][...] = (hv * _rstd(hv) * ins[1][...]).astype(BF16)

    got = _call(name, body, (t // tt,), [_rows(tt, d), _whole((1, d))], [_rows(tt, d)],
                [jax.ShapeDtypeStruct((t, d), BF16)], [h, g], (), ("parallel",), comm)
    return got[0] if comm is None else (got[0][0], got[1])


def mix_out_fwd(name, gates, yc, ya, h, w, tt, gcol=0):
    t, d = yc.shape

    def body(g_ref, yc_ref, ya_ref, h_ref, w_ref, m_ref, o_ref):
        merged = (jax.nn.sigmoid(g_ref[:, :d].astype(F32)) * yc_ref[...].astype(F32)
                  + jax.nn.sigmoid(g_ref[:, d:].astype(F32)) * ya_ref[...].astype(F32)).astype(BF16)
        m_ref[...] = merged
        o_ref[...] = h_ref[...] + _dot(merged, w_ref[...], NN)

    return _pcall(body, name=name,
                  out_shape=(jax.ShapeDtypeStruct((t, d), BF16), jax.ShapeDtypeStruct((t, d), F32)),
                  grid=(t // tt,),
                  in_specs=[_rows(tt, 2 * d, gcol), _rows(tt, d), _rows(tt, d), _rows(tt, d), _whole((d, d))],
                  out_specs=(_rows(tt, d), _rows(tt, d)),
                  compiler_params=_params(("parallel",)))(gates, yc, ya, h, w)


def mix_out_bwd(name, dh, w, gates, yc, ya, tt, gcol=0):
    t, d = yc.shape

    def body(dh_ref, w_ref, g_ref, yc_ref, ya_ref, dyc_ref, dya_ref, dg_ref):
        dmv = _dot(dh_ref[...], w_ref[...], NT)
        sc = jax.nn.sigmoid(g_ref[:, :d].astype(F32))
        sa = jax.nn.sigmoid(g_ref[:, d:].astype(F32))
        dyc_ref[...] = (dmv * sc).astype(BF16)
        dya_ref[...] = (dmv * sa).astype(BF16)
        dg_ref[:, :d] = (dmv * yc_ref[...].astype(F32) * sc * (1.0 - sc)).astype(BF16)
        dg_ref[:, d:] = (dmv * ya_ref[...].astype(F32) * sa * (1.0 - sa)).astype(BF16)

    return _pcall(body, name=name,
                  out_shape=(jax.ShapeDtypeStruct((t, d), BF16), jax.ShapeDtypeStruct((t, d), BF16),
                             jax.ShapeDtypeStruct((t, 2 * d), BF16)),
                  grid=(t // tt,),
                  in_specs=[_rows(tt, d), _whole((d, d)), _rows(tt, 2 * d, gcol), _rows(tt, d), _rows(tt, d)],
                  out_specs=(_rows(tt, d), _rows(tt, d), _rows(tt, 2 * d)),
                  compiler_params=_params(("parallel",)))(dh, w, gates, yc, ya)


def _shift_down(cur, prev8, s):
    tt = cur.shape[0]
    rolled = pltpu.roll(cur, s, 0)
    row8 = lax.broadcasted_iota(jnp.int32, prev8.shape, 0)
    first8 = jnp.where(row8 < s, pltpu.roll(prev8, s, 0), rolled[:8])
    return jnp.concatenate([first8, rolled[8:]], axis=0) if tt > 8 else first8


def _shift_up(cur, next8, s):
    tt = cur.shape[0]
    rolled = pltpu.roll(cur, tt - s, 0)
    row8 = lax.broadcasted_iota(jnp.int32, next8.shape, 0)
    last8 = jnp.where(row8 >= 8 - s, pltpu.roll(next8, 8 - s, 0), rolled[tt - 8:])
    return jnp.concatenate([rolled[:tt - 8], last8], axis=0) if tt > 8 else last8


def _prev_rows(tt, d, col):
    return pl.BlockSpec((BF16_ROWS, d), lambda i: (jnp.maximum(i * (tt // BF16_ROWS) - 1, 0), col))


def _next_rows(tt, d, col, t):
    return pl.BlockSpec((BF16_ROWS, d),
                        lambda i: (jnp.minimum((i + 1) * (tt // BF16_ROWS), t // BF16_ROWS - 1), col))


def conv_out_fwd(name, cbx, cw8, w_out, tt):
    t = cbx.shape[0]
    d = w_out.shape[0]
    d3 = 3 * d

    def body(cb_ref, cc_ref, cx_ref, pc_ref, px_ref, w_ref, wo_ref, o_ref, y_ref):
        has_prev = (pl.program_id(0) > 0).astype(F32)
        cc = cc_ref[...].astype(F32) * cx_ref[...].astype(F32)
        prev = pc_ref[...].astype(F32)[8:] * px_ref[...].astype(F32)[8:] * has_prev
        w = w_ref[...]
        conv = w[0:1] * _shift_down(cc, prev, 2) + w[1:2] * _shift_down(cc, prev, 1) + w[2:3] * cc
        ycin = (cb_ref[...].astype(F32) * conv).astype(BF16)
        o_ref[...] = ycin
        y_ref[...] = _dot(ycin, wo_ref[...], NN).astype(BF16)

    out = jax.ShapeDtypeStruct((t, d), BF16)
    return _pcall(body, name=name, out_shape=(out, out), grid=(t // tt,),
                  in_specs=[_rows(tt, d, 0), _rows(tt, d, 1), _rows(tt, d, 2), _prev_rows(tt, d, 1),
                            _prev_rows(tt, d, 2), _whole((8, d)), _whole((d, d))],
                  out_specs=(_rows(tt, d), _rows(tt, d)),
                  compiler_params=_params(("parallel",)))(cbx, cbx, cbx, cbx, cbx, cw8, w_out)


def conv_out_bwd(name, dyc, w_out, cbx, cw8, tt):
    t = cbx.shape[0]
    d = w_out.shape[0]
    d3 = 3 * d
    n = t // tt

    def body(dy_ref, ndy_ref, wo_ref, cb_ref, cc_ref, cx_ref, pc_ref, px_ref, ncb_ref, w_ref, o_ref, dw_ref):
        i = pl.program_id(0)
        has_prev = (i > 0).astype(F32)
        has_next = (i < n - 1).astype(F32)
        cb = cb_ref[...].astype(F32)
        ccv = cc_ref[...].astype(F32)
        cxv = cx_ref[...].astype(F32)
        cc = ccv * cxv
        prev = pc_ref[...].astype(F32)[8:] * px_ref[...].astype(F32)[8:] * has_prev
        w = w_ref[...]
        cc1 = _shift_down(cc, prev, 1)
        cc2 = _shift_down(cc, prev, 2)
        conv = w[0:1] * cc2 + w[1:2] * cc1 + w[2:3] * cc
        dyv = _dot(dy_ref[...], wo_ref[...], NT)
        dconv = dyv * cb
        dnext = _dot(ndy_ref[...], wo_ref[...], NT)[:8] * ncb_ref[...].astype(F32)[:8] * has_next
        dcc = w[2:3] * dconv + w[1:2] * _shift_up(dconv, dnext, 1) + w[0:1] * _shift_up(dconv, dnext, 2)
        o_ref[:, :d] = (dyv * conv).astype(BF16)
        o_ref[:, d:2 * d] = (dcc * cxv).astype(BF16)
        o_ref[:, 2 * d:] = (dcc * ccv).astype(BF16)

        @pl.when(i == 0)
        def _():
            dw_ref[...] = jnp.zeros_like(dw_ref)

        dw_ref[0:1, :] += jnp.sum(dconv * cc2, axis=0, keepdims=True)
        dw_ref[1:2, :] += jnp.sum(dconv * cc1, axis=0, keepdims=True)
        dw_ref[2:3, :] += jnp.sum(dconv * cc, axis=0, keepdims=True)

    return _pcall(body, name=name,
                  out_shape=(jax.ShapeDtypeStruct((t, d3), BF16), jax.ShapeDtypeStruct((8, d), F32)),
                  grid=(n,),
                  in_specs=[_rows(tt, d), _next_rows(tt, d, 0, t),
                            _whole((d, d)), _rows(tt, d, 0), _rows(tt, d, 1), _rows(tt, d, 2),
                            _prev_rows(tt, d, 1), _prev_rows(tt, d, 2), _next_rows(tt, d, 0, t), _whole((8, d))],
                  out_specs=(_rows(tt, d3), _whole((8, d))),
                  compiler_params=_params(("arbitrary",)))(dyc, dyc, w_out, cbx, cbx, cbx, cbx, cbx, cbx, cw8)


def tail(name, h3, p, tgt, gp, gf, w_gate, w_proj, tt):
    t, d = h3.shape
    pd = p.shape[1]

    def body(h_ref, p_ref, tg_ref, gp_ref, gf_ref, wg_ref, wp_ref, np_ref, dh_ref, dpp_ref, dzg_ref, dgf_ref,
             loss_ref):
        hv = h_ref[...]
        npl = (hv * _rstd(hv) * gp_ref[...]).astype(BF16)
        np_ref[...] = npl
        pg = jax.nn.sigmoid(_dot(npl, wg_ref[...], NN))
        ppv = _dot(p_ref[...].astype(BF16), wp_ref[...], NN)
        h4 = hv + pg * ppv
        r4 = _rstd(h4)
        hn = h4 * r4
        gfv = gf_ref[...]
        err = hn * gfv - tg_ref[...]
        dy = err * (1.0 / d)
        gy = dy * gfv
        dh4 = r4 * (gy - hn * jnp.mean(gy * hn, axis=-1, keepdims=True))
        dh_ref[...] = dh4
        dpp_ref[...] = (dh4 * pg).astype(BF16)
        dzg_ref[...] = (dh4 * ppv * pg * (1.0 - pg)).astype(BF16)

        @pl.when(pl.program_id(0) == 0)
        def _():
            dgf_ref[...] = jnp.zeros_like(dgf_ref)
            loss_ref[...] = jnp.zeros_like(loss_ref)

        dgf_ref[...] += jnp.sum(dy * hn, axis=0, keepdims=True)
        tok = jnp.mean(err * err, axis=-1, keepdims=True)
        loss_ref[...] += 0.5 * jnp.sum(tok, axis=0, keepdims=True) * jnp.ones((1, loss_ref.shape[1]), F32)

    return _pcall(body, name=name,
                  out_shape=(jax.ShapeDtypeStruct((t, d), BF16), jax.ShapeDtypeStruct((t, d), F32),
                             jax.ShapeDtypeStruct((t, d), BF16), jax.ShapeDtypeStruct((t, d), BF16),
                             jax.ShapeDtypeStruct((1, d), F32), jax.ShapeDtypeStruct((1, d), F32)),
                  grid=(t // tt,),
                  in_specs=[_rows(tt, d), _rows(tt, pd), _rows(tt, d), _whole((1, d)), _whole((1, d)),
                            _whole((d, d)), _whole((pd, d))],
                  out_specs=(_rows(tt, d), _rows(tt, d), _rows(tt, d), _rows(tt, d), _whole((1, d)),
                             _whole((1, d))),
                  compiler_params=_params(("arbitrary",)))(h3, p, tgt, gp, gf, w_gate, w_proj)


SCALE = 1.0 / math.sqrt(HEAD_DIM)


def _log_stick(z):
    return -(jnp.maximum(z, 0.0) + jnp.log(1.0 + jnp.exp(-jnp.abs(z))))


def _tri_sum(x, tri):
    hi = x.astype(BF16)
    lo = (x - hi.astype(F32)).astype(BF16)
    return _dot(hi, tri, NN) + _dot(lo, tri, NN)


KEY_BLOCK = 128
NEAR = 3
THIN_ROWS = 32


def _pad_block(x):
    n = x.shape[0]
    return x if n == KEY_BLOCK else jnp.concatenate([x, jnp.zeros((KEY_BLOCK - n, x.shape[1]), x.dtype)], axis=0)


def _sb_near(qs, jds, k_ref, below, upper, last_rows):
    near_rows = (KEY_BLOCK,) * (NEAR - 1) + (last_rows,)
    pairs = [(s, b) for s in range(len(qs)) for b in range(NEAR)]
    rows = {(s, b): _block_rows(jnp.maximum(jds[s] - b, 0), KEY_BLOCK) for s, b in pairs}
    z = {(s, b): _dot(qs[s][:near_rows[b]], k_ref[rows[s, b], :], NT) * SCALE for s, b in pairs}
    lg = {(s, b): jnp.where(below, _log_stick(z[s, b]), 0.0) if b == 0 else _log_stick(z[s, b]) for s, b in pairs}
    cum = {(s, b): _tri_sum(lg[s, b], upper) for s, b in pairs}
    out, carries = [], []
    for s in range(len(qs)):
        c = cum[s, 0][:, 0:1]
        blocks = [(rows[s, 0], z[s, 0], jnp.exp(jnp.where(below, z[s, 0] + cum[s, 0], -1e30)))]
        for b in range(1, NEAR):
            live = jds[s] >= b
            off = c[:near_rows[b]] + jnp.where(live, 0.0, -1e30)
            blocks.append((rows[s, b], z[s, b], jnp.exp(z[s, b] + cum[s, b] + off)))
            c = c + _pad_block(jnp.where(live, cum[s, b][:, 0:1], 0.0))
        out.append(blocks)
        carries.append(c)
    return out, carries


def _sb_far(q, kj, upper, c, skip):
    z = _dot(q, kj, NT) * SCALE
    cum = _tri_sum(_log_stick(z), upper)
    return z, jnp.exp(z + cum + (c + jnp.where(skip, -1e30, 0.0))), c + jnp.where(skip, 0.0, cum[:, 0:1])


def _took_it(j, jd, last_rows):
    first = lax.broadcasted_iota(jnp.int32, (KEY_BLOCK, 1), 0) < last_rows
    return jnp.logical_and(j == jd - (NEAR - 1), first)


def _block_rows(j, size):
    return pl.ds(pl.multiple_of(j * size, size), size)


def _sweep_on(st):
    return jnp.logical_and(st[0] >= 0, jnp.max(st[1]) > -STICK_EXIT)


def attn_fwd(name, qkv, tq, d, col0=0):
    t = qkv.shape[0]
    nh = d // HEAD_DIM
    q0 = col0 // HEAD_DIM
    nq = t // tq
    tb = KEY_BLOCK
    nsub = tq // tb

    def body(q_ref, k_ref, v_ref, o_ref):
        i = pl.program_id(1)
        row = lax.broadcasted_iota(jnp.int32, (tb, tb), 0)
        col = lax.broadcasted_iota(jnp.int32, (tb, tb), 1)
        upper = (row >= col).astype(BF16)
        qs = [q_ref[s * tb:(s + 1) * tb, :] for s in range(nsub)]
        jds = [i * nsub + s for s in range(nsub)]
        near, carries = _sb_near(qs, jds, k_ref, col < row, upper, THIN_ROWS)
        state = []
        for s in range(nsub):
            acc = jnp.zeros((tb, HEAD_DIM), F32)
            for rows, _, a in near[s]:
                acc = acc + _pad_block(_dot(a.astype(BF16), v_ref[rows, :], NN))
            state.append((qs[s], jds[s], carries[s], acc))
        for s, (q, jd, c, acc) in enumerate(state):

            def step(st, q=q, jd=jd):
                rows = _block_rows(st[0], tb)
                _, a, c2 = _sb_far(q, k_ref[rows, :], upper, st[1], _took_it(st[0], jd, THIN_ROWS))
                return st[0] - 1, c2, st[2] + _dot(a.astype(BF16), v_ref[rows, :], NN)

            _, _, acc = lax.while_loop(_sweep_on, step, (jd - (NEAR - 1), c, acc))
            o_ref[s * tb:(s + 1) * tb, :] = acc.astype(o_ref.dtype)

    return _pcall(body, name=name, out_shape=jax.ShapeDtypeStruct((t, d), BF16), grid=(nh, nq),
                  in_specs=[pl.BlockSpec((tq, HEAD_DIM), lambda h, i: (i, q0 + h)),
                            pl.BlockSpec((t, HEAD_DIM), lambda h, i: (0, q0 + nh + h)),
                            pl.BlockSpec((t, HEAD_DIM), lambda h, i: (0, q0 + 2 * nh + h))],
                  out_specs=pl.BlockSpec((tq, HEAD_DIM), lambda h, i: (i, h)),
                  compiler_params=_params(("parallel", "arbitrary")))(qkv, qkv, qkv)


def attn_bwd(name, qkv, do, tq, col0=0):
    d = do.shape[1]
    t = qkv.shape[0]
    nh = d // HEAD_DIM
    q0 = col0 // HEAD_DIM
    nq = t // tq
    tb = KEY_BLOCK
    nsub = tq // tb

    def body(q_ref, k_ref, v_ref, do_ref, dq_ref, dk_ref, dv_ref, dk_acc, dv_acc, g_buf, z_buf):
        i = pl.program_id(1)

        @pl.when(i == 0)
        def _():
            dk_acc[...] = jnp.zeros_like(dk_acc)
            dv_acc[...] = jnp.zeros_like(dv_acc)

        row = lax.broadcasted_iota(jnp.int32, (tb, tb), 0)
        col = lax.broadcasted_iota(jnp.int32, (tb, tb), 1)
        below = col < row
        upper = (row >= col).astype(BF16)
        lower = (row <= col).astype(BF16)

        qs = [q_ref[s * tb:(s + 1) * tb, :] for s in range(nsub)]
        dos = [do_ref[s * tb:(s + 1) * tb, :] for s in range(nsub)]
        jds = [i * nsub + s for s in range(nsub)]
        near, carries = _sb_near(qs, jds, k_ref, below, upper, KEY_BLOCK)
        da = [[_dot(dos[s][:a.shape[0]], v_ref[rows, :], NT) for rows, _, a in near[s]] for s in range(nsub)]
        state = []
        for s in range(nsub):
            kept = [(rows, z, da[s][b] * a) for b, (rows, z, a) in enumerate(near[s])]
            for rows, _, a in near[s]:
                dv_acc[rows, :] += _dot(a.astype(BF16), dos[s][:a.shape[0]], TN)
            state.append((qs[s], dos[s], jds[s], carries[s], kept))

        carried = []
        for s, (q, dov, jd, c, kept) in enumerate(state):
            def step(st, s=s, q=q, dov=dov, jd=jd):
                j = st[0]
                rows = _block_rows(j, tb)
                z, a, c2 = _sb_far(q, k_ref[rows, :], upper, st[1], _took_it(j, jd, KEY_BLOCK))
                g_buf[jd - j] = _dot(dov, v_ref[rows, :], NT) * a
                z_buf[jd - j] = z
                dv_acc[rows, :] += _dot(a.astype(BF16), dov, TN)
                return j - 1, c2

            j_stop, _ = lax.while_loop(_sweep_on, step, (jd - (NEAR - 1), c))

            def far(j, st, s=s, q=q, jd=jd):
                run, dq = st
                rows = _block_rows(j, tb)
                g = g_buf[jd - j]
                dz = (g - jax.nn.sigmoid(z_buf[jd - j]) * (run + _tri_sum(g, lower))).astype(BF16)
                dk_acc[rows, :] += _dot(dz, q, TN)
                return run + jnp.sum(g, axis=1, keepdims=True), dq + _dot(dz, k_ref[rows, :], NN)

            carried.append(lax.fori_loop(j_stop + 1, jd - (NEAR - 1) + 1, far,
                                         (jnp.zeros((tb, 1), F32), jnp.zeros((tb, HEAD_DIM), F32))))

        tri = [[_dot(g.astype(BF16), lower, NN) for _, _, g in st[4]] for st in state]
        sig = [[jax.nn.sigmoid(z) for _, z, _ in st[4]] for st in state]
        for s, (q, dov, jd, c, kept) in enumerate(state):
            run, dq = carried[s]
            for b in reversed(range(NEAR)):
                rows, z, g = kept[b]
                n = g.shape[0]
                dz = g - sig[s][b] * (run[:n] + tri[s][b])
                if b == 0:
                    dz = jnp.where(below, dz, 0.0)
                dz = dz.astype(BF16)
                dk_acc[rows, :] += _dot(dz, q[:n], TN)
                dq = dq + _pad_block(_dot(dz, k_ref[rows, :], NN))
                if b:
                    run = run + _pad_block(jnp.sum(g, axis=1, keepdims=True))
            dq_ref[s * tb:(s + 1) * tb, :] = (dq * SCALE).astype(BF16)

        @pl.when(i == nq - 1)
        def _():
            dk_ref[...] = (dk_acc[...] * SCALE).astype(BF16)
            dv_ref[...] = dv_acc[...].astype(BF16)

    blk = pl.BlockSpec((tq, HEAD_DIM), lambda h, i: (i, h))
    col_h = pl.BlockSpec((t, HEAD_DIM), lambda h, i: (0, h))
    out = jax.ShapeDtypeStruct((t, d), BF16)
    return _pcall(body, name=name, out_shape=(out, out, out), grid=(nh, nq),
                  in_specs=[pl.BlockSpec((tq, HEAD_DIM), lambda h, i: (i, q0 + h)),
                            pl.BlockSpec((t, HEAD_DIM), lambda h, i: (0, q0 + nh + h)),
                            pl.BlockSpec((t, HEAD_DIM), lambda h, i: (0, q0 + 2 * nh + h)),
                            blk],
                  out_specs=(blk, col_h, col_h),
                  scratch_shapes=[pltpu.VMEM((t, HEAD_DIM), F32), pltpu.VMEM((t, HEAD_DIM), F32),
                                  pltpu.VMEM((t // tb, tb, tb), F32), pltpu.VMEM((t // tb, tb, tb), F32)],
                  compiler_params=_params(("parallel", "arbitrary")))(qkv, qkv, qkv, do)


def _place():
    x, y, c = lax.axis_index("x"), lax.axis_index("y"), lax.axis_index("c")
    chips = [(1 - x, y), (x, 1 - y), (1 - x, 1 - y)]
    return x, y, c, chips


def _remote(src, dst, send_sem, recv_sem, dev):
    return pltpu.make_async_remote_copy(src_ref=src, dst_ref=dst, send_sem=send_sem, recv_sem=recv_sem,
                                        device_id=dev, device_id_type=MESH)


def place_shards(name, ws, chip):
    tiles, steps = _job_tiles([w.shape for w in ws], 1 << 20, BF16_ROWS)
    nj = len(ws)

    def body(chip_ref, *refs):
        i = pl.program_id(0)
        for k, (_, n) in enumerate(tiles):
            @pl.when(i < n)
            def _(w_ref=refs[k], o_ref=refs[nj + k]):
                o_ref[...] = w_ref[...].astype(BF16)

    spec = pltpu.PrefetchScalarGridSpec(
        num_scalar_prefetch=1, grid=(steps,),
        in_specs=[pl.BlockSpec((tr, w.shape[1]), lambda i, s, n=n: (jnp.minimum(i, n - 1), 0))
                  for w, (tr, n) in zip(ws, tiles)],
        out_specs=[pl.BlockSpec((None, tr, w.shape[1]), lambda i, s, n=n: (s[0], jnp.minimum(i, n - 1), 0))
                   for w, (tr, n) in zip(ws, tiles)])
    return _pcall(body, name=name, out_shape=[jax.ShapeDtypeStruct((N_CHIPS,) + w.shape, BF16) for w in ws],
                  grid_spec=spec, compiler_params=_params(("arbitrary",)))(chip, *ws)


class Comm:
    def __init__(self, ins, outs, aliases, sems, first, mid, last):
        self.ins, self.outs, self.aliases, self.sems = list(ins), list(outs), dict(aliases), list(sems)
        self.first, self.mid, self.last = first, mid, last


def run_comm(name, comm):
    ni, no = len(comm.ins), len(comm.outs)

    def body(*refs):
        ins, outs, sems = refs[:ni], refs[ni:ni + no], refs[ni + no:]
        comm.first(ins, outs, sems)
        comm.mid(ins, outs, sems)
        comm.last(ins, outs, sems)

    return _pcall(body, name=name, out_shape=comm.outs, in_specs=[ANY] * ni, out_specs=[ANY] * no,
                  input_output_aliases=comm.aliases, scratch_shapes=comm.sems, compiler_params=_params())(*comm.ins)


def gather_comm(bufs):
    n = len(bufs)

    def half(out, w, which):
        pr = out[w].shape[1] // 2
        return pl.ds(pl.multiple_of(which * pr, BF16_ROWS), pr)

    def first(ins, out, sems):
        isend, irecv, _, _ = sems
        x, y, c, chips = _place()
        for w in range(n):
            mine = out[w].at[2 * x + y, half(out, w, c)]
            for j, (cx, cy) in enumerate(chips):
                _remote(mine, mine, isend.at[3 * w + j], irecv.at[3 * w + j], (cx, cy, c)).start()

    def mid(ins, out, sems):
        isend, irecv, dsend, drecv = sems
        x, y, c, chips = _place()
        sib = (x, y, 1 - c)
        for w in range(n):
            for j, (cx, cy) in enumerate(chips):
                landed = out[w].at[2 * cx + cy, half(out, w, c)]
                _remote(landed, landed, isend.at[3 * w + j], irecv.at[3 * w + j], sib).wait_recv()
                _remote(landed, landed, dsend.at[3 * w + j], drecv.at[3 * w + j], sib).start()

    def last(ins, out, sems):
        isend, irecv, dsend, drecv = sems
        x, y, c, chips = _place()
        sib = (x, y, 1 - c)
        for w in range(n):
            for j, (cx, cy) in enumerate(chips):
                landed = out[w].at[2 * cx + cy, half(out, w, 1 - c)]
                _remote(landed, landed, dsend.at[3 * w + j], drecv.at[3 * w + j], sib).wait_recv()
        for w in range(n):
            sent = out[w].at[0, half(out, w, c)]
            for j in range(3):
                _remote(sent, sent, isend.at[3 * w + j], irecv.at[3 * w + j], sib).wait_send()
                _remote(sent, sent, dsend.at[3 * w + j], drecv.at[3 * w + j], sib).wait_send()

    return Comm(bufs, [jax.ShapeDtypeStruct(s.shape, s.dtype) for s in bufs], {w: w for w in range(n)},
                [pltpu.SemaphoreType.DMA((3 * n,))] * 4, first, mid, last)


def _nothing(ins, outs, sems):
    return None


def join_comms(a, b):
    ni, no, ns = len(a.ins), len(a.outs), len(a.sems)

    def both(f, g):
        def hook(ins, outs, sems):
            f(ins[:ni], outs[:no], sems[:ns])
            g(ins[ni:], outs[no:], sems[ns:])
        return hook

    aliases = dict(a.aliases)
    aliases.update({ni + k: no + v for k, v in b.aliases.items()})
    return Comm(a.ins + b.ins, a.outs + b.outs, aliases, a.sems + b.sems,
                both(a.first, b.first), both(a.mid, b.mid), both(a.last, b.last))


def exchange_comm(pieces):
    n = len(pieces)

    def copies(src, out, sems):
        x, y, c, _ = _place()
        return [_remote(src[w].at[k, 1 - c], out[w].at[k], sems[0].at[N_CHIPS * w + k], sems[1].at[N_CHIPS * w + k],
                        (x, y, 1 - c)) for w in range(n) for k in range(N_CHIPS)]

    def first(src, out, sems):
        for cp in copies(src, out, sems):
            cp.start()

    def last(src, out, sems):
        for cp in copies(src, out, sems):
            cp.wait()

    return Comm(pieces, [jax.ShapeDtypeStruct((N_CHIPS,) + s.shape[2:], s.dtype) for s in pieces], {},
                [pltpu.SemaphoreType.DMA((N_CHIPS * n,))] * 2, first, _nothing, last)


def scatter_comm(parts):
    n = len(parts)

    def copies(src, out, sems):
        x, y, c, chips = _place()
        return [_remote(src[w].at[2 * cx + cy], out[w].at[j], sems[0].at[3 * w + j], sems[1].at[3 * w + j], (cx, cy, c))
                for w in range(n) for j, (cx, cy) in enumerate(chips)]

    def first(src, out, sems):
        for cp in copies(src, out, sems):
            cp.start()

    def last(src, out, sems):
        for cp in copies(src, out, sems):
            cp.wait()

    return Comm(parts, [jax.ShapeDtypeStruct((3,) + s.shape[1:], s.dtype) for s in parts], {},
                [pltpu.SemaphoreType.DMA((3 * n,))] * 2, first, _nothing, last)


def share_comm(halves):
    n = len(halves)

    def first(ins, buf, sems):
        x, y, c, _ = _place()
        for w in range(n):
            _remote(buf[w].at[c], buf[w].at[c], sems[0].at[w], sems[1].at[w], (x, y, 1 - c)).start()

    def last(ins, buf, sems):
        x, y, c, _ = _place()
        for w in range(n):
            landed = buf[w].at[1 - c]
            _remote(landed, landed, sems[0].at[w], sems[1].at[w], (x, y, 1 - c)).wait_recv()
        for w in range(n):
            _remote(buf[w].at[c], buf[w].at[c], sems[0].at[w], sems[1].at[w], (x, y, 1 - c)).wait_send()

    return Comm(halves, [jax.ShapeDtypeStruct(s.shape, s.dtype) for s in halves], {w: w for w in range(n)},
                [pltpu.SemaphoreType.DMA((n,))] * 2, first, _nothing, last)


def gather_small(name, blk, reduce):
    r, cdim = blk.shape

    def body(in_ref, out_ref, *rest):
        if reduce:
            buf, send_sem, recv_sem = rest
        else:
            buf = out_ref
            send_sem, recv_sem = rest
        x, y, c, _ = _place()
        me = 4 * x + 2 * y + c
        buf[me] = in_ref[...]
        peers = []
        for dx in range(2):
            for dy in range(2):
                for dc in range(2):
                    if dx or dy or dc:
                        peers.append((dx, dy, dc))
        copies = []
        for s, (dx, dy, dc) in enumerate(peers):
            cp = _remote(in_ref, buf.at[me], send_sem.at[s], recv_sem.at[s],
                         ((1 - x if dx else x), (1 - y if dy else y), (1 - c if dc else c)))
            cp.start()
            copies.append(cp)
        for s, (dx, dy, dc) in enumerate(peers):
            px, py, pc_ = (1 - x if dx else x), (1 - y if dy else y), (1 - c if dc else c)
            landed = buf.at[4 * px + 2 * py + pc_]
            _remote(landed, landed, send_sem.at[s], recv_sem.at[s], (x, y, c)).wait_recv()
        for cp in copies:
            cp.wait_send()
        if reduce:
            tot = buf[0]
            for s in range(1, N_DEV):
                tot = tot + buf[s]
            out_ref[...] = tot

    vm = pl.BlockSpec(memory_space=pltpu.VMEM)
    out_shape = jax.ShapeDtypeStruct((r, cdim) if reduce else (N_DEV, r, cdim), F32)
    scratch = ([pltpu.VMEM((N_DEV, r, cdim), F32)] if reduce else []) + [pltpu.SemaphoreType.DMA((N_DEV - 1,))] * 2
    return _pcall(body, name=name, out_shape=out_shape, in_specs=[vm], out_specs=vm, scratch_shapes=scratch,
                  compiler_params=_params())(blk)


def _job_tiles(shapes, tile_bytes, mult):
    tiles = []
    for rows, cols in shapes:
        tr = _tile(rows, max(mult, tile_bytes // (4 * cols)), mult)
        tiles.append((tr, rows // tr))
    return tiles, max(n for _, n in tiles)


def sum_cores(name, owns, gots, place):
    nj = len(owns)
    tiles, _ = _job_tiles([o.shape[2:] for o in owns], 1 << 21, BF16_ROWS)
    steps = max(N_CHIPS * n for _, n in tiles)

    def body(place_ref, *refs):
        i = pl.program_id(0)
        for k, (_, n) in enumerate(tiles):
            @pl.when(i < N_CHIPS * n)
            def _(own_ref=refs[2 * k], got_ref=refs[2 * k + 1], o_ref=refs[2 * nj + k]):
                o_ref[...] = (own_ref[...].astype(F32) + got_ref[...].astype(F32)).astype(o_ref.dtype)

    in_specs, out_specs, out_shape, args = [], [], [], []
    for own, got, (tr, n) in zip(owns, gots, tiles):
        pc = own.shape[3]
        last = N_CHIPS * n - 1
        in_specs += [pl.BlockSpec((None, None, tr, pc),
                                  lambda i, s, n=n, last=last: (jnp.minimum(i, last) // n, s[1], jnp.minimum(i, last) % n, 0)),
                     pl.BlockSpec((None, tr, pc),
                                  lambda i, s, n=n, last=last: (jnp.minimum(i, last) // n, jnp.minimum(i, last) % n, 0))]
        out_specs.append(pl.BlockSpec((None, tr, pc),
                                      lambda i, s, n=n, last=last: (jnp.minimum(i, last) // n, jnp.minimum(i, last) % n, 0)))
        out_shape.append(jax.ShapeDtypeStruct(got.shape, BF16))
        args += [own, got]
    spec = pltpu.PrefetchScalarGridSpec(num_scalar_prefetch=1, grid=(steps,), in_specs=in_specs, out_specs=out_specs)
    return _pcall(body, name=name, out_shape=out_shape, grid_spec=spec,
                  compiler_params=_params(("arbitrary",)))(place, *args)


def sum_chips(name, parts, gots, place):
    nj = len(parts)
    tiles, steps = _job_tiles([p.shape[1:] for p in parts], 1 << 20, BF16_ROWS)

    def body(place_ref, *refs):
        i = pl.program_id(0)
        for k, (_, n) in enumerate(tiles):
            @pl.when(i < n)
            def _(part_ref=refs[2 * k], got_ref=refs[2 * k + 1], o_ref=refs[2 * nj + k]):
                tot = part_ref[...].astype(F32)
                for j in range(3):
                    tot = tot + got_ref[j].astype(F32)
                o_ref[...] = tot

    in_specs, out_specs, out_shape, args = [], [], [], []
    for part, got, (tr, n) in zip(parts, gots, tiles):
        pc = part.shape[2]
        in_specs += [pl.BlockSpec((None, tr, pc), lambda i, s, n=n: (s[0], jnp.minimum(i, n - 1), 0)),
                     pl.BlockSpec((3, tr, pc), lambda i, s, n=n: (0, jnp.minimum(i, n - 1), 0))]
        out_specs.append(pl.BlockSpec((None, tr, pc), lambda i, s, n=n: (s[1], jnp.minimum(i, n - 1), 0)))
        out_shape.append(jax.ShapeDtypeStruct((2,) + part.shape[1:], F32))
        args += [part, got]
    spec = pltpu.PrefetchScalarGridSpec(num_scalar_prefetch=1, grid=(steps,), in_specs=in_specs, out_specs=out_specs)
    return _pcall(body, name=name, out_shape=out_shape, grid_spec=spec,
                  compiler_params=_params(("arbitrary",)))(place, *args)


def adamw(name, jobs):
    c1 = 1.0 / (1.0 - ADAM_B1 ** ADAM_STEP)
    c2 = 1.0 / (1.0 - ADAM_B2 ** ADAM_STEP)
    nj = len(jobs)
    tiles, steps = _job_tiles([j[0].shape for j in jobs], 1 << 18, 8)

    def body(*refs):
        i = pl.program_id(0)
        for k, (_, n) in enumerate(tiles):
            w_ref, g_ref, m_ref, v_ref = refs[4 * k:4 * k + 4]
            d_ref, nm_ref, nv_ref = refs[4 * nj + 3 * k:4 * nj + 3 * k + 3]

            @pl.when(i < n)
            def _(w_ref=w_ref, g_ref=g_ref, m_ref=m_ref, v_ref=v_ref, d_ref=d_ref, nm_ref=nm_ref, nv_ref=nv_ref):
                gv = g_ref[...]
                nm = ADAM_B1 * m_ref[...] + (1.0 - ADAM_B1) * gv
                nv = ADAM_B2 * v_ref[...] + (1.0 - ADAM_B2) * (gv * gv)
                nm_ref[...] = nm
                nv_ref[...] = nv
                d_ref[...] = -ADAM_LR * ((nm * c1) / (jnp.sqrt(nv * c2) + ADAM_EPS) + ADAM_WD * w_ref[...])

    in_specs, out_specs, out_shape, args = [], [], [], []
    for (w, g, m, v), (tr, n) in zip(jobs, tiles):
        spec = pl.BlockSpec((tr, w.shape[1]), lambda i, n=n: (jnp.minimum(i, n - 1), 0))
        in_specs += [spec] * 4
        out_specs += [spec] * 3
        out_shape += [jax.ShapeDtypeStruct(w.shape, F32)] * 3
        args += [w, g, m, v]
    res = _pcall(body, name=name, out_shape=out_shape, grid=(steps,), in_specs=in_specs, out_specs=out_specs,
                 compiler_params=_params(("arbitrary",)))(*args)
    return [tuple(res[3 * k:3 * k + 3]) for k in range(nj)]


MATS = ["ffn1_w_in", "ffn1_w_out", "w_mix_in", "w_conv_out", "w_attn_out", "w_mix_out", "ffn2_w_in", "ffn2_w_out",
        "w_ple_gate", "w_ple_proj"]
COL_SHARDED = {"ffn1_w_in", "w_mix_in", "ffn2_w_in", "w_ple_proj"}
NORMS = ["ffn1_norm", "mix_norm", "ffn2_norm", "ple_norm", "final_norm"]
WEIGHTS = ["ffn1_norm", "ffn1_w_in", "ffn1_w_out", "mix_norm", "w_mix_in", "conv_w", "w_conv_out", "w_attn_out",
           "w_mix_out", "ffn2_norm", "ffn2_w_in", "ffn2_w_out", "ple_norm", "w_ple_gate", "w_ple_proj", "final_norm"]


def _pad_rows(a, rows):
    return jnp.concatenate([a, jnp.zeros((rows - a.shape[0],) + a.shape[1:], a.dtype)], axis=0)


def _step(x, p, tgt, w, m, v):
    t, d = x.shape
    tt = _tile(t, 256)
    tm = _tile(t, 512)
    tm2 = _tile(t, 1024)
    tq = _tile(t, 1024)

    chip = 2 * lax.axis_index("x") + lax.axis_index("y")
    place = jnp.stack([chip, lax.axis_index("c")]).astype(jnp.int32)

    placed = dict(zip(MATS, place_shards("place_shards", [w[k] for k in MATS], place)))
    full = {}

    def keep(names, bufs):
        for k, buf in zip(names, bufs):
            full[k] = buf if k in COL_SHARDED else buf.reshape(-1, buf.shape[2])

    def gather_of(names):
        return gather_comm([placed[k] for k in names])

    cw_all = gather_small("gather_conv_w", _pad_rows(w["conv_w"], 8), False)
    cw8 = jnp.concatenate([cw_all[2 * k] for k in range(N_CHIPS)], axis=1)
    g1, gm, g2, gp, gf = (w[k].reshape(1, d) for k in NORMS)

    def ffn_fwd(tag, h, g, first, w_in_name, w_out_name, riders):
        if first:
            n, bufs = rms_fwd(tag + "_norm", h, g, tt, comm=gather_of(first))
            keep(first, bufs)
            (a, s), bufs = ffn_in_act(tag + "_in", n, full[w_in_name], tm, comm=gather_of(riders))
            keep(riders, bufs)
        else:
            a, s, n = ffn_in_act(tag + "_in", h, full[w_in_name], tm, gain=g)
        return n, a, s, mm_nn(tag + "_out", s, full[w_out_name], F32, tm, res=h, alpha=0.5)

    n1, a1, s1, h1 = ffn_fwd("ffn1", x, g1, ["ffn1_w_in"], "ffn1_w_in", "ffn1_w_out", ["ffn1_w_out", "w_mix_in"])
    wmix = full["w_mix_in"]
    riders = ["w_conv_out", "w_attn_out", "w_mix_out", "ffn2_w_in", "ffn2_w_out", "w_ple_gate", "w_ple_proj"]
    (mixin, u), bufs = mm_nn_stacked("mix_in", h1, wmix, BF16, tm2, d, comm=gather_of(riders), gain=gm)
    keep(riders, bufs)
    cbx = qkv = gates = mixin
    wpp = full["w_ple_proj"]
    wpp = jnp.transpose(wpp, (1, 0, 2)).reshape(wpp.shape[1], -1)
    ycin, y_conv = conv_out_fwd("conv_out", cbx, cw8, full["w_conv_out"], tm)
    o = attn_fwd("attn", qkv, tq, d, 3 * d)
    y_attn = mm_nn("attn_out", o, full["w_attn_out"], BF16, tm)
    merged, h2 = mix_out_fwd("mix_out", gates, y_conv, y_attn, h1, full["w_mix_out"], tm, 3)
    n2, a2, s2, h3 = ffn_fwd("ffn2", h2, g2, [], "ffn2_w_in", "ffn2_w_out", [])

    pieces, chip_sums, halves = {}, {}, {}

    def as_pieces(k):
        pc = pieces[k]
        return pc if k in COL_SHARDED else pc.reshape(N_CHIPS, 2, pc.shape[0] // (2 * N_CHIPS), pc.shape[1])

    def sum_siblings(tag, names):
        pcs = [as_pieces(k) for k in names]
        got = run_comm("exchange_" + tag, exchange_comm(pcs))
        chip_sums.update(zip(names, sum_cores("sum_cores_" + tag, pcs, got, place)))

    def scatter_of(names):
        return scatter_comm([chip_sums[k] for k in names])

    def sum_landed(tag, names, landed):
        halves.update(zip(names, sum_chips("sum_chips_" + tag, [chip_sums[k] for k in names], landed, place)))

    npl, dh4, dpp, dzg, dgf, loss_row = tail("tail", h3, p, tgt, gp, gf, full["w_ple_gate"], wpp, tm)
    dwpp = mm_tn_whole("ple_proj_dw", p, dpp, tm2)
    pieces["w_ple_proj"] = jnp.transpose(dwpp.reshape(2, p.shape[1] // 2, N_CHIPS, d // N_CHIPS), (2, 0, 1, 3))
    pieces["w_ple_gate"] = mm_tn_rows("ple_gate_dw", npl, dzg, tm2)
    dh3, df2, dgp = mm_nt("ple_gate_dx", dzg, full["w_ple_gate"], F32, tm2, d, norm=(h3, gp, dh4), alpha=0.5)
    w_in, w_out = full["ffn2_w_in"], full["ffn2_w_out"]
    pieces["ffn2_w_out"] = mm_tn_rows("ffn2_dwout", s2, df2, tm2)
    da2 = ffn_ds_dact("ffn2_ds", df2, w_out, a2, tm2)
    pieces["ffn2_w_in"] = mm_tn_cols("ffn2_dwin", n2, da2, tm2)
    dh2, dh2b, dg2 = mm_nt_stacked("ffn2_dn", da2, w_in, tm2, (h2, g2, dh3))
    pieces["w_mix_out"] = mm_tn_rows("mix_out_dw", merged, dh2b, tm2)
    dyc, dya, dgates = mix_out_bwd("mix_out_dx", dh2b, full["w_mix_out"], gates, y_conv, y_attn, tm, 3)
    pieces["w_conv_out"] = mm_tn_rows("conv_out_dw", ycin, dyc, tm2)
    dcbx, dcw8 = conv_out_bwd("conv_out_dx", dyc, full["w_conv_out"], cbx, cw8, tt)
    pieces["w_attn_out"] = mm_tn_rows("attn_out_dw", o, dya, tm2)
    do = mm_nt("attn_out_dx", dya, full["w_attn_out"], BF16, tm, d)
    dq, dk, dv = attn_bwd("attn_bwd", qkv, do, tq, 3 * d)
    dmix = [dcbx, dq, dk, dv, dgates]
    early = ["ffn2_w_in", "ffn2_w_out", "w_ple_gate", "w_ple_proj", "w_mix_out", "w_conv_out", "w_attn_out"]
    swap = exchange_comm([as_pieces(k) for k in early])
    pieces["w_mix_in"], got = mm_tn_parts("mix_in_dw", u, dmix, tm2, comm=swap)
    chip_sums.update(zip(early, sum_cores("sum_cores_early", swap.ins, got, place)))
    swap = exchange_comm([as_pieces("w_mix_in")])
    (dh1, df1, dgm), landed = mm_nt_parts("mix_in_dx", dmix, wmix, tm2, (h1, gm, dh2), 0.5,
                                          comm=join_comms(scatter_of(early), swap))
    sum_landed("early", early, landed[:len(early)])
    chip_sums["w_mix_in"] = sum_cores("sum_cores_mix", swap.ins, landed[len(early):], place)[0]
    w_in, w_out = full["ffn1_w_in"], full["ffn1_w_out"]
    pieces["ffn1_w_out"] = mm_tn_rows("ffn1_dwout", s1, df1, tm2)
    da1 = ffn_ds_dact("ffn1_ds", df1, w_out, a1, tm2)
    pieces["ffn1_w_in"], landed = mm_tn_cols("ffn1_dwin", n1, da1, tm2, comm=scatter_of(["w_mix_in"]))
    sum_landed("mix", ["w_mix_in"], landed)
    late = ["ffn1_w_in", "ffn1_w_out"]
    sum_siblings("late", late)
    done = early + ["w_mix_in"]
    (dx, _, dg1), landed = mm_nt_stacked(
        "ffn1_dn", da1, w_in, tm2, (x, g1, dh1),
        comm=join_comms(scatter_of(late), share_comm([halves[k] for k in done])))
    sum_landed("late", late, landed[:len(late)])
    shared = dict(zip(done, landed[len(late):]))

    shared.update(zip(late, run_comm("share_halves", share_comm([halves[k] for k in late]))))
    grad, delta, new_m, new_v = {}, {}, {}, {}
    for k in MATS:
        grad[k] = shared[k].reshape(w[k].shape)

    small = jnp.concatenate([dg1, dgm, dg2, dgp, dgf, dcw8[:3], loss_row, jnp.zeros((7, d), F32)], axis=0)
    tot = gather_small("sum_small", small, True)
    loss = tot[8, 0]
    norm_w = jnp.concatenate([w[k].reshape(1, d) for k in NORMS] + [jnp.zeros((3, d), F32)], axis=0)
    norm_m = jnp.concatenate([m[k].reshape(1, d) for k in NORMS] + [jnp.zeros((3, d), F32)], axis=0)
    norm_v = jnp.concatenate([v[k].reshape(1, d) for k in NORMS] + [jnp.ones((3, d), F32)], axis=0)
    norm_g = jnp.concatenate([tot[0:5], jnp.zeros((3, d), F32)], axis=0)
    cs = d // N_CHIPS
    gcw = lax.dynamic_slice(tot[5:8], (0, chip * cs), (3, cs))
    conv_job = (_pad_rows(w["conv_w"], 8), _pad_rows(gcw, 8), _pad_rows(m["conv_w"], 8),
                jnp.concatenate([v["conv_w"], jnp.ones((5, cs), F32)], axis=0))

    steps = adamw("adamw", [(w[k], grad[k], m[k], v[k]) for k in MATS]
                  + [(norm_w, norm_g, norm_m, norm_v), conv_job])
    for k, res in zip(MATS, steps):
        delta[k], new_m[k], new_v[k] = res
    nd, nm, nv = steps[len(MATS)]
    for r, k in enumerate(NORMS):
        grad[k] = norm_g[r].reshape(w[k].shape)
        delta[k], new_m[k], new_v[k] = (a[r].reshape(w[k].shape) for a in (nd, nm, nv))
    cd, cm, cv = steps[len(MATS) + 1]
    grad["conv_w"], delta["conv_w"], new_m["conv_w"], new_v["conv_w"] = gcw, cd[:3], cm[:3], cv[:3]
    return loss, dx, grad, delta, new_m, new_v


def kernel(x, p, ffn1_norm, ffn1_w_in, ffn1_w_out, mix_norm, w_mix_in, conv_w, w_conv_out, w_attn_out, w_mix_out, ffn2_norm, ffn2_w_in, ffn2_w_out, ple_norm, w_ple_gate, w_ple_proj, final_norm, loss_target, m_ffn1_norm, m_ffn1_w_in, m_ffn1_w_out, m_mix_norm, m_w_mix_in, m_conv_w, m_w_conv_out, m_w_attn_out, m_w_mix_out, m_ffn2_norm, m_ffn2_w_in, m_ffn2_w_out, m_ple_norm, m_w_ple_gate, m_w_ple_proj, m_final_norm, v_ffn1_norm, v_ffn1_w_in, v_ffn1_w_out, v_mix_norm, v_w_mix_in, v_conv_w, v_w_conv_out, v_w_attn_out, v_w_mix_out, v_ffn2_norm, v_ffn2_w_in, v_ffn2_w_out, v_ple_norm, v_w_ple_gate, v_w_ple_proj, v_final_norm):
    ws = (ffn1_norm, ffn1_w_in, ffn1_w_out, mix_norm, w_mix_in, conv_w, w_conv_out, w_attn_out, w_mix_out, ffn2_norm,
          ffn2_w_in, ffn2_w_out, ple_norm, w_ple_gate, w_ple_proj, final_norm)
    ms = (m_ffn1_norm, m_ffn1_w_in, m_ffn1_w_out, m_mix_norm, m_w_mix_in, m_conv_w, m_w_conv_out, m_w_attn_out,
          m_w_mix_out, m_ffn2_norm, m_ffn2_w_in, m_ffn2_w_out, m_ple_norm, m_w_ple_gate, m_w_ple_proj, m_final_norm)
    vs = (v_ffn1_norm, v_ffn1_w_in, v_ffn1_w_out, v_mix_norm, v_w_mix_in, v_conv_w, v_w_conv_out, v_w_attn_out,
          v_w_mix_out, v_ffn2_norm, v_ffn2_w_in, v_ffn2_w_out, v_ple_norm, v_w_ple_gate, v_w_ple_proj, v_final_norm)
    assert x.shape[0] == 1 and p.shape[:2] == (1, 1), "one sequence and one layer per device"

    def strip(a):
        return a[0] if a.ndim == 3 or (a.ndim == 2 and a.shape[0] == 1) else a

    w = {k: strip(a) for k, a in zip(WEIGHTS, ws)}
    m = {k: strip(a) for k, a in zip(WEIGHTS, ms)}
    v = {k: strip(a) for k, a in zip(WEIGHTS, vs)}
    loss, dx, grad, delta, new_m, new_v = _step(x[0], p[0, 0], loss_target[0], w, m, v)
    shapes = [a.shape for a in ws]
    outs = [loss, dx[None]]
    for res in (grad, delta, new_m, new_v):
        outs += [res[k].reshape(s) for k, s in zip(WEIGHTS, shapes)]
    return tuple(outs)
```

```python
import functools
import math

import jax
import jax.numpy as jnp
from jax import lax
from jax.experimental import pallas as pl
from jax.experimental.pallas import tpu as pltpu

F32 = jnp.float32
BF16 = jnp.bfloat16
MESH = pl.DeviceIdType.MESH
ANY = pl.BlockSpec(memory_space=pl.ANY)

HEAD_DIM = 128
NORM_EPS = 1e-6
N_CHIPS = 4
N_DEV = 8
BF16_ROWS = 16
VMEM_LIMIT = 56 * 1024 * 1024
ACC_BYTES = 8 * 1024 * 1024
STICK_EXIT = 110.0

ADAM_LR = 0.001
ADAM_B1 = 0.9
ADAM_B2 = 0.999
ADAM_EPS = 1e-08
ADAM_WD = 0.01
ADAM_STEP = 10

NN = (((1,), (0,)), ((), ()))
NT = (((1,), (1,)), ((), ()))
TN = (((0,), (0,)), ((), ()))


def _params(sem=None, **kw):
    if sem is not None:
        kw["dimension_semantics"] = sem
    return pltpu.CompilerParams(vmem_limit_bytes=VMEM_LIMIT, **kw)


def _pcall(body, **kw):
    return pl.pallas_call(body, **kw)


def _tile(n, pref, mult=8):
    best = None
    for d in range(mult, min(n, pref) + 1, mult):
        if n % d == 0:
            best = d
    return best if best is not None else n


def _dot(a, b, dims):
    return lax.dot_general(a, b, dims, preferred_element_type=F32)


def _call(name, body, grid, in_specs, out_specs, out_shape, args, scratch=(), sem=None, comm=None):
    n_in, n_out, n_sc = len(in_specs), len(out_specs), len(scratch)
    if comm is None:
        def plain(*refs):
            body(refs[:n_in], refs[n_in:n_in + n_out], refs[n_in + n_out:])

        return _pcall(plain, name=name, out_shape=list(out_shape), grid=grid, in_specs=list(in_specs),
                      out_specs=list(out_specs), scratch_shapes=list(scratch), compiler_params=_params(sem))(*args)
    n_cin, n_cout = len(comm.ins), len(comm.outs)
    steps = math.prod(grid)

    def hosted(*refs):
        ins, c_ins = refs[:n_in], refs[n_in:n_in + n_cin]
        outs = refs[n_in + n_cin:n_in + n_cin + n_out]
        c_outs = refs[n_in + n_cin + n_out:n_in + n_cin + n_out + n_cout]
        rest = refs[n_in + n_cin + n_out + n_cout:]
        sems = rest[n_sc:]
        step = pl.program_id(0)
        for ax in range(1, len(grid)):
            step = step * grid[ax] + pl.program_id(ax)

        @pl.when(step == 0)
        def _():
            comm.first(c_ins, c_outs, sems)

        body(ins, outs, rest[:n_sc])

        @pl.when(step == (3 * steps) // 4)
        def _():
            comm.mid(c_ins, c_outs, sems)

        @pl.when(step == steps - 1)
        def _():
            comm.last(c_ins, c_outs, sems)

    res = _pcall(hosted, name=name, out_shape=list(out_shape) + comm.outs, grid=grid,
                 in_specs=list(in_specs) + [ANY] * n_cin, out_specs=list(out_specs) + [ANY] * n_cout,
                 input_output_aliases={n_in + k: n_out + v for k, v in comm.aliases.items()},
                 scratch_shapes=list(scratch) + comm.sems,
                 compiler_params=_params(("arbitrary",) * len(grid)))(*args, *comm.ins)
    return list(res[:n_out]), list(res[n_out:])


NORM_CHUNK = 256


def _norm_bwd_tile(read_dn, rows, first, h_ref, g_ref, dr_ref, dh_ref, dhb_ref, dg_ref, alpha):
    @pl.when(first)
    def _():
        dg_ref[...] = jnp.zeros_like(dg_ref)

    gv = g_ref[...]
    tot = jnp.zeros_like(gv)
    for c0 in range(0, rows, NORM_CHUNK):
        sl = slice(c0, min(rows, c0 + NORM_CHUNK))
        hv = h_ref[sl, :]
        rs = _rstd(hv)
        hn = hv * rs
        dnv = read_dn(sl)
        gy = dnv * gv
        dh = dr_ref[sl, :] + rs * (gy - hn * jnp.mean(gy * hn, axis=-1, keepdims=True))
        dh_ref[sl, :] = dh
        dhb_ref[sl, :] = (alpha * dh).astype(BF16)
        tot = tot + jnp.sum(dnv * hn, axis=0, keepdims=True)
    dg_ref[...] += tot


def _mm(name, a, b, out_sds, grid, a_spec, b_spec, o_spec, dims, acc_shape, res=None, alpha=1.0, comm=None,
        norm=None, gain=None):
    nk = grid[2]

    def body(ins, outs, scratch):
        a_ref, b_ref = ins[:2]
        r_ref = ins[2] if res is not None else None
        o_ref = outs[0]
        if gain is not None:
            n_ref = scratch[-1]

            @pl.when(jnp.logical_and(pl.program_id(1) == 0, pl.program_id(2) == 0))
            def _():
                hv = a_ref[...]
                n_ref[...] = (hv * _rstd(hv) * ins[-1][...]).astype(BF16)
                outs[-1][...] = n_ref[...]

            a_ref = n_ref

        def finish(read):
            if norm is not None:
                first = jnp.logical_and(pl.program_id(0) == 0, pl.program_id(1) == 0)
                _norm_bwd_tile(read, o_ref.shape[0], first, *ins[2:5], *outs, alpha)
                return
            r = read(slice(None))
            if alpha != 1.0:
                r = r * alpha
            if r_ref is not None:
                r = r_ref[...] + r
            if len(o_ref.shape) == 3:
                half = o_ref.shape[1]
                o_ref[0] = r[:half].astype(o_ref.dtype)
                o_ref[1] = r[half:].astype(o_ref.dtype)
            else:
                o_ref[...] = r.astype(o_ref.dtype)

        if nk == 1:
            part = _dot(a_ref[...].astype(BF16), b_ref[...].astype(BF16), dims)
            finish(lambda sl: part[sl])
        else:
            acc_ref = scratch[0]
            kk = pl.program_id(2)

            @pl.when(kk == 0)
            def _():
                acc_ref[...] = jnp.zeros_like(acc_ref)

            acc_ref[...] += _dot(a_ref[...].astype(BF16), b_ref[...].astype(BF16), dims)

            @pl.when(kk == nk - 1)
            def _():
                finish(lambda sl: acc_ref[sl, :])

    in_specs = [a_spec, b_spec]
    args = [a, b]
    out_specs, out_shape = [o_spec], [out_sds]
    sem = ("parallel", "parallel", "arbitrary")
    if res is not None:
        in_specs.append(o_spec)
        args.append(res)
    if norm is not None:
        width = out_sds.shape[1]
        whole = pl.BlockSpec((1, width), lambda i, j, r: (0, 0))
        in_specs += [o_spec, whole, o_spec]
        args += list(norm)
        out_specs = [o_spec, o_spec, whole]
        out_shape = [jax.ShapeDtypeStruct(out_sds.shape, F32), jax.ShapeDtypeStruct(out_sds.shape, BF16),
                     jax.ShapeDtypeStruct((1, width), F32)]
        sem = ("arbitrary", "arbitrary", "arbitrary")
    scratch = [] if nk == 1 else [pltpu.VMEM(acc_shape, F32)]
    if gain is not None:
        in_specs.append(pl.BlockSpec((1, a.shape[1]), lambda i, j, r: (0, 0)))
        args.append(gain)
        out_specs.append(a_spec)
        out_shape.append(jax.ShapeDtypeStruct(a.shape, BF16))
        scratch.append(pltpu.VMEM(a_spec.block_shape, BF16))
        sem = ("parallel", "arbitrary", "arbitrary")
    got = _call(name, body, grid, in_specs, out_specs, out_shape, args, scratch, sem, comm)
    if norm is not None or gain is not None:
        return got if comm is None else (got[0], got[1])
    return got[0] if comm is None else (got[0][0], got[1])


def ffn_in_act(name, n, w4, tm, comm=None, gain=None):
    t, d = n.shape
    cs = w4.shape[2]

    def body(ins, outs, scratch):
        wg_ref, wu_ref = ins[-2:]
        a_ref, s_ref = outs[:2]
        if gain is None:
            nv = ins[0][...]
        else:
            hv = ins[0][...]
            nv = (hv * _rstd(hv) * ins[1][...]).astype(BF16)

            @pl.when(pl.program_id(0) == 0)
            def _():
                outs[2][...] = nv
        gate = _dot(nv, wg_ref[...], NN)
        up = _dot(nv, wu_ref[...], NN)
        a_ref[0] = gate.astype(BF16)
        a_ref[1] = up.astype(BF16)
        s_ref[...] = (gate * jax.nn.sigmoid(gate) * up).astype(BF16)

    rows = pl.BlockSpec((tm, d), lambda j, i: (i, 0))
    in_specs = [rows] + ([] if gain is None else [pl.BlockSpec((1, d), lambda j, i: (0, 0))])
    in_specs += [pl.BlockSpec((None, d, cs), lambda j, i: (j, 0, 0)),
                 pl.BlockSpec((None, d, cs), lambda j, i: (2 + j, 0, 0))]
    out_specs = [pl.BlockSpec((2, tm, cs), lambda j, i: (0, i, j)), pl.BlockSpec((tm, cs), lambda j, i: (i, j))]
    out_shape = [jax.ShapeDtypeStruct((2, t, 2 * cs), BF16), jax.ShapeDtypeStruct((t, 2 * cs), BF16)]
    if gain is not None:
        out_specs.append(pl.BlockSpec((tm, d), lambda j, i: (jnp.where(j == 0, i, t // tm - 1), 0)))
        out_shape.append(jax.ShapeDtypeStruct((t, d), BF16))
    got = _call(name, body, (2, t // tm), in_specs, out_specs, out_shape,
                [n] + ([] if gain is None else [gain]) + [w4, w4], (), ("arbitrary", "arbitrary"), comm)
    return got if comm is None else (got[0], got[1])


def ffn_ds_dact(name, df, w_out, a3, tm):
    t, d = df.shape
    f = w_out.shape[0]
    cs = f // 2

    def body(ins, outs, scratch):
        df_ref, w_ref, a_ref = ins
        ds = _dot(df_ref[...], w_ref[...], NT)
        for c0 in range(0, tm, NORM_CHUNK):
            sl = slice(c0, min(tm, c0 + NORM_CHUNK))
            gate = a_ref[0, sl, :].astype(F32)
            up = a_ref[1, sl, :].astype(F32)
            sg = jax.nn.sigmoid(gate)
            outs[0][0, sl, :] = (ds[sl] * up * sg * (1.0 + gate * (1.0 - sg))).astype(BF16)
            outs[0][1, sl, :] = (ds[sl] * gate * sg).astype(BF16)

    blk = pl.BlockSpec((2, tm, cs), lambda i, j: (0, i, j))
    return _call(name, body, (t // tm, 2),
                 [pl.BlockSpec((tm, d), lambda i, j: (i, 0)), pl.BlockSpec((cs, d), lambda i, j: (j, 0)), blk],
                 [blk], [jax.ShapeDtypeStruct((2, t, f), BF16)], [df, w_out, a3], (), ("parallel", "parallel"))[0]


def _part_ranges(parts, d):
    out, lo = [], 0
    for p in parts:
        out.append((lo, p.shape[1] // d))
        lo += p.shape[1] // d
    return out, lo


def mm_nt_parts(name, parts, w4, tm, norm, alpha, comm=None):
    m = parts[0].shape[0]
    d, cs = w4.shape[1], w4.shape[2]
    per = cs // d
    ranges, nblk = _part_ranges(parts, d)
    np_ = len(parts)
    nt = m // tm
    chunk = tm // nblk

    def body(ins, outs, scratch):
        w_ref, acc = ins[np_], scratch[0]
        i, r = pl.program_id(0), pl.program_id(1)

        @pl.when(jnp.logical_and(i < nt, r == 0))
        def _():
            acc[i % 2] = jnp.zeros(acc.shape[1:], F32)

        for (lo, n), a_ref in zip(ranges, ins[:np_]):
            @pl.when(jnp.logical_and(i < nt, jnp.logical_and(r >= lo, r < lo + n)))
            def _(a_ref=a_ref):
                acc[i % 2] += _dot(a_ref[...], w_ref[...], NT)

        @pl.when(i > 0)
        def _():
            rows = pl.ds(pl.multiple_of(r * chunk, chunk), chunk)
            first = jnp.logical_and(i == 1, r == 0)
            _norm_bwd_tile(lambda sl: acc[(i - 1) % 2, rows, :][sl], chunk, first, *ins[np_ + 1:], *outs, alpha)

    def ahead(i, r):
        return jnp.where(i < nt, r, nblk - 1)

    rows = pl.BlockSpec((chunk, d), lambda i, r: (jnp.where(i == 0, 0, (i - 1) * nblk + r), 0))
    whole = pl.BlockSpec((1, d), lambda i, r: (0, 0))
    specs = [pl.BlockSpec((tm, d), lambda i, r, lo=lo, n=n: (jnp.minimum(i, nt - 1), jnp.clip(ahead(i, r) - lo, 0, n - 1)))
             for lo, n in ranges]
    specs += [pl.BlockSpec((None, d, d), lambda i, r: (ahead(i, r) // per, 0, ahead(i, r) % per)), rows, whole, rows]
    got = _call(name, body, (nt + 1, nblk), specs, [rows, rows, whole],
                [jax.ShapeDtypeStruct((m, d), F32), jax.ShapeDtypeStruct((m, d), BF16),
                 jax.ShapeDtypeStruct((1, d), F32)],
                list(parts) + [w4] + list(norm), [pltpu.VMEM((2, tm, d), F32)], ("arbitrary", "arbitrary"), comm)
    return got if comm is None else (got[0], got[1])


def mm_tn_parts(name, xa, parts, tt, comm=None):
    t, k = xa.shape
    d = k
    pr = k // 2
    ranges, nblk = _part_ranges(parts, d)
    per = nblk // N_CHIPS

    def body(ins, outs, scratch):
        x_ref, acc = ins[0], scratch[0]
        jb, r = pl.program_id(0), pl.program_id(1)

        @pl.when(r == 0)
        def _():
            acc[...] = jnp.zeros_like(acc)

        for (lo, n), p_ref in zip(ranges, ins[1:]):
            @pl.when(jnp.logical_and(jb >= lo, jb < lo + n))
            def _(p_ref=p_ref):
                acc[...] += _dot(x_ref[...], p_ref[...], TN)

        @pl.when(r == t // tt - 1)
        def _():
            outs[0][0] = acc[:pr].astype(BF16)
            outs[0][1] = acc[pr:].astype(BF16)

    def part_spec(lo, n):
        return pl.BlockSpec((tt, d), lambda jb, r: (jnp.where(jnp.logical_and(jb >= lo, jb < lo + n), r, 0),
                                                    jnp.clip(jb - lo, 0, n - 1)))

    specs = [pl.BlockSpec((tt, k), lambda jb, r: (r, 0))] + [part_spec(lo, n) for lo, n in ranges]
    got = _call(name, body, (nblk, t // tt), specs,
                [pl.BlockSpec((None, 2, pr, d), lambda jb, r: (jb // per, 0, 0, jb % per))],
                [jax.ShapeDtypeStruct((N_CHIPS, 2, pr, per * d), BF16)], [xa] + list(parts),
                [pltpu.VMEM((k, d), F32)], ("parallel", "arbitrary"), comm)
    return got[0] if comm is None else (got[0][0], got[1])


def mm_nn(name, a, w, out_dtype, tm, res=None, alpha=1.0):
    m, k = a.shape
    n = w.shape[1]
    return _mm(name, a, w, jax.ShapeDtypeStruct((m, n), out_dtype), (m // tm, 1, 1),
               pl.BlockSpec((tm, k), lambda i, j, r: (i, 0)),
               pl.BlockSpec((k, n), lambda i, j, r: (0, 0)),
               pl.BlockSpec((tm, n), lambda i, j, r: (i, 0)), NN, None, res=res, alpha=alpha)


def mm_nn_stacked(name, a, w4, out_dtype, tm, tn, j0=0, nj=None, comm=None, gain=None):
    m, k = a.shape
    cs = w4.shape[2]
    per = cs // tn
    nj = N_CHIPS * per - j0 if nj is None else nj
    return _mm(name, a, w4, jax.ShapeDtypeStruct((m, nj * tn), out_dtype), (m // tm, nj, 1),
               pl.BlockSpec((tm, k), lambda i, j, r: (i, 0)),
               pl.BlockSpec((None, k, tn), lambda i, j, r: ((j + j0) // per, 0, (j + j0) % per)),
               pl.BlockSpec((tm, tn), lambda i, j, r: (i, j)), NN, None, comm=comm, gain=gain)


def mm_nt(name, dy, w, out_dtype, tm, tko, norm=None, alpha=1.0):
    m, n = dy.shape
    k = w.shape[0]
    return _mm(name, dy, w, jax.ShapeDtypeStruct((m, k), out_dtype), (m // tm, k // tko, 1),
               pl.BlockSpec((tm, n), lambda i, j, r: (i, 0)),
               pl.BlockSpec((tko, n), lambda i, j, r: (j, 0)),
               pl.BlockSpec((tm, tko), lambda i, j, r: (i, j)), NT, None, norm=norm, alpha=alpha)


def mm_nt_stacked(name, dy, w4, tm, norm, alpha=1.0, comm=None):
    m = dy.shape[1]
    k, cs = w4.shape[1], w4.shape[2]
    nt = m // tm
    chunk = tm // N_CHIPS

    def body(ins, outs, scratch):
        dy_ref, w_ref, h_ref, g_ref, dr_ref = ins
        dh_ref, dhb_ref, dg_ref = outs
        acc = scratch[0]
        i, r = pl.program_id(0), pl.program_id(1)

        @pl.when(jnp.logical_and(i < nt, r == 0))
        def _():
            acc[i % 2] = jnp.zeros(acc.shape[1:], F32)

        @pl.when(i < nt)
        def _():
            acc[i % 2] += _dot(dy_ref[...], w_ref[...], NT)

        @pl.when(i > 0)
        def _():
            rows = pl.ds(pl.multiple_of(r * chunk, chunk), chunk)
            first = jnp.logical_and(i == 1, r == 0)
            _norm_bwd_tile(lambda sl: acc[(i - 1) % 2, rows, :][sl], chunk, first, h_ref, g_ref, dr_ref,
                           dh_ref, dhb_ref, dg_ref, alpha)

    def behind(i, r):
        return (jnp.where(i == 0, 0, (i - 1) * N_CHIPS + r), 0)

    def ahead(i, r):
        return jnp.where(i < nt, r, N_CHIPS - 1)

    rows = pl.BlockSpec((chunk, k), behind)
    whole = pl.BlockSpec((1, k), lambda i, r: (0, 0))
    got = _call(name, body, (nt + 1, N_CHIPS),
                [pl.BlockSpec((None, tm, cs), lambda i, r: (ahead(i, r) // 2, jnp.minimum(i, nt - 1), ahead(i, r) % 2)),
                 pl.BlockSpec((None, k, cs), lambda i, r: (ahead(i, r), 0, 0)), rows, whole, rows],
                [rows, rows, whole],
                [jax.ShapeDtypeStruct((m, k), F32), jax.ShapeDtypeStruct((m, k), BF16),
                 jax.ShapeDtypeStruct((1, k), F32)],
                [dy, w4] + list(norm), [pltpu.VMEM((2, tm, k), F32)], ("arbitrary", "arbitrary"), comm)
    return got if comm is None else (got[0], got[1])


def mm_tn_rows(name, xa, dy, tt):
    t, k = xa.shape
    n = dy.shape[1]
    tkr = k if k * n * 4 <= ACC_BYTES else k // 2
    return _mm(name, xa, dy, jax.ShapeDtypeStruct((k, n), BF16), (k // tkr, 1, t // tt),
               pl.BlockSpec((tt, tkr), lambda i, j, r: (r, i)),
               pl.BlockSpec((tt, n), lambda i, j, r: (r, 0)),
               pl.BlockSpec((tkr, n), lambda i, j, r: (i, 0)), TN, (tkr, n))


def mm_tn_whole(name, xa, dy, tt):
    t, k = xa.shape
    n = dy.shape[1]
    return _mm(name, xa, dy, jax.ShapeDtypeStruct((k, n), BF16), (1, 1, t // tt),
               pl.BlockSpec((tt, k), lambda i, j, r: (r, 0)),
               pl.BlockSpec((tt, n), lambda i, j, r: (r, 0)),
               pl.BlockSpec((k, n), lambda i, j, r: (0, 0)), TN, (k, n))


def mm_tn_cols(name, xa, dy, tt, comm=None):
    t, k = xa.shape
    pr = k // 2
    if dy.ndim == 3:
        cs = dy.shape[2] // 2
        dy_spec = pl.BlockSpec((None, tt, cs), lambda i, j, r: (j // 2, r, j % 2))
    else:
        cs = dy.shape[1] // N_CHIPS
        dy_spec = pl.BlockSpec((tt, cs), lambda i, j, r: (r, j))
    return _mm(name, xa, dy, jax.ShapeDtypeStruct((N_CHIPS, 2, pr, cs), BF16), (1, N_CHIPS, t // tt),
               pl.BlockSpec((tt, k), lambda i, j, r: (r, 0)), dy_spec,
               pl.BlockSpec((None, 2, pr, cs), lambda i, j, r: (j, 0, 0, 0)), TN, (k, cs), comm=comm)


def _rows(tt, w, col=0):
    return pl.BlockSpec((tt, w), lambda i: (i, col))


def _whole(shape):
    return pl.BlockSpec(shape, lambda i: (0,) * len(shape))


def _rstd(h):
    return lax.rsqrt(jnp.mean(h * h, axis=-1, keepdims=True) + NORM_EPS)


def rms_fwd(name, h, g, tt, comm=None):
    t, d = h.shape

    def body(ins, outs, scratch):
        hv = ins[0][...]
        outs[0][...] = (hv * _rstd(hv) * ins[1][...]).astype(BF16)

    got = _call(name, body, (t // tt,), [_rows(tt, d), _whole((1, d))], [_rows(tt, d)],
                [jax.ShapeDtypeStruct((t, d), BF16)], [h, g], (), ("parallel",), comm)
    return got[0] if comm is None else (got[0][0], got[1])


def mix_out_fwd(name, gates, yc, ya, h, w, tt, gcol=0):
    t, d = yc.shape

    def body(g_ref, yc_ref, ya_ref, h_ref, w_ref, m_ref, o_ref):
        merged = (jax.nn.sigmoid(g_ref[:, :d].astype(F32)) * yc_ref[...].astype(F32)
                  + jax.nn.sigmoid(g_ref[:, d:].astype(F32)) * ya_ref[...].astype(F32)).astype(BF16)
        m_ref[...] = merged
        o_ref[...] = h_ref[...] + _dot(merged, w_ref[...], NN)

    return _pcall(body, name=name,
                  out_shape=(jax.ShapeDtypeStruct((t, d), BF16), jax.ShapeDtypeStruct((t, d), F32)),
                  grid=(t // tt,),
                  in_specs=[_rows(tt, 2 * d, gcol), _rows(tt, d), _rows(tt, d), _rows(tt, d), _whole((d, d))],
                  out_specs=(_rows(tt, d), _rows(tt, d)),
                  compiler_params=_params(("parallel",)))(gates, yc, ya, h, w)


def mix_out_bwd(name, dh, w, gates, yc, ya, tt, gcol=0):
    t, d = yc.shape

    def body(dh_ref, w_ref, g_ref, yc_ref, ya_ref, dyc_ref, dya_ref, dg_ref):
        dmv = _dot(dh_ref[...], w_ref[...], NT)
        sc = jax.nn.sigmoid(g_ref[:, :d].astype(F32))
        sa = jax.nn.sigmoid(g_ref[:, d:].astype(F32))
        dyc_ref[...] = (dmv * sc).astype(BF16)
        dya_ref[...] = (dmv * sa).astype(BF16)
        dg_ref[:, :d] = (dmv * yc_ref[...].astype(F32) * sc * (1.0 - sc)).astype(BF16)
        dg_ref[:, d:] = (dmv * ya_ref[...].astype(F32) * sa * (1.0 - sa)).astype(BF16)

    return _pcall(body, name=name,
                  out_shape=(jax.ShapeDtypeStruct((t, d), BF16), jax.ShapeDtypeStruct((t, d), BF16),
                             jax.ShapeDtypeStruct((t, 2 * d), BF16)),
                  grid=(t // tt,),
                  in_specs=[_rows(tt, d), _whole((d, d)), _rows(tt, 2 * d, gcol), _rows(tt, d), _rows(tt, d)],
                  out_specs=(_rows(tt, d), _rows(tt, d), _rows(tt, 2 * d)),
                  compiler_params=_params(("parallel",)))(dh, w, gates, yc, ya)


def _shift_down(cur, prev8, s):
    tt = cur.shape[0]
    rolled = pltpu.roll(cur, s, 0)
    row8 = lax.broadcasted_iota(jnp.int32, prev8.shape, 0)
    first8 = jnp.where(row8 < s, pltpu.roll(prev8, s, 0), rolled[:8])
    return jnp.concatenate([first8, rolled[8:]], axis=0) if tt > 8 else first8


def _shift_up(cur, next8, s):
    tt = cur.shape[0]
    rolled = pltpu.roll(cur, tt - s, 0)
    row8 = lax.broadcasted_iota(jnp.int32, next8.shape, 0)
    last8 = jnp.where(row8 >= 8 - s, pltpu.roll(next8, 8 - s, 0), rolled[tt - 8:])
    return jnp.concatenate([rolled[:tt - 8], last8], axis=0) if tt > 8 else last8


def _prev_rows(tt, d, col):
    return pl.BlockSpec((BF16_ROWS, d), lambda i: (jnp.maximum(i * (tt // BF16_ROWS) - 1, 0), col))


def _next_rows(tt, d, col, t):
    return pl.BlockSpec((BF16_ROWS, d),
                        lambda i: (jnp.minimum((i + 1) * (tt // BF16_ROWS), t // BF16_ROWS - 1), col))


def conv_out_fwd(name, cbx, cw8, w_out, tt):
    t = cbx.shape[0]
    d = w_out.shape[0]
    d3 = 3 * d

    def body(cb_ref, cc_ref, cx_ref, pc_ref, px_ref, w_ref, wo_ref, o_ref, y_ref):
        has_prev = (pl.program_id(0) > 0).astype(F32)
        cc = cc_ref[...].astype(F32) * cx_ref[...].astype(F32)
        prev = pc_ref[...].astype(F32)[8:] * px_ref[...].astype(F32)[8:] * has_prev
        w = w_ref[...]
        conv = w[0:1] * _shift_down(cc, prev, 2) + w[1:2] * _shift_down(cc, prev, 1) + w[2:3] * cc
        ycin = (cb_ref[...].astype(F32) * conv).astype(BF16)
        o_ref[...] = ycin
        y_ref[...] = _dot(ycin, wo_ref[...], NN).astype(BF16)

    out = jax.ShapeDtypeStruct((t, d), BF16)
    return _pcall(body, name=name, out_shape=(out, out), grid=(t // tt,),
                  in_specs=[_rows(tt, d, 0), _rows(tt, d, 1), _rows(tt, d, 2), _prev_rows(tt, d, 1),
                            _prev_rows(tt, d, 2), _whole((8, d)), _whole((d, d))],
                  out_specs=(_rows(tt, d), _rows(tt, d)),
                  compiler_params=_params(("parallel",)))(cbx, cbx, cbx, cbx, cbx, cw8, w_out)


def conv_out_bwd(name, dyc, w_out, cbx, cw8, tt):
    t = cbx.shape[0]
    d = w_out.shape[0]
    d3 = 3 * d
    n = t // tt

    def body(dy_ref, ndy_ref, wo_ref, cb_ref, cc_ref, cx_ref, pc_ref, px_ref, ncb_ref, w_ref, o_ref, dw_ref):
        i = pl.program_id(0)
        has_prev = (i > 0).astype(F32)
        has_next = (i < n - 1).astype(F32)
        cb = cb_ref[...].astype(F32)
        ccv = cc_ref[...].astype(F32)
        cxv = cx_ref[...].astype(F32)
        cc = ccv * cxv
        prev = pc_ref[...].astype(F32)[8:] * px_ref[...].astype(F32)[8:] * has_prev
        w = w_ref[...]
        cc1 = _shift_down(cc, prev, 1)
        cc2 = _shift_down(cc, prev, 2)
        conv = w[0:1] * cc2 + w[1:2] * cc1 + w[2:3] * cc
        dyv = _dot(dy_ref[...], wo_ref[...], NT)
        dconv = dyv * cb
        dnext = _dot(ndy_ref[...], wo_ref[...], NT)[:8] * ncb_ref[...].astype(F32)[:8] * has_next
        dcc = w[2:3] * dconv + w[1:2] * _shift_up(dconv, dnext, 1) + w[0:1] * _shift_up(dconv, dnext, 2)
        o_ref[:, :d] = (dyv * conv).astype(BF16)
        o_ref[:, d:2 * d] = (dcc * cxv).astype(BF16)
        o_ref[:, 2 * d:] = (dcc * ccv).astype(BF16)

        @pl.when(i == 0)
        def _():
            dw_ref[...] = jnp.zeros_like(dw_ref)

        dw_ref[0:1, :] += jnp.sum(dconv * cc2, axis=0, keepdims=True)
        dw_ref[1:2, :] += jnp.sum(dconv * cc1, axis=0, keepdims=True)
        dw_ref[2:3, :] += jnp.sum(dconv * cc, axis=0, keepdims=True)

    return _pcall(body, name=name,
                  out_shape=(jax.ShapeDtypeStruct((t, d3), BF16), jax.ShapeDtypeStruct((8, d), F32)),
                  grid=(n,),
                  in_specs=[_rows(tt, d), _next_rows(tt, d, 0, t),
                            _whole((d, d)), _rows(tt, d, 0), _rows(tt, d, 1), _rows(tt, d, 2),
                            _prev_rows(tt, d, 1), _prev_rows(tt, d, 2), _next_rows(tt, d, 0, t), _whole((8, d))],
                  out_specs=(_rows(tt, d3), _whole((8, d))),
                  compiler_params=_params(("arbitrary",)))(dyc, dyc, w_out, cbx, cbx, cbx, cbx, cbx, cbx, cw8)


def tail(name, h3, p, tgt, gp, gf, w_gate, w_proj, tt):
    t, d = h3.shape
    pd = p.shape[1]

    def body(h_ref, p_ref, tg_ref, gp_ref, gf_ref, wg_ref, wp_ref, np_ref, dh_ref, dpp_ref, dzg_ref, dgf_ref,
             loss_ref):
        hv = h_ref[...]
        npl = (hv * _rstd(hv) * gp_ref[...]).astype(BF16)
        np_ref[...] = npl
        pg = jax.nn.sigmoid(_dot(npl, wg_ref[...], NN))
        ppv = _dot(p_ref[...].astype(BF16), wp_ref[...], NN)
        h4 = hv + pg * ppv
        r4 = _rstd(h4)
        hn = h4 * r4
        gfv = gf_ref[...]
        err = hn * gfv - tg_ref[...]
        dy = err * (1.0 / d)
        gy = dy * gfv
        dh4 = r4 * (gy - hn * jnp.mean(gy * hn, axis=-1, keepdims=True))
        dh_ref[...] = dh4
        dpp_ref[...] = (dh4 * pg).astype(BF16)
        dzg_ref[...] = (dh4 * ppv * pg * (1.0 - pg)).astype(BF16)

        @pl.when(pl.program_id(0) == 0)
        def _():
            dgf_ref[...] = jnp.zeros_like(dgf_ref)
            loss_ref[...] = jnp.zeros_like(loss_ref)

        dgf_ref[...] += jnp.sum(dy * hn, axis=0, keepdims=True)
        tok = jnp.mean(err * err, axis=-1, keepdims=True)
        loss_ref[...] += 0.5 * jnp.sum(tok, axis=0, keepdims=True) * jnp.ones((1, loss_ref.shape[1]), F32)

    return _pcall(body, name=name,
                  out_shape=(jax.ShapeDtypeStruct((t, d), BF16), jax.ShapeDtypeStruct((t, d), F32),
                             jax.ShapeDtypeStruct((t, d), BF16), jax.ShapeDtypeStruct((t, d), BF16),
                             jax.ShapeDtypeStruct((1, d), F32), jax.ShapeDtypeStruct((1, d), F32)),
                  grid=(t // tt,),
                  in_specs=[_rows(tt, d), _rows(tt, pd), _rows(tt, d), _whole((1, d)), _whole((1, d)),
                            _whole((d, d)), _whole((pd, d))],
                  out_specs=(_rows(tt, d), _rows(tt, d), _rows(tt, d), _rows(tt, d), _whole((1, d)),
                             _whole((1, d))),
                  compiler_params=_params(("arbitrary",)))(h3, p, tgt, gp, gf, w_gate, w_proj)


SCALE = 1.0 / math.sqrt(HEAD_DIM)


def _log_stick(z):
    return -(jnp.maximum(z, 0.0) + jnp.log(1.0 + jnp.exp(-jnp.abs(z))))


def _tri_sum(x, tri):
    hi = x.astype(BF16)
    lo = (x - hi.astype(F32)).astype(BF16)
    return _dot(hi, tri, NN) + _dot(lo, tri, NN)


KEY_BLOCK = 128
NEAR = 3
THIN_ROWS = 32


def _pad_block(x):
    n = x.shape[0]
    return x if n == KEY_BLOCK else jnp.concatenate([x, jnp.zeros((KEY_BLOCK - n, x.shape[1]), x.dtype)], axis=0)


def _sb_near(qs, jds, k_ref, below, upper, last_rows):
    near_rows = (KEY_BLOCK,) * (NEAR - 1) + (last_rows,)
    pairs = [(s, b) for s in range(len(qs)) for b in range(NEAR)]
    rows = {(s, b): _block_rows(jnp.maximum(jds[s] - b, 0), KEY_BLOCK) for s, b in pairs}
    z = {(s, b): _dot(qs[s][:near_rows[b]], k_ref[rows[s, b], :], NT) * SCALE for s, b in pairs}
    lg = {(s, b): jnp.where(below, _log_stick(z[s, b]), 0.0) if b == 0 else _log_stick(z[s, b]) for s, b in pairs}
    cum = {(s, b): _tri_sum(lg[s, b], upper) for s, b in pairs}
    out, carries = [], []
    for s in range(len(qs)):
        c = cum[s, 0][:, 0:1]
        blocks = [(rows[s, 0], z[s, 0], jnp.exp(jnp.where(below, z[s, 0] + cum[s, 0], -1e30)))]
        for b in range(1, NEAR):
            live = jds[s] >= b
            off = c[:near_rows[b]] + jnp.where(live, 0.0, -1e30)
            blocks.append((rows[s, b], z[s, b], jnp.exp(z[s, b] + cum[s, b] + off)))
            c = c + _pad_block(jnp.where(live, cum[s, b][:, 0:1], 0.0))
        out.append(blocks)
        carries.append(c)
    return out, carries


def _sb_far(q, kj, upper, c, skip):
    z = _dot(q, kj, NT) * SCALE
    cum = _tri_sum(_log_stick(z), upper)
    return z, jnp.exp(z + cum + (c + jnp.where(skip, -1e30, 0.0))), c + jnp.where(skip, 0.0, cum[:, 0:1])


def _took_it(j, jd, last_rows):
    first = lax.broadcasted_iota(jnp.int32, (KEY_BLOCK, 1), 0) < last_rows
    return jnp.logical_and(j == jd - (NEAR - 1), first)


def _block_rows(j, size):
    return pl.ds(pl.multiple_of(j * size, size), size)


def _sweep_on(st):
    return jnp.logical_and(st[0] >= 0, jnp.max(st[1]) > -STICK_EXIT)


def attn_fwd(name, qkv, tq, d, col0=0):
    t = qkv.shape[0]
    nh = d // HEAD_DIM
    q0 = col0 // HEAD_DIM
    nq = t // tq
    tb = KEY_BLOCK
    nsub = tq // tb

    def body(q_ref, k_ref, v_ref, o_ref):
        i = pl.program_id(1)
        row = lax.broadcasted_iota(jnp.int32, (tb, tb), 0)
        col = lax.broadcasted_iota(jnp.int32, (tb, tb), 1)
        upper = (row >= col).astype(BF16)
        qs = [q_ref[s * tb:(s + 1) * tb, :] for s in range(nsub)]
        jds = [i * nsub + s for s in range(nsub)]
        near, carries = _sb_near(qs, jds, k_ref, col < row, upper, THIN_ROWS)
        state = []
        for s in range(nsub):
            acc = jnp.zeros((tb, HEAD_DIM), F32)
            for rows, _, a in near[s]:
                acc = acc + _pad_block(_dot(a.astype(BF16), v_ref[rows, :], NN))
            state.append((qs[s], jds[s], carries[s], acc))
        for s, (q, jd, c, acc) in enumerate(state):

            def step(st, q=q, jd=jd):
                rows = _block_rows(st[0], tb)
                _, a, c2 = _sb_far(q, k_ref[rows, :], upper, st[1], _took_it(st[0], jd, THIN_ROWS))
                return st[0] - 1, c2, st[2] + _dot(a.astype(BF16), v_ref[rows, :], NN)

            _, _, acc = lax.while_loop(_sweep_on, step, (jd - (NEAR - 1), c, acc))
            o_ref[s * tb:(s + 1) * tb, :] = acc.astype(o_ref.dtype)

    return _pcall(body, name=name, out_shape=jax.ShapeDtypeStruct((t, d), BF16), grid=(nh, nq),
                  in_specs=[pl.BlockSpec((tq, HEAD_DIM), lambda h, i: (i, q0 + h)),
                            pl.BlockSpec((t, HEAD_DIM), lambda h, i: (0, q0 + nh + h)),
                            pl.BlockSpec((t, HEAD_DIM), lambda h, i: (0, q0 + 2 * nh + h))],
                  out_specs=pl.BlockSpec((tq, HEAD_DIM), lambda h, i: (i, h)),
                  compiler_params=_params(("parallel", "arbitrary")))(qkv, qkv, qkv)


def attn_bwd(name, qkv, do, tq, col0=0):
    d = do.shape[1]
    t = qkv.shape[0]
    nh = d // HEAD_DIM
    q0 = col0 // HEAD_DIM
    nq = t // tq
    tb = KEY_BLOCK
    nsub = tq // tb

    def body(q_ref, k_ref, v_ref, do_ref, dq_ref, dk_ref, dv_ref, dk_acc, dv_acc, g_buf, z_buf):
        i = pl.program_id(1)

        @pl.when(i == 0)
        def _():
            dk_acc[...] = jnp.zeros_like(dk_acc)
            dv_acc[...] = jnp.zeros_like(dv_acc)

        row = lax.broadcasted_iota(jnp.int32, (tb, tb), 0)
        col = lax.broadcasted_iota(jnp.int32, (tb, tb), 1)
        below = col < row
        upper = (row >= col).astype(BF16)
        lower = (row <= col).astype(BF16)

        qs = [q_ref[s * tb:(s + 1) * tb, :] for s in range(nsub)]
        dos = [do_ref[s * tb:(s + 1) * tb, :] for s in range(nsub)]
        jds = [i * nsub + s for s in range(nsub)]
        near, carries = _sb_near(qs, jds, k_ref, below, upper, KEY_BLOCK)
        da = [[_dot(dos[s][:a.shape[0]], v_ref[rows, :], NT) for rows, _, a in near[s]] for s in range(nsub)]
        state = []
        for s in range(nsub):
            kept = [(rows, z, da[s][b] * a) for b, (rows, z, a) in enumerate(near[s])]
            for rows, _, a in near[s]:
                dv_acc[rows, :] += _dot(a.astype(BF16), dos[s][:a.shape[0]], TN)
            state.append((qs[s], dos[s], jds[s], carries[s], kept))

        carried = []
        for s, (q, dov, jd, c, kept) in enumerate(state):
            def step(st, s=s, q=q, dov=dov, jd=jd):
                j = st[0]
                rows = _block_rows(j, tb)
                z, a, c2 = _sb_far(q, k_ref[rows, :], upper, st[1], _took_it(j, jd, KEY_BLOCK))
                g_buf[jd - j] = _dot(dov, v_ref[rows, :], NT) * a
                z_buf[jd - j] = z
                dv_acc[rows, :] += _dot(a.astype(BF16), dov, TN)
                return j - 1, c2

            j_stop, _ = lax.while_loop(_sweep_on, step, (jd - (NEAR - 1), c))

            def far(j, st, s=s, q=q, jd=jd):
                run, dq = st
                rows = _block_rows(j, tb)
                g = g_buf[jd - j]
                dz = (g - jax.nn.sigmoid(z_buf[jd - j]) * (run + _tri_sum(g, lower))).astype(BF16)
                dk_acc[rows, :] += _dot(dz, q, TN)
                return run + jnp.sum(g, axis=1, keepdims=True), dq + _dot(dz, k_ref[rows, :], NN)

            carried.append(lax.fori_loop(j_stop + 1, jd - (NEAR - 1) + 1, far,
                                         (jnp.zeros((tb, 1), F32), jnp.zeros((tb, HEAD_DIM), F32))))

        tri = [[_dot(g.astype(BF16), lower, NN) for _, _, g in st[4]] for st in state]
        sig = [[jax.nn.sigmoid(z) for _, z, _ in st[4]] for st in state]
        for s, (q, dov, jd, c, kept) in enumerate(state):
            run, dq = carried[s]
            for b in reversed(range(NEAR)):
                rows, z, g = kept[b]
                n = g.shape[0]
                dz = g - sig[s][b] * (run[:n] + tri[s][b])
                if b == 0:
                    dz = jnp.where(below, dz, 0.0)
                dz = dz.astype(BF16)
                dk_acc[rows, :] += _dot(dz, q[:n], TN)
                dq = dq + _pad_block(_dot(dz, k_ref[rows, :], NN))
                if b:
                    run = run + _pad_block(jnp.sum(g, axis=1, keepdims=True))
            dq_ref[s * tb:(s + 1) * tb, :] = (dq * SCALE).astype(BF16)

        @pl.when(i == nq - 1)
        def _():
            dk_ref[...] = (dk_acc[...] * SCALE).astype(BF16)
            dv_ref[...] = dv_acc[...].astype(BF16)

    blk = pl.BlockSpec((tq, HEAD_DIM), lambda h, i: (i, h))
    col_h = pl.BlockSpec((t, HEAD_DIM), lambda h, i: (0, h))
    out = jax.ShapeDtypeStruct((t, d), BF16)
    return _pcall(body, name=name, out_shape=(out, out, out), grid=(nh, nq),
                  in_specs=[pl.BlockSpec((tq, HEAD_DIM), lambda h, i: (i, q0 + h)),
                            pl.BlockSpec((t, HEAD_DIM), lambda h, i: (0, q0 + nh + h)),
                            pl.BlockSpec((t, HEAD_DIM), lambda h, i: (0, q0 + 2 * nh + h)),
                            blk],
                  out_specs=(blk, col_h, col_h),
                  scratch_shapes=[pltpu.VMEM((t, HEAD_DIM), F32), pltpu.VMEM((t, HEAD_DIM), F32),
                                  pltpu.VMEM((t // tb, tb, tb), F32), pltpu.VMEM((t // tb, tb, tb), F32)],
                  compiler_params=_params(("parallel", "arbitrary")))(qkv, qkv, qkv, do)


def _place():
    x, y, c = lax.axis_index("x"), lax.axis_index("y"), lax.axis_index("c")
    chips = [(1 - x, y), (x, 1 - y), (1 - x, 1 - y)]
    return x, y, c, chips


def _remote(src, dst, send_sem, recv_sem, dev):
    return pltpu.make_async_remote_copy(src_ref=src, dst_ref=dst, send_sem=send_sem, recv_sem=recv_sem,
                                        device_id=dev, device_id_type=MESH)


def place_shards(name, ws, chip):
    tiles, steps = _job_tiles([w.shape for w in ws], 1 << 20, BF16_ROWS)
    nj = len(ws)

    def body(chip_ref, *refs):
        i = pl.program_id(0)
        for k, (_, n) in enumerate(tiles):
            @pl.when(i < n)
            def _(w_ref=refs[k], o_ref=refs[nj + k]):
                o_ref[...] = w_ref[...].astype(BF16)

    spec = pltpu.PrefetchScalarGridSpec(
        num_scalar_prefetch=1, grid=(steps,),
        in_specs=[pl.BlockSpec((tr, w.shape[1]), lambda i, s, n=n: (jnp.minimum(i, n - 1), 0))
                  for w, (tr, n) in zip(ws, tiles)],
        out_specs=[pl.BlockSpec((None, tr, w.shape[1]), lambda i, s, n=n: (s[0], jnp.minimum(i, n - 1), 0))
                   for w, (tr, n) in zip(ws, tiles)])
    return _pcall(body, name=name, out_shape=[jax.ShapeDtypeStruct((N_CHIPS,) + w.shape, BF16) for w in ws],
                  grid_spec=spec, compiler_params=_params(("arbitrary",)))(chip, *ws)


class Comm:
    def __init__(self, ins, outs, aliases, sems, first, mid, last):
        self.ins, self.outs, self.aliases, self.sems = list(ins), list(outs), dict(aliases), list(sems)
        self.first, self.mid, self.last = first, mid, last


def run_comm(name, comm):
    ni, no = len(comm.ins), len(comm.outs)

    def body(*refs):
        ins, outs, sems = refs[:ni], refs[ni:ni + no], refs[ni + no:]
        comm.first(ins, outs, sems)
        comm.mid(ins, outs, sems)
        comm.last(ins, outs, sems)

    return _pcall(body, name=name, out_shape=comm.outs, in_specs=[ANY] * ni, out_specs=[ANY] * no,
                  input_output_aliases=comm.aliases, scratch_shapes=comm.sems, compiler_params=_params())(*comm.ins)


def gather_comm(bufs):
    n = len(bufs)

    def half(out, w, which):
        pr = out[w].shape[1] // 2
        return pl.ds(pl.multiple_of(which * pr, BF16_ROWS), pr)

    def first(ins, out, sems):
        isend, irecv, _, _ = sems
        x, y, c, chips = _place()
        for w in range(n):
            mine = out[w].at[2 * x + y, half(out, w, c)]
            for j, (cx, cy) in enumerate(chips):
                _remote(mine, mine, isend.at[3 * w + j], irecv.at[3 * w + j], (cx, cy, c)).start()

    def mid(ins, out, sems):
        isend, irecv, dsend, drecv = sems
        x, y, c, chips = _place()
        sib = (x, y, 1 - c)
        for w in range(n):
            for j, (cx, cy) in enumerate(chips):
                landed = out[w].at[2 * cx + cy, half(out, w, c)]
                _remote(landed, landed, isend.at[3 * w + j], irecv.at[3 * w + j], sib).wait_recv()
                _remote(landed, landed, dsend.at[3 * w + j], drecv.at[3 * w + j], sib).start()

    def last(ins, out, sems):
        isend, irecv, dsend, drecv = sems
        x, y, c, chips = _place()
        sib = (x, y, 1 - c)
        for w in range(n):
            for j, (cx, cy) in enumerate(chips):
                landed = out[w].at[2 * cx + cy, half(out, w, 1 - c)]
                _remote(landed, landed, dsend.at[3 * w + j], drecv.at[3 * w + j], sib).wait_recv()
        for w in range(n):
            sent = out[w].at[0, half(out, w, c)]
            for j in range(3):
                _remote(sent, sent, isend.at[3 * w + j], irecv.at[3 * w + j], sib).wait_send()
                _remote(sent, sent, dsend.at[3 * w + j], drecv.at[3 * w + j], sib).wait_send()

    return Comm(bufs, [jax.ShapeDtypeStruct(s.shape, s.dtype) for s in bufs], {w: w for w in range(n)},
                [pltpu.SemaphoreType.DMA((3 * n,))] * 4, first, mid, last)


def _nothing(ins, outs, sems):
    return None


def join_comms(a, b):
    ni, no, ns = len(a.ins), len(a.outs), len(a.sems)

    def both(f, g):
        def hook(ins, outs, sems):
            f(ins[:ni], outs[:no], sems[:ns])
            g(ins[ni:], outs[no:], sems[ns:])
        return hook

    aliases = dict(a.aliases)
    aliases.update({ni + k: no + v for k, v in b.aliases.items()})
    return Comm(a.ins + b.ins, a.outs + b.outs, aliases, a.sems + b.sems,
                both(a.first, b.first), both(a.mid, b.mid), both(a.last, b.last))


def exchange_comm(pieces):
    n = len(pieces)

    def copies(src, out, sems):
        x, y, c, _ = _place()
        return [_remote(src[w].at[k, 1 - c], out[w].at[k], sems[0].at[N_CHIPS * w + k], sems[1].at[N_CHIPS * w + k],
                        (x, y, 1 - c)) for w in range(n) for k in range(N_CHIPS)]

    def first(src, out, sems):
        for cp in copies(src, out, sems):
            cp.start()

    def last(src, out, sems):
        for cp in copies(src, out, sems):
            cp.wait()

    return Comm(pieces, [jax.ShapeDtypeStruct((N_CHIPS,) + s.shape[2:], s.dtype) for s in pieces], {},
                [pltpu.SemaphoreType.DMA((N_CHIPS * n,))] * 2, first, _nothing, last)


def scatter_comm(parts):
    n = len(parts)

    def copies(src, out, sems):
        x, y, c, chips = _place()
        return [_remote(src[w].at[2 * cx + cy], out[w].at[j], sems[0].at[3 * w + j], sems[1].at[3 * w + j], (cx, cy, c))
                for w in range(n) for j, (cx, cy) in enumerate(chips)]

    def first(src, out, sems):
        for cp in copies(src, out, sems):
            cp.start()

    def last(src, out, sems):
        for cp in copies(src, out, sems):
            cp.wait()

    return Comm(parts, [jax.ShapeDtypeStruct((3,) + s.shape[1:], s.dtype) for s in parts], {},
                [pltpu.SemaphoreType.DMA((3 * n,))] * 2, first, _nothing, last)


def share_comm(halves):
    n = len(halves)

    def first(ins, buf, sems):
        x, y, c, _ = _place()
        for w in range(n):
            _remote(buf[w].at[c], buf[w].at[c], sems[0].at[w], sems[1].at[w], (x, y, 1 - c)).start()

    def last(ins, buf, sems):
        x, y, c, _ = _place()
        for w in range(n):
            landed = buf[w].at[1 - c]
            _remote(landed, landed, sems[0].at[w], sems[1].at[w], (x, y, 1 - c)).wait_recv()
        for w in range(n):
            _remote(buf[w].at[c], buf[w].at[c], sems[0].at[w], sems[1].at[w], (x, y, 1 - c)).wait_send()

    return Comm(halves, [jax.ShapeDtypeStruct(s.shape, s.dtype) for s in halves], {w: w for w in range(n)},
                [pltpu.SemaphoreType.DMA((n,))] * 2, first, _nothing, last)


def gather_small(name, blk, reduce):
    r, cdim = blk.shape

    def body(in_ref, out_ref, *rest):
        if reduce:
            buf, send_sem, recv_sem = rest
        else:
            buf = out_ref
            send_sem, recv_sem = rest
        x, y, c, _ = _place()
        me = 4 * x + 2 * y + c
        buf[me] = in_ref[...]
        peers = []
        for dx in range(2):
            for dy in range(2):
                for dc in range(2):
                    if dx or dy or dc:
                        peers.append((dx, dy, dc))
        copies = []
        for s, (dx, dy, dc) in enumerate(peers):
            cp = _remote(in_ref, buf.at[me], send_sem.at[s], recv_sem.at[s],
                         ((1 - x if dx else x), (1 - y if dy else y), (1 - c if dc else c)))
            cp.start()
            copies.append(cp)
        for s, (dx, dy, dc) in enumerate(peers):
            px, py, pc_ = (1 - x if dx else x), (1 - y if dy else y), (1 - c if dc else c)
            landed = buf.at[4 * px + 2 * py + pc_]
            _remote(landed, landed, send_sem.at[s], recv_sem.at[s], (x, y, c)).wait_recv()
        for cp in copies:
            cp.wait_send()
        if reduce:
            tot = buf[0]
            for s in range(1, N_DEV):
                tot = tot + buf[s]
            out_ref[...] = tot

    vm = pl.BlockSpec(memory_space=pltpu.VMEM)
    out_shape = jax.ShapeDtypeStruct((r, cdim) if reduce else (N_DEV, r, cdim), F32)
    scratch = ([pltpu.VMEM((N_DEV, r, cdim), F32)] if reduce else []) + [pltpu.SemaphoreType.DMA((N_DEV - 1,))] * 2
    return _pcall(body, name=name, out_shape=out_shape, in_specs=[vm], out_specs=vm, scratch_shapes=scratch,
                  compiler_params=_params())(blk)


def _job_tiles(shapes, tile_bytes, mult):
    tiles = []
    for rows, cols in shapes:
        tr = _tile(rows, max(mult, tile_bytes // (4 * cols)), mult)
        tiles.append((tr, rows // tr))
    return tiles, max(n for _, n in tiles)


def sum_cores(name, owns, gots, place):
    nj = len(owns)
    tiles, _ = _job_tiles([o.shape[2:] for o in owns], 1 << 21, BF16_ROWS)
    steps = max(N_CHIPS * n for _, n in tiles)

    def body(place_ref, *refs):
        i = pl.program_id(0)
        for k, (_, n) in enumerate(tiles):
            @pl.when(i < N_CHIPS * n)
            def _(own_ref=refs[2 * k], got_ref=refs[2 * k + 1], o_ref=refs[2 * nj + k]):
                o_ref[...] = (own_ref[...].astype(F32) + got_ref[...].astype(F32)).astype(o_ref.dtype)

    in_specs, out_specs, out_shape, args = [], [], [], []
    for own, got, (tr, n) in zip(owns, gots, tiles):
        pc = own.shape[3]
        last = N_CHIPS * n - 1
        in_specs += [pl.BlockSpec((None, None, tr, pc),
                                  lambda i, s, n=n, last=last: (jnp.minimum(i, last) // n, s[1], jnp.minimum(i, last) % n, 0)),
                     pl.BlockSpec((None, tr, pc),
                                  lambda i, s, n=n, last=last: (jnp.minimum(i, last) // n, jnp.minimum(i, last) % n, 0))]
        out_specs.append(pl.BlockSpec((None, tr, pc),
                                      lambda i, s, n=n, last=last: (jnp.minimum(i, last) // n, jnp.minimum(i, last) % n, 0)))
        out_shape.append(jax.ShapeDtypeStruct(got.shape, BF16))
        args += [own, got]
    spec = pltpu.PrefetchScalarGridSpec(num_scalar_prefetch=1, grid=(steps,), in_specs=in_specs, out_specs=out_specs)
    return _pcall(body, name=name, out_shape=out_shape, grid_spec=spec,
                  compiler_params=_params(("arbitrary",)))(place, *args)


def sum_chips(name, parts, gots, place):
    nj = len(parts)
    tiles, steps = _job_tiles([p.shape[1:] for p in parts], 1 << 20, BF16_ROWS)

    def body(place_ref, *refs):
        i = pl.program_id(0)
        for k, (_, n) in enumerate(tiles):
            @pl.when(i < n)
            def _(part_ref=refs[2 * k], got_ref=refs[2 * k + 1], o_ref=refs[2 * nj + k]):
                tot = part_ref[...].astype(F32)
                for j in range(3):
                    tot = tot + got_ref[j].astype(F32)
                o_ref[...] = tot

    in_specs, out_specs, out_shape, args = [], [], [], []
    for part, got, (tr, n) in zip(parts, gots, tiles):
        pc = part.shape[2]
        in_specs += [pl.BlockSpec((None, tr, pc), lambda i, s, n=n: (s[0], jnp.minimum(i, n - 1), 0)),
                     pl.BlockSpec((3, tr, pc), lambda i, s, n=n: (0, jnp.minimum(i, n - 1), 0))]
        out_specs.append(pl.BlockSpec((None, tr, pc), lambda i, s, n=n: (s[1], jnp.minimum(i, n - 1), 0)))
        out_shape.append(jax.ShapeDtypeStruct((2,) + part.shape[1:], F32))
        args += [part, got]
    spec = pltpu.PrefetchScalarGridSpec(num_scalar_prefetch=1, grid=(steps,), in_specs=in_specs, out_specs=out_specs)
    return _pcall(body, name=name, out_shape=out_shape, grid_spec=spec,
                  compiler_params=_params(("arbitrary",)))(place, *args)


def adamw(name, jobs):
    c1 = 1.0 / (1.0 - ADAM_B1 ** ADAM_STEP)
    c2 = 1.0 / (1.0 - ADAM_B2 ** ADAM_STEP)
    nj = len(jobs)
    tiles, steps = _job_tiles([j[0].shape for j in jobs], 1 << 18, 8)

    def body(*refs):
        i = pl.program_id(0)
        for k, (_, n) in enumerate(tiles):
            w_ref, g_ref, m_ref, v_ref = refs[4 * k:4 * k + 4]
            d_ref, nm_ref, nv_ref = refs[4 * nj + 3 * k:4 * nj + 3 * k + 3]

            @pl.when(i < n)
            def _(w_ref=w_ref, g_ref=g_ref, m_ref=m_ref, v_ref=v_ref, d_ref=d_ref, nm_ref=nm_ref, nv_ref=nv_ref):
                gv = g_ref[...]
                nm = ADAM_B1 * m_ref[...] + (1.0 - ADAM_B1) * gv
                nv = ADAM_B2 * v_ref[...] + (1.0 - ADAM_B2) * (gv * gv)
                nm_ref[...] = nm
                nv_ref[...] = nv
                d_ref[...] = -ADAM_LR * ((nm * c1) / (jnp.sqrt(nv * c2) + ADAM_EPS) + ADAM_WD * w_ref[...])

    in_specs, out_specs, out_shape, args = [], [], [], []
    for (w, g, m, v), (tr, n) in zip(jobs, tiles):
        spec = pl.BlockSpec((tr, w.shape[1]), lambda i, n=n: (jnp.minimum(i, n - 1), 0))
        in_specs += [spec] * 4
        out_specs += [spec] * 3
        out_shape += [jax.ShapeDtypeStruct(w.shape, F32)] * 3
        args += [w, g, m, v]
    res = _pcall(body, name=name, out_shape=out_shape, grid=(steps,), in_specs=in_specs, out_specs=out_specs,
                 compiler_params=_params(("arbitrary",)))(*args)
    return [tuple(res[3 * k:3 * k + 3]) for k in range(nj)]


MATS = ["ffn1_w_in", "ffn1_w_out", "w_mix_in", "w_conv_out", "w_attn_out", "w_mix_out", "ffn2_w_in", "ffn2_w_out",
        "w_ple_gate", "w_ple_proj"]
COL_SHARDED = {"ffn1_w_in", "w_mix_in", "ffn2_w_in", "w_ple_proj"}
NORMS = ["ffn1_norm", "mix_norm", "ffn2_norm", "ple_norm", "final_norm"]
WEIGHTS = ["ffn1_norm", "ffn1_w_in", "ffn1_w_out", "mix_norm", "w_mix_in", "conv_w", "w_conv_out", "w_attn_out",
           "w_mix_out", "ffn2_norm", "ffn2_w_in", "ffn2_w_out", "ple_norm", "w_ple_gate", "w_ple_proj", "final_norm"]


def _pad_rows(a, rows):
    return jnp.concatenate([a, jnp.zeros((rows - a.shape[0],) + a.shape[1:], a.dtype)], axis=0)


def _step(x, p, tgt, w, m, v):
    t, d = x.shape
    tt = _tile(t, 256)
    tm = _tile(t, 512)
    tm2 = _tile(t, 1024)
    tq = _tile(t, 1024)

    chip = 2 * lax.axis_index("x") + lax.axis_index("y")
    place = jnp.stack([chip, lax.axis_index("c")]).astype(jnp.int32)

    placed = dict(zip(MATS, place_shards("place_shards", [w[k] for k in MATS], place)))
    full = {}

    def keep(names, bufs):
        for k, buf in zip(names, bufs):
            full[k] = buf if k in COL_SHARDED else buf.reshape(-1, buf.shape[2])

    def gather_of(names):
        return gather_comm([placed[k] for k in names])

    cw_all = gather_small("gather_conv_w", _pad_rows(w["conv_w"], 8), False)
    cw8 = jnp.concatenate([cw_all[2 * k] for k in range(N_CHIPS)], axis=1)
    g1, gm, g2, gp, gf = (w[k].reshape(1, d) for k in NORMS)

    def ffn_fwd(tag, h, g, first, w_in_name, w_out_name, riders):
        if first:
            n, bufs = rms_fwd(tag + "_norm", h, g, tt, comm=gather_of(first))
            keep(first, bufs)
            (a, s), bufs = ffn_in_act(tag + "_in", n, full[w_in_name], tm, comm=gather_of(riders))
            keep(riders, bufs)
        else:
            a, s, n = ffn_in_act(tag + "_in", h, full[w_in_name], tm, gain=g)
        return n, a, s, mm_nn(tag + "_out", s, full[w_out_name], F32, tm, res=h, alpha=0.5)

    n1, a1, s1, h1 = ffn_fwd("ffn1", x, g1, ["ffn1_w_in"], "ffn1_w_in", "ffn1_w_out", ["ffn1_w_out", "w_mix_in"])
    wmix = full["w_mix_in"]
    riders = ["w_conv_out", "w_attn_out", "w_mix_out", "ffn2_w_in", "ffn2_w_out", "w_ple_gate", "w_ple_proj"]
    (mixin, u), bufs = mm_nn_stacked("mix_in", h1, wmix, BF16, tm2, d, comm=gather_of(riders), gain=gm)
    keep(riders, bufs)
    cbx = qkv = gates = mixin
    wpp = full["w_ple_proj"]
    wpp = jnp.transpose(wpp, (1, 0, 2)).reshape(wpp.shape[1], -1)
    ycin, y_conv = conv_out_fwd("conv_out", cbx, cw8, full["w_conv_out"], tm)
    o = attn_fwd("attn", qkv, tq, d, 3 * d)
    y_attn = mm_nn("attn_out", o, full["w_attn_out"], BF16, tm2)
    merged, h2 = mix_out_fwd("mix_out", gates, y_conv, y_attn, h1, full["w_mix_out"], tm, 3)
    n2, a2, s2, h3 = ffn_fwd("ffn2", h2, g2, [], "ffn2_w_in", "ffn2_w_out", [])

    pieces, chip_sums, halves = {}, {}, {}

    def as_pieces(k):
        pc = pieces[k]
        return pc if k in COL_SHARDED else pc.reshape(N_CHIPS, 2, pc.shape[0] // (2 * N_CHIPS), pc.shape[1])

    def sum_siblings(tag, names):
        pcs = [as_pieces(k) for k in names]
        got = run_comm("exchange_" + tag, exchange_comm(pcs))
        chip_sums.update(zip(names, sum_cores("sum_cores_" + tag, pcs, got, place)))

    def scatter_of(names):
        return scatter_comm([chip_sums[k] for k in names])

    def sum_landed(tag, names, landed):
        halves.update(zip(names, sum_chips("sum_chips_" + tag, [chip_sums[k] for k in names], landed, place)))

    npl, dh4, dpp, dzg, dgf, loss_row = tail("tail", h3, p, tgt, gp, gf, full["w_ple_gate"], wpp, tm)
    dwpp = mm_tn_whole("ple_proj_dw", p, dpp, tm2)
    pieces["w_ple_proj"] = jnp.transpose(dwpp.reshape(2, p.shape[1] // 2, N_CHIPS, d // N_CHIPS), (2, 0, 1, 3))
    pieces["w_ple_gate"] = mm_tn_rows("ple_gate_dw", npl, dzg, tm2)
    dh3, df2, dgp = mm_nt("ple_gate_dx", dzg, full["w_ple_gate"], F32, tm2, d, norm=(h3, gp, dh4), alpha=0.5)
    w_in, w_out = full["ffn2_w_in"], full["ffn2_w_out"]
    pieces["ffn2_w_out"] = mm_tn_rows("ffn2_dwout", s2, df2, tm2)
    da2 = ffn_ds_dact("ffn2_ds", df2, w_out, a2, tm2)
    pieces["ffn2_w_in"] = mm_tn_cols("ffn2_dwin", n2, da2, tm2)
    dh2, dh2b, dg2 = mm_nt_stacked("ffn2_dn", da2, w_in, tm2, (h2, g2, dh3))
    pieces["w_mix_out"] = mm_tn_rows("mix_out_dw", merged, dh2b, tm2)
    dyc, dya, dgates = mix_out_bwd("mix_out_dx", dh2b, full["w_mix_out"], gates, y_conv, y_attn, tm, 3)
    pieces["w_conv_out"] = mm_tn_rows("conv_out_dw", ycin, dyc, tm2)
    dcbx, dcw8 = conv_out_bwd("conv_out_dx", dyc, full["w_conv_out"], cbx, cw8, tt)
    pieces["w_attn_out"] = mm_tn_rows("attn_out_dw", o, dya, tm2)
    do = mm_nt("attn_out_dx", dya, full["w_attn_out"], BF16, tm2, d)
    dq, dk, dv = attn_bwd("attn_bwd", qkv, do, tq, 3 * d)
    dmix = [dcbx, dq, dk, dv, dgates]
    early = ["ffn2_w_in", "ffn2_w_out", "w_ple_gate", "w_ple_proj", "w_mix_out", "w_conv_out", "w_attn_out"]
    swap = exchange_comm([as_pieces(k) for k in early])
    pieces["w_mix_in"], got = mm_tn_parts("mix_in_dw", u, dmix, tm2, comm=swap)
    chip_sums.update(zip(early, sum_cores("sum_cores_early", swap.ins, got, place)))
    swap = exchange_comm([as_pieces("w_mix_in")])
    (dh1, df1, dgm), landed = mm_nt_parts("mix_in_dx", dmix, wmix, tm2, (h1, gm, dh2), 0.5,
                                          comm=join_comms(scatter_of(early), swap))
    sum_landed("early", early, landed[:len(early)])
    chip_sums["w_mix_in"] = sum_cores("sum_cores_mix", swap.ins, landed[len(early):], place)[0]
    w_in, w_out = full["ffn1_w_in"], full["ffn1_w_out"]
    pieces["ffn1_w_out"] = mm_tn_rows("ffn1_dwout", s1, df1, tm2)
    da1 = ffn_ds_dact("ffn1_ds", df1, w_out, a1, tm2)
    pieces["ffn1_w_in"], landed = mm_tn_cols("ffn1_dwin", n1, da1, tm2, comm=scatter_of(["w_mix_in"]))
    sum_landed("mix", ["w_mix_in"], landed)
    late = ["ffn1_w_in", "ffn1_w_out"]
    sum_siblings("late", late)
    done = early + ["w_mix_in"]
    (dx, _, dg1), landed = mm_nt_stacked(
        "ffn1_dn", da1, w_in, tm2, (x, g1, dh1),
        comm=join_comms(scatter_of(late), share_comm([halves[k] for k in done])))
    sum_landed("late", late, landed[:len(late)])
    shared = dict(zip(done, landed[len(late):]))

    shared.update(zip(late, run_comm("share_halves", share_comm([halves[k] for k in late]))))
    grad, delta, new_m, new_v = {}, {}, {}, {}
    for k in MATS:
        grad[k] = shared[k].reshape(w[k].shape)

    small = jnp.concatenate([dg1, dgm, dg2, dgp, dgf, dcw8[:3], loss_row, jnp.zeros((7, d), F32)], axis=0)
    tot = gather_small("sum_small", small, True)
    loss = tot[8, 0]
    norm_w = jnp.concatenate([w[k].reshape(1, d) for k in NORMS] + [jnp.zeros((3, d), F32)], axis=0)
    norm_m = jnp.concatenate([m[k].reshape(1, d) for k in NORMS] + [jnp.zeros((3, d), F32)], axis=0)
    norm_v = jnp.concatenate([v[k].reshape(1, d) for k in NORMS] + [jnp.ones((3, d), F32)], axis=0)
    norm_g = jnp.concatenate([tot[0:5], jnp.zeros((3, d), F32)], axis=0)
    cs = d // N_CHIPS
    gcw = lax.dynamic_slice(tot[5:8], (0, chip * cs), (3, cs))
    conv_job = (_pad_rows(w["conv_w"], 8), _pad_rows(gcw, 8), _pad_rows(m["conv_w"], 8),
                jnp.concatenate([v["conv_w"], jnp.ones((5, cs), F32)], axis=0))

    steps = adamw("adamw", [(w[k], grad[k], m[k], v[k]) for k in MATS]
                  + [(norm_w, norm_g, norm_m, norm_v), conv_job])
    for k, res in zip(MATS, steps):
        delta[k], new_m[k], new_v[k] = res
    nd, nm, nv = steps[len(MATS)]
    for r, k in enumerate(NORMS):
        grad[k] = norm_g[r].reshape(w[k].shape)
        delta[k], new_m[k], new_v[k] = (a[r].reshape(w[k].shape) for a in (nd, nm, nv))
    cd, cm, cv = steps[len(MATS) + 1]
    grad["conv_w"], delta["conv_w"], new_m["conv_w"], new_v["conv_w"] = gcw, cd[:3], cm[:3], cv[:3]
    return loss, dx, grad, delta, new_m, new_v


def kernel(x, p, ffn1_norm, ffn1_w_in, ffn1_w_out, mix_norm, w_mix_in, conv_w, w_conv_out, w_attn_out, w_mix_out, ffn2_norm, ffn2_w_in, ffn2_w_out, ple_norm, w_ple_gate, w_ple_proj, final_norm, loss_target, m_ffn1_norm, m_ffn1_w_in, m_ffn1_w_out, m_mix_norm, m_w_mix_in, m_conv_w, m_w_conv_out, m_w_attn_out, m_w_mix_out, m_ffn2_norm, m_ffn2_w_in, m_ffn2_w_out, m_ple_norm, m_w_ple_gate, m_w_ple_proj, m_final_norm, v_ffn1_norm, v_ffn1_w_in, v_ffn1_w_out, v_mix_norm, v_w_mix_in, v_conv_w, v_w_conv_out, v_w_attn_out, v_w_mix_out, v_ffn2_norm, v_ffn2_w_in, v_ffn2_w_out, v_ple_norm, v_w_ple_gate, v_w_ple_proj, v_final_norm):
    ws = (ffn1_norm, ffn1_w_in, ffn1_w_out, mix_norm, w_mix_in, conv_w, w_conv_out, w_attn_out, w_mix_out, ffn2_norm,
          ffn2_w_in, ffn2_w_out, ple_norm, w_ple_gate, w_ple_proj, final_norm)
    ms = (m_ffn1_norm, m_ffn1_w_in, m_ffn1_w_out, m_mix_norm, m_w_mix_in, m_conv_w, m_w_conv_out, m_w_attn_out,
          m_w_mix_out, m_ffn2_norm, m_ffn2_w_in, m_ffn2_w_out, m_ple_norm, m_w_ple_gate, m_w_ple_proj, m_final_norm)
    vs = (v_ffn1_norm, v_ffn1_w_in, v_ffn1_w_out, v_mix_norm, v_w_mix_in, v_conv_w, v_w_conv_out, v_w_attn_out,
          v_w_mix_out, v_ffn2_norm, v_ffn2_w_in, v_ffn2_w_out, v_ple_norm, v_w_ple_gate, v_w_ple_proj, v_final_norm)
    assert x.shape[0] == 1 and p.shape[:2] == (1, 1), "one sequence and one layer per device"

    def strip(a):
        return a[0] if a.ndim == 3 or (a.ndim == 2 and a.shape[0] == 1) else a

    w = {k: strip(a) for k, a in zip(WEIGHTS, ws)}
    m = {k: strip(a) for k, a in zip(WEIGHTS, ms)}
    v = {k: strip(a) for k, a in zip(WEIGHTS, vs)}
    loss, dx, grad, delta, new_m, new_v = _step(x[0], p[0, 0], loss_target[0], w, m, v)
    shapes = [a.shape for a in ws]
    outs = [loss, dx[None]]
    for res in (grad, delta, new_m, new_v):
        outs += [res[k].reshape(s) for k, s in zip(WEIGHTS, shapes)]
    return tuple(outs)
```

```python
import functools
import math

import jax
import jax.numpy as jnp
from jax import lax
from jax.experimental import pallas as pl
from jax.experimental.pallas import tpu as pltpu

F32 = jnp.float32
BF16 = jnp.bfloat16
MESH = pl.DeviceIdType.MESH
ANY = pl.BlockSpec(memory_space=pl.ANY)

HEAD_DIM = 128
NORM_EPS = 1e-6
N_CHIPS = 4
N_DEV = 8
BF16_ROWS = 16
VMEM_LIMIT = 56 * 1024 * 1024
ACC_BYTES = 8 * 1024 * 1024
STICK_EXIT = 110.0

ADAM_LR = 0.001
ADAM_B1 = 0.9
ADAM_B2 = 0.999
ADAM_EPS = 1e-08
ADAM_WD = 0.01
ADAM_STEP = 10

NN = (((1,), (0,)), ((), ()))
NT = (((1,), (1,)), ((), ()))
TN = (((0,), (0,)), ((), ()))


def _params(sem=None, **kw):
    if sem is not None:
        kw["dimension_semantics"] = sem
    return pltpu.CompilerParams(vmem_limit_bytes=VMEM_LIMIT, **kw)


def _pcall(body, **kw):
    return pl.pallas_call(body, **kw)


def _tile(n, pref, mult=8):
    best = None
    for d in range(mult, min(n, pref) + 1, mult):
        if n % d == 0:
            best = d
    return best if best is not None else n


def _dot(a, b, dims):
    return lax.dot_general(a, b, dims, preferred_element_type=F32)


def _call(name, body, grid, in_specs, out_specs, out_shape, args, scratch=(), sem=None, comm=None):
    n_in, n_out, n_sc = len(in_specs), len(out_specs), len(scratch)
    if comm is None:
        def plain(*refs):
            body(refs[:n_in], refs[n_in:n_in + n_out], refs[n_in + n_out:])

        return _pcall(plain, name=name, out_shape=list(out_shape), grid=grid, in_specs=list(in_specs),
                      out_specs=list(out_specs), scratch_shapes=list(scratch), compiler_params=_params(sem))(*args)
    n_cin, n_cout = len(comm.ins), len(comm.outs)
    steps = math.prod(grid)

    def hosted(*refs):
        ins, c_ins = refs[:n_in], refs[n_in:n_in + n_cin]
        outs = refs[n_in + n_cin:n_in + n_cin + n_out]
        c_outs = refs[n_in + n_cin + n_out:n_in + n_cin + n_out + n_cout]
        rest = refs[n_in + n_cin + n_out + n_cout:]
        sems = rest[n_sc:]
        step = pl.program_id(0)
        for ax in range(1, len(grid)):
            step = step * grid[ax] + pl.program_id(ax)

        @pl.when(step == 0)
        def _():
            comm.first(c_ins, c_outs, sems)

        body(ins, outs, rest[:n_sc])

        @pl.when(step == (3 * steps) // 4)
        def _():
            comm.mid(c_ins, c_outs, sems)

        @pl.when(step == steps - 1)
        def _():
            comm.last(c_ins, c_outs, sems)

    res = _pcall(hosted, name=name, out_shape=list(out_shape) + comm.outs, grid=grid,
                 in_specs=list(in_specs) + [ANY] * n_cin, out_specs=list(out_specs) + [ANY] * n_cout,
                 input_output_aliases={n_in + k: n_out + v for k, v in comm.aliases.items()},
                 scratch_shapes=list(scratch) + comm.sems,
                 compiler_params=_params(("arbitrary",) * len(grid)))(*args, *comm.ins)
    return list(res[:n_out]), list(res[n_out:])


NORM_CHUNK = 256


def _norm_bwd_tile(read_dn, rows, first, h_ref, g_ref, dr_ref, dh_ref, dhb_ref, dg_ref, alpha):
    @pl.when(first)
    def _():
        dg_ref[...] = jnp.zeros_like(dg_ref)

    gv = g_ref[...]
    tot = jnp.zeros_like(gv)
    for c0 in range(0, rows, NORM_CHUNK):
        sl = slice(c0, min(rows, c0 + NORM_CHUNK))
        hv = h_ref[sl, :]
        rs = _rstd(hv)
        hn = hv * rs
        dnv = read_dn(sl)
        gy = dnv * gv
        dh = dr_ref[sl, :] + rs * (gy - hn * jnp.mean(gy * hn, axis=-1, keepdims=True))
        dh_ref[sl, :] = dh
        dhb_ref[sl, :] = (alpha * dh).astype(BF16)
        tot = tot + jnp.sum(dnv * hn, axis=0, keepdims=True)
    dg_ref[...] += tot


def _mm(name, a, b, out_sds, grid, a_spec, b_spec, o_spec, dims, acc_shape, res=None, alpha=1.0, comm=None,
        norm=None, gain=None):
    nk = grid[2]

    def body(ins, outs, scratch):
        a_ref, b_ref = ins[:2]
        r_ref = ins[2] if res is not None else None
        o_ref = outs[0]
        if gain is not None:
            n_ref = scratch[-1]

            @pl.when(jnp.logical_and(pl.program_id(1) == 0, pl.program_id(2) == 0))
            def _():
                hv = a_ref[...]
                n_ref[...] = (hv * _rstd(hv) * ins[-1][...]).astype(BF16)
                outs[-1][...] = n_ref[...]

            a_ref = n_ref

        def finish(read):
            if norm is not None:
                first = jnp.logical_and(pl.program_id(0) == 0, pl.program_id(1) == 0)
                _norm_bwd_tile(read, o_ref.shape[0], first, *ins[2:5], *outs, alpha)
                return
            r = read(slice(None))
            if alpha != 1.0:
                r = r * alpha
            if r_ref is not None:
                r = r_ref[...] + r
            if len(o_ref.shape) == 3:
                half = o_ref.shape[1]
                o_ref[0] = r[:half].astype(o_ref.dtype)
                o_ref[1] = r[half:].astype(o_ref.dtype)
            else:
                o_ref[...] = r.astype(o_ref.dtype)

        if nk == 1:
            part = _dot(a_ref[...].astype(BF16), b_ref[...].astype(BF16), dims)
            finish(lambda sl: part[sl])
        else:
            acc_ref = scratch[0]
            kk = pl.program_id(2)

            @pl.when(kk == 0)
            def _():
                acc_ref[...] = jnp.zeros_like(acc_ref)

            acc_ref[...] += _dot(a_ref[...].astype(BF16), b_ref[...].astype(BF16), dims)

            @pl.when(kk == nk - 1)
            def _():
                finish(lambda sl: acc_ref[sl, :])

    in_specs = [a_spec, b_spec]
    args = [a, b]
    out_specs, out_shape = [o_spec], [out_sds]
    sem = ("parallel", "parallel", "arbitrary")
    if res is not None:
        in_specs.append(o_spec)
        args.append(res)
    if norm is not None:
        width = out_sds.shape[1]
        whole = pl.BlockSpec((1, width), lambda i, j, r: (0, 0))
        in_specs += [o_spec, whole, o_spec]
        args += list(norm)
        out_specs = [o_spec, o_spec, whole]
        out_shape = [jax.ShapeDtypeStruct(out_sds.shape, F32), jax.ShapeDtypeStruct(out_sds.shape, BF16),
                     jax.ShapeDtypeStruct((1, width), F32)]
        sem = ("arbitrary", "arbitrary", "arbitrary")
    scratch = [] if nk == 1 else [pltpu.VMEM(acc_shape, F32)]
    if gain is not None:
        in_specs.append(pl.BlockSpec((1, a.shape[1]), lambda i, j, r: (0, 0)))
        args.append(gain)
        out_specs.append(a_spec)
        out_shape.append(jax.ShapeDtypeStruct(a.shape, BF16))
        scratch.append(pltpu.VMEM(a_spec.block_shape, BF16))
        sem = ("parallel", "arbitrary", "arbitrary")
    got = _call(name, body, grid, in_specs, out_specs, out_shape, args, scratch, sem, comm)
    if norm is not None or gain is not None:
        return got if comm is None else (got[0], got[1])
    return got[0] if comm is None else (got[0][0], got[1])


def ffn_in_act(name, n, w4, tm, comm=None, gain=None):
    t, d = n.shape
    cs = w4.shape[2]

    def body(ins, outs, scratch):
        wg_ref, wu_ref = ins[-2:]
        a_ref, s_ref = outs[:2]
        if gain is None:
            nv = ins[0][...]
        else:
            hv = ins[0][...]
            nv = (hv * _rstd(hv) * ins[1][...]).astype(BF16)

            @pl.when(pl.program_id(0) == 0)
            def _():
                outs[2][...] = nv
        gate = _dot(nv, wg_ref[...], NN)
        up = _dot(nv, wu_ref[...], NN)
        a_ref[0] = gate.astype(BF16)
        a_ref[1] = up.astype(BF16)
        s_ref[...] = (gate * jax.nn.sigmoid(gate) * up).astype(BF16)

    rows = pl.BlockSpec((tm, d), lambda j, i: (i, 0))
    in_specs = [rows] + ([] if gain is None else [pl.BlockSpec((1, d), lambda j, i: (0, 0))])
    in_specs += [pl.BlockSpec((None, d, cs), lambda j, i: (j, 0, 0)),
                 pl.BlockSpec((None, d, cs), lambda j, i: (2 + j, 0, 0))]
    out_specs = [pl.BlockSpec((2, tm, cs), lambda j, i: (0, i, j)), pl.BlockSpec((tm, cs), lambda j, i: (i, j))]
    out_shape = [jax.ShapeDtypeStruct((2, t, 2 * cs), BF16), jax.ShapeDtypeStruct((t, 2 * cs), BF16)]
    if gain is not None:
        out_specs.append(pl.BlockSpec((tm, d), lambda j, i: (jnp.where(j == 0, i, t // tm - 1), 0)))
        out_shape.append(jax.ShapeDtypeStruct((t, d), BF16))
    got = _call(name, body, (2, t // tm), in_specs, out_specs, out_shape,
                [n] + ([] if gain is None else [gain]) + [w4, w4], (), ("arbitrary", "arbitrary"), comm)
    return got if comm is None else (got[0], got[1])


def ffn_ds_dact(name, df, w_out, a3, tm):
    t, d = df.shape
    f = w_out.shape[0]
    cs = f // 2

    def body(ins, outs, scratch):
        df_ref, w_ref, a_ref = ins
        ds = _dot(df_ref[...], w_ref[...], NT)
        for c0 in range(0, tm, NORM_CHUNK):
            sl = slice(c0, min(tm, c0 + NORM_CHUNK))
            gate = a_ref[0, sl, :].astype(F32)
            up = a_ref[1, sl, :].astype(F32)
            sg = jax.nn.sigmoid(gate)
            outs[0][0, sl, :] = (ds[sl] * up * sg * (1.0 + gate * (1.0 - sg))).astype(BF16)
            outs[0][1, sl, :] = (ds[sl] * gate * sg).astype(BF16)

    blk = pl.BlockSpec((2, tm, cs), lambda i, j: (0, i, j))
    return _call(name, body, (t // tm, 2),
                 [pl.BlockSpec((tm, d), lambda i, j: (i, 0)), pl.BlockSpec((cs, d), lambda i, j: (j, 0)), blk],
                 [blk], [jax.ShapeDtypeStruct((2, t, f), BF16)], [df, w_out, a3], (), ("parallel", "parallel"))[0]


def _part_ranges(parts, d):
    out, lo = [], 0
    for p in parts:
        out.append((lo, p.shape[1] // d))
        lo += p.shape[1] // d
    return out, lo


def mm_nt_parts(name, parts, w4, tm, norm, alpha, comm=None):
    m = parts[0].shape[0]
    d, cs = w4.shape[1], w4.shape[2]
    per = cs // d
    ranges, nblk = _part_ranges(parts, d)
    np_ = len(parts)
    nt = m // tm
    chunk = tm // nblk

    def body(ins, outs, scratch):
        w_ref, acc = ins[np_], scratch[0]
        i, r = pl.program_id(0), pl.program_id(1)

        @pl.when(jnp.logical_and(i < nt, r == 0))
        def _():
            acc[i % 2] = jnp.zeros(acc.shape[1:], F32)

        for (lo, n), a_ref in zip(ranges, ins[:np_]):
            @pl.when(jnp.logical_and(i < nt, jnp.logical_and(r >= lo, r < lo + n)))
            def _(a_ref=a_ref):
                acc[i % 2] += _dot(a_ref[...], w_ref[...], NT)

        @pl.when(i > 0)
        def _():
            rows = pl.ds(pl.multiple_of(r * chunk, chunk), chunk)
            first = jnp.logical_and(i == 1, r == 0)
            _norm_bwd_tile(lambda sl: acc[(i - 1) % 2, rows, :][sl], chunk, first, *ins[np_ + 1:], *outs, alpha)

    def ahead(i, r):
        return jnp.where(i < nt, r, nblk - 1)

    rows = pl.BlockSpec((chunk, d), lambda i, r: (jnp.where(i == 0, 0, (i - 1) * nblk + r), 0))
    whole = pl.BlockSpec((1, d), lambda i, r: (0, 0))
    specs = [pl.BlockSpec((tm, d), lambda i, r, lo=lo, n=n: (jnp.minimum(i, nt - 1), jnp.clip(ahead(i, r) - lo, 0, n - 1)))
             for lo, n in ranges]
    specs += [pl.BlockSpec((None, d, d), lambda i, r: (ahead(i, r) // per, 0, ahead(i, r) % per)), rows, whole, rows]
    got = _call(name, body, (nt + 1, nblk), specs, [rows, rows, whole],
                [jax.ShapeDtypeStruct((m, d), F32), jax.ShapeDtypeStruct((m, d), BF16),
                 jax.ShapeDtypeStruct((1, d), F32)],
                list(parts) + [w4] + list(norm), [pltpu.VMEM((2, tm, d), F32)], ("arbitrary", "arbitrary"), comm)
    return got if comm is None else (got[0], got[1])


def mm_tn_parts(name, xa, parts, tt, comm=None):
    t, k = xa.shape
    d = k
    pr = k // 2
    ranges, nblk = _part_ranges(parts, d)
    per = nblk // N_CHIPS

    def body(ins, outs, scratch):
        x_ref, acc = ins[0], scratch[0]
        jb, r = pl.program_id(0), pl.program_id(1)

        @pl.when(r == 0)
        def _():
            acc[...] = jnp.zeros_like(acc)

        for (lo, n), p_ref in zip(ranges, ins[1:]):
            @pl.when(jnp.logical_and(jb >= lo, jb < lo + n))
            def _(p_ref=p_ref):
                acc[...] += _dot(x_ref[...], p_ref[...], TN)

        @pl.when(r == t // tt - 1)
        def _():
            outs[0][0] = acc[:pr].astype(BF16)
            outs[0][1] = acc[pr:].astype(BF16)

    def part_spec(lo, n):
        return pl.BlockSpec((tt, d), lambda jb, r: (jnp.where(jnp.logical_and(jb >= lo, jb < lo + n), r, 0),
                                                    jnp.clip(jb - lo, 0, n - 1)))

    specs = [pl.BlockSpec((tt, k), lambda jb, r: (r, 0))] + [part_spec(lo, n) for lo, n in ranges]
    got = _call(name, body, (nblk, t // tt), specs,
                [pl.BlockSpec((None, 2, pr, d), lambda jb, r: (jb // per, 0, 0, jb % per))],
                [jax.ShapeDtypeStruct((N_CHIPS, 2, pr, per * d), BF16)], [xa] + list(parts),
                [pltpu.VMEM((k, d), F32)], ("parallel", "arbitrary"), comm)
    return got[0] if comm is None else (got[0][0], got[1])


def mm_nn(name, a, w, out_dtype, tm, res=None, alpha=1.0):
    m, k = a.shape
    n = w.shape[1]
    return _mm(name, a, w, jax.ShapeDtypeStruct((m, n), out_dtype), (m // tm, 1, 1),
               pl.BlockSpec((tm, k), lambda i, j, r: (i, 0)),
               pl.BlockSpec((k, n), lambda i, j, r: (0, 0)),
               pl.BlockSpec((tm, n), lambda i, j, r: (i, 0)), NN, None, res=res, alpha=alpha)


def mm_nn_stacked(name, a, w4, out_dtype, tm, tn, j0=0, nj=None, comm=None, gain=None):
    m, k = a.shape
    cs = w4.shape[2]
    per = cs // tn
    nj = N_CHIPS * per - j0 if nj is None else nj
    return _mm(name, a, w4, jax.ShapeDtypeStruct((m, nj * tn), out_dtype), (m // tm, nj, 1),
               pl.BlockSpec((tm, k), lambda i, j, r: (i, 0)),
               pl.BlockSpec((None, k, tn), lambda i, j, r: ((j + j0) // per, 0, (j + j0) % per)),
               pl.BlockSpec((tm, tn), lambda i, j, r: (i, j)), NN, None, comm=comm, gain=gain)


def mm_nt(name, dy, w, out_dtype, tm, tko, norm=None, alpha=1.0):
    m, n = dy.shape
    k = w.shape[0]
    return _mm(name, dy, w, jax.ShapeDtypeStruct((m, k), out_dtype), (m // tm, k // tko, 1),
               pl.BlockSpec((tm, n), lambda i, j, r: (i, 0)),
               pl.BlockSpec((tko, n), lambda i, j, r: (j, 0)),
               pl.BlockSpec((tm, tko), lambda i, j, r: (i, j)), NT, None, norm=norm, alpha=alpha)


def mm_nt_stacked(name, dy, w4, tm, norm, alpha=1.0, comm=None):
    m = dy.shape[1]
    k, cs = w4.shape[1], w4.shape[2]
    nt = m // tm
    chunk = tm // N_CHIPS

    def body(ins, outs, scratch):
        dy_ref, w_ref, h_ref, g_ref, dr_ref = ins
        dh_ref, dhb_ref, dg_ref = outs
        acc = scratch[0]
        i, r = pl.program_id(0), pl.program_id(1)

        @pl.when(jnp.logical_and(i < nt, r == 0))
        def _():
            acc[i % 2] = jnp.zeros(acc.shape[1:], F32)

        @pl.when(i < nt)
        def _():
            acc[i % 2] += _dot(dy_ref[...], w_ref[...], NT)

        @pl.when(i > 0)
        def _():
            rows = pl.ds(pl.multiple_of(r * chunk, chunk), chunk)
            first = jnp.logical_and(i == 1, r == 0)
            _norm_bwd_tile(lambda sl: acc[(i - 1) % 2, rows, :][sl], chunk, first, h_ref, g_ref, dr_ref,
                           dh_ref, dhb_ref, dg_ref, alpha)

    def behind(i, r):
        return (jnp.where(i == 0, 0, (i - 1) * N_CHIPS + r), 0)

    def ahead(i, r):
        return jnp.where(i < nt, r, N_CHIPS - 1)

    rows = pl.BlockSpec((chunk, k), behind)
    whole = pl.BlockSpec((1, k), lambda i, r: (0, 0))
    got = _call(name, body, (nt + 1, N_CHIPS),
                [pl.BlockSpec((None, tm, cs), lambda i, r: (ahead(i, r) // 2, jnp.minimum(i, nt - 1), ahead(i, r) % 2)),
                 pl.BlockSpec((None, k, cs), lambda i, r: (ahead(i, r), 0, 0)), rows, whole, rows],
                [rows, rows, whole],
                [jax.ShapeDtypeStruct((m, k), F32), jax.ShapeDtypeStruct((m, k), BF16),
                 jax.ShapeDtypeStruct((1, k), F32)],
                [dy, w4] + list(norm), [pltpu.VMEM((2, tm, k), F32)], ("arbitrary", "arbitrary"), comm)
    return got if comm is None else (got[0], got[1])


def mm_tn_rows(name, xa, dy, tt):
    t, k = xa.shape
    n = dy.shape[1]
    tkr = k if k * n * 4 <= ACC_BYTES else k // 2
    return _mm(name, xa, dy, jax.ShapeDtypeStruct((k, n), BF16), (k // tkr, 1, t // tt),
               pl.BlockSpec((tt, tkr), lambda i, j, r: (r, i)),
               pl.BlockSpec((tt, n), lambda i, j, r: (r, 0)),
               pl.BlockSpec((tkr, n), lambda i, j, r: (i, 0)), TN, (tkr, n))


def mm_tn_whole(name, xa, dy, tt):
    t, k = xa.shape
    n = dy.shape[1]
    return _mm(name, xa, dy, jax.ShapeDtypeStruct((k, n), BF16), (1, 1, t // tt),
               pl.BlockSpec((tt, k), lambda i, j, r: (r, 0)),
               pl.BlockSpec((tt, n), lambda i, j, r: (r, 0)),
               pl.BlockSpec((k, n), lambda i, j, r: (0, 0)), TN, (k, n))


def mm_tn_cols(name, xa, dy, tt, comm=None):
    t, k = xa.shape
    pr = k // 2
    if dy.ndim == 3:
        cs = dy.shape[2] // 2
        dy_spec = pl.BlockSpec((None, tt, cs), lambda i, j, r: (j // 2, r, j % 2))
    else:
        cs = dy.shape[1] // N_CHIPS
        dy_spec = pl.BlockSpec((tt, cs), lambda i, j, r: (r, j))
    return _mm(name, xa, dy, jax.ShapeDtypeStruct((N_CHIPS, 2, pr, cs), BF16), (1, N_CHIPS, t // tt),
               pl.BlockSpec((tt, k), lambda i, j, r: (r, 0)), dy_spec,
               pl.BlockSpec((None, 2, pr, cs), lambda i, j, r: (j, 0, 0, 0)), TN, (k, cs), comm=comm)


def _rows(tt, w, col=0):
    return pl.BlockSpec((tt, w), lambda i: (i, col))


def _whole(shape):
    return pl.BlockSpec(shape, lambda i: (0,) * len(shape))


def _rstd(h):
    return lax.rsqrt(jnp.mean(h * h, axis=-1, keepdims=True) + NORM_EPS)


def rms_fwd(name, h, g, tt, comm=None):
    t, d = h.shape

    def body(ins, outs, scratch):
        hv = ins[0][...]
        outs[0][...] = (hv * _rstd(hv) * ins[1][...]).astype(BF16)

    got = _call(name, body, (t // tt,), [_rows(tt, d), _whole((1, d))], [_rows(tt, d)],
                [jax.ShapeDtypeStruct((t, d), BF16)], [h, g], (), ("parallel",), comm)
    return got[0] if comm is None else (got[0][0], got[1])


def mix_out_fwd(name, gates, yc, ya, h, w, tt, gcol=0):
    t, d = yc.shape

    def body(g_ref, yc_ref, ya_ref, h_ref, w_ref, m_ref, o_ref):
        for c0 in range(0, tt, NORM_CHUNK):
            sl = slice(c0, min(tt, c0 + NORM_CHUNK))
            m_ref[sl, :] = (jax.nn.sigmoid(g_ref[sl, :d].astype(F32)) * yc_ref[sl, :].astype(F32)
                            + jax.nn.sigmoid(g_ref[sl, d:].astype(F32)) * ya_ref[sl, :].astype(F32)).astype(BF16)
        o_ref[...] = h_ref[...] + _dot(m_ref[...], w_ref[...], NN)

    return _pcall(body, name=name,
                  out_shape=(jax.ShapeDtypeStruct((t, d), BF16), jax.ShapeDtypeStruct((t, d), F32)),
                  grid=(t // tt,),
                  in_specs=[_rows(tt, 2 * d, gcol), _rows(tt, d), _rows(tt, d), _rows(tt, d), _whole((d, d))],
                  out_specs=(_rows(tt, d), _rows(tt, d)),
                  compiler_params=_params(("parallel",)))(gates, yc, ya, h, w)


def mix_out_bwd(name, dh, w, gates, yc, ya, tt, gcol=0):
    t, d = yc.shape

    def body(dh_ref, w_ref, g_ref, yc_ref, ya_ref, dyc_ref, dya_ref, dg_ref):
        dm = _dot(dh_ref[...], w_ref[...], NT)
        for c0 in range(0, tt, NORM_CHUNK):
            sl = slice(c0, min(tt, c0 + NORM_CHUNK))
            dmv = dm[sl]
            sc = jax.nn.sigmoid(g_ref[sl, :d].astype(F32))
            sa = jax.nn.sigmoid(g_ref[sl, d:].astype(F32))
            dyc_ref[sl, :] = (dmv * sc).astype(BF16)
            dya_ref[sl, :] = (dmv * sa).astype(BF16)
            dg_ref[sl, :d] = (dmv * yc_ref[sl, :].astype(F32) * sc * (1.0 - sc)).astype(BF16)
            dg_ref[sl, d:] = (dmv * ya_ref[sl, :].astype(F32) * sa * (1.0 - sa)).astype(BF16)

    return _pcall(body, name=name,
                  out_shape=(jax.ShapeDtypeStruct((t, d), BF16), jax.ShapeDtypeStruct((t, d), BF16),
                             jax.ShapeDtypeStruct((t, 2 * d), BF16)),
                  grid=(t // tt,),
                  in_specs=[_rows(tt, d), _whole((d, d)), _rows(tt, 2 * d, gcol), _rows(tt, d), _rows(tt, d)],
                  out_specs=(_rows(tt, d), _rows(tt, d), _rows(tt, 2 * d)),
                  compiler_params=_params(("parallel",)))(dh, w, gates, yc, ya)


def _shift_down(cur, prev8, s):
    tt = cur.shape[0]
    rolled = pltpu.roll(cur, s, 0)
    row8 = lax.broadcasted_iota(jnp.int32, prev8.shape, 0)
    first8 = jnp.where(row8 < s, pltpu.roll(prev8, s, 0), rolled[:8])
    return jnp.concatenate([first8, rolled[8:]], axis=0) if tt > 8 else first8


def _shift_up(cur, next8, s):
    tt = cur.shape[0]
    rolled = pltpu.roll(cur, tt - s, 0)
    row8 = lax.broadcasted_iota(jnp.int32, next8.shape, 0)
    last8 = jnp.where(row8 >= 8 - s, pltpu.roll(next8, 8 - s, 0), rolled[tt - 8:])
    return jnp.concatenate([rolled[:tt - 8], last8], axis=0) if tt > 8 else last8


def _prev_rows(tt, d, col):
    return pl.BlockSpec((BF16_ROWS, d), lambda i: (jnp.maximum(i * (tt // BF16_ROWS) - 1, 0), col))


def _next_rows(tt, d, col, t):
    return pl.BlockSpec((BF16_ROWS, d),
                        lambda i: (jnp.minimum((i + 1) * (tt // BF16_ROWS), t // BF16_ROWS - 1), col))


def conv_out_fwd(name, cbx, cw8, w_out, tt):
    t = cbx.shape[0]
    d = w_out.shape[0]
    d3 = 3 * d

    def body(cb_ref, cc_ref, cx_ref, pc_ref, px_ref, w_ref, wo_ref, o_ref, y_ref):
        has_prev = (pl.program_id(0) > 0).astype(F32)
        cc = cc_ref[...].astype(F32) * cx_ref[...].astype(F32)
        prev = pc_ref[...].astype(F32)[8:] * px_ref[...].astype(F32)[8:] * has_prev
        w = w_ref[...]
        conv = w[0:1] * _shift_down(cc, prev, 2) + w[1:2] * _shift_down(cc, prev, 1) + w[2:3] * cc
        ycin = (cb_ref[...].astype(F32) * conv).astype(BF16)
        o_ref[...] = ycin
        y_ref[...] = _dot(ycin, wo_ref[...], NN).astype(BF16)

    out = jax.ShapeDtypeStruct((t, d), BF16)
    return _pcall(body, name=name, out_shape=(out, out), grid=(t // tt,),
                  in_specs=[_rows(tt, d, 0), _rows(tt, d, 1), _rows(tt, d, 2), _prev_rows(tt, d, 1),
                            _prev_rows(tt, d, 2), _whole((8, d)), _whole((d, d))],
                  out_specs=(_rows(tt, d), _rows(tt, d)),
                  compiler_params=_params(("parallel",)))(cbx, cbx, cbx, cbx, cbx, cw8, w_out)


def conv_out_bwd(name, dyc, w_out, cbx, cw8, tt):
    t = cbx.shape[0]
    d = w_out.shape[0]
    d3 = 3 * d
    n = t // tt

    def body(dy_ref, ndy_ref, wo_ref, cb_ref, cc_ref, cx_ref, pc_ref, px_ref, ncb_ref, w_ref, o_ref, dw_ref):
        i = pl.program_id(0)
        has_prev = (i > 0).astype(F32)
        has_next = (i < n - 1).astype(F32)
        cb = cb_ref[...].astype(F32)
        ccv = cc_ref[...].astype(F32)
        cxv = cx_ref[...].astype(F32)
        cc = ccv * cxv
        prev = pc_ref[...].astype(F32)[8:] * px_ref[...].astype(F32)[8:] * has_prev
        w = w_ref[...]
        cc1 = _shift_down(cc, prev, 1)
        cc2 = _shift_down(cc, prev, 2)
        conv = w[0:1] * cc2 + w[1:2] * cc1 + w[2:3] * cc
        dyv = _dot(dy_ref[...], wo_ref[...], NT)
        dconv = dyv * cb
        dnext = _dot(ndy_ref[...], wo_ref[...], NT)[:8] * ncb_ref[...].astype(F32)[:8] * has_next
        dcc = w[2:3] * dconv + w[1:2] * _shift_up(dconv, dnext, 1) + w[0:1] * _shift_up(dconv, dnext, 2)
        o_ref[:, :d] = (dyv * conv).astype(BF16)
        o_ref[:, d:2 * d] = (dcc * cxv).astype(BF16)
        o_ref[:, 2 * d:] = (dcc * ccv).astype(BF16)

        @pl.when(i == 0)
        def _():
            dw_ref[...] = jnp.zeros_like(dw_ref)

        dw_ref[0:1, :] += jnp.sum(dconv * cc2, axis=0, keepdims=True)
        dw_ref[1:2, :] += jnp.sum(dconv * cc1, axis=0, keepdims=True)
        dw_ref[2:3, :] += jnp.sum(dconv * cc, axis=0, keepdims=True)

    return _pcall(body, name=name,
                  out_shape=(jax.ShapeDtypeStruct((t, d3), BF16), jax.ShapeDtypeStruct((8, d), F32)),
                  grid=(n,),
                  in_specs=[_rows(tt, d), _next_rows(tt, d, 0, t),
                            _whole((d, d)), _rows(tt, d, 0), _rows(tt, d, 1), _rows(tt, d, 2),
                            _prev_rows(tt, d, 1), _prev_rows(tt, d, 2), _next_rows(tt, d, 0, t), _whole((8, d))],
                  out_specs=(_rows(tt, d3), _whole((8, d))),
                  compiler_params=_params(("arbitrary",)))(dyc, dyc, w_out, cbx, cbx, cbx, cbx, cbx, cbx, cw8)


def tail(name, h3, p, tgt, gp, gf, w_gate, w_proj, tt):
    t, d = h3.shape
    pd = p.shape[1]

    def body(h_ref, p_ref, tg_ref, gp_ref, gf_ref, wg_ref, wp_ref, np_ref, dh_ref, dpp_ref, dzg_ref, dgf_ref,
             loss_ref):
        hv = h_ref[...]
        npl = (hv * _rstd(hv) * gp_ref[...]).astype(BF16)
        np_ref[...] = npl
        pg = jax.nn.sigmoid(_dot(npl, wg_ref[...], NN))
        ppv = _dot(p_ref[...].astype(BF16), wp_ref[...], NN)
        h4 = hv + pg * ppv
        r4 = _rstd(h4)
        hn = h4 * r4
        gfv = gf_ref[...]
        err = hn * gfv - tg_ref[...]
        dy = err * (1.0 / d)
        gy = dy * gfv
        dh4 = r4 * (gy - hn * jnp.mean(gy * hn, axis=-1, keepdims=True))
        dh_ref[...] = dh4
        dpp_ref[...] = (dh4 * pg).astype(BF16)
        dzg_ref[...] = (dh4 * ppv * pg * (1.0 - pg)).astype(BF16)

        @pl.when(pl.program_id(0) == 0)
        def _():
            dgf_ref[...] = jnp.zeros_like(dgf_ref)
            loss_ref[...] = jnp.zeros_like(loss_ref)

        dgf_ref[...] += jnp.sum(dy * hn, axis=0, keepdims=True)
        tok = jnp.mean(err * err, axis=-1, keepdims=True)
        loss_ref[...] += 0.5 * jnp.sum(tok, axis=0, keepdims=True) * jnp.ones((1, loss_ref.shape[1]), F32)

    return _pcall(body, name=name,
                  out_shape=(jax.ShapeDtypeStruct((t, d), BF16), jax.ShapeDtypeStruct((t, d), F32),
                             jax.ShapeDtypeStruct((t, d), BF16), jax.ShapeDtypeStruct((t, d), BF16),
                             jax.ShapeDtypeStruct((1, d), F32), jax.ShapeDtypeStruct((1, d), F32)),
                  grid=(t // tt,),
                  in_specs=[_rows(tt, d), _rows(tt, pd), _rows(tt, d), _whole((1, d)), _whole((1, d)),
                            _whole((d, d)), _whole((pd, d))],
                  out_specs=(_rows(tt, d), _rows(tt, d), _rows(tt, d), _rows(tt, d), _whole((1, d)),
                             _whole((1, d))),
                  compiler_params=_params(("arbitrary",)))(h3, p, tgt, gp, gf, w_gate, w_proj)


SCALE = 1.0 / math.sqrt(HEAD_DIM)


def _log_stick(z):
    return -(jnp.maximum(z, 0.0) + jnp.log(1.0 + jnp.exp(-jnp.abs(z))))


def _tri_sum(x, tri):
    hi = x.astype(BF16)
    lo = (x - hi.astype(F32)).astype(BF16)
    return _dot(hi, tri, NN) + _dot(lo, tri, NN)


KEY_BLOCK = 128
NEAR = 3
THIN_ROWS = 32


def _pad_block(x):
    n = x.shape[0]
    return x if n == KEY_BLOCK else jnp.concatenate([x, jnp.zeros((KEY_BLOCK - n, x.shape[1]), x.dtype)], axis=0)


def _sb_near(qs, jds, k_ref, below, upper, last_rows):
    near_rows = (KEY_BLOCK,) * (NEAR - 1) + (last_rows,)
    pairs = [(s, b) for s in range(len(qs)) for b in range(NEAR)]
    rows = {(s, b): _block_rows(jnp.maximum(jds[s] - b, 0), KEY_BLOCK) for s, b in pairs}
    z = {(s, b): _dot(qs[s][:near_rows[b]], k_ref[rows[s, b], :], NT) * SCALE for s, b in pairs}
    lg = {(s, b): jnp.where(below, _log_stick(z[s, b]), 0.0) if b == 0 else _log_stick(z[s, b]) for s, b in pairs}
    cum = {(s, b): _tri_sum(lg[s, b], upper) for s, b in pairs}
    out, carries = [], []
    for s in range(len(qs)):
        c = cum[s, 0][:, 0:1]
        blocks = [(rows[s, 0], z[s, 0], jnp.exp(jnp.where(below, z[s, 0] + cum[s, 0], -1e30)))]
        for b in range(1, NEAR):
            live = jds[s] >= b
            off = c[:near_rows[b]] + jnp.where(live, 0.0, -1e30)
            blocks.append((rows[s, b], z[s, b], jnp.exp(z[s, b] + cum[s, b] + off)))
            c = c + _pad_block(jnp.where(live, cum[s, b][:, 0:1], 0.0))
        out.append(blocks)
        carries.append(c)
    return out, carries


def _sb_far(q, kj, upper, c, skip):
    z = _dot(q, kj, NT) * SCALE
    cum = _tri_sum(_log_stick(z), upper)
    return z, jnp.exp(z + cum + (c + jnp.where(skip, -1e30, 0.0))), c + jnp.where(skip, 0.0, cum[:, 0:1])


def _took_it(j, jd, last_rows):
    first = lax.broadcasted_iota(jnp.int32, (KEY_BLOCK, 1), 0) < last_rows
    return jnp.logical_and(j == jd - (NEAR - 1), first)


def _block_rows(j, size):
    return pl.ds(pl.multiple_of(j * size, size), size)


def _sweep_on(st):
    return jnp.logical_and(st[0] >= 0, jnp.max(st[1]) > -STICK_EXIT)


def attn_fwd(name, qkv, tq, d, col0=0):
    t = qkv.shape[0]
    nh = d // HEAD_DIM
    q0 = col0 // HEAD_DIM
    nq = t // tq
    tb = KEY_BLOCK
    nsub = tq // tb

    def body(q_ref, k_ref, v_ref, o_ref):
        i = pl.program_id(1)
        row = lax.broadcasted_iota(jnp.int32, (tb, tb), 0)
        col = lax.broadcasted_iota(jnp.int32, (tb, tb), 1)
        upper = (row >= col).astype(BF16)
        qs = [q_ref[s * tb:(s + 1) * tb, :] for s in range(nsub)]
        jds = [i * nsub + s for s in range(nsub)]
        near, carries = _sb_near(qs, jds, k_ref, col < row, upper, THIN_ROWS)
        state = []
        for s in range(nsub):
            acc = jnp.zeros((tb, HEAD_DIM), F32)
            for rows, _, a in near[s]:
                acc = acc + _pad_block(_dot(a.astype(BF16), v_ref[rows, :], NN))
            state.append((qs[s], jds[s], carries[s], acc))
        for s, (q, jd, c, acc) in enumerate(state):

            def step(st, q=q, jd=jd):
                rows = _block_rows(st[0], tb)
                _, a, c2 = _sb_far(q, k_ref[rows, :], upper, st[1], _took_it(st[0], jd, THIN_ROWS))
                return st[0] - 1, c2, st[2] + _dot(a.astype(BF16), v_ref[rows, :], NN)

            _, _, acc = lax.while_loop(_sweep_on, step, (jd - (NEAR - 1), c, acc))
            o_ref[s * tb:(s + 1) * tb, :] = acc.astype(o_ref.dtype)

    return _pcall(body, name=name, out_shape=jax.ShapeDtypeStruct((t, d), BF16), grid=(nh, nq),
                  in_specs=[pl.BlockSpec((tq, HEAD_DIM), lambda h, i: (i, q0 + h)),
                            pl.BlockSpec((t, HEAD_DIM), lambda h, i: (0, q0 + nh + h)),
                            pl.BlockSpec((t, HEAD_DIM), lambda h, i: (0, q0 + 2 * nh + h))],
                  out_specs=pl.BlockSpec((tq, HEAD_DIM), lambda h, i: (i, h)),
                  compiler_params=_params(("parallel", "arbitrary")))(qkv, qkv, qkv)


def attn_bwd(name, qkv, do, tq, col0=0):
    d = do.shape[1]
    t = qkv.shape[0]
    nh = d // HEAD_DIM
    q0 = col0 // HEAD_DIM
    nq = t // tq
    tb = KEY_BLOCK
    nsub = tq // tb

    def body(q_ref, k_ref, v_ref, do_ref, dq_ref, dk_ref, dv_ref, dk_acc, dv_acc, g_buf, z_buf):
        i = pl.program_id(1)

        @pl.when(i == 0)
        def _():
            dk_acc[...] = jnp.zeros_like(dk_acc)
            dv_acc[...] = jnp.zeros_like(dv_acc)

        row = lax.broadcasted_iota(jnp.int32, (tb, tb), 0)
        col = lax.broadcasted_iota(jnp.int32, (tb, tb), 1)
        below = col < row
        upper = (row >= col).astype(BF16)
        lower = (row <= col).astype(BF16)

        qs = [q_ref[s * tb:(s + 1) * tb, :] for s in range(nsub)]
        dos = [do_ref[s * tb:(s + 1) * tb, :] for s in range(nsub)]
        jds = [i * nsub + s for s in range(nsub)]
        near, carries = _sb_near(qs, jds, k_ref, below, upper, KEY_BLOCK)
        da = [[_dot(dos[s][:a.shape[0]], v_ref[rows, :], NT) for rows, _, a in near[s]] for s in range(nsub)]
        state = []
        for s in range(nsub):
            kept = [(rows, z, da[s][b] * a) for b, (rows, z, a) in enumerate(near[s])]
            for rows, _, a in near[s]:
                dv_acc[rows, :] += _dot(a.astype(BF16), dos[s][:a.shape[0]], TN)
            state.append((qs[s], dos[s], jds[s], carries[s], kept))

        carried = []
        for s, (q, dov, jd, c, kept) in enumerate(state):
            def step(st, s=s, q=q, dov=dov, jd=jd):
                j = st[0]
                rows = _block_rows(j, tb)
                z, a, c2 = _sb_far(q, k_ref[rows, :], upper, st[1], _took_it(j, jd, KEY_BLOCK))
                g_buf[jd - j] = _dot(dov, v_ref[rows, :], NT) * a
                z_buf[jd - j] = z
                dv_acc[rows, :] += _dot(a.astype(BF16), dov, TN)
                return j - 1, c2

            j_stop, _ = lax.while_loop(_sweep_on, step, (jd - (NEAR - 1), c))

            def far(j, st, s=s, q=q, jd=jd):
                run, dq = st
                rows = _block_rows(j, tb)
                g = g_buf[jd - j]
                dz = (g - jax.nn.sigmoid(z_buf[jd - j]) * (run + _tri_sum(g, lower))).astype(BF16)
                dk_acc[rows, :] += _dot(dz, q, TN)
                return run + jnp.sum(g, axis=1, keepdims=True), dq + _dot(dz, k_ref[rows, :], NN)

            carried.append(lax.fori_loop(j_stop + 1, jd - (NEAR - 1) + 1, far,
                                         (jnp.zeros((tb, 1), F32), jnp.zeros((tb, HEAD_DIM), F32))))

        tri = [[_dot(g.astype(BF16), lower, NN) for _, _, g in st[4]] for st in state]
        sig = [[jax.nn.sigmoid(z) for _, z, _ in st[4]] for st in state]
        for s, (q, dov, jd, c, kept) in enumerate(state):
            run, dq = carried[s]
            for b in reversed(range(NEAR)):
                rows, z, g = kept[b]
                n = g.shape[0]
                dz = g - sig[s][b] * (run[:n] + tri[s][b])
                if b == 0:
                    dz = jnp.where(below, dz, 0.0)
                dz = dz.astype(BF16)
                dk_acc[rows, :] += _dot(dz, q[:n], TN)
                dq = dq + _pad_block(_dot(dz, k_ref[rows, :], NN))
                if b:
                    run = run + _pad_block(jnp.sum(g, axis=1, keepdims=True))
            dq_ref[s * tb:(s + 1) * tb, :] = (dq * SCALE).astype(BF16)

        @pl.when(i == nq - 1)
        def _():
            dk_ref[...] = (dk_acc[...] * SCALE).astype(BF16)
            dv_ref[...] = dv_acc[...].astype(BF16)

    blk = pl.BlockSpec((tq, HEAD_DIM), lambda h, i: (i, h))
    col_h = pl.BlockSpec((t, HEAD_DIM), lambda h, i: (0, h))
    out = jax.ShapeDtypeStruct((t, d), BF16)
    return _pcall(body, name=name, out_shape=(out, out, out), grid=(nh, nq),
                  in_specs=[pl.BlockSpec((tq, HEAD_DIM), lambda h, i: (i, q0 + h)),
                            pl.BlockSpec((t, HEAD_DIM), lambda h, i: (0, q0 + nh + h)),
                            pl.BlockSpec((t, HEAD_DIM), lambda h, i: (0, q0 + 2 * nh + h)),
                            blk],
                  out_specs=(blk, col_h, col_h),
                  scratch_shapes=[pltpu.VMEM((t, HEAD_DIM), F32), pltpu.VMEM((t, HEAD_DIM), F32),
                                  pltpu.VMEM((t // tb, tb, tb), F32), pltpu.VMEM((t // tb, tb, tb), F32)],
                  compiler_params=_params(("parallel", "arbitrary")))(qkv, qkv, qkv, do)


def _place():
    x, y, c = lax.axis_index("x"), lax.axis_index("y"), lax.axis_index("c")
    chips = [(1 - x, y), (x, 1 - y), (1 - x, 1 - y)]
    return x, y, c, chips


def _remote(src, dst, send_sem, recv_sem, dev):
    return pltpu.make_async_remote_copy(src_ref=src, dst_ref=dst, send_sem=send_sem, recv_sem=recv_sem,
                                        device_id=dev, device_id_type=MESH)


def place_shards(name, ws, chip):
    tiles, steps = _job_tiles([w.shape for w in ws], 1 << 20, BF16_ROWS)
    nj = len(ws)

    def body(chip_ref, *refs):
        i = pl.program_id(0)
        for k, (_, n) in enumerate(tiles):
            @pl.when(i < n)
            def _(w_ref=refs[k], o_ref=refs[nj + k]):
                o_ref[...] = w_ref[...].astype(BF16)

    spec = pltpu.PrefetchScalarGridSpec(
        num_scalar_prefetch=1, grid=(steps,),
        in_specs=[pl.BlockSpec((tr, w.shape[1]), lambda i, s, n=n: (jnp.minimum(i, n - 1), 0))
                  for w, (tr, n) in zip(ws, tiles)],
        out_specs=[pl.BlockSpec((None, tr, w.shape[1]), lambda i, s, n=n: (s[0], jnp.minimum(i, n - 1), 0))
                   for w, (tr, n) in zip(ws, tiles)])
    return _pcall(body, name=name, out_shape=[jax.ShapeDtypeStruct((N_CHIPS,) + w.shape, BF16) for w in ws],
                  grid_spec=spec, compiler_params=_params(("arbitrary",)))(chip, *ws)


class Comm:
    def __init__(self, ins, outs, aliases, sems, first, mid, last):
        self.ins, self.outs, self.aliases, self.sems = list(ins), list(outs), dict(aliases), list(sems)
        self.first, self.mid, self.last = first, mid, last


def run_comm(name, comm):
    ni, no = len(comm.ins), len(comm.outs)

    def body(*refs):
        ins, outs, sems = refs[:ni], refs[ni:ni + no], refs[ni + no:]
        comm.first(ins, outs, sems)
        comm.mid(ins, outs, sems)
        comm.last(ins, outs, sems)

    return _pcall(body, name=name, out_shape=comm.outs, in_specs=[ANY] * ni, out_specs=[ANY] * no,
                  input_output_aliases=comm.aliases, scratch_shapes=comm.sems, compiler_params=_params())(*comm.ins)


def gather_comm(bufs):
    n = len(bufs)

    def half(out, w, which):
        pr = out[w].shape[1] // 2
        return pl.ds(pl.multiple_of(which * pr, BF16_ROWS), pr)

    def first(ins, out, sems):
        isend, irecv, _, _ = sems
        x, y, c, chips = _place()
        for w in range(n):
            mine = out[w].at[2 * x + y, half(out, w, c)]
            for j, (cx, cy) in enumerate(chips):
                _remote(mine, mine, isend.at[3 * w + j], irecv.at[3 * w + j], (cx, cy, c)).start()

    def mid(ins, out, sems):
        isend, irecv, dsend, drecv = sems
        x, y, c, chips = _place()
        sib = (x, y, 1 - c)
        for w in range(n):
            for j, (cx, cy) in enumerate(chips):
                landed = out[w].at[2 * cx + cy, half(out, w, c)]
                _remote(landed, landed, isend.at[3 * w + j], irecv.at[3 * w + j], sib).wait_recv()
                _remote(landed, landed, dsend.at[3 * w + j], drecv.at[3 * w + j], sib).start()

    def last(ins, out, sems):
        isend, irecv, dsend, drecv = sems
        x, y, c, chips = _place()
        sib = (x, y, 1 - c)
        for w in range(n):
            for j, (cx, cy) in enumerate(chips):
                landed = out[w].at[2 * cx + cy, half(out, w, 1 - c)]
                _remote(landed, landed, dsend.at[3 * w + j], drecv.at[3 * w + j], sib).wait_recv()
        for w in range(n):
            sent = out[w].at[0, half(out, w, c)]
            for j in range(3):
                _remote(sent, sent, isend.at[3 * w + j], irecv.at[3 * w + j], sib).wait_send()
                _remote(sent, sent, dsend.at[3 * w + j], drecv.at[3 * w + j], sib).wait_send()

    return Comm(bufs, [jax.ShapeDtypeStruct(s.shape, s.dtype) for s in bufs], {w: w for w in range(n)},
                [pltpu.SemaphoreType.DMA((3 * n,))] * 4, first, mid, last)


def _nothing(ins, outs, sems):
    return None


def join_comms(a, b):
    ni, no, ns = len(a.ins), len(a.outs), len(a.sems)

    def both(f, g):
        def hook(ins, outs, sems):
            f(ins[:ni], outs[:no], sems[:ns])
            g(ins[ni:], outs[no:], sems[ns:])
        return hook

    aliases = dict(a.aliases)
    aliases.update({ni + k: no + v for k, v in b.aliases.items()})
    return Comm(a.ins + b.ins, a.outs + b.outs, aliases, a.sems + b.sems,
                both(a.first, b.first), both(a.mid, b.mid), both(a.last, b.last))


def exchange_comm(pieces):
    n = len(pieces)

    def copies(src, out, sems):
        x, y, c, _ = _place()
        return [_remote(src[w].at[k, 1 - c], out[w].at[k], sems[0].at[N_CHIPS * w + k], sems[1].at[N_CHIPS * w + k],
                        (x, y, 1 - c)) for w in range(n) for k in range(N_CHIPS)]

    def first(src, out, sems):
        for cp in copies(src, out, sems):
            cp.start()

    def last(src, out, sems):
        for cp in copies(src, out, sems):
            cp.wait()

    return Comm(pieces, [jax.ShapeDtypeStruct((N_CHIPS,) + s.shape[2:], s.dtype) for s in pieces], {},
                [pltpu.SemaphoreType.DMA((N_CHIPS * n,))] * 2, first, _nothing, last)


def scatter_comm(parts):
    n = len(parts)

    def copies(src, out, sems):
        x, y, c, chips = _place()
        return [_remote(src[w].at[2 * cx + cy], out[w].at[j], sems[0].at[3 * w + j], sems[1].at[3 * w + j], (cx, cy, c))
                for w in range(n) for j, (cx, cy) in enumerate(chips)]

    def first(src, out, sems):
        for cp in copies(src, out, sems):
            cp.start()

    def last(src, out, sems):
        for cp in copies(src, out, sems):
            cp.wait()

    return Comm(parts, [jax.ShapeDtypeStruct((3,) + s.shape[1:], s.dtype) for s in parts], {},
                [pltpu.SemaphoreType.DMA((3 * n,))] * 2, first, _nothing, last)


def share_comm(halves):
    n = len(halves)

    def first(ins, buf, sems):
        x, y, c, _ = _place()
        for w in range(n):
            _remote(buf[w].at[c], buf[w].at[c], sems[0].at[w], sems[1].at[w], (x, y, 1 - c)).start()

    def last(ins, buf, sems):
        x, y, c, _ = _place()
        for w in range(n):
            landed = buf[w].at[1 - c]
            _remote(landed, landed, sems[0].at[w], sems[1].at[w], (x, y, 1 - c)).wait_recv()
        for w in range(n):
            _remote(buf[w].at[c], buf[w].at[c], sems[0].at[w], sems[1].at[w], (x, y, 1 - c)).wait_send()

    return Comm(halves, [jax.ShapeDtypeStruct(s.shape, s.dtype) for s in halves], {w: w for w in range(n)},
                [pltpu.SemaphoreType.DMA((n,))] * 2, first, _nothing, last)


def gather_small(name, blk, reduce):
    r, cdim = blk.shape

    def body(in_ref, out_ref, *rest):
        if reduce:
            buf, send_sem, recv_sem = rest
        else:
            buf = out_ref
            send_sem, recv_sem = rest
        x, y, c, _ = _place()
        me = 4 * x + 2 * y + c
        buf[me] = in_ref[...]
        peers = []
        for dx in range(2):
            for dy in range(2):
                for dc in range(2):
                    if dx or dy or dc:
                        peers.append((dx, dy, dc))
        copies = []
        for s, (dx, dy, dc) in enumerate(peers):
            cp = _remote(in_ref, buf.at[me], send_sem.at[s], recv_sem.at[s],
                         ((1 - x if dx else x), (1 - y if dy else y), (1 - c if dc else c)))
            cp.start()
            copies.append(cp)
        for s, (dx, dy, dc) in enumerate(peers):
            px, py, pc_ = (1 - x if dx else x), (1 - y if dy else y), (1 - c if dc else c)
            landed = buf.at[4 * px + 2 * py + pc_]
            _remote(landed, landed, send_sem.at[s], recv_sem.at[s], (x, y, c)).wait_recv()
        for cp in copies:
            cp.wait_send()
        if reduce:
            tot = buf[0]
            for s in range(1, N_DEV):
                tot = tot + buf[s]
            out_ref[...] = tot

    vm = pl.BlockSpec(memory_space=pltpu.VMEM)
    out_shape = jax.ShapeDtypeStruct((r, cdim) if reduce else (N_DEV, r, cdim), F32)
    scratch = ([pltpu.VMEM((N_DEV, r, cdim), F32)] if reduce else []) + [pltpu.SemaphoreType.DMA((N_DEV - 1,))] * 2
    return _pcall(body, name=name, out_shape=out_shape, in_specs=[vm], out_specs=vm, scratch_shapes=scratch,
                  compiler_params=_params())(blk)


def _job_tiles(shapes, tile_bytes, mult):
    tiles = []
    for rows, cols in shapes:
        tr = _tile(rows, max(mult, tile_bytes // (4 * cols)), mult)
        tiles.append((tr, rows // tr))
    return tiles, max(n for _, n in tiles)


def sum_cores(name, owns, gots, place):
    nj = len(owns)
    tiles, _ = _job_tiles([o.shape[2:] for o in owns], 1 << 21, BF16_ROWS)
    steps = max(N_CHIPS * n for _, n in tiles)

    def body(place_ref, *refs):
        i = pl.program_id(0)
        for k, (_, n) in enumerate(tiles):
            @pl.when(i < N_CHIPS * n)
            def _(own_ref=refs[2 * k], got_ref=refs[2 * k + 1], o_ref=refs[2 * nj + k]):
                o_ref[...] = (own_ref[...].astype(F32) + got_ref[...].astype(F32)).astype(o_ref.dtype)

    in_specs, out_specs, out_shape, args = [], [], [], []
    for own, got, (tr, n) in zip(owns, gots, tiles):
        pc = own.shape[3]
        last = N_CHIPS * n - 1
        in_specs += [pl.BlockSpec((None, None, tr, pc),
                                  lambda i, s, n=n, last=last: (jnp.minimum(i, last) // n, s[1], jnp.minimum(i, last) % n, 0)),
                     pl.BlockSpec((None, tr, pc),
                                  lambda i, s, n=n, last=last: (jnp.minimum(i, last) // n, jnp.minimum(i, last) % n, 0))]
        out_specs.append(pl.BlockSpec((None, tr, pc),
                                      lambda i, s, n=n, last=last: (jnp.minimum(i, last) // n, jnp.minimum(i, last) % n, 0)))
        out_shape.append(jax.ShapeDtypeStruct(got.shape, BF16))
        args += [own, got]
    spec = pltpu.PrefetchScalarGridSpec(num_scalar_prefetch=1, grid=(steps,), in_specs=in_specs, out_specs=out_specs)
    return _pcall(body, name=name, out_shape=out_shape, grid_spec=spec,
                  compiler_params=_params(("arbitrary",)))(place, *args)


def sum_chips(name, parts, gots, place):
    nj = len(parts)
    tiles, steps = _job_tiles([p.shape[1:] for p in parts], 1 << 20, BF16_ROWS)

    def body(place_ref, *refs):
        i = pl.program_id(0)
        for k, (_, n) in enumerate(tiles):
            @pl.when(i < n)
            def _(part_ref=refs[2 * k], got_ref=refs[2 * k + 1], o_ref=refs[2 * nj + k]):
                tot = part_ref[...].astype(F32)
                for j in range(3):
                    tot = tot + got_ref[j].astype(F32)
                o_ref[...] = tot

    in_specs, out_specs, out_shape, args = [], [], [], []
    for part, got, (tr, n) in zip(parts, gots, tiles):
        pc = part.shape[2]
        in_specs += [pl.BlockSpec((None, tr, pc), lambda i, s, n=n: (s[0], jnp.minimum(i, n - 1), 0)),
                     pl.BlockSpec((3, tr, pc), lambda i, s, n=n: (0, jnp.minimum(i, n - 1), 0))]
        out_specs.append(pl.BlockSpec((None, tr, pc), lambda i, s, n=n: (s[1], jnp.minimum(i, n - 1), 0)))
        out_shape.append(jax.ShapeDtypeStruct((2,) + part.shape[1:], F32))
        args += [part, got]
    spec = pltpu.PrefetchScalarGridSpec(num_scalar_prefetch=1, grid=(steps,), in_specs=in_specs, out_specs=out_specs)
    return _pcall(body, name=name, out_shape=out_shape, grid_spec=spec,
                  compiler_params=_params(("arbitrary",)))(place, *args)


def adamw(name, jobs):
    c1 = 1.0 / (1.0 - ADAM_B1 ** ADAM_STEP)
    c2 = 1.0 / (1.0 - ADAM_B2 ** ADAM_STEP)
    nj = len(jobs)
    tiles, steps = _job_tiles([j[0].shape for j in jobs], 1 << 18, 8)

    def body(*refs):
        i = pl.program_id(0)
        for k, (_, n) in enumerate(tiles):
            w_ref, g_ref, m_ref, v_ref = refs[4 * k:4 * k + 4]
            d_ref, nm_ref, nv_ref = refs[4 * nj + 3 * k:4 * nj + 3 * k + 3]

            @pl.when(i < n)
            def _(w_ref=w_ref, g_ref=g_ref, m_ref=m_ref, v_ref=v_ref, d_ref=d_ref, nm_ref=nm_ref, nv_ref=nv_ref):
                gv = g_ref[...]
                nm = ADAM_B1 * m_ref[...] + (1.0 - ADAM_B1) * gv
                nv = ADAM_B2 * v_ref[...] + (1.0 - ADAM_B2) * (gv * gv)
                nm_ref[...] = nm
                nv_ref[...] = nv
                d_ref[...] = -ADAM_LR * ((nm * c1) / (jnp.sqrt(nv * c2) + ADAM_EPS) + ADAM_WD * w_ref[...])

    in_specs, out_specs, out_shape, args = [], [], [], []
    for (w, g, m, v), (tr, n) in zip(jobs, tiles):
        spec = pl.BlockSpec((tr, w.shape[1]), lambda i, n=n: (jnp.minimum(i, n - 1), 0))
        in_specs += [spec] * 4
        out_specs += [spec] * 3
        out_shape += [jax.ShapeDtypeStruct(w.shape, F32)] * 3
        args += [w, g, m, v]
    res = _pcall(body, name=name, out_shape=out_shape, grid=(steps,), in_specs=in_specs, out_specs=out_specs,
                 compiler_params=_params(("arbitrary",)))(*args)
    return [tuple(res[3 * k:3 * k + 3]) for k in range(nj)]


MATS = ["ffn1_w_in", "ffn1_w_out", "w_mix_in", "w_conv_out", "w_attn_out", "w_mix_out", "ffn2_w_in", "ffn2_w_out",
        "w_ple_gate", "w_ple_proj"]
COL_SHARDED = {"ffn1_w_in", "w_mix_in", "ffn2_w_in", "w_ple_proj"}
NORMS = ["ffn1_norm", "mix_norm", "ffn2_norm", "ple_norm", "final_norm"]
WEIGHTS = ["ffn1_norm", "ffn1_w_in", "ffn1_w_out", "mix_norm", "w_mix_in", "conv_w", "w_conv_out", "w_attn_out",
           "w_mix_out", "ffn2_norm", "ffn2_w_in", "ffn2_w_out", "ple_norm", "w_ple_gate", "w_ple_proj", "final_norm"]


def _pad_rows(a, rows):
    return jnp.concatenate([a, jnp.zeros((rows - a.shape[0],) + a.shape[1:], a.dtype)], axis=0)


def _step(x, p, tgt, w, m, v):
    t, d = x.shape
    tt = _tile(t, 256)
    tm = _tile(t, 512)
    tm2 = _tile(t, 1024)
    tq = _tile(t, 1024)

    chip = 2 * lax.axis_index("x") + lax.axis_index("y")
    place = jnp.stack([chip, lax.axis_index("c")]).astype(jnp.int32)

    placed = dict(zip(MATS, place_shards("place_shards", [w[k] for k in MATS], place)))
    full = {}

    def keep(names, bufs):
        for k, buf in zip(names, bufs):
            full[k] = buf if k in COL_SHARDED else buf.reshape(-1, buf.shape[2])

    def gather_of(names):
        return gather_comm([placed[k] for k in names])

    cw_all = gather_small("gather_conv_w", _pad_rows(w["conv_w"], 8), False)
    cw8 = jnp.concatenate([cw_all[2 * k] for k in range(N_CHIPS)], axis=1)
    g1, gm, g2, gp, gf = (w[k].reshape(1, d) for k in NORMS)

    def ffn_fwd(tag, h, g, first, w_in_name, w_out_name, riders):
        if first:
            n, bufs = rms_fwd(tag + "_norm", h, g, tt, comm=gather_of(first))
            keep(first, bufs)
            (a, s), bufs = ffn_in_act(tag + "_in", n, full[w_in_name], tm, comm=gather_of(riders))
            keep(riders, bufs)
        else:
            a, s, n = ffn_in_act(tag + "_in", h, full[w_in_name], tm, gain=g)
        return n, a, s, mm_nn(tag + "_out", s, full[w_out_name], F32, tm, res=h, alpha=0.5)

    n1, a1, s1, h1 = ffn_fwd("ffn1", x, g1, ["ffn1_w_in"], "ffn1_w_in", "ffn1_w_out", ["ffn1_w_out", "w_mix_in"])
    wmix = full["w_mix_in"]
    riders = ["w_conv_out", "w_attn_out", "w_mix_out", "ffn2_w_in", "ffn2_w_out", "w_ple_gate", "w_ple_proj"]
    (mixin, u), bufs = mm_nn_stacked("mix_in", h1, wmix, BF16, tm2, d, comm=gather_of(riders), gain=gm)
    keep(riders, bufs)
    cbx = qkv = gates = mixin
    wpp = full["w_ple_proj"]
    wpp = jnp.transpose(wpp, (1, 0, 2)).reshape(wpp.shape[1], -1)
    ycin, y_conv = conv_out_fwd("conv_out", cbx, cw8, full["w_conv_out"], tm)
    o = attn_fwd("attn", qkv, tq, d, 3 * d)
    y_attn = mm_nn("attn_out", o, full["w_attn_out"], BF16, tm2)
    merged, h2 = mix_out_fwd("mix_out", gates, y_conv, y_attn, h1, full["w_mix_out"], tm2, 3)
    n2, a2, s2, h3 = ffn_fwd("ffn2", h2, g2, [], "ffn2_w_in", "ffn2_w_out", [])

    pieces, chip_sums, halves = {}, {}, {}

    def as_pieces(k):
        pc = pieces[k]
        return pc if k in COL_SHARDED else pc.reshape(N_CHIPS, 2, pc.shape[0] // (2 * N_CHIPS), pc.shape[1])

    def sum_siblings(tag, names):
        pcs = [as_pieces(k) for k in names]
        got = run_comm("exchange_" + tag, exchange_comm(pcs))
        chip_sums.update(zip(names, sum_cores("sum_cores_" + tag, pcs, got, place)))

    def scatter_of(names):
        return scatter_comm([chip_sums[k] for k in names])

    def sum_landed(tag, names, landed):
        halves.update(zip(names, sum_chips("sum_chips_" + tag, [chip_sums[k] for k in names], landed, place)))

    npl, dh4, dpp, dzg, dgf, loss_row = tail("tail", h3, p, tgt, gp, gf, full["w_ple_gate"], wpp, tm)
    dwpp = mm_tn_whole("ple_proj_dw", p, dpp, tm2)
    pieces["w_ple_proj"] = jnp.transpose(dwpp.reshape(2, p.shape[1] // 2, N_CHIPS, d // N_CHIPS), (2, 0, 1, 3))
    pieces["w_ple_gate"] = mm_tn_rows("ple_gate_dw", npl, dzg, tm2)
    dh3, df2, dgp = mm_nt("ple_gate_dx", dzg, full["w_ple_gate"], F32, tm2, d, norm=(h3, gp, dh4), alpha=0.5)
    w_in, w_out = full["ffn2_w_in"], full["ffn2_w_out"]
    pieces["ffn2_w_out"] = mm_tn_rows("ffn2_dwout", s2, df2, tm2)
    da2 = ffn_ds_dact("ffn2_ds", df2, w_out, a2, tm2)
    pieces["ffn2_w_in"] = mm_tn_cols("ffn2_dwin", n2, da2, tm2)
    dh2, dh2b, dg2 = mm_nt_stacked("ffn2_dn", da2, w_in, tm2, (h2, g2, dh3))
    pieces["w_mix_out"] = mm_tn_rows("mix_out_dw", merged, dh2b, tm2)
    dyc, dya, dgates = mix_out_bwd("mix_out_dx", dh2b, full["w_mix_out"], gates, y_conv, y_attn, tm2, 3)
    pieces["w_conv_out"] = mm_tn_rows("conv_out_dw", ycin, dyc, tm2)
    dcbx, dcw8 = conv_out_bwd("conv_out_dx", dyc, full["w_conv_out"], cbx, cw8, tt)
    pieces["w_attn_out"] = mm_tn_rows("attn_out_dw", o, dya, tm2)
    do = mm_nt("attn_out_dx", dya, full["w_attn_out"], BF16, tm2, d)
    dq, dk, dv = attn_bwd("attn_bwd", qkv, do, tq, 3 * d)
    dmix = [dcbx, dq, dk, dv, dgates]
    early = ["ffn2_w_in", "ffn2_w_out", "w_ple_gate", "w_ple_proj", "w_mix_out", "w_conv_out", "w_attn_out"]
    swap = exchange_comm([as_pieces(k) for k in early])
    pieces["w_mix_in"], got = mm_tn_parts("mix_in_dw", u, dmix, tm2, comm=swap)
    chip_sums.update(zip(early, sum_cores("sum_cores_early", swap.ins, got, place)))
    swap = exchange_comm([as_pieces("w_mix_in")])
    (dh1, df1, dgm), landed = mm_nt_parts("mix_in_dx", dmix, wmix, tm2, (h1, gm, dh2), 0.5,
                                          comm=join_comms(scatter_of(early), swap))
    sum_landed("early", early, landed[:len(early)])
    chip_sums["w_mix_in"] = sum_cores("sum_cores_mix", swap.ins, landed[len(early):], place)[0]
    w_in, w_out = full["ffn1_w_in"], full["ffn1_w_out"]
    pieces["ffn1_w_out"] = mm_tn_rows("ffn1_dwout", s1, df1, tm2)
    da1 = ffn_ds_dact("ffn1_ds", df1, w_out, a1, tm2)
    pieces["ffn1_w_in"], landed = mm_tn_cols("ffn1_dwin", n1, da1, tm2, comm=scatter_of(["w_mix_in"]))
    sum_landed("mix", ["w_mix_in"], landed)
    late = ["ffn1_w_in", "ffn1_w_out"]
    sum_siblings("late", late)
    done = early + ["w_mix_in"]
    (dx, _, dg1), landed = mm_nt_stacked(
        "ffn1_dn", da1, w_in, tm2, (x, g1, dh1),
        comm=join_comms(scatter_of(late), share_comm([halves[k] for k in done])))
    sum_landed("late", late, landed[:len(late)])
    shared = dict(zip(done, landed[len(late):]))

    shared.update(zip(late, run_comm("share_halves", share_comm([halves[k] for k in late]))))
    grad, delta, new_m, new_v = {}, {}, {}, {}
    for k in MATS:
        grad[k] = shared[k].reshape(w[k].shape)

    small = jnp.concatenate([dg1, dgm, dg2, dgp, dgf, dcw8[:3], loss_row, jnp.zeros((7, d), F32)], axis=0)
    tot = gather_small("sum_small", small, True)
    loss = tot[8, 0]
    norm_w = jnp.concatenate([w[k].reshape(1, d) for k in NORMS] + [jnp.zeros((3, d), F32)], axis=0)
    norm_m = jnp.concatenate([m[k].reshape(1, d) for k in NORMS] + [jnp.zeros((3, d), F32)], axis=0)
    norm_v = jnp.concatenate([v[k].reshape(1, d) for k in NORMS] + [jnp.ones((3, d), F32)], axis=0)
    norm_g = jnp.concatenate([tot[0:5], jnp.zeros((3, d), F32)], axis=0)
    cs = d // N_CHIPS
    gcw = lax.dynamic_slice(tot[5:8], (0, chip * cs), (3, cs))
    conv_job = (_pad_rows(w["conv_w"], 8), _pad_rows(gcw, 8), _pad_rows(m["conv_w"], 8),
                jnp.concatenate([v["conv_w"], jnp.ones((5, cs), F32)], axis=0))

    steps = adamw("adamw", [(w[k], grad[k], m[k], v[k]) for k in MATS]
                  + [(norm_w, norm_g, norm_m, norm_v), conv_job])
    for k, res in zip(MATS, steps):
        delta[k], new_m[k], new_v[k] = res
    nd, nm, nv = steps[len(MATS)]
    for r, k in enumerate(NORMS):
        grad[k] = norm_g[r].reshape(w[k].shape)
        delta[k], new_m[k], new_v[k] = (a[r].reshape(w[k].shape) for a in (nd, nm, nv))
    cd, cm, cv = steps[len(MATS) + 1]
    grad["conv_w"], delta["conv_w"], new_m["conv_w"], new_v["conv_w"] = gcw, cd[:3], cm[:3], cv[:3]
    return loss, dx, grad, delta, new_m, new_v


def kernel(x, p, ffn1_norm, ffn1_w_in, ffn1_w_out, mix_norm, w_mix_in, conv_w, w_conv_out, w_attn_out, w_mix_out, ffn2_norm, ffn2_w_in, ffn2_w_out, ple_norm, w_ple_gate, w_ple_proj, final_norm, loss_target, m_ffn1_norm, m_ffn1_w_in, m_ffn1_w_out, m_mix_norm, m_w_mix_in, m_conv_w, m_w_conv_out, m_w_attn_out, m_w_mix_out, m_ffn2_norm, m_ffn2_w_in, m_ffn2_w_out, m_ple_norm, m_w_ple_gate, m_w_ple_proj, m_final_norm, v_ffn1_norm, v_ffn1_w_in, v_ffn1_w_out, v_mix_norm, v_w_mix_in, v_conv_w, v_w_conv_out, v_w_attn_out, v_w_mix_out, v_ffn2_norm, v_ffn2_w_in, v_ffn2_w_out, v_ple_norm, v_w_ple_gate, v_w_ple_proj, v_final_norm):
    ws = (ffn1_norm, ffn1_w_in, ffn1_w_out, mix_norm, w_mix_in, conv_w, w_conv_out, w_attn_out, w_mix_out, ffn2_norm,
          ffn2_w_in, ffn2_w_out, ple_norm, w_ple_gate, w_ple_proj, final_norm)
    ms = (m_ffn1_norm, m_ffn1_w_in, m_ffn1_w_out, m_mix_norm, m_w_mix_in, m_conv_w, m_w_conv_out, m_w_attn_out,
          m_w_mix_out, m_ffn2_norm, m_ffn2_w_in, m_ffn2_w_out, m_ple_norm, m_w_ple_gate, m_w_ple_proj, m_final_norm)
    vs = (v_ffn1_norm, v_ffn1_w_in, v_ffn1_w_out, v_mix_norm, v_w_mix_in, v_conv_w, v_w_conv_out, v_w_attn_out,
          v_w_mix_out, v_ffn2_norm, v_ffn2_w_in, v_ffn2_w_out, v_ple_norm, v_w_ple_gate, v_w_ple_proj, v_final_norm)
    assert x.shape[0] == 1 and p.shape[:2] == (1, 1), "one sequence and one layer per device"

    def strip(a):
        return a[0] if a.ndim == 3 or (a.ndim == 2 and a.shape[0] == 1) else a

    w = {k: strip(a) for k, a in zip(WEIGHTS, ws)}
    m = {k: strip(a) for k, a in zip(WEIGHTS, ms)}
    v = {k: strip(a) for k, a in zip(WEIGHTS, vs)}
    loss, dx, grad, delta, new_m, new_v = _step(x[0], p[0, 0], loss_target[0], w, m, v)
    shapes = [a.shape for a in ws]
    outs = [loss, dx[None]]
    for res in (grad, delta, new_m, new_v):
        outs += [res[k].reshape(s) for k, s in zip(WEIGHTS, shapes)]
    return tuple(outs)
```
